```python
import jax, jax.numpy as jnp
from jax import lax
import numpy as np

D_MODEL = 1024
BATCH = 8
SEQ = 8192
DEPTH = 4

D_A = D_MODEL // 2
D_B = D_MODEL // 2
D_IN_CONV = 2 * D_A + 3 * D_B
CONV_A_WIDTH = 31
CONV_B_WIDTH = 3
HEAD_DIM = 64
N_HEADS = D_MODEL // HEAD_DIM
BLOCK_Q = 128
D_FF = ((8 * D_MODEL // 3 + 127) // 128) * 128
FFN_CONV_WIDTH = 3
N_EVEN = (DEPTH + 1) // 2
N_ODD = DEPTH // 2
EPS = 1e-6

kernel_name = "hybrid_conformer_shortconv_stickbreaking"


def rmsnorm(x, g):
    xf = x.astype(jnp.float32)
    y = xf * lax.rsqrt(jnp.mean(xf * xf, axis=-1, keepdims=True) + EPS)
    return (y * g.astype(jnp.float32)).astype(x.dtype)


def layernorm(x, g, b):
    xf = x.astype(jnp.float32)
    mu = jnp.mean(xf, axis=-1, keepdims=True)
    xc = xf - mu
    y = xc * lax.rsqrt(jnp.mean(xc * xc, axis=-1, keepdims=True) + EPS)
    return (y * g.astype(jnp.float32) + b.astype(jnp.float32)).astype(x.dtype)


def causal_dwconv(x, w, b=None):
    K, C = w.shape
    y = lax.conv_general_dilated(
        x, w[:, None, :].astype(x.dtype), window_strides=(1,),
        padding=[(K - 1, 0)], dimension_numbers=("NWC", "WIO", "NWC"),
        feature_group_count=C)
    if b is not None:
        y = y + b.astype(x.dtype)
    return y


def conv_mixer(h, w_in, a_dw_w, a_dw_b, a_ln_g, a_ln_b, b_dw_w, w_out):
    p = h @ w_in
    a_val, a_gate, b_gb, b_gc, b_h = jnp.split(
        p, [D_A, 2 * D_A, 2 * D_A + D_B, 2 * D_A + 2 * D_B], axis=-1)
    a = causal_dwconv(a_val * jax.nn.sigmoid(a_gate), a_dw_w, a_dw_b)
    a = jax.nn.silu(layernorm(a, a_ln_g, a_ln_b))
    b = b_gb * causal_dwconv(b_gc * b_h, b_dw_w)
    return jnp.concatenate([a, b], axis=-1) @ w_out


def stick_breaking_attention(q, k, v):
    B, H, S, dh = q.shape
    nb = S // BLOCK_Q
    scale = dh ** -0.5
    qb = q.astype(jnp.float32).reshape(B, H, nb, BLOCK_Q, dh).transpose(2, 0, 1, 3, 4)
    kf = k.astype(jnp.float32)
    starts = jnp.arange(nb, dtype=jnp.int32) * BLOCK_Q
    kpos = jnp.arange(S, dtype=jnp.int32)

    def block(args):
        q_blk, start = args
        z = jnp.einsum("bhqd,bhsd->bhqs", q_blk, kf) * scale
        qpos = start + jnp.arange(BLOCK_Q, dtype=jnp.int32)
        valid = kpos[None, :] < qpos[:, None]
        log_keep = jnp.where(valid, jax.nn.log_sigmoid(-z), 0.0)
        rc = lax.cumsum(log_keep, axis=3, reverse=True)
        after = jnp.concatenate([rc[..., 1:], jnp.zeros_like(rc[..., :1])], axis=-1)
        weight = jnp.where(valid, jnp.exp(jax.nn.log_sigmoid(z) + after), 0.0)
        return jnp.einsum("bhqs,bhsd->bhqd", weight.astype(v.dtype), v)

    out = lax.map(block, (qb, starts))
    return out.transpose(1, 2, 0, 3, 4).reshape(B, H, S, dh)


def attn_mixer(h, w_qkv, q_g, k_g, w_o):
    B, S, _ = h.shape
    qkv = (h @ w_qkv).reshape(B, S, 3, N_HEADS, HEAD_DIM)
    q = rmsnorm(qkv[:, :, 0], q_g).transpose(0, 2, 1, 3)
    k = rmsnorm(qkv[:, :, 1], k_g).transpose(0, 2, 1, 3)
    v = qkv[:, :, 2].transpose(0, 2, 1, 3)
    o = stick_breaking_attention(q, k, v)
    return o.transpose(0, 2, 1, 3).reshape(B, S, N_HEADS * HEAD_DIM) @ w_o


def conv_ffn(h, w_up, dw_w, dw_b, w_down):
    u = causal_dwconv(h @ w_up, dw_w, dw_b)
    gate, val = jnp.split(u, 2, axis=-1)
    return (jax.nn.silu(gate) * val) @ w_down


def _fwd_setup_inputs(seed: int = 0) -> dict:
    key = jax.random.key(seed)
    ks = jax.random.split(key, 18)
    f32 = jnp.float32

    def nrm(k, shape, scale):
        return jax.random.normal(k, shape, f32) * scale

    def gain(k, shape):
        return 1.0 + 0.02 * jax.random.normal(k, shape, f32)

    return {
        "x": nrm(ks[0], (BATCH, SEQ, D_MODEL), 1.0),
        "mix_norm_g": gain(ks[1], (DEPTH, D_MODEL)),
        "ffn_norm_g": gain(ks[2], (DEPTH, D_MODEL)),
        "conv_w_in": nrm(ks[3], (N_EVEN, D_MODEL, D_IN_CONV), D_MODEL ** -0.5),
        "conv_a_dw_w": nrm(ks[4], (N_EVEN, CONV_A_WIDTH, D_A), CONV_A_WIDTH ** -0.5),
        "conv_a_dw_b": nrm(ks[5], (N_EVEN, D_A), 0.02),
        "conv_a_ln_g": gain(ks[6], (N_EVEN, D_A)),
        "conv_a_ln_b": nrm(ks[7], (N_EVEN, D_A), 0.02),
        "conv_b_dw_w": nrm(ks[8], (N_EVEN, CONV_B_WIDTH, D_B), CONV_B_WIDTH ** -0.5),
        "conv_w_out": nrm(ks[9], (N_EVEN, D_A + D_B, D_MODEL), (D_A + D_B) ** -0.5),
        "attn_w_qkv": nrm(ks[10], (N_ODD, D_MODEL, 3 * N_HEADS * HEAD_DIM), D_MODEL ** -0.5),
        "attn_q_g": gain(ks[11], (N_ODD, HEAD_DIM)),
        "attn_k_g": gain(ks[12], (N_ODD, HEAD_DIM)),
        "attn_w_o": nrm(ks[13], (N_ODD, N_HEADS * HEAD_DIM, D_MODEL), (N_HEADS * HEAD_DIM) ** -0.5),
        "ffn_w_up": nrm(ks[14], (DEPTH, D_MODEL, 2 * D_FF), D_MODEL ** -0.5),
        "ffn_dw_w": nrm(ks[15], (DEPTH, FFN_CONV_WIDTH, 2 * D_FF), FFN_CONV_WIDTH ** -0.5),
        "ffn_dw_b": nrm(ks[16], (DEPTH, 2 * D_FF), 0.02),
        "ffn_w_down": nrm(ks[17], (DEPTH, D_FF, D_MODEL), D_FF ** -0.5),
    }


def _fwd_reference(x, mix_norm_g, ffn_norm_g, conv_w_in, conv_a_dw_w, conv_a_dw_b,
              conv_a_ln_g, conv_a_ln_b, conv_b_dw_w, conv_w_out, attn_w_qkv,
              attn_q_g, attn_k_g, attn_w_o, ffn_w_up, ffn_dw_w, ffn_dw_b,
              ffn_w_down):
    for layer in range(DEPTH):
        i = layer // 2
        h = rmsnorm(x, mix_norm_g[layer])
        if layer % 2 == 0:
            x = x + conv_mixer(h, conv_w_in[i], conv_a_dw_w[i], conv_a_dw_b[i],
                               conv_a_ln_g[i], conv_a_ln_b[i], conv_b_dw_w[i],
                               conv_w_out[i])
        else:
            x = x + attn_mixer(h, attn_w_qkv[i], attn_q_g[i], attn_k_g[i], attn_w_o[i])
        h = rmsnorm(x, ffn_norm_g[layer])
        x = x + conv_ffn(h, ffn_w_up[layer], ffn_dw_w[layer], ffn_dw_b[layer], ffn_w_down[layer])
    return x


import jax as _jax
import jax.numpy as _jnp

TWIN_FORMAT = 'train_step'
FWD_PARAMS = ['x', 'mix_norm_g', 'ffn_norm_g', 'conv_w_in', 'conv_a_dw_w', 'conv_a_dw_b', 'conv_a_ln_g', 'conv_a_ln_b', 'conv_b_dw_w', 'conv_w_out', 'attn_w_qkv', 'attn_q_g', 'attn_k_g', 'attn_w_o', 'ffn_w_up', 'ffn_dw_w', 'ffn_dw_b', 'ffn_w_down']
TWIN_WEIGHTS = ['mix_norm_g', 'ffn_norm_g', 'conv_w_in', 'conv_a_dw_w', 'conv_a_dw_b', 'conv_a_ln_g', 'conv_a_ln_b', 'conv_b_dw_w', 'conv_w_out', 'attn_w_qkv', 'attn_q_g', 'attn_k_g', 'attn_w_o', 'ffn_w_up', 'ffn_dw_w', 'ffn_dw_b', 'ffn_w_down']
TWIN_DIFF_INPUT = 'x'
TWIN_INPUTS = ['x', 'mix_norm_g', 'ffn_norm_g', 'conv_w_in', 'conv_a_dw_w', 'conv_a_dw_b', 'conv_a_ln_g', 'conv_a_ln_b', 'conv_b_dw_w', 'conv_w_out', 'attn_w_qkv', 'attn_q_g', 'attn_k_g', 'attn_w_o', 'ffn_w_up', 'ffn_dw_w', 'ffn_dw_b', 'ffn_w_down', 'loss_target', 'm_mix_norm_g', 'm_ffn_norm_g', 'm_conv_w_in', 'm_conv_a_dw_w', 'm_conv_a_dw_b', 'm_conv_a_ln_g', 'm_conv_a_ln_b', 'm_conv_b_dw_w', 'm_conv_w_out', 'm_attn_w_qkv', 'm_attn_q_g', 'm_attn_k_g', 'm_attn_w_o', 'm_ffn_w_up', 'm_ffn_dw_w', 'm_ffn_dw_b', 'm_ffn_w_down', 'v_mix_norm_g', 'v_ffn_norm_g', 'v_conv_w_in', 'v_conv_a_dw_w', 'v_conv_a_dw_b', 'v_conv_a_ln_g', 'v_conv_a_ln_b', 'v_conv_b_dw_w', 'v_conv_w_out', 'v_attn_w_qkv', 'v_attn_q_g', 'v_attn_k_g', 'v_attn_w_o', 'v_ffn_w_up', 'v_ffn_dw_w', 'v_ffn_dw_b', 'v_ffn_w_down']
TWIN_OUTPUTS = ['loss', 'grad_x', 'grad_mix_norm_g', 'grad_ffn_norm_g', 'grad_conv_w_in', 'grad_conv_a_dw_w', 'grad_conv_a_dw_b', 'grad_conv_a_ln_g', 'grad_conv_a_ln_b', 'grad_conv_b_dw_w', 'grad_conv_w_out', 'grad_attn_w_qkv', 'grad_attn_q_g', 'grad_attn_k_g', 'grad_attn_w_o', 'grad_ffn_w_up', 'grad_ffn_dw_w', 'grad_ffn_dw_b', 'grad_ffn_w_down', 'delta_mix_norm_g', 'delta_ffn_norm_g', 'delta_conv_w_in', 'delta_conv_a_dw_w', 'delta_conv_a_dw_b', 'delta_conv_a_ln_g', 'delta_conv_a_ln_b', 'delta_conv_b_dw_w', 'delta_conv_w_out', 'delta_attn_w_qkv', 'delta_attn_q_g', 'delta_attn_k_g', 'delta_attn_w_o', 'delta_ffn_w_up', 'delta_ffn_dw_w', 'delta_ffn_dw_b', 'delta_ffn_w_down', 'new_m_mix_norm_g', 'new_m_ffn_norm_g', 'new_m_conv_w_in', 'new_m_conv_a_dw_w', 'new_m_conv_a_dw_b', 'new_m_conv_a_ln_g', 'new_m_conv_a_ln_b', 'new_m_conv_b_dw_w', 'new_m_conv_w_out', 'new_m_attn_w_qkv', 'new_m_attn_q_g', 'new_m_attn_k_g', 'new_m_attn_w_o', 'new_m_ffn_w_up', 'new_m_ffn_dw_w', 'new_m_ffn_dw_b', 'new_m_ffn_w_down', 'new_v_mix_norm_g', 'new_v_ffn_norm_g', 'new_v_conv_w_in', 'new_v_conv_a_dw_w', 'new_v_conv_a_dw_b', 'new_v_conv_a_ln_g', 'new_v_conv_a_ln_b', 'new_v_conv_b_dw_w', 'new_v_conv_w_out', 'new_v_attn_w_qkv', 'new_v_attn_q_g', 'new_v_attn_k_g', 'new_v_attn_w_o', 'new_v_ffn_w_up', 'new_v_ffn_dw_w', 'new_v_ffn_dw_b', 'new_v_ffn_w_down']
TWIN_LEAF_KINDS = {'loss': 'loss', 'grad_x': 'grad_x', 'grad_mix_norm_g': 'grad_w', 'grad_ffn_norm_g': 'grad_w', 'grad_conv_w_in': 'grad_w', 'grad_conv_a_dw_w': 'grad_w', 'grad_conv_a_dw_b': 'grad_w', 'grad_conv_a_ln_g': 'grad_w', 'grad_conv_a_ln_b': 'grad_w', 'grad_conv_b_dw_w': 'grad_w', 'grad_conv_w_out': 'grad_w', 'grad_attn_w_qkv': 'grad_w', 'grad_attn_q_g': 'grad_w', 'grad_attn_k_g': 'grad_w', 'grad_attn_w_o': 'grad_w', 'grad_ffn_w_up': 'grad_w', 'grad_ffn_dw_w': 'grad_w', 'grad_ffn_dw_b': 'grad_w', 'grad_ffn_w_down': 'grad_w', 'delta_mix_norm_g': 'delta_w', 'delta_ffn_norm_g': 'delta_w', 'delta_conv_w_in': 'delta_w', 'delta_conv_a_dw_w': 'delta_w', 'delta_conv_a_dw_b': 'delta_w', 'delta_conv_a_ln_g': 'delta_w', 'delta_conv_a_ln_b': 'delta_w', 'delta_conv_b_dw_w': 'delta_w', 'delta_conv_w_out': 'delta_w', 'delta_attn_w_qkv': 'delta_w', 'delta_attn_q_g': 'delta_w', 'delta_attn_k_g': 'delta_w', 'delta_attn_w_o': 'delta_w', 'delta_ffn_w_up': 'delta_w', 'delta_ffn_dw_w': 'delta_w', 'delta_ffn_dw_b': 'delta_w', 'delta_ffn_w_down': 'delta_w', 'new_m_mix_norm_g': 'new_m', 'new_m_ffn_norm_g': 'new_m', 'new_m_conv_w_in': 'new_m', 'new_m_conv_a_dw_w': 'new_m', 'new_m_conv_a_dw_b': 'new_m', 'new_m_conv_a_ln_g': 'new_m', 'new_m_conv_a_ln_b': 'new_m', 'new_m_conv_b_dw_w': 'new_m', 'new_m_conv_w_out': 'new_m', 'new_m_attn_w_qkv': 'new_m', 'new_m_attn_q_g': 'new_m', 'new_m_attn_k_g': 'new_m', 'new_m_attn_w_o': 'new_m', 'new_m_ffn_w_up': 'new_m', 'new_m_ffn_dw_w': 'new_m', 'new_m_ffn_dw_b': 'new_m', 'new_m_ffn_w_down': 'new_m', 'new_v_mix_norm_g': 'new_v', 'new_v_ffn_norm_g': 'new_v', 'new_v_conv_w_in': 'new_v', 'new_v_conv_a_dw_w': 'new_v', 'new_v_conv_a_dw_b': 'new_v', 'new_v_conv_a_ln_g': 'new_v', 'new_v_conv_a_ln_b': 'new_v', 'new_v_conv_b_dw_w': 'new_v', 'new_v_conv_w_out': 'new_v', 'new_v_attn_w_qkv': 'new_v', 'new_v_attn_q_g': 'new_v', 'new_v_attn_k_g': 'new_v', 'new_v_attn_w_o': 'new_v', 'new_v_ffn_w_up': 'new_v', 'new_v_ffn_dw_w': 'new_v', 'new_v_ffn_dw_b': 'new_v', 'new_v_ffn_w_down': 'new_v'}


def _forward(args):
    return _fwd_reference(*[args[k] for k in FWD_PARAMS])


def _output_shape():
    def fwd():
        inp = _fwd_setup_inputs(0)
        return _fwd_reference(*[inp[k] for k in FWD_PARAMS])
    out = _jax.eval_shape(fwd)
    return out.shape, out.dtype

N_MICROBATCH = 1
ADAM_LR = 0.001
ADAM_B1 = 0.9
ADAM_B2 = 0.999
ADAM_EPS = 1e-08
ADAM_WD = 0.01
ADAM_STEP = 10
PER_EXAMPLE_BATCH_AXIS = {'x': 0, 'loss_target': 0}
SHARED_INPUTS = []
_WEIGHT_DTYPES = {'mix_norm_g': _jnp.float32, 'ffn_norm_g': _jnp.float32, 'conv_w_in': _jnp.float32, 'conv_a_dw_w': _jnp.float32, 'conv_a_dw_b': _jnp.float32, 'conv_a_ln_g': _jnp.float32, 'conv_a_ln_b': _jnp.float32, 'conv_b_dw_w': _jnp.float32, 'conv_w_out': _jnp.float32, 'attn_w_qkv': _jnp.float32, 'attn_q_g': _jnp.float32, 'attn_k_g': _jnp.float32, 'attn_w_o': _jnp.float32, 'ffn_w_up': _jnp.float32, 'ffn_dw_w': _jnp.float32, 'ffn_dw_b': _jnp.float32, 'ffn_w_down': _jnp.float32}
MOMENT_SCALE = {'mix_norm_g': 6.902416e+01, 'ffn_norm_g': 5.045301e+01, 'conv_w_in': 1.890221e+00, 'conv_a_dw_w': 1.830249e+00, 'conv_a_dw_b': 3.001522e+01, 'conv_a_ln_g': 3.116877e+01, 'conv_a_ln_b': 2.573316e+01, 'conv_b_dw_w': 3.564603e+01, 'conv_w_out': 4.537698e+00, 'attn_w_qkv': 1.506046e+00, 'attn_q_g': 6.132219e+01, 'attn_k_g': 6.110265e+01, 'attn_w_o': 2.489574e+00, 'ffn_w_up': 9.835059e-01, 'ffn_dw_w': 7.018854e+00, 'ffn_dw_b': 6.898767e+00, 'ffn_w_down': 1.081674e+00}


def _to_microbatches(a, axis):
    t = _jnp.moveaxis(a, axis, 0)
    t = t.reshape((N_MICROBATCH, t.shape[0] // N_MICROBATCH) + t.shape[1:])
    return _jnp.moveaxis(t, 1, axis + 1)


def setup_inputs(seed: int = 0) -> dict:
    inp = _fwd_setup_inputs(seed)
    key = _jax.random.fold_in(_jax.random.key(seed), 7919)
    shape, _ = _output_shape()
    out = dict(inp)
    out["loss_target"] = _jax.random.normal(_jax.random.fold_in(key, 0), shape, _jnp.float32)
    for i, name in enumerate(TWIN_WEIGHTS):
        w = inp[name].astype(_jnp.float32)
        if MOMENT_SCALE is None:
            s = _jnp.sqrt(_jnp.mean(_jnp.square(w)) + 1e-30)
        else:
            s = MOMENT_SCALE[name]
        km, kv = _jax.random.split(_jax.random.fold_in(key, i + 1))
        out[name] = w
        out["m_" + name] = s * _jax.random.normal(km, w.shape, _jnp.float32)
        out["v_" + name] = (s * s) * _jax.random.uniform(kv, w.shape, _jnp.float32, 0.5, 1.5)
    if N_MICROBATCH > 1:
        for name, axis in PER_EXAMPLE_BATCH_AXIS.items():
            out[name] = _to_microbatches(out[name], axis)
    return {'x': out['x'], 'mix_norm_g': out['mix_norm_g'], 'ffn_norm_g': out['ffn_norm_g'], 'conv_w_in': out['conv_w_in'], 'conv_a_dw_w': out['conv_a_dw_w'], 'conv_a_dw_b': out['conv_a_dw_b'], 'conv_a_ln_g': out['conv_a_ln_g'], 'conv_a_ln_b': out['conv_a_ln_b'], 'conv_b_dw_w': out['conv_b_dw_w'], 'conv_w_out': out['conv_w_out'], 'attn_w_qkv': out['attn_w_qkv'], 'attn_q_g': out['attn_q_g'], 'attn_k_g': out['attn_k_g'], 'attn_w_o': out['attn_w_o'], 'ffn_w_up': out['ffn_w_up'], 'ffn_dw_w': out['ffn_dw_w'], 'ffn_dw_b': out['ffn_dw_b'], 'ffn_w_down': out['ffn_w_down'], 'loss_target': out['loss_target'], 'm_mix_norm_g': out['m_mix_norm_g'], 'm_ffn_norm_g': out['m_ffn_norm_g'], 'm_conv_w_in': out['m_conv_w_in'], 'm_conv_a_dw_w': out['m_conv_a_dw_w'], 'm_conv_a_dw_b': out['m_conv_a_dw_b'], 'm_conv_a_ln_g': out['m_conv_a_ln_g'], 'm_conv_a_ln_b': out['m_conv_a_ln_b'], 'm_conv_b_dw_w': out['m_conv_b_dw_w'], 'm_conv_w_out': out['m_conv_w_out'], 'm_attn_w_qkv': out['m_attn_w_qkv'], 'm_attn_q_g': out['m_attn_q_g'], 'm_attn_k_g': out['m_attn_k_g'], 'm_attn_w_o': out['m_attn_w_o'], 'm_ffn_w_up': out['m_ffn_w_up'], 'm_ffn_dw_w': out['m_ffn_dw_w'], 'm_ffn_dw_b': out['m_ffn_dw_b'], 'm_ffn_w_down': out['m_ffn_w_down'], 'v_mix_norm_g': out['v_mix_norm_g'], 'v_ffn_norm_g': out['v_ffn_norm_g'], 'v_conv_w_in': out['v_conv_w_in'], 'v_conv_a_dw_w': out['v_conv_a_dw_w'], 'v_conv_a_dw_b': out['v_conv_a_dw_b'], 'v_conv_a_ln_g': out['v_conv_a_ln_g'], 'v_conv_a_ln_b': out['v_conv_a_ln_b'], 'v_conv_b_dw_w': out['v_conv_b_dw_w'], 'v_conv_w_out': out['v_conv_w_out'], 'v_attn_w_qkv': out['v_attn_w_qkv'], 'v_attn_q_g': out['v_attn_q_g'], 'v_attn_k_g': out['v_attn_k_g'], 'v_attn_w_o': out['v_attn_w_o'], 'v_ffn_w_up': out['v_ffn_w_up'], 'v_ffn_dw_w': out['v_ffn_dw_w'], 'v_ffn_dw_b': out['v_ffn_dw_b'], 'v_ffn_w_down': out['v_ffn_w_down']}


def _loss(weights, diff, rest, loss_target):
    with _jax.named_scope("forward"):
        args = {**rest, TWIN_DIFF_INPUT: diff, **{k: w.astype(_WEIGHT_DTYPES[k]) for k, w in weights.items()}}
        y = _forward(args)
    with _jax.named_scope("loss_head"):
        err = _jnp.square(y.astype(_jnp.float32) - loss_target)
        return 0.5 * _jnp.sum(_jnp.mean(err, axis=-1)) if err.ndim else 0.5 * err


def _adamw(w, g, m, v):
    m = ADAM_B1 * m + (1.0 - ADAM_B1) * g
    v = ADAM_B2 * v + (1.0 - ADAM_B2) * _jnp.square(g)
    m_hat = m / (1.0 - ADAM_B1 ** ADAM_STEP)
    v_hat = v / (1.0 - ADAM_B2 ** ADAM_STEP)
    delta = -ADAM_LR * (m_hat / (_jnp.sqrt(v_hat) + ADAM_EPS) + ADAM_WD * w)
    return delta, m, v


def reference(x, mix_norm_g, ffn_norm_g, conv_w_in, conv_a_dw_w, conv_a_dw_b, conv_a_ln_g, conv_a_ln_b, conv_b_dw_w, conv_w_out, attn_w_qkv, attn_q_g, attn_k_g, attn_w_o, ffn_w_up, ffn_dw_w, ffn_dw_b, ffn_w_down, loss_target, m_mix_norm_g, m_ffn_norm_g, m_conv_w_in, m_conv_a_dw_w, m_conv_a_dw_b, m_conv_a_ln_g, m_conv_a_ln_b, m_conv_b_dw_w, m_conv_w_out, m_attn_w_qkv, m_attn_q_g, m_attn_k_g, m_attn_w_o, m_ffn_w_up, m_ffn_dw_w, m_ffn_dw_b, m_ffn_w_down, v_mix_norm_g, v_ffn_norm_g, v_conv_w_in, v_conv_a_dw_w, v_conv_a_dw_b, v_conv_a_ln_g, v_conv_a_ln_b, v_conv_b_dw_w, v_conv_w_out, v_attn_w_qkv, v_attn_q_g, v_attn_k_g, v_attn_w_o, v_ffn_w_up, v_ffn_dw_w, v_ffn_dw_b, v_ffn_w_down):
    given = dict(x=x, mix_norm_g=mix_norm_g, ffn_norm_g=ffn_norm_g, conv_w_in=conv_w_in, conv_a_dw_w=conv_a_dw_w, conv_a_dw_b=conv_a_dw_b, conv_a_ln_g=conv_a_ln_g, conv_a_ln_b=conv_a_ln_b, conv_b_dw_w=conv_b_dw_w, conv_w_out=conv_w_out, attn_w_qkv=attn_w_qkv, attn_q_g=attn_q_g, attn_k_g=attn_k_g, attn_w_o=attn_w_o, ffn_w_up=ffn_w_up, ffn_dw_w=ffn_dw_w, ffn_dw_b=ffn_dw_b, ffn_w_down=ffn_w_down, loss_target=loss_target, m_mix_norm_g=m_mix_norm_g, m_ffn_norm_g=m_ffn_norm_g, m_conv_w_in=m_conv_w_in, m_conv_a_dw_w=m_conv_a_dw_w, m_conv_a_dw_b=m_conv_a_dw_b, m_conv_a_ln_g=m_conv_a_ln_g, m_conv_a_ln_b=m_conv_a_ln_b, m_conv_b_dw_w=m_conv_b_dw_w, m_conv_w_out=m_conv_w_out, m_attn_w_qkv=m_attn_w_qkv, m_attn_q_g=m_attn_q_g, m_attn_k_g=m_attn_k_g, m_attn_w_o=m_attn_w_o, m_ffn_w_up=m_ffn_w_up, m_ffn_dw_w=m_ffn_dw_w, m_ffn_dw_b=m_ffn_dw_b, m_ffn_w_down=m_ffn_w_down, v_mix_norm_g=v_mix_norm_g, v_ffn_norm_g=v_ffn_norm_g, v_conv_w_in=v_conv_w_in, v_conv_a_dw_w=v_conv_a_dw_w, v_conv_a_dw_b=v_conv_a_dw_b, v_conv_a_ln_g=v_conv_a_ln_g, v_conv_a_ln_b=v_conv_a_ln_b, v_conv_b_dw_w=v_conv_b_dw_w, v_conv_w_out=v_conv_w_out, v_attn_w_qkv=v_attn_w_qkv, v_attn_q_g=v_attn_q_g, v_attn_k_g=v_attn_k_g, v_attn_w_o=v_attn_w_o, v_ffn_w_up=v_ffn_w_up, v_ffn_dw_w=v_ffn_dw_w, v_ffn_dw_b=v_ffn_dw_b, v_ffn_w_down=v_ffn_w_down)
    weights = {n: given[n] for n in TWIN_WEIGHTS}
    shared = {n: given[n] for n in SHARED_INPUTS}
    per_example = {n: given[n] for n in ['x']}
    grad_fn = _jax.value_and_grad(_loss, argnums=(0, 1))

    def one_microbatch(ex, loss_target):
        ex = dict(ex)
        diff = ex.pop(TWIN_DIFF_INPUT)
        return grad_fn(weights, diff, {**shared, **ex}, loss_target)

    if N_MICROBATCH == 1:
        loss, (grad_w, grad_x) = one_microbatch(per_example, given["loss_target"])
    else:
        def body(carry, xs):
            loss_sum, grad_sum = carry
            l_k, (gw_k, gx_k) = one_microbatch(xs[0], xs[1])
            with _jax.named_scope("update"):
                return (loss_sum + l_k, _jax.tree.map(_jnp.add, grad_sum, gw_k)), gx_k

        init = (_jnp.zeros((), _jnp.float32), _jax.tree.map(_jnp.zeros_like, weights))
        (loss, grad_w), grad_x = _jax.lax.scan(body, init, (per_example, given["loss_target"]))
    with _jax.named_scope("update"):
        delta_w, new_m, new_v = {}, {}, {}
        for n in TWIN_WEIGHTS:
            delta_w[n], new_m[n], new_v[n] = _adamw(weights[n], grad_w[n], given["m_" + n], given["v_" + n])
    return (loss, grad_x, *[grad_w[n] for n in TWIN_WEIGHTS], *[delta_w[n] for n in TWIN_WEIGHTS],
            *[new_m[n] for n in TWIN_WEIGHTS], *[new_v[n] for n in TWIN_WEIGHTS])
```

```python
import jax
import jax.numpy as jnp
from jax import lax
from jax.experimental import pallas as pl
from jax.experimental.pallas import tpu as pltpu

F32 = jnp.float32
BF16 = jnp.bfloat16
EPS = 1e-6
N_DEV = 8
MESH_ID = pl.DeviceIdType.MESH

ADAM_LR = 0.001
ADAM_B1 = 0.9
ADAM_B2 = 0.999
ADAM_EPS = 1e-08
ADAM_WD = 0.01
ADAM_STEP = 10

V7X_VMEM_LIMIT_BYTES = 48 * 1024 * 1024
PACK_QUANTUM = 2048
A_HALO = 32
S_HALO = 8
ATTN_BLOCK = 256


def _tile(n, prefs):
    for p in prefs:
        if n % p == 0:
            return p
    return n


def _params(sem=None):
    return pltpu.CompilerParams(dimension_semantics=sem, vmem_limit_bytes=V7X_VMEM_LIMIT_BYTES)


def _sigmoid(x):
    return 1.0 / (1.0 + jnp.exp(-x))


def _dot(a, b, ca, cb):
    return lax.dot_general(a, b, (((ca,), (cb,)), ((), ())), preferred_element_type=F32)


def _my_index():
    return 4 * lax.axis_index("x") + 2 * lax.axis_index("y") + lax.axis_index("c")


def _all_gather(shard, name):
    rows, lanes = shard.shape

    def body(x_ref, out_ref, send_sems, recv_sems, local_sem):
        x, y, c = lax.axis_index("x"), lax.axis_index("y"), lax.axis_index("c")
        me, sibling = (x, y, c), (x, y, 1 - c)
        chips = [(1 - x, y), (x, 1 - y), (1 - x, 1 - y)]

        def slot(px, py, pc):
            return out_ref.at[4 * px + 2 * py + pc]

        def copy(k, block, to, src=None):
            return pltpu.make_async_remote_copy(
                src_ref=slot(*block) if src is None else src,
                dst_ref=slot(*block),
                send_sem=send_sems.at[k],
                recv_sem=recv_sems.at[k],
                device_id=to,
                device_id_type=MESH_ID,
            )

        mine = pltpu.make_async_copy(x_ref, slot(*me), local_sem)
        mine.start()
        first = [copy(0, me, sibling, src=x_ref)]
        first += [copy(1 + j, me, (*chip, c), src=x_ref) for j, chip in enumerate(chips)]
        for cp in first:
            cp.start()
        passed = [copy(4 + j, (*chip, c), sibling) for j, chip in enumerate(chips)]
        for j, chip in enumerate(chips):
            copy(1 + j, (*chip, c), me).wait_recv()
            passed[j].start()
        copy(0, sibling, me).wait_recv()
        for j, chip in enumerate(chips):
            copy(4 + j, (*chip, 1 - c), me).wait_recv()
        for cp in first + passed:
            cp.wait_send()
        mine.wait()

    return pl.pallas_call(
        body,
        name=name,
        out_shape=jax.ShapeDtypeStruct((N_DEV, rows, lanes), shard.dtype),
        in_specs=[pl.BlockSpec(memory_space=pl.ANY)],
        out_specs=pl.BlockSpec(memory_space=pl.ANY),
        scratch_shapes=[
            pltpu.SemaphoreType.DMA((7,)),
            pltpu.SemaphoreType.DMA((7,)),
            pltpu.SemaphoreType.DMA,
        ],
    )(shard)


def _exchange(g, name):
    def body(g_ref, out_ref, send_sems, recv_sems, local_sem):
        x, y, c = lax.axis_index("x"), lax.axis_index("y"), lax.axis_index("c")
        me = 4 * x + 2 * y + c
        local = pltpu.make_async_copy(g_ref.at[me], out_ref.at[me], local_sem)
        local.start()
        copies = []
        for k in range(1, N_DEV):
            px = (x + ((k >> 2) & 1)) % 2
            py = (y + ((k >> 1) & 1)) % 2
            pc = (c + (k & 1)) % 2
            cp = pltpu.make_async_remote_copy(
                src_ref=g_ref.at[4 * px + 2 * py + pc],
                dst_ref=out_ref.at[me],
                send_sem=send_sems.at[k - 1],
                recv_sem=recv_sems.at[k - 1],
                device_id=(px, py, pc),
                device_id_type=MESH_ID,
            )
            cp.start()
            copies.append(cp)
        for cp in copies:
            cp.wait_recv()
        for cp in copies:
            cp.wait_send()
        local.wait()

    return pl.pallas_call(
        body,
        name=name,
        out_shape=jax.ShapeDtypeStruct(g.shape, g.dtype),
        in_specs=[pl.BlockSpec(memory_space=pl.ANY)],
        out_specs=pl.BlockSpec(memory_space=pl.ANY),
        scratch_shapes=[
            pltpu.SemaphoreType.DMA((N_DEV - 1,)),
            pltpu.SemaphoreType.DMA((N_DEV - 1,)),
            pltpu.SemaphoreType.DMA,
        ],
    )(g)


def _padded(n):
    return -(-n // PACK_QUANTUM) * PACK_QUANTUM


def _pack(parts, dtype, lead=()):
    flat = []
    for p in parts:
        p = p.astype(dtype).reshape(lead + (-1,))
        n = p.shape[-1]
        flat.append(jnp.pad(p, [(0, 0)] * len(lead) + [(0, _padded(n) - n)]))
    return jnp.concatenate(flat, axis=-1).reshape(lead + (-1, 128))


def _unpack(buf, shapes, lead=()):
    flat = buf.reshape(lead + (-1,))
    out, off = [], 0
    for shp in shapes:
        n = 1
        for d in shp:
            n *= d
        out.append(flat[..., off:off + n].reshape(lead + tuple(shp)))
        off += _padded(n)
    return out


def _matmul(a, b, *, ta=False, tb=False, out_dtype=F32, add=None, name):
    if ta:
        kdim, m = a.shape
    else:
        m, kdim = a.shape
    n = b.shape[0] if tb else b.shape[1]
    bm = _tile(m, (1024, 512, 256, 128))
    bn = _tile(n, (512, 256, 128))
    bk = _tile(kdim, (1024, 512, 256, 128))
    nk = kdim // bk

    a_spec = pl.BlockSpec((bk, bm), lambda i, j, k: (k, i)) if ta else pl.BlockSpec((bm, bk), lambda i, j, k: (i, k))
    b_spec = pl.BlockSpec((bn, bk), lambda i, j, k: (j, k)) if tb else pl.BlockSpec((bk, bn), lambda i, j, k: (k, j))
    o_spec = pl.BlockSpec((bm, bn), lambda i, j, k: (i, j))
    in_specs = [a_spec, b_spec] + ([o_spec] if add is not None else [])
    operands = [a, b] + ([add] if add is not None else [])

    def body(*refs):
        if add is not None:
            a_ref, b_ref, add_ref, o_ref, acc_ref = refs
        else:
            a_ref, b_ref, o_ref, acc_ref = refs
        k = pl.program_id(2)

        @pl.when(k == 0)
        def _():
            acc_ref[...] = jnp.zeros_like(acc_ref)

        acc_ref[...] += _dot(a_ref[...].astype(BF16), b_ref[...].astype(BF16), 0 if ta else 1, 1 if tb else 0)

        @pl.when(k == nk - 1)
        def _():
            r = acc_ref[...]
            if add is not None:
                r = r + add_ref[...]
            o_ref[...] = r.astype(o_ref.dtype)

    return pl.pallas_call(
        body,
        name=name,
        grid=(m // bm, n // bn, nk),
        in_specs=in_specs,
        out_specs=o_spec,
        out_shape=jax.ShapeDtypeStruct((m, n), out_dtype),
        scratch_shapes=[pltpu.VMEM((bm, bn), F32)],
        compiler_params=_params(("parallel", "parallel", "arbitrary")),
    )(*operands)


def _rmsnorm_fwd(x, g, *, post_scale=1.0, name):
    r, d = x.shape
    tr = _tile(r, (1024, 512, 256, 128))

    def body(x_ref, g_ref, o_ref):
        xv = x_ref[...]
        y = xv * lax.rsqrt(jnp.mean(xv * xv, axis=-1, keepdims=True) + EPS)
        y = y * g_ref[...]
        if post_scale != 1.0:
            y = y * post_scale
        o_ref[...] = y.astype(o_ref.dtype)

    return pl.pallas_call(
        body,
        name=name,
        grid=(r // tr,),
        in_specs=[pl.BlockSpec((tr, d), lambda i: (i, 0)), pl.BlockSpec((1, d), lambda i: (0, 0))],
        out_specs=pl.BlockSpec((tr, d), lambda i: (i, 0)),
        out_shape=jax.ShapeDtypeStruct((r, d), BF16),
        compiler_params=_params(("parallel",)),
    )(x, g)


def _rmsnorm_bwd(x, g, dy, res, *, post_scale=1.0, name):
    r, d = x.shape
    tr = _tile(r, (512, 256, 128))
    row = pl.BlockSpec((tr, d), lambda i: (i, 0))
    vec = pl.BlockSpec((1, d), lambda i: (0, 0))
    operands = [x, g, dy] + ([res] if res is not None else [])

    def body(*refs):
        if res is not None:
            x_ref, g_ref, dy_ref, res_ref, dx_ref, dg_ref = refs
        else:
            x_ref, g_ref, dy_ref, dx_ref, dg_ref = refs
        xv = x_ref[...]
        dyv = dy_ref[...].astype(F32)
        if post_scale != 1.0:
            dyv = dyv * post_scale
        rs = lax.rsqrt(jnp.mean(xv * xv, axis=-1, keepdims=True) + EPS)
        xn = xv * rs
        dyg = dyv * g_ref[...]
        dx = rs * (dyg - xn * jnp.mean(dyg * xn, axis=-1, keepdims=True))
        if res is not None:
            dx = dx + res_ref[...]
        dx_ref[...] = dx

        @pl.when(pl.program_id(0) == 0)
        def _():
            dg_ref[...] = jnp.zeros_like(dg_ref)

        dg_ref[...] += jnp.sum(dyv * xn, axis=0, keepdims=True)

    return pl.pallas_call(
        body,
        name=name,
        grid=(r // tr,),
        in_specs=[row, vec, row] + ([row] if res is not None else []),
        out_specs=(row, vec),
        out_shape=(jax.ShapeDtypeStruct((r, d), F32), jax.ShapeDtypeStruct((1, d), F32)),
        compiler_params=_params(("arbitrary",)),
    )(*operands)


def _loss_head(y, target, *, name):
    s, d = y.shape
    ts = _tile(s, (512, 256, 128))
    row = pl.BlockSpec((ts, d), lambda i: (i, 0))
    acc = pl.BlockSpec((8, 128), lambda i: (0, 0))

    def body(y_ref, t_ref, l_ref, dy_ref):
        e = y_ref[...] - t_ref[...]
        dy_ref[...] = e * (1.0 / d)

        @pl.when(pl.program_id(0) == 0)
        def _():
            l_ref[...] = jnp.zeros_like(l_ref)

        sq = jnp.sum(jnp.sum(e * e, axis=-1, keepdims=True), axis=0, keepdims=True)
        l_ref[...] += jnp.broadcast_to(sq, l_ref.shape)

    return pl.pallas_call(
        body,
        name=name,
        grid=(s // ts,),
        in_specs=[row, row],
        out_specs=(acc, row),
        out_shape=(jax.ShapeDtypeStruct((8, 128), F32), jax.ShapeDtypeStruct((s, d), F32)),
        compiler_params=_params(("arbitrary",)),
    )(y, target)


def _ffn_tiles(s, f):
    return _tile(s, (256, 128)), _tile(f, (512, 256, 128))


def _ffn_act_fwd(ug, uv, wg, wv, bg, bv, *, name):
    s, f = ug.shape
    ts, tc = _ffn_tiles(s, f)
    hb = ts // S_HALO
    main = pl.BlockSpec((ts, tc), lambda j, i: (i, j))
    prev = pl.BlockSpec((S_HALO, tc), lambda j, i: (jnp.maximum(i * hb - 1, 0), j))
    taps = pl.BlockSpec((3, tc), lambda j, i: (0, j))
    bias = pl.BlockSpec((1, tc), lambda j, i: (0, j))

    def body(ug_ref, ugp_ref, uv_ref, uvp_ref, wg_ref, wv_ref, bg_ref, bv_ref, o_ref, gbuf, vbuf):
        first = pl.program_id(1) == 0

        def conv(m_ref, p_ref, w_ref, b_ref, buf):
            buf[0:S_HALO, :] = jnp.where(first, 0.0, p_ref[...])
            buf[S_HALO:, :] = m_ref[...]
            w = w_ref[...]
            acc = b_ref[...] + buf[S_HALO - 2:S_HALO - 2 + ts, :] * w[0:1, :]
            acc = acc + buf[S_HALO - 1:S_HALO - 1 + ts, :] * w[1:2, :]
            return acc + buf[S_HALO:, :] * w[2:3, :]

        gate = conv(ug_ref, ugp_ref, wg_ref, bg_ref, gbuf)
        val = conv(uv_ref, uvp_ref, wv_ref, bv_ref, vbuf)
        o_ref[...] = (gate * _sigmoid(gate) * val).astype(o_ref.dtype)

    return pl.pallas_call(
        body,
        name=name,
        grid=(f // tc, s // ts),
        in_specs=[main, prev, main, prev, taps, taps, bias, bias],
        out_specs=main,
        out_shape=jax.ShapeDtypeStruct((s, f), BF16),
        scratch_shapes=[pltpu.VMEM((S_HALO + ts, tc), F32), pltpu.VMEM((S_HALO + ts, tc), F32)],
        compiler_params=_params(("parallel", "arbitrary")),
    )(ug, ug, uv, uv, wg, wv, bg, bv)


def _ffn_act_bwd(ug, uv, df, wg, wv, bg, bv, *, name):
    s, f = ug.shape
    ts, tc = _ffn_tiles(s, f)
    hb = ts // S_HALO
    last_hb = s // S_HALO - 1
    n_row = s // ts
    main = pl.BlockSpec((ts, tc), lambda j, i: (i, j))
    prev = pl.BlockSpec((S_HALO, tc), lambda j, i: (jnp.maximum(i * hb - 1, 0), j))
    nxt = pl.BlockSpec((S_HALO, tc), lambda j, i: (jnp.minimum((i + 1) * hb, last_hb), j))
    taps = pl.BlockSpec((3, tc), lambda j, i: (0, j))
    bias = pl.BlockSpec((1, tc), lambda j, i: (0, j))
    tacc = pl.BlockSpec((8, tc), lambda j, i: (0, j))
    ext = ts + S_HALO

    def body(ug_ref, ugp_ref, ugn_ref, uv_ref, uvp_ref, uvn_ref, df_ref, dfn_ref, wg_ref, wv_ref, bg_ref, bv_ref,
             dug_ref, duv_ref, dwg_ref, dwv_ref, dbg_ref, dbv_ref, gbuf, vbuf, dgbuf, dvbuf):
        i = pl.program_id(1)
        first = i == 0
        last = i == n_row - 1

        @pl.when(first)
        def _():
            dwg_ref[...] = jnp.zeros_like(dwg_ref)
            dwv_ref[...] = jnp.zeros_like(dwv_ref)
            dbg_ref[...] = jnp.zeros_like(dbg_ref)
            dbv_ref[...] = jnp.zeros_like(dbv_ref)

        def fill(buf, p_ref, m_ref, n_ref):
            buf[0:S_HALO, :] = jnp.where(first, 0.0, p_ref[...])
            buf[S_HALO:S_HALO + ts, :] = m_ref[...]
            buf[S_HALO + ts:, :] = n_ref[...]

        def conv_ext(buf, w_ref, b_ref):
            w = w_ref[...]
            acc = b_ref[...] + buf[S_HALO - 2:S_HALO - 2 + ext, :] * w[0:1, :]
            acc = acc + buf[S_HALO - 1:S_HALO - 1 + ext, :] * w[1:2, :]
            return acc + buf[S_HALO:S_HALO + ext, :] * w[2:3, :]

        fill(gbuf, ugp_ref, ug_ref, ugn_ref)
        fill(vbuf, uvp_ref, uv_ref, uvn_ref)
        gate = conv_ext(gbuf, wg_ref, bg_ref)
        val = conv_ext(vbuf, wv_ref, bv_ref)
        dgbuf[0:ts, :] = df_ref[...]
        dgbuf[ts:, :] = jnp.where(last, 0.0, dfn_ref[...])
        dfe = dgbuf[...]
        sg = _sigmoid(gate)
        dgate = dfe * val * (sg * (1.0 + gate * (1.0 - sg)))
        dval = dfe * (gate * sg)
        dgbuf[...] = dgate
        dvbuf[...] = dval

        def back(dbuf, ubuf, w_ref, du_ref, dw_ref, db_ref):
            w = w_ref[...]
            du = dbuf[2:2 + ts, :] * w[0:1, :] + dbuf[1:1 + ts, :] * w[1:2, :] + dbuf[0:ts, :] * w[2:3, :]
            du_ref[...] = du.astype(du_ref.dtype)
            dm = dbuf[0:ts, :]
            db_ref[...] += jnp.sum(dm, axis=0, keepdims=True)
            for k in range(3):
                dw_ref[k:k + 1, :] += jnp.sum(dm * ubuf[S_HALO - 2 + k:S_HALO - 2 + k + ts, :], axis=0, keepdims=True)

        back(dgbuf, gbuf, wg_ref, dug_ref, dwg_ref, dbg_ref)
        back(dvbuf, vbuf, wv_ref, duv_ref, dwv_ref, dbv_ref)

    return pl.pallas_call(
        body,
        name=name,
        grid=(f // tc, n_row),
        in_specs=[main, prev, nxt, main, prev, nxt, main, nxt, taps, taps, bias, bias],
        out_specs=(main, main, tacc, tacc, bias, bias),
        out_shape=(
            jax.ShapeDtypeStruct((s, f), BF16), jax.ShapeDtypeStruct((s, f), BF16),
            jax.ShapeDtypeStruct((8, f), F32), jax.ShapeDtypeStruct((8, f), F32),
            jax.ShapeDtypeStruct((1, f), F32), jax.ShapeDtypeStruct((1, f), F32),
        ),
        scratch_shapes=[
            pltpu.VMEM((S_HALO + ext, tc), F32), pltpu.VMEM((S_HALO + ext, tc), F32),
            pltpu.VMEM((ext, tc), F32), pltpu.VMEM((ext, tc), F32),
        ],
        compiler_params=_params(("parallel", "arbitrary")),
    )(ug, ug, ug, uv, uv, uv, df, df, wg, wv, bg, bv)


def _layernorm_stats(a1):
    mu = jnp.mean(a1, axis=-1, keepdims=True)
    xc = a1 - mu
    rstd = lax.rsqrt(jnp.mean(xc * xc, axis=-1, keepdims=True) + EPS)
    return xc * rstd, rstd


def _conv_act_fwd(p, wa, ba, lng, lnb, wb, *, name):
    s, c5 = p.shape
    c = c5 // 5
    ka, kb = wa.shape[0], wb.shape[0]
    ts = _tile(s, (256, 128))
    hb = ts // A_HALO
    main = pl.BlockSpec((ts, c5), lambda i: (i, 0))
    prev = pl.BlockSpec((A_HALO, c5), lambda i: (jnp.maximum(i * hb - 1, 0), 0))

    def full(arr):
        return pl.BlockSpec(arr.shape, lambda i: (0, 0))

    def body(p_ref, pp_ref, wa_ref, ba_ref, lng_ref, lnb_ref, wb_ref, ab_ref, a1_ref, gbuf, cbuf):
        first = pl.program_id(0) == 0
        gbuf[0:A_HALO, :] = jnp.where(first, 0.0, pp_ref[:, 0:c] * _sigmoid(pp_ref[:, c:2 * c]))
        gbuf[A_HALO:, :] = p_ref[:, 0:c] * _sigmoid(p_ref[:, c:2 * c])
        cbuf[0:A_HALO, :] = jnp.where(first, 0.0, pp_ref[:, 3 * c:4 * c] * pp_ref[:, 4 * c:5 * c])
        cbuf[A_HALO:, :] = p_ref[:, 3 * c:4 * c] * p_ref[:, 4 * c:5 * c]

        a1 = jnp.broadcast_to(ba_ref[...], (ts, c))
        for k in range(ka):
            off = A_HALO - (ka - 1) + k
            a1 = a1 + gbuf[off:off + ts, :] * wa_ref[k:k + 1, :]
        a1_ref[...] = a1
        xhat, _ = _layernorm_stats(a1)
        a2 = xhat * lng_ref[...] + lnb_ref[...]
        ab_ref[:, 0:c] = (a2 * _sigmoid(a2)).astype(ab_ref.dtype)

        cv = jnp.zeros((ts, c), F32)
        for k in range(kb):
            off = A_HALO - (kb - 1) + k
            cv = cv + cbuf[off:off + ts, :] * wb_ref[k:k + 1, :]
        ab_ref[:, c:2 * c] = (p_ref[:, 2 * c:3 * c] * cv).astype(ab_ref.dtype)

    return pl.pallas_call(
        body,
        name=name,
        grid=(s // ts,),
        in_specs=[main, prev, full(wa), full(ba), full(lng), full(lnb), full(wb)],
        out_specs=(pl.BlockSpec((ts, 2 * c), lambda i: (i, 0)), pl.BlockSpec((ts, c), lambda i: (i, 0))),
        out_shape=(jax.ShapeDtypeStruct((s, 2 * c), BF16), jax.ShapeDtypeStruct((s, c), F32)),
        scratch_shapes=[pltpu.VMEM((A_HALO + ts, c), F32), pltpu.VMEM((A_HALO + ts, c), F32)],
        compiler_params=_params(("parallel",)),
    )(p, p, wa, ba, lng, lnb, wb)


def _conv_act_bwd(p, a1, dab, wa, lng, lnb, wb, *, name):
    s, c5 = p.shape
    c = c5 // 5
    ka, kb = wa.shape[0], wb.shape[0]
    ts = _tile(s, (256, 128))
    hb = ts // A_HALO
    n_row = s // ts
    last_hb = s // A_HALO - 1
    ext = ts + A_HALO

    def rows(width):
        return (pl.BlockSpec((ts, width), lambda i: (i, 0)),
                pl.BlockSpec((A_HALO, width), lambda i: (jnp.maximum(i * hb - 1, 0), 0)),
                pl.BlockSpec((A_HALO, width), lambda i: (jnp.minimum((i + 1) * hb, last_hb), 0)))

    p_main, p_prev, p_next = rows(c5)
    d_main, _, d_next = rows(2 * c)
    a_main, _, a_next = rows(c)

    def full(shape):
        return pl.BlockSpec(shape, lambda i: (0, 0))

    def body(p_ref, pp_ref, pn_ref, a1_ref, a1n_ref, d_ref, dn_ref, wa_ref, lng_ref, lnb_ref, wb_ref,
             dp_ref, dwa_ref, dba_ref, dlng_ref, dlnb_ref, dwb_ref, gbuf, cbuf, da1buf, dcvbuf):
        i = pl.program_id(0)
        first = i == 0
        last = i == n_row - 1

        @pl.when(first)
        def _():
            dwa_ref[...] = jnp.zeros_like(dwa_ref)
            dba_ref[...] = jnp.zeros_like(dba_ref)
            dlng_ref[...] = jnp.zeros_like(dlng_ref)
            dlnb_ref[...] = jnp.zeros_like(dlnb_ref)
            dwb_ref[...] = jnp.zeros_like(dwb_ref)

        sig_gate = _sigmoid(p_ref[:, c:2 * c])
        gbuf[0:A_HALO, :] = jnp.where(first, 0.0, pp_ref[:, 0:c] * _sigmoid(pp_ref[:, c:2 * c]))
        gbuf[A_HALO:, :] = p_ref[:, 0:c] * sig_gate
        cbuf[0:A_HALO, :] = jnp.where(first, 0.0, pp_ref[:, 3 * c:4 * c] * pp_ref[:, 4 * c:5 * c])
        cbuf[A_HALO:, :] = p_ref[:, 3 * c:4 * c] * p_ref[:, 4 * c:5 * c]

        def ln_back(a1v, dav):
            xhat, rstd = _layernorm_stats(a1v)
            a2 = xhat * lng_ref[...] + lnb_ref[...]
            sg = _sigmoid(a2)
            da2 = dav * (sg * (1.0 + a2 * (1.0 - sg)))
            dxh = da2 * lng_ref[...]
            da1 = rstd * (dxh - jnp.mean(dxh, axis=-1, keepdims=True)
                          - xhat * jnp.mean(dxh * xhat, axis=-1, keepdims=True))
            return da1, da2, xhat

        da1_m, da2_m, xhat_m = ln_back(a1_ref[...], d_ref[:, 0:c])
        da1_n, _, _ = ln_back(a1n_ref[...], dn_ref[:, 0:c])
        da1buf[0:ts, :] = da1_m
        da1buf[ts:, :] = jnp.where(last, 0.0, da1_n)
        dlng_ref[...] += jnp.sum(da2_m * xhat_m, axis=0, keepdims=True)
        dlnb_ref[...] += jnp.sum(da2_m, axis=0, keepdims=True)
        dba_ref[...] += jnp.sum(da1_m, axis=0, keepdims=True)

        dglu = jnp.zeros((ts, c), F32)
        for k in range(ka):
            off = (ka - 1) - k
            dglu = dglu + da1buf[off:off + ts, :] * wa_ref[k:k + 1, :]
            goff = A_HALO - (ka - 1) + k
            dwa_ref[k:k + 1, :] += jnp.sum(da1_m * gbuf[goff:goff + ts, :], axis=0, keepdims=True)
        dp_ref[:, 0:c] = (dglu * sig_gate).astype(dp_ref.dtype)
        dp_ref[:, c:2 * c] = (dglu * p_ref[:, 0:c] * sig_gate * (1.0 - sig_gate)).astype(dp_ref.dtype)

        cv = jnp.zeros((ts, c), F32)
        for k in range(kb):
            off = A_HALO - (kb - 1) + k
            cv = cv + cbuf[off:off + ts, :] * wb_ref[k:k + 1, :]
        db_m = d_ref[:, c:2 * c]
        dp_ref[:, 2 * c:3 * c] = (db_m * cv).astype(dp_ref.dtype)
        dcv_m = db_m * p_ref[:, 2 * c:3 * c]
        dcvbuf[0:ts, :] = dcv_m
        dcvbuf[ts:, :] = jnp.where(last, 0.0, dn_ref[:, c:2 * c] * pn_ref[:, 2 * c:3 * c])
        dbc = jnp.zeros((ts, c), F32)
        for k in range(kb):
            off = (kb - 1) - k
            dbc = dbc + dcvbuf[off:off + ts, :] * wb_ref[k:k + 1, :]
            coff = A_HALO - (kb - 1) + k
            dwb_ref[k:k + 1, :] += jnp.sum(dcv_m * cbuf[coff:coff + ts, :], axis=0, keepdims=True)
        dp_ref[:, 3 * c:4 * c] = (dbc * p_ref[:, 4 * c:5 * c]).astype(dp_ref.dtype)
        dp_ref[:, 4 * c:5 * c] = (dbc * p_ref[:, 3 * c:4 * c]).astype(dp_ref.dtype)

    return pl.pallas_call(
        body,
        name=name,
        grid=(n_row,),
        in_specs=[p_main, p_prev, p_next, a_main, a_next, d_main, d_next,
                  full(wa.shape), full(lng.shape), full(lnb.shape), full(wb.shape)],
        out_specs=(pl.BlockSpec((ts, c5), lambda i: (i, 0)), full((32, c)), full((1, c)), full((1, c)),
                   full((1, c)), full((8, c))),
        out_shape=(
            jax.ShapeDtypeStruct((s, c5), BF16), jax.ShapeDtypeStruct((32, c), F32),
            jax.ShapeDtypeStruct((1, c), F32), jax.ShapeDtypeStruct((1, c), F32),
            jax.ShapeDtypeStruct((1, c), F32), jax.ShapeDtypeStruct((8, c), F32),
        ),
        scratch_shapes=[
            pltpu.VMEM((A_HALO + ts, c), F32), pltpu.VMEM((A_HALO + ts, c), F32),
            pltpu.VMEM((ext, c), F32), pltpu.VMEM((ext, c), F32),
        ],
        compiler_params=_params(("arbitrary",)),
    )(p, p, p, a1, a1, dab, dab, wa, lng, lnb, wb)


def _split_dot(v, tri):
    hi = v.astype(BF16)
    lo = (v - hi.astype(F32)).astype(BF16)
    return _dot(hi, tri, 1, 0) + _dot(lo, tri, 1, 0)


def _attn_block_scores(q, kb, tri_gt, valid, r_run, masked):
    z = _dot(q, kb, 1, 1)
    lk = jnp.minimum(-z, 0.0) - jnp.log1p(jnp.exp(-jnp.abs(z)))
    if masked:
        lk = jnp.where(valid, lk, 0.0)
    after = _split_dot(lk, tri_gt)
    w = jnp.exp(z + lk + after + r_run)
    if masked:
        w = jnp.where(valid, w, 0.0)
    return z, lk, w


def _attn_fwd(q, k, v, *, name):
    h, s, dh = q.shape
    t = _tile(s, (ATTN_BLOCK, 128))
    qspec = pl.BlockSpec((1, t, dh), lambda hh, i: (hh, i, 0))
    kspec = pl.BlockSpec((1, s, dh), lambda hh, i: (hh, 0, 0))

    def body(q_ref, k_ref, v_ref, o_ref):
        i = pl.program_id(1)
        qv = q_ref[0]
        r_io = lax.broadcasted_iota(jnp.int32, (t, t), 0)
        c_io = lax.broadcasted_iota(jnp.int32, (t, t), 1)
        tri_gt = (r_io > c_io).astype(BF16)
        valid = c_io < r_io

        def block(j, carry, masked):
            r_run, acc = carry
            start = pl.multiple_of(j * t, t)
            kb = k_ref[0, pl.ds(start, t), :]
            vb = v_ref[0, pl.ds(start, t), :]
            _, lk, w = _attn_block_scores(qv, kb, tri_gt, valid, r_run, masked)
            acc = acc + _dot(w.astype(BF16), vb, 1, 0)
            return r_run + jnp.sum(lk, axis=-1, keepdims=True), acc

        carry = block(i, (jnp.zeros((t, 1), F32), jnp.zeros((t, dh), F32)), True)
        carry = lax.fori_loop(0, i, lambda jj, cr: block(i - 1 - jj, cr, False), carry)
        o_ref[0] = carry[1]

    return pl.pallas_call(
        body,
        name=name,
        grid=(h, s // t),
        in_specs=[qspec, kspec, kspec],
        out_specs=qspec,
        out_shape=jax.ShapeDtypeStruct((h, s, dh), F32),
        compiler_params=_params(("parallel", "arbitrary")),
    )(q, k, v)


def _attn_bwd(q, k, v, o, do, *, name):
    h, s, dh = q.shape
    t = _tile(s, (ATTN_BLOCK, 128))
    qspec = pl.BlockSpec((1, t, dh), lambda hh, i: (hh, i, 0))
    kspec = pl.BlockSpec((1, s, dh), lambda hh, i: (hh, 0, 0))

    def body(q_ref, k_ref, v_ref, o_ref, do_ref, dq_ref, dk_ref, dv_ref):
        i = pl.program_id(1)

        @pl.when(i == 0)
        def _():
            dk_ref[...] = jnp.zeros_like(dk_ref)
            dv_ref[...] = jnp.zeros_like(dv_ref)

        qv = q_ref[0]
        do16 = do_ref[0].astype(BF16)
        delta = jnp.sum(do16.astype(F32) * o_ref[0], axis=-1, keepdims=True)
        r_io = lax.broadcasted_iota(jnp.int32, (t, t), 0)
        c_io = lax.broadcasted_iota(jnp.int32, (t, t), 1)
        tri_gt = (r_io > c_io).astype(BF16)
        tri_ge = (r_io >= c_io).astype(BF16)
        valid = c_io < r_io

        def block(j, carry, masked):
            r_run, e_run, dq = carry
            start = pl.multiple_of(j * t, t)
            kb = k_ref[0, pl.ds(start, t), :]
            vb = v_ref[0, pl.ds(start, t), :]
            z, lk, w = _attn_block_scores(qv, kb, tri_gt, valid, r_run, masked)
            beta = jnp.exp(z + lk)
            w16 = w.astype(BF16)
            e = _dot(do16, vb, 1, 1) * w16.astype(F32)
            e_before = delta - e_run - _split_dot(e, tri_ge)
            dz = e - beta * (e + e_before)
            if masked:
                dz = jnp.where(valid, dz, 0.0)
            dz16 = dz.astype(BF16)
            dq = dq + _dot(dz16, kb, 1, 0)
            dk_ref[0, pl.ds(start, t), :] += _dot(dz16, qv, 0, 0)
            dv_ref[0, pl.ds(start, t), :] += _dot(w16, do16, 0, 0)
            return (r_run + jnp.sum(lk, axis=-1, keepdims=True), e_run + jnp.sum(e, axis=-1, keepdims=True), dq)

        zero = jnp.zeros((t, 1), F32)
        carry = block(i, (zero, zero, jnp.zeros((t, dh), F32)), True)
        carry = lax.fori_loop(0, i, lambda jj, cr: block(i - 1 - jj, cr, False), carry)
        dq_ref[0] = carry[2]

    shp = jax.ShapeDtypeStruct((h, s, dh), F32)
    return pl.pallas_call(
        body,
        name=name,
        grid=(h, s // t),
        in_specs=[qspec, kspec, kspec, qspec, qspec],
        out_specs=(qspec, kspec, kspec),
        out_shape=(shp, shp, shp),
        compiler_params=_params(("parallel", "arbitrary")),
    )(q, k, v, o, do)


def _sum_slabs(r, *, name):
    n, rows, lanes = r.shape
    tr = _tile(rows, (1024, 512, 256, 128, 64, 32, 16))

    def body(r_ref, o_ref):
        acc = r_ref[0]
        for d in range(1, n):
            acc = acc + r_ref[d]
        o_ref[...] = acc

    return pl.pallas_call(
        body,
        name=name,
        grid=(rows // tr,),
        in_specs=[pl.BlockSpec((n, tr, lanes), lambda i: (0, i, 0))],
        out_specs=pl.BlockSpec((tr, lanes), lambda i: (i, 0)),
        out_shape=jax.ShapeDtypeStruct((rows, lanes), F32),
        compiler_params=_params(("parallel",)),
    )(r)


def _adamw(w, g, m, v, *, name):
    rows, lanes = w.shape
    tr = _tile(rows, (1024, 512, 256, 128, 64, 32, 16))
    spec = pl.BlockSpec((tr, lanes), lambda i: (i, 0))
    c1 = 1.0 - ADAM_B1 ** ADAM_STEP
    c2 = 1.0 - ADAM_B2 ** ADAM_STEP

    def body(w_ref, g_ref, m_ref, v_ref, d_ref, nm_ref, nv_ref):
        gv = g_ref[...]
        nm = ADAM_B1 * m_ref[...] + (1.0 - ADAM_B1) * gv
        nv = ADAM_B2 * v_ref[...] + (1.0 - ADAM_B2) * (gv * gv)
        nm_ref[...] = nm
        nv_ref[...] = nv
        d_ref[...] = -ADAM_LR * ((nm / c1) / (jnp.sqrt(nv / c2) + ADAM_EPS) + ADAM_WD * w_ref[...])

    shp = jax.ShapeDtypeStruct((rows, lanes), F32)
    return pl.pallas_call(
        body,
        name=name,
        grid=(rows // tr,),
        in_specs=[spec, spec, spec, spec],
        out_specs=(spec, spec, spec),
        out_shape=(shp, shp, shp),
        compiler_params=_params(("parallel",)),
    )(w, g, m, v)


_MATMUL_WEIGHTS = (("conv_w_in", 2), ("conv_w_out", 1), ("attn_w_qkv", 2), ("attn_w_o", 1), ("ffn_w_up", 2),
                   ("ffn_w_down", 1))
_TAP_WEIGHTS = (("conv_a_dw_w", 2), ("conv_b_dw_w", 2), ("ffn_dw_w", 2))
_REPLICATED = ("mix_norm_g", "ffn_norm_g", "conv_a_dw_b", "conv_a_ln_g", "conv_a_ln_b", "attn_q_g", "attn_k_g",
               "ffn_dw_b")
_WEIGHT_ORDER = ("mix_norm_g", "ffn_norm_g", "conv_w_in", "conv_a_dw_w", "conv_a_dw_b", "conv_a_ln_g",
                 "conv_a_ln_b", "conv_b_dw_w", "conv_w_out", "attn_w_qkv", "attn_q_g", "attn_k_g", "attn_w_o",
                 "ffn_w_up", "ffn_dw_w", "ffn_dw_b", "ffn_w_down")


def _assemble(gathered, axis):
    n, l, a, b = gathered.shape
    if axis == 2:
        return jnp.transpose(gathered, (1, 2, 0, 3)).reshape(l, a, n * b)
    return jnp.transpose(gathered, (1, 0, 2, 3)).reshape(l, n * a, b)


def _split_shards(full, axis):
    l, a, b = full.shape
    if axis == 2:
        return jnp.transpose(full.reshape(l, a, N_DEV, b // N_DEV), (2, 0, 1, 3))
    return jnp.transpose(full.reshape(l, N_DEV, a // N_DEV, b), (1, 0, 2, 3))


def kernel(x, mix_norm_g, ffn_norm_g, conv_w_in, conv_a_dw_w, conv_a_dw_b, conv_a_ln_g, conv_a_ln_b, conv_b_dw_w, conv_w_out, attn_w_qkv, attn_q_g, attn_k_g, attn_w_o, ffn_w_up, ffn_dw_w, ffn_dw_b, ffn_w_down, loss_target, m_mix_norm_g, m_ffn_norm_g, m_conv_w_in, m_conv_a_dw_w, m_conv_a_dw_b, m_conv_a_ln_g, m_conv_a_ln_b, m_conv_b_dw_w, m_conv_w_out, m_attn_w_qkv, m_attn_q_g, m_attn_k_g, m_attn_w_o, m_ffn_w_up, m_ffn_dw_w, m_ffn_dw_b, m_ffn_w_down, v_mix_norm_g, v_ffn_norm_g, v_conv_w_in, v_conv_a_dw_w, v_conv_a_dw_b, v_conv_a_ln_g, v_conv_a_ln_b, v_conv_b_dw_w, v_conv_w_out, v_attn_w_qkv, v_attn_q_g, v_attn_k_g, v_attn_w_o, v_ffn_w_up, v_ffn_dw_w, v_ffn_dw_b, v_ffn_w_down):
    args = dict(locals())
    weights = {n: args[n] for n in _WEIGHT_ORDER}
    mom_m = {n: args["m_" + n] for n in _WEIGHT_ORDER}
    mom_v = {n: args["v_" + n] for n in _WEIGHT_ORDER}

    _, s, d = x.shape
    depth = mix_norm_g.shape[0]
    dh = attn_q_g.shape[1]
    heads = d // dh
    x0 = x.reshape(s, d)
    target = loss_target.reshape(s, d)

    mm_shapes = [weights[n].shape for n, _ in _MATMUL_WEIGHTS]
    g_mm = _all_gather(_pack([weights[n] for n, _ in _MATMUL_WEIGHTS], BF16), "gather_matmul_weights")
    full = {}
    for (n, axis), part in zip(_MATMUL_WEIGHTS, _unpack(g_mm, mm_shapes, lead=(N_DEV,))):
        full[n] = _assemble(part, axis)
    tap_shapes = [weights[n].shape for n, _ in _TAP_WEIGHTS]
    g_tap = _all_gather(_pack([weights[n] for n, _ in _TAP_WEIGHTS], F32), "gather_tap_weights")
    for (n, axis), part in zip(_TAP_WEIGHTS, _unpack(g_tap, tap_shapes, lead=(N_DEV,))):
        full[n] = _assemble(part, axis)
    f = full["ffn_w_down"].shape[1]
    c = conv_a_ln_g.shape[1]

    saved = []
    xs = x0
    for layer in range(depth):
        i = layer // 2
        sv = {"x_in": xs}
        hn = _rmsnorm_fwd(xs, mix_norm_g[layer:layer + 1], name=f"mix_norm_fwd_{layer}")
        sv["h"] = hn
        if layer % 2 == 0:
            p = _matmul(hn, full["conv_w_in"][i], name=f"conv_in_{layer}")
            ab, a1 = _conv_act_fwd(p, full["conv_a_dw_w"][i], conv_a_dw_b[i:i + 1], conv_a_ln_g[i:i + 1],
                                   conv_a_ln_b[i:i + 1], full["conv_b_dw_w"][i], name=f"conv_act_fwd_{layer}")
            sv.update(p=p, a1=a1, mix=ab)
            xs = _matmul(ab, full["conv_w_out"][i], add=xs, name=f"conv_out_{layer}")
        else:
            qkv = _matmul(hn, full["attn_w_qkv"][i], name=f"attn_qkv_{layer}")
            qkv = jnp.transpose(qkv.reshape(s, 3, heads, dh), (1, 2, 0, 3))
            q_raw = qkv[0].reshape(heads * s, dh)
            k_raw = qkv[1].reshape(heads * s, dh)
            qn = _rmsnorm_fwd(q_raw, attn_q_g[i:i + 1], post_scale=dh ** -0.5, name=f"q_norm_fwd_{layer}")
            kn = _rmsnorm_fwd(k_raw, attn_k_g[i:i + 1], name=f"k_norm_fwd_{layer}")
            qn = qn.reshape(heads, s, dh)
            kn = kn.reshape(heads, s, dh)
            v16 = qkv[2].astype(BF16)
            o = _attn_fwd(qn, kn, v16, name=f"attn_fwd_{layer}")
            o2 = jnp.transpose(o, (1, 0, 2)).reshape(s, heads * dh).astype(BF16)
            sv.update(q_raw=q_raw, k_raw=k_raw, qn=qn, kn=kn, v16=v16, o=o, mix=o2)
            xs = _matmul(o2, full["attn_w_o"][i], add=xs, name=f"attn_out_{layer}")
        sv["x_mid"] = xs
        h2 = _rmsnorm_fwd(xs, ffn_norm_g[layer:layer + 1], name=f"ffn_norm_fwd_{layer}")
        w_up = full["ffn_w_up"][layer]
        ug = _matmul(h2, w_up[:, :f], name=f"ffn_up_gate_{layer}")
        uv = _matmul(h2, w_up[:, f:], name=f"ffn_up_val_{layer}")
        taps = full["ffn_dw_w"][layer]
        bias = ffn_dw_b[layer:layer + 1]
        act = _ffn_act_fwd(ug, uv, taps[:, :f], taps[:, f:], bias[:, :f], bias[:, f:], name=f"ffn_act_fwd_{layer}")
        sv.update(h2=h2, ug=ug, uv=uv, act=act)
        xs = _matmul(act, full["ffn_w_down"][layer], add=xs, name=f"ffn_down_{layer}")
        saved.append(sv)

    sq, dx = _loss_head(xs, target, name="loss_head")
    loss = lax.psum(0.5 * sq[0, 0] / d, ("x", "y", "c"))

    grads = {n: [None] * weights[n].shape[0] for n in _WEIGHT_ORDER}
    for layer in reversed(range(depth)):
        i = layer // 2
        sv = saved[layer]
        w_up = full["ffn_w_up"][layer]
        taps = full["ffn_dw_w"][layer]
        bias = ffn_dw_b[layer:layer + 1]
        dact = _matmul(dx, full["ffn_w_down"][layer], tb=True, name=f"ffn_down_dx_{layer}")
        grads["ffn_w_down"][layer] = _matmul(sv["act"], dx, ta=True, name=f"ffn_down_dw_{layer}")
        dug, duv, dwg, dwv, dbg, dbv = _ffn_act_bwd(sv["ug"], sv["uv"], dact, taps[:, :f], taps[:, f:], bias[:, :f],
                                                   bias[:, f:], name=f"ffn_act_bwd_{layer}")
        grads["ffn_dw_w"][layer] = jnp.concatenate([dwg[:3], dwv[:3]], axis=1)
        grads["ffn_dw_b"][layer] = jnp.concatenate([dbg[0], dbv[0]], axis=0)
        dh2 = _matmul(dug, w_up[:, :f], tb=True, name=f"ffn_up_gate_dx_{layer}")
        dh2 = _matmul(duv, w_up[:, f:], tb=True, add=dh2, name=f"ffn_up_val_dx_{layer}")
        grads["ffn_w_up"][layer] = jnp.concatenate(
            [_matmul(sv["h2"], dug, ta=True, name=f"ffn_up_gate_dw_{layer}"),
             _matmul(sv["h2"], duv, ta=True, name=f"ffn_up_val_dw_{layer}")], axis=1)
        dx, dg = _rmsnorm_bwd(sv["x_mid"], ffn_norm_g[layer:layer + 1], dh2, dx, name=f"ffn_norm_bwd_{layer}")
        grads["ffn_norm_g"][layer] = dg[0]

        if layer % 2 == 0:
            dab = _matmul(dx, full["conv_w_out"][i], tb=True, name=f"conv_out_dx_{layer}")
            grads["conv_w_out"][i] = _matmul(sv["mix"], dx, ta=True, name=f"conv_out_dw_{layer}")
            ka = full["conv_a_dw_w"].shape[1]
            kb = full["conv_b_dw_w"].shape[1]
            dp, dwa, dba, dlng, dlnb, dwb = _conv_act_bwd(
                sv["p"], sv["a1"], dab, full["conv_a_dw_w"][i], conv_a_ln_g[i:i + 1], conv_a_ln_b[i:i + 1],
                full["conv_b_dw_w"][i], name=f"conv_act_bwd_{layer}")
            grads["conv_a_dw_w"][i] = dwa[:ka]
            grads["conv_a_dw_b"][i] = dba[0]
            grads["conv_a_ln_g"][i] = dlng[0]
            grads["conv_a_ln_b"][i] = dlnb[0]
            grads["conv_b_dw_w"][i] = dwb[:kb]
            dhn = _matmul(dp, full["conv_w_in"][i], tb=True, name=f"conv_in_dx_{layer}")
            grads["conv_w_in"][i] = _matmul(sv["h"], dp, ta=True, name=f"conv_in_dw_{layer}")
        else:
            do2 = _matmul(dx, full["attn_w_o"][i], tb=True, name=f"attn_out_dx_{layer}")
            grads["attn_w_o"][i] = _matmul(sv["mix"], dx, ta=True, name=f"attn_out_dw_{layer}")
            do = jnp.transpose(do2.reshape(s, heads, dh), (1, 0, 2))
            dq, dk, dv = _attn_bwd(sv["qn"], sv["kn"], sv["v16"], sv["o"], do, name=f"attn_bwd_{layer}")
            dq_raw, dqg = _rmsnorm_bwd(sv["q_raw"], attn_q_g[i:i + 1], dq.reshape(heads * s, dh), None,
                                       post_scale=dh ** -0.5, name=f"q_norm_bwd_{layer}")
            dk_raw, dkg = _rmsnorm_bwd(sv["k_raw"], attn_k_g[i:i + 1], dk.reshape(heads * s, dh), None,
                                       name=f"k_norm_bwd_{layer}")
            grads["attn_q_g"][i] = dqg[0]
            grads["attn_k_g"][i] = dkg[0]
            dqkv = jnp.stack([dq_raw.reshape(heads, s, dh), dk_raw.reshape(heads, s, dh), dv], axis=0)
            dqkv = jnp.transpose(dqkv, (2, 0, 1, 3)).reshape(s, 3 * heads * dh).astype(BF16)
            dhn = _matmul(dqkv, full["attn_w_qkv"][i], tb=True, name=f"attn_qkv_dx_{layer}")
            grads["attn_w_qkv"][i] = _matmul(sv["h"], dqkv, ta=True, name=f"attn_qkv_dw_{layer}")
        dx, dg = _rmsnorm_bwd(sv["x_in"], mix_norm_g[layer:layer + 1], dhn, dx, name=f"mix_norm_bwd_{layer}")
        grads["mix_norm_g"][layer] = dg[0]

    local = {n: jnp.stack(grads[n], axis=0) for n in _WEIGHT_ORDER}

    me = _my_index()
    send = _pack([_split_shards(local[n], axis) for n, axis in _MATMUL_WEIGHTS], F32, lead=(N_DEV,))
    g_big = _sum_slabs(_exchange(send, "exchange_matmul_grads"), name="sum_matmul_grads")
    small_names = list(_REPLICATED) + [n for n, _ in _TAP_WEIGHTS]
    small_shapes = [local[n].shape for n in small_names]
    g_small = _sum_slabs(_all_gather(_pack([local[n] for n in small_names], F32), "gather_small_grads"),
                         name="sum_small_grads")
    reduced = dict(zip(small_names, _unpack(g_small, small_shapes)))
    final_grads = dict(zip([n for n, _ in _MATMUL_WEIGHTS], _unpack(g_big, mm_shapes)))
    for n in _REPLICATED:
        final_grads[n] = reduced[n]
    for n, axis in _TAP_WEIGHTS:
        final_grads[n] = lax.dynamic_index_in_dim(_split_shards(reduced[n], axis), me, axis=0, keepdims=False)

    order = list(_WEIGHT_ORDER)
    shapes = [weights[n].shape for n in order]
    delta, new_m, new_v = _adamw(
        _pack([weights[n] for n in order], F32), _pack([final_grads[n] for n in order], F32),
        _pack([mom_m[n] for n in order], F32), _pack([mom_v[n] for n in order], F32), name="adamw")
    delta = _unpack(delta, shapes)
    new_m = _unpack(new_m, shapes)
    new_v = _unpack(new_v, shapes)

    return (loss, dx.reshape(x.shape), *[final_grads[n] for n in order], *delta, *new_m, *new_v)
```

```python
import jax
import jax.numpy as jnp
from jax import lax
from jax.experimental import pallas as pl
from jax.experimental.pallas import tpu as pltpu

F32 = jnp.float32
BF16 = jnp.bfloat16
EPS = 1e-6
N_DEV = 8
MESH_ID = pl.DeviceIdType.MESH

ADAM_LR = 0.001
ADAM_B1 = 0.9
ADAM_B2 = 0.999
ADAM_EPS = 1e-08
ADAM_WD = 0.01
ADAM_STEP = 10

V7X_VMEM_LIMIT_BYTES = 48 * 1024 * 1024
PACK_QUANTUM = 2048
A_HALO = 32
S_HALO = 8
LANES = 128
ATTN_Q_BLOCK = 1024
ATTN_K_BLOCK = 256
V7X_ATTN_BWD_VMEM_LIMIT_BYTES = 56 * 1024 * 1024
MATMUL_MAX_SINGLE_K = 3072


def _tile(n, prefs):
    for p in prefs:
        if n % p == 0:
            return p
    return n


def _params(sem=None):
    return pltpu.CompilerParams(dimension_semantics=sem, vmem_limit_bytes=V7X_VMEM_LIMIT_BYTES)


def _sigmoid(x):
    return 1.0 / (1.0 + jnp.exp(-x))


def _dot(a, b, ca, cb):
    return lax.dot_general(a, b, (((ca,), (cb,)), ((), ())), preferred_element_type=F32)


def _my_index():
    return 4 * lax.axis_index("x") + 2 * lax.axis_index("y") + lax.axis_index("c")


def _all_gather(shards, name):
    n = len(shards)

    def body(*refs):
        x_refs, out_refs = refs[:n], refs[n:2 * n]
        send_sems, recv_sems, local_sems = refs[2 * n:]
        x, y, c = lax.axis_index("x"), lax.axis_index("y"), lax.axis_index("c")
        me, sibling = (x, y, c), (x, y, 1 - c)
        chips = [(1 - x, y), (x, 1 - y), (1 - x, 1 - y)]

        def copy(a, k, block, to, src=None):
            slot = out_refs[a].at[4 * block[0] + 2 * block[1] + block[2]]
            return pltpu.make_async_remote_copy(
                src_ref=slot if src is None else src, dst_ref=slot,
                send_sem=send_sems.at[7 * a + k], recv_sem=recv_sems.at[7 * a + k],
                device_id=to, device_id_type=MESH_ID)

        mine = [pltpu.make_async_copy(x_refs[a], out_refs[a].at[4 * x + 2 * y + c], local_sems.at[a])
                for a in range(n)]
        for cp in mine:
            cp.start()
        first = []
        for a in range(n):
            first.append(copy(a, 0, me, sibling, src=x_refs[a]))
            first += [copy(a, 1 + j, me, (*chip, c), src=x_refs[a]) for j, chip in enumerate(chips)]
        for cp in first:
            cp.start()
        passed = []
        for j, chip in enumerate(chips):
            for a in range(n):
                copy(a, 1 + j, (*chip, c), me).wait_recv()
                fwd = copy(a, 4 + j, (*chip, c), sibling)
                fwd.start()
                passed.append(fwd)
        for a in range(n):
            copy(a, 0, sibling, me).wait_recv()
            for j, chip in enumerate(chips):
                copy(a, 4 + j, (*chip, 1 - c), me).wait_recv()
        for cp in first + passed:
            cp.wait_send()
        for cp in mine:
            cp.wait()

    hbm = pl.BlockSpec(memory_space=pl.ANY)
    return pl.pallas_call(
        body,
        name=name,
        out_shape=[jax.ShapeDtypeStruct((N_DEV,) + sh.shape, sh.dtype) for sh in shards],
        in_specs=[hbm] * n,
        out_specs=[hbm] * n,
        scratch_shapes=[
            pltpu.SemaphoreType.DMA((7 * n,)),
            pltpu.SemaphoreType.DMA((7 * n,)),
            pltpu.SemaphoreType.DMA((n,)),
        ],
    )(*shards)


def _exchange(slabs, name):
    n = len(slabs)

    def body(*refs):
        g_refs, out_refs = refs[:n], refs[n:2 * n]
        send_sems, recv_sems, local_sems = refs[2 * n:]
        x, y, c = lax.axis_index("x"), lax.axis_index("y"), lax.axis_index("c")
        me = 4 * x + 2 * y + c
        local = [pltpu.make_async_copy(g_refs[a].at[me], out_refs[a].at[me], local_sems.at[a]) for a in range(n)]
        for cp in local:
            cp.start()
        copies = []
        for k in range(1, N_DEV):
            px = (x + ((k >> 2) & 1)) % 2
            py = (y + ((k >> 1) & 1)) % 2
            pc = (c + (k & 1)) % 2
            for a in range(n):
                cp = pltpu.make_async_remote_copy(
                    src_ref=g_refs[a].at[4 * px + 2 * py + pc], dst_ref=out_refs[a].at[me],
                    send_sem=send_sems.at[7 * a + k - 1], recv_sem=recv_sems.at[7 * a + k - 1],
                    device_id=(px, py, pc), device_id_type=MESH_ID)
                cp.start()
                copies.append(cp)
        for cp in copies:
            cp.wait_recv()
        for cp in copies:
            cp.wait_send()
        for cp in local:
            cp.wait()

    hbm = pl.BlockSpec(memory_space=pl.ANY)
    return pl.pallas_call(
        body,
        name=name,
        out_shape=[jax.ShapeDtypeStruct(g.shape, g.dtype) for g in slabs],
        in_specs=[hbm] * n,
        out_specs=[hbm] * n,
        scratch_shapes=[
            pltpu.SemaphoreType.DMA((7 * n,)),
            pltpu.SemaphoreType.DMA((7 * n,)),
            pltpu.SemaphoreType.DMA((n,)),
        ],
    )(*slabs)


def _assemble_cols(g, *, name):
    n, r, w = g.shape
    tr = _tile(r, (256, 128, 64, 32, 16))

    def body(g_ref, o_ref):
        for j in range(n):
            o_ref[:, j * w:(j + 1) * w] = g_ref[j]

    return pl.pallas_call(
        body,
        name=name,
        grid=(r // tr,),
        in_specs=[pl.BlockSpec((n, tr, w), lambda i: (0, i, 0))],
        out_specs=pl.BlockSpec((tr, n * w), lambda i: (i, 0)),
        out_shape=jax.ShapeDtypeStruct((r, n * w), g.dtype),
        compiler_params=_params(("parallel",)),
    )(g)


def _split_cols(x, *, name):
    r, nw = x.shape
    w = nw // N_DEV
    tr = _tile(r, (256, 128, 64, 32, 16))

    def body(x_ref, o_ref):
        for j in range(N_DEV):
            o_ref[j] = x_ref[:, j * w:(j + 1) * w]

    return pl.pallas_call(
        body,
        name=name,
        grid=(r // tr,),
        in_specs=[pl.BlockSpec((tr, nw), lambda i: (i, 0))],
        out_specs=pl.BlockSpec((N_DEV, tr, w), lambda i: (0, i, 0)),
        out_shape=jax.ShapeDtypeStruct((N_DEV, r, w), x.dtype),
        compiler_params=_params(("parallel",)),
    )(x)


def _padded(n):
    return -(-n // PACK_QUANTUM) * PACK_QUANTUM


def _pack(parts, dtype, lead=()):
    flat = []
    for p in parts:
        p = p.astype(dtype).reshape(lead + (-1,))
        n = p.shape[-1]
        flat.append(jnp.pad(p, [(0, 0)] * len(lead) + [(0, _padded(n) - n)]))
    return jnp.concatenate(flat, axis=-1).reshape(lead + (-1, 128))


def _unpack(buf, shapes, lead=()):
    flat = buf.reshape(lead + (-1,))
    out, off = [], 0
    for shp in shapes:
        n = 1
        for d in shp:
            n *= d
        out.append(flat[..., off:off + n].reshape(lead + tuple(shp)))
        off += _padded(n)
    return out


def _half_if_tiled(n):
    half = n // 2
    return (half,) if n % 2 == 0 and half % LANES == 0 and half <= 1536 else ()


def _matmul(a, b, *, ta=False, tb=False, out_dtype=F32, add=None, name):
    if ta:
        kdim, m = a.shape
    else:
        m, kdim = a.shape
    n = b.shape[0] if tb else b.shape[1]
    bm = _tile(m, (1024,) + _half_if_tiled(m) + (512, 256, 128))
    bn = _tile(n, (512,) + _half_if_tiled(n) + (256, 128))
    bk = kdim if kdim <= MATMUL_MAX_SINGLE_K else _tile(kdim, (1024, 512, 256, 128))
    nk = kdim // bk

    a_spec = pl.BlockSpec((bk, bm), lambda i, j, k: (k, i)) if ta else pl.BlockSpec((bm, bk), lambda i, j, k: (i, k))
    b_spec = pl.BlockSpec((bn, bk), lambda i, j, k: (j, k)) if tb else pl.BlockSpec((bk, bn), lambda i, j, k: (k, j))
    o_spec = pl.BlockSpec((bm, bn), lambda i, j, k: (i, j))
    in_specs = [a_spec, b_spec] + ([o_spec] if add is not None else [])
    operands = [a, b] + ([add] if add is not None else [])

    def product(a_ref, b_ref):
        return _dot(a_ref[...].astype(BF16), b_ref[...].astype(BF16), 0 if ta else 1, 1 if tb else 0)

    def finish(r, refs):
        if add is not None:
            r = r + refs[2][...]
        return r

    def body_single(*refs):
        o_ref = refs[-1]
        o_ref[...] = finish(product(refs[0], refs[1]), refs).astype(o_ref.dtype)

    def body_multi(*refs):
        o_ref, acc_ref = refs[-2], refs[-1]
        k = pl.program_id(2)

        @pl.when(k == 0)
        def _():
            acc_ref[...] = jnp.zeros_like(acc_ref)

        acc_ref[...] += product(refs[0], refs[1])

        @pl.when(k == nk - 1)
        def _():
            o_ref[...] = finish(acc_ref[...], refs).astype(o_ref.dtype)

    return pl.pallas_call(
        body_single if nk == 1 else body_multi,
        name=name,
        grid=(m // bm, n // bn, nk),
        in_specs=in_specs,
        out_specs=o_spec,
        out_shape=jax.ShapeDtypeStruct((m, n), out_dtype),
        scratch_shapes=[] if nk == 1 else [pltpu.VMEM((bm, bn), F32)],
        compiler_params=_params(("parallel", "parallel", "arbitrary")),
    )(*operands)


def _rmsnorm_fwd(x, g, *, post_scale=1.0, name):
    r, d = x.shape
    tr = _tile(r, (1024, 512, 256, 128))

    def body(x_ref, g_ref, o_ref):
        xv = x_ref[...]
        y = xv * lax.rsqrt(jnp.mean(xv * xv, axis=-1, keepdims=True) + EPS)
        y = y * g_ref[...]
        if post_scale != 1.0:
            y = y * post_scale
        o_ref[...] = y.astype(o_ref.dtype)

    return pl.pallas_call(
        body,
        name=name,
        grid=(r // tr,),
        in_specs=[pl.BlockSpec((tr, d), lambda i: (i, 0)), pl.BlockSpec((1, d), lambda i: (0, 0))],
        out_specs=pl.BlockSpec((tr, d), lambda i: (i, 0)),
        out_shape=jax.ShapeDtypeStruct((r, d), BF16),
        compiler_params=_params(("parallel",)),
    )(x, g)


def _rmsnorm_bwd(x, g, dy, res, *, post_scale=1.0, name):
    r, d = x.shape
    tr = _tile(r, (512, 256, 128))
    row = pl.BlockSpec((tr, d), lambda i: (i, 0))
    vec = pl.BlockSpec((1, d), lambda i: (0, 0))
    operands = [x, g, dy] + ([res] if res is not None else [])

    def body(*refs):
        if res is not None:
            x_ref, g_ref, dy_ref, res_ref, dx_ref, dx16_ref, dg_ref = refs
        else:
            x_ref, g_ref, dy_ref, dx_ref, dx16_ref, dg_ref = refs
        xv = x_ref[...]
        dyv = dy_ref[...].astype(F32)
        if post_scale != 1.0:
            dyv = dyv * post_scale
        rs = lax.rsqrt(jnp.mean(xv * xv, axis=-1, keepdims=True) + EPS)
        xn = xv * rs
        dyg = dyv * g_ref[...]
        dx = rs * (dyg - xn * jnp.mean(dyg * xn, axis=-1, keepdims=True))
        if res is not None:
            dx = dx + res_ref[...]
        dx_ref[...] = dx
        dx16_ref[...] = dx.astype(dx16_ref.dtype)

        @pl.when(pl.program_id(0) == 0)
        def _():
            dg_ref[...] = jnp.zeros_like(dg_ref)

        dg_ref[...] += jnp.sum(dyv * xn, axis=0, keepdims=True)

    return pl.pallas_call(
        body,
        name=name,
        grid=(r // tr,),
        in_specs=[row, vec, row] + ([row] if res is not None else []),
        out_specs=(row, row, vec),
        out_shape=(jax.ShapeDtypeStruct((r, d), F32), jax.ShapeDtypeStruct((r, d), BF16),
                   jax.ShapeDtypeStruct((1, d), F32)),
        compiler_params=_params(("arbitrary",)),
    )(*operands)


def _loss_head(y, target, *, name):
    s, d = y.shape
    ts = _tile(s, (512, 256, 128))
    row = pl.BlockSpec((ts, d), lambda i: (i, 0))
    acc = pl.BlockSpec((8, 128), lambda i: (0, 0))

    def body(y_ref, t_ref, l_ref, dy_ref, dy16_ref):
        e = y_ref[...] - t_ref[...]
        dy = e * (1.0 / d)
        dy_ref[...] = dy
        dy16_ref[...] = dy.astype(dy16_ref.dtype)

        @pl.when(pl.program_id(0) == 0)
        def _():
            l_ref[...] = jnp.zeros_like(l_ref)

        sq = jnp.sum(jnp.sum(e * e, axis=-1, keepdims=True), axis=0, keepdims=True)
        l_ref[...] += jnp.broadcast_to(sq, l_ref.shape)

    return pl.pallas_call(
        body,
        name=name,
        grid=(s // ts,),
        in_specs=[row, row],
        out_specs=(acc, row, row),
        out_shape=(jax.ShapeDtypeStruct((8, 128), F32), jax.ShapeDtypeStruct((s, d), F32),
                   jax.ShapeDtypeStruct((s, d), BF16)),
        compiler_params=_params(("arbitrary",)),
    )(y, target)


def _ffn_tiles(s, f):
    return _tile(s, (512, 256, 128)), _tile(f, (512, 256, 128))


def _ffn_act_fwd(ug, uv, wg, wv, bg, bv, *, name):
    s, f = ug.shape
    ts, tc = _ffn_tiles(s, f)
    hb = ts // S_HALO
    main = pl.BlockSpec((ts, tc), lambda j, i: (i, j))
    prev = pl.BlockSpec((S_HALO, tc), lambda j, i: (jnp.maximum(i * hb - 1, 0), j))
    taps = pl.BlockSpec((3, tc), lambda j, i: (0, j))
    bias = pl.BlockSpec((1, tc), lambda j, i: (0, j))

    def body(ug_ref, ugp_ref, uv_ref, uvp_ref, wg_ref, wv_ref, bg_ref, bv_ref, o_ref, gbuf, vbuf):
        first = pl.program_id(1) == 0

        def conv(m_ref, p_ref, w_ref, b_ref, buf):
            buf[0:S_HALO, :] = jnp.where(first, 0.0, p_ref[...])
            buf[S_HALO:, :] = m_ref[...]
            w = w_ref[...]
            acc = b_ref[...] + buf[S_HALO - 2:S_HALO - 2 + ts, :] * w[0:1, :]
            acc = acc + buf[S_HALO - 1:S_HALO - 1 + ts, :] * w[1:2, :]
            return acc + buf[S_HALO:, :] * w[2:3, :]

        gate = conv(ug_ref, ugp_ref, wg_ref, bg_ref, gbuf)
        val = conv(uv_ref, uvp_ref, wv_ref, bv_ref, vbuf)
        o_ref[...] = (gate * _sigmoid(gate) * val).astype(o_ref.dtype)

    return pl.pallas_call(
        body,
        name=name,
        grid=(f // tc, s // ts),
        in_specs=[main, prev, main, prev, taps, taps, bias, bias],
        out_specs=main,
        out_shape=jax.ShapeDtypeStruct((s, f), BF16),
        scratch_shapes=[pltpu.VMEM((S_HALO + ts, tc), F32), pltpu.VMEM((S_HALO + ts, tc), F32)],
        compiler_params=_params(("parallel", "arbitrary")),
    )(ug, ug, uv, uv, wg, wv, bg, bv)


def _ffn_act_bwd(ug, uv, df, wg, wv, bg, bv, *, name):
    s, f = ug.shape
    ts, tc = _ffn_tiles(s, f)
    hb = ts // S_HALO
    last_hb = s // S_HALO - 1
    n_row = s // ts
    main = pl.BlockSpec((ts, tc), lambda j, i: (i, j))
    prev = pl.BlockSpec((S_HALO, tc), lambda j, i: (jnp.maximum(i * hb - 1, 0), j))
    nxt = pl.BlockSpec((S_HALO, tc), lambda j, i: (jnp.minimum((i + 1) * hb, last_hb), j))
    taps = pl.BlockSpec((3, tc), lambda j, i: (0, j))
    bias = pl.BlockSpec((1, tc), lambda j, i: (0, j))
    tacc = pl.BlockSpec((8, tc), lambda j, i: (0, j))
    ext = ts + S_HALO

    def body(ug_ref, ugp_ref, ugn_ref, uv_ref, uvp_ref, uvn_ref, df_ref, dfn_ref, wg_ref, wv_ref, bg_ref, bv_ref,
             dug_ref, duv_ref, dwg_ref, dwv_ref, dbg_ref, dbv_ref, gbuf, vbuf, dgbuf, dvbuf):
        i = pl.program_id(1)
        first = i == 0
        last = i == n_row - 1

        @pl.when(first)
        def _():
            dwg_ref[...] = jnp.zeros_like(dwg_ref)
            dwv_ref[...] = jnp.zeros_like(dwv_ref)
            dbg_ref[...] = jnp.zeros_like(dbg_ref)
            dbv_ref[...] = jnp.zeros_like(dbv_ref)

        def fill(buf, p_ref, m_ref, n_ref):
            buf[0:S_HALO, :] = jnp.where(first, 0.0, p_ref[...])
            buf[S_HALO:S_HALO + ts, :] = m_ref[...]
            buf[S_HALO + ts:, :] = n_ref[...]

        def conv_ext(buf, w_ref, b_ref):
            w = w_ref[...]
            acc = b_ref[...] + buf[S_HALO - 2:S_HALO - 2 + ext, :] * w[0:1, :]
            acc = acc + buf[S_HALO - 1:S_HALO - 1 + ext, :] * w[1:2, :]
            return acc + buf[S_HALO:S_HALO + ext, :] * w[2:3, :]

        fill(gbuf, ugp_ref, ug_ref, ugn_ref)
        fill(vbuf, uvp_ref, uv_ref, uvn_ref)
        gate = conv_ext(gbuf, wg_ref, bg_ref)
        val = conv_ext(vbuf, wv_ref, bv_ref)
        dgbuf[0:ts, :] = df_ref[...]
        dgbuf[ts:, :] = jnp.where(last, 0.0, dfn_ref[...])
        dfe = dgbuf[...]
        sg = _sigmoid(gate)
        dgate = dfe * val * (sg * (1.0 + gate * (1.0 - sg)))
        dval = dfe * (gate * sg)
        dgbuf[...] = dgate
        dvbuf[...] = dval

        def back(dbuf, ubuf, w_ref, du_ref, dw_ref, db_ref):
            w = w_ref[...]
            du = dbuf[2:2 + ts, :] * w[0:1, :] + dbuf[1:1 + ts, :] * w[1:2, :] + dbuf[0:ts, :] * w[2:3, :]
            du_ref[...] = du.astype(du_ref.dtype)
            dm = dbuf[0:ts, :]
            db_ref[...] += jnp.sum(dm, axis=0, keepdims=True)
            for k in range(3):
                dw_ref[k:k + 1, :] += jnp.sum(dm * ubuf[S_HALO - 2 + k:S_HALO - 2 + k + ts, :], axis=0, keepdims=True)

        back(dgbuf, gbuf, wg_ref, dug_ref, dwg_ref, dbg_ref)
        back(dvbuf, vbuf, wv_ref, duv_ref, dwv_ref, dbv_ref)

    return pl.pallas_call(
        body,
        name=name,
        grid=(f // tc, n_row),
        in_specs=[main, prev, nxt, main, prev, nxt, main, nxt, taps, taps, bias, bias],
        out_specs=(main, main, tacc, tacc, bias, bias),
        out_shape=(
            jax.ShapeDtypeStruct((s, f), BF16), jax.ShapeDtypeStruct((s, f), BF16),
            jax.ShapeDtypeStruct((8, f), F32), jax.ShapeDtypeStruct((8, f), F32),
            jax.ShapeDtypeStruct((1, f), F32), jax.ShapeDtypeStruct((1, f), F32),
        ),
        scratch_shapes=[
            pltpu.VMEM((S_HALO + ext, tc), F32), pltpu.VMEM((S_HALO + ext, tc), F32),
            pltpu.VMEM((ext, tc), F32), pltpu.VMEM((ext, tc), F32),
        ],
        compiler_params=_params(("parallel", "arbitrary")),
    )(ug, ug, ug, uv, uv, uv, df, df, wg, wv, bg, bv)


def _layernorm_stats(a1):
    mu = jnp.mean(a1, axis=-1, keepdims=True)
    xc = a1 - mu
    rstd = lax.rsqrt(jnp.mean(xc * xc, axis=-1, keepdims=True) + EPS)
    return xc * rstd, rstd


def _conv_act_fwd(p, wa, ba, lng, lnb, wb, *, name):
    s, c5 = p.shape
    c = c5 // 5
    ka, kb = wa.shape[0], wb.shape[0]
    ts = _tile(s, (256, 128))
    hb = ts // A_HALO
    main = pl.BlockSpec((ts, c5), lambda i: (i, 0))
    prev = pl.BlockSpec((A_HALO, c5), lambda i: (jnp.maximum(i * hb - 1, 0), 0))

    def full(arr):
        return pl.BlockSpec(arr.shape, lambda i: (0, 0))

    def body(p_ref, pp_ref, wa_ref, ba_ref, lng_ref, lnb_ref, wb_ref, ab_ref, a1_ref, gbuf, cbuf):
        first = pl.program_id(0) == 0
        gbuf[0:A_HALO, :] = jnp.where(first, 0.0, pp_ref[:, 0:c] * _sigmoid(pp_ref[:, c:2 * c]))
        gbuf[A_HALO:, :] = p_ref[:, 0:c] * _sigmoid(p_ref[:, c:2 * c])
        cbuf[0:A_HALO, :] = jnp.where(first, 0.0, pp_ref[:, 3 * c:4 * c] * pp_ref[:, 4 * c:5 * c])
        cbuf[A_HALO:, :] = p_ref[:, 3 * c:4 * c] * p_ref[:, 4 * c:5 * c]

        a1 = jnp.broadcast_to(ba_ref[...], (ts, c))
        for k in range(ka):
            off = A_HALO - (ka - 1) + k
            a1 = a1 + gbuf[off:off + ts, :] * wa_ref[k:k + 1, :]
        a1_ref[...] = a1
        xhat, _ = _layernorm_stats(a1)
        a2 = xhat * lng_ref[...] + lnb_ref[...]
        ab_ref[:, 0:c] = (a2 * _sigmoid(a2)).astype(ab_ref.dtype)

        cv = jnp.zeros((ts, c), F32)
        for k in range(kb):
            off = A_HALO - (kb - 1) + k
            cv = cv + cbuf[off:off + ts, :] * wb_ref[k:k + 1, :]
        ab_ref[:, c:2 * c] = (p_ref[:, 2 * c:3 * c] * cv).astype(ab_ref.dtype)

    return pl.pallas_call(
        body,
        name=name,
        grid=(s // ts,),
        in_specs=[main, prev, full(wa), full(ba), full(lng), full(lnb), full(wb)],
        out_specs=(pl.BlockSpec((ts, 2 * c), lambda i: (i, 0)), pl.BlockSpec((ts, c), lambda i: (i, 0))),
        out_shape=(jax.ShapeDtypeStruct((s, 2 * c), BF16), jax.ShapeDtypeStruct((s, c), F32)),
        scratch_shapes=[pltpu.VMEM((A_HALO + ts, c), F32), pltpu.VMEM((A_HALO + ts, c), F32)],
        compiler_params=_params(("parallel",)),
    )(p, p, wa, ba, lng, lnb, wb)


def _conv_act_bwd(p, a1, dab, wa, lng, lnb, wb, *, name):
    s, c5 = p.shape
    c = c5 // 5
    ka, kb = wa.shape[0], wb.shape[0]
    ts = _tile(s, (256, 128))
    hb = ts // A_HALO
    n_row = s // ts
    last_hb = s // A_HALO - 1
    ext = ts + A_HALO

    def rows(width):
        return (pl.BlockSpec((ts, width), lambda i: (i, 0)),
                pl.BlockSpec((A_HALO, width), lambda i: (jnp.maximum(i * hb - 1, 0), 0)),
                pl.BlockSpec((A_HALO, width), lambda i: (jnp.minimum((i + 1) * hb, last_hb), 0)))

    p_main, p_prev, p_next = rows(c5)
    d_main, _, d_next = rows(2 * c)
    a_main, _, a_next = rows(c)

    def full(shape):
        return pl.BlockSpec(shape, lambda i: (0, 0))

    def body(p_ref, pp_ref, pn_ref, a1_ref, a1n_ref, d_ref, dn_ref, wa_ref, lng_ref, lnb_ref, wb_ref,
             dp_ref, dwa_ref, dba_ref, dlng_ref, dlnb_ref, dwb_ref, gbuf, cbuf, da1buf, dcvbuf):
        i = pl.program_id(0)
        first = i == 0
        last = i == n_row - 1

        @pl.when(first)
        def _():
            dwa_ref[...] = jnp.zeros_like(dwa_ref)
            dba_ref[...] = jnp.zeros_like(dba_ref)
            dlng_ref[...] = jnp.zeros_like(dlng_ref)
            dlnb_ref[...] = jnp.zeros_like(dlnb_ref)
            dwb_ref[...] = jnp.zeros_like(dwb_ref)

        sig_gate = _sigmoid(p_ref[:, c:2 * c])
        gbuf[0:A_HALO, :] = jnp.where(first, 0.0, pp_ref[:, 0:c] * _sigmoid(pp_ref[:, c:2 * c]))
        gbuf[A_HALO:, :] = p_ref[:, 0:c] * sig_gate
        cbuf[0:A_HALO, :] = jnp.where(first, 0.0, pp_ref[:, 3 * c:4 * c] * pp_ref[:, 4 * c:5 * c])
        cbuf[A_HALO:, :] = p_ref[:, 3 * c:4 * c] * p_ref[:, 4 * c:5 * c]

        def ln_back(a1v, dav):
            xhat, rstd = _layernorm_stats(a1v)
            a2 = xhat * lng_ref[...] + lnb_ref[...]
            sg = _sigmoid(a2)
            da2 = dav * (sg * (1.0 + a2 * (1.0 - sg)))
            dxh = da2 * lng_ref[...]
            da1 = rstd * (dxh - jnp.mean(dxh, axis=-1, keepdims=True)
                          - xhat * jnp.mean(dxh * xhat, axis=-1, keepdims=True))
            return da1, da2, xhat

        da1_m, da2_m, xhat_m = ln_back(a1_ref[...], d_ref[:, 0:c])
        da1_n, _, _ = ln_back(a1n_ref[...], dn_ref[:, 0:c])
        da1buf[0:ts, :] = da1_m
        da1buf[ts:, :] = jnp.where(last, 0.0, da1_n)
        dlng_ref[...] += jnp.sum(da2_m * xhat_m, axis=0, keepdims=True)
        dlnb_ref[...] += jnp.sum(da2_m, axis=0, keepdims=True)
        dba_ref[...] += jnp.sum(da1_m, axis=0, keepdims=True)

        dglu = jnp.zeros((ts, c), F32)
        for k in range(ka):
            off = (ka - 1) - k
            dglu = dglu + da1buf[off:off + ts, :] * wa_ref[k:k + 1, :]
            goff = A_HALO - (ka - 1) + k
            dwa_ref[k:k + 1, :] += jnp.sum(da1_m * gbuf[goff:goff + ts, :], axis=0, keepdims=True)
        dp_ref[:, 0:c] = (dglu * sig_gate).astype(dp_ref.dtype)
        dp_ref[:, c:2 * c] = (dglu * p_ref[:, 0:c] * sig_gate * (1.0 - sig_gate)).astype(dp_ref.dtype)

        cv = jnp.zeros((ts, c), F32)
        for k in range(kb):
            off = A_HALO - (kb - 1) + k
            cv = cv + cbuf[off:off + ts, :] * wb_ref[k:k + 1, :]
        db_m = d_ref[:, c:2 * c]
        dp_ref[:, 2 * c:3 * c] = (db_m * cv).astype(dp_ref.dtype)
        dcv_m = db_m * p_ref[:, 2 * c:3 * c]
        dcvbuf[0:ts, :] = dcv_m
        dcvbuf[ts:, :] = jnp.where(last, 0.0, dn_ref[:, c:2 * c] * pn_ref[:, 2 * c:3 * c])
        dbc = jnp.zeros((ts, c), F32)
        for k in range(kb):
            off = (kb - 1) - k
            dbc = dbc + dcvbuf[off:off + ts, :] * wb_ref[k:k + 1, :]
            coff = A_HALO - (kb - 1) + k
            dwb_ref[k:k + 1, :] += jnp.sum(dcv_m * cbuf[coff:coff + ts, :], axis=0, keepdims=True)
        dp_ref[:, 3 * c:4 * c] = (dbc * p_ref[:, 4 * c:5 * c]).astype(dp_ref.dtype)
        dp_ref[:, 4 * c:5 * c] = (dbc * p_ref[:, 3 * c:4 * c]).astype(dp_ref.dtype)

    return pl.pallas_call(
        body,
        name=name,
        grid=(n_row,),
        in_specs=[p_main, p_prev, p_next, a_main, a_next, d_main, d_next,
                  full(wa.shape), full(lng.shape), full(lnb.shape), full(wb.shape)],
        out_specs=(pl.BlockSpec((ts, c5), lambda i: (i, 0)), full((32, c)), full((1, c)), full((1, c)),
                   full((1, c)), full((8, c))),
        out_shape=(
            jax.ShapeDtypeStruct((s, c5), BF16), jax.ShapeDtypeStruct((32, c), F32),
            jax.ShapeDtypeStruct((1, c), F32), jax.ShapeDtypeStruct((1, c), F32),
            jax.ShapeDtypeStruct((1, c), F32), jax.ShapeDtypeStruct((8, c), F32),
        ),
        scratch_shapes=[
            pltpu.VMEM((A_HALO + ts, c), F32), pltpu.VMEM((A_HALO + ts, c), F32),
            pltpu.VMEM((ext, c), F32), pltpu.VMEM((ext, c), F32),
        ],
        compiler_params=_params(("arbitrary",)),
    )(p, p, p, a1, a1, dab, dab, wa, lng, lnb, wb)


def _head_pair_geometry(width, dh):
    assert 2 * dh == LANES, "the attention kernels pair two heads in one 128-lane block"
    return width // (3 * LANES)


def _group_sum(v, lo):
    s_lo = jnp.sum(jnp.where(lo, v, 0.0), axis=-1, keepdims=True)
    s_hi = jnp.sum(jnp.where(lo, 0.0, v), axis=-1, keepdims=True)
    return jnp.where(lo, s_lo, s_hi)


def _qkv_prep_fwd(qkv, gains, dh, *, name):
    s, width = qkv.shape
    nsec = _head_pair_geometry(width, dh)
    tr = _tile(s, (1024, 512, 256, 128))
    blk = pl.BlockSpec((tr, LANES), lambda i, cb: (i, cb))

    def body(x_ref, g_ref, o_ref):
        is_v = pl.program_id(1) >= 2 * nsec
        lo = lax.broadcasted_iota(jnp.int32, (1, LANES), 1) < dh
        xv = x_ref[...]
        rs = lax.rsqrt(_group_sum(xv * xv, lo) * (1.0 / dh) + EPS)
        y = xv * rs * g_ref[0:1, :]
        o_ref[...] = jnp.where(is_v, xv, y).astype(o_ref.dtype)

    return pl.pallas_call(
        body,
        name=name,
        grid=(s // tr, 3 * nsec),
        in_specs=[blk, pl.BlockSpec((8, LANES), lambda i, cb: (cb // nsec, 0))],
        out_specs=blk,
        out_shape=jax.ShapeDtypeStruct((s, width), BF16),
        compiler_params=_params(("parallel", "parallel")),
    )(qkv, gains)


def _qkv_prep_bwd(qkv, dq, dk, dv, gains, dh, *, name):
    s, width = qkv.shape
    nsec = _head_pair_geometry(width, dh)
    tr = _tile(s, (1024, 512, 256, 128))
    blk = pl.BlockSpec((tr, LANES), lambda cb, i: (i, cb))
    gspec = pl.BlockSpec((8, LANES), lambda cb, i: (cb // nsec, 0))

    def sec_spec(sec):
        return pl.BlockSpec((tr, LANES), lambda cb, i: (i, jnp.clip(cb - sec * nsec, 0, nsec - 1)))

    def body(x_ref, dq_ref, dk_ref, dv_ref, g_ref, o_ref, dg_ref):
        cb = pl.program_id(0)
        sec = cb // nsec

        @pl.when((cb % nsec == 0) & (pl.program_id(1) == 0))
        def _():
            dg_ref[...] = jnp.zeros_like(dg_ref)

        lo = lax.broadcasted_iota(jnp.int32, (1, LANES), 1) < dh
        d = jnp.where(sec == 0, dq_ref[...], jnp.where(sec == 1, dk_ref[...], dv_ref[...]))
        xv = x_ref[...]
        rs = lax.rsqrt(_group_sum(xv * xv, lo) * (1.0 / dh) + EPS)
        xn = xv * rs
        dyg = d * g_ref[0:1, :]
        dx = rs * (dyg - xn * (_group_sum(dyg * xn, lo) * (1.0 / dh)))
        o_ref[...] = jnp.where(sec == 2, d, dx).astype(o_ref.dtype)
        dg_ref[...] += jnp.broadcast_to(jnp.sum(d * xn, axis=0, keepdims=True), dg_ref.shape)

    return pl.pallas_call(
        body,
        name=name,
        grid=(3 * nsec, s // tr),
        in_specs=[blk, sec_spec(0), sec_spec(1), sec_spec(2), gspec],
        out_specs=(blk, gspec),
        out_shape=(jax.ShapeDtypeStruct((s, width), BF16), jax.ShapeDtypeStruct((24, LANES), F32)),
        compiler_params=_params(("arbitrary", "arbitrary")),
    )(qkv, dq, dk, dv, gains)


def _split_dot(v, tri):
    hi = v.astype(BF16)
    lo = (v - hi.astype(F32)).astype(BF16)
    return _dot(hi, tri, 1, 0) + _dot(lo, tri, 1, 0)


def _attn_scores(qm, kb, tri_gt, valid, r_run):
    z = _dot(qm, kb, 1, 1)
    zl = jnp.minimum(z, 0.0) - jnp.log(1.0 + jnp.exp(-jnp.abs(z)))
    lk = zl - z
    if valid is not None:
        lk = jnp.where(valid, lk, 0.0)
    after = _split_dot(lk, tri_gt)
    w = jnp.exp(zl + after + r_run)
    if valid is not None:
        w = jnp.where(valid, w, 0.0)
    return zl, lk, w


def _attn_specs(s, width, dh, tq):
    nsec = _head_pair_geometry(width, dh)
    qspec = pl.BlockSpec((tq, LANES), lambda h, i: (i, h))
    kspec = pl.BlockSpec((s, LANES), lambda h, i: (0, nsec + h))
    vspec = pl.BlockSpec((s, LANES), lambda h, i: (0, 2 * nsec + h))
    return nsec, qspec, kspec, vspec


def _attn_masks(tq, tk, dh):
    lo = lax.broadcasted_iota(jnp.int32, (1, LANES), 1) < dh
    r_io = lax.broadcasted_iota(jnp.int32, (tk, tk), 0)
    c_io = lax.broadcasted_iota(jnp.int32, (tk, tk), 1)
    qrow = lax.broadcasted_iota(jnp.int32, (tq, tk), 0)
    kcol = lax.broadcasted_iota(jnp.int32, (tq, tk), 1)
    return lo, r_io, c_io, qrow, kcol


def _attn_fwd(qkv16, dh, *, name):
    s, width = qkv16.shape
    tq = _tile(s, (ATTN_Q_BLOCK, ATTN_K_BLOCK))
    tk = _tile(tq, (ATTN_K_BLOCK,))
    nd = tq // tk
    nsec, qspec, kspec, vspec = _attn_specs(s, width, dh, tq)

    def body(q_ref, k_ref, v_ref, o_ref):
        i = pl.program_id(1)
        lo, r_io, c_io, qrow, kcol = _attn_masks(tq, tk, dh)
        q2 = q_ref[...]
        zq = jnp.zeros_like(q2)
        qs = (jnp.where(lo, q2, zq), jnp.where(lo, zq, q2))
        tri_gt = (r_io > c_io).astype(BF16)

        def block(j, carry, diag):
            rr, acc = carry
            start = pl.multiple_of(j * tk, tk)
            kb = k_ref[pl.ds(start, tk), :]
            vb = v_ref[pl.ds(start, tk), :]
            valid = None if diag is None else kcol + diag * tk < qrow
            outs, new_r = [], []
            for hh in range(2):
                _, lk, w = _attn_scores(qs[hh], kb, tri_gt, valid, rr[hh])
                outs.append(_dot(w.astype(BF16), vb, 1, 0))
                new_r.append(rr[hh] + jnp.sum(lk, axis=-1, keepdims=True))
            return tuple(new_r), acc + jnp.where(lo, outs[0], outs[1])

        zero = jnp.zeros((tq, 1), F32)
        carry = ((zero, zero), jnp.zeros((tq, LANES), F32))
        for dg in reversed(range(nd)):
            carry = block(i * nd + dg, carry, dg)
        below = i * nd
        carry = lax.fori_loop(0, below, lambda jj, cr: block(below - 1 - jj, cr, None), carry)
        o_ref[...] = carry[1]

    return pl.pallas_call(
        body,
        name=name,
        grid=(nsec, s // tq),
        in_specs=[qspec, kspec, vspec],
        out_specs=qspec,
        out_shape=jax.ShapeDtypeStruct((s, width // 3), F32),
        compiler_params=_params(("parallel", "arbitrary")),
    )(qkv16, qkv16, qkv16)


def _attn_bwd(qkv16, o, do, dh, *, name):
    s, width = qkv16.shape
    tq = _tile(s, (ATTN_Q_BLOCK, ATTN_K_BLOCK))
    tk = _tile(tq, (ATTN_K_BLOCK,))
    nd = tq // tk
    nsec, qspec, kspec, vspec = _attn_specs(s, width, dh, tq)
    accspec = pl.BlockSpec((s, LANES), lambda h, i: (0, h))

    def body(q_ref, k_ref, v_ref, o_ref, do_ref, dq_ref, dk_ref, dv_ref):
        i = pl.program_id(1)

        @pl.when(i == 0)
        def _():
            dk_ref[...] = jnp.zeros_like(dk_ref)
            dv_ref[...] = jnp.zeros_like(dv_ref)

        lo, r_io, c_io, qrow, kcol = _attn_masks(tq, tk, dh)
        q2 = q_ref[...]
        zq = jnp.zeros_like(q2)
        qs = (jnp.where(lo, q2, zq), jnp.where(lo, zq, q2))
        do16 = do_ref[...].astype(BF16)
        dos = (jnp.where(lo, do16, zq), jnp.where(lo, zq, do16))
        prod = do16.astype(F32) * o_ref[...]
        deltas = (jnp.sum(jnp.where(lo, prod, 0.0), axis=-1, keepdims=True),
                  jnp.sum(jnp.where(lo, 0.0, prod), axis=-1, keepdims=True))
        tri_gt = (r_io > c_io).astype(BF16)
        tri_ge = (r_io >= c_io).astype(BF16)

        def block(j, carry, diag):
            rr, er, dq = carry
            start = pl.multiple_of(j * tk, tk)
            kb = k_ref[pl.ds(start, tk), :]
            vb = v_ref[pl.ds(start, tk), :]
            valid = None if diag is None else kcol + diag * tk < qrow
            new_r, new_e, dqs = [], [], []
            dk_blk = dv_blk = None
            for hh in range(2):
                zl, lk, w = _attn_scores(qs[hh], kb, tri_gt, valid, rr[hh])
                beta = jnp.exp(zl)
                w16 = w.astype(BF16)
                e = _dot(dos[hh], vb, 1, 1) * w16.astype(F32)
                e_before = deltas[hh] - er[hh] - _split_dot(e, tri_ge)
                dz = e - beta * (e + e_before)
                if valid is not None:
                    dz = jnp.where(valid, dz, 0.0)
                dz16 = dz.astype(BF16)
                dqs.append(_dot(dz16, kb, 1, 0))
                dkc = _dot(dz16, qs[hh], 0, 0)
                dvc = _dot(w16, dos[hh], 0, 0)
                dk_blk = dkc if dk_blk is None else dk_blk + dkc
                dv_blk = dvc if dv_blk is None else dv_blk + dvc
                new_r.append(rr[hh] + jnp.sum(lk, axis=-1, keepdims=True))
                new_e.append(er[hh] + jnp.sum(e, axis=-1, keepdims=True))
            dk_ref[pl.ds(start, tk), :] += dk_blk
            dv_ref[pl.ds(start, tk), :] += dv_blk
            return tuple(new_r), tuple(new_e), dq + jnp.where(lo, dqs[0], dqs[1])

        zero = jnp.zeros((tq, 1), F32)
        carry = ((zero, zero), (zero, zero), jnp.zeros((tq, LANES), F32))
        for dg in reversed(range(nd)):
            carry = block(i * nd + dg, carry, dg)
        below = i * nd
        carry = lax.fori_loop(0, below, lambda jj, cr: block(below - 1 - jj, cr, None), carry)
        dq_ref[...] = carry[2]

    shp = jax.ShapeDtypeStruct((s, width // 3), F32)
    return pl.pallas_call(
        body,
        name=name,
        grid=(nsec, s // tq),
        in_specs=[qspec, kspec, vspec, qspec, qspec],
        out_specs=(qspec, accspec, accspec),
        out_shape=(shp, shp, shp),
        compiler_params=pltpu.CompilerParams(dimension_semantics=("parallel", "arbitrary"),
                                             vmem_limit_bytes=V7X_ATTN_BWD_VMEM_LIMIT_BYTES),
    )(qkv16, qkv16, qkv16, o, do)


def _adam_update(wv, gv, mv, vv):
    c1 = 1.0 - ADAM_B1 ** ADAM_STEP
    c2 = 1.0 - ADAM_B2 ** ADAM_STEP
    nm = ADAM_B1 * mv + (1.0 - ADAM_B1) * gv
    nv = ADAM_B2 * vv + (1.0 - ADAM_B2) * (gv * gv)
    delta = -ADAM_LR * ((nm / c1) / (jnp.sqrt(nv / c2) + ADAM_EPS) + ADAM_WD * wv)
    return delta, nm, nv


def _sum_slabs(r, *, name):
    n, rows, lanes = r.shape
    tr = _tile(rows, (1024, 512, 256, 128, 64, 32, 16))

    def body(r_ref, o_ref):
        acc = r_ref[0]
        for d in range(1, n):
            acc = acc + r_ref[d]
        o_ref[...] = acc

    return pl.pallas_call(
        body,
        name=name,
        grid=(rows // tr,),
        in_specs=[pl.BlockSpec((n, tr, lanes), lambda i: (0, i, 0))],
        out_specs=pl.BlockSpec((tr, lanes), lambda i: (i, 0)),
        out_shape=jax.ShapeDtypeStruct((rows, lanes), F32),
        compiler_params=_params(("parallel",)),
    )(r)


def _adamw(w, g, m, v, *, name):
    rows, cols = w.shape
    tr = _tile(rows, (1024, 512, 256, 128, 64, 32, 16))
    spec = pl.BlockSpec((tr, cols), lambda i: (i, 0))

    def body(w_ref, g_ref, m_ref, v_ref, d_ref, nm_ref, nv_ref):
        d_ref[...], nm_ref[...], nv_ref[...] = _adam_update(w_ref[...], g_ref[...], m_ref[...], v_ref[...])

    shp = jax.ShapeDtypeStruct((rows, cols), F32)
    return pl.pallas_call(
        body,
        name=name,
        grid=(rows // tr,),
        in_specs=[spec, spec, spec, spec],
        out_specs=(spec, spec, spec),
        out_shape=(shp, shp, shp),
        compiler_params=_params(("parallel",)),
    )(w, g, m, v)


def _adamw_sum(w, r, m, v, *, name):
    n, rows, cols = r.shape
    tr = _tile(rows, (256, 128, 64, 32, 16, 8))
    spec = pl.BlockSpec((tr, cols), lambda i: (i, 0))

    def body(w_ref, r_ref, m_ref, v_ref, g_ref, d_ref, nm_ref, nv_ref):
        gv = r_ref[0]
        for d in range(1, n):
            gv = gv + r_ref[d]
        g_ref[...] = gv
        d_ref[...], nm_ref[...], nv_ref[...] = _adam_update(w_ref[...], gv, m_ref[...], v_ref[...])

    shp = jax.ShapeDtypeStruct((rows, cols), F32)
    return pl.pallas_call(
        body,
        name=name,
        grid=(rows // tr,),
        in_specs=[spec, pl.BlockSpec((n, tr, cols), lambda i: (0, i, 0)), spec, spec],
        out_specs=(spec, spec, spec, spec),
        out_shape=(shp, shp, shp, shp),
        compiler_params=_params(("parallel",)),
    )(w, r, m, v)


_MATMUL_WEIGHTS = (("conv_w_in", 2), ("conv_w_out", 1), ("attn_w_qkv", 2), ("attn_w_o", 1), ("ffn_w_up", 2),
                   ("ffn_w_down", 1))
_TAP_WEIGHTS = (("conv_a_dw_w", 2), ("conv_b_dw_w", 2), ("ffn_dw_w", 2))
_REPLICATED = ("mix_norm_g", "ffn_norm_g", "conv_a_dw_b", "conv_a_ln_g", "conv_a_ln_b", "attn_q_g", "attn_k_g",
               "ffn_dw_b")
_WEIGHT_ORDER = ("mix_norm_g", "ffn_norm_g", "conv_w_in", "conv_a_dw_w", "conv_a_dw_b", "conv_a_ln_g",
                 "conv_a_ln_b", "conv_b_dw_w", "conv_w_out", "attn_w_qkv", "attn_q_g", "attn_k_g", "attn_w_o",
                 "ffn_w_up", "ffn_dw_w", "ffn_dw_b", "ffn_w_down")


def _assemble(gathered, axis):
    n, l, a, b = gathered.shape
    if axis == 2:
        return jnp.transpose(gathered, (1, 2, 0, 3)).reshape(l, a, n * b)
    return jnp.transpose(gathered, (1, 0, 2, 3)).reshape(l, n * a, b)


def _split_shards(full, axis):
    l, a, b = full.shape
    if axis == 2:
        return jnp.transpose(full.reshape(l, a, N_DEV, b // N_DEV), (2, 0, 1, 3))
    return jnp.transpose(full.reshape(l, N_DEV, a // N_DEV, b), (1, 0, 2, 3))


def kernel(x, mix_norm_g, ffn_norm_g, conv_w_in, conv_a_dw_w, conv_a_dw_b, conv_a_ln_g, conv_a_ln_b, conv_b_dw_w, conv_w_out, attn_w_qkv, attn_q_g, attn_k_g, attn_w_o, ffn_w_up, ffn_dw_w, ffn_dw_b, ffn_w_down, loss_target, m_mix_norm_g, m_ffn_norm_g, m_conv_w_in, m_conv_a_dw_w, m_conv_a_dw_b, m_conv_a_ln_g, m_conv_a_ln_b, m_conv_b_dw_w, m_conv_w_out, m_attn_w_qkv, m_attn_q_g, m_attn_k_g, m_attn_w_o, m_ffn_w_up, m_ffn_dw_w, m_ffn_dw_b, m_ffn_w_down, v_mix_norm_g, v_ffn_norm_g, v_conv_w_in, v_conv_a_dw_w, v_conv_a_dw_b, v_conv_a_ln_g, v_conv_a_ln_b, v_conv_b_dw_w, v_conv_w_out, v_attn_w_qkv, v_attn_q_g, v_attn_k_g, v_attn_w_o, v_ffn_w_up, v_ffn_dw_w, v_ffn_dw_b, v_ffn_w_down):
    args = dict(locals())
    weights = {n: args[n] for n in _WEIGHT_ORDER}
    mom_m = {n: args["m_" + n] for n in _WEIGHT_ORDER}
    mom_v = {n: args["v_" + n] for n in _WEIGHT_ORDER}

    _, s, d = x.shape
    depth = mix_norm_g.shape[0]
    dh = attn_q_g.shape[1]
    x0 = x.reshape(s, d)
    target = loss_target.reshape(s, d)

    gathered = _all_gather([weights[n].astype(BF16) for n, _ in _MATMUL_WEIGHTS], "gather_matmul_weights")
    full = {}
    for (n, axis), g in zip(_MATMUL_WEIGHTS, gathered):
        _, l, a, b = g.shape
        if axis == 2:
            full[n] = _assemble_cols(g.reshape(N_DEV, l * a, b), name=f"assemble_{n}").reshape(l, a, N_DEV * b)
        else:
            full[n] = jnp.transpose(g, (1, 0, 2, 3)).reshape(l, N_DEV * a, b)
    tap_shapes = [weights[n].shape for n, _ in _TAP_WEIGHTS]
    g_tap = _all_gather([_pack([weights[n] for n, _ in _TAP_WEIGHTS], F32)], "gather_tap_weights")[0]
    for (n, axis), part in zip(_TAP_WEIGHTS, _unpack(g_tap, tap_shapes, lead=(N_DEV,))):
        full[n] = _assemble(part, axis)
    f = full["ffn_w_down"].shape[1]

    saved = []
    xs = x0
    for layer in range(depth):
        i = layer // 2
        sv = {"x_in": xs}
        hn = _rmsnorm_fwd(xs, mix_norm_g[layer:layer + 1], name=f"mix_norm_fwd_{layer}")
        sv["h"] = hn
        if layer % 2 == 0:
            p = _matmul(hn, full["conv_w_in"][i], name=f"conv_in_{layer}")
            ab, a1 = _conv_act_fwd(p, full["conv_a_dw_w"][i], conv_a_dw_b[i:i + 1], conv_a_ln_g[i:i + 1],
                                   conv_a_ln_b[i:i + 1], full["conv_b_dw_w"][i], name=f"conv_act_fwd_{layer}")
            sv.update(p=p, a1=a1, mix=ab)
            xs = _matmul(ab, full["conv_w_out"][i], add=xs, name=f"conv_out_{layer}")
        else:
            reps = LANES // dh
            gains = jnp.concatenate([
                jnp.broadcast_to(jnp.tile(attn_q_g[i], reps) * dh ** -0.5, (8, LANES)),
                jnp.broadcast_to(jnp.tile(attn_k_g[i], reps), (8, LANES)),
                jnp.ones((8, LANES), F32)], axis=0)
            qkv = _matmul(hn, full["attn_w_qkv"][i], name=f"attn_qkv_{layer}")
            qkv16 = _qkv_prep_fwd(qkv, gains, dh, name=f"qkv_prep_fwd_{layer}")
            o = _attn_fwd(qkv16, dh, name=f"attn_fwd_{layer}")
            sv.update(qkv=qkv, qkv16=qkv16, gains=gains, mix=o)
            xs = _matmul(o, full["attn_w_o"][i], add=xs, name=f"attn_out_{layer}")
        sv["x_mid"] = xs
        h2 = _rmsnorm_fwd(xs, ffn_norm_g[layer:layer + 1], name=f"ffn_norm_fwd_{layer}")
        w_up = full["ffn_w_up"][layer]
        ug = _matmul(h2, w_up[:, :f], name=f"ffn_up_gate_{layer}")
        uv = _matmul(h2, w_up[:, f:], name=f"ffn_up_val_{layer}")
        taps = full["ffn_dw_w"][layer]
        bias = ffn_dw_b[layer:layer + 1]
        act = _ffn_act_fwd(ug, uv, taps[:, :f], taps[:, f:], bias[:, :f], bias[:, f:], name=f"ffn_act_fwd_{layer}")
        sv.update(h2=h2, ug=ug, uv=uv, act=act)
        xs = _matmul(act, full["ffn_w_down"][layer], add=xs, name=f"ffn_down_{layer}")
        saved.append(sv)

    sq, dx, dx16 = _loss_head(xs, target, name="loss_head")
    loss = lax.psum(0.5 * sq[0, 0] / d, ("x", "y", "c"))

    grads = {n: [None] * weights[n].shape[0] for n in _WEIGHT_ORDER}
    for layer in reversed(range(depth)):
        i = layer // 2
        sv = saved[layer]
        w_up = full["ffn_w_up"][layer]
        taps = full["ffn_dw_w"][layer]
        bias = ffn_dw_b[layer:layer + 1]
        dact = _matmul(dx16, full["ffn_w_down"][layer], tb=True, name=f"ffn_down_dx_{layer}")
        grads["ffn_w_down"][layer] = _matmul(sv["act"], dx16, ta=True, name=f"ffn_down_dw_{layer}")
        dug, duv, dwg, dwv, dbg, dbv = _ffn_act_bwd(sv["ug"], sv["uv"], dact, taps[:, :f], taps[:, f:], bias[:, :f],
                                                   bias[:, f:], name=f"ffn_act_bwd_{layer}")
        grads["ffn_dw_w"][layer] = jnp.concatenate([dwg[:3], dwv[:3]], axis=1)
        grads["ffn_dw_b"][layer] = jnp.concatenate([dbg[0], dbv[0]], axis=0)
        dh2 = _matmul(dug, w_up[:, :f], tb=True, name=f"ffn_up_gate_dx_{layer}")
        dh2 = _matmul(duv, w_up[:, f:], tb=True, add=dh2, name=f"ffn_up_val_dx_{layer}")
        grads["ffn_w_up"][layer] = jnp.concatenate(
            [_matmul(sv["h2"], dug, ta=True, name=f"ffn_up_gate_dw_{layer}"),
             _matmul(sv["h2"], duv, ta=True, name=f"ffn_up_val_dw_{layer}")], axis=1)
        dx, dx16, dg = _rmsnorm_bwd(sv["x_mid"], ffn_norm_g[layer:layer + 1], dh2, dx, name=f"ffn_norm_bwd_{layer}")
        grads["ffn_norm_g"][layer] = dg[0]

        if layer % 2 == 0:
            dab = _matmul(dx16, full["conv_w_out"][i], tb=True, name=f"conv_out_dx_{layer}")
            grads["conv_w_out"][i] = _matmul(sv["mix"], dx16, ta=True, name=f"conv_out_dw_{layer}")
            ka = full["conv_a_dw_w"].shape[1]
            kb = full["conv_b_dw_w"].shape[1]
            dp, dwa, dba, dlng, dlnb, dwb = _conv_act_bwd(
                sv["p"], sv["a1"], dab, full["conv_a_dw_w"][i], conv_a_ln_g[i:i + 1], conv_a_ln_b[i:i + 1],
                full["conv_b_dw_w"][i], name=f"conv_act_bwd_{layer}")
            grads["conv_a_dw_w"][i] = dwa[:ka]
            grads["conv_a_dw_b"][i] = dba[0]
            grads["conv_a_ln_g"][i] = dlng[0]
            grads["conv_a_ln_b"][i] = dlnb[0]
            grads["conv_b_dw_w"][i] = dwb[:kb]
            dhn = _matmul(dp, full["conv_w_in"][i], tb=True, name=f"conv_in_dx_{layer}")
            grads["conv_w_in"][i] = _matmul(sv["h"], dp, ta=True, name=f"conv_in_dw_{layer}")
        else:
            do = _matmul(dx16, full["attn_w_o"][i], tb=True, name=f"attn_out_dx_{layer}")
            grads["attn_w_o"][i] = _matmul(sv["mix"], dx16, ta=True, name=f"attn_out_dw_{layer}")
            dq, dk, dv = _attn_bwd(sv["qkv16"], sv["mix"], do, dh, name=f"attn_bwd_{layer}")
            dqkv, dgains = _qkv_prep_bwd(sv["qkv"], dq, dk, dv, sv["gains"], dh, name=f"qkv_prep_bwd_{layer}")
            grads["attn_q_g"][i] = dgains[0].reshape(-1, dh).sum(axis=0) * dh ** -0.5
            grads["attn_k_g"][i] = dgains[8].reshape(-1, dh).sum(axis=0)
            dhn = _matmul(dqkv, full["attn_w_qkv"][i], tb=True, name=f"attn_qkv_dx_{layer}")
            grads["attn_w_qkv"][i] = _matmul(sv["h"], dqkv, ta=True, name=f"attn_qkv_dw_{layer}")
        dx, dx16, dg = _rmsnorm_bwd(sv["x_in"], mix_norm_g[layer:layer + 1], dhn, dx, name=f"mix_norm_bwd_{layer}")
        grads["mix_norm_g"][layer] = dg[0]

    me = _my_index()
    sends = []
    for n, axis in _MATMUL_WEIGHTS:
        l, a, b = weights[n].shape
        if axis == 2:
            stacked = jnp.stack(grads[n], axis=0)
            sends.append(_split_cols(stacked.reshape(l * a, N_DEV * b), name=f"split_{n}").reshape(N_DEV, l, a, b))
        else:
            sends.append(jnp.stack([g.reshape(N_DEV, a, b) for g in grads[n]], axis=1))
    received = _exchange(sends, "exchange_matmul_grads")
    final_grads, delta, new_m, new_v = {}, {}, {}, {}
    for (n, _), r in zip(_MATMUL_WEIGHTS, received):
        l, a, b = weights[n].shape
        outs = _adamw_sum(weights[n].reshape(l * a, b), r.reshape(N_DEV, l * a, b), mom_m[n].reshape(l * a, b),
                          mom_v[n].reshape(l * a, b), name=f"adamw_{n}")
        final_grads[n], delta[n], new_m[n], new_v[n] = [o.reshape(l, a, b) for o in outs]

    small_names = list(_REPLICATED) + [n for n, _ in _TAP_WEIGHTS]
    local = {n: jnp.stack(grads[n], axis=0) for n in small_names}
    small_shapes = [local[n].shape for n in small_names]
    g_small = _sum_slabs(_all_gather([_pack([local[n] for n in small_names], F32)], "gather_small_grads")[0],
                         name="sum_small_grads")
    reduced = dict(zip(small_names, _unpack(g_small, small_shapes)))
    for n in _REPLICATED:
        final_grads[n] = reduced[n]
    for n, axis in _TAP_WEIGHTS:
        final_grads[n] = lax.dynamic_index_in_dim(_split_shards(reduced[n], axis), me, axis=0, keepdims=False)
    shapes = [weights[n].shape for n in small_names]
    d_s, m_s, v_s = _adamw(
        _pack([weights[n] for n in small_names], F32), _pack([final_grads[n] for n in small_names], F32),
        _pack([mom_m[n] for n in small_names], F32), _pack([mom_v[n] for n in small_names], F32), name="adamw_small")
    delta.update(zip(small_names, _unpack(d_s, shapes)))
    new_m.update(zip(small_names, _unpack(m_s, shapes)))
    new_v.update(zip(small_names, _unpack(v_s, shapes)))

    order = list(_WEIGHT_ORDER)
    return (loss, dx.reshape(x.shape), *[final_grads[n] for n in order], *[delta[n] for n in order],
            *[new_m[n] for n in order], *[new_v[n] for n in order])
```

```python
import jax
import jax.numpy as jnp
from jax import lax
from jax.experimental import pallas as pl
from jax.experimental.pallas import tpu as pltpu

F32 = jnp.float32
BF16 = jnp.bfloat16
EPS = 1e-6
N_DEV = 8
MESH_ID = pl.DeviceIdType.MESH

ADAM_LR = 0.001
ADAM_B1 = 0.9
ADAM_B2 = 0.999
ADAM_EPS = 1e-08
ADAM_WD = 0.01
ADAM_STEP = 10

V7X_VMEM_LIMIT_BYTES = 48 * 1024 * 1024
PACK_QUANTUM = 2048
A_HALO = 32
S_HALO = 8
LANES = 128
ATTN_Q_BLOCK = 256
ATTN_K_BLOCK = 256
V7X_ATTN_BWD_VMEM_LIMIT_BYTES = 56 * 1024 * 1024
MATMUL_MAX_SINGLE_K = 3072
ATTN_DEAD_LOG_WEIGHT = -110.0


def _tile(n, prefs):
    for p in prefs:
        if n % p == 0:
            return p
    return n


def _params(sem=None):
    return pltpu.CompilerParams(dimension_semantics=sem, vmem_limit_bytes=V7X_VMEM_LIMIT_BYTES)


def _sigmoid(x):
    return 1.0 / (1.0 + jnp.exp(-x))


def _dot(a, b, ca, cb):
    return lax.dot_general(a, b, (((ca,), (cb,)), ((), ())), preferred_element_type=F32)


def _my_index():
    return 4 * lax.axis_index("x") + 2 * lax.axis_index("y") + lax.axis_index("c")


def _all_gather(shards, name):
    n = len(shards)

    def body(*refs):
        x_refs, out_refs = refs[:n], refs[n:2 * n]
        send_sems, recv_sems, local_sems = refs[2 * n:]
        x, y, c = lax.axis_index("x"), lax.axis_index("y"), lax.axis_index("c")
        me, sibling = (x, y, c), (x, y, 1 - c)
        chips = [(1 - x, y), (x, 1 - y), (1 - x, 1 - y)]

        def copy(a, k, block, to, src=None):
            slot = out_refs[a].at[4 * block[0] + 2 * block[1] + block[2]]
            return pltpu.make_async_remote_copy(
                src_ref=slot if src is None else src, dst_ref=slot,
                send_sem=send_sems.at[7 * a + k], recv_sem=recv_sems.at[7 * a + k],
                device_id=to, device_id_type=MESH_ID)

        mine = [pltpu.make_async_copy(x_refs[a], out_refs[a].at[4 * x + 2 * y + c], local_sems.at[a])
                for a in range(n)]
        for cp in mine:
            cp.start()
        first = []
        for a in range(n):
            first.append(copy(a, 0, me, sibling, src=x_refs[a]))
            first += [copy(a, 1 + j, me, (*chip, c), src=x_refs[a]) for j, chip in enumerate(chips)]
        for cp in first:
            cp.start()
        passed = []
        for j, chip in enumerate(chips):
            for a in range(n):
                copy(a, 1 + j, (*chip, c), me).wait_recv()
                fwd = copy(a, 4 + j, (*chip, c), sibling)
                fwd.start()
                passed.append(fwd)
        for a in range(n):
            copy(a, 0, sibling, me).wait_recv()
            for j, chip in enumerate(chips):
                copy(a, 4 + j, (*chip, 1 - c), me).wait_recv()
        for cp in first + passed:
            cp.wait_send()
        for cp in mine:
            cp.wait()

    hbm = pl.BlockSpec(memory_space=pl.ANY)
    return pl.pallas_call(
        body,
        name=name,
        out_shape=[jax.ShapeDtypeStruct((N_DEV,) + sh.shape, sh.dtype) for sh in shards],
        in_specs=[hbm] * n,
        out_specs=[hbm] * n,
        scratch_shapes=[
            pltpu.SemaphoreType.DMA((7 * n,)),
            pltpu.SemaphoreType.DMA((7 * n,)),
            pltpu.SemaphoreType.DMA((n,)),
        ],
    )(*shards)


def _exchange(slabs, name):
    n = len(slabs)

    def body(*refs):
        g_refs, out_refs = refs[:n], refs[n:2 * n]
        send_sems, recv_sems, local_sems = refs[2 * n:]
        x, y, c = lax.axis_index("x"), lax.axis_index("y"), lax.axis_index("c")
        me = 4 * x + 2 * y + c
        local = [pltpu.make_async_copy(g_refs[a].at[me], out_refs[a].at[me], local_sems.at[a]) for a in range(n)]
        for cp in local:
            cp.start()
        copies = []
        for k in range(1, N_DEV):
            px = (x + ((k >> 2) & 1)) % 2
            py = (y + ((k >> 1) & 1)) % 2
            pc = (c + (k & 1)) % 2
            for a in range(n):
                cp = pltpu.make_async_remote_copy(
                    src_ref=g_refs[a].at[4 * px + 2 * py + pc], dst_ref=out_refs[a].at[me],
                    send_sem=send_sems.at[7 * a + k - 1], recv_sem=recv_sems.at[7 * a + k - 1],
                    device_id=(px, py, pc), device_id_type=MESH_ID)
                cp.start()
                copies.append(cp)
        for cp in copies:
            cp.wait_recv()
        for cp in copies:
            cp.wait_send()
        for cp in local:
            cp.wait()

    hbm = pl.BlockSpec(memory_space=pl.ANY)
    return pl.pallas_call(
        body,
        name=name,
        out_shape=[jax.ShapeDtypeStruct(g.shape, g.dtype) for g in slabs],
        in_specs=[hbm] * n,
        out_specs=[hbm] * n,
        scratch_shapes=[
            pltpu.SemaphoreType.DMA((7 * n,)),
            pltpu.SemaphoreType.DMA((7 * n,)),
            pltpu.SemaphoreType.DMA((n,)),
        ],
    )(*slabs)


def _assemble_cols(g, *, name):
    n, r, w = g.shape
    tr = _tile(r, (256, 128, 64, 32, 16))

    def body(g_ref, o_ref):
        for j in range(n):
            o_ref[:, j * w:(j + 1) * w] = g_ref[j]

    return pl.pallas_call(
        body,
        name=name,
        grid=(r // tr,),
        in_specs=[pl.BlockSpec((n, tr, w), lambda i: (0, i, 0))],
        out_specs=pl.BlockSpec((tr, n * w), lambda i: (i, 0)),
        out_shape=jax.ShapeDtypeStruct((r, n * w), g.dtype),
        compiler_params=_params(("parallel",)),
    )(g)


def _split_cols(x, *, name):
    r, nw = x.shape
    w = nw // N_DEV
    tr = _tile(r, (256, 128, 64, 32, 16))

    def body(x_ref, o_ref):
        for j in range(N_DEV):
            o_ref[j] = x_ref[:, j * w:(j + 1) * w]

    return pl.pallas_call(
        body,
        name=name,
        grid=(r // tr,),
        in_specs=[pl.BlockSpec((tr, nw), lambda i: (i, 0))],
        out_specs=pl.BlockSpec((N_DEV, tr, w), lambda i: (0, i, 0)),
        out_shape=jax.ShapeDtypeStruct((N_DEV, r, w), x.dtype),
        compiler_params=_params(("parallel",)),
    )(x)


def _padded(n):
    return -(-n // PACK_QUANTUM) * PACK_QUANTUM


def _pack(parts, dtype, lead=()):
    flat = []
    for p in parts:
        p = p.astype(dtype).reshape(lead + (-1,))
        n = p.shape[-1]
        flat.append(jnp.pad(p, [(0, 0)] * len(lead) + [(0, _padded(n) - n)]))
    return jnp.concatenate(flat, axis=-1).reshape(lead + (-1, 128))


def _unpack(buf, shapes, lead=()):
    flat = buf.reshape(lead + (-1,))
    out, off = [], 0
    for shp in shapes:
        n = 1
        for d in shp:
            n *= d
        out.append(flat[..., off:off + n].reshape(lead + tuple(shp)))
        off += _padded(n)
    return out


def _half_if_tiled(n):
    half = n // 2
    return (half,) if n % 2 == 0 and half % LANES == 0 and half <= 1536 else ()


def _matmul(a, b, *, ta=False, tb=False, out_dtype=F32, add=None, name):
    if ta:
        kdim, m = a.shape
    else:
        m, kdim = a.shape
    n = b.shape[0] if tb else b.shape[1]
    bm = _tile(m, (1024,) + _half_if_tiled(m) + (512, 256, 128))
    bn = _tile(n, (512,) + _half_if_tiled(n) + (256, 128))
    bk = kdim if kdim <= MATMUL_MAX_SINGLE_K else _tile(kdim, (1024, 512, 256, 128))
    nk = kdim // bk

    a_spec = pl.BlockSpec((bk, bm), lambda i, j, k: (k, i)) if ta else pl.BlockSpec((bm, bk), lambda i, j, k: (i, k))
    b_spec = pl.BlockSpec((bn, bk), lambda i, j, k: (j, k)) if tb else pl.BlockSpec((bk, bn), lambda i, j, k: (k, j))
    o_spec = pl.BlockSpec((bm, bn), lambda i, j, k: (i, j))
    in_specs = [a_spec, b_spec] + ([o_spec] if add is not None else [])
    operands = [a, b] + ([add] if add is not None else [])

    def product(a_ref, b_ref):
        return _dot(a_ref[...].astype(BF16), b_ref[...].astype(BF16), 0 if ta else 1, 1 if tb else 0)

    def finish(r, refs):
        if add is not None:
            r = r + refs[2][...]
        return r

    def body_single(*refs):
        o_ref = refs[-1]
        o_ref[...] = finish(product(refs[0], refs[1]), refs).astype(o_ref.dtype)

    def body_multi(*refs):
        o_ref, acc_ref = refs[-2], refs[-1]
        k = pl.program_id(2)

        @pl.when(k == 0)
        def _():
            acc_ref[...] = jnp.zeros_like(acc_ref)

        acc_ref[...] += product(refs[0], refs[1])

        @pl.when(k == nk - 1)
        def _():
            o_ref[...] = finish(acc_ref[...], refs).astype(o_ref.dtype)

    return pl.pallas_call(
        body_single if nk == 1 else body_multi,
        name=name,
        grid=(m // bm, n // bn, nk),
        in_specs=in_specs,
        out_specs=o_spec,
        out_shape=jax.ShapeDtypeStruct((m, n), out_dtype),
        scratch_shapes=[] if nk == 1 else [pltpu.VMEM((bm, bn), F32)],
        compiler_params=_params(("parallel", "parallel", "arbitrary")),
    )(*operands)


def _weight_grad(x, dy, *, name):
    return _matmul(x, dy, ta=True, out_dtype=BF16, name=name)


def _rmsnorm_fwd(x, g, *, name):
    r, d = x.shape
    tr = _tile(r, (1024, 512, 256, 128))

    def body(x_ref, g_ref, o_ref):
        xv = x_ref[...]
        y = xv * lax.rsqrt(jnp.mean(xv * xv, axis=-1, keepdims=True) + EPS)
        y = y * g_ref[...]
        o_ref[...] = y.astype(o_ref.dtype)

    return pl.pallas_call(
        body,
        name=name,
        grid=(r // tr,),
        in_specs=[pl.BlockSpec((tr, d), lambda i: (i, 0)), pl.BlockSpec((1, d), lambda i: (0, 0))],
        out_specs=pl.BlockSpec((tr, d), lambda i: (i, 0)),
        out_shape=jax.ShapeDtypeStruct((r, d), BF16),
        compiler_params=_params(("parallel",)),
    )(x, g)


def _rmsnorm_bwd(x, g, dy, res, *, name):
    r, d = x.shape
    tr = _tile(r, (512, 256, 128))
    row = pl.BlockSpec((tr, d), lambda i: (i, 0))
    vec = pl.BlockSpec((1, d), lambda i: (0, 0))
    operands = [x, g, dy] + ([res] if res is not None else [])

    def body(*refs):
        if res is not None:
            x_ref, g_ref, dy_ref, res_ref, dx_ref, dx16_ref, dg_ref = refs
        else:
            x_ref, g_ref, dy_ref, dx_ref, dx16_ref, dg_ref = refs
        xv = x_ref[...]
        dyv = dy_ref[...].astype(F32)
        rs = lax.rsqrt(jnp.mean(xv * xv, axis=-1, keepdims=True) + EPS)
        xn = xv * rs
        dyg = dyv * g_ref[...]
        dx = rs * (dyg - xn * jnp.mean(dyg * xn, axis=-1, keepdims=True))
        if res is not None:
            dx = dx + res_ref[...]
        dx_ref[...] = dx
        dx16_ref[...] = dx.astype(dx16_ref.dtype)

        @pl.when(pl.program_id(0) == 0)
        def _():
            dg_ref[...] = jnp.zeros_like(dg_ref)

        dg_ref[...] += jnp.sum(dyv * xn, axis=0, keepdims=True)

    return pl.pallas_call(
        body,
        name=name,
        grid=(r // tr,),
        in_specs=[row, vec, row] + ([row] if res is not None else []),
        out_specs=(row, row, vec),
        out_shape=(jax.ShapeDtypeStruct((r, d), F32), jax.ShapeDtypeStruct((r, d), BF16),
                   jax.ShapeDtypeStruct((1, d), F32)),
        compiler_params=_params(("arbitrary",)),
    )(*operands)


def _loss_head(y, target, *, name):
    s, d = y.shape
    ts = _tile(s, (512, 256, 128))
    row = pl.BlockSpec((ts, d), lambda i: (i, 0))
    acc = pl.BlockSpec((8, 128), lambda i: (0, 0))

    def body(y_ref, t_ref, l_ref, dy_ref, dy16_ref):
        e = y_ref[...] - t_ref[...]
        dy = e * (1.0 / d)
        dy_ref[...] = dy
        dy16_ref[...] = dy.astype(dy16_ref.dtype)

        @pl.when(pl.program_id(0) == 0)
        def _():
            l_ref[...] = jnp.zeros_like(l_ref)

        sq = jnp.sum(jnp.sum(e * e, axis=-1, keepdims=True), axis=0, keepdims=True)
        l_ref[...] += jnp.broadcast_to(sq, l_ref.shape)

    return pl.pallas_call(
        body,
        name=name,
        grid=(s // ts,),
        in_specs=[row, row],
        out_specs=(acc, row, row),
        out_shape=(jax.ShapeDtypeStruct((8, 128), F32), jax.ShapeDtypeStruct((s, d), F32),
                   jax.ShapeDtypeStruct((s, d), BF16)),
        compiler_params=_params(("arbitrary",)),
    )(y, target)


def _ffn_tiles(s, f):
    return _tile(s, (512, 256, 128)), _tile(f, (512, 256, 128))


def _ffn_act_fwd(ug, uv, wg, wv, bg, bv, *, name):
    s, f = ug.shape
    ts, tc = _ffn_tiles(s, f)
    hb = ts // S_HALO
    main = pl.BlockSpec((ts, tc), lambda j, i: (i, j))
    prev = pl.BlockSpec((S_HALO, tc), lambda j, i: (jnp.maximum(i * hb - 1, 0), j))
    taps = pl.BlockSpec((3, tc), lambda j, i: (0, j))
    bias = pl.BlockSpec((1, tc), lambda j, i: (0, j))

    def body(ug_ref, ugp_ref, uv_ref, uvp_ref, wg_ref, wv_ref, bg_ref, bv_ref, o_ref, gbuf, vbuf):
        first = pl.program_id(1) == 0

        def conv(m_ref, p_ref, w_ref, b_ref, buf):
            buf[0:S_HALO, :] = jnp.where(first, 0.0, p_ref[...])
            buf[S_HALO:, :] = m_ref[...]
            w = w_ref[...]
            acc = b_ref[...] + buf[S_HALO - 2:S_HALO - 2 + ts, :] * w[0:1, :]
            acc = acc + buf[S_HALO - 1:S_HALO - 1 + ts, :] * w[1:2, :]
            return acc + buf[S_HALO:, :] * w[2:3, :]

        gate = conv(ug_ref, ugp_ref, wg_ref, bg_ref, gbuf)
        val = conv(uv_ref, uvp_ref, wv_ref, bv_ref, vbuf)
        o_ref[...] = (gate * _sigmoid(gate) * val).astype(o_ref.dtype)

    return pl.pallas_call(
        body,
        name=name,
        grid=(f // tc, s // ts),
        in_specs=[main, prev, main, prev, taps, taps, bias, bias],
        out_specs=main,
        out_shape=jax.ShapeDtypeStruct((s, f), BF16),
        scratch_shapes=[pltpu.VMEM((S_HALO + ts, tc), F32), pltpu.VMEM((S_HALO + ts, tc), F32)],
        compiler_params=_params(("parallel", "arbitrary")),
    )(ug, ug, uv, uv, wg, wv, bg, bv)


def _ffn_act_bwd(ug, uv, df, wg, wv, bg, bv, *, name):
    s, f = ug.shape
    ts, tc = _ffn_tiles(s, f)
    hb = ts // S_HALO
    last_hb = s // S_HALO - 1
    n_row = s // ts
    main = pl.BlockSpec((ts, tc), lambda j, i: (i, j))
    prev = pl.BlockSpec((S_HALO, tc), lambda j, i: (jnp.maximum(i * hb - 1, 0), j))
    nxt = pl.BlockSpec((S_HALO, tc), lambda j, i: (jnp.minimum((i + 1) * hb, last_hb), j))
    taps = pl.BlockSpec((3, tc), lambda j, i: (0, j))
    bias = pl.BlockSpec((1, tc), lambda j, i: (0, j))
    tacc = pl.BlockSpec((8, tc), lambda j, i: (0, j))
    ext = ts + S_HALO

    def body(ug_ref, ugp_ref, ugn_ref, uv_ref, uvp_ref, uvn_ref, df_ref, dfn_ref, wg_ref, wv_ref, bg_ref, bv_ref,
             dug_ref, duv_ref, dwg_ref, dwv_ref, dbg_ref, dbv_ref, gbuf, vbuf, dgbuf, dvbuf):
        i = pl.program_id(1)
        first = i == 0
        last = i == n_row - 1

        @pl.when(first)
        def _():
            dwg_ref[...] = jnp.zeros_like(dwg_ref)
            dwv_ref[...] = jnp.zeros_like(dwv_ref)
            dbg_ref[...] = jnp.zeros_like(dbg_ref)
            dbv_ref[...] = jnp.zeros_like(dbv_ref)

        def fill(buf, p_ref, m_ref, n_ref):
            buf[0:S_HALO, :] = jnp.where(first, 0.0, p_ref[...])
            buf[S_HALO:S_HALO + ts, :] = m_ref[...]
            buf[S_HALO + ts:, :] = n_ref[...]

        def conv_ext(buf, w_ref, b_ref):
            w = w_ref[...]
            acc = b_ref[...] + buf[S_HALO - 2:S_HALO - 2 + ext, :] * w[0:1, :]
            acc = acc + buf[S_HALO - 1:S_HALO - 1 + ext, :] * w[1:2, :]
            return acc + buf[S_HALO:S_HALO + ext, :] * w[2:3, :]

        fill(gbuf, ugp_ref, ug_ref, ugn_ref)
        fill(vbuf, uvp_ref, uv_ref, uvn_ref)
        gate = conv_ext(gbuf, wg_ref, bg_ref)
        val = conv_ext(vbuf, wv_ref, bv_ref)
        dgbuf[0:ts, :] = df_ref[...]
        dgbuf[ts:, :] = jnp.where(last, 0.0, dfn_ref[...])
        dfe = dgbuf[...]
        sg = _sigmoid(gate)
        dgate = dfe * val * (sg * (1.0 + gate * (1.0 - sg)))
        dval = dfe * (gate * sg)
        dgbuf[...] = dgate
        dvbuf[...] = dval

        def back(dbuf, ubuf, w_ref, du_ref, dw_ref, db_ref):
            w = w_ref[...]
            du = dbuf[2:2 + ts, :] * w[0:1, :] + dbuf[1:1 + ts, :] * w[1:2, :] + dbuf[0:ts, :] * w[2:3, :]
            du_ref[...] = du.astype(du_ref.dtype)
            dm = dbuf[0:ts, :]
            db_ref[...] += jnp.sum(dm, axis=0, keepdims=True)
            for k in range(3):
                dw_ref[k:k + 1, :] += jnp.sum(dm * ubuf[S_HALO - 2 + k:S_HALO - 2 + k + ts, :], axis=0, keepdims=True)

        back(dgbuf, gbuf, wg_ref, dug_ref, dwg_ref, dbg_ref)
        back(dvbuf, vbuf, wv_ref, duv_ref, dwv_ref, dbv_ref)

    return pl.pallas_call(
        body,
        name=name,
        grid=(f // tc, n_row),
        in_specs=[main, prev, nxt, main, prev, nxt, main, nxt, taps, taps, bias, bias],
        out_specs=(main, main, tacc, tacc, bias, bias),
        out_shape=(
            jax.ShapeDtypeStruct((s, f), BF16), jax.ShapeDtypeStruct((s, f), BF16),
            jax.ShapeDtypeStruct((8, f), F32), jax.ShapeDtypeStruct((8, f), F32),
            jax.ShapeDtypeStruct((1, f), F32), jax.ShapeDtypeStruct((1, f), F32),
        ),
        scratch_shapes=[
            pltpu.VMEM((S_HALO + ext, tc), F32), pltpu.VMEM((S_HALO + ext, tc), F32),
            pltpu.VMEM((ext, tc), F32), pltpu.VMEM((ext, tc), F32),
        ],
        compiler_params=_params(("parallel", "arbitrary")),
    )(ug, ug, ug, uv, uv, uv, df, df, wg, wv, bg, bv)


def _layernorm_stats(a1):
    mu = jnp.mean(a1, axis=-1, keepdims=True)
    xc = a1 - mu
    rstd = lax.rsqrt(jnp.mean(xc * xc, axis=-1, keepdims=True) + EPS)
    return xc * rstd, rstd


def _conv_act_fwd(p, wa, ba, lng, lnb, wb, *, name):
    s, c5 = p.shape
    c = c5 // 5
    ka, kb = wa.shape[0], wb.shape[0]
    ts = _tile(s, (256, 128))
    hb = ts // A_HALO
    main = pl.BlockSpec((ts, c5), lambda i: (i, 0))
    prev = pl.BlockSpec((A_HALO, c5), lambda i: (jnp.maximum(i * hb - 1, 0), 0))

    def full(arr):
        return pl.BlockSpec(arr.shape, lambda i: (0, 0))

    def body(p_ref, pp_ref, wa_ref, ba_ref, lng_ref, lnb_ref, wb_ref, ab_ref, a1_ref, gbuf, cbuf):
        first = pl.program_id(0) == 0
        gbuf[0:A_HALO, :] = jnp.where(first, 0.0, pp_ref[:, 0:c] * _sigmoid(pp_ref[:, c:2 * c]))
        gbuf[A_HALO:, :] = p_ref[:, 0:c] * _sigmoid(p_ref[:, c:2 * c])
        cbuf[0:A_HALO, :] = jnp.where(first, 0.0, pp_ref[:, 3 * c:4 * c] * pp_ref[:, 4 * c:5 * c])
        cbuf[A_HALO:, :] = p_ref[:, 3 * c:4 * c] * p_ref[:, 4 * c:5 * c]

        a1 = jnp.broadcast_to(ba_ref[...], (ts, c))
        for k in range(ka):
            off = A_HALO - (ka - 1) + k
            a1 = a1 + gbuf[off:off + ts, :] * wa_ref[k:k + 1, :]
        a1_ref[...] = a1
        xhat, _ = _layernorm_stats(a1)
        a2 = xhat * lng_ref[...] + lnb_ref[...]
        ab_ref[:, 0:c] = (a2 * _sigmoid(a2)).astype(ab_ref.dtype)

        cv = jnp.zeros((ts, c), F32)
        for k in range(kb):
            off = A_HALO - (kb - 1) + k
            cv = cv + cbuf[off:off + ts, :] * wb_ref[k:k + 1, :]
        ab_ref[:, c:2 * c] = (p_ref[:, 2 * c:3 * c] * cv).astype(ab_ref.dtype)

    return pl.pallas_call(
        body,
        name=name,
        grid=(s // ts,),
        in_specs=[main, prev, full(wa), full(ba), full(lng), full(lnb), full(wb)],
        out_specs=(pl.BlockSpec((ts, 2 * c), lambda i: (i, 0)), pl.BlockSpec((ts, c), lambda i: (i, 0))),
        out_shape=(jax.ShapeDtypeStruct((s, 2 * c), BF16), jax.ShapeDtypeStruct((s, c), F32)),
        scratch_shapes=[pltpu.VMEM((A_HALO + ts, c), F32), pltpu.VMEM((A_HALO + ts, c), F32)],
        compiler_params=_params(("parallel",)),
    )(p, p, wa, ba, lng, lnb, wb)


def _conv_act_bwd(p, a1, dab, wa, lng, lnb, wb, *, name):
    s, c5 = p.shape
    c = c5 // 5
    ka, kb = wa.shape[0], wb.shape[0]
    ts = _tile(s, (256, 128))
    hb = ts // A_HALO
    n_row = s // ts
    last_hb = s // A_HALO - 1
    ext = ts + A_HALO

    def rows(width):
        return (pl.BlockSpec((ts, width), lambda i: (i, 0)),
                pl.BlockSpec((A_HALO, width), lambda i: (jnp.maximum(i * hb - 1, 0), 0)),
                pl.BlockSpec((A_HALO, width), lambda i: (jnp.minimum((i + 1) * hb, last_hb), 0)))

    p_main, p_prev, p_next = rows(c5)
    d_main, _, d_next = rows(2 * c)
    a_main, _, a_next = rows(c)

    def full(shape):
        return pl.BlockSpec(shape, lambda i: (0, 0))

    def body(p_ref, pp_ref, pn_ref, a1_ref, a1n_ref, d_ref, dn_ref, wa_ref, lng_ref, lnb_ref, wb_ref,
             dp_ref, dwa_ref, dba_ref, dlng_ref, dlnb_ref, dwb_ref, gbuf, cbuf, da1buf, dcvbuf):
        i = pl.program_id(0)
        first = i == 0
        last = i == n_row - 1

        @pl.when(first)
        def _():
            dwa_ref[...] = jnp.zeros_like(dwa_ref)
            dba_ref[...] = jnp.zeros_like(dba_ref)
            dlng_ref[...] = jnp.zeros_like(dlng_ref)
            dlnb_ref[...] = jnp.zeros_like(dlnb_ref)
            dwb_ref[...] = jnp.zeros_like(dwb_ref)

        sig_gate = _sigmoid(p_ref[:, c:2 * c])
        gbuf[0:A_HALO, :] = jnp.where(first, 0.0, pp_ref[:, 0:c] * _sigmoid(pp_ref[:, c:2 * c]))
        gbuf[A_HALO:, :] = p_ref[:, 0:c] * sig_gate
        cbuf[0:A_HALO, :] = jnp.where(first, 0.0, pp_ref[:, 3 * c:4 * c] * pp_ref[:, 4 * c:5 * c])
        cbuf[A_HALO:, :] = p_ref[:, 3 * c:4 * c] * p_ref[:, 4 * c:5 * c]

        def ln_back(a1v, dav):
            xhat, rstd = _layernorm_stats(a1v)
            a2 = xhat * lng_ref[...] + lnb_ref[...]
            sg = _sigmoid(a2)
            da2 = dav * (sg * (1.0 + a2 * (1.0 - sg)))
            dxh = da2 * lng_ref[...]
            da1 = rstd * (dxh - jnp.mean(dxh, axis=-1, keepdims=True)
                          - xhat * jnp.mean(dxh * xhat, axis=-1, keepdims=True))
            return da1, da2, xhat

        da1_m, da2_m, xhat_m = ln_back(a1_ref[...], d_ref[:, 0:c])
        da1_n, _, _ = ln_back(a1n_ref[...], dn_ref[:, 0:c])
        da1buf[0:ts, :] = da1_m
        da1buf[ts:, :] = jnp.where(last, 0.0, da1_n)
        dlng_ref[...] += jnp.sum(da2_m * xhat_m, axis=0, keepdims=True)
        dlnb_ref[...] += jnp.sum(da2_m, axis=0, keepdims=True)
        dba_ref[...] += jnp.sum(da1_m, axis=0, keepdims=True)

        dglu = jnp.zeros((ts, c), F32)
        for k in range(ka):
            off = (ka - 1) - k
            dglu = dglu + da1buf[off:off + ts, :] * wa_ref[k:k + 1, :]
            goff = A_HALO - (ka - 1) + k
            dwa_ref[k:k + 1, :] += jnp.sum(da1_m * gbuf[goff:goff + ts, :], axis=0, keepdims=True)
        dp_ref[:, 0:c] = (dglu * sig_gate).astype(dp_ref.dtype)
        dp_ref[:, c:2 * c] = (dglu * p_ref[:, 0:c] * sig_gate * (1.0 - sig_gate)).astype(dp_ref.dtype)

        cv = jnp.zeros((ts, c), F32)
        for k in range(kb):
            off = A_HALO - (kb - 1) + k
            cv = cv + cbuf[off:off + ts, :] * wb_ref[k:k + 1, :]
        db_m = d_ref[:, c:2 * c]
        dp_ref[:, 2 * c:3 * c] = (db_m * cv).astype(dp_ref.dtype)
        dcv_m = db_m * p_ref[:, 2 * c:3 * c]
        dcvbuf[0:ts, :] = dcv_m
        dcvbuf[ts:, :] = jnp.where(last, 0.0, dn_ref[:, c:2 * c] * pn_ref[:, 2 * c:3 * c])
        dbc = jnp.zeros((ts, c), F32)
        for k in range(kb):
            off = (kb - 1) - k
            dbc = dbc + dcvbuf[off:off + ts, :] * wb_ref[k:k + 1, :]
            coff = A_HALO - (kb - 1) + k
            dwb_ref[k:k + 1, :] += jnp.sum(dcv_m * cbuf[coff:coff + ts, :], axis=0, keepdims=True)
        dp_ref[:, 3 * c:4 * c] = (dbc * p_ref[:, 4 * c:5 * c]).astype(dp_ref.dtype)
        dp_ref[:, 4 * c:5 * c] = (dbc * p_ref[:, 3 * c:4 * c]).astype(dp_ref.dtype)

    return pl.pallas_call(
        body,
        name=name,
        grid=(n_row,),
        in_specs=[p_main, p_prev, p_next, a_main, a_next, d_main, d_next,
                  full(wa.shape), full(lng.shape), full(lnb.shape), full(wb.shape)],
        out_specs=(pl.BlockSpec((ts, c5), lambda i: (i, 0)), full((32, c)), full((1, c)), full((1, c)),
                   full((1, c)), full((8, c))),
        out_shape=(
            jax.ShapeDtypeStruct((s, c5), BF16), jax.ShapeDtypeStruct((32, c), F32),
            jax.ShapeDtypeStruct((1, c), F32), jax.ShapeDtypeStruct((1, c), F32),
            jax.ShapeDtypeStruct((1, c), F32), jax.ShapeDtypeStruct((8, c), F32),
        ),
        scratch_shapes=[
            pltpu.VMEM((A_HALO + ts, c), F32), pltpu.VMEM((A_HALO + ts, c), F32),
            pltpu.VMEM((ext, c), F32), pltpu.VMEM((ext, c), F32),
        ],
        compiler_params=_params(("arbitrary",)),
    )(p, p, p, a1, a1, dab, dab, wa, lng, lnb, wb)


def _head_pair_geometry(width, dh):
    assert 2 * dh == LANES, "the attention kernels pair two heads in one 128-lane block"
    return width // (3 * LANES)


def _group_sum(v, lo):
    s_lo = jnp.sum(jnp.where(lo, v, 0.0), axis=-1, keepdims=True)
    s_hi = jnp.sum(jnp.where(lo, 0.0, v), axis=-1, keepdims=True)
    return jnp.where(lo, s_lo, s_hi)


def _qkv_prep_fwd(qkv, gains, dh, *, name):
    s, width = qkv.shape
    nsec = _head_pair_geometry(width, dh)
    tr = _tile(s, (1024, 512, 256, 128))
    blk = pl.BlockSpec((tr, LANES), lambda i, cb: (i, cb))

    def body(x_ref, g_ref, o_ref):
        is_v = pl.program_id(1) >= 2 * nsec
        lo = lax.broadcasted_iota(jnp.int32, (1, LANES), 1) < dh
        xv = x_ref[...]
        rs = lax.rsqrt(_group_sum(xv * xv, lo) * (1.0 / dh) + EPS)
        y = xv * rs * g_ref[0:1, :]
        o_ref[...] = jnp.where(is_v, xv, y).astype(o_ref.dtype)

    return pl.pallas_call(
        body,
        name=name,
        grid=(s // tr, 3 * nsec),
        in_specs=[blk, pl.BlockSpec((8, LANES), lambda i, cb: (cb // nsec, 0))],
        out_specs=blk,
        out_shape=jax.ShapeDtypeStruct((s, width), BF16),
        compiler_params=_params(("parallel", "parallel")),
    )(qkv, gains)


def _qkv_prep_bwd(qkv, dq, dk, dv, gains, dh, *, name):
    s, width = qkv.shape
    nsec = _head_pair_geometry(width, dh)
    tr = _tile(s, (1024, 512, 256, 128))
    blk = pl.BlockSpec((tr, LANES), lambda cb, i: (i, cb))
    gspec = pl.BlockSpec((8, LANES), lambda cb, i: (cb // nsec, 0))

    def sec_spec(sec):
        return pl.BlockSpec((tr, LANES), lambda cb, i: (i, jnp.clip(cb - sec * nsec, 0, nsec - 1)))

    def body(x_ref, dq_ref, dk_ref, dv_ref, g_ref, o_ref, dg_ref):
        cb = pl.program_id(0)
        sec = cb // nsec

        @pl.when((cb % nsec == 0) & (pl.program_id(1) == 0))
        def _():
            dg_ref[...] = jnp.zeros_like(dg_ref)

        lo = lax.broadcasted_iota(jnp.int32, (1, LANES), 1) < dh
        d = jnp.where(sec == 0, dq_ref[...], jnp.where(sec == 1, dk_ref[...], dv_ref[...]))
        xv = x_ref[...]
        rs = lax.rsqrt(_group_sum(xv * xv, lo) * (1.0 / dh) + EPS)
        xn = xv * rs
        dyg = d * g_ref[0:1, :]
        dx = rs * (dyg - xn * (_group_sum(dyg * xn, lo) * (1.0 / dh)))
        o_ref[...] = jnp.where(sec == 2, d, dx).astype(o_ref.dtype)
        dg_ref[...] += jnp.broadcast_to(jnp.sum(d * xn, axis=0, keepdims=True), dg_ref.shape)

    return pl.pallas_call(
        body,
        name=name,
        grid=(3 * nsec, s // tr),
        in_specs=[blk, sec_spec(0), sec_spec(1), sec_spec(2), gspec],
        out_specs=(blk, gspec),
        out_shape=(jax.ShapeDtypeStruct((s, width), BF16), jax.ShapeDtypeStruct((24, LANES), F32)),
        compiler_params=_params(("arbitrary", "arbitrary")),
    )(qkv, dq, dk, dv, gains)


def _split_dot(v, tri):
    hi = v.astype(BF16)
    lo = (v - hi.astype(F32)).astype(BF16)
    return _dot(hi, tri, 1, 0) + _dot(lo, tri, 1, 0)


def _attn_scores(qm, kb, tri_gt, valid, r_run):
    z = _dot(qm, kb, 1, 1)
    zl = jnp.minimum(z, 0.0) - jnp.log(1.0 + jnp.exp(-jnp.abs(z)))
    lk = zl - z
    if valid is not None:
        lk = jnp.where(valid, lk, 0.0)
    after = _split_dot(lk, tri_gt)
    w = jnp.exp(zl + after + r_run)
    if valid is not None:
        w = jnp.where(valid, w, 0.0)
    return zl, lk, w


def _attn_alive(rr):
    return jnp.max(jnp.maximum(rr[0], rr[1])) > ATTN_DEAD_LOG_WEIGHT


def _attn_specs(s, width, dh, tq):
    nsec = _head_pair_geometry(width, dh)
    qspec = pl.BlockSpec((tq, LANES), lambda h, i: (i, h))
    kspec = pl.BlockSpec((s, LANES), lambda h, i: (0, nsec + h))
    vspec = pl.BlockSpec((s, LANES), lambda h, i: (0, 2 * nsec + h))
    return nsec, qspec, kspec, vspec


def _attn_masks(tq, tk, dh):
    lo = lax.broadcasted_iota(jnp.int32, (1, LANES), 1) < dh
    r_io = lax.broadcasted_iota(jnp.int32, (tk, tk), 0)
    c_io = lax.broadcasted_iota(jnp.int32, (tk, tk), 1)
    qrow = lax.broadcasted_iota(jnp.int32, (tq, tk), 0)
    kcol = lax.broadcasted_iota(jnp.int32, (tq, tk), 1)
    return lo, r_io, c_io, qrow, kcol


def _attn_fwd(qkv16, dh, *, name):
    s, width = qkv16.shape
    tq = _tile(s, (ATTN_Q_BLOCK, ATTN_K_BLOCK))
    tk = _tile(tq, (ATTN_K_BLOCK,))
    nd = tq // tk
    nsec, qspec, kspec, vspec = _attn_specs(s, width, dh, tq)

    def body(q_ref, k_ref, v_ref, o_ref):
        i = pl.program_id(1)
        lo, r_io, c_io, qrow, kcol = _attn_masks(tq, tk, dh)
        q2 = q_ref[...]
        zq = jnp.zeros_like(q2)
        qs = (jnp.where(lo, q2, zq), jnp.where(lo, zq, q2))
        tri_gt = (r_io > c_io).astype(BF16)

        def block(j, carry, diag):
            rr, acc = carry
            start = pl.multiple_of(j * tk, tk)
            kb = k_ref[pl.ds(start, tk), :]
            vb = v_ref[pl.ds(start, tk), :]
            valid = None if diag is None else kcol + diag * tk < qrow
            outs, new_r = [], []
            for hh in range(2):
                _, lk, w = _attn_scores(qs[hh], kb, tri_gt, valid, rr[hh])
                outs.append(_dot(w.astype(BF16), vb, 1, 0))
                new_r.append(rr[hh] + jnp.sum(lk, axis=-1, keepdims=True))
            return tuple(new_r), acc + jnp.where(lo, outs[0], outs[1])

        zero = jnp.zeros((tq, 1), F32)
        carry = ((zero, zero), jnp.zeros((tq, LANES), F32))
        for dg in reversed(range(nd)):
            carry = block(i * nd + dg, carry, dg)
        below = i * nd

        def step(st):
            jj, _, cr = st
            cr = block(below - 1 - jj, cr, None)
            return jj + 1, _attn_alive(cr[0]), cr

        _, _, carry = lax.while_loop(lambda st: (st[0] < below) & st[1], step,
                                     (jnp.int32(0), _attn_alive(carry[0]), carry))
        o_ref[...] = carry[1]

    return pl.pallas_call(
        body,
        name=name,
        grid=(nsec, s // tq),
        in_specs=[qspec, kspec, vspec],
        out_specs=qspec,
        out_shape=jax.ShapeDtypeStruct((s, width // 3), F32),
        compiler_params=_params(("parallel", "arbitrary")),
    )(qkv16, qkv16, qkv16)


def _attn_bwd(qkv16, o, do, dh, *, name):
    s, width = qkv16.shape
    tq = _tile(s, (ATTN_Q_BLOCK, ATTN_K_BLOCK))
    tk = _tile(tq, (ATTN_K_BLOCK,))
    nd = tq // tk
    nsec, qspec, kspec, vspec = _attn_specs(s, width, dh, tq)
    accspec = pl.BlockSpec((s, LANES), lambda h, i: (0, h))

    def body(q_ref, k_ref, v_ref, o_ref, do_ref, dq_ref, dk_ref, dv_ref):
        i = pl.program_id(1)

        @pl.when(i == 0)
        def _():
            dk_ref[...] = jnp.zeros_like(dk_ref)
            dv_ref[...] = jnp.zeros_like(dv_ref)

        lo, r_io, c_io, qrow, kcol = _attn_masks(tq, tk, dh)
        q2 = q_ref[...]
        zq = jnp.zeros_like(q2)
        qs = (jnp.where(lo, q2, zq), jnp.where(lo, zq, q2))
        do16 = do_ref[...].astype(BF16)
        dos = (jnp.where(lo, do16, zq), jnp.where(lo, zq, do16))
        prod = do16.astype(F32) * o_ref[...]
        deltas = (jnp.sum(jnp.where(lo, prod, 0.0), axis=-1, keepdims=True),
                  jnp.sum(jnp.where(lo, 0.0, prod), axis=-1, keepdims=True))
        tri_gt = (r_io > c_io).astype(BF16)
        tri_ge = (r_io >= c_io).astype(BF16)

        def block(j, carry, diag):
            rr, er, dq = carry
            start = pl.multiple_of(j * tk, tk)
            kb = k_ref[pl.ds(start, tk), :]
            vb = v_ref[pl.ds(start, tk), :]
            valid = None if diag is None else kcol + diag * tk < qrow
            new_r, new_e, dqs = [], [], []
            dk_blk = dv_blk = None
            for hh in range(2):
                zl, lk, w = _attn_scores(qs[hh], kb, tri_gt, valid, rr[hh])
                beta = jnp.exp(zl)
                w16 = w.astype(BF16)
                e = _dot(dos[hh], vb, 1, 1) * w16.astype(F32)
                e_before = deltas[hh] - er[hh] - _split_dot(e, tri_ge)
                dz = e - beta * (e + e_before)
                if valid is not None:
                    dz = jnp.where(valid, dz, 0.0)
                dz16 = dz.astype(BF16)
                dqs.append(_dot(dz16, kb, 1, 0))
                dkc = _dot(dz16, qs[hh], 0, 0)
                dvc = _dot(w16, dos[hh], 0, 0)
                dk_blk = dkc if dk_blk is None else dk_blk + dkc
                dv_blk = dvc if dv_blk is None else dv_blk + dvc
                new_r.append(rr[hh] + jnp.sum(lk, axis=-1, keepdims=True))
                new_e.append(er[hh] + jnp.sum(e, axis=-1, keepdims=True))
            dk_ref[pl.ds(start, tk), :] += dk_blk
            dv_ref[pl.ds(start, tk), :] += dv_blk
            return tuple(new_r), tuple(new_e), dq + jnp.where(lo, dqs[0], dqs[1])

        zero = jnp.zeros((tq, 1), F32)
        carry = ((zero, zero), (zero, zero), jnp.zeros((tq, LANES), F32))
        for dg in reversed(range(nd)):
            carry = block(i * nd + dg, carry, dg)
        below = i * nd

        def step(st):
            jj, _, cr = st
            cr = block(below - 1 - jj, cr, None)
            return jj + 1, _attn_alive(cr[0]), cr

        _, _, carry = lax.while_loop(lambda st: (st[0] < below) & st[1], step,
                                     (jnp.int32(0), _attn_alive(carry[0]), carry))
        dq_ref[...] = carry[2]

    shp = jax.ShapeDtypeStruct((s, width // 3), F32)
    return pl.pallas_call(
        body,
        name=name,
        grid=(nsec, s // tq),
        in_specs=[qspec, kspec, vspec, qspec, qspec],
        out_specs=(qspec, accspec, accspec),
        out_shape=(shp, shp, shp),
        compiler_params=pltpu.CompilerParams(dimension_semantics=("parallel", "arbitrary"),
                                             vmem_limit_bytes=V7X_ATTN_BWD_VMEM_LIMIT_BYTES),
    )(qkv16, qkv16, qkv16, o, do)


def _adam_update(wv, gv, mv, vv):
    c1 = 1.0 - ADAM_B1 ** ADAM_STEP
    c2 = 1.0 - ADAM_B2 ** ADAM_STEP
    nm = ADAM_B1 * mv + (1.0 - ADAM_B1) * gv
    nv = ADAM_B2 * vv + (1.0 - ADAM_B2) * (gv * gv)
    delta = -ADAM_LR * ((nm / c1) / (jnp.sqrt(nv / c2) + ADAM_EPS) + ADAM_WD * wv)
    return delta, nm, nv


def _sum_slabs(r, *, name):
    n, rows, lanes = r.shape
    tr = _tile(rows, (1024, 512, 256, 128, 64, 32, 16))

    def body(r_ref, o_ref):
        acc = r_ref[0]
        for d in range(1, n):
            acc = acc + r_ref[d]
        o_ref[...] = acc

    return pl.pallas_call(
        body,
        name=name,
        grid=(rows // tr,),
        in_specs=[pl.BlockSpec((n, tr, lanes), lambda i: (0, i, 0))],
        out_specs=pl.BlockSpec((tr, lanes), lambda i: (i, 0)),
        out_shape=jax.ShapeDtypeStruct((rows, lanes), F32),
        compiler_params=_params(("parallel",)),
    )(r)


def _adamw(w, g, m, v, *, name):
    rows, cols = w.shape
    tr = _tile(rows, (1024, 512, 256, 128, 64, 32, 16))
    spec = pl.BlockSpec((tr, cols), lambda i: (i, 0))

    def body(w_ref, g_ref, m_ref, v_ref, d_ref, nm_ref, nv_ref):
        d_ref[...], nm_ref[...], nv_ref[...] = _adam_update(w_ref[...], g_ref[...], m_ref[...], v_ref[...])

    shp = jax.ShapeDtypeStruct((rows, cols), F32)
    return pl.pallas_call(
        body,
        name=name,
        grid=(rows // tr,),
        in_specs=[spec, spec, spec, spec],
        out_specs=(spec, spec, spec),
        out_shape=(shp, shp, shp),
        compiler_params=_params(("parallel",)),
    )(w, g, m, v)


def _adamw_sum(w, r, m, v, *, name):
    n, rows, cols = r.shape
    tr = _tile(rows, (256, 128, 64, 32, 16, 8))
    spec = pl.BlockSpec((tr, cols), lambda i: (i, 0))

    def body(w_ref, r_ref, m_ref, v_ref, g_ref, d_ref, nm_ref, nv_ref):
        gv = r_ref[0].astype(F32)
        for d in range(1, n):
            gv = gv + r_ref[d].astype(F32)
        g_ref[...] = gv
        d_ref[...], nm_ref[...], nv_ref[...] = _adam_update(w_ref[...], gv, m_ref[...], v_ref[...])

    shp = jax.ShapeDtypeStruct((rows, cols), F32)
    return pl.pallas_call(
        body,
        name=name,
        grid=(rows // tr,),
        in_specs=[spec, pl.BlockSpec((n, tr, cols), lambda i: (0, i, 0)), spec, spec],
        out_specs=(spec, spec, spec, spec),
        out_shape=(shp, shp, shp, shp),
        compiler_params=_params(("parallel",)),
    )(w, r, m, v)


_MATMUL_WEIGHTS = (("conv_w_in", 2), ("conv_w_out", 1), ("attn_w_qkv", 2), ("attn_w_o", 1), ("ffn_w_up", 2),
                   ("ffn_w_down", 1))
_TAP_WEIGHTS = (("conv_a_dw_w", 2), ("conv_b_dw_w", 2), ("ffn_dw_w", 2))
_REPLICATED = ("mix_norm_g", "ffn_norm_g", "conv_a_dw_b", "conv_a_ln_g", "conv_a_ln_b", "attn_q_g", "attn_k_g",
               "ffn_dw_b")
_WEIGHT_ORDER = ("mix_norm_g", "ffn_norm_g", "conv_w_in", "conv_a_dw_w", "conv_a_dw_b", "conv_a_ln_g",
                 "conv_a_ln_b", "conv_b_dw_w", "conv_w_out", "attn_w_qkv", "attn_q_g", "attn_k_g", "attn_w_o",
                 "ffn_w_up", "ffn_dw_w", "ffn_dw_b", "ffn_w_down")


def _assemble(gathered, axis):
    n, l, a, b = gathered.shape
    if axis == 2:
        return jnp.transpose(gathered, (1, 2, 0, 3)).reshape(l, a, n * b)
    return jnp.transpose(gathered, (1, 0, 2, 3)).reshape(l, n * a, b)


def _split_shards(full, axis):
    l, a, b = full.shape
    if axis == 2:
        return jnp.transpose(full.reshape(l, a, N_DEV, b // N_DEV), (2, 0, 1, 3))
    return jnp.transpose(full.reshape(l, N_DEV, a // N_DEV, b), (1, 0, 2, 3))


def kernel(x, mix_norm_g, ffn_norm_g, conv_w_in, conv_a_dw_w, conv_a_dw_b, conv_a_ln_g, conv_a_ln_b, conv_b_dw_w, conv_w_out, attn_w_qkv, attn_q_g, attn_k_g, attn_w_o, ffn_w_up, ffn_dw_w, ffn_dw_b, ffn_w_down, loss_target, m_mix_norm_g, m_ffn_norm_g, m_conv_w_in, m_conv_a_dw_w, m_conv_a_dw_b, m_conv_a_ln_g, m_conv_a_ln_b, m_conv_b_dw_w, m_conv_w_out, m_attn_w_qkv, m_attn_q_g, m_attn_k_g, m_attn_w_o, m_ffn_w_up, m_ffn_dw_w, m_ffn_dw_b, m_ffn_w_down, v_mix_norm_g, v_ffn_norm_g, v_conv_w_in, v_conv_a_dw_w, v_conv_a_dw_b, v_conv_a_ln_g, v_conv_a_ln_b, v_conv_b_dw_w, v_conv_w_out, v_attn_w_qkv, v_attn_q_g, v_attn_k_g, v_attn_w_o, v_ffn_w_up, v_ffn_dw_w, v_ffn_dw_b, v_ffn_w_down):
    args = dict(locals())
    weights = {n: args[n] for n in _WEIGHT_ORDER}
    mom_m = {n: args["m_" + n] for n in _WEIGHT_ORDER}
    mom_v = {n: args["v_" + n] for n in _WEIGHT_ORDER}

    _, s, d = x.shape
    depth = mix_norm_g.shape[0]
    dh = attn_q_g.shape[1]
    x0 = x.reshape(s, d)
    target = loss_target.reshape(s, d)

    gathered = _all_gather([weights[n].astype(BF16) for n, _ in _MATMUL_WEIGHTS], "gather_matmul_weights")
    full = {}
    for (n, axis), g in zip(_MATMUL_WEIGHTS, gathered):
        _, l, a, b = g.shape
        if axis == 2:
            full[n] = _assemble_cols(g.reshape(N_DEV, l * a, b), name=f"assemble_{n}").reshape(l, a, N_DEV * b)
        else:
            full[n] = jnp.transpose(g, (1, 0, 2, 3)).reshape(l, N_DEV * a, b)
    tap_shapes = [weights[n].shape for n, _ in _TAP_WEIGHTS]
    g_tap = _all_gather([_pack([weights[n] for n, _ in _TAP_WEIGHTS], F32)], "gather_tap_weights")[0]
    for (n, axis), part in zip(_TAP_WEIGHTS, _unpack(g_tap, tap_shapes, lead=(N_DEV,))):
        full[n] = _assemble(part, axis)
    f = full["ffn_w_down"].shape[1]

    saved = []
    xs = x0
    for layer in range(depth):
        i = layer // 2
        sv = {"x_in": xs}
        hn = _rmsnorm_fwd(xs, mix_norm_g[layer:layer + 1], name=f"mix_norm_fwd_{layer}")
        sv["h"] = hn
        if layer % 2 == 0:
            p = _matmul(hn, full["conv_w_in"][i], name=f"conv_in_{layer}")
            ab, a1 = _conv_act_fwd(p, full["conv_a_dw_w"][i], conv_a_dw_b[i:i + 1], conv_a_ln_g[i:i + 1],
                                   conv_a_ln_b[i:i + 1], full["conv_b_dw_w"][i], name=f"conv_act_fwd_{layer}")
            sv.update(p=p, a1=a1, mix=ab)
            xs = _matmul(ab, full["conv_w_out"][i], add=xs, name=f"conv_out_{layer}")
        else:
            reps = LANES // dh
            gains = jnp.concatenate([
                jnp.broadcast_to(jnp.tile(attn_q_g[i], reps) * dh ** -0.5, (8, LANES)),
                jnp.broadcast_to(jnp.tile(attn_k_g[i], reps), (8, LANES)),
                jnp.ones((8, LANES), F32)], axis=0)
            qkv = _matmul(hn, full["attn_w_qkv"][i], name=f"attn_qkv_{layer}")
            qkv16 = _qkv_prep_fwd(qkv, gains, dh, name=f"qkv_prep_fwd_{layer}")
            o = _attn_fwd(qkv16, dh, name=f"attn_fwd_{layer}")
            sv.update(qkv=qkv, qkv16=qkv16, gains=gains, mix=o)
            xs = _matmul(o, full["attn_w_o"][i], add=xs, name=f"attn_out_{layer}")
        sv["x_mid"] = xs
        h2 = _rmsnorm_fwd(xs, ffn_norm_g[layer:layer + 1], name=f"ffn_norm_fwd_{layer}")
        w_up = full["ffn_w_up"][layer]
        ug = _matmul(h2, w_up[:, :f], name=f"ffn_up_gate_{layer}")
        uv = _matmul(h2, w_up[:, f:], name=f"ffn_up_val_{layer}")
        taps = full["ffn_dw_w"][layer]
        bias = ffn_dw_b[layer:layer + 1]
        act = _ffn_act_fwd(ug, uv, taps[:, :f], taps[:, f:], bias[:, :f], bias[:, f:], name=f"ffn_act_fwd_{layer}")
        sv.update(h2=h2, ug=ug, uv=uv, act=act)
        xs = _matmul(act, full["ffn_w_down"][layer], add=xs, name=f"ffn_down_{layer}")
        saved.append(sv)

    sq, dx, dx16 = _loss_head(xs, target, name="loss_head")
    loss = lax.psum(0.5 * sq[0, 0] / d, ("x", "y", "c"))

    grads = {n: [None] * weights[n].shape[0] for n in _WEIGHT_ORDER}
    for layer in reversed(range(depth)):
        i = layer // 2
        sv = saved[layer]
        w_up = full["ffn_w_up"][layer]
        taps = full["ffn_dw_w"][layer]
        bias = ffn_dw_b[layer:layer + 1]
        dact = _matmul(dx16, full["ffn_w_down"][layer], tb=True, name=f"ffn_down_dx_{layer}")
        grads["ffn_w_down"][layer] = _weight_grad(sv["act"], dx16, name=f"ffn_down_dw_{layer}")
        dug, duv, dwg, dwv, dbg, dbv = _ffn_act_bwd(sv["ug"], sv["uv"], dact, taps[:, :f], taps[:, f:], bias[:, :f],
                                                   bias[:, f:], name=f"ffn_act_bwd_{layer}")
        grads["ffn_dw_w"][layer] = jnp.concatenate([dwg[:3], dwv[:3]], axis=1)
        grads["ffn_dw_b"][layer] = jnp.concatenate([dbg[0], dbv[0]], axis=0)
        dh2 = _matmul(dug, w_up[:, :f], tb=True, name=f"ffn_up_gate_dx_{layer}")
        dh2 = _matmul(duv, w_up[:, f:], tb=True, add=dh2, name=f"ffn_up_val_dx_{layer}")
        grads["ffn_w_up"][layer] = jnp.concatenate(
            [_weight_grad(sv["h2"], dug, name=f"ffn_up_gate_dw_{layer}"),
             _weight_grad(sv["h2"], duv, name=f"ffn_up_val_dw_{layer}")], axis=1)
        dx, dx16, dg = _rmsnorm_bwd(sv["x_mid"], ffn_norm_g[layer:layer + 1], dh2, dx, name=f"ffn_norm_bwd_{layer}")
        grads["ffn_norm_g"][layer] = dg[0]

        if layer % 2 == 0:
            dab = _matmul(dx16, full["conv_w_out"][i], tb=True, name=f"conv_out_dx_{layer}")
            grads["conv_w_out"][i] = _weight_grad(sv["mix"], dx16, name=f"conv_out_dw_{layer}")
            ka = full["conv_a_dw_w"].shape[1]
            kb = full["conv_b_dw_w"].shape[1]
            dp, dwa, dba, dlng, dlnb, dwb = _conv_act_bwd(
                sv["p"], sv["a1"], dab, full["conv_a_dw_w"][i], conv_a_ln_g[i:i + 1], conv_a_ln_b[i:i + 1],
                full["conv_b_dw_w"][i], name=f"conv_act_bwd_{layer}")
            grads["conv_a_dw_w"][i] = dwa[:ka]
            grads["conv_a_dw_b"][i] = dba[0]
            grads["conv_a_ln_g"][i] = dlng[0]
            grads["conv_a_ln_b"][i] = dlnb[0]
            grads["conv_b_dw_w"][i] = dwb[:kb]
            dhn = _matmul(dp, full["conv_w_in"][i], tb=True, name=f"conv_in_dx_{layer}")
            grads["conv_w_in"][i] = _weight_grad(sv["h"], dp, name=f"conv_in_dw_{layer}")
        else:
            do = _matmul(dx16, full["attn_w_o"][i], tb=True, name=f"attn_out_dx_{layer}")
            grads["attn_w_o"][i] = _weight_grad(sv["mix"], dx16, name=f"attn_out_dw_{layer}")
            dq, dk, dv = _attn_bwd(sv["qkv16"], sv["mix"], do, dh, name=f"attn_bwd_{layer}")
            dqkv, dgains = _qkv_prep_bwd(sv["qkv"], dq, dk, dv, sv["gains"], dh, name=f"qkv_prep_bwd_{layer}")
            grads["attn_q_g"][i] = dgains[0].reshape(-1, dh).sum(axis=0) * dh ** -0.5
            grads["attn_k_g"][i] = dgains[8].reshape(-1, dh).sum(axis=0)
            dhn = _matmul(dqkv, full["attn_w_qkv"][i], tb=True, name=f"attn_qkv_dx_{layer}")
            grads["attn_w_qkv"][i] = _weight_grad(sv["h"], dqkv, name=f"attn_qkv_dw_{layer}")
        dx, dx16, dg = _rmsnorm_bwd(sv["x_in"], mix_norm_g[layer:layer + 1], dhn, dx, name=f"mix_norm_bwd_{layer}")
        grads["mix_norm_g"][layer] = dg[0]

    me = _my_index()
    sends = []
    for n, axis in _MATMUL_WEIGHTS:
        l, a, b = weights[n].shape
        if axis == 2:
            stacked = jnp.stack(grads[n], axis=0)
            sends.append(_split_cols(stacked.reshape(l * a, N_DEV * b), name=f"split_{n}").reshape(N_DEV, l, a, b))
        else:
            sends.append(jnp.stack([g.reshape(N_DEV, a, b) for g in grads[n]], axis=1))
    received = _exchange(sends, "exchange_matmul_grads")
    final_grads, delta, new_m, new_v = {}, {}, {}, {}
    for (n, _), r in zip(_MATMUL_WEIGHTS, received):
        l, a, b = weights[n].shape
        outs = _adamw_sum(weights[n].reshape(l * a, b), r.reshape(N_DEV, l * a, b), mom_m[n].reshape(l * a, b),
                          mom_v[n].reshape(l * a, b), name=f"adamw_{n}")
        final_grads[n], delta[n], new_m[n], new_v[n] = [o.reshape(l, a, b) for o in outs]

    small_names = list(_REPLICATED) + [n for n, _ in _TAP_WEIGHTS]
    local = {n: jnp.stack(grads[n], axis=0) for n in small_names}
    small_shapes = [local[n].shape for n in small_names]
    g_small = _sum_slabs(_all_gather([_pack([local[n] for n in small_names], F32)], "gather_small_grads")[0],
                         name="sum_small_grads")
    reduced = dict(zip(small_names, _unpack(g_small, small_shapes)))
    for n in _REPLICATED:
        final_grads[n] = reduced[n]
    for n, axis in _TAP_WEIGHTS:
        final_grads[n] = lax.dynamic_index_in_dim(_split_shards(reduced[n], axis), me, axis=0, keepdims=False)
    shapes = [weights[n].shape for n in small_names]
    d_s, m_s, v_s = _adamw(
        _pack([weights[n] for n in small_names], F32), _pack([final_grads[n] for n in small_names], F32),
        _pack([mom_m[n] for n in small_names], F32), _pack([mom_v[n] for n in small_names], F32), name="adamw_small")
    delta.update(zip(small_names, _unpack(d_s, shapes)))
    new_m.update(zip(small_names, _unpack(m_s, shapes)))
    new_v.update(zip(small_names, _unpack(v_s, shapes)))

    order = list(_WEIGHT_ORDER)
    return (loss, dx.reshape(x.shape), *[final_grads[n] for n in order], *[delta[n] for n in order],
            *[new_m[n] for n in order], *[new_v[n] for n in order])
```

```python
import jax
import jax.numpy as jnp
from jax import lax
from jax.experimental import pallas as pl
from jax.experimental.pallas import tpu as pltpu

F32 = jnp.float32
BF16 = jnp.bfloat16
EPS = 1e-6
N_DEV = 8
MESH_ID = pl.DeviceIdType.MESH

ADAM_LR = 0.001
ADAM_B1 = 0.9
ADAM_B2 = 0.999
ADAM_EPS = 1e-08
ADAM_WD = 0.01
ADAM_STEP = 10

V7X_VMEM_LIMIT_BYTES = 48 * 1024 * 1024
PACK_QUANTUM = 2048
A_HALO = 32
S_HALO = 8
LANES = 128
ATTN_Q_BLOCK = 256
ATTN_K_BLOCK = 256
V7X_ATTN_BWD_VMEM_LIMIT_BYTES = 56 * 1024 * 1024
MATMUL_MAX_SINGLE_K = 3072
FFN_CHUNK_ROWS = 64
ATTN_DEAD_LOG_WEIGHT = -110.0


def _tile(n, prefs):
    for p in prefs:
        if n % p == 0:
            return p
    return n


def _params(sem=None):
    return pltpu.CompilerParams(dimension_semantics=sem, vmem_limit_bytes=V7X_VMEM_LIMIT_BYTES)


def _sigmoid(x):
    return 1.0 / (1.0 + jnp.exp(-x))


def _dot(a, b, ca, cb):
    return lax.dot_general(a, b, (((ca,), (cb,)), ((), ())), preferred_element_type=F32)


def _my_index():
    return 4 * lax.axis_index("x") + 2 * lax.axis_index("y") + lax.axis_index("c")


def _all_gather(shards, name):
    n = len(shards)

    def body(*refs):
        x_refs, out_refs = refs[:n], refs[n:2 * n]
        send_sems, recv_sems, local_sems = refs[2 * n:]
        x, y, c = lax.axis_index("x"), lax.axis_index("y"), lax.axis_index("c")
        me, sibling = (x, y, c), (x, y, 1 - c)
        chips = [(1 - x, y), (x, 1 - y), (1 - x, 1 - y)]

        def copy(a, k, block, to, src=None):
            slot = out_refs[a].at[4 * block[0] + 2 * block[1] + block[2]]
            return pltpu.make_async_remote_copy(
                src_ref=slot if src is None else src, dst_ref=slot,
                send_sem=send_sems.at[7 * a + k], recv_sem=recv_sems.at[7 * a + k],
                device_id=to, device_id_type=MESH_ID)

        mine = [pltpu.make_async_copy(x_refs[a], out_refs[a].at[4 * x + 2 * y + c], local_sems.at[a])
                for a in range(n)]
        for cp in mine:
            cp.start()
        first = []
        for a in range(n):
            first.append(copy(a, 0, me, sibling, src=x_refs[a]))
            first += [copy(a, 1 + j, me, (*chip, c), src=x_refs[a]) for j, chip in enumerate(chips)]
        for cp in first:
            cp.start()
        passed = []
        for j, chip in enumerate(chips):
            for a in range(n):
                copy(a, 1 + j, (*chip, c), me).wait_recv()
                fwd = copy(a, 4 + j, (*chip, c), sibling)
                fwd.start()
                passed.append(fwd)
        for a in range(n):
            copy(a, 0, sibling, me).wait_recv()
            for j, chip in enumerate(chips):
                copy(a, 4 + j, (*chip, 1 - c), me).wait_recv()
        for cp in first + passed:
            cp.wait_send()
        for cp in mine:
            cp.wait()

    hbm = pl.BlockSpec(memory_space=pl.ANY)
    return pl.pallas_call(
        body,
        name=name,
        out_shape=[jax.ShapeDtypeStruct((N_DEV,) + sh.shape, sh.dtype) for sh in shards],
        in_specs=[hbm] * n,
        out_specs=[hbm] * n,
        scratch_shapes=[
            pltpu.SemaphoreType.DMA((7 * n,)),
            pltpu.SemaphoreType.DMA((7 * n,)),
            pltpu.SemaphoreType.DMA((n,)),
        ],
    )(*shards)


def _exchange(slabs, name):
    n = len(slabs)

    def body(*refs):
        g_refs, out_refs = refs[:n], refs[n:2 * n]
        send_sems, recv_sems, local_sems = refs[2 * n:]
        x, y, c = lax.axis_index("x"), lax.axis_index("y"), lax.axis_index("c")
        me = 4 * x + 2 * y + c
        local = [pltpu.make_async_copy(g_refs[a].at[me], out_refs[a].at[me], local_sems.at[a]) for a in range(n)]
        for cp in local:
            cp.start()
        copies = []
        for k in range(1, N_DEV):
            px = (x + ((k >> 2) & 1)) % 2
            py = (y + ((k >> 1) & 1)) % 2
            pc = (c + (k & 1)) % 2
            for a in range(n):
                cp = pltpu.make_async_remote_copy(
                    src_ref=g_refs[a].at[4 * px + 2 * py + pc], dst_ref=out_refs[a].at[me],
                    send_sem=send_sems.at[7 * a + k - 1], recv_sem=recv_sems.at[7 * a + k - 1],
                    device_id=(px, py, pc), device_id_type=MESH_ID)
                cp.start()
                copies.append(cp)
        for cp in copies:
            cp.wait_recv()
        for cp in copies:
            cp.wait_send()
        for cp in local:
            cp.wait()

    hbm = pl.BlockSpec(memory_space=pl.ANY)
    return pl.pallas_call(
        body,
        name=name,
        out_shape=[jax.ShapeDtypeStruct(g.shape, g.dtype) for g in slabs],
        in_specs=[hbm] * n,
        out_specs=[hbm] * n,
        scratch_shapes=[
            pltpu.SemaphoreType.DMA((7 * n,)),
            pltpu.SemaphoreType.DMA((7 * n,)),
            pltpu.SemaphoreType.DMA((n,)),
        ],
    )(*slabs)


def _assemble_cols(g, *, name):
    n, r, w = g.shape
    tr = _tile(r, (256, 128, 64, 32, 16))

    def body(g_ref, o_ref):
        for j in range(n):
            o_ref[:, j * w:(j + 1) * w] = g_ref[j]

    return pl.pallas_call(
        body,
        name=name,
        grid=(r // tr,),
        in_specs=[pl.BlockSpec((n, tr, w), lambda i: (0, i, 0))],
        out_specs=pl.BlockSpec((tr, n * w), lambda i: (i, 0)),
        out_shape=jax.ShapeDtypeStruct((r, n * w), g.dtype),
        compiler_params=_params(("parallel",)),
    )(g)


def _split_cols(x, *, name):
    r, nw = x.shape
    w = nw // N_DEV
    tr = _tile(r, (256, 128, 64, 32, 16))

    def body(x_ref, o_ref):
        for j in range(N_DEV):
            o_ref[j] = x_ref[:, j * w:(j + 1) * w]

    return pl.pallas_call(
        body,
        name=name,
        grid=(r // tr,),
        in_specs=[pl.BlockSpec((tr, nw), lambda i: (i, 0))],
        out_specs=pl.BlockSpec((N_DEV, tr, w), lambda i: (0, i, 0)),
        out_shape=jax.ShapeDtypeStruct((N_DEV, r, w), x.dtype),
        compiler_params=_params(("parallel",)),
    )(x)


def _padded(n):
    return -(-n // PACK_QUANTUM) * PACK_QUANTUM


def _pack(parts, dtype, lead=()):
    flat = []
    for p in parts:
        p = p.astype(dtype).reshape(lead + (-1,))
        n = p.shape[-1]
        flat.append(jnp.pad(p, [(0, 0)] * len(lead) + [(0, _padded(n) - n)]))
    return jnp.concatenate(flat, axis=-1).reshape(lead + (-1, 128))


def _unpack(buf, shapes, lead=()):
    flat = buf.reshape(lead + (-1,))
    out, off = [], 0
    for shp in shapes:
        n = 1
        for d in shp:
            n *= d
        out.append(flat[..., off:off + n].reshape(lead + tuple(shp)))
        off += _padded(n)
    return out


def _half_if_tiled(n):
    half = n // 2
    return (half,) if n % 2 == 0 and half % LANES == 0 and half <= 1536 else ()


def _matmul(a, b, *, ta=False, tb=False, out_dtype=F32, add=None, name):
    if ta:
        kdim, m = a.shape
    else:
        m, kdim = a.shape
    n = b.shape[0] if tb else b.shape[1]
    bm = _tile(m, (1024,) + _half_if_tiled(m) + (512, 256, 128))
    bn = _tile(n, (512,) + _half_if_tiled(n) + (256, 128))
    bk = kdim if kdim <= MATMUL_MAX_SINGLE_K else _tile(kdim, (1024, 512, 256, 128))
    nk = kdim // bk

    a_spec = pl.BlockSpec((bk, bm), lambda i, j, k: (k, i)) if ta else pl.BlockSpec((bm, bk), lambda i, j, k: (i, k))
    b_spec = pl.BlockSpec((bn, bk), lambda i, j, k: (j, k)) if tb else pl.BlockSpec((bk, bn), lambda i, j, k: (k, j))
    o_spec = pl.BlockSpec((bm, bn), lambda i, j, k: (i, j))
    in_specs = [a_spec, b_spec] + ([o_spec] if add is not None else [])
    operands = [a, b] + ([add] if add is not None else [])

    def product(a_ref, b_ref):
        return _dot(a_ref[...].astype(BF16), b_ref[...].astype(BF16), 0 if ta else 1, 1 if tb else 0)

    def finish(r, refs):
        if add is not None:
            r = r + refs[2][...]
        return r

    def body_single(*refs):
        o_ref = refs[-1]
        o_ref[...] = finish(product(refs[0], refs[1]), refs).astype(o_ref.dtype)

    def body_multi(*refs):
        o_ref, acc_ref = refs[-2], refs[-1]
        k = pl.program_id(2)

        @pl.when(k == 0)
        def _():
            acc_ref[...] = jnp.zeros_like(acc_ref)

        acc_ref[...] += product(refs[0], refs[1])

        @pl.when(k == nk - 1)
        def _():
            o_ref[...] = finish(acc_ref[...], refs).astype(o_ref.dtype)

    return pl.pallas_call(
        body_single if nk == 1 else body_multi,
        name=name,
        grid=(m // bm, n // bn, nk),
        in_specs=in_specs,
        out_specs=o_spec,
        out_shape=jax.ShapeDtypeStruct((m, n), out_dtype),
        scratch_shapes=[] if nk == 1 else [pltpu.VMEM((bm, bn), F32)],
        compiler_params=_params(("parallel", "parallel", "arbitrary")),
    )(*operands)


def _weight_grad(x, dy, *, name):
    return _matmul(x, dy, ta=True, out_dtype=BF16, name=name)


def _rmsnorm_fwd(x, g, *, name):
    r, d = x.shape
    tr = _tile(r, (1024, 512, 256, 128))

    def body(x_ref, g_ref, o_ref):
        xv = x_ref[...]
        y = xv * lax.rsqrt(jnp.mean(xv * xv, axis=-1, keepdims=True) + EPS)
        y = y * g_ref[...]
        o_ref[...] = y.astype(o_ref.dtype)

    return pl.pallas_call(
        body,
        name=name,
        grid=(r // tr,),
        in_specs=[pl.BlockSpec((tr, d), lambda i: (i, 0)), pl.BlockSpec((1, d), lambda i: (0, 0))],
        out_specs=pl.BlockSpec((tr, d), lambda i: (i, 0)),
        out_shape=jax.ShapeDtypeStruct((r, d), BF16),
        compiler_params=_params(("parallel",)),
    )(x, g)


def _rmsnorm_bwd(x, g, dy, res, *, name):
    r, d = x.shape
    tr = _tile(r, (512, 256, 128))
    row = pl.BlockSpec((tr, d), lambda i: (i, 0))
    vec = pl.BlockSpec((1, d), lambda i: (0, 0))
    operands = [x, g, dy] + ([res] if res is not None else [])

    def body(*refs):
        if res is not None:
            x_ref, g_ref, dy_ref, res_ref, dx_ref, dx16_ref, dg_ref = refs
        else:
            x_ref, g_ref, dy_ref, dx_ref, dx16_ref, dg_ref = refs
        xv = x_ref[...]
        dyv = dy_ref[...].astype(F32)
        rs = lax.rsqrt(jnp.mean(xv * xv, axis=-1, keepdims=True) + EPS)
        xn = xv * rs
        dyg = dyv * g_ref[...]
        dx = rs * (dyg - xn * jnp.mean(dyg * xn, axis=-1, keepdims=True))
        if res is not None:
            dx = dx + res_ref[...]
        dx_ref[...] = dx
        dx16_ref[...] = dx.astype(dx16_ref.dtype)

        @pl.when(pl.program_id(0) == 0)
        def _():
            dg_ref[...] = jnp.zeros_like(dg_ref)

        dg_ref[...] += jnp.sum(dyv * xn, axis=0, keepdims=True)

    return pl.pallas_call(
        body,
        name=name,
        grid=(r // tr,),
        in_specs=[row, vec, row] + ([row] if res is not None else []),
        out_specs=(row, row, vec),
        out_shape=(jax.ShapeDtypeStruct((r, d), F32), jax.ShapeDtypeStruct((r, d), BF16),
                   jax.ShapeDtypeStruct((1, d), F32)),
        compiler_params=_params(("arbitrary",)),
    )(*operands)


def _loss_head(y, target, *, name):
    s, d = y.shape
    ts = _tile(s, (512, 256, 128))
    row = pl.BlockSpec((ts, d), lambda i: (i, 0))
    acc = pl.BlockSpec((8, 128), lambda i: (0, 0))

    def body(y_ref, t_ref, l_ref, dy_ref, dy16_ref):
        e = y_ref[...] - t_ref[...]
        dy = e * (1.0 / d)
        dy_ref[...] = dy
        dy16_ref[...] = dy.astype(dy16_ref.dtype)

        @pl.when(pl.program_id(0) == 0)
        def _():
            l_ref[...] = jnp.zeros_like(l_ref)

        sq = jnp.sum(jnp.sum(e * e, axis=-1, keepdims=True), axis=0, keepdims=True)
        l_ref[...] += jnp.broadcast_to(sq, l_ref.shape)

    return pl.pallas_call(
        body,
        name=name,
        grid=(s // ts,),
        in_specs=[row, row],
        out_specs=(acc, row, row),
        out_shape=(jax.ShapeDtypeStruct((8, 128), F32), jax.ShapeDtypeStruct((s, d), F32),
                   jax.ShapeDtypeStruct((s, d), BF16)),
        compiler_params=_params(("arbitrary",)),
    )(y, target)


def _ffn_tiles(s, f):
    return _tile(s, (512, 256, 128)), _tile(f, _half_if_tiled(f) + (512, 256, 128))


def _conv3(xs, w, b):
    return b + xs[0] * w[0:1] + xs[1] * w[1:2] + xs[2] * w[2:3]


def _shifted3(x, rows):
    return [x[S_HALO - 2 + k:S_HALO - 2 + k + rows] for k in range(3)]


def _ffn_act_fwd(ug, uv, wg, wv, bg, bv, *, name):
    s, f = ug.shape
    ts, tc = _ffn_tiles(s, f)
    rc = _tile(ts, (FFN_CHUNK_ROWS, 32, 16, 8))
    hb = ts // S_HALO
    main = pl.BlockSpec((ts, tc), lambda j, i: (i, j))
    prev = pl.BlockSpec((S_HALO, tc), lambda j, i: (jnp.maximum(i * hb - 1, 0), j))
    taps = pl.BlockSpec((3, tc), lambda j, i: (0, j))
    bias = pl.BlockSpec((1, tc), lambda j, i: (0, j))

    def body(ug_ref, ugp_ref, uv_ref, uvp_ref, wg_ref, wv_ref, bg_ref, bv_ref, o_ref, gbuf, vbuf):
        first = pl.program_id(1) == 0
        for buf, p_ref, m_ref in ((gbuf, ugp_ref, ug_ref), (vbuf, uvp_ref, uv_ref)):
            buf[0:S_HALO, :] = jnp.where(first, 0.0, p_ref[...])
            buf[S_HALO:, :] = m_ref[...]

        def chunk(ci, carry):
            r0 = pl.multiple_of(ci * rc, rc)
            for lb in range(tc // LANES):
                ls = pl.ds(lb * LANES, LANES)
                gate = _conv3(_shifted3(gbuf[pl.ds(r0, rc + S_HALO), ls], rc), wg_ref[:, ls], bg_ref[:, ls])
                val = _conv3(_shifted3(vbuf[pl.ds(r0, rc + S_HALO), ls], rc), wv_ref[:, ls], bv_ref[:, ls])
                o_ref[pl.ds(r0, rc), ls] = (gate * _sigmoid(gate) * val).astype(o_ref.dtype)
            return carry

        lax.fori_loop(0, ts // rc, chunk, 0)

    return pl.pallas_call(
        body,
        name=name,
        grid=(f // tc, s // ts),
        in_specs=[main, prev, main, prev, taps, taps, bias, bias],
        out_specs=main,
        out_shape=jax.ShapeDtypeStruct((s, f), BF16),
        scratch_shapes=[pltpu.VMEM((S_HALO + ts, tc), F32), pltpu.VMEM((S_HALO + ts, tc), F32)],
        compiler_params=_params(("parallel", "arbitrary")),
    )(ug, ug, uv, uv, wg, wv, bg, bv)


def _ffn_act_bwd(ug, uv, df, wg, wv, bg, bv, *, name):
    s, f = ug.shape
    ts, tc = _ffn_tiles(s, f)
    rc = _tile(ts, (FFN_CHUNK_ROWS, 32, 16, 8))
    hb = ts // S_HALO
    last_hb = s // S_HALO - 1
    n_row = s // ts
    main = pl.BlockSpec((ts, tc), lambda j, i: (i, j))
    prev = pl.BlockSpec((S_HALO, tc), lambda j, i: (jnp.maximum(i * hb - 1, 0), j))
    nxt = pl.BlockSpec((S_HALO, tc), lambda j, i: (jnp.minimum((i + 1) * hb, last_hb), j))
    taps = pl.BlockSpec((3, tc), lambda j, i: (0, j))
    bias = pl.BlockSpec((1, tc), lambda j, i: (0, j))
    sums = pl.BlockSpec((32, tc), lambda j, i: (0, j))
    ext = rc + S_HALO

    def body(ug_ref, ugp_ref, ugn_ref, uv_ref, uvp_ref, uvn_ref, df_ref, dfn_ref, wg_ref, wv_ref, bg_ref, bv_ref,
             dug_ref, duv_ref, sumg_ref, sumv_ref, gbuf, vbuf, dfbuf):
        i = pl.program_id(1)
        first = i == 0
        last = i == n_row - 1

        @pl.when(first)
        def _():
            sumg_ref[...] = jnp.zeros_like(sumg_ref)
            sumv_ref[...] = jnp.zeros_like(sumv_ref)

        for buf, p_ref, m_ref, n_ref in ((gbuf, ugp_ref, ug_ref, ugn_ref), (vbuf, uvp_ref, uv_ref, uvn_ref)):
            buf[0:S_HALO, :] = jnp.where(first, 0.0, p_ref[...])
            buf[S_HALO:S_HALO + ts, :] = m_ref[...]
            buf[S_HALO + ts:, :] = n_ref[...]
        dfbuf[0:ts, :] = df_ref[...]
        dfbuf[ts:, :] = jnp.where(last, 0.0, dfn_ref[...])

        def chunk(ci, carry):
            r0 = pl.multiple_of(ci * rc, rc)
            for lb in range(tc // LANES):
                ls = pl.ds(lb * LANES, LANES)
                xg = _shifted3(gbuf[pl.ds(r0, ext + S_HALO), ls], ext)
                xv = _shifted3(vbuf[pl.ds(r0, ext + S_HALO), ls], ext)
                wgv, wvv = wg_ref[:, ls], wv_ref[:, ls]
                gate = _conv3(xg, wgv, bg_ref[:, ls])
                val = _conv3(xv, wvv, bv_ref[:, ls])
                dfe = dfbuf[pl.ds(r0, ext), ls]
                sg = _sigmoid(gate)
                dgate = dfe * val * (sg * (1.0 + gate * (1.0 - sg)))
                dval = dfe * (gate * sg)
                for dz, xs, w, du_ref, sum_ref in ((dgate, xg, wgv, dug_ref, sumg_ref),
                                                   (dval, xv, wvv, duv_ref, sumv_ref)):
                    du = dz[2:2 + rc] * w[0:1] + dz[1:1 + rc] * w[1:2] + dz[0:rc] * w[2:3]
                    du_ref[pl.ds(r0, rc), ls] = du.astype(du_ref.dtype)
                    dm = dz[0:rc]
                    for k in range(3):
                        sum_ref[8 * k:8 * k + 8, ls] += (dm * xs[k][0:rc]).reshape(rc // 8, 8, LANES).sum(axis=0)
                    sum_ref[24:32, ls] += dm.reshape(rc // 8, 8, LANES).sum(axis=0)
            return carry

        lax.fori_loop(0, ts // rc, chunk, 0)

    return pl.pallas_call(
        body,
        name=name,
        grid=(f // tc, n_row),
        in_specs=[main, prev, nxt, main, prev, nxt, main, nxt, taps, taps, bias, bias],
        out_specs=(main, main, sums, sums),
        out_shape=(jax.ShapeDtypeStruct((s, f), BF16), jax.ShapeDtypeStruct((s, f), BF16),
                   jax.ShapeDtypeStruct((32, f), F32), jax.ShapeDtypeStruct((32, f), F32)),
        scratch_shapes=[pltpu.VMEM((2 * S_HALO + ts, tc), F32), pltpu.VMEM((2 * S_HALO + ts, tc), F32),
                        pltpu.VMEM((S_HALO + ts, tc), F32)],
        compiler_params=_params(("parallel", "arbitrary")),
    )(ug, ug, ug, uv, uv, uv, df, df, wg, wv, bg, bv)


def _layernorm_stats(a1):
    mu = jnp.mean(a1, axis=-1, keepdims=True)
    xc = a1 - mu
    rstd = lax.rsqrt(jnp.mean(xc * xc, axis=-1, keepdims=True) + EPS)
    return xc * rstd, rstd


def _conv_act_fwd(p, wa, ba, lng, lnb, wb, *, name):
    s, c5 = p.shape
    c = c5 // 5
    ka, kb = wa.shape[0], wb.shape[0]
    ts = _tile(s, (256, 128))
    hb = ts // A_HALO
    main = pl.BlockSpec((ts, c5), lambda i: (i, 0))
    prev = pl.BlockSpec((A_HALO, c5), lambda i: (jnp.maximum(i * hb - 1, 0), 0))

    def full(arr):
        return pl.BlockSpec(arr.shape, lambda i: (0, 0))

    def body(p_ref, pp_ref, wa_ref, ba_ref, lng_ref, lnb_ref, wb_ref, ab_ref, a1_ref, gbuf, cbuf):
        first = pl.program_id(0) == 0
        gbuf[0:A_HALO, :] = jnp.where(first, 0.0, pp_ref[:, 0:c] * _sigmoid(pp_ref[:, c:2 * c]))
        gbuf[A_HALO:, :] = p_ref[:, 0:c] * _sigmoid(p_ref[:, c:2 * c])
        cbuf[0:A_HALO, :] = jnp.where(first, 0.0, pp_ref[:, 3 * c:4 * c] * pp_ref[:, 4 * c:5 * c])
        cbuf[A_HALO:, :] = p_ref[:, 3 * c:4 * c] * p_ref[:, 4 * c:5 * c]

        a1 = jnp.broadcast_to(ba_ref[...], (ts, c))
        for k in range(ka):
            off = A_HALO - (ka - 1) + k
            a1 = a1 + gbuf[off:off + ts, :] * wa_ref[k:k + 1, :]
        a1_ref[...] = a1
        xhat, _ = _layernorm_stats(a1)
        a2 = xhat * lng_ref[...] + lnb_ref[...]
        ab_ref[:, 0:c] = (a2 * _sigmoid(a2)).astype(ab_ref.dtype)

        cv = jnp.zeros((ts, c), F32)
        for k in range(kb):
            off = A_HALO - (kb - 1) + k
            cv = cv + cbuf[off:off + ts, :] * wb_ref[k:k + 1, :]
        ab_ref[:, c:2 * c] = (p_ref[:, 2 * c:3 * c] * cv).astype(ab_ref.dtype)

    return pl.pallas_call(
        body,
        name=name,
        grid=(s // ts,),
        in_specs=[main, prev, full(wa), full(ba), full(lng), full(lnb), full(wb)],
        out_specs=(pl.BlockSpec((ts, 2 * c), lambda i: (i, 0)), pl.BlockSpec((ts, c), lambda i: (i, 0))),
        out_shape=(jax.ShapeDtypeStruct((s, 2 * c), BF16), jax.ShapeDtypeStruct((s, c), F32)),
        scratch_shapes=[pltpu.VMEM((A_HALO + ts, c), F32), pltpu.VMEM((A_HALO + ts, c), F32)],
        compiler_params=_params(("parallel",)),
    )(p, p, wa, ba, lng, lnb, wb)


def _conv_act_bwd(p, a1, dab, wa, lng, lnb, wb, *, name):
    s, c5 = p.shape
    c = c5 // 5
    ka, kb = wa.shape[0], wb.shape[0]
    ts = _tile(s, (256, 128))
    hb = ts // A_HALO
    n_row = s // ts
    last_hb = s // A_HALO - 1
    ext = ts + A_HALO

    def rows(width):
        return (pl.BlockSpec((ts, width), lambda i: (i, 0)),
                pl.BlockSpec((A_HALO, width), lambda i: (jnp.maximum(i * hb - 1, 0), 0)),
                pl.BlockSpec((A_HALO, width), lambda i: (jnp.minimum((i + 1) * hb, last_hb), 0)))

    p_main, p_prev, p_next = rows(c5)
    d_main, _, d_next = rows(2 * c)
    a_main, _, a_next = rows(c)

    def full(shape):
        return pl.BlockSpec(shape, lambda i: (0, 0))

    def body(p_ref, pp_ref, pn_ref, a1_ref, a1n_ref, d_ref, dn_ref, wa_ref, lng_ref, lnb_ref, wb_ref,
             dp_ref, dwa_ref, dba_ref, dlng_ref, dlnb_ref, dwb_ref, gbuf, cbuf, da1buf, dcvbuf):
        i = pl.program_id(0)
        first = i == 0
        last = i == n_row - 1

        @pl.when(first)
        def _():
            dwa_ref[...] = jnp.zeros_like(dwa_ref)
            dba_ref[...] = jnp.zeros_like(dba_ref)
            dlng_ref[...] = jnp.zeros_like(dlng_ref)
            dlnb_ref[...] = jnp.zeros_like(dlnb_ref)
            dwb_ref[...] = jnp.zeros_like(dwb_ref)

        sig_gate = _sigmoid(p_ref[:, c:2 * c])
        gbuf[0:A_HALO, :] = jnp.where(first, 0.0, pp_ref[:, 0:c] * _sigmoid(pp_ref[:, c:2 * c]))
        gbuf[A_HALO:, :] = p_ref[:, 0:c] * sig_gate
        cbuf[0:A_HALO, :] = jnp.where(first, 0.0, pp_ref[:, 3 * c:4 * c] * pp_ref[:, 4 * c:5 * c])
        cbuf[A_HALO:, :] = p_ref[:, 3 * c:4 * c] * p_ref[:, 4 * c:5 * c]

        def ln_back(a1v, dav):
            xhat, rstd = _layernorm_stats(a1v)
            a2 = xhat * lng_ref[...] + lnb_ref[...]
            sg = _sigmoid(a2)
            da2 = dav * (sg * (1.0 + a2 * (1.0 - sg)))
            dxh = da2 * lng_ref[...]
            da1 = rstd * (dxh - jnp.mean(dxh, axis=-1, keepdims=True)
                          - xhat * jnp.mean(dxh * xhat, axis=-1, keepdims=True))
            return da1, da2, xhat

        da1_m, da2_m, xhat_m = ln_back(a1_ref[...], d_ref[:, 0:c])
        da1_n, _, _ = ln_back(a1n_ref[...], dn_ref[:, 0:c])
        da1buf[0:ts, :] = da1_m
        da1buf[ts:, :] = jnp.where(last, 0.0, da1_n)
        dlng_ref[...] += jnp.sum(da2_m * xhat_m, axis=0, keepdims=True)
        dlnb_ref[...] += jnp.sum(da2_m, axis=0, keepdims=True)
        dba_ref[...] += jnp.sum(da1_m, axis=0, keepdims=True)

        dglu = jnp.zeros((ts, c), F32)
        for k in range(ka):
            off = (ka - 1) - k
            dglu = dglu + da1buf[off:off + ts, :] * wa_ref[k:k + 1, :]
            goff = A_HALO - (ka - 1) + k
            dwa_ref[k:k + 1, :] += jnp.sum(da1_m * gbuf[goff:goff + ts, :], axis=0, keepdims=True)
        dp_ref[:, 0:c] = (dglu * sig_gate).astype(dp_ref.dtype)
        dp_ref[:, c:2 * c] = (dglu * p_ref[:, 0:c] * sig_gate * (1.0 - sig_gate)).astype(dp_ref.dtype)

        cv = jnp.zeros((ts, c), F32)
        for k in range(kb):
            off = A_HALO - (kb - 1) + k
            cv = cv + cbuf[off:off + ts, :] * wb_ref[k:k + 1, :]
        db_m = d_ref[:, c:2 * c]
        dp_ref[:, 2 * c:3 * c] = (db_m * cv).astype(dp_ref.dtype)
        dcv_m = db_m * p_ref[:, 2 * c:3 * c]
        dcvbuf[0:ts, :] = dcv_m
        dcvbuf[ts:, :] = jnp.where(last, 0.0, dn_ref[:, c:2 * c] * pn_ref[:, 2 * c:3 * c])
        dbc = jnp.zeros((ts, c), F32)
        for k in range(kb):
            off = (kb - 1) - k
            dbc = dbc + dcvbuf[off:off + ts, :] * wb_ref[k:k + 1, :]
            coff = A_HALO - (kb - 1) + k
            dwb_ref[k:k + 1, :] += jnp.sum(dcv_m * cbuf[coff:coff + ts, :], axis=0, keepdims=True)
        dp_ref[:, 3 * c:4 * c] = (dbc * p_ref[:, 4 * c:5 * c]).astype(dp_ref.dtype)
        dp_ref[:, 4 * c:5 * c] = (dbc * p_ref[:, 3 * c:4 * c]).astype(dp_ref.dtype)

    return pl.pallas_call(
        body,
        name=name,
        grid=(n_row,),
        in_specs=[p_main, p_prev, p_next, a_main, a_next, d_main, d_next,
                  full(wa.shape), full(lng.shape), full(lnb.shape), full(wb.shape)],
        out_specs=(pl.BlockSpec((ts, c5), lambda i: (i, 0)), full((32, c)), full((1, c)), full((1, c)),
                   full((1, c)), full((8, c))),
        out_shape=(
            jax.ShapeDtypeStruct((s, c5), BF16), jax.ShapeDtypeStruct((32, c), F32),
            jax.ShapeDtypeStruct((1, c), F32), jax.ShapeDtypeStruct((1, c), F32),
            jax.ShapeDtypeStruct((1, c), F32), jax.ShapeDtypeStruct((8, c), F32),
        ),
        scratch_shapes=[
            pltpu.VMEM((A_HALO + ts, c), F32), pltpu.VMEM((A_HALO + ts, c), F32),
            pltpu.VMEM((ext, c), F32), pltpu.VMEM((ext, c), F32),
        ],
        compiler_params=_params(("arbitrary",)),
    )(p, p, p, a1, a1, dab, dab, wa, lng, lnb, wb)


def _head_pair_geometry(width, dh):
    assert 2 * dh == LANES, "the attention kernels pair two heads in one 128-lane block"
    return width // (3 * LANES)


def _group_sum(v, lo):
    s_lo = jnp.sum(jnp.where(lo, v, 0.0), axis=-1, keepdims=True)
    s_hi = jnp.sum(jnp.where(lo, 0.0, v), axis=-1, keepdims=True)
    return jnp.where(lo, s_lo, s_hi)


def _qkv_prep_fwd(qkv, gains, dh, *, name):
    s, width = qkv.shape
    nsec = _head_pair_geometry(width, dh)
    tr = _tile(s, (256, 128))
    blk = pl.BlockSpec((tr, nsec * LANES), lambda i, sec: (i, sec))

    def body(x_ref, g_ref, o_ref):
        sec = pl.program_id(1)

        @pl.when(sec == 2)
        def _():
            o_ref[...] = x_ref[...].astype(o_ref.dtype)

        @pl.when(sec != 2)
        def _():
            lo = lax.broadcasted_iota(jnp.int32, (1, LANES), 1) < dh
            for lb in range(nsec):
                ls = pl.ds(lb * LANES, LANES)
                xv = x_ref[:, ls]
                rs = lax.rsqrt(_group_sum(xv * xv, lo) * (1.0 / dh) + EPS)
                o_ref[:, ls] = (xv * rs * g_ref[0:1, :]).astype(o_ref.dtype)

    return pl.pallas_call(
        body,
        name=name,
        grid=(s // tr, 3),
        in_specs=[blk, pl.BlockSpec((8, LANES), lambda i, sec: (sec, 0))],
        out_specs=blk,
        out_shape=jax.ShapeDtypeStruct((s, width), BF16),
        compiler_params=_params(("parallel", "parallel")),
    )(qkv, gains)


def _qkv_prep_bwd(qkv, dq, dk, dv, gains, dh, *, name):
    s, width = qkv.shape
    nsec = _head_pair_geometry(width, dh)
    tr = _tile(s, (256, 128))
    blk = pl.BlockSpec((tr, nsec * LANES), lambda sec, i: (i, sec))
    gspec = pl.BlockSpec((8, LANES), lambda sec, i: (sec, 0))

    def grad_spec(own):
        return pl.BlockSpec((tr, nsec * LANES), lambda sec, i: (jnp.where(sec == own, i, 0), 0))

    def body(x_ref, dq_ref, dk_ref, dv_ref, g_ref, o_ref, dg_ref):
        sec = pl.program_id(0)

        @pl.when(pl.program_id(1) == 0)
        def _():
            dg_ref[...] = jnp.zeros_like(dg_ref)

        def normed(d_ref):
            lo = lax.broadcasted_iota(jnp.int32, (1, LANES), 1) < dh
            dg = jnp.zeros((1, LANES), F32)
            for lb in range(nsec):
                ls = pl.ds(lb * LANES, LANES)
                xv = x_ref[:, ls]
                d = d_ref[:, ls]
                rs = lax.rsqrt(_group_sum(xv * xv, lo) * (1.0 / dh) + EPS)
                xn = xv * rs
                dyg = d * g_ref[0:1, :]
                dx = rs * (dyg - xn * (_group_sum(dyg * xn, lo) * (1.0 / dh)))
                o_ref[:, ls] = dx.astype(o_ref.dtype)
                dg = dg + jnp.sum(d * xn, axis=0, keepdims=True)
            dg_ref[...] += jnp.broadcast_to(dg, dg_ref.shape)

        @pl.when(sec == 0)
        def _():
            normed(dq_ref)

        @pl.when(sec == 1)
        def _():
            normed(dk_ref)

        @pl.when(sec == 2)
        def _():
            o_ref[...] = dv_ref[...].astype(o_ref.dtype)

    return pl.pallas_call(
        body,
        name=name,
        grid=(3, s // tr),
        in_specs=[blk, grad_spec(0), grad_spec(1), grad_spec(2), gspec],
        out_specs=(blk, gspec),
        out_shape=(jax.ShapeDtypeStruct((s, width), BF16), jax.ShapeDtypeStruct((24, LANES), F32)),
        compiler_params=_params(("arbitrary", "arbitrary")),
    )(qkv, dq, dk, dv, gains)


def _split_dot(v, tri):
    hi = v.astype(BF16)
    lo = (v - hi.astype(F32)).astype(BF16)
    return _dot(hi, tri, 1, 0) + _dot(lo, tri, 1, 0)


def _attn_scores(qm, kb, tri_gt, valid, r_run):
    z = _dot(qm, kb, 1, 1)
    zl = jnp.minimum(z, 0.0) - jnp.log(1.0 + jnp.exp(-jnp.abs(z)))
    lk = zl - z
    if valid is not None:
        lk = jnp.where(valid, lk, 0.0)
    after = _split_dot(lk, tri_gt)
    w = jnp.exp(zl + after + r_run)
    if valid is not None:
        w = jnp.where(valid, w, 0.0)
    return zl, lk, w


def _attn_alive(rr):
    return jnp.max(jnp.maximum(rr[0], rr[1])) > ATTN_DEAD_LOG_WEIGHT


def _attn_specs(s, width, dh, tq):
    nsec = _head_pair_geometry(width, dh)
    qspec = pl.BlockSpec((tq, LANES), lambda h, i: (i, h))
    kspec = pl.BlockSpec((s, LANES), lambda h, i: (0, nsec + h))
    vspec = pl.BlockSpec((s, LANES), lambda h, i: (0, 2 * nsec + h))
    return nsec, qspec, kspec, vspec


def _attn_masks(tq, tk, dh):
    lo = lax.broadcasted_iota(jnp.int32, (1, LANES), 1) < dh
    r_io = lax.broadcasted_iota(jnp.int32, (tk, tk), 0)
    c_io = lax.broadcasted_iota(jnp.int32, (tk, tk), 1)
    qrow = lax.broadcasted_iota(jnp.int32, (tq, tk), 0)
    kcol = lax.broadcasted_iota(jnp.int32, (tq, tk), 1)
    return lo, r_io, c_io, qrow, kcol


def _attn_fwd(qkv16, dh, *, name):
    s, width = qkv16.shape
    tq = _tile(s, (ATTN_Q_BLOCK, ATTN_K_BLOCK))
    tk = _tile(tq, (ATTN_K_BLOCK,))
    nd = tq // tk
    nsec, qspec, kspec, vspec = _attn_specs(s, width, dh, tq)

    def body(q_ref, k_ref, v_ref, o_ref):
        i = pl.program_id(1)
        lo, r_io, c_io, qrow, kcol = _attn_masks(tq, tk, dh)
        q2 = q_ref[...]
        zq = jnp.zeros_like(q2)
        qs = (jnp.where(lo, q2, zq), jnp.where(lo, zq, q2))
        tri_gt = (r_io > c_io).astype(BF16)

        def block(j, carry, diag):
            rr, acc = carry
            start = pl.multiple_of(j * tk, tk)
            kb = k_ref[pl.ds(start, tk), :]
            vb = v_ref[pl.ds(start, tk), :]
            valid = None if diag is None else kcol + diag * tk < qrow
            outs, new_r = [], []
            for hh in range(2):
                _, lk, w = _attn_scores(qs[hh], kb, tri_gt, valid, rr[hh])
                outs.append(_dot(w.astype(BF16), vb, 1, 0))
                new_r.append(rr[hh] + jnp.sum(lk, axis=-1, keepdims=True))
            return tuple(new_r), acc + jnp.where(lo, outs[0], outs[1])

        zero = jnp.zeros((tq, 1), F32)
        carry = ((zero, zero), jnp.zeros((tq, LANES), F32))
        for dg in reversed(range(nd)):
            carry = block(i * nd + dg, carry, dg)
        below = i * nd

        def step(st):
            jj, _, cr = st
            cr = block(below - 1 - jj, cr, None)
            return jj + 1, _attn_alive(cr[0]), cr

        _, _, carry = lax.while_loop(lambda st: (st[0] < below) & st[1], step,
                                     (jnp.int32(0), _attn_alive(carry[0]), carry))
        o_ref[...] = carry[1]

    return pl.pallas_call(
        body,
        name=name,
        grid=(nsec, s // tq),
        in_specs=[qspec, kspec, vspec],
        out_specs=qspec,
        out_shape=jax.ShapeDtypeStruct((s, width // 3), F32),
        compiler_params=_params(("parallel", "arbitrary")),
    )(qkv16, qkv16, qkv16)


def _attn_bwd(qkv16, o, do, dh, *, name):
    s, width = qkv16.shape
    tq = _tile(s, (ATTN_Q_BLOCK, ATTN_K_BLOCK))
    tk = _tile(tq, (ATTN_K_BLOCK,))
    nd = tq // tk
    nsec, qspec, kspec, vspec = _attn_specs(s, width, dh, tq)
    accspec = pl.BlockSpec((s, LANES), lambda h, i: (0, h))

    def body(q_ref, k_ref, v_ref, o_ref, do_ref, dq_ref, dk_ref, dv_ref):
        i = pl.program_id(1)

        @pl.when(i == 0)
        def _():
            dk_ref[...] = jnp.zeros_like(dk_ref)
            dv_ref[...] = jnp.zeros_like(dv_ref)

        lo, r_io, c_io, qrow, kcol = _attn_masks(tq, tk, dh)
        q2 = q_ref[...]
        zq = jnp.zeros_like(q2)
        qs = (jnp.where(lo, q2, zq), jnp.where(lo, zq, q2))
        do16 = do_ref[...].astype(BF16)
        dos = (jnp.where(lo, do16, zq), jnp.where(lo, zq, do16))
        prod = do16.astype(F32) * o_ref[...]
        deltas = (jnp.sum(jnp.where(lo, prod, 0.0), axis=-1, keepdims=True),
                  jnp.sum(jnp.where(lo, 0.0, prod), axis=-1, keepdims=True))
        tri_gt = (r_io > c_io).astype(BF16)
        tri_ge = (r_io >= c_io).astype(BF16)

        def block(j, carry, diag):
            rr, er, dq = carry
            start = pl.multiple_of(j * tk, tk)
            kb = k_ref[pl.ds(start, tk), :]
            vb = v_ref[pl.ds(start, tk), :]
            valid = None if diag is None else kcol + diag * tk < qrow
            new_r, new_e, dqs = [], [], []
            dk_blk = dv_blk = None
            for hh in range(2):
                zl, lk, w = _attn_scores(qs[hh], kb, tri_gt, valid, rr[hh])
                beta = jnp.exp(zl)
                w16 = w.astype(BF16)
                e = _dot(dos[hh], vb, 1, 1) * w16.astype(F32)
                e_before = deltas[hh] - er[hh] - _split_dot(e, tri_ge)
                dz = e - beta * (e + e_before)
                if valid is not None:
                    dz = jnp.where(valid, dz, 0.0)
                dz16 = dz.astype(BF16)
                dqs.append(_dot(dz16, kb, 1, 0))
                dkc = _dot(dz16, qs[hh], 0, 0)
                dvc = _dot(w16, dos[hh], 0, 0)
                dk_blk = dkc if dk_blk is None else dk_blk + dkc
                dv_blk = dvc if dv_blk is None else dv_blk + dvc
                new_r.append(rr[hh] + jnp.sum(lk, axis=-1, keepdims=True))
                new_e.append(er[hh] + jnp.sum(e, axis=-1, keepdims=True))
            dk_ref[pl.ds(start, tk), :] += dk_blk
            dv_ref[pl.ds(start, tk), :] += dv_blk
            return tuple(new_r), tuple(new_e), dq + jnp.where(lo, dqs[0], dqs[1])

        zero = jnp.zeros((tq, 1), F32)
        carry = ((zero, zero), (zero, zero), jnp.zeros((tq, LANES), F32))
        for dg in reversed(range(nd)):
            carry = block(i * nd + dg, carry, dg)
        below = i * nd

        def step(st):
            jj, _, cr = st
            cr = block(below - 1 - jj, cr, None)
            return jj + 1, _attn_alive(cr[0]), cr

        _, _, carry = lax.while_loop(lambda st: (st[0] < below) & st[1], step,
                                     (jnp.int32(0), _attn_alive(carry[0]), carry))
        dq_ref[...] = carry[2]

    shp = jax.ShapeDtypeStruct((s, width // 3), F32)
    return pl.pallas_call(
        body,
        name=name,
        grid=(nsec, s // tq),
        in_specs=[qspec, kspec, vspec, qspec, qspec],
        out_specs=(qspec, accspec, accspec),
        out_shape=(shp, shp, shp),
        compiler_params=pltpu.CompilerParams(dimension_semantics=("parallel", "arbitrary"),
                                             vmem_limit_bytes=V7X_ATTN_BWD_VMEM_LIMIT_BYTES),
    )(qkv16, qkv16, qkv16, o, do)


def _adam_update(wv, gv, mv, vv):
    c1 = 1.0 - ADAM_B1 ** ADAM_STEP
    c2 = 1.0 - ADAM_B2 ** ADAM_STEP
    nm = ADAM_B1 * mv + (1.0 - ADAM_B1) * gv
    nv = ADAM_B2 * vv + (1.0 - ADAM_B2) * (gv * gv)
    delta = -ADAM_LR * ((nm / c1) / (jnp.sqrt(nv / c2) + ADAM_EPS) + ADAM_WD * wv)
    return delta, nm, nv


def _sum_slabs(r, *, name):
    n, rows, lanes = r.shape
    tr = _tile(rows, (1024, 512, 256, 128, 64, 32, 16))

    def body(r_ref, o_ref):
        acc = r_ref[0]
        for d in range(1, n):
            acc = acc + r_ref[d]
        o_ref[...] = acc

    return pl.pallas_call(
        body,
        name=name,
        grid=(rows // tr,),
        in_specs=[pl.BlockSpec((n, tr, lanes), lambda i: (0, i, 0))],
        out_specs=pl.BlockSpec((tr, lanes), lambda i: (i, 0)),
        out_shape=jax.ShapeDtypeStruct((rows, lanes), F32),
        compiler_params=_params(("parallel",)),
    )(r)


def _adamw(w, g, m, v, *, name):
    rows, cols = w.shape
    tr = _tile(rows, (1024, 512, 256, 128, 64, 32, 16))
    spec = pl.BlockSpec((tr, cols), lambda i: (i, 0))

    def body(w_ref, g_ref, m_ref, v_ref, d_ref, nm_ref, nv_ref):
        d_ref[...], nm_ref[...], nv_ref[...] = _adam_update(w_ref[...], g_ref[...], m_ref[...], v_ref[...])

    shp = jax.ShapeDtypeStruct((rows, cols), F32)
    return pl.pallas_call(
        body,
        name=name,
        grid=(rows // tr,),
        in_specs=[spec, spec, spec, spec],
        out_specs=(spec, spec, spec),
        out_shape=(shp, shp, shp),
        compiler_params=_params(("parallel",)),
    )(w, g, m, v)


def _adamw_sum(w, r, m, v, *, name):
    n, rows, cols = r.shape
    tr = _tile(rows, (256, 128, 64, 32, 16, 8))
    spec = pl.BlockSpec((tr, cols), lambda i: (i, 0))

    def body(w_ref, r_ref, m_ref, v_ref, g_ref, d_ref, nm_ref, nv_ref):
        gv = r_ref[0].astype(F32)
        for d in range(1, n):
            gv = gv + r_ref[d].astype(F32)
        g_ref[...] = gv
        d_ref[...], nm_ref[...], nv_ref[...] = _adam_update(w_ref[...], gv, m_ref[...], v_ref[...])

    shp = jax.ShapeDtypeStruct((rows, cols), F32)
    return pl.pallas_call(
        body,
        name=name,
        grid=(rows // tr,),
        in_specs=[spec, pl.BlockSpec((n, tr, cols), lambda i: (0, i, 0)), spec, spec],
        out_specs=(spec, spec, spec, spec),
        out_shape=(shp, shp, shp, shp),
        compiler_params=_params(("parallel",)),
    )(w, r, m, v)


_MATMUL_WEIGHTS = (("conv_w_in", 2), ("conv_w_out", 1), ("attn_w_qkv", 2), ("attn_w_o", 1), ("ffn_w_up", 2),
                   ("ffn_w_down", 1))
_TAP_WEIGHTS = (("conv_a_dw_w", 2), ("conv_b_dw_w", 2), ("ffn_dw_w", 2))
_REPLICATED = ("mix_norm_g", "ffn_norm_g", "conv_a_dw_b", "conv_a_ln_g", "conv_a_ln_b", "attn_q_g", "attn_k_g",
               "ffn_dw_b")
_WEIGHT_ORDER = ("mix_norm_g", "ffn_norm_g", "conv_w_in", "conv_a_dw_w", "conv_a_dw_b", "conv_a_ln_g",
                 "conv_a_ln_b", "conv_b_dw_w", "conv_w_out", "attn_w_qkv", "attn_q_g", "attn_k_g", "attn_w_o",
                 "ffn_w_up", "ffn_dw_w", "ffn_dw_b", "ffn_w_down")


def _assemble(gathered, axis):
    n, l, a, b = gathered.shape
    if axis == 2:
        return jnp.transpose(gathered, (1, 2, 0, 3)).reshape(l, a, n * b)
    return jnp.transpose(gathered, (1, 0, 2, 3)).reshape(l, n * a, b)


def _split_shards(full, axis):
    l, a, b = full.shape
    if axis == 2:
        return jnp.transpose(full.reshape(l, a, N_DEV, b // N_DEV), (2, 0, 1, 3))
    return jnp.transpose(full.reshape(l, N_DEV, a // N_DEV, b), (1, 0, 2, 3))


def kernel(x, mix_norm_g, ffn_norm_g, conv_w_in, conv_a_dw_w, conv_a_dw_b, conv_a_ln_g, conv_a_ln_b, conv_b_dw_w, conv_w_out, attn_w_qkv, attn_q_g, attn_k_g, attn_w_o, ffn_w_up, ffn_dw_w, ffn_dw_b, ffn_w_down, loss_target, m_mix_norm_g, m_ffn_norm_g, m_conv_w_in, m_conv_a_dw_w, m_conv_a_dw_b, m_conv_a_ln_g, m_conv_a_ln_b, m_conv_b_dw_w, m_conv_w_out, m_attn_w_qkv, m_attn_q_g, m_attn_k_g, m_attn_w_o, m_ffn_w_up, m_ffn_dw_w, m_ffn_dw_b, m_ffn_w_down, v_mix_norm_g, v_ffn_norm_g, v_conv_w_in, v_conv_a_dw_w, v_conv_a_dw_b, v_conv_a_ln_g, v_conv_a_ln_b, v_conv_b_dw_w, v_conv_w_out, v_attn_w_qkv, v_attn_q_g, v_attn_k_g, v_attn_w_o, v_ffn_w_up, v_ffn_dw_w, v_ffn_dw_b, v_ffn_w_down):
    args = dict(locals())
    weights = {n: args[n] for n in _WEIGHT_ORDER}
    mom_m = {n: args["m_" + n] for n in _WEIGHT_ORDER}
    mom_v = {n: args["v_" + n] for n in _WEIGHT_ORDER}

    _, s, d = x.shape
    depth = mix_norm_g.shape[0]
    dh = attn_q_g.shape[1]
    x0 = x.reshape(s, d)
    target = loss_target.reshape(s, d)

    gathered = _all_gather([weights[n].astype(BF16) for n, _ in _MATMUL_WEIGHTS], "gather_matmul_weights")
    full = {}
    for (n, axis), g in zip(_MATMUL_WEIGHTS, gathered):
        _, l, a, b = g.shape
        if axis == 2:
            full[n] = _assemble_cols(g.reshape(N_DEV, l * a, b), name=f"assemble_{n}").reshape(l, a, N_DEV * b)
        else:
            full[n] = jnp.transpose(g, (1, 0, 2, 3)).reshape(l, N_DEV * a, b)
    tap_shapes = [weights[n].shape for n, _ in _TAP_WEIGHTS]
    g_tap = _all_gather([_pack([weights[n] for n, _ in _TAP_WEIGHTS], F32)], "gather_tap_weights")[0]
    for (n, axis), part in zip(_TAP_WEIGHTS, _unpack(g_tap, tap_shapes, lead=(N_DEV,))):
        full[n] = _assemble(part, axis)
    f = full["ffn_w_down"].shape[1]

    saved = []
    xs = x0
    for layer in range(depth):
        i = layer // 2
        sv = {"x_in": xs}
        hn = _rmsnorm_fwd(xs, mix_norm_g[layer:layer + 1], name=f"mix_norm_fwd_{layer}")
        sv["h"] = hn
        if layer % 2 == 0:
            p = _matmul(hn, full["conv_w_in"][i], name=f"conv_in_{layer}")
            ab, a1 = _conv_act_fwd(p, full["conv_a_dw_w"][i], conv_a_dw_b[i:i + 1], conv_a_ln_g[i:i + 1],
                                   conv_a_ln_b[i:i + 1], full["conv_b_dw_w"][i], name=f"conv_act_fwd_{layer}")
            sv.update(p=p, a1=a1, mix=ab)
            xs = _matmul(ab, full["conv_w_out"][i], add=xs, name=f"conv_out_{layer}")
        else:
            reps = LANES // dh
            gains = jnp.concatenate([
                jnp.broadcast_to(jnp.tile(attn_q_g[i], reps) * dh ** -0.5, (8, LANES)),
                jnp.broadcast_to(jnp.tile(attn_k_g[i], reps), (8, LANES)),
                jnp.ones((8, LANES), F32)], axis=0)
            qkv = _matmul(hn, full["attn_w_qkv"][i], name=f"attn_qkv_{layer}")
            qkv16 = _qkv_prep_fwd(qkv, gains, dh, name=f"qkv_prep_fwd_{layer}")
            o = _attn_fwd(qkv16, dh, name=f"attn_fwd_{layer}")
            sv.update(qkv=qkv, qkv16=qkv16, gains=gains, mix=o)
            xs = _matmul(o, full["attn_w_o"][i], add=xs, name=f"attn_out_{layer}")
        sv["x_mid"] = xs
        h2 = _rmsnorm_fwd(xs, ffn_norm_g[layer:layer + 1], name=f"ffn_norm_fwd_{layer}")
        w_up = full["ffn_w_up"][layer]
        ug = _matmul(h2, w_up[:, :f], name=f"ffn_up_gate_{layer}")
        uv = _matmul(h2, w_up[:, f:], name=f"ffn_up_val_{layer}")
        taps = full["ffn_dw_w"][layer]
        bias = ffn_dw_b[layer:layer + 1]
        act = _ffn_act_fwd(ug, uv, taps[:, :f], taps[:, f:], bias[:, :f], bias[:, f:], name=f"ffn_act_fwd_{layer}")
        sv.update(h2=h2, ug=ug, uv=uv, act=act)
        xs = _matmul(act, full["ffn_w_down"][layer], add=xs, name=f"ffn_down_{layer}")
        saved.append(sv)

    sq, dx, dx16 = _loss_head(xs, target, name="loss_head")
    loss = lax.psum(0.5 * sq[0, 0] / d, ("x", "y", "c"))

    grads = {n: [None] * weights[n].shape[0] for n in _WEIGHT_ORDER}
    for layer in reversed(range(depth)):
        i = layer // 2
        sv = saved[layer]
        w_up = full["ffn_w_up"][layer]
        taps = full["ffn_dw_w"][layer]
        bias = ffn_dw_b[layer:layer + 1]
        dact = _matmul(dx16, full["ffn_w_down"][layer], tb=True, name=f"ffn_down_dx_{layer}")
        grads["ffn_w_down"][layer] = _weight_grad(sv["act"], dx16, name=f"ffn_down_dw_{layer}")
        dug, duv, sumg, sumv = _ffn_act_bwd(sv["ug"], sv["uv"], dact, taps[:, :f], taps[:, f:], bias[:, :f],
                                            bias[:, f:], name=f"ffn_act_bwd_{layer}")
        sums = jnp.concatenate([sumg, sumv], axis=1).reshape(4, 8, 2 * f).sum(axis=1)
        grads["ffn_dw_w"][layer] = sums[:3]
        grads["ffn_dw_b"][layer] = sums[3]
        dh2 = _matmul(dug, w_up[:, :f], tb=True, name=f"ffn_up_gate_dx_{layer}")
        dh2 = _matmul(duv, w_up[:, f:], tb=True, add=dh2, name=f"ffn_up_val_dx_{layer}")
        grads["ffn_w_up"][layer] = jnp.concatenate(
            [_weight_grad(sv["h2"], dug, name=f"ffn_up_gate_dw_{layer}"),
             _weight_grad(sv["h2"], duv, name=f"ffn_up_val_dw_{layer}")], axis=1)
        dx, dx16, dg = _rmsnorm_bwd(sv["x_mid"], ffn_norm_g[layer:layer + 1], dh2, dx, name=f"ffn_norm_bwd_{layer}")
        grads["ffn_norm_g"][layer] = dg[0]

        if layer % 2 == 0:
            dab = _matmul(dx16, full["conv_w_out"][i], tb=True, name=f"conv_out_dx_{layer}")
            grads["conv_w_out"][i] = _weight_grad(sv["mix"], dx16, name=f"conv_out_dw_{layer}")
            ka = full["conv_a_dw_w"].shape[1]
            kb = full["conv_b_dw_w"].shape[1]
            dp, dwa, dba, dlng, dlnb, dwb = _conv_act_bwd(
                sv["p"], sv["a1"], dab, full["conv_a_dw_w"][i], conv_a_ln_g[i:i + 1], conv_a_ln_b[i:i + 1],
                full["conv_b_dw_w"][i], name=f"conv_act_bwd_{layer}")
            grads["conv_a_dw_w"][i] = dwa[:ka]
            grads["conv_a_dw_b"][i] = dba[0]
            grads["conv_a_ln_g"][i] = dlng[0]
            grads["conv_a_ln_b"][i] = dlnb[0]
            grads["conv_b_dw_w"][i] = dwb[:kb]
            dhn = _matmul(dp, full["conv_w_in"][i], tb=True, name=f"conv_in_dx_{layer}")
            grads["conv_w_in"][i] = _weight_grad(sv["h"], dp, name=f"conv_in_dw_{layer}")
        else:
            do = _matmul(dx16, full["attn_w_o"][i], tb=True, name=f"attn_out_dx_{layer}")
            grads["attn_w_o"][i] = _weight_grad(sv["mix"], dx16, name=f"attn_out_dw_{layer}")
            dq, dk, dv = _attn_bwd(sv["qkv16"], sv["mix"], do, dh, name=f"attn_bwd_{layer}")
            dqkv, dgains = _qkv_prep_bwd(sv["qkv"], dq, dk, dv, sv["gains"], dh, name=f"qkv_prep_bwd_{layer}")
            grads["attn_q_g"][i] = dgains[0].reshape(-1, dh).sum(axis=0) * dh ** -0.5
            grads["attn_k_g"][i] = dgains[8].reshape(-1, dh).sum(axis=0)
            dhn = _matmul(dqkv, full["attn_w_qkv"][i], tb=True, name=f"attn_qkv_dx_{layer}")
            grads["attn_w_qkv"][i] = _weight_grad(sv["h"], dqkv, name=f"attn_qkv_dw_{layer}")
        dx, dx16, dg = _rmsnorm_bwd(sv["x_in"], mix_norm_g[layer:layer + 1], dhn, dx, name=f"mix_norm_bwd_{layer}")
        grads["mix_norm_g"][layer] = dg[0]

    me = _my_index()
    sends = []
    for n, axis in _MATMUL_WEIGHTS:
        l, a, b = weights[n].shape
        if axis == 2:
            stacked = jnp.stack(grads[n], axis=0)
            sends.append(_split_cols(stacked.reshape(l * a, N_DEV * b), name=f"split_{n}").reshape(N_DEV, l, a, b))
        else:
            sends.append(jnp.stack([g.reshape(N_DEV, a, b) for g in grads[n]], axis=1))
    received = _exchange(sends, "exchange_matmul_grads")
    final_grads, delta, new_m, new_v = {}, {}, {}, {}
    for (n, _), r in zip(_MATMUL_WEIGHTS, received):
        l, a, b = weights[n].shape
        outs = _adamw_sum(weights[n].reshape(l * a, b), r.reshape(N_DEV, l * a, b), mom_m[n].reshape(l * a, b),
                          mom_v[n].reshape(l * a, b), name=f"adamw_{n}")
        final_grads[n], delta[n], new_m[n], new_v[n] = [o.reshape(l, a, b) for o in outs]

    small_names = list(_REPLICATED) + [n for n, _ in _TAP_WEIGHTS]
    local = {n: jnp.stack(grads[n], axis=0) for n in small_names}
    small_shapes = [local[n].shape for n in small_names]
    g_small = _sum_slabs(_all_gather([_pack([local[n] for n in small_names], F32)], "gather_small_grads")[0],
                         name="sum_small_grads")
    reduced = dict(zip(small_names, _unpack(g_small, small_shapes)))
    for n in _REPLICATED:
        final_grads[n] = reduced[n]
    for n, axis in _TAP_WEIGHTS:
        final_grads[n] = lax.dynamic_index_in_dim(_split_shards(reduced[n], axis), me, axis=0, keepdims=False)
    shapes = [weights[n].shape for n in small_names]
    d_s, m_s, v_s = _adamw(
        _pack([weights[n] for n in small_names], F32), _pack([final_grads[n] for n in small_names], F32),
        _pack([mom_m[n] for n in small_names], F32), _pack([mom_v[n] for n in small_names], F32), name="adamw_small")
    delta.update(zip(small_names, _unpack(d_s, shapes)))
    new_m.update(zip(small_names, _unpack(m_s, shapes)))
    new_v.update(zip(small_names, _unpack(v_s, shapes)))

    order = list(_WEIGHT_ORDER)
    return (loss, dx.reshape(x.shape), *[final_grads[n] for n in order], *[delta[n] for n in order],
            *[new_m[n] for n in order], *[new_v[n] for n in order])
```

```python
import jax
import jax.numpy as jnp
from jax import lax
from jax.experimental import pallas as pl
from jax.experimental.pallas import tpu as pltpu

F32 = jnp.float32
BF16 = jnp.bfloat16
EPS = 1e-6
N_DEV = 8
MESH_ID = pl.DeviceIdType.MESH

ADAM_LR = 0.001
ADAM_B1 = 0.9
ADAM_B2 = 0.999
ADAM_EPS = 1e-08
ADAM_WD = 0.01
ADAM_STEP = 10

V7X_VMEM_LIMIT_BYTES = 48 * 1024 * 1024
PACK_QUANTUM = 2048
A_HALO = 32
S_HALO = 8
LANES = 128
ATTN_Q_BLOCK = 256
ATTN_K_BLOCK = 256
MATMUL_MAX_SINGLE_K = 3072
FFN_CHUNK_ROWS = 64
ATTN_DEAD_LOG_WEIGHT = -110.0


def _tile(n, prefs):
    for p in prefs:
        if n % p == 0:
            return p
    return n


def _params(sem=None):
    return pltpu.CompilerParams(dimension_semantics=sem, vmem_limit_bytes=V7X_VMEM_LIMIT_BYTES)


def _sigmoid(x):
    return 1.0 / (1.0 + jnp.exp(-x))


def _dot(a, b, ca, cb):
    return lax.dot_general(a, b, (((ca,), (cb,)), ((), ())), preferred_element_type=F32)


def _my_index():
    return 4 * lax.axis_index("x") + 2 * lax.axis_index("y") + lax.axis_index("c")


def _all_gather(shards, name):
    n = len(shards)

    def body(*refs):
        x_refs, out_refs = refs[:n], refs[n:2 * n]
        send_sems, recv_sems, local_sems = refs[2 * n:]
        x, y, c = lax.axis_index("x"), lax.axis_index("y"), lax.axis_index("c")
        me, sibling = (x, y, c), (x, y, 1 - c)
        chips = [(1 - x, y), (x, 1 - y), (1 - x, 1 - y)]

        def copy(a, k, block, to, src=None):
            slot = out_refs[a].at[4 * block[0] + 2 * block[1] + block[2]]
            return pltpu.make_async_remote_copy(
                src_ref=slot if src is None else src, dst_ref=slot,
                send_sem=send_sems.at[7 * a + k], recv_sem=recv_sems.at[7 * a + k],
                device_id=to, device_id_type=MESH_ID)

        mine = [pltpu.make_async_copy(x_refs[a], out_refs[a].at[4 * x + 2 * y + c], local_sems.at[a])
                for a in range(n)]
        for cp in mine:
            cp.start()
        first = []
        for a in range(n):
            first.append(copy(a, 0, me, sibling, src=x_refs[a]))
            first += [copy(a, 1 + j, me, (*chip, c), src=x_refs[a]) for j, chip in enumerate(chips)]
        for cp in first:
            cp.start()
        passed = []
        for j, chip in enumerate(chips):
            for a in range(n):
                copy(a, 1 + j, (*chip, c), me).wait_recv()
                fwd = copy(a, 4 + j, (*chip, c), sibling)
                fwd.start()
                passed.append(fwd)
        for a in range(n):
            copy(a, 0, sibling, me).wait_recv()
            for j, chip in enumerate(chips):
                copy(a, 4 + j, (*chip, 1 - c), me).wait_recv()
        for cp in first + passed:
            cp.wait_send()
        for cp in mine:
            cp.wait()

    hbm = pl.BlockSpec(memory_space=pl.ANY)
    return pl.pallas_call(
        body,
        name=name,
        out_shape=[jax.ShapeDtypeStruct((N_DEV,) + sh.shape, sh.dtype) for sh in shards],
        in_specs=[hbm] * n,
        out_specs=[hbm] * n,
        scratch_shapes=[
            pltpu.SemaphoreType.DMA((7 * n,)),
            pltpu.SemaphoreType.DMA((7 * n,)),
            pltpu.SemaphoreType.DMA((n,)),
        ],
    )(*shards)


def _exchange(slabs, name):
    n = len(slabs)

    def body(*refs):
        copies = _exchange_copies(refs[:n], refs[n:2 * n], *refs[2 * n:])
        _exchange_start(copies)
        _exchange_wait(copies)

    hbm = pl.BlockSpec(memory_space=pl.ANY)
    return pl.pallas_call(
        body,
        name=name,
        out_shape=[jax.ShapeDtypeStruct(g.shape, g.dtype) for g in slabs],
        in_specs=[hbm] * n,
        out_specs=[hbm] * n,
        scratch_shapes=_exchange_semaphores(n),
    )(*slabs)


def _exchange_semaphores(n):
    return [pltpu.SemaphoreType.DMA((7 * n,)), pltpu.SemaphoreType.DMA((7 * n,)), pltpu.SemaphoreType.DMA((n,))]


def _exchange_copies(g_refs, out_refs, send_sems, recv_sems, local_sems):
    n = len(g_refs)
    x, y, c = lax.axis_index("x"), lax.axis_index("y"), lax.axis_index("c")
    me = 4 * x + 2 * y + c
    local = [pltpu.make_async_copy(g_refs[a].at[me], out_refs[a].at[me], local_sems.at[a]) for a in range(n)]
    remote = []
    for k in range(1, N_DEV):
        px = (x + ((k >> 2) & 1)) % 2
        py = (y + ((k >> 1) & 1)) % 2
        pc = (c + (k & 1)) % 2
        for a in range(n):
            remote.append(pltpu.make_async_remote_copy(
                src_ref=g_refs[a].at[4 * px + 2 * py + pc], dst_ref=out_refs[a].at[me],
                send_sem=send_sems.at[7 * a + k - 1], recv_sem=recv_sems.at[7 * a + k - 1],
                device_id=(px, py, pc), device_id_type=MESH_ID))
    return local, remote


def _exchange_start(copies):
    for cp in copies[0] + copies[1]:
        cp.start()


def _exchange_wait(copies):
    local, remote = copies
    for cp in remote:
        cp.wait_recv()
    for cp in remote:
        cp.wait_send()
    for cp in local:
        cp.wait()


def _carrying_call(body, cargo, *, name, grid, in_specs, out_specs, out_shape, scratch_shapes, operands,
                   compiler_params):
    n = len(cargo)
    n_in, n_out, n_scr = len(in_specs), len(out_specs), len(scratch_shapes)
    if n == 0:
        return pl.pallas_call(body, name=name, grid=grid, in_specs=in_specs, out_specs=out_specs,
                              out_shape=out_shape, scratch_shapes=scratch_shapes,
                              compiler_params=compiler_params)(*operands), []

    def carrying(*refs):
        cuts = [0, n_in, n_in + n, n_in + n + n_out, n_in + 2 * n + n_out, n_in + 2 * n + n_out + n_scr, len(refs)]
        ins, g_refs, outs, r_refs, scr, sems = [refs[a:b] for a, b in zip(cuts[:-1], cuts[1:])]
        steps = [pl.program_id(ax) for ax in range(len(grid))]
        first = steps[0] == 0
        last = steps[0] == grid[0] - 1
        for ax in range(1, len(grid)):
            first = first & (steps[ax] == 0)
            last = last & (steps[ax] == grid[ax] - 1)

        @pl.when(first)
        def _():
            _exchange_start(_exchange_copies(g_refs, r_refs, *sems))

        body(*ins, *outs, *scr)

        @pl.when(last)
        def _():
            _exchange_wait(_exchange_copies(g_refs, r_refs, *sems))

    hbm = pl.BlockSpec(memory_space=pl.ANY)
    res = pl.pallas_call(
        carrying,
        name=name,
        grid=grid,
        in_specs=list(in_specs) + [hbm] * n,
        out_specs=list(out_specs) + [hbm] * n,
        out_shape=list(out_shape) + [jax.ShapeDtypeStruct(g.shape, g.dtype) for g in cargo],
        scratch_shapes=list(scratch_shapes) + _exchange_semaphores(n),
        compiler_params=compiler_params,
    )(*operands, *cargo)
    return res[:n_out], res[n_out:]


def _assemble_cols(g, *, name):
    n, r, w = g.shape
    tr = _tile(r, (256, 128, 64, 32, 16))

    def body(g_ref, o_ref):
        for j in range(n):
            o_ref[:, j * w:(j + 1) * w] = g_ref[j]

    return pl.pallas_call(
        body,
        name=name,
        grid=(r // tr,),
        in_specs=[pl.BlockSpec((n, tr, w), lambda i: (0, i, 0))],
        out_specs=pl.BlockSpec((tr, n * w), lambda i: (i, 0)),
        out_shape=jax.ShapeDtypeStruct((r, n * w), g.dtype),
        compiler_params=_params(("parallel",)),
    )(g)


def _split_cols(x, *, name):
    r, nw = x.shape
    w = nw // N_DEV
    tr = _tile(r, (256, 128, 64, 32, 16))

    def body(x_ref, o_ref):
        for j in range(N_DEV):
            o_ref[j] = x_ref[:, j * w:(j + 1) * w]

    return pl.pallas_call(
        body,
        name=name,
        grid=(r // tr,),
        in_specs=[pl.BlockSpec((tr, nw), lambda i: (i, 0))],
        out_specs=pl.BlockSpec((N_DEV, tr, w), lambda i: (0, i, 0)),
        out_shape=jax.ShapeDtypeStruct((N_DEV, r, w), x.dtype),
        compiler_params=_params(("parallel",)),
    )(x)


def _padded(n):
    return -(-n // PACK_QUANTUM) * PACK_QUANTUM


def _pack(parts, dtype, lead=()):
    flat = []
    for p in parts:
        p = p.astype(dtype).reshape(lead + (-1,))
        n = p.shape[-1]
        flat.append(jnp.pad(p, [(0, 0)] * len(lead) + [(0, _padded(n) - n)]))
    return jnp.concatenate(flat, axis=-1).reshape(lead + (-1, 128))


def _unpack(buf, shapes, lead=()):
    flat = buf.reshape(lead + (-1,))
    out, off = [], 0
    for shp in shapes:
        n = 1
        for d in shp:
            n *= d
        out.append(flat[..., off:off + n].reshape(lead + tuple(shp)))
        off += _padded(n)
    return out


def _half_if_tiled(n):
    half = n // 2
    return (half,) if n % 2 == 0 and half % LANES == 0 and half <= 1536 else ()


def _matmul(a, b, *, ta=False, tb=False, out_dtype=F32, add=None, name):
    if ta:
        kdim, m = a.shape
    else:
        m, kdim = a.shape
    n = b.shape[0] if tb else b.shape[1]
    bm = _tile(m, (1024,) + _half_if_tiled(m) + (512, 256, 128))
    bn = _tile(n, (512,) + _half_if_tiled(n) + (256, 128))
    bk = kdim if kdim <= MATMUL_MAX_SINGLE_K else _tile(kdim, (1024, 512, 256, 128))
    nk = kdim // bk

    a_spec = pl.BlockSpec((bk, bm), lambda i, j, k: (k, i)) if ta else pl.BlockSpec((bm, bk), lambda i, j, k: (i, k))
    b_spec = pl.BlockSpec((bn, bk), lambda i, j, k: (j, k)) if tb else pl.BlockSpec((bk, bn), lambda i, j, k: (k, j))
    o_spec = pl.BlockSpec((bm, bn), lambda i, j, k: (i, j))
    in_specs = [a_spec, b_spec] + ([o_spec] if add is not None else [])
    operands = [a, b] + ([add] if add is not None else [])

    def product(a_ref, b_ref):
        return _dot(a_ref[...].astype(BF16), b_ref[...].astype(BF16), 0 if ta else 1, 1 if tb else 0)

    def finish(r, refs):
        if add is not None:
            r = r + refs[2][...]
        return r

    def body_single(*refs):
        o_ref = refs[-1]
        o_ref[...] = finish(product(refs[0], refs[1]), refs).astype(o_ref.dtype)

    def body_multi(*refs):
        o_ref, acc_ref = refs[-2], refs[-1]
        k = pl.program_id(2)

        @pl.when(k == 0)
        def _():
            acc_ref[...] = jnp.zeros_like(acc_ref)

        acc_ref[...] += product(refs[0], refs[1])

        @pl.when(k == nk - 1)
        def _():
            o_ref[...] = finish(acc_ref[...], refs).astype(o_ref.dtype)

    return pl.pallas_call(
        body_single if nk == 1 else body_multi,
        name=name,
        grid=(m // bm, n // bn, nk),
        in_specs=in_specs,
        out_specs=o_spec,
        out_shape=jax.ShapeDtypeStruct((m, n), out_dtype),
        scratch_shapes=[] if nk == 1 else [pltpu.VMEM((bm, bn), F32)],
        compiler_params=_params(("parallel", "parallel", "arbitrary")),
    )(*operands)


def _weight_grad(x, dy, *, name):
    return _matmul(x, dy, ta=True, out_dtype=BF16, name=name)


def _rmsnorm_fwd(x, g, *, name):
    r, d = x.shape
    tr = _tile(r, (1024, 512, 256, 128))

    def body(x_ref, g_ref, o_ref):
        xv = x_ref[...]
        y = xv * lax.rsqrt(jnp.mean(xv * xv, axis=-1, keepdims=True) + EPS)
        y = y * g_ref[...]
        o_ref[...] = y.astype(o_ref.dtype)

    return pl.pallas_call(
        body,
        name=name,
        grid=(r // tr,),
        in_specs=[pl.BlockSpec((tr, d), lambda i: (i, 0)), pl.BlockSpec((1, d), lambda i: (0, 0))],
        out_specs=pl.BlockSpec((tr, d), lambda i: (i, 0)),
        out_shape=jax.ShapeDtypeStruct((r, d), BF16),
        compiler_params=_params(("parallel",)),
    )(x, g)


def _rmsnorm_bwd(x, g, dy, res, *, name):
    r, d = x.shape
    tr = _tile(r, (512, 256, 128))
    row = pl.BlockSpec((tr, d), lambda i: (i, 0))
    vec = pl.BlockSpec((1, d), lambda i: (0, 0))
    operands = [x, g, dy] + ([res] if res is not None else [])

    def body(*refs):
        if res is not None:
            x_ref, g_ref, dy_ref, res_ref, dx_ref, dx16_ref, dg_ref = refs
        else:
            x_ref, g_ref, dy_ref, dx_ref, dx16_ref, dg_ref = refs
        xv = x_ref[...]
        dyv = dy_ref[...].astype(F32)
        rs = lax.rsqrt(jnp.mean(xv * xv, axis=-1, keepdims=True) + EPS)
        xn = xv * rs
        dyg = dyv * g_ref[...]
        dx = rs * (dyg - xn * jnp.mean(dyg * xn, axis=-1, keepdims=True))
        if res is not None:
            dx = dx + res_ref[...]
        dx_ref[...] = dx
        dx16_ref[...] = dx.astype(dx16_ref.dtype)

        @pl.when(pl.program_id(0) == 0)
        def _():
            dg_ref[...] = jnp.zeros_like(dg_ref)

        dg_ref[...] += jnp.sum(dyv * xn, axis=0, keepdims=True)

    return pl.pallas_call(
        body,
        name=name,
        grid=(r // tr,),
        in_specs=[row, vec, row] + ([row] if res is not None else []),
        out_specs=(row, row, vec),
        out_shape=(jax.ShapeDtypeStruct((r, d), F32), jax.ShapeDtypeStruct((r, d), BF16),
                   jax.ShapeDtypeStruct((1, d), F32)),
        compiler_params=_params(("arbitrary",)),
    )(*operands)


def _loss_head(y, target, *, name):
    s, d = y.shape
    ts = _tile(s, (512, 256, 128))
    row = pl.BlockSpec((ts, d), lambda i: (i, 0))
    acc = pl.BlockSpec((8, 128), lambda i: (0, 0))

    def body(y_ref, t_ref, l_ref, dy_ref, dy16_ref):
        e = y_ref[...] - t_ref[...]
        dy = e * (1.0 / d)
        dy_ref[...] = dy
        dy16_ref[...] = dy.astype(dy16_ref.dtype)

        @pl.when(pl.program_id(0) == 0)
        def _():
            l_ref[...] = jnp.zeros_like(l_ref)

        sq = jnp.sum(jnp.sum(e * e, axis=-1, keepdims=True), axis=0, keepdims=True)
        l_ref[...] += jnp.broadcast_to(sq, l_ref.shape)

    return pl.pallas_call(
        body,
        name=name,
        grid=(s // ts,),
        in_specs=[row, row],
        out_specs=(acc, row, row),
        out_shape=(jax.ShapeDtypeStruct((8, 128), F32), jax.ShapeDtypeStruct((s, d), F32),
                   jax.ShapeDtypeStruct((s, d), BF16)),
        compiler_params=_params(("arbitrary",)),
    )(y, target)


def _ffn_tiles(s, f):
    return _tile(s, (512, 256, 128)), _tile(f, _half_if_tiled(f) + (512, 256, 128))


def _conv3(xs, w, b):
    return b + xs[0] * w[0:1] + xs[1] * w[1:2] + xs[2] * w[2:3]


def _shifted3(x, rows):
    return [x[S_HALO - 2 + k:S_HALO - 2 + k + rows] for k in range(3)]


def _ffn_act_fwd(ug, uv, wg, wv, bg, bv, *, name):
    s, f = ug.shape
    ts, tc = _ffn_tiles(s, f)
    rc = _tile(ts, (FFN_CHUNK_ROWS, 32, 16, 8))
    hb = ts // S_HALO
    main = pl.BlockSpec((ts, tc), lambda j, i: (i, j))
    prev = pl.BlockSpec((S_HALO, tc), lambda j, i: (jnp.maximum(i * hb - 1, 0), j))
    taps = pl.BlockSpec((3, tc), lambda j, i: (0, j))
    bias = pl.BlockSpec((1, tc), lambda j, i: (0, j))

    def body(ug_ref, ugp_ref, uv_ref, uvp_ref, wg_ref, wv_ref, bg_ref, bv_ref, o_ref, gbuf, vbuf):
        first = pl.program_id(1) == 0
        for buf, p_ref, m_ref in ((gbuf, ugp_ref, ug_ref), (vbuf, uvp_ref, uv_ref)):
            buf[0:S_HALO, :] = jnp.where(first, 0.0, p_ref[...])
            buf[S_HALO:, :] = m_ref[...]

        def chunk(ci, carry):
            r0 = pl.multiple_of(ci * rc, rc)
            for lb in range(tc // LANES):
                ls = pl.ds(lb * LANES, LANES)
                gate = _conv3(_shifted3(gbuf[pl.ds(r0, rc + S_HALO), ls], rc), wg_ref[:, ls], bg_ref[:, ls])
                val = _conv3(_shifted3(vbuf[pl.ds(r0, rc + S_HALO), ls], rc), wv_ref[:, ls], bv_ref[:, ls])
                o_ref[pl.ds(r0, rc), ls] = (gate * _sigmoid(gate) * val).astype(o_ref.dtype)
            return carry

        lax.fori_loop(0, ts // rc, chunk, 0)

    return pl.pallas_call(
        body,
        name=name,
        grid=(f // tc, s // ts),
        in_specs=[main, prev, main, prev, taps, taps, bias, bias],
        out_specs=main,
        out_shape=jax.ShapeDtypeStruct((s, f), BF16),
        scratch_shapes=[pltpu.VMEM((S_HALO + ts, tc), F32), pltpu.VMEM((S_HALO + ts, tc), F32)],
        compiler_params=_params(("parallel", "arbitrary")),
    )(ug, ug, uv, uv, wg, wv, bg, bv)


def _ffn_act_bwd(ug, uv, df, wg, wv, bg, bv, *, cargo=(), name):
    s, f = ug.shape
    ts, tc = _ffn_tiles(s, f)
    rc = _tile(ts, (FFN_CHUNK_ROWS, 32, 16, 8))
    hb = ts // S_HALO
    last_hb = s // S_HALO - 1
    n_row = s // ts
    main = pl.BlockSpec((ts, tc), lambda j, i: (i, j))
    prev = pl.BlockSpec((S_HALO, tc), lambda j, i: (jnp.maximum(i * hb - 1, 0), j))
    nxt = pl.BlockSpec((S_HALO, tc), lambda j, i: (jnp.minimum((i + 1) * hb, last_hb), j))
    taps = pl.BlockSpec((3, tc), lambda j, i: (0, j))
    bias = pl.BlockSpec((1, tc), lambda j, i: (0, j))
    sums = pl.BlockSpec((32, tc), lambda j, i: (0, j))
    ext = rc + S_HALO

    def body(ug_ref, ugp_ref, ugn_ref, uv_ref, uvp_ref, uvn_ref, df_ref, dfn_ref, wg_ref, wv_ref, bg_ref, bv_ref,
             dug_ref, duv_ref, sumg_ref, sumv_ref, gbuf, vbuf, dfbuf):
        i = pl.program_id(1)
        first = i == 0
        last = i == n_row - 1

        @pl.when(first)
        def _():
            sumg_ref[...] = jnp.zeros_like(sumg_ref)
            sumv_ref[...] = jnp.zeros_like(sumv_ref)

        for buf, p_ref, m_ref, n_ref in ((gbuf, ugp_ref, ug_ref, ugn_ref), (vbuf, uvp_ref, uv_ref, uvn_ref)):
            buf[0:S_HALO, :] = jnp.where(first, 0.0, p_ref[...])
            buf[S_HALO:S_HALO + ts, :] = m_ref[...]
            buf[S_HALO + ts:, :] = n_ref[...]
        dfbuf[0:ts, :] = df_ref[...]
        dfbuf[ts:, :] = jnp.where(last, 0.0, dfn_ref[...])

        def chunk(ci, carry):
            r0 = pl.multiple_of(ci * rc, rc)
            for lb in range(tc // LANES):
                ls = pl.ds(lb * LANES, LANES)
                xg = _shifted3(gbuf[pl.ds(r0, ext + S_HALO), ls], ext)
                xv = _shifted3(vbuf[pl.ds(r0, ext + S_HALO), ls], ext)
                wgv, wvv = wg_ref[:, ls], wv_ref[:, ls]
                gate = _conv3(xg, wgv, bg_ref[:, ls])
                val = _conv3(xv, wvv, bv_ref[:, ls])
                dfe = dfbuf[pl.ds(r0, ext), ls]
                sg = _sigmoid(gate)
                dgate = dfe * val * (sg * (1.0 + gate * (1.0 - sg)))
                dval = dfe * (gate * sg)
                for dz, xs, w, du_ref, sum_ref in ((dgate, xg, wgv, dug_ref, sumg_ref),
                                                   (dval, xv, wvv, duv_ref, sumv_ref)):
                    du = dz[2:2 + rc] * w[0:1] + dz[1:1 + rc] * w[1:2] + dz[0:rc] * w[2:3]
                    du_ref[pl.ds(r0, rc), ls] = du.astype(du_ref.dtype)
                    dm = dz[0:rc]
                    for k in range(3):
                        sum_ref[8 * k:8 * k + 8, ls] += (dm * xs[k][0:rc]).reshape(rc // 8, 8, LANES).sum(axis=0)
                    sum_ref[24:32, ls] += dm.reshape(rc // 8, 8, LANES).sum(axis=0)
            return carry

        lax.fori_loop(0, ts // rc, chunk, 0)

    return _carrying_call(
        body,
        cargo,
        name=name,
        grid=(f // tc, n_row),
        in_specs=[main, prev, nxt, main, prev, nxt, main, nxt, taps, taps, bias, bias],
        out_specs=(main, main, sums, sums),
        out_shape=(jax.ShapeDtypeStruct((s, f), BF16), jax.ShapeDtypeStruct((s, f), BF16),
                   jax.ShapeDtypeStruct((32, f), F32), jax.ShapeDtypeStruct((32, f), F32)),
        scratch_shapes=[pltpu.VMEM((2 * S_HALO + ts, tc), F32), pltpu.VMEM((2 * S_HALO + ts, tc), F32),
                        pltpu.VMEM((S_HALO + ts, tc), F32)],
        operands=(ug, ug, ug, uv, uv, uv, df, df, wg, wv, bg, bv),
        compiler_params=_params(("arbitrary", "arbitrary")),
    )


def _layernorm_stats(a1):
    mu = jnp.mean(a1, axis=-1, keepdims=True)
    xc = a1 - mu
    rstd = lax.rsqrt(jnp.mean(xc * xc, axis=-1, keepdims=True) + EPS)
    return xc * rstd, rstd


def _conv_act_fwd(p, wa, ba, lng, lnb, wb, *, name):
    s, c5 = p.shape
    c = c5 // 5
    ka, kb = wa.shape[0], wb.shape[0]
    ts = _tile(s, (256, 128))
    hb = ts // A_HALO
    main = pl.BlockSpec((ts, c5), lambda i: (i, 0))
    prev = pl.BlockSpec((A_HALO, c5), lambda i: (jnp.maximum(i * hb - 1, 0), 0))

    def full(arr):
        return pl.BlockSpec(arr.shape, lambda i: (0, 0))

    def body(p_ref, pp_ref, wa_ref, ba_ref, lng_ref, lnb_ref, wb_ref, ab_ref, a1_ref, gbuf, cbuf):
        first = pl.program_id(0) == 0
        gbuf[0:A_HALO, :] = jnp.where(first, 0.0, pp_ref[:, 0:c] * _sigmoid(pp_ref[:, c:2 * c]))
        gbuf[A_HALO:, :] = p_ref[:, 0:c] * _sigmoid(p_ref[:, c:2 * c])
        cbuf[0:A_HALO, :] = jnp.where(first, 0.0, pp_ref[:, 3 * c:4 * c] * pp_ref[:, 4 * c:5 * c])
        cbuf[A_HALO:, :] = p_ref[:, 3 * c:4 * c] * p_ref[:, 4 * c:5 * c]

        a1 = jnp.broadcast_to(ba_ref[...], (ts, c))
        for k in range(ka):
            off = A_HALO - (ka - 1) + k
            a1 = a1 + gbuf[off:off + ts, :] * wa_ref[k:k + 1, :]
        a1_ref[...] = a1
        xhat, _ = _layernorm_stats(a1)
        a2 = xhat * lng_ref[...] + lnb_ref[...]
        ab_ref[:, 0:c] = (a2 * _sigmoid(a2)).astype(ab_ref.dtype)

        cv = jnp.zeros((ts, c), F32)
        for k in range(kb):
            off = A_HALO - (kb - 1) + k
            cv = cv + cbuf[off:off + ts, :] * wb_ref[k:k + 1, :]
        ab_ref[:, c:2 * c] = (p_ref[:, 2 * c:3 * c] * cv).astype(ab_ref.dtype)

    return pl.pallas_call(
        body,
        name=name,
        grid=(s // ts,),
        in_specs=[main, prev, full(wa), full(ba), full(lng), full(lnb), full(wb)],
        out_specs=(pl.BlockSpec((ts, 2 * c), lambda i: (i, 0)), pl.BlockSpec((ts, c), lambda i: (i, 0))),
        out_shape=(jax.ShapeDtypeStruct((s, 2 * c), BF16), jax.ShapeDtypeStruct((s, c), F32)),
        scratch_shapes=[pltpu.VMEM((A_HALO + ts, c), F32), pltpu.VMEM((A_HALO + ts, c), F32)],
        compiler_params=_params(("parallel",)),
    )(p, p, wa, ba, lng, lnb, wb)


def _conv_act_bwd(p, a1, dab, wa, lng, lnb, wb, *, cargo=(), name):
    s, c5 = p.shape
    c = c5 // 5
    ka, kb = wa.shape[0], wb.shape[0]
    ts = _tile(s, (256, 128))
    hb = ts // A_HALO
    n_row = s // ts
    last_hb = s // A_HALO - 1
    ext = ts + A_HALO

    def rows(width):
        return (pl.BlockSpec((ts, width), lambda i: (i, 0)),
                pl.BlockSpec((A_HALO, width), lambda i: (jnp.maximum(i * hb - 1, 0), 0)),
                pl.BlockSpec((A_HALO, width), lambda i: (jnp.minimum((i + 1) * hb, last_hb), 0)))

    p_main, p_prev, p_next = rows(c5)
    d_main, _, d_next = rows(2 * c)
    a_main, _, a_next = rows(c)

    def full(shape):
        return pl.BlockSpec(shape, lambda i: (0, 0))

    def body(p_ref, pp_ref, pn_ref, a1_ref, a1n_ref, d_ref, dn_ref, wa_ref, lng_ref, lnb_ref, wb_ref,
             dp_ref, dwa_ref, dba_ref, dlng_ref, dlnb_ref, dwb_ref, gbuf, cbuf, da1buf, dcvbuf):
        i = pl.program_id(0)
        first = i == 0
        last = i == n_row - 1

        @pl.when(first)
        def _():
            dwa_ref[...] = jnp.zeros_like(dwa_ref)
            dba_ref[...] = jnp.zeros_like(dba_ref)
            dlng_ref[...] = jnp.zeros_like(dlng_ref)
            dlnb_ref[...] = jnp.zeros_like(dlnb_ref)
            dwb_ref[...] = jnp.zeros_like(dwb_ref)

        sig_gate = _sigmoid(p_ref[:, c:2 * c])
        gbuf[0:A_HALO, :] = jnp.where(first, 0.0, pp_ref[:, 0:c] * _sigmoid(pp_ref[:, c:2 * c]))
        gbuf[A_HALO:, :] = p_ref[:, 0:c] * sig_gate
        cbuf[0:A_HALO, :] = jnp.where(first, 0.0, pp_ref[:, 3 * c:4 * c] * pp_ref[:, 4 * c:5 * c])
        cbuf[A_HALO:, :] = p_ref[:, 3 * c:4 * c] * p_ref[:, 4 * c:5 * c]

        def ln_back(a1v, dav):
            xhat, rstd = _layernorm_stats(a1v)
            a2 = xhat * lng_ref[...] + lnb_ref[...]
            sg = _sigmoid(a2)
            da2 = dav * (sg * (1.0 + a2 * (1.0 - sg)))
            dxh = da2 * lng_ref[...]
            da1 = rstd * (dxh - jnp.mean(dxh, axis=-1, keepdims=True)
                          - xhat * jnp.mean(dxh * xhat, axis=-1, keepdims=True))
            return da1, da2, xhat

        da1_m, da2_m, xhat_m = ln_back(a1_ref[...], d_ref[:, 0:c])
        da1_n, _, _ = ln_back(a1n_ref[...], dn_ref[:, 0:c])
        da1buf[0:ts, :] = da1_m
        da1buf[ts:, :] = jnp.where(last, 0.0, da1_n)
        dlng_ref[...] += jnp.sum(da2_m * xhat_m, axis=0, keepdims=True)
        dlnb_ref[...] += jnp.sum(da2_m, axis=0, keepdims=True)
        dba_ref[...] += jnp.sum(da1_m, axis=0, keepdims=True)

        dglu = jnp.zeros((ts, c), F32)
        for k in range(ka):
            off = (ka - 1) - k
            dglu = dglu + da1buf[off:off + ts, :] * wa_ref[k:k + 1, :]
            goff = A_HALO - (ka - 1) + k
            dwa_ref[k:k + 1, :] += jnp.sum(da1_m * gbuf[goff:goff + ts, :], axis=0, keepdims=True)
        dp_ref[:, 0:c] = (dglu * sig_gate).astype(dp_ref.dtype)
        dp_ref[:, c:2 * c] = (dglu * p_ref[:, 0:c] * sig_gate * (1.0 - sig_gate)).astype(dp_ref.dtype)

        cv = jnp.zeros((ts, c), F32)
        for k in range(kb):
            off = A_HALO - (kb - 1) + k
            cv = cv + cbuf[off:off + ts, :] * wb_ref[k:k + 1, :]
        db_m = d_ref[:, c:2 * c]
        dp_ref[:, 2 * c:3 * c] = (db_m * cv).astype(dp_ref.dtype)
        dcv_m = db_m * p_ref[:, 2 * c:3 * c]
        dcvbuf[0:ts, :] = dcv_m
        dcvbuf[ts:, :] = jnp.where(last, 0.0, dn_ref[:, c:2 * c] * pn_ref[:, 2 * c:3 * c])
        dbc = jnp.zeros((ts, c), F32)
        for k in range(kb):
            off = (kb - 1) - k
            dbc = dbc + dcvbuf[off:off + ts, :] * wb_ref[k:k + 1, :]
            coff = A_HALO - (kb - 1) + k
            dwb_ref[k:k + 1, :] += jnp.sum(dcv_m * cbuf[coff:coff + ts, :], axis=0, keepdims=True)
        dp_ref[:, 3 * c:4 * c] = (dbc * p_ref[:, 4 * c:5 * c]).astype(dp_ref.dtype)
        dp_ref[:, 4 * c:5 * c] = (dbc * p_ref[:, 3 * c:4 * c]).astype(dp_ref.dtype)

    return _carrying_call(
        body,
        cargo,
        name=name,
        grid=(n_row,),
        in_specs=[p_main, p_prev, p_next, a_main, a_next, d_main, d_next,
                  full(wa.shape), full(lng.shape), full(lnb.shape), full(wb.shape)],
        out_specs=(pl.BlockSpec((ts, c5), lambda i: (i, 0)), full((32, c)), full((1, c)), full((1, c)),
                   full((1, c)), full((8, c))),
        out_shape=(
            jax.ShapeDtypeStruct((s, c5), BF16), jax.ShapeDtypeStruct((32, c), F32),
            jax.ShapeDtypeStruct((1, c), F32), jax.ShapeDtypeStruct((1, c), F32),
            jax.ShapeDtypeStruct((1, c), F32), jax.ShapeDtypeStruct((8, c), F32),
        ),
        scratch_shapes=[
            pltpu.VMEM((A_HALO + ts, c), F32), pltpu.VMEM((A_HALO + ts, c), F32),
            pltpu.VMEM((ext, c), F32), pltpu.VMEM((ext, c), F32),
        ],
        operands=(p, p, p, a1, a1, dab, dab, wa, lng, lnb, wb),
        compiler_params=_params(("arbitrary",)),
    )


def _head_pair_geometry(width, dh):
    assert 2 * dh == LANES, "the attention kernels pair two heads in one 128-lane block"
    return width // (3 * LANES)


def _group_sum(v, lo):
    s_lo = jnp.sum(jnp.where(lo, v, 0.0), axis=-1, keepdims=True)
    s_hi = jnp.sum(jnp.where(lo, 0.0, v), axis=-1, keepdims=True)
    return jnp.where(lo, s_lo, s_hi)


def _qkv_prep_fwd(qkv, gains, dh, *, name):
    s, width = qkv.shape
    nsec = _head_pair_geometry(width, dh)
    tr = _tile(s, (256, 128))
    blk = pl.BlockSpec((tr, nsec * LANES), lambda i, sec: (i, sec))

    def body(x_ref, g_ref, o_ref):
        sec = pl.program_id(1)

        @pl.when(sec == 2)
        def _():
            o_ref[...] = x_ref[...].astype(o_ref.dtype)

        @pl.when(sec != 2)
        def _():
            lo = lax.broadcasted_iota(jnp.int32, (1, LANES), 1) < dh
            for lb in range(nsec):
                ls = pl.ds(lb * LANES, LANES)
                xv = x_ref[:, ls]
                rs = lax.rsqrt(_group_sum(xv * xv, lo) * (1.0 / dh) + EPS)
                o_ref[:, ls] = (xv * rs * g_ref[0:1, :]).astype(o_ref.dtype)

    return pl.pallas_call(
        body,
        name=name,
        grid=(s // tr, 3),
        in_specs=[blk, pl.BlockSpec((8, LANES), lambda i, sec: (sec, 0))],
        out_specs=blk,
        out_shape=jax.ShapeDtypeStruct((s, width), BF16),
        compiler_params=_params(("parallel", "parallel")),
    )(qkv, gains)


def _qkv_prep_bwd(qkv, dq, dk, dv, gains, dh, *, name):
    s, width = qkv.shape
    nsec = _head_pair_geometry(width, dh)
    tr = _tile(s, (256, 128))
    blk = pl.BlockSpec((tr, nsec * LANES), lambda sec, i: (i, sec))
    gspec = pl.BlockSpec((8, LANES), lambda sec, i: (sec, 0))

    def grad_spec(own):
        return pl.BlockSpec((tr, nsec * LANES), lambda sec, i: (jnp.where(sec == own, i, 0), 0))

    def body(x_ref, dq_ref, dk_ref, dv_ref, g_ref, o_ref, dg_ref):
        sec = pl.program_id(0)

        @pl.when(pl.program_id(1) == 0)
        def _():
            dg_ref[...] = jnp.zeros_like(dg_ref)

        def normed(d_ref):
            lo = lax.broadcasted_iota(jnp.int32, (1, LANES), 1) < dh
            dg = jnp.zeros((1, LANES), F32)
            for lb in range(nsec):
                ls = pl.ds(lb * LANES, LANES)
                xv = x_ref[:, ls]
                d = d_ref[:, ls]
                rs = lax.rsqrt(_group_sum(xv * xv, lo) * (1.0 / dh) + EPS)
                xn = xv * rs
                dyg = d * g_ref[0:1, :]
                dx = rs * (dyg - xn * (_group_sum(dyg * xn, lo) * (1.0 / dh)))
                o_ref[:, ls] = dx.astype(o_ref.dtype)
                dg = dg + jnp.sum(d * xn, axis=0, keepdims=True)
            dg_ref[...] += jnp.broadcast_to(dg, dg_ref.shape)

        @pl.when(sec == 0)
        def _():
            normed(dq_ref)

        @pl.when(sec == 1)
        def _():
            normed(dk_ref)

        @pl.when(sec == 2)
        def _():
            o_ref[...] = dv_ref[...].astype(o_ref.dtype)

    return pl.pallas_call(
        body,
        name=name,
        grid=(3, s // tr),
        in_specs=[blk, grad_spec(0), grad_spec(1), grad_spec(2), gspec],
        out_specs=(blk, gspec),
        out_shape=(jax.ShapeDtypeStruct((s, width), BF16), jax.ShapeDtypeStruct((24, LANES), F32)),
        compiler_params=_params(("arbitrary", "arbitrary")),
    )(qkv, dq, dk, dv, gains)


def _split_dot(v, tri):
    hi = v.astype(BF16)
    lo = (v - hi.astype(F32)).astype(BF16)
    return _dot(hi, tri, 1, 0) + _dot(lo, tri, 1, 0)


def _attn_scores(qm, kb, tri_gt, valid, r_run):
    z = _dot(qm, kb, 1, 1)
    zl = jnp.minimum(z, 0.0) - jnp.log(1.0 + jnp.exp(-jnp.abs(z)))
    lk = zl - z
    if valid is not None:
        lk = jnp.where(valid, lk, 0.0)
    after = _split_dot(lk, tri_gt)
    w = jnp.exp(zl + after + r_run)
    if valid is not None:
        w = jnp.where(valid, w, 0.0)
    return zl, lk, w


def _attn_alive(rr):
    return jnp.max(jnp.maximum(rr[0], rr[1])) > ATTN_DEAD_LOG_WEIGHT


def _attn_specs(s, width, dh, tq):
    nsec = _head_pair_geometry(width, dh)
    qspec = pl.BlockSpec((tq, LANES), lambda h, i: (i, h))
    kspec = pl.BlockSpec((s, LANES), lambda h, i: (0, nsec + h))
    vspec = pl.BlockSpec((s, LANES), lambda h, i: (0, 2 * nsec + h))
    return nsec, qspec, kspec, vspec


def _attn_masks(tq, tk, dh):
    lo = lax.broadcasted_iota(jnp.int32, (1, LANES), 1) < dh
    r_io = lax.broadcasted_iota(jnp.int32, (tk, tk), 0)
    c_io = lax.broadcasted_iota(jnp.int32, (tk, tk), 1)
    qrow = lax.broadcasted_iota(jnp.int32, (tq, tk), 0)
    kcol = lax.broadcasted_iota(jnp.int32, (tq, tk), 1)
    return lo, r_io, c_io, qrow, kcol


def _attn_fwd(qkv16, dh, *, name):
    s, width = qkv16.shape
    tq = _tile(s, (ATTN_Q_BLOCK, ATTN_K_BLOCK))
    tk = _tile(tq, (ATTN_K_BLOCK,))
    nd = tq // tk
    nsec, qspec, kspec, vspec = _attn_specs(s, width, dh, tq)

    def body(q_ref, k_ref, v_ref, o_ref):
        i = pl.program_id(1)
        lo, r_io, c_io, qrow, kcol = _attn_masks(tq, tk, dh)
        q2 = q_ref[...]
        zq = jnp.zeros_like(q2)
        qs = (jnp.where(lo, q2, zq), jnp.where(lo, zq, q2))
        tri_gt = (r_io > c_io).astype(BF16)

        def block(j, carry, diag):
            rr, acc = carry
            start = pl.multiple_of(j * tk, tk)
            kb = k_ref[pl.ds(start, tk), :]
            vb = v_ref[pl.ds(start, tk), :]
            valid = None if diag is None else kcol + diag * tk < qrow
            outs, new_r = [], []
            for hh in range(2):
                _, lk, w = _attn_scores(qs[hh], kb, tri_gt, valid, rr[hh])
                outs.append(_dot(w.astype(BF16), vb, 1, 0))
                new_r.append(rr[hh] + jnp.sum(lk, axis=-1, keepdims=True))
            return tuple(new_r), acc + jnp.where(lo, outs[0], outs[1])

        zero = jnp.zeros((tq, 1), F32)
        carry = ((zero, zero), jnp.zeros((tq, LANES), F32))
        for dg in reversed(range(nd)):
            carry = block(i * nd + dg, carry, dg)
        below = i * nd

        def step(st):
            jj, _, cr = st
            cr = block(below - 1 - jj, cr, None)
            return jj + 1, _attn_alive(cr[0]), cr

        _, _, carry = lax.while_loop(lambda st: (st[0] < below) & st[1], step,
                                     (jnp.int32(0), _attn_alive(carry[0]), carry))
        o_ref[...] = carry[1]

    return pl.pallas_call(
        body,
        name=name,
        grid=(nsec, s // tq),
        in_specs=[qspec, kspec, vspec],
        out_specs=qspec,
        out_shape=jax.ShapeDtypeStruct((s, width // 3), F32),
        compiler_params=_params(("parallel", "arbitrary")),
    )(qkv16, qkv16, qkv16)


def _attn_bwd(qkv16, o, do, dh, *, cargo=(), name):
    s, width = qkv16.shape
    tq = _tile(s, (ATTN_Q_BLOCK, ATTN_K_BLOCK))
    tk = _tile(tq, (ATTN_K_BLOCK,))
    nd = tq // tk
    nsec, qspec, kspec, vspec = _attn_specs(s, width, dh, tq)
    accspec = pl.BlockSpec((s, LANES), lambda h, i: (0, h))

    def body(q_ref, k_ref, v_ref, o_ref, do_ref, dq_ref, dk_ref, dv_ref):
        i = pl.program_id(1)

        @pl.when(i == 0)
        def _():
            dk_ref[...] = jnp.zeros_like(dk_ref)
            dv_ref[...] = jnp.zeros_like(dv_ref)

        lo, r_io, c_io, qrow, kcol = _attn_masks(tq, tk, dh)
        q2 = q_ref[...]
        zq = jnp.zeros_like(q2)
        qs = (jnp.where(lo, q2, zq), jnp.where(lo, zq, q2))
        do16 = do_ref[...].astype(BF16)
        dos = (jnp.where(lo, do16, zq), jnp.where(lo, zq, do16))
        prod = do16.astype(F32) * o_ref[...]
        deltas = (jnp.sum(jnp.where(lo, prod, 0.0), axis=-1, keepdims=True),
                  jnp.sum(jnp.where(lo, 0.0, prod), axis=-1, keepdims=True))
        tri_gt = (r_io > c_io).astype(BF16)
        tri_ge = (r_io >= c_io).astype(BF16)

        def block(j, carry, diag):
            rr, er, dq = carry
            start = pl.multiple_of(j * tk, tk)
            kb = k_ref[pl.ds(start, tk), :]
            vb = v_ref[pl.ds(start, tk), :]
            valid = None if diag is None else kcol + diag * tk < qrow
            new_r, new_e, dqs = [], [], []
            dk_blk = dv_blk = None
            for hh in range(2):
                zl, lk, w = _attn_scores(qs[hh], kb, tri_gt, valid, rr[hh])
                beta = jnp.exp(zl)
                w16 = w.astype(BF16)
                e = _dot(dos[hh], vb, 1, 1) * w16.astype(F32)
                e_before = deltas[hh] - er[hh] - _split_dot(e, tri_ge)
                dz = e - beta * (e + e_before)
                if valid is not None:
                    dz = jnp.where(valid, dz, 0.0)
                dz16 = dz.astype(BF16)
                dqs.append(_dot(dz16, kb, 1, 0))
                dkc = _dot(dz16, qs[hh], 0, 0)
                dvc = _dot(w16, dos[hh], 0, 0)
                dk_blk = dkc if dk_blk is None else dk_blk + dkc
                dv_blk = dvc if dv_blk is None else dv_blk + dvc
                new_r.append(rr[hh] + jnp.sum(lk, axis=-1, keepdims=True))
                new_e.append(er[hh] + jnp.sum(e, axis=-1, keepdims=True))
            dk_ref[pl.ds(start, tk), :] += dk_blk
            dv_ref[pl.ds(start, tk), :] += dv_blk
            return tuple(new_r), tuple(new_e), dq + jnp.where(lo, dqs[0], dqs[1])

        zero = jnp.zeros((tq, 1), F32)
        carry = ((zero, zero), (zero, zero), jnp.zeros((tq, LANES), F32))
        for dg in reversed(range(nd)):
            carry = block(i * nd + dg, carry, dg)
        below = i * nd

        def step(st):
            jj, _, cr = st
            cr = block(below - 1 - jj, cr, None)
            return jj + 1, _attn_alive(cr[0]), cr

        _, _, carry = lax.while_loop(lambda st: (st[0] < below) & st[1], step,
                                     (jnp.int32(0), _attn_alive(carry[0]), carry))
        dq_ref[...] = carry[2]

    shp = jax.ShapeDtypeStruct((s, width // 3), F32)
    return _carrying_call(
        body,
        cargo,
        name=name,
        grid=(nsec, s // tq),
        in_specs=[qspec, kspec, vspec, qspec, qspec],
        out_specs=(qspec, accspec, accspec),
        out_shape=(shp, shp, shp),
        scratch_shapes=[],
        operands=(qkv16, qkv16, qkv16, o, do),
        compiler_params=_params(("arbitrary", "arbitrary")),
    )


def _adam_update(wv, gv, mv, vv):
    c1 = 1.0 - ADAM_B1 ** ADAM_STEP
    c2 = 1.0 - ADAM_B2 ** ADAM_STEP
    nm = ADAM_B1 * mv + (1.0 - ADAM_B1) * gv
    nv = ADAM_B2 * vv + (1.0 - ADAM_B2) * (gv * gv)
    delta = -ADAM_LR * ((nm / c1) / (jnp.sqrt(nv / c2) + ADAM_EPS) + ADAM_WD * wv)
    return delta, nm, nv


def _sum_slabs(r, *, name):
    n, rows, lanes = r.shape
    tr = _tile(rows, (1024, 512, 256, 128, 64, 32, 16))

    def body(r_ref, o_ref):
        acc = r_ref[0]
        for d in range(1, n):
            acc = acc + r_ref[d]
        o_ref[...] = acc

    return pl.pallas_call(
        body,
        name=name,
        grid=(rows // tr,),
        in_specs=[pl.BlockSpec((n, tr, lanes), lambda i: (0, i, 0))],
        out_specs=pl.BlockSpec((tr, lanes), lambda i: (i, 0)),
        out_shape=jax.ShapeDtypeStruct((rows, lanes), F32),
        compiler_params=_params(("parallel",)),
    )(r)


def _adamw(w, g, m, v, *, name):
    rows, cols = w.shape
    tr = _tile(rows, (1024, 512, 256, 128, 64, 32, 16))
    spec = pl.BlockSpec((tr, cols), lambda i: (i, 0))

    def body(w_ref, g_ref, m_ref, v_ref, d_ref, nm_ref, nv_ref):
        d_ref[...], nm_ref[...], nv_ref[...] = _adam_update(w_ref[...], g_ref[...], m_ref[...], v_ref[...])

    shp = jax.ShapeDtypeStruct((rows, cols), F32)
    return pl.pallas_call(
        body,
        name=name,
        grid=(rows // tr,),
        in_specs=[spec, spec, spec, spec],
        out_specs=(spec, spec, spec),
        out_shape=(shp, shp, shp),
        compiler_params=_params(("parallel",)),
    )(w, g, m, v)


def _adamw_sum(w, r, m, v, *, name):
    n, rows, cols = r.shape
    tr = _tile(rows, (256, 128, 64, 32, 16, 8))
    spec = pl.BlockSpec((tr, cols), lambda i: (i, 0))

    def body(w_ref, r_ref, m_ref, v_ref, g_ref, d_ref, nm_ref, nv_ref):
        gv = r_ref[0].astype(F32)
        for d in range(1, n):
            gv = gv + r_ref[d].astype(F32)
        g_ref[...] = gv
        d_ref[...], nm_ref[...], nv_ref[...] = _adam_update(w_ref[...], gv, m_ref[...], v_ref[...])

    shp = jax.ShapeDtypeStruct((rows, cols), F32)
    return pl.pallas_call(
        body,
        name=name,
        grid=(rows // tr,),
        in_specs=[spec, pl.BlockSpec((n, tr, cols), lambda i: (0, i, 0)), spec, spec],
        out_specs=(spec, spec, spec, spec),
        out_shape=(shp, shp, shp, shp),
        compiler_params=_params(("parallel",)),
    )(w, r, m, v)


_MATMUL_WEIGHTS = (("conv_w_in", 2), ("conv_w_out", 1), ("attn_w_qkv", 2), ("attn_w_o", 1), ("ffn_w_up", 2),
                   ("ffn_w_down", 1))
_TAP_WEIGHTS = (("conv_a_dw_w", 2), ("conv_b_dw_w", 2), ("ffn_dw_w", 2))
_REPLICATED = ("mix_norm_g", "ffn_norm_g", "conv_a_dw_b", "conv_a_ln_g", "conv_a_ln_b", "attn_q_g", "attn_k_g",
               "ffn_dw_b")
_WEIGHT_ORDER = ("mix_norm_g", "ffn_norm_g", "conv_w_in", "conv_a_dw_w", "conv_a_dw_b", "conv_a_ln_g",
                 "conv_a_ln_b", "conv_b_dw_w", "conv_w_out", "attn_w_qkv", "attn_q_g", "attn_k_g", "attn_w_o",
                 "ffn_w_up", "ffn_dw_w", "ffn_dw_b", "ffn_w_down")


def _assemble(gathered, axis):
    n, l, a, b = gathered.shape
    if axis == 2:
        return jnp.transpose(gathered, (1, 2, 0, 3)).reshape(l, a, n * b)
    return jnp.transpose(gathered, (1, 0, 2, 3)).reshape(l, n * a, b)


def _split_shards(full, axis):
    l, a, b = full.shape
    if axis == 2:
        return jnp.transpose(full.reshape(l, a, N_DEV, b // N_DEV), (2, 0, 1, 3))
    return jnp.transpose(full.reshape(l, N_DEV, a // N_DEV, b), (1, 0, 2, 3))


def kernel(x, mix_norm_g, ffn_norm_g, conv_w_in, conv_a_dw_w, conv_a_dw_b, conv_a_ln_g, conv_a_ln_b, conv_b_dw_w, conv_w_out, attn_w_qkv, attn_q_g, attn_k_g, attn_w_o, ffn_w_up, ffn_dw_w, ffn_dw_b, ffn_w_down, loss_target, m_mix_norm_g, m_ffn_norm_g, m_conv_w_in, m_conv_a_dw_w, m_conv_a_dw_b, m_conv_a_ln_g, m_conv_a_ln_b, m_conv_b_dw_w, m_conv_w_out, m_attn_w_qkv, m_attn_q_g, m_attn_k_g, m_attn_w_o, m_ffn_w_up, m_ffn_dw_w, m_ffn_dw_b, m_ffn_w_down, v_mix_norm_g, v_ffn_norm_g, v_conv_w_in, v_conv_a_dw_w, v_conv_a_dw_b, v_conv_a_ln_g, v_conv_a_ln_b, v_conv_b_dw_w, v_conv_w_out, v_attn_w_qkv, v_attn_q_g, v_attn_k_g, v_attn_w_o, v_ffn_w_up, v_ffn_dw_w, v_ffn_dw_b, v_ffn_w_down):
    args = dict(locals())
    weights = {n: args[n] for n in _WEIGHT_ORDER}
    mom_m = {n: args["m_" + n] for n in _WEIGHT_ORDER}
    mom_v = {n: args["v_" + n] for n in _WEIGHT_ORDER}

    _, s, d = x.shape
    depth = mix_norm_g.shape[0]
    dh = attn_q_g.shape[1]
    x0 = x.reshape(s, d)
    target = loss_target.reshape(s, d)

    gathered = _all_gather([weights[n].astype(BF16) for n, _ in _MATMUL_WEIGHTS], "gather_matmul_weights")
    full = {}
    for (n, axis), g in zip(_MATMUL_WEIGHTS, gathered):
        _, l, a, b = g.shape
        if axis == 2:
            full[n] = _assemble_cols(g.reshape(N_DEV, l * a, b), name=f"assemble_{n}").reshape(l, a, N_DEV * b)
        else:
            full[n] = jnp.transpose(g, (1, 0, 2, 3)).reshape(l, N_DEV * a, b)
    tap_shapes = [weights[n].shape for n, _ in _TAP_WEIGHTS]
    g_tap = _all_gather([_pack([weights[n] for n, _ in _TAP_WEIGHTS], F32)], "gather_tap_weights")[0]
    for (n, axis), part in zip(_TAP_WEIGHTS, _unpack(g_tap, tap_shapes, lead=(N_DEV,))):
        full[n] = _assemble(part, axis)
    f = full["ffn_w_down"].shape[1]

    saved = []
    xs = x0
    for layer in range(depth):
        i = layer // 2
        sv = {"x_in": xs}
        hn = _rmsnorm_fwd(xs, mix_norm_g[layer:layer + 1], name=f"mix_norm_fwd_{layer}")
        sv["h"] = hn
        if layer % 2 == 0:
            p = _matmul(hn, full["conv_w_in"][i], name=f"conv_in_{layer}")
            ab, a1 = _conv_act_fwd(p, full["conv_a_dw_w"][i], conv_a_dw_b[i:i + 1], conv_a_ln_g[i:i + 1],
                                   conv_a_ln_b[i:i + 1], full["conv_b_dw_w"][i], name=f"conv_act_fwd_{layer}")
            sv.update(p=p, a1=a1, mix=ab)
            xs = _matmul(ab, full["conv_w_out"][i], add=xs, name=f"conv_out_{layer}")
        else:
            reps = LANES // dh
            gains = jnp.concatenate([
                jnp.broadcast_to(jnp.tile(attn_q_g[i], reps) * dh ** -0.5, (8, LANES)),
                jnp.broadcast_to(jnp.tile(attn_k_g[i], reps), (8, LANES)),
                jnp.ones((8, LANES), F32)], axis=0)
            qkv = _matmul(hn, full["attn_w_qkv"][i], name=f"attn_qkv_{layer}")
            qkv16 = _qkv_prep_fwd(qkv, gains, dh, name=f"qkv_prep_fwd_{layer}")
            o = _attn_fwd(qkv16, dh, name=f"attn_fwd_{layer}")
            sv.update(qkv=qkv, qkv16=qkv16, gains=gains, mix=o)
            xs = _matmul(o, full["attn_w_o"][i], add=xs, name=f"attn_out_{layer}")
        sv["x_mid"] = xs
        h2 = _rmsnorm_fwd(xs, ffn_norm_g[layer:layer + 1], name=f"ffn_norm_fwd_{layer}")
        w_up = full["ffn_w_up"][layer]
        ug = _matmul(h2, w_up[:, :f], name=f"ffn_up_gate_{layer}")
        uv = _matmul(h2, w_up[:, f:], name=f"ffn_up_val_{layer}")
        taps = full["ffn_dw_w"][layer]
        bias = ffn_dw_b[layer:layer + 1]
        act = _ffn_act_fwd(ug, uv, taps[:, :f], taps[:, f:], bias[:, :f], bias[:, f:], name=f"ffn_act_fwd_{layer}")
        sv.update(h2=h2, ug=ug, uv=uv, act=act)
        xs = _matmul(act, full["ffn_w_down"][layer], add=xs, name=f"ffn_down_{layer}")
        saved.append(sv)

    sq, dx, dx16 = _loss_head(xs, target, name="loss_head")
    loss = lax.psum(0.5 * sq[0, 0] / d, ("x", "y", "c"))

    grads = {n: [None] * weights[n].shape[0] for n in _WEIGHT_ORDER}
    shard_axis = dict(_MATMUL_WEIGHTS)
    queued, arrived = [], {}

    def queue(n, idx, g):
        a, b = weights[n].shape[1:]
        slabs = _split_cols(g, name=f"split_{n}_{idx}") if shard_axis[n] == 2 else g.reshape(N_DEV, a, b)
        queued.append(((n, idx), slabs))

    def carried(call):
        keys = [k for k, _ in queued]
        slabs = [sl for _, sl in queued]
        queued.clear()
        outs, received = call(slabs)
        arrived.update(zip(keys, received))
        return outs

    for layer in reversed(range(depth)):
        i = layer // 2
        sv = saved[layer]
        w_up = full["ffn_w_up"][layer]
        taps = full["ffn_dw_w"][layer]
        bias = ffn_dw_b[layer:layer + 1]
        dact = _matmul(dx16, full["ffn_w_down"][layer], tb=True, name=f"ffn_down_dx_{layer}")
        queue("ffn_w_down", layer, _weight_grad(sv["act"], dx16, name=f"ffn_down_dw_{layer}"))
        dug, duv, sumg, sumv = carried(lambda cargo: _ffn_act_bwd(
            sv["ug"], sv["uv"], dact, taps[:, :f], taps[:, f:], bias[:, :f], bias[:, f:], cargo=cargo,
            name=f"ffn_act_bwd_{layer}"))
        sums = jnp.concatenate([sumg, sumv], axis=1).reshape(4, 8, 2 * f).sum(axis=1)
        grads["ffn_dw_w"][layer] = sums[:3]
        grads["ffn_dw_b"][layer] = sums[3]
        dh2 = _matmul(dug, w_up[:, :f], tb=True, name=f"ffn_up_gate_dx_{layer}")
        dh2 = _matmul(duv, w_up[:, f:], tb=True, add=dh2, name=f"ffn_up_val_dx_{layer}")
        queue("ffn_w_up", layer, jnp.concatenate(
            [_weight_grad(sv["h2"], dug, name=f"ffn_up_gate_dw_{layer}"),
             _weight_grad(sv["h2"], duv, name=f"ffn_up_val_dw_{layer}")], axis=1))
        dx, dx16, dg = _rmsnorm_bwd(sv["x_mid"], ffn_norm_g[layer:layer + 1], dh2, dx, name=f"ffn_norm_bwd_{layer}")
        grads["ffn_norm_g"][layer] = dg[0]

        if layer % 2 == 0:
            dab = _matmul(dx16, full["conv_w_out"][i], tb=True, name=f"conv_out_dx_{layer}")
            queue("conv_w_out", i, _weight_grad(sv["mix"], dx16, name=f"conv_out_dw_{layer}"))
            ka = full["conv_a_dw_w"].shape[1]
            kb = full["conv_b_dw_w"].shape[1]
            dp, dwa, dba, dlng, dlnb, dwb = carried(lambda cargo: _conv_act_bwd(
                sv["p"], sv["a1"], dab, full["conv_a_dw_w"][i], conv_a_ln_g[i:i + 1], conv_a_ln_b[i:i + 1],
                full["conv_b_dw_w"][i], cargo=cargo, name=f"conv_act_bwd_{layer}"))
            grads["conv_a_dw_w"][i] = dwa[:ka]
            grads["conv_a_dw_b"][i] = dba[0]
            grads["conv_a_ln_g"][i] = dlng[0]
            grads["conv_a_ln_b"][i] = dlnb[0]
            grads["conv_b_dw_w"][i] = dwb[:kb]
            dhn = _matmul(dp, full["conv_w_in"][i], tb=True, name=f"conv_in_dx_{layer}")
            queue("conv_w_in", i, _weight_grad(sv["h"], dp, name=f"conv_in_dw_{layer}"))
        else:
            do = _matmul(dx16, full["attn_w_o"][i], tb=True, name=f"attn_out_dx_{layer}")
            queue("attn_w_o", i, _weight_grad(sv["mix"], dx16, name=f"attn_out_dw_{layer}"))
            dq, dk, dv = carried(lambda cargo: _attn_bwd(sv["qkv16"], sv["mix"], do, dh, cargo=cargo,
                                                         name=f"attn_bwd_{layer}"))
            dqkv, dgains = _qkv_prep_bwd(sv["qkv"], dq, dk, dv, sv["gains"], dh, name=f"qkv_prep_bwd_{layer}")
            grads["attn_q_g"][i] = dgains[0].reshape(-1, dh).sum(axis=0) * dh ** -0.5
            grads["attn_k_g"][i] = dgains[8].reshape(-1, dh).sum(axis=0)
            dhn = _matmul(dqkv, full["attn_w_qkv"][i], tb=True, name=f"attn_qkv_dx_{layer}")
            queue("attn_w_qkv", i, _weight_grad(sv["h"], dqkv, name=f"attn_qkv_dw_{layer}"))
        dx, dx16, dg = _rmsnorm_bwd(sv["x_in"], mix_norm_g[layer:layer + 1], dhn, dx, name=f"mix_norm_bwd_{layer}")
        grads["mix_norm_g"][layer] = dg[0]
    carried(lambda cargo: ((), _exchange(cargo, "exchange_last_grads")))

    me = _my_index()
    final_grads, delta, new_m, new_v = {}, {}, {}, {}
    for n, _ in _MATMUL_WEIGHTS:
        per_layer = [_adamw_sum(weights[n][idx], arrived[(n, idx)], mom_m[n][idx], mom_v[n][idx],
                                name=f"adamw_{n}_{idx}") for idx in range(weights[n].shape[0])]
        final_grads[n], delta[n], new_m[n], new_v[n] = [jnp.stack(outs, axis=0) for outs in zip(*per_layer)]

    small_names = list(_REPLICATED) + [n for n, _ in _TAP_WEIGHTS]
    local = {n: jnp.stack(grads[n], axis=0) for n in small_names}
    small_shapes = [local[n].shape for n in small_names]
    g_small = _sum_slabs(_all_gather([_pack([local[n] for n in small_names], F32)], "gather_small_grads")[0],
                         name="sum_small_grads")
    reduced = dict(zip(small_names, _unpack(g_small, small_shapes)))
    for n in _REPLICATED:
        final_grads[n] = reduced[n]
    for n, axis in _TAP_WEIGHTS:
        final_grads[n] = lax.dynamic_index_in_dim(_split_shards(reduced[n], axis), me, axis=0, keepdims=False)
    shapes = [weights[n].shape for n in small_names]
    d_s, m_s, v_s = _adamw(
        _pack([weights[n] for n in small_names], F32), _pack([final_grads[n] for n in small_names], F32),
        _pack([mom_m[n] for n in small_names], F32), _pack([mom_v[n] for n in small_names], F32), name="adamw_small")
    delta.update(zip(small_names, _unpack(d_s, shapes)))
    new_m.update(zip(small_names, _unpack(m_s, shapes)))
    new_v.update(zip(small_names, _unpack(v_s, shapes)))

    order = list(_WEIGHT_ORDER)
    return (loss, dx.reshape(x.shape), *[final_grads[n] for n in order], *[delta[n] for n in order],
            *[new_m[n] for n in order], *[new_v[n] for n in order])
```

```python
import jax
import jax.numpy as jnp
from jax import lax
from jax.experimental import pallas as pl
from jax.experimental.pallas import tpu as pltpu

F32 = jnp.float32
BF16 = jnp.bfloat16
EPS = 1e-6
N_DEV = 8
MESH_ID = pl.DeviceIdType.MESH

ADAM_LR = 0.001
ADAM_B1 = 0.9
ADAM_B2 = 0.999
ADAM_EPS = 1e-08
ADAM_WD = 0.01
ADAM_STEP = 10

V7X_VMEM_LIMIT_BYTES = 48 * 1024 * 1024
PACK_QUANTUM = 2048
A_HALO = 32
S_HALO = 8
LANES = 128
ATTN_Q_BLOCK = 256
ATTN_K_BLOCK = 256
MATMUL_MAX_SINGLE_K = 3072
FFN_CHUNK_ROWS = 64
ATTN_DEAD_LOG_WEIGHT = -110.0


def _tile(n, prefs):
    for p in prefs:
        if n % p == 0:
            return p
    return n


def _params(sem=None):
    return pltpu.CompilerParams(dimension_semantics=sem, vmem_limit_bytes=V7X_VMEM_LIMIT_BYTES)


def _sigmoid(x):
    return 1.0 / (1.0 + jnp.exp(-x))


def _dot(a, b, ca, cb):
    return lax.dot_general(a, b, (((ca,), (cb,)), ((), ())), preferred_element_type=F32)


def _my_index():
    return 4 * lax.axis_index("x") + 2 * lax.axis_index("y") + lax.axis_index("c")


def _all_gather(shards, name):
    n = len(shards)

    def body(*refs):
        x_refs, out_refs = refs[:n], refs[n:2 * n]
        send_sems, recv_sems, local_sems = refs[2 * n:]
        x, y, c = lax.axis_index("x"), lax.axis_index("y"), lax.axis_index("c")
        me, sibling = (x, y, c), (x, y, 1 - c)
        chips = [(1 - x, y), (x, 1 - y), (1 - x, 1 - y)]

        def copy(a, k, block, to, src=None):
            slot = out_refs[a].at[4 * block[0] + 2 * block[1] + block[2]]
            return pltpu.make_async_remote_copy(
                src_ref=slot if src is None else src, dst_ref=slot,
                send_sem=send_sems.at[7 * a + k], recv_sem=recv_sems.at[7 * a + k],
                device_id=to, device_id_type=MESH_ID)

        mine = [pltpu.make_async_copy(x_refs[a], out_refs[a].at[4 * x + 2 * y + c], local_sems.at[a])
                for a in range(n)]
        for cp in mine:
            cp.start()
        first = []
        for a in range(n):
            first.append(copy(a, 0, me, sibling, src=x_refs[a]))
            first += [copy(a, 1 + j, me, (*chip, c), src=x_refs[a]) for j, chip in enumerate(chips)]
        for cp in first:
            cp.start()
        passed = []
        for j, chip in enumerate(chips):
            for a in range(n):
                copy(a, 1 + j, (*chip, c), me).wait_recv()
                fwd = copy(a, 4 + j, (*chip, c), sibling)
                fwd.start()
                passed.append(fwd)
        for a in range(n):
            copy(a, 0, sibling, me).wait_recv()
            for j, chip in enumerate(chips):
                copy(a, 4 + j, (*chip, 1 - c), me).wait_recv()
        for cp in first + passed:
            cp.wait_send()
        for cp in mine:
            cp.wait()

    hbm = pl.BlockSpec(memory_space=pl.ANY)
    return pl.pallas_call(
        body,
        name=name,
        out_shape=[jax.ShapeDtypeStruct((N_DEV,) + sh.shape, sh.dtype) for sh in shards],
        in_specs=[hbm] * n,
        out_specs=[hbm] * n,
        scratch_shapes=[
            pltpu.SemaphoreType.DMA((7 * n,)),
            pltpu.SemaphoreType.DMA((7 * n,)),
            pltpu.SemaphoreType.DMA((n,)),
        ],
    )(*shards)


def _exchange(slabs, name):
    n = len(slabs)

    def body(*refs):
        copies = _exchange_copies(refs[:n], refs[n:2 * n], *refs[2 * n:])
        _exchange_start(copies)
        _exchange_wait(copies)

    hbm = pl.BlockSpec(memory_space=pl.ANY)
    return pl.pallas_call(
        body,
        name=name,
        out_shape=[jax.ShapeDtypeStruct(g.shape, g.dtype) for g in slabs],
        in_specs=[hbm] * n,
        out_specs=[hbm] * n,
        scratch_shapes=_exchange_semaphores(n),
    )(*slabs)


def _exchange_semaphores(n):
    return [pltpu.SemaphoreType.DMA((7 * n,)), pltpu.SemaphoreType.DMA((7 * n,)), pltpu.SemaphoreType.DMA((n,))]


def _exchange_copies(g_refs, out_refs, send_sems, recv_sems, local_sems, gather=False):
    n = len(g_refs)
    x, y, c = lax.axis_index("x"), lax.axis_index("y"), lax.axis_index("c")
    me = 4 * x + 2 * y + c

    def part(a, d):
        return g_refs[a] if gather else g_refs[a].at[d]

    local = [pltpu.make_async_copy(part(a, me), out_refs[a].at[me], local_sems.at[a]) for a in range(n)]
    remote = []
    for k in range(1, N_DEV):
        px = (x + ((k >> 2) & 1)) % 2
        py = (y + ((k >> 1) & 1)) % 2
        pc = (c + (k & 1)) % 2
        for a in range(n):
            remote.append(pltpu.make_async_remote_copy(
                src_ref=part(a, 4 * px + 2 * py + pc), dst_ref=out_refs[a].at[me],
                send_sem=send_sems.at[7 * a + k - 1], recv_sem=recv_sems.at[7 * a + k - 1],
                device_id=(px, py, pc), device_id_type=MESH_ID))
    return local, remote


def _exchange_start(copies):
    for cp in copies[0] + copies[1]:
        cp.start()


def _exchange_wait(copies):
    local, remote = copies
    for cp in remote:
        cp.wait_recv()
    for cp in remote:
        cp.wait_send()
    for cp in local:
        cp.wait()


def _carrying_call(body, cargo, *, name, grid, in_specs, out_specs, out_shape, scratch_shapes, operands,
                   compiler_params, gather=False):
    n = len(cargo)
    n_in, n_out, n_scr = len(in_specs), len(out_specs), len(scratch_shapes)
    if n == 0:
        return pl.pallas_call(body, name=name, grid=grid, in_specs=in_specs, out_specs=out_specs,
                              out_shape=out_shape, scratch_shapes=scratch_shapes,
                              compiler_params=compiler_params)(*operands), []

    def carrying(*refs):
        cuts = [0, n_in, n_in + n, n_in + n + n_out, n_in + 2 * n + n_out, n_in + 2 * n + n_out + n_scr, len(refs)]
        ins, g_refs, outs, r_refs, scr, sems = [refs[a:b] for a, b in zip(cuts[:-1], cuts[1:])]
        steps = [pl.program_id(ax) for ax in range(len(grid))]
        first = steps[0] == 0
        last = steps[0] == grid[0] - 1
        for ax in range(1, len(grid)):
            first = first & (steps[ax] == 0)
            last = last & (steps[ax] == grid[ax] - 1)

        @pl.when(first)
        def _():
            _exchange_start(_exchange_copies(g_refs, r_refs, *sems, gather=gather))

        body(*ins, *outs, *scr)

        @pl.when(last)
        def _():
            _exchange_wait(_exchange_copies(g_refs, r_refs, *sems, gather=gather))

    hbm = pl.BlockSpec(memory_space=pl.ANY)
    lead = (N_DEV,) if gather else ()
    res = pl.pallas_call(
        carrying,
        name=name,
        grid=grid,
        in_specs=list(in_specs) + [hbm] * n,
        out_specs=list(out_specs) + [hbm] * n,
        out_shape=list(out_shape) + [jax.ShapeDtypeStruct(lead + g.shape, g.dtype) for g in cargo],
        scratch_shapes=list(scratch_shapes) + _exchange_semaphores(n),
        compiler_params=compiler_params,
    )(*operands, *cargo)
    return res[:n_out], res[n_out:]


def _assemble_cols(g, *, name):
    n, r, w = g.shape
    tr = _tile(r, (256, 128, 64, 32, 16))

    def body(g_ref, o_ref):
        for j in range(n):
            o_ref[:, j * w:(j + 1) * w] = g_ref[j]

    return pl.pallas_call(
        body,
        name=name,
        grid=(r // tr,),
        in_specs=[pl.BlockSpec((n, tr, w), lambda i: (0, i, 0))],
        out_specs=pl.BlockSpec((tr, n * w), lambda i: (i, 0)),
        out_shape=jax.ShapeDtypeStruct((r, n * w), g.dtype),
        compiler_params=_params(("parallel",)),
    )(g)


def _split_cols(x, *, name):
    r, nw = x.shape
    w = nw // N_DEV
    tr = _tile(r, (256, 128, 64, 32, 16))

    def body(x_ref, o_ref):
        for j in range(N_DEV):
            o_ref[j] = x_ref[:, j * w:(j + 1) * w]

    return pl.pallas_call(
        body,
        name=name,
        grid=(r // tr,),
        in_specs=[pl.BlockSpec((tr, nw), lambda i: (i, 0))],
        out_specs=pl.BlockSpec((N_DEV, tr, w), lambda i: (0, i, 0)),
        out_shape=jax.ShapeDtypeStruct((N_DEV, r, w), x.dtype),
        compiler_params=_params(("parallel",)),
    )(x)


def _padded(n):
    return -(-n // PACK_QUANTUM) * PACK_QUANTUM


def _pack(parts, dtype, lead=()):
    flat = []
    for p in parts:
        p = p.astype(dtype).reshape(lead + (-1,))
        n = p.shape[-1]
        flat.append(jnp.pad(p, [(0, 0)] * len(lead) + [(0, _padded(n) - n)]))
    return jnp.concatenate(flat, axis=-1).reshape(lead + (-1, 128))


def _unpack(buf, shapes, lead=()):
    flat = buf.reshape(lead + (-1,))
    out, off = [], 0
    for shp in shapes:
        n = 1
        for d in shp:
            n *= d
        out.append(flat[..., off:off + n].reshape(lead + tuple(shp)))
        off += _padded(n)
    return out


def _half_if_tiled(n):
    half = n // 2
    return (half,) if n % 2 == 0 and half % LANES == 0 and half <= 1536 else ()


def _matmul(a, b, *, ta=False, tb=False, out_dtype=F32, add=None, gather_cargo=None, name):
    if ta:
        kdim, m = a.shape
    else:
        m, kdim = a.shape
    n = b.shape[0] if tb else b.shape[1]
    bm = _tile(m, (1024,) + _half_if_tiled(m) + (512, 256, 128))
    bn = _tile(n, (512,) + _half_if_tiled(n) + (256, 128))
    bk = kdim if kdim <= MATMUL_MAX_SINGLE_K else _tile(kdim, (1024, 512, 256, 128))
    nk = kdim // bk

    a_spec = pl.BlockSpec((bk, bm), lambda i, j, k: (k, i)) if ta else pl.BlockSpec((bm, bk), lambda i, j, k: (i, k))
    b_spec = pl.BlockSpec((bn, bk), lambda i, j, k: (j, k)) if tb else pl.BlockSpec((bk, bn), lambda i, j, k: (k, j))
    o_spec = pl.BlockSpec((bm, bn), lambda i, j, k: (i, j))
    in_specs = [a_spec, b_spec] + ([o_spec] if add is not None else [])
    operands = [a, b] + ([add] if add is not None else [])

    def product(a_ref, b_ref):
        return _dot(a_ref[...].astype(BF16), b_ref[...].astype(BF16), 0 if ta else 1, 1 if tb else 0)

    def finish(r, refs):
        if add is not None:
            r = r + refs[2][...]
        return r

    def body_single(*refs):
        o_ref = refs[-1]
        o_ref[...] = finish(product(refs[0], refs[1]), refs).astype(o_ref.dtype)

    def body_multi(*refs):
        o_ref, acc_ref = refs[-2], refs[-1]
        k = pl.program_id(2)

        @pl.when(k == 0)
        def _():
            acc_ref[...] = jnp.zeros_like(acc_ref)

        acc_ref[...] += product(refs[0], refs[1])

        @pl.when(k == nk - 1)
        def _():
            o_ref[...] = finish(acc_ref[...], refs).astype(o_ref.dtype)

    outs, received = _carrying_call(
        body_single if nk == 1 else body_multi,
        () if gather_cargo is None else gather_cargo,
        gather=True,
        name=name,
        grid=(m // bm, n // bn, nk),
        in_specs=in_specs,
        out_specs=[o_spec],
        out_shape=[jax.ShapeDtypeStruct((m, n), out_dtype)],
        scratch_shapes=[] if nk == 1 else [pltpu.VMEM((bm, bn), F32)],
        operands=operands,
        compiler_params=_params(("parallel", "parallel", "arbitrary") if gather_cargo is None else ("arbitrary",) * 3),
    )
    return outs[0] if gather_cargo is None else (outs[0], received)


def _weight_grad(x, dy, *, name):
    return _matmul(x, dy, ta=True, out_dtype=BF16, name=name)


def _rmsnorm_fwd(x, g, *, name):
    r, d = x.shape
    tr = _tile(r, (1024, 512, 256, 128))

    def body(x_ref, g_ref, o_ref):
        xv = x_ref[...]
        y = xv * lax.rsqrt(jnp.mean(xv * xv, axis=-1, keepdims=True) + EPS)
        y = y * g_ref[...]
        o_ref[...] = y.astype(o_ref.dtype)

    return pl.pallas_call(
        body,
        name=name,
        grid=(r // tr,),
        in_specs=[pl.BlockSpec((tr, d), lambda i: (i, 0)), pl.BlockSpec((1, d), lambda i: (0, 0))],
        out_specs=pl.BlockSpec((tr, d), lambda i: (i, 0)),
        out_shape=jax.ShapeDtypeStruct((r, d), BF16),
        compiler_params=_params(("parallel",)),
    )(x, g)


def _rmsnorm_bwd(x, g, dy, res, *, name):
    r, d = x.shape
    tr = _tile(r, (512, 256, 128))
    row = pl.BlockSpec((tr, d), lambda i: (i, 0))
    vec = pl.BlockSpec((1, d), lambda i: (0, 0))
    operands = [x, g, dy] + ([res] if res is not None else [])

    def body(*refs):
        if res is not None:
            x_ref, g_ref, dy_ref, res_ref, dx_ref, dx16_ref, dg_ref = refs
        else:
            x_ref, g_ref, dy_ref, dx_ref, dx16_ref, dg_ref = refs
        xv = x_ref[...]
        dyv = dy_ref[...].astype(F32)
        rs = lax.rsqrt(jnp.mean(xv * xv, axis=-1, keepdims=True) + EPS)
        xn = xv * rs
        dyg = dyv * g_ref[...]
        dx = rs * (dyg - xn * jnp.mean(dyg * xn, axis=-1, keepdims=True))
        if res is not None:
            dx = dx + res_ref[...]
        dx_ref[...] = dx
        dx16_ref[...] = dx.astype(dx16_ref.dtype)

        @pl.when(pl.program_id(0) == 0)
        def _():
            dg_ref[...] = jnp.zeros_like(dg_ref)

        dg_ref[...] += jnp.sum(dyv * xn, axis=0, keepdims=True)

    return pl.pallas_call(
        body,
        name=name,
        grid=(r // tr,),
        in_specs=[row, vec, row] + ([row] if res is not None else []),
        out_specs=(row, row, vec),
        out_shape=(jax.ShapeDtypeStruct((r, d), F32), jax.ShapeDtypeStruct((r, d), BF16),
                   jax.ShapeDtypeStruct((1, d), F32)),
        compiler_params=_params(("arbitrary",)),
    )(*operands)


def _loss_head(y, target, *, name):
    s, d = y.shape
    ts = _tile(s, (512, 256, 128))
    row = pl.BlockSpec((ts, d), lambda i: (i, 0))
    acc = pl.BlockSpec((8, 128), lambda i: (0, 0))

    def body(y_ref, t_ref, l_ref, dy_ref, dy16_ref):
        e = y_ref[...] - t_ref[...]
        dy = e * (1.0 / d)
        dy_ref[...] = dy
        dy16_ref[...] = dy.astype(dy16_ref.dtype)

        @pl.when(pl.program_id(0) == 0)
        def _():
            l_ref[...] = jnp.zeros_like(l_ref)

        sq = jnp.sum(jnp.sum(e * e, axis=-1, keepdims=True), axis=0, keepdims=True)
        l_ref[...] += jnp.broadcast_to(sq, l_ref.shape)

    return pl.pallas_call(
        body,
        name=name,
        grid=(s // ts,),
        in_specs=[row, row],
        out_specs=(acc, row, row),
        out_shape=(jax.ShapeDtypeStruct((8, 128), F32), jax.ShapeDtypeStruct((s, d), F32),
                   jax.ShapeDtypeStruct((s, d), BF16)),
        compiler_params=_params(("arbitrary",)),
    )(y, target)


def _ffn_tiles(s, f):
    return _tile(s, (512, 256, 128)), _tile(f, _half_if_tiled(f) + (512, 256, 128))


def _conv3(xs, w, b):
    return b + xs[0] * w[0:1] + xs[1] * w[1:2] + xs[2] * w[2:3]


def _shifted3(x, rows):
    return [x[S_HALO - 2 + k:S_HALO - 2 + k + rows] for k in range(3)]


def _ffn_act_fwd(ug, uv, wg, wv, bg, bv, *, gather_cargo=(), name):
    s, f = ug.shape
    ts, tc = _ffn_tiles(s, f)
    rc = _tile(ts, (FFN_CHUNK_ROWS, 32, 16, 8))
    hb = ts // S_HALO
    main = pl.BlockSpec((ts, tc), lambda j, i: (i, j))
    prev = pl.BlockSpec((S_HALO, tc), lambda j, i: (jnp.maximum(i * hb - 1, 0), j))
    taps = pl.BlockSpec((3, tc), lambda j, i: (0, j))
    bias = pl.BlockSpec((1, tc), lambda j, i: (0, j))

    def body(ug_ref, ugp_ref, uv_ref, uvp_ref, wg_ref, wv_ref, bg_ref, bv_ref, o_ref, gbuf, vbuf):
        first = pl.program_id(1) == 0
        for buf, p_ref, m_ref in ((gbuf, ugp_ref, ug_ref), (vbuf, uvp_ref, uv_ref)):
            buf[0:S_HALO, :] = jnp.where(first, 0.0, p_ref[...])
            buf[S_HALO:, :] = m_ref[...]

        def chunk(ci, carry):
            r0 = pl.multiple_of(ci * rc, rc)
            for lb in range(tc // LANES):
                ls = pl.ds(lb * LANES, LANES)
                gate = _conv3(_shifted3(gbuf[pl.ds(r0, rc + S_HALO), ls], rc), wg_ref[:, ls], bg_ref[:, ls])
                val = _conv3(_shifted3(vbuf[pl.ds(r0, rc + S_HALO), ls], rc), wv_ref[:, ls], bv_ref[:, ls])
                o_ref[pl.ds(r0, rc), ls] = (gate * _sigmoid(gate) * val).astype(o_ref.dtype)
            return carry

        lax.fori_loop(0, ts // rc, chunk, 0)

    outs, gathered = _carrying_call(
        body,
        gather_cargo,
        gather=True,
        name=name,
        grid=(f // tc, s // ts),
        in_specs=[main, prev, main, prev, taps, taps, bias, bias],
        out_specs=[main],
        out_shape=[jax.ShapeDtypeStruct((s, f), BF16)],
        scratch_shapes=[pltpu.VMEM((S_HALO + ts, tc), F32), pltpu.VMEM((S_HALO + ts, tc), F32)],
        operands=(ug, ug, uv, uv, wg, wv, bg, bv),
        compiler_params=_params(("arbitrary", "arbitrary")),
    )
    return outs[0], gathered


def _ffn_act_bwd(ug, uv, df, wg, wv, bg, bv, *, cargo=(), name):
    s, f = ug.shape
    ts, tc = _ffn_tiles(s, f)
    rc = _tile(ts, (FFN_CHUNK_ROWS, 32, 16, 8))
    hb = ts // S_HALO
    last_hb = s // S_HALO - 1
    n_row = s // ts
    main = pl.BlockSpec((ts, tc), lambda j, i: (i, j))
    prev = pl.BlockSpec((S_HALO, tc), lambda j, i: (jnp.maximum(i * hb - 1, 0), j))
    nxt = pl.BlockSpec((S_HALO, tc), lambda j, i: (jnp.minimum((i + 1) * hb, last_hb), j))
    taps = pl.BlockSpec((3, tc), lambda j, i: (0, j))
    bias = pl.BlockSpec((1, tc), lambda j, i: (0, j))
    sums = pl.BlockSpec((32, tc), lambda j, i: (0, j))
    ext = rc + S_HALO

    def body(ug_ref, ugp_ref, ugn_ref, uv_ref, uvp_ref, uvn_ref, df_ref, dfn_ref, wg_ref, wv_ref, bg_ref, bv_ref,
             dug_ref, duv_ref, sumg_ref, sumv_ref, gbuf, vbuf, dfbuf):
        i = pl.program_id(1)
        first = i == 0
        last = i == n_row - 1

        @pl.when(first)
        def _():
            sumg_ref[...] = jnp.zeros_like(sumg_ref)
            sumv_ref[...] = jnp.zeros_like(sumv_ref)

        for buf, p_ref, m_ref, n_ref in ((gbuf, ugp_ref, ug_ref, ugn_ref), (vbuf, uvp_ref, uv_ref, uvn_ref)):
            buf[0:S_HALO, :] = jnp.where(first, 0.0, p_ref[...])
            buf[S_HALO:S_HALO + ts, :] = m_ref[...]
            buf[S_HALO + ts:, :] = n_ref[...]
        dfbuf[0:ts, :] = df_ref[...]
        dfbuf[ts:, :] = jnp.where(last, 0.0, dfn_ref[...])

        def chunk(ci, carry):
            r0 = pl.multiple_of(ci * rc, rc)
            for lb in range(tc // LANES):
                ls = pl.ds(lb * LANES, LANES)
                xg = _shifted3(gbuf[pl.ds(r0, ext + S_HALO), ls], ext)
                xv = _shifted3(vbuf[pl.ds(r0, ext + S_HALO), ls], ext)
                wgv, wvv = wg_ref[:, ls], wv_ref[:, ls]
                gate = _conv3(xg, wgv, bg_ref[:, ls])
                val = _conv3(xv, wvv, bv_ref[:, ls])
                dfe = dfbuf[pl.ds(r0, ext), ls]
                sg = _sigmoid(gate)
                dgate = dfe * val * (sg * (1.0 + gate * (1.0 - sg)))
                dval = dfe * (gate * sg)
                for dz, xs, w, du_ref, sum_ref in ((dgate, xg, wgv, dug_ref, sumg_ref),
                                                   (dval, xv, wvv, duv_ref, sumv_ref)):
                    du = dz[2:2 + rc] * w[0:1] + dz[1:1 + rc] * w[1:2] + dz[0:rc] * w[2:3]
                    du_ref[pl.ds(r0, rc), ls] = du.astype(du_ref.dtype)
                    dm = dz[0:rc]
                    for k in range(3):
                        sum_ref[8 * k:8 * k + 8, ls] += (dm * xs[k][0:rc]).reshape(rc // 8, 8, LANES).sum(axis=0)
                    sum_ref[24:32, ls] += dm.reshape(rc // 8, 8, LANES).sum(axis=0)
            return carry

        lax.fori_loop(0, ts // rc, chunk, 0)

    return _carrying_call(
        body,
        cargo,
        name=name,
        grid=(f // tc, n_row),
        in_specs=[main, prev, nxt, main, prev, nxt, main, nxt, taps, taps, bias, bias],
        out_specs=(main, main, sums, sums),
        out_shape=(jax.ShapeDtypeStruct((s, f), BF16), jax.ShapeDtypeStruct((s, f), BF16),
                   jax.ShapeDtypeStruct((32, f), F32), jax.ShapeDtypeStruct((32, f), F32)),
        scratch_shapes=[pltpu.VMEM((2 * S_HALO + ts, tc), F32), pltpu.VMEM((2 * S_HALO + ts, tc), F32),
                        pltpu.VMEM((S_HALO + ts, tc), F32)],
        operands=(ug, ug, ug, uv, uv, uv, df, df, wg, wv, bg, bv),
        compiler_params=_params(("arbitrary", "arbitrary")),
    )


def _layernorm_stats(a1):
    mu = jnp.mean(a1, axis=-1, keepdims=True)
    xc = a1 - mu
    rstd = lax.rsqrt(jnp.mean(xc * xc, axis=-1, keepdims=True) + EPS)
    return xc * rstd, rstd


def _conv_act_fwd(p, wa, ba, lng, lnb, wb, *, gather_cargo=(), name):
    s, c5 = p.shape
    c = c5 // 5
    ka, kb = wa.shape[0], wb.shape[0]
    ts = _tile(s, (256, 128))
    hb = ts // A_HALO
    main = pl.BlockSpec((ts, c5), lambda i: (i, 0))
    prev = pl.BlockSpec((A_HALO, c5), lambda i: (jnp.maximum(i * hb - 1, 0), 0))

    def full(arr):
        return pl.BlockSpec(arr.shape, lambda i: (0, 0))

    def body(p_ref, pp_ref, wa_ref, ba_ref, lng_ref, lnb_ref, wb_ref, ab_ref, a1_ref, gbuf, cbuf):
        first = pl.program_id(0) == 0
        gbuf[0:A_HALO, :] = jnp.where(first, 0.0, pp_ref[:, 0:c] * _sigmoid(pp_ref[:, c:2 * c]))
        gbuf[A_HALO:, :] = p_ref[:, 0:c] * _sigmoid(p_ref[:, c:2 * c])
        cbuf[0:A_HALO, :] = jnp.where(first, 0.0, pp_ref[:, 3 * c:4 * c] * pp_ref[:, 4 * c:5 * c])
        cbuf[A_HALO:, :] = p_ref[:, 3 * c:4 * c] * p_ref[:, 4 * c:5 * c]

        a1 = jnp.broadcast_to(ba_ref[...], (ts, c))
        for k in range(ka):
            off = A_HALO - (ka - 1) + k
            a1 = a1 + gbuf[off:off + ts, :] * wa_ref[k:k + 1, :]
        a1_ref[...] = a1
        xhat, _ = _layernorm_stats(a1)
        a2 = xhat * lng_ref[...] + lnb_ref[...]
        ab_ref[:, 0:c] = (a2 * _sigmoid(a2)).astype(ab_ref.dtype)

        cv = jnp.zeros((ts, c), F32)
        for k in range(kb):
            off = A_HALO - (kb - 1) + k
            cv = cv + cbuf[off:off + ts, :] * wb_ref[k:k + 1, :]
        ab_ref[:, c:2 * c] = (p_ref[:, 2 * c:3 * c] * cv).astype(ab_ref.dtype)

    return _carrying_call(
        body,
        gather_cargo,
        gather=True,
        name=name,
        grid=(s // ts,),
        in_specs=[main, prev, full(wa), full(ba), full(lng), full(lnb), full(wb)],
        out_specs=(pl.BlockSpec((ts, 2 * c), lambda i: (i, 0)), pl.BlockSpec((ts, c), lambda i: (i, 0))),
        out_shape=(jax.ShapeDtypeStruct((s, 2 * c), BF16), jax.ShapeDtypeStruct((s, c), F32)),
        scratch_shapes=[pltpu.VMEM((A_HALO + ts, c), F32), pltpu.VMEM((A_HALO + ts, c), F32)],
        operands=(p, p, wa, ba, lng, lnb, wb),
        compiler_params=_params(("arbitrary",)),
    )


def _conv_act_bwd(p, a1, dab, wa, lng, lnb, wb, *, cargo=(), name):
    s, c5 = p.shape
    c = c5 // 5
    ka, kb = wa.shape[0], wb.shape[0]
    ts = _tile(s, (256, 128))
    hb = ts // A_HALO
    n_row = s // ts
    last_hb = s // A_HALO - 1
    ext = ts + A_HALO

    def rows(width):
        return (pl.BlockSpec((ts, width), lambda i: (i, 0)),
                pl.BlockSpec((A_HALO, width), lambda i: (jnp.maximum(i * hb - 1, 0), 0)),
                pl.BlockSpec((A_HALO, width), lambda i: (jnp.minimum((i + 1) * hb, last_hb), 0)))

    p_main, p_prev, p_next = rows(c5)
    d_main, _, d_next = rows(2 * c)
    a_main, _, a_next = rows(c)

    def full(shape):
        return pl.BlockSpec(shape, lambda i: (0, 0))

    def body(p_ref, pp_ref, pn_ref, a1_ref, a1n_ref, d_ref, dn_ref, wa_ref, lng_ref, lnb_ref, wb_ref,
             dp_ref, dwa_ref, dba_ref, dlng_ref, dlnb_ref, dwb_ref, gbuf, cbuf, da1buf, dcvbuf):
        i = pl.program_id(0)
        first = i == 0
        last = i == n_row - 1

        @pl.when(first)
        def _():
            dwa_ref[...] = jnp.zeros_like(dwa_ref)
            dba_ref[...] = jnp.zeros_like(dba_ref)
            dlng_ref[...] = jnp.zeros_like(dlng_ref)
            dlnb_ref[...] = jnp.zeros_like(dlnb_ref)
            dwb_ref[...] = jnp.zeros_like(dwb_ref)

        sig_gate = _sigmoid(p_ref[:, c:2 * c])
        gbuf[0:A_HALO, :] = jnp.where(first, 0.0, pp_ref[:, 0:c] * _sigmoid(pp_ref[:, c:2 * c]))
        gbuf[A_HALO:, :] = p_ref[:, 0:c] * sig_gate
        cbuf[0:A_HALO, :] = jnp.where(first, 0.0, pp_ref[:, 3 * c:4 * c] * pp_ref[:, 4 * c:5 * c])
        cbuf[A_HALO:, :] = p_ref[:, 3 * c:4 * c] * p_ref[:, 4 * c:5 * c]

        def ln_back(a1v, dav):
            xhat, rstd = _layernorm_stats(a1v)
            a2 = xhat * lng_ref[...] + lnb_ref[...]
            sg = _sigmoid(a2)
            da2 = dav * (sg * (1.0 + a2 * (1.0 - sg)))
            dxh = da2 * lng_ref[...]
            da1 = rstd * (dxh - jnp.mean(dxh, axis=-1, keepdims=True)
                          - xhat * jnp.mean(dxh * xhat, axis=-1, keepdims=True))
            return da1, da2, xhat

        da1_m, da2_m, xhat_m = ln_back(a1_ref[...], d_ref[:, 0:c])
        da1_n, _, _ = ln_back(a1n_ref[...], dn_ref[:, 0:c])
        da1buf[0:ts, :] = da1_m
        da1buf[ts:, :] = jnp.where(last, 0.0, da1_n)
        dlng_ref[...] += jnp.sum(da2_m * xhat_m, axis=0, keepdims=True)
        dlnb_ref[...] += jnp.sum(da2_m, axis=0, keepdims=True)
        dba_ref[...] += jnp.sum(da1_m, axis=0, keepdims=True)

        dglu = jnp.zeros((ts, c), F32)
        for k in range(ka):
            off = (ka - 1) - k
            dglu = dglu + da1buf[off:off + ts, :] * wa_ref[k:k + 1, :]
            goff = A_HALO - (ka - 1) + k
            dwa_ref[k:k + 1, :] += jnp.sum(da1_m * gbuf[goff:goff + ts, :], axis=0, keepdims=True)
        dp_ref[:, 0:c] = (dglu * sig_gate).astype(dp_ref.dtype)
        dp_ref[:, c:2 * c] = (dglu * p_ref[:, 0:c] * sig_gate * (1.0 - sig_gate)).astype(dp_ref.dtype)

        cv = jnp.zeros((ts, c), F32)
        for k in range(kb):
            off = A_HALO - (kb - 1) + k
            cv = cv + cbuf[off:off + ts, :] * wb_ref[k:k + 1, :]
        db_m = d_ref[:, c:2 * c]
        dp_ref[:, 2 * c:3 * c] = (db_m * cv).astype(dp_ref.dtype)
        dcv_m = db_m * p_ref[:, 2 * c:3 * c]
        dcvbuf[0:ts, :] = dcv_m
        dcvbuf[ts:, :] = jnp.where(last, 0.0, dn_ref[:, c:2 * c] * pn_ref[:, 2 * c:3 * c])
        dbc = jnp.zeros((ts, c), F32)
        for k in range(kb):
            off = (kb - 1) - k
            dbc = dbc + dcvbuf[off:off + ts, :] * wb_ref[k:k + 1, :]
            coff = A_HALO - (kb - 1) + k
            dwb_ref[k:k + 1, :] += jnp.sum(dcv_m * cbuf[coff:coff + ts, :], axis=0, keepdims=True)
        dp_ref[:, 3 * c:4 * c] = (dbc * p_ref[:, 4 * c:5 * c]).astype(dp_ref.dtype)
        dp_ref[:, 4 * c:5 * c] = (dbc * p_ref[:, 3 * c:4 * c]).astype(dp_ref.dtype)

    return _carrying_call(
        body,
        cargo,
        name=name,
        grid=(n_row,),
        in_specs=[p_main, p_prev, p_next, a_main, a_next, d_main, d_next,
                  full(wa.shape), full(lng.shape), full(lnb.shape), full(wb.shape)],
        out_specs=(pl.BlockSpec((ts, c5), lambda i: (i, 0)), full((32, c)), full((1, c)), full((1, c)),
                   full((1, c)), full((8, c))),
        out_shape=(
            jax.ShapeDtypeStruct((s, c5), BF16), jax.ShapeDtypeStruct((32, c), F32),
            jax.ShapeDtypeStruct((1, c), F32), jax.ShapeDtypeStruct((1, c), F32),
            jax.ShapeDtypeStruct((1, c), F32), jax.ShapeDtypeStruct((8, c), F32),
        ),
        scratch_shapes=[
            pltpu.VMEM((A_HALO + ts, c), F32), pltpu.VMEM((A_HALO + ts, c), F32),
            pltpu.VMEM((ext, c), F32), pltpu.VMEM((ext, c), F32),
        ],
        operands=(p, p, p, a1, a1, dab, dab, wa, lng, lnb, wb),
        compiler_params=_params(("arbitrary",)),
    )


def _head_pair_geometry(width, dh):
    assert 2 * dh == LANES, "the attention kernels pair two heads in one 128-lane block"
    return width // (3 * LANES)


def _group_sum(v, lo):
    s_lo = jnp.sum(jnp.where(lo, v, 0.0), axis=-1, keepdims=True)
    s_hi = jnp.sum(jnp.where(lo, 0.0, v), axis=-1, keepdims=True)
    return jnp.where(lo, s_lo, s_hi)


def _qkv_prep_fwd(qkv, gains, dh, *, name):
    s, width = qkv.shape
    nsec = _head_pair_geometry(width, dh)
    tr = _tile(s, (256, 128))
    blk = pl.BlockSpec((tr, nsec * LANES), lambda i, sec: (i, sec))

    def body(x_ref, g_ref, o_ref):
        sec = pl.program_id(1)

        @pl.when(sec == 2)
        def _():
            o_ref[...] = x_ref[...].astype(o_ref.dtype)

        @pl.when(sec != 2)
        def _():
            lo = lax.broadcasted_iota(jnp.int32, (1, LANES), 1) < dh
            for lb in range(nsec):
                ls = pl.ds(lb * LANES, LANES)
                xv = x_ref[:, ls]
                rs = lax.rsqrt(_group_sum(xv * xv, lo) * (1.0 / dh) + EPS)
                o_ref[:, ls] = (xv * rs * g_ref[0:1, :]).astype(o_ref.dtype)

    return pl.pallas_call(
        body,
        name=name,
        grid=(s // tr, 3),
        in_specs=[blk, pl.BlockSpec((8, LANES), lambda i, sec: (sec, 0))],
        out_specs=blk,
        out_shape=jax.ShapeDtypeStruct((s, width), BF16),
        compiler_params=_params(("parallel", "parallel")),
    )(qkv, gains)


def _qkv_prep_bwd(qkv, dq, dk, dv, gains, dh, *, name):
    s, width = qkv.shape
    nsec = _head_pair_geometry(width, dh)
    tr = _tile(s, (256, 128))
    blk = pl.BlockSpec((tr, nsec * LANES), lambda sec, i: (i, sec))
    gspec = pl.BlockSpec((8, LANES), lambda sec, i: (sec, 0))

    def grad_spec(own):
        return pl.BlockSpec((tr, nsec * LANES), lambda sec, i: (jnp.where(sec == own, i, 0), 0))

    def body(x_ref, dq_ref, dk_ref, dv_ref, g_ref, o_ref, dg_ref):
        sec = pl.program_id(0)

        @pl.when(pl.program_id(1) == 0)
        def _():
            dg_ref[...] = jnp.zeros_like(dg_ref)

        def normed(d_ref):
            lo = lax.broadcasted_iota(jnp.int32, (1, LANES), 1) < dh
            dg = jnp.zeros((1, LANES), F32)
            for lb in range(nsec):
                ls = pl.ds(lb * LANES, LANES)
                xv = x_ref[:, ls]
                d = d_ref[:, ls]
                rs = lax.rsqrt(_group_sum(xv * xv, lo) * (1.0 / dh) + EPS)
                xn = xv * rs
                dyg = d * g_ref[0:1, :]
                dx = rs * (dyg - xn * (_group_sum(dyg * xn, lo) * (1.0 / dh)))
                o_ref[:, ls] = dx.astype(o_ref.dtype)
                dg = dg + jnp.sum(d * xn, axis=0, keepdims=True)
            dg_ref[...] += jnp.broadcast_to(dg, dg_ref.shape)

        @pl.when(sec == 0)
        def _():
            normed(dq_ref)

        @pl.when(sec == 1)
        def _():
            normed(dk_ref)

        @pl.when(sec == 2)
        def _():
            o_ref[...] = dv_ref[...].astype(o_ref.dtype)

    return pl.pallas_call(
        body,
        name=name,
        grid=(3, s // tr),
        in_specs=[blk, grad_spec(0), grad_spec(1), grad_spec(2), gspec],
        out_specs=(blk, gspec),
        out_shape=(jax.ShapeDtypeStruct((s, width), BF16), jax.ShapeDtypeStruct((24, LANES), F32)),
        compiler_params=_params(("arbitrary", "arbitrary")),
    )(qkv, dq, dk, dv, gains)


def _split_dot(v, tri):
    hi = v.astype(BF16)
    lo = (v - hi.astype(F32)).astype(BF16)
    return _dot(hi, tri, 1, 0) + _dot(lo, tri, 1, 0)


def _attn_scores(qm, kb, tri_gt, valid, r_run):
    z = _dot(qm, kb, 1, 1)
    zl = jnp.minimum(z, 0.0) - jnp.log(1.0 + jnp.exp(-jnp.abs(z)))
    lk = zl - z
    if valid is not None:
        lk = jnp.where(valid, lk, 0.0)
    after = _split_dot(lk, tri_gt)
    w = jnp.exp(zl + after + r_run)
    if valid is not None:
        w = jnp.where(valid, w, 0.0)
    return zl, lk, w


def _attn_alive(rr):
    return jnp.max(jnp.maximum(rr[0], rr[1])) > ATTN_DEAD_LOG_WEIGHT


def _attn_specs(s, width, dh, tq):
    nsec = _head_pair_geometry(width, dh)
    qspec = pl.BlockSpec((tq, LANES), lambda h, i: (i, h))
    kspec = pl.BlockSpec((s, LANES), lambda h, i: (0, nsec + h))
    vspec = pl.BlockSpec((s, LANES), lambda h, i: (0, 2 * nsec + h))
    return nsec, qspec, kspec, vspec


def _attn_masks(tq, tk, dh):
    lo = lax.broadcasted_iota(jnp.int32, (1, LANES), 1) < dh
    r_io = lax.broadcasted_iota(jnp.int32, (tk, tk), 0)
    c_io = lax.broadcasted_iota(jnp.int32, (tk, tk), 1)
    qrow = lax.broadcasted_iota(jnp.int32, (tq, tk), 0)
    kcol = lax.broadcasted_iota(jnp.int32, (tq, tk), 1)
    return lo, r_io, c_io, qrow, kcol


def _attn_fwd(qkv16, dh, *, gather_cargo=(), name):
    s, width = qkv16.shape
    tq = _tile(s, (ATTN_Q_BLOCK, ATTN_K_BLOCK))
    tk = _tile(tq, (ATTN_K_BLOCK,))
    nd = tq // tk
    nsec, qspec, kspec, vspec = _attn_specs(s, width, dh, tq)

    def body(q_ref, k_ref, v_ref, o_ref):
        i = pl.program_id(1)
        lo, r_io, c_io, qrow, kcol = _attn_masks(tq, tk, dh)
        q2 = q_ref[...]
        zq = jnp.zeros_like(q2)
        qs = (jnp.where(lo, q2, zq), jnp.where(lo, zq, q2))
        tri_gt = (r_io > c_io).astype(BF16)

        def block(j, carry, diag):
            rr, acc = carry
            start = pl.multiple_of(j * tk, tk)
            kb = k_ref[pl.ds(start, tk), :]
            vb = v_ref[pl.ds(start, tk), :]
            valid = None if diag is None else kcol + diag * tk < qrow
            outs, new_r = [], []
            for hh in range(2):
                _, lk, w = _attn_scores(qs[hh], kb, tri_gt, valid, rr[hh])
                outs.append(_dot(w.astype(BF16), vb, 1, 0))
                new_r.append(rr[hh] + jnp.sum(lk, axis=-1, keepdims=True))
            return tuple(new_r), acc + jnp.where(lo, outs[0], outs[1])

        zero = jnp.zeros((tq, 1), F32)
        carry = ((zero, zero), jnp.zeros((tq, LANES), F32))
        for dg in reversed(range(nd)):
            carry = block(i * nd + dg, carry, dg)
        below = i * nd

        def step(st):
            jj, _, cr = st
            cr = block(below - 1 - jj, cr, None)
            return jj + 1, _attn_alive(cr[0]), cr

        _, _, carry = lax.while_loop(lambda st: (st[0] < below) & st[1], step,
                                     (jnp.int32(0), _attn_alive(carry[0]), carry))
        o_ref[...] = carry[1]

    outs, gathered = _carrying_call(
        body,
        gather_cargo,
        gather=True,
        name=name,
        grid=(nsec, s // tq),
        in_specs=[qspec, kspec, vspec],
        out_specs=[qspec],
        out_shape=[jax.ShapeDtypeStruct((s, width // 3), F32)],
        scratch_shapes=[],
        operands=(qkv16, qkv16, qkv16),
        compiler_params=_params(("arbitrary", "arbitrary")),
    )
    return outs[0], gathered


def _attn_bwd(qkv16, o, do, dh, *, cargo=(), name):
    s, width = qkv16.shape
    tq = _tile(s, (ATTN_Q_BLOCK, ATTN_K_BLOCK))
    tk = _tile(tq, (ATTN_K_BLOCK,))
    nd = tq // tk
    nsec, qspec, kspec, vspec = _attn_specs(s, width, dh, tq)
    accspec = pl.BlockSpec((s, LANES), lambda h, i: (0, h))

    def body(q_ref, k_ref, v_ref, o_ref, do_ref, dq_ref, dk_ref, dv_ref):
        i = pl.program_id(1)

        @pl.when(i == 0)
        def _():
            dk_ref[...] = jnp.zeros_like(dk_ref)
            dv_ref[...] = jnp.zeros_like(dv_ref)

        lo, r_io, c_io, qrow, kcol = _attn_masks(tq, tk, dh)
        q2 = q_ref[...]
        zq = jnp.zeros_like(q2)
        qs = (jnp.where(lo, q2, zq), jnp.where(lo, zq, q2))
        do16 = do_ref[...].astype(BF16)
        dos = (jnp.where(lo, do16, zq), jnp.where(lo, zq, do16))
        prod = do16.astype(F32) * o_ref[...]
        deltas = (jnp.sum(jnp.where(lo, prod, 0.0), axis=-1, keepdims=True),
                  jnp.sum(jnp.where(lo, 0.0, prod), axis=-1, keepdims=True))
        tri_gt = (r_io > c_io).astype(BF16)
        tri_ge = (r_io >= c_io).astype(BF16)

        def block(j, carry, diag):
            rr, er, dq = carry
            start = pl.multiple_of(j * tk, tk)
            kb = k_ref[pl.ds(start, tk), :]
            vb = v_ref[pl.ds(start, tk), :]
            valid = None if diag is None else kcol + diag * tk < qrow
            new_r, new_e, dqs = [], [], []
            dk_blk = dv_blk = None
            for hh in range(2):
                zl, lk, w = _attn_scores(qs[hh], kb, tri_gt, valid, rr[hh])
                beta = jnp.exp(zl)
                w16 = w.astype(BF16)
                e = _dot(dos[hh], vb, 1, 1) * w16.astype(F32)
                e_before = deltas[hh] - er[hh] - _split_dot(e, tri_ge)
                dz = e - beta * (e + e_before)
                if valid is not None:
                    dz = jnp.where(valid, dz, 0.0)
                dz16 = dz.astype(BF16)
                dqs.append(_dot(dz16, kb, 1, 0))
                dkc = _dot(dz16, qs[hh], 0, 0)
                dvc = _dot(w16, dos[hh], 0, 0)
                dk_blk = dkc if dk_blk is None else dk_blk + dkc
                dv_blk = dvc if dv_blk is None else dv_blk + dvc
                new_r.append(rr[hh] + jnp.sum(lk, axis=-1, keepdims=True))
                new_e.append(er[hh] + jnp.sum(e, axis=-1, keepdims=True))
            dk_ref[pl.ds(start, tk), :] += dk_blk
            dv_ref[pl.ds(start, tk), :] += dv_blk
            return tuple(new_r), tuple(new_e), dq + jnp.where(lo, dqs[0], dqs[1])

        zero = jnp.zeros((tq, 1), F32)
        carry = ((zero, zero), (zero, zero), jnp.zeros((tq, LANES), F32))
        for dg in reversed(range(nd)):
            carry = block(i * nd + dg, carry, dg)
        below = i * nd

        def step(st):
            jj, _, cr = st
            cr = block(below - 1 - jj, cr, None)
            return jj + 1, _attn_alive(cr[0]), cr

        _, _, carry = lax.while_loop(lambda st: (st[0] < below) & st[1], step,
                                     (jnp.int32(0), _attn_alive(carry[0]), carry))
        dq_ref[...] = carry[2]

    shp = jax.ShapeDtypeStruct((s, width // 3), F32)
    return _carrying_call(
        body,
        cargo,
        name=name,
        grid=(nsec, s // tq),
        in_specs=[qspec, kspec, vspec, qspec, qspec],
        out_specs=(qspec, accspec, accspec),
        out_shape=(shp, shp, shp),
        scratch_shapes=[],
        operands=(qkv16, qkv16, qkv16, o, do),
        compiler_params=_params(("arbitrary", "arbitrary")),
    )


def _adam_update(wv, gv, mv, vv):
    c1 = 1.0 - ADAM_B1 ** ADAM_STEP
    c2 = 1.0 - ADAM_B2 ** ADAM_STEP
    nm = ADAM_B1 * mv + (1.0 - ADAM_B1) * gv
    nv = ADAM_B2 * vv + (1.0 - ADAM_B2) * (gv * gv)
    delta = -ADAM_LR * ((nm / c1) / (jnp.sqrt(nv / c2) + ADAM_EPS) + ADAM_WD * wv)
    return delta, nm, nv


def _sum_slabs(r, *, name):
    n, rows, lanes = r.shape
    tr = _tile(rows, (1024, 512, 256, 128, 64, 32, 16))

    def body(r_ref, o_ref):
        acc = r_ref[0]
        for d in range(1, n):
            acc = acc + r_ref[d]
        o_ref[...] = acc

    return pl.pallas_call(
        body,
        name=name,
        grid=(rows // tr,),
        in_specs=[pl.BlockSpec((n, tr, lanes), lambda i: (0, i, 0))],
        out_specs=pl.BlockSpec((tr, lanes), lambda i: (i, 0)),
        out_shape=jax.ShapeDtypeStruct((rows, lanes), F32),
        compiler_params=_params(("parallel",)),
    )(r)


def _adamw(w, g, m, v, *, name):
    rows, cols = w.shape
    tr = _tile(rows, (1024, 512, 256, 128, 64, 32, 16))
    spec = pl.BlockSpec((tr, cols), lambda i: (i, 0))

    def body(w_ref, g_ref, m_ref, v_ref, d_ref, nm_ref, nv_ref):
        d_ref[...], nm_ref[...], nv_ref[...] = _adam_update(w_ref[...], g_ref[...], m_ref[...], v_ref[...])

    shp = jax.ShapeDtypeStruct((rows, cols), F32)
    return pl.pallas_call(
        body,
        name=name,
        grid=(rows // tr,),
        in_specs=[spec, spec, spec, spec],
        out_specs=(spec, spec, spec),
        out_shape=(shp, shp, shp),
        compiler_params=_params(("parallel",)),
    )(w, g, m, v)


def _adamw_sum(w, r, m, v, *, name):
    n, rows, cols = r.shape
    tr = _tile(rows, (256, 128, 64, 32, 16, 8))
    spec = pl.BlockSpec((tr, cols), lambda i: (i, 0))

    def body(w_ref, r_ref, m_ref, v_ref, g_ref, d_ref, nm_ref, nv_ref):
        gv = r_ref[0].astype(F32)
        for d in range(1, n):
            gv = gv + r_ref[d].astype(F32)
        g_ref[...] = gv
        d_ref[...], nm_ref[...], nv_ref[...] = _adam_update(w_ref[...], gv, m_ref[...], v_ref[...])

    shp = jax.ShapeDtypeStruct((rows, cols), F32)
    return pl.pallas_call(
        body,
        name=name,
        grid=(rows // tr,),
        in_specs=[spec, pl.BlockSpec((n, tr, cols), lambda i: (0, i, 0)), spec, spec],
        out_specs=(spec, spec, spec, spec),
        out_shape=(shp, shp, shp, shp),
        compiler_params=_params(("parallel",)),
    )(w, r, m, v)


_MATMUL_WEIGHTS = (("conv_w_in", 2), ("conv_w_out", 1), ("attn_w_qkv", 2), ("attn_w_o", 1), ("ffn_w_up", 2),
                   ("ffn_w_down", 1))
_TAP_WEIGHTS = (("conv_a_dw_w", 2), ("conv_b_dw_w", 2), ("ffn_dw_w", 2))
_REPLICATED = ("mix_norm_g", "ffn_norm_g", "conv_a_dw_b", "conv_a_ln_g", "conv_a_ln_b", "attn_q_g", "attn_k_g",
               "ffn_dw_b")
_WEIGHT_ORDER = ("mix_norm_g", "ffn_norm_g", "conv_w_in", "conv_a_dw_w", "conv_a_dw_b", "conv_a_ln_g",
                 "conv_a_ln_b", "conv_b_dw_w", "conv_w_out", "attn_w_qkv", "attn_q_g", "attn_k_g", "attn_w_o",
                 "ffn_w_up", "ffn_dw_w", "ffn_dw_b", "ffn_w_down")


def _assemble(gathered, axis):
    n, l, a, b = gathered.shape
    if axis == 2:
        return jnp.transpose(gathered, (1, 2, 0, 3)).reshape(l, a, n * b)
    return jnp.transpose(gathered, (1, 0, 2, 3)).reshape(l, n * a, b)


def _split_shards(full, axis):
    l, a, b = full.shape
    if axis == 2:
        return jnp.transpose(full.reshape(l, a, N_DEV, b // N_DEV), (2, 0, 1, 3))
    return jnp.transpose(full.reshape(l, N_DEV, a // N_DEV, b), (1, 0, 2, 3))


def kernel(x, mix_norm_g, ffn_norm_g, conv_w_in, conv_a_dw_w, conv_a_dw_b, conv_a_ln_g, conv_a_ln_b, conv_b_dw_w, conv_w_out, attn_w_qkv, attn_q_g, attn_k_g, attn_w_o, ffn_w_up, ffn_dw_w, ffn_dw_b, ffn_w_down, loss_target, m_mix_norm_g, m_ffn_norm_g, m_conv_w_in, m_conv_a_dw_w, m_conv_a_dw_b, m_conv_a_ln_g, m_conv_a_ln_b, m_conv_b_dw_w, m_conv_w_out, m_attn_w_qkv, m_attn_q_g, m_attn_k_g, m_attn_w_o, m_ffn_w_up, m_ffn_dw_w, m_ffn_dw_b, m_ffn_w_down, v_mix_norm_g, v_ffn_norm_g, v_conv_w_in, v_conv_a_dw_w, v_conv_a_dw_b, v_conv_a_ln_g, v_conv_a_ln_b, v_conv_b_dw_w, v_conv_w_out, v_attn_w_qkv, v_attn_q_g, v_attn_k_g, v_attn_w_o, v_ffn_w_up, v_ffn_dw_w, v_ffn_dw_b, v_ffn_w_down):
    args = dict(locals())
    weights = {n: args[n] for n in _WEIGHT_ORDER}
    mom_m = {n: args["m_" + n] for n in _WEIGHT_ORDER}
    mom_v = {n: args["v_" + n] for n in _WEIGHT_ORDER}

    _, s, d = x.shape
    depth = mix_norm_g.shape[0]
    dh = attn_q_g.shape[1]
    x0 = x.reshape(s, d)
    target = loss_target.reshape(s, d)

    shard_axis = dict(_MATMUL_WEIGHTS)
    shard16 = {n: weights[n].astype(BF16) for n, _ in _MATMUL_WEIGHTS}
    full = {}

    def layer_keys(layer):
        if layer >= depth:
            return []
        mixer = ("conv_w_in", "conv_w_out") if layer % 2 == 0 else ("attn_w_qkv", "attn_w_o")
        return [(mixer[0], layer // 2), (mixer[1], layer // 2), ("ffn_w_up", layer), ("ffn_w_down", layer)]

    def shards(keys):
        return [shard16[n][idx] for n, idx in keys]

    def assemble(keys, gathered):
        for (n, idx), g in zip(keys, gathered):
            if shard_axis[n] == 2:
                full[(n, idx)] = _assemble_cols(g, name=f"assemble_{n}_{idx}")
            else:
                full[(n, idx)] = g.reshape(N_DEV * g.shape[1], g.shape[2])

    assemble(layer_keys(0), _all_gather(shards(layer_keys(0)), "gather_layer0_weights"))
    tap_shapes = [weights[n].shape for n, _ in _TAP_WEIGHTS]
    g_tap = _all_gather([_pack([weights[n] for n, _ in _TAP_WEIGHTS], F32)], "gather_tap_weights")[0]
    for (n, axis), part in zip(_TAP_WEIGHTS, _unpack(g_tap, tap_shapes, lead=(N_DEV,))):
        full[n] = _assemble(part, axis)
    f = ffn_w_down.shape[1] * N_DEV

    saved = []
    xs = x0
    for layer in range(depth):
        i = layer // 2
        nxt = layer_keys(layer + 1)
        sv = {"x_in": xs}
        hn = _rmsnorm_fwd(xs, mix_norm_g[layer:layer + 1], name=f"mix_norm_fwd_{layer}")
        sv["h"] = hn
        if layer % 2 == 0:
            p = _matmul(hn, full[("conv_w_in", i)], name=f"conv_in_{layer}")
            (ab, a1), got = _conv_act_fwd(p, full["conv_a_dw_w"][i], conv_a_dw_b[i:i + 1], conv_a_ln_g[i:i + 1],
                                          conv_a_ln_b[i:i + 1], full["conv_b_dw_w"][i], gather_cargo=shards(nxt[0:1]),
                                          name=f"conv_act_fwd_{layer}")
            assemble(nxt[0:1], got)
            sv.update(p=p, a1=a1, mix=ab)
            xs = _matmul(ab, full[("conv_w_out", i)], add=xs, name=f"conv_out_{layer}")
            riders = [nxt[1:2], nxt[2:3], nxt[3:4]]
        else:
            reps = LANES // dh
            gains = jnp.concatenate([
                jnp.broadcast_to(jnp.tile(attn_q_g[i], reps) * dh ** -0.5, (8, LANES)),
                jnp.broadcast_to(jnp.tile(attn_k_g[i], reps), (8, LANES)),
                jnp.ones((8, LANES), F32)], axis=0)
            qkv = _matmul(hn, full[("attn_w_qkv", i)], name=f"attn_qkv_{layer}")
            qkv16 = _qkv_prep_fwd(qkv, gains, dh, name=f"qkv_prep_fwd_{layer}")
            o, got = _attn_fwd(qkv16, dh, gather_cargo=shards(nxt), name=f"attn_fwd_{layer}")
            assemble(nxt, got)
            sv.update(qkv=qkv, qkv16=qkv16, gains=gains, mix=o)
            xs = _matmul(o, full[("attn_w_o", i)], add=xs, name=f"attn_out_{layer}")
            riders = [[], [], []]
        sv["x_mid"] = xs
        h2 = _rmsnorm_fwd(xs, ffn_norm_g[layer:layer + 1], name=f"ffn_norm_fwd_{layer}")
        w_up = full[("ffn_w_up", layer)]
        ug, got = _matmul(h2, w_up[:, :f], gather_cargo=shards(riders[0]), name=f"ffn_up_gate_{layer}")
        assemble(riders[0], got)
        uv = _matmul(h2, w_up[:, f:], name=f"ffn_up_val_{layer}")
        taps = full["ffn_dw_w"][layer]
        bias = ffn_dw_b[layer:layer + 1]
        act, got = _ffn_act_fwd(ug, uv, taps[:, :f], taps[:, f:], bias[:, :f], bias[:, f:],
                                gather_cargo=shards(riders[1]), name=f"ffn_act_fwd_{layer}")
        assemble(riders[1], got)
        sv.update(h2=h2, ug=ug, uv=uv, act=act)
        xs, got = _matmul(act, full[("ffn_w_down", layer)], add=xs, gather_cargo=shards(riders[2]),
                          name=f"ffn_down_{layer}")
        assemble(riders[2], got)
        saved.append(sv)

    sq, dx, dx16 = _loss_head(xs, target, name="loss_head")
    loss = lax.psum(0.5 * sq[0, 0] / d, ("x", "y", "c"))

    grads = {n: [None] * weights[n].shape[0] for n in _WEIGHT_ORDER}
    shard_axis = dict(_MATMUL_WEIGHTS)
    queued, arrived = [], {}

    def queue(n, idx, g):
        a, b = weights[n].shape[1:]
        slabs = _split_cols(g, name=f"split_{n}_{idx}") if shard_axis[n] == 2 else g.reshape(N_DEV, a, b)
        queued.append(((n, idx), slabs))

    def carried(call):
        keys = [k for k, _ in queued]
        slabs = [sl for _, sl in queued]
        queued.clear()
        outs, received = call(slabs)
        arrived.update(zip(keys, received))
        return outs

    for layer in reversed(range(depth)):
        i = layer // 2
        sv = saved[layer]
        w_up = full[("ffn_w_up", layer)]
        taps = full["ffn_dw_w"][layer]
        bias = ffn_dw_b[layer:layer + 1]
        dact = _matmul(dx16, full[("ffn_w_down", layer)], tb=True, name=f"ffn_down_dx_{layer}")
        queue("ffn_w_down", layer, _weight_grad(sv["act"], dx16, name=f"ffn_down_dw_{layer}"))
        dug, duv, sumg, sumv = carried(lambda cargo: _ffn_act_bwd(
            sv["ug"], sv["uv"], dact, taps[:, :f], taps[:, f:], bias[:, :f], bias[:, f:], cargo=cargo,
            name=f"ffn_act_bwd_{layer}"))
        sums = jnp.concatenate([sumg, sumv], axis=1).reshape(4, 8, 2 * f).sum(axis=1)
        grads["ffn_dw_w"][layer] = sums[:3]
        grads["ffn_dw_b"][layer] = sums[3]
        dh2 = _matmul(dug, w_up[:, :f], tb=True, name=f"ffn_up_gate_dx_{layer}")
        dh2 = _matmul(duv, w_up[:, f:], tb=True, add=dh2, name=f"ffn_up_val_dx_{layer}")
        queue("ffn_w_up", layer, jnp.concatenate(
            [_weight_grad(sv["h2"], dug, name=f"ffn_up_gate_dw_{layer}"),
             _weight_grad(sv["h2"], duv, name=f"ffn_up_val_dw_{layer}")], axis=1))
        dx, dx16, dg = _rmsnorm_bwd(sv["x_mid"], ffn_norm_g[layer:layer + 1], dh2, dx, name=f"ffn_norm_bwd_{layer}")
        grads["ffn_norm_g"][layer] = dg[0]

        if layer % 2 == 0:
            dab = _matmul(dx16, full[("conv_w_out", i)], tb=True, name=f"conv_out_dx_{layer}")
            queue("conv_w_out", i, _weight_grad(sv["mix"], dx16, name=f"conv_out_dw_{layer}"))
            ka = full["conv_a_dw_w"].shape[1]
            kb = full["conv_b_dw_w"].shape[1]
            dp, dwa, dba, dlng, dlnb, dwb = carried(lambda cargo: _conv_act_bwd(
                sv["p"], sv["a1"], dab, full["conv_a_dw_w"][i], conv_a_ln_g[i:i + 1], conv_a_ln_b[i:i + 1],
                full["conv_b_dw_w"][i], cargo=cargo, name=f"conv_act_bwd_{layer}"))
            grads["conv_a_dw_w"][i] = dwa[:ka]
            grads["conv_a_dw_b"][i] = dba[0]
            grads["conv_a_ln_g"][i] = dlng[0]
            grads["conv_a_ln_b"][i] = dlnb[0]
            grads["conv_b_dw_w"][i] = dwb[:kb]
            dhn = _matmul(dp, full[("conv_w_in", i)], tb=True, name=f"conv_in_dx_{layer}")
            queue("conv_w_in", i, _weight_grad(sv["h"], dp, name=f"conv_in_dw_{layer}"))
        else:
            do = _matmul(dx16, full[("attn_w_o", i)], tb=True, name=f"attn_out_dx_{layer}")
            queue("attn_w_o", i, _weight_grad(sv["mix"], dx16, name=f"attn_out_dw_{layer}"))
            dq, dk, dv = carried(lambda cargo: _attn_bwd(sv["qkv16"], sv["mix"], do, dh, cargo=cargo,
                                                         name=f"attn_bwd_{layer}"))
            dqkv, dgains = _qkv_prep_bwd(sv["qkv"], dq, dk, dv, sv["gains"], dh, name=f"qkv_prep_bwd_{layer}")
            grads["attn_q_g"][i] = dgains[0].reshape(-1, dh).sum(axis=0) * dh ** -0.5
            grads["attn_k_g"][i] = dgains[8].reshape(-1, dh).sum(axis=0)
            dhn = _matmul(dqkv, full[("attn_w_qkv", i)], tb=True, name=f"attn_qkv_dx_{layer}")
            queue("attn_w_qkv", i, _weight_grad(sv["h"], dqkv, name=f"attn_qkv_dw_{layer}"))
        dx, dx16, dg = _rmsnorm_bwd(sv["x_in"], mix_norm_g[layer:layer + 1], dhn, dx, name=f"mix_norm_bwd_{layer}")
        grads["mix_norm_g"][layer] = dg[0]
    carried(lambda cargo: ((), _exchange(cargo, "exchange_last_grads")))

    me = _my_index()
    final_grads, delta, new_m, new_v = {}, {}, {}, {}
    for n, _ in _MATMUL_WEIGHTS:
        per_layer = [_adamw_sum(weights[n][idx], arrived[(n, idx)], mom_m[n][idx], mom_v[n][idx],
                                name=f"adamw_{n}_{idx}") for idx in range(weights[n].shape[0])]
        final_grads[n], delta[n], new_m[n], new_v[n] = [jnp.stack(outs, axis=0) for outs in zip(*per_layer)]

    small_names = list(_REPLICATED) + [n for n, _ in _TAP_WEIGHTS]
    local = {n: jnp.stack(grads[n], axis=0) for n in small_names}
    small_shapes = [local[n].shape for n in small_names]
    g_small = _sum_slabs(_all_gather([_pack([local[n] for n in small_names], F32)], "gather_small_grads")[0],
                         name="sum_small_grads")
    reduced = dict(zip(small_names, _unpack(g_small, small_shapes)))
    for n in _REPLICATED:
        final_grads[n] = reduced[n]
    for n, axis in _TAP_WEIGHTS:
        final_grads[n] = lax.dynamic_index_in_dim(_split_shards(reduced[n], axis), me, axis=0, keepdims=False)
    shapes = [weights[n].shape for n in small_names]
    d_s, m_s, v_s = _adamw(
        _pack([weights[n] for n in small_names], F32), _pack([final_grads[n] for n in small_names], F32),
        _pack([mom_m[n] for n in small_names], F32), _pack([mom_v[n] for n in small_names], F32), name="adamw_small")
    delta.update(zip(small_names, _unpack(d_s, shapes)))
    new_m.update(zip(small_names, _unpack(m_s, shapes)))
    new_v.update(zip(small_names, _unpack(v_s, shapes)))

    order = list(_WEIGHT_ORDER)
    return (loss, dx.reshape(x.shape), *[final_grads[n] for n in order], *[delta[n] for n in order],
            *[new_m[n] for n in order], *[new_v[n] for n in order])
```

```python
import jax
import jax.numpy as jnp
from jax import lax
from jax.experimental import pallas as pl
from jax.experimental.pallas import tpu as pltpu

F32 = jnp.float32
BF16 = jnp.bfloat16
EPS = 1e-6
N_DEV = 8
MESH_ID = pl.DeviceIdType.MESH

ADAM_LR = 0.001
ADAM_B1 = 0.9
ADAM_B2 = 0.999
ADAM_EPS = 1e-08
ADAM_WD = 0.01
ADAM_STEP = 10

V7X_VMEM_LIMIT_BYTES = 48 * 1024 * 1024
PACK_QUANTUM = 2048
A_HALO = 32
S_HALO = 8
FFN_HALO_BLOCK = 16
LANES = 128
ATTN_Q_BLOCK = 256
ATTN_K_BLOCK = 256
MATMUL_MAX_SINGLE_K = 3072
MXU_WIDTH = 256
FFN_CHUNK_ROWS = 64
ATTN_DEAD_LOG_WEIGHT = -110.0


def _tile(n, prefs):
    for p in prefs:
        if n % p == 0:
            return p
    return n


def _params(sem=None):
    return pltpu.CompilerParams(dimension_semantics=sem, vmem_limit_bytes=V7X_VMEM_LIMIT_BYTES)


def _sigmoid(x):
    return 1.0 / (1.0 + jnp.exp(-x))


def _dot(a, b, ca, cb):
    return lax.dot_general(a, b, (((ca,), (cb,)), ((), ())), preferred_element_type=F32)


def _my_index():
    return 4 * lax.axis_index("x") + 2 * lax.axis_index("y") + lax.axis_index("c")


def _all_gather(shards, name):
    n = len(shards)

    def body(*refs):
        x_refs, out_refs = refs[:n], refs[n:2 * n]
        send_sems, recv_sems, local_sems = refs[2 * n:]
        x, y, c = lax.axis_index("x"), lax.axis_index("y"), lax.axis_index("c")
        me, sibling = (x, y, c), (x, y, 1 - c)
        chips = [(1 - x, y), (x, 1 - y), (1 - x, 1 - y)]

        def copy(a, k, block, to, src=None):
            slot = out_refs[a].at[4 * block[0] + 2 * block[1] + block[2]]
            return pltpu.make_async_remote_copy(
                src_ref=slot if src is None else src, dst_ref=slot,
                send_sem=send_sems.at[7 * a + k], recv_sem=recv_sems.at[7 * a + k],
                device_id=to, device_id_type=MESH_ID)

        mine = [pltpu.make_async_copy(x_refs[a], out_refs[a].at[4 * x + 2 * y + c], local_sems.at[a])
                for a in range(n)]
        for cp in mine:
            cp.start()
        first = []
        for a in range(n):
            first.append(copy(a, 0, me, sibling, src=x_refs[a]))
            first += [copy(a, 1 + j, me, (*chip, c), src=x_refs[a]) for j, chip in enumerate(chips)]
        for cp in first:
            cp.start()
        passed = []
        for j, chip in enumerate(chips):
            for a in range(n):
                copy(a, 1 + j, (*chip, c), me).wait_recv()
                fwd = copy(a, 4 + j, (*chip, c), sibling)
                fwd.start()
                passed.append(fwd)
        for a in range(n):
            copy(a, 0, sibling, me).wait_recv()
            for j, chip in enumerate(chips):
                copy(a, 4 + j, (*chip, 1 - c), me).wait_recv()
        for cp in first + passed:
            cp.wait_send()
        for cp in mine:
            cp.wait()

    hbm = pl.BlockSpec(memory_space=pl.ANY)
    return pl.pallas_call(
        body,
        name=name,
        out_shape=[jax.ShapeDtypeStruct((N_DEV,) + sh.shape, sh.dtype) for sh in shards],
        in_specs=[hbm] * n,
        out_specs=[hbm] * n,
        scratch_shapes=[
            pltpu.SemaphoreType.DMA((7 * n,)),
            pltpu.SemaphoreType.DMA((7 * n,)),
            pltpu.SemaphoreType.DMA((n,)),
        ],
    )(*shards)


def _exchange(slabs, name):
    n = len(slabs)

    def body(*refs):
        copies = _exchange_copies(refs[:n], refs[n:2 * n], *refs[2 * n:])
        _exchange_start(copies)
        _exchange_wait(copies)

    hbm = pl.BlockSpec(memory_space=pl.ANY)
    return pl.pallas_call(
        body,
        name=name,
        out_shape=[jax.ShapeDtypeStruct(g.shape, g.dtype) for g in slabs],
        in_specs=[hbm] * n,
        out_specs=[hbm] * n,
        scratch_shapes=_exchange_semaphores(n),
    )(*slabs)


def _exchange_semaphores(n):
    return [pltpu.SemaphoreType.DMA((7 * n,)), pltpu.SemaphoreType.DMA((7 * n,)), pltpu.SemaphoreType.DMA((n,))]


def _exchange_copies(g_refs, out_refs, send_sems, recv_sems, local_sems, gather=False):
    n = len(g_refs)
    x, y, c = lax.axis_index("x"), lax.axis_index("y"), lax.axis_index("c")
    me = 4 * x + 2 * y + c

    def part(a, d):
        return g_refs[a] if gather else g_refs[a].at[d]

    local = [pltpu.make_async_copy(part(a, me), out_refs[a].at[me], local_sems.at[a]) for a in range(n)]
    remote = []
    for k in range(1, N_DEV):
        px = (x + ((k >> 2) & 1)) % 2
        py = (y + ((k >> 1) & 1)) % 2
        pc = (c + (k & 1)) % 2
        for a in range(n):
            remote.append(pltpu.make_async_remote_copy(
                src_ref=part(a, 4 * px + 2 * py + pc), dst_ref=out_refs[a].at[me],
                send_sem=send_sems.at[7 * a + k - 1], recv_sem=recv_sems.at[7 * a + k - 1],
                device_id=(px, py, pc), device_id_type=MESH_ID))
    return local, remote


def _exchange_start(copies):
    for cp in copies[0] + copies[1]:
        cp.start()


def _exchange_wait(copies):
    local, remote = copies
    for cp in remote:
        cp.wait_recv()
    for cp in remote:
        cp.wait_send()
    for cp in local:
        cp.wait()


def _carrying_call(body, cargo, *, name, grid, in_specs, out_specs, out_shape, scratch_shapes, operands,
                   compiler_params, gather=False):
    n = len(cargo)
    n_in, n_out, n_scr = len(in_specs), len(out_specs), len(scratch_shapes)
    if n == 0:
        return pl.pallas_call(body, name=name, grid=grid, in_specs=in_specs, out_specs=out_specs,
                              out_shape=out_shape, scratch_shapes=scratch_shapes,
                              compiler_params=compiler_params)(*operands), []

    def carrying(*refs):
        cuts = [0, n_in, n_in + n, n_in + n + n_out, n_in + 2 * n + n_out, n_in + 2 * n + n_out + n_scr, len(refs)]
        ins, g_refs, outs, r_refs, scr, sems = [refs[a:b] for a, b in zip(cuts[:-1], cuts[1:])]
        steps = [pl.program_id(ax) for ax in range(len(grid))]
        first = steps[0] == 0
        last = steps[0] == grid[0] - 1
        for ax in range(1, len(grid)):
            first = first & (steps[ax] == 0)
            last = last & (steps[ax] == grid[ax] - 1)

        @pl.when(first)
        def _():
            _exchange_start(_exchange_copies(g_refs, r_refs, *sems, gather=gather))

        body(*ins, *outs, *scr)

        @pl.when(last)
        def _():
            _exchange_wait(_exchange_copies(g_refs, r_refs, *sems, gather=gather))

    hbm = pl.BlockSpec(memory_space=pl.ANY)
    lead = (N_DEV,) if gather else ()
    res = pl.pallas_call(
        carrying,
        name=name,
        grid=grid,
        in_specs=list(in_specs) + [hbm] * n,
        out_specs=list(out_specs) + [hbm] * n,
        out_shape=list(out_shape) + [jax.ShapeDtypeStruct(lead + g.shape, g.dtype) for g in cargo],
        scratch_shapes=list(scratch_shapes) + _exchange_semaphores(n),
        compiler_params=compiler_params,
    )(*operands, *cargo)
    return res[:n_out], res[n_out:]


def _assemble_cols(g, *, name):
    n, r, w = g.shape
    tr = _tile(r, (256, 128, 64, 32, 16))

    def body(g_ref, o_ref):
        for j in range(n):
            o_ref[:, j * w:(j + 1) * w] = g_ref[j]

    return pl.pallas_call(
        body,
        name=name,
        grid=(r // tr,),
        in_specs=[pl.BlockSpec((n, tr, w), lambda i: (0, i, 0))],
        out_specs=pl.BlockSpec((tr, n * w), lambda i: (i, 0)),
        out_shape=jax.ShapeDtypeStruct((r, n * w), g.dtype),
        compiler_params=_params(("parallel",)),
    )(g)


def _split_cols(x, *, name):
    r, nw = x.shape
    w = nw // N_DEV
    tr = _tile(r, (256, 128, 64, 32, 16))

    def body(x_ref, o_ref):
        for j in range(N_DEV):
            o_ref[j] = x_ref[:, j * w:(j + 1) * w]

    return pl.pallas_call(
        body,
        name=name,
        grid=(r // tr,),
        in_specs=[pl.BlockSpec((tr, nw), lambda i: (i, 0))],
        out_specs=pl.BlockSpec((N_DEV, tr, w), lambda i: (0, i, 0)),
        out_shape=jax.ShapeDtypeStruct((N_DEV, r, w), x.dtype),
        compiler_params=_params(("parallel",)),
    )(x)


def _padded(n):
    return -(-n // PACK_QUANTUM) * PACK_QUANTUM


def _pack(parts, dtype, lead=()):
    flat = []
    for p in parts:
        p = p.astype(dtype).reshape(lead + (-1,))
        n = p.shape[-1]
        flat.append(jnp.pad(p, [(0, 0)] * len(lead) + [(0, _padded(n) - n)]))
    return jnp.concatenate(flat, axis=-1).reshape(lead + (-1, 128))


def _unpack(buf, shapes, lead=()):
    flat = buf.reshape(lead + (-1,))
    out, off = [], 0
    for shp in shapes:
        n = 1
        for d in shp:
            n *= d
        out.append(flat[..., off:off + n].reshape(lead + tuple(shp)))
        off += _padded(n)
    return out


def _half_if_tiled(n):
    half = n // 2
    return (half,) if n % 2 == 0 and half % LANES == 0 and half <= 1536 else ()


def _matmul(a, b, *, ta=False, tb=False, out_dtype=F32, add=None, gather_cargo=None, name):
    if ta:
        kdim, m = a.shape
    else:
        m, kdim = a.shape
    n = b.shape[0] if tb else b.shape[1]
    bm = _tile(m, (1024,) + _half_if_tiled(m) + (512, 256, 128))
    bn = _tile(n, (512, 256, 128))
    if bn < 512 and n % MXU_WIDTH == 0 and n <= MATMUL_MAX_SINGLE_K:
        bn, bm = n, _tile(m, (512, 256, 128))
    bk = kdim if kdim <= MATMUL_MAX_SINGLE_K else _tile(kdim, (1024, 512, 256, 128))
    nk = kdim // bk

    a_spec = pl.BlockSpec((bk, bm), lambda i, j, k: (k, i)) if ta else pl.BlockSpec((bm, bk), lambda i, j, k: (i, k))
    b_spec = pl.BlockSpec((bn, bk), lambda i, j, k: (j, k)) if tb else pl.BlockSpec((bk, bn), lambda i, j, k: (k, j))
    o_spec = pl.BlockSpec((bm, bn), lambda i, j, k: (i, j))
    in_specs = [a_spec, b_spec] + ([o_spec] if add is not None else [])
    operands = [a, b] + ([add] if add is not None else [])

    def product(a_ref, b_ref):
        return _dot(a_ref[...].astype(BF16), b_ref[...].astype(BF16), 0 if ta else 1, 1 if tb else 0)

    def finish(r, refs):
        if add is not None:
            r = r + refs[2][...]
        return r

    def body_single(*refs):
        o_ref = refs[-1]
        o_ref[...] = finish(product(refs[0], refs[1]), refs).astype(o_ref.dtype)

    def body_multi(*refs):
        o_ref, acc_ref = refs[-2], refs[-1]
        k = pl.program_id(2)

        @pl.when(k == 0)
        def _():
            acc_ref[...] = jnp.zeros_like(acc_ref)

        acc_ref[...] += product(refs[0], refs[1])

        @pl.when(k == nk - 1)
        def _():
            o_ref[...] = finish(acc_ref[...], refs).astype(o_ref.dtype)

    outs, received = _carrying_call(
        body_single if nk == 1 else body_multi,
        () if gather_cargo is None else gather_cargo,
        gather=True,
        name=name,
        grid=(m // bm, n // bn, nk),
        in_specs=in_specs,
        out_specs=[o_spec],
        out_shape=[jax.ShapeDtypeStruct((m, n), out_dtype)],
        scratch_shapes=[] if nk == 1 else [pltpu.VMEM((bm, bn), F32)],
        operands=operands,
        compiler_params=_params(("parallel", "parallel", "arbitrary") if gather_cargo is None else ("arbitrary",) * 3),
    )
    return outs[0] if gather_cargo is None else (outs[0], received)


def _weight_grad(x, dy, *, name):
    return _matmul(x, dy, ta=True, out_dtype=BF16, name=name)


def _rmsnorm_fwd(x, g, *, name):
    r, d = x.shape
    tr = _tile(r, (1024, 512, 256, 128))

    def body(x_ref, g_ref, o_ref):
        xv = x_ref[...]
        y = xv * lax.rsqrt(jnp.mean(xv * xv, axis=-1, keepdims=True) + EPS)
        y = y * g_ref[...]
        o_ref[...] = y.astype(o_ref.dtype)

    return pl.pallas_call(
        body,
        name=name,
        grid=(r // tr,),
        in_specs=[pl.BlockSpec((tr, d), lambda i: (i, 0)), pl.BlockSpec((1, d), lambda i: (0, 0))],
        out_specs=pl.BlockSpec((tr, d), lambda i: (i, 0)),
        out_shape=jax.ShapeDtypeStruct((r, d), BF16),
        compiler_params=_params(("parallel",)),
    )(x, g)


def _rmsnorm_bwd(x, g, dy, res, *, name):
    r, d = x.shape
    tr = _tile(r, (512, 256, 128))
    row = pl.BlockSpec((tr, d), lambda i: (i, 0))
    vec = pl.BlockSpec((1, d), lambda i: (0, 0))
    operands = [x, g, dy] + ([res] if res is not None else [])

    def body(*refs):
        if res is not None:
            x_ref, g_ref, dy_ref, res_ref, dx_ref, dx16_ref, dg_ref = refs
        else:
            x_ref, g_ref, dy_ref, dx_ref, dx16_ref, dg_ref = refs
        xv = x_ref[...]
        dyv = dy_ref[...].astype(F32)
        rs = lax.rsqrt(jnp.mean(xv * xv, axis=-1, keepdims=True) + EPS)
        xn = xv * rs
        dyg = dyv * g_ref[...]
        dx = rs * (dyg - xn * jnp.mean(dyg * xn, axis=-1, keepdims=True))
        if res is not None:
            dx = dx + res_ref[...]
        dx_ref[...] = dx
        dx16_ref[...] = dx.astype(dx16_ref.dtype)

        @pl.when(pl.program_id(0) == 0)
        def _():
            dg_ref[...] = jnp.zeros_like(dg_ref)

        dg_ref[...] += jnp.sum(dyv * xn, axis=0, keepdims=True)

    return pl.pallas_call(
        body,
        name=name,
        grid=(r // tr,),
        in_specs=[row, vec, row] + ([row] if res is not None else []),
        out_specs=(row, row, vec),
        out_shape=(jax.ShapeDtypeStruct((r, d), F32), jax.ShapeDtypeStruct((r, d), BF16),
                   jax.ShapeDtypeStruct((1, d), F32)),
        compiler_params=_params(("arbitrary",)),
    )(*operands)


def _loss_head(y, target, *, name):
    s, d = y.shape
    ts = _tile(s, (512, 256, 128))
    row = pl.BlockSpec((ts, d), lambda i: (i, 0))
    acc = pl.BlockSpec((8, 128), lambda i: (0, 0))

    def body(y_ref, t_ref, l_ref, dy_ref, dy16_ref):
        e = y_ref[...] - t_ref[...]
        dy = e * (1.0 / d)
        dy_ref[...] = dy
        dy16_ref[...] = dy.astype(dy16_ref.dtype)

        @pl.when(pl.program_id(0) == 0)
        def _():
            l_ref[...] = jnp.zeros_like(l_ref)

        sq = jnp.sum(jnp.sum(e * e, axis=-1, keepdims=True), axis=0, keepdims=True)
        l_ref[...] += jnp.broadcast_to(sq, l_ref.shape)

    return pl.pallas_call(
        body,
        name=name,
        grid=(s // ts,),
        in_specs=[row, row],
        out_specs=(acc, row, row),
        out_shape=(jax.ShapeDtypeStruct((8, 128), F32), jax.ShapeDtypeStruct((s, d), F32),
                   jax.ShapeDtypeStruct((s, d), BF16)),
        compiler_params=_params(("arbitrary",)),
    )(y, target)


def _ffn_tiles(s, f):
    return _tile(s, (512, 256, 128)), _tile(f, _half_if_tiled(f) + (512, 256, 128))


def _conv3(xs, w, b):
    return b + xs[0] * w[0:1] + xs[1] * w[1:2] + xs[2] * w[2:3]


def _shifted3(x, rows):
    return [x[S_HALO - 2 + k:S_HALO - 2 + k + rows] for k in range(3)]


def _ffn_act_fwd(ug, uv, wg, wv, bg, bv, *, gather_cargo=(), name):
    s, f = ug.shape
    ts, tc = _ffn_tiles(s, f)
    rc = _tile(ts, (FFN_CHUNK_ROWS, 32, 16, 8))
    hb = ts // FFN_HALO_BLOCK
    main = pl.BlockSpec((ts, tc), lambda j, i: (i, j))
    prev = pl.BlockSpec((FFN_HALO_BLOCK, tc), lambda j, i: (jnp.maximum(i * hb - 1, 0), j))
    taps = pl.BlockSpec((3, tc), lambda j, i: (0, j))
    bias = pl.BlockSpec((1, tc), lambda j, i: (0, j))
    lead = FFN_HALO_BLOCK - S_HALO

    def body(ug_ref, ugp_ref, uv_ref, uvp_ref, wg_ref, wv_ref, bg_ref, bv_ref, o_ref, gbuf, vbuf):
        first = pl.program_id(1) == 0
        for buf, p_ref, m_ref in ((gbuf, ugp_ref, ug_ref), (vbuf, uvp_ref, uv_ref)):
            buf[0:FFN_HALO_BLOCK, :] = jnp.where(first, 0.0, p_ref[...].astype(F32))
            buf[FFN_HALO_BLOCK:, :] = m_ref[...].astype(F32)

        def chunk(ci, carry):
            r0 = pl.multiple_of(ci * rc, rc)
            rows = pl.ds(pl.multiple_of(r0 + lead, S_HALO), rc + S_HALO)
            for lb in range(tc // LANES):
                ls = pl.ds(lb * LANES, LANES)
                gate = _conv3(_shifted3(gbuf[rows, ls], rc), wg_ref[:, ls], bg_ref[:, ls])
                val = _conv3(_shifted3(vbuf[rows, ls], rc), wv_ref[:, ls], bv_ref[:, ls])
                o_ref[pl.ds(r0, rc), ls] = (gate * _sigmoid(gate) * val).astype(o_ref.dtype)
            return carry

        lax.fori_loop(0, ts // rc, chunk, 0)

    outs, gathered = _carrying_call(
        body,
        gather_cargo,
        gather=True,
        name=name,
        grid=(f // tc, s // ts),
        in_specs=[main, prev, main, prev, taps, taps, bias, bias],
        out_specs=[main],
        out_shape=[jax.ShapeDtypeStruct((s, f), BF16)],
        scratch_shapes=[pltpu.VMEM((FFN_HALO_BLOCK + ts, tc), F32), pltpu.VMEM((FFN_HALO_BLOCK + ts, tc), F32)],
        operands=(ug, ug, uv, uv, wg, wv, bg, bv),
        compiler_params=_params(("arbitrary", "arbitrary")),
    )
    return outs[0], gathered


def _ffn_act_bwd(ug, uv, df, wg, wv, bg, bv, *, cargo=(), name):
    s, f = ug.shape
    ts, tc = _ffn_tiles(s, f)
    rc = _tile(ts, (FFN_CHUNK_ROWS, 32, 16, 8))
    hb = ts // FFN_HALO_BLOCK
    last_hb = s // FFN_HALO_BLOCK - 1
    n_row = s // ts
    lead = FFN_HALO_BLOCK - S_HALO
    main = pl.BlockSpec((ts, tc), lambda j, i: (i, j))
    prev = pl.BlockSpec((FFN_HALO_BLOCK, tc), lambda j, i: (jnp.maximum(i * hb - 1, 0), j))
    nxt = pl.BlockSpec((FFN_HALO_BLOCK, tc), lambda j, i: (jnp.minimum((i + 1) * hb, last_hb), j))
    taps = pl.BlockSpec((3, tc), lambda j, i: (0, j))
    bias = pl.BlockSpec((1, tc), lambda j, i: (0, j))
    sums = pl.BlockSpec((32, tc), lambda j, i: (0, j))
    ext = rc + S_HALO

    def body(ug_ref, ugp_ref, ugn_ref, uv_ref, uvp_ref, uvn_ref, df_ref, dfn_ref, wg_ref, wv_ref, bg_ref, bv_ref,
             dug_ref, duv_ref, sumg_ref, sumv_ref, gbuf, vbuf, dfbuf):
        i = pl.program_id(1)
        first = i == 0
        last = i == n_row - 1

        @pl.when(first)
        def _():
            sumg_ref[...] = jnp.zeros_like(sumg_ref)
            sumv_ref[...] = jnp.zeros_like(sumv_ref)

        for buf, p_ref, m_ref, n_ref in ((gbuf, ugp_ref, ug_ref, ugn_ref), (vbuf, uvp_ref, uv_ref, uvn_ref)):
            buf[0:FFN_HALO_BLOCK, :] = jnp.where(first, 0.0, p_ref[...].astype(F32))
            buf[FFN_HALO_BLOCK:FFN_HALO_BLOCK + ts, :] = m_ref[...].astype(F32)
            buf[FFN_HALO_BLOCK + ts:, :] = n_ref[...].astype(F32)
        dfbuf[0:ts, :] = df_ref[...].astype(F32)
        dfbuf[ts:, :] = jnp.where(last, 0.0, dfn_ref[...].astype(F32))

        def chunk(ci, carry):
            r0 = pl.multiple_of(ci * rc, rc)
            rows = pl.ds(pl.multiple_of(r0 + lead, S_HALO), ext + S_HALO)
            for lb in range(tc // LANES):
                ls = pl.ds(lb * LANES, LANES)
                xg = _shifted3(gbuf[rows, ls], ext)
                xv = _shifted3(vbuf[rows, ls], ext)
                wgv, wvv = wg_ref[:, ls], wv_ref[:, ls]
                gate = _conv3(xg, wgv, bg_ref[:, ls])
                val = _conv3(xv, wvv, bv_ref[:, ls])
                dfe = dfbuf[pl.ds(r0, ext), ls]
                sg = _sigmoid(gate)
                dgate = dfe * val * (sg * (1.0 + gate * (1.0 - sg)))
                dval = dfe * (gate * sg)
                for dz, xs, w, du_ref, sum_ref in ((dgate, xg, wgv, dug_ref, sumg_ref),
                                                   (dval, xv, wvv, duv_ref, sumv_ref)):
                    du = dz[2:2 + rc] * w[0:1] + dz[1:1 + rc] * w[1:2] + dz[0:rc] * w[2:3]
                    du_ref[pl.ds(r0, rc), ls] = du.astype(du_ref.dtype)
                    dm = dz[0:rc]
                    for k in range(3):
                        sum_ref[8 * k:8 * k + 8, ls] += (dm * xs[k][0:rc]).reshape(rc // 8, 8, LANES).sum(axis=0)
                    sum_ref[24:32, ls] += dm.reshape(rc // 8, 8, LANES).sum(axis=0)
            return carry

        lax.fori_loop(0, ts // rc, chunk, 0)

    return _carrying_call(
        body,
        cargo,
        name=name,
        grid=(f // tc, n_row),
        in_specs=[main, prev, nxt, main, prev, nxt, main, nxt, taps, taps, bias, bias],
        out_specs=(main, main, sums, sums),
        out_shape=(jax.ShapeDtypeStruct((s, f), BF16), jax.ShapeDtypeStruct((s, f), BF16),
                   jax.ShapeDtypeStruct((32, f), F32), jax.ShapeDtypeStruct((32, f), F32)),
        scratch_shapes=[pltpu.VMEM((2 * FFN_HALO_BLOCK + ts, tc), F32), pltpu.VMEM((2 * FFN_HALO_BLOCK + ts, tc), F32),
                        pltpu.VMEM((FFN_HALO_BLOCK + ts, tc), F32)],
        operands=(ug, ug, ug, uv, uv, uv, df, df, wg, wv, bg, bv),
        compiler_params=_params(("arbitrary", "arbitrary")),
    )


def _layernorm_stats(a1):
    mu = jnp.mean(a1, axis=-1, keepdims=True)
    xc = a1 - mu
    rstd = lax.rsqrt(jnp.mean(xc * xc, axis=-1, keepdims=True) + EPS)
    return xc * rstd, rstd


def _conv_act_fwd(p, wa, ba, lng, lnb, wb, *, gather_cargo=(), name):
    s, c5 = p.shape
    c = c5 // 5
    ka, kb = wa.shape[0], wb.shape[0]
    ts = _tile(s, (256, 128))
    hb = ts // A_HALO
    main = pl.BlockSpec((ts, c5), lambda i: (i, 0))
    prev = pl.BlockSpec((A_HALO, c5), lambda i: (jnp.maximum(i * hb - 1, 0), 0))

    def full(arr):
        return pl.BlockSpec(arr.shape, lambda i: (0, 0))

    def body(p_ref, pp_ref, wa_ref, ba_ref, lng_ref, lnb_ref, wb_ref, ab_ref, a1_ref, gbuf, cbuf):
        first = pl.program_id(0) == 0
        gbuf[0:A_HALO, :] = jnp.where(first, 0.0, pp_ref[:, 0:c] * _sigmoid(pp_ref[:, c:2 * c]))
        gbuf[A_HALO:, :] = p_ref[:, 0:c] * _sigmoid(p_ref[:, c:2 * c])
        cbuf[0:A_HALO, :] = jnp.where(first, 0.0, pp_ref[:, 3 * c:4 * c] * pp_ref[:, 4 * c:5 * c])
        cbuf[A_HALO:, :] = p_ref[:, 3 * c:4 * c] * p_ref[:, 4 * c:5 * c]

        a1 = jnp.broadcast_to(ba_ref[...], (ts, c))
        for k in range(ka):
            off = A_HALO - (ka - 1) + k
            a1 = a1 + gbuf[off:off + ts, :] * wa_ref[k:k + 1, :]
        a1_ref[...] = a1
        xhat, _ = _layernorm_stats(a1)
        a2 = xhat * lng_ref[...] + lnb_ref[...]
        ab_ref[:, 0:c] = (a2 * _sigmoid(a2)).astype(ab_ref.dtype)

        cv = jnp.zeros((ts, c), F32)
        for k in range(kb):
            off = A_HALO - (kb - 1) + k
            cv = cv + cbuf[off:off + ts, :] * wb_ref[k:k + 1, :]
        ab_ref[:, c:2 * c] = (p_ref[:, 2 * c:3 * c] * cv).astype(ab_ref.dtype)

    return _carrying_call(
        body,
        gather_cargo,
        gather=True,
        name=name,
        grid=(s // ts,),
        in_specs=[main, prev, full(wa), full(ba), full(lng), full(lnb), full(wb)],
        out_specs=(pl.BlockSpec((ts, 2 * c), lambda i: (i, 0)), pl.BlockSpec((ts, c), lambda i: (i, 0))),
        out_shape=(jax.ShapeDtypeStruct((s, 2 * c), BF16), jax.ShapeDtypeStruct((s, c), F32)),
        scratch_shapes=[pltpu.VMEM((A_HALO + ts, c), F32), pltpu.VMEM((A_HALO + ts, c), F32)],
        operands=(p, p, wa, ba, lng, lnb, wb),
        compiler_params=_params(("arbitrary",)),
    )


def _conv_act_bwd(p, a1, dab, wa, lng, lnb, wb, *, cargo=(), name):
    s, c5 = p.shape
    c = c5 // 5
    ka, kb = wa.shape[0], wb.shape[0]
    ts = _tile(s, (256, 128))
    hb = ts // A_HALO
    n_row = s // ts
    last_hb = s // A_HALO - 1
    ext = ts + A_HALO

    def rows(width):
        return (pl.BlockSpec((ts, width), lambda i: (i, 0)),
                pl.BlockSpec((A_HALO, width), lambda i: (jnp.maximum(i * hb - 1, 0), 0)),
                pl.BlockSpec((A_HALO, width), lambda i: (jnp.minimum((i + 1) * hb, last_hb), 0)))

    p_main, p_prev, p_next = rows(c5)
    d_main, _, d_next = rows(2 * c)
    a_main, _, a_next = rows(c)

    def full(shape):
        return pl.BlockSpec(shape, lambda i: (0, 0))

    def body(p_ref, pp_ref, pn_ref, a1_ref, a1n_ref, d_ref, dn_ref, wa_ref, lng_ref, lnb_ref, wb_ref,
             dp_ref, dwa_ref, dba_ref, dlng_ref, dlnb_ref, dwb_ref, gbuf, cbuf, da1buf, dcvbuf):
        i = pl.program_id(0)
        first = i == 0
        last = i == n_row - 1

        @pl.when(first)
        def _():
            dwa_ref[...] = jnp.zeros_like(dwa_ref)
            dba_ref[...] = jnp.zeros_like(dba_ref)
            dlng_ref[...] = jnp.zeros_like(dlng_ref)
            dlnb_ref[...] = jnp.zeros_like(dlnb_ref)
            dwb_ref[...] = jnp.zeros_like(dwb_ref)

        sig_gate = _sigmoid(p_ref[:, c:2 * c])
        gbuf[0:A_HALO, :] = jnp.where(first, 0.0, pp_ref[:, 0:c] * _sigmoid(pp_ref[:, c:2 * c]))
        gbuf[A_HALO:, :] = p_ref[:, 0:c] * sig_gate
        cbuf[0:A_HALO, :] = jnp.where(first, 0.0, pp_ref[:, 3 * c:4 * c] * pp_ref[:, 4 * c:5 * c])
        cbuf[A_HALO:, :] = p_ref[:, 3 * c:4 * c] * p_ref[:, 4 * c:5 * c]

        def ln_back(a1v, dav):
            xhat, rstd = _layernorm_stats(a1v)
            a2 = xhat * lng_ref[...] + lnb_ref[...]
            sg = _sigmoid(a2)
            da2 = dav * (sg * (1.0 + a2 * (1.0 - sg)))
            dxh = da2 * lng_ref[...]
            da1 = rstd * (dxh - jnp.mean(dxh, axis=-1, keepdims=True)
                          - xhat * jnp.mean(dxh * xhat, axis=-1, keepdims=True))
            return da1, da2, xhat

        da1_m, da2_m, xhat_m = ln_back(a1_ref[...], d_ref[:, 0:c])
        da1_n, _, _ = ln_back(a1n_ref[...], dn_ref[:, 0:c])
        da1buf[0:ts, :] = da1_m
        da1buf[ts:, :] = jnp.where(last, 0.0, da1_n)
        dlng_ref[...] += jnp.sum(da2_m * xhat_m, axis=0, keepdims=True)
        dlnb_ref[...] += jnp.sum(da2_m, axis=0, keepdims=True)
        dba_ref[...] += jnp.sum(da1_m, axis=0, keepdims=True)

        dglu = jnp.zeros((ts, c), F32)
        for k in range(ka):
            off = (ka - 1) - k
            dglu = dglu + da1buf[off:off + ts, :] * wa_ref[k:k + 1, :]
            goff = A_HALO - (ka - 1) + k
            dwa_ref[k:k + 1, :] += jnp.sum(da1_m * gbuf[goff:goff + ts, :], axis=0, keepdims=True)
        dp_ref[:, 0:c] = (dglu * sig_gate).astype(dp_ref.dtype)
        dp_ref[:, c:2 * c] = (dglu * p_ref[:, 0:c] * sig_gate * (1.0 - sig_gate)).astype(dp_ref.dtype)

        cv = jnp.zeros((ts, c), F32)
        for k in range(kb):
            off = A_HALO - (kb - 1) + k
            cv = cv + cbuf[off:off + ts, :] * wb_ref[k:k + 1, :]
        db_m = d_ref[:, c:2 * c]
        dp_ref[:, 2 * c:3 * c] = (db_m * cv).astype(dp_ref.dtype)
        dcv_m = db_m * p_ref[:, 2 * c:3 * c]
        dcvbuf[0:ts, :] = dcv_m
        dcvbuf[ts:, :] = jnp.where(last, 0.0, dn_ref[:, c:2 * c] * pn_ref[:, 2 * c:3 * c])
        dbc = jnp.zeros((ts, c), F32)
        for k in range(kb):
            off = (kb - 1) - k
            dbc = dbc + dcvbuf[off:off + ts, :] * wb_ref[k:k + 1, :]
            coff = A_HALO - (kb - 1) + k
            dwb_ref[k:k + 1, :] += jnp.sum(dcv_m * cbuf[coff:coff + ts, :], axis=0, keepdims=True)
        dp_ref[:, 3 * c:4 * c] = (dbc * p_ref[:, 4 * c:5 * c]).astype(dp_ref.dtype)
        dp_ref[:, 4 * c:5 * c] = (dbc * p_ref[:, 3 * c:4 * c]).astype(dp_ref.dtype)

    return _carrying_call(
        body,
        cargo,
        name=name,
        grid=(n_row,),
        in_specs=[p_main, p_prev, p_next, a_main, a_next, d_main, d_next,
                  full(wa.shape), full(lng.shape), full(lnb.shape), full(wb.shape)],
        out_specs=(pl.BlockSpec((ts, c5), lambda i: (i, 0)), full((32, c)), full((1, c)), full((1, c)),
                   full((1, c)), full((8, c))),
        out_shape=(
            jax.ShapeDtypeStruct((s, c5), BF16), jax.ShapeDtypeStruct((32, c), F32),
            jax.ShapeDtypeStruct((1, c), F32), jax.ShapeDtypeStruct((1, c), F32),
            jax.ShapeDtypeStruct((1, c), F32), jax.ShapeDtypeStruct((8, c), F32),
        ),
        scratch_shapes=[
            pltpu.VMEM((A_HALO + ts, c), F32), pltpu.VMEM((A_HALO + ts, c), F32),
            pltpu.VMEM((ext, c), F32), pltpu.VMEM((ext, c), F32),
        ],
        operands=(p, p, p, a1, a1, dab, dab, wa, lng, lnb, wb),
        compiler_params=_params(("arbitrary",)),
    )


def _head_pair_geometry(width, dh):
    assert 2 * dh == LANES, "the attention kernels pair two heads in one 128-lane block"
    return width // (3 * LANES)


def _group_sum(v, lo):
    s_lo = jnp.sum(jnp.where(lo, v, 0.0), axis=-1, keepdims=True)
    s_hi = jnp.sum(jnp.where(lo, 0.0, v), axis=-1, keepdims=True)
    return jnp.where(lo, s_lo, s_hi)


def _qkv_prep_fwd(qkv, gains, dh, *, name):
    s, width = qkv.shape
    nsec = _head_pair_geometry(width, dh)
    tr = _tile(s, (256, 128))
    blk = pl.BlockSpec((tr, nsec * LANES), lambda i, sec: (i, sec))

    def body(x_ref, g_ref, o_ref):
        sec = pl.program_id(1)

        @pl.when(sec == 2)
        def _():
            o_ref[...] = x_ref[...].astype(o_ref.dtype)

        @pl.when(sec != 2)
        def _():
            lo = lax.broadcasted_iota(jnp.int32, (1, LANES), 1) < dh
            for lb in range(nsec):
                ls = pl.ds(lb * LANES, LANES)
                xv = x_ref[:, ls]
                rs = lax.rsqrt(_group_sum(xv * xv, lo) * (1.0 / dh) + EPS)
                o_ref[:, ls] = (xv * rs * g_ref[0:1, :]).astype(o_ref.dtype)

    return pl.pallas_call(
        body,
        name=name,
        grid=(s // tr, 3),
        in_specs=[blk, pl.BlockSpec((8, LANES), lambda i, sec: (sec, 0))],
        out_specs=blk,
        out_shape=jax.ShapeDtypeStruct((s, width), BF16),
        compiler_params=_params(("parallel", "parallel")),
    )(qkv, gains)


def _qkv_prep_bwd(qkv, dq, dk, dv, gains, dh, *, name):
    s, width = qkv.shape
    nsec = _head_pair_geometry(width, dh)
    tr = _tile(s, (256, 128))
    blk = pl.BlockSpec((tr, nsec * LANES), lambda sec, i: (i, sec))
    gspec = pl.BlockSpec((8, LANES), lambda sec, i: (sec, 0))

    def grad_spec(own):
        return pl.BlockSpec((tr, nsec * LANES), lambda sec, i: (jnp.where(sec == own, i, 0), 0))

    def body(x_ref, dq_ref, dk_ref, dv_ref, g_ref, o_ref, dg_ref):
        sec = pl.program_id(0)

        @pl.when(pl.program_id(1) == 0)
        def _():
            dg_ref[...] = jnp.zeros_like(dg_ref)

        def normed(d_ref):
            lo = lax.broadcasted_iota(jnp.int32, (1, LANES), 1) < dh
            dg = jnp.zeros((1, LANES), F32)
            for lb in range(nsec):
                ls = pl.ds(lb * LANES, LANES)
                xv = x_ref[:, ls]
                d = d_ref[:, ls]
                rs = lax.rsqrt(_group_sum(xv * xv, lo) * (1.0 / dh) + EPS)
                xn = xv * rs
                dyg = d * g_ref[0:1, :]
                dx = rs * (dyg - xn * (_group_sum(dyg * xn, lo) * (1.0 / dh)))
                o_ref[:, ls] = dx.astype(o_ref.dtype)
                dg = dg + jnp.sum(d * xn, axis=0, keepdims=True)
            dg_ref[...] += jnp.broadcast_to(dg, dg_ref.shape)

        @pl.when(sec == 0)
        def _():
            normed(dq_ref)

        @pl.when(sec == 1)
        def _():
            normed(dk_ref)

        @pl.when(sec == 2)
        def _():
            o_ref[...] = dv_ref[...].astype(o_ref.dtype)

    return pl.pallas_call(
        body,
        name=name,
        grid=(3, s // tr),
        in_specs=[blk, grad_spec(0), grad_spec(1), grad_spec(2), gspec],
        out_specs=(blk, gspec),
        out_shape=(jax.ShapeDtypeStruct((s, width), BF16), jax.ShapeDtypeStruct((24, LANES), F32)),
        compiler_params=_params(("arbitrary", "arbitrary")),
    )(qkv, dq, dk, dv, gains)


def _split_dot(v, tri):
    hi = v.astype(BF16)
    lo = (v - hi.astype(F32)).astype(BF16)
    return _dot(hi, tri, 1, 0) + _dot(lo, tri, 1, 0)


def _attn_scores(qm, kb, tri_gt, valid, r_run):
    z = _dot(qm, kb, 1, 1)
    zl = jnp.minimum(z, 0.0) - jnp.log(1.0 + jnp.exp(-jnp.abs(z)))
    lk = zl - z
    if valid is not None:
        lk = jnp.where(valid, lk, 0.0)
    after = _split_dot(lk, tri_gt)
    w = jnp.exp(zl + after + r_run)
    if valid is not None:
        w = jnp.where(valid, w, 0.0)
    return zl, lk, w


def _attn_alive(rr):
    return jnp.max(jnp.maximum(rr[0], rr[1])) > ATTN_DEAD_LOG_WEIGHT


def _attn_specs(s, width, dh, tq):
    nsec = _head_pair_geometry(width, dh)
    qspec = pl.BlockSpec((tq, LANES), lambda h, i: (i, h))
    kspec = pl.BlockSpec((s, LANES), lambda h, i: (0, nsec + h))
    vspec = pl.BlockSpec((s, LANES), lambda h, i: (0, 2 * nsec + h))
    return nsec, qspec, kspec, vspec


def _attn_masks(tq, tk, dh):
    lo = lax.broadcasted_iota(jnp.int32, (1, LANES), 1) < dh
    r_io = lax.broadcasted_iota(jnp.int32, (tk, tk), 0)
    c_io = lax.broadcasted_iota(jnp.int32, (tk, tk), 1)
    qrow = lax.broadcasted_iota(jnp.int32, (tq, tk), 0)
    kcol = lax.broadcasted_iota(jnp.int32, (tq, tk), 1)
    return lo, r_io, c_io, qrow, kcol


def _attn_fwd(qkv16, dh, *, gather_cargo=(), name):
    s, width = qkv16.shape
    tq = _tile(s, (ATTN_Q_BLOCK, ATTN_K_BLOCK))
    tk = _tile(tq, (ATTN_K_BLOCK,))
    nd = tq // tk
    nsec, qspec, kspec, vspec = _attn_specs(s, width, dh, tq)

    def body(q_ref, k_ref, v_ref, o_ref):
        i = pl.program_id(1)
        lo, r_io, c_io, qrow, kcol = _attn_masks(tq, tk, dh)
        q2 = q_ref[...]
        zq = jnp.zeros_like(q2)
        qs = (jnp.where(lo, q2, zq), jnp.where(lo, zq, q2))
        tri_gt = (r_io > c_io).astype(BF16)

        def block(j, carry, diag):
            rr, acc = carry
            start = pl.multiple_of(j * tk, tk)
            kb = k_ref[pl.ds(start, tk), :]
            vb = v_ref[pl.ds(start, tk), :]
            valid = None if diag is None else kcol + diag * tk < qrow
            outs, new_r = [], []
            for hh in range(2):
                _, lk, w = _attn_scores(qs[hh], kb, tri_gt, valid, rr[hh])
                outs.append(_dot(w.astype(BF16), vb, 1, 0))
                new_r.append(rr[hh] + jnp.sum(lk, axis=-1, keepdims=True))
            return tuple(new_r), acc + jnp.where(lo, outs[0], outs[1])

        zero = jnp.zeros((tq, 1), F32)
        carry = ((zero, zero), jnp.zeros((tq, LANES), F32))
        for dg in reversed(range(nd)):
            carry = block(i * nd + dg, carry, dg)
        below = i * nd

        def step(st):
            jj, _, cr = st
            cr = block(below - 1 - jj, cr, None)
            return jj + 1, _attn_alive(cr[0]), cr

        _, _, carry = lax.while_loop(lambda st: (st[0] < below) & st[1], step,
                                     (jnp.int32(0), _attn_alive(carry[0]), carry))
        o_ref[...] = carry[1]

    outs, gathered = _carrying_call(
        body,
        gather_cargo,
        gather=True,
        name=name,
        grid=(nsec, s // tq),
        in_specs=[qspec, kspec, vspec],
        out_specs=[qspec],
        out_shape=[jax.ShapeDtypeStruct((s, width // 3), F32)],
        scratch_shapes=[],
        operands=(qkv16, qkv16, qkv16),
        compiler_params=_params(("arbitrary", "arbitrary")),
    )
    return outs[0], gathered


def _attn_bwd(qkv16, o, do, dh, *, cargo=(), name):
    s, width = qkv16.shape
    tq = _tile(s, (ATTN_Q_BLOCK, ATTN_K_BLOCK))
    tk = _tile(tq, (ATTN_K_BLOCK,))
    nd = tq // tk
    nsec, qspec, kspec, vspec = _attn_specs(s, width, dh, tq)
    accspec = pl.BlockSpec((s, LANES), lambda h, i: (0, h))

    def body(q_ref, k_ref, v_ref, o_ref, do_ref, dq_ref, dk_ref, dv_ref):
        i = pl.program_id(1)

        @pl.when(i == 0)
        def _():
            dk_ref[...] = jnp.zeros_like(dk_ref)
            dv_ref[...] = jnp.zeros_like(dv_ref)

        lo, r_io, c_io, qrow, kcol = _attn_masks(tq, tk, dh)
        q2 = q_ref[...]
        zq = jnp.zeros_like(q2)
        qs = (jnp.where(lo, q2, zq), jnp.where(lo, zq, q2))
        do16 = do_ref[...].astype(BF16)
        dos = (jnp.where(lo, do16, zq), jnp.where(lo, zq, do16))
        prod = do16.astype(F32) * o_ref[...]
        deltas = (jnp.sum(jnp.where(lo, prod, 0.0), axis=-1, keepdims=True),
                  jnp.sum(jnp.where(lo, 0.0, prod), axis=-1, keepdims=True))
        tri_gt = (r_io > c_io).astype(BF16)
        tri_ge = (r_io >= c_io).astype(BF16)

        def block(j, carry, diag):
            rr, er, dq = carry
            start = pl.multiple_of(j * tk, tk)
            kb = k_ref[pl.ds(start, tk), :]
            vb = v_ref[pl.ds(start, tk), :]
            valid = None if diag is None else kcol + diag * tk < qrow
            new_r, new_e, dqs = [], [], []
            dk_blk = dv_blk = None
            for hh in range(2):
                zl, lk, w = _attn_scores(qs[hh], kb, tri_gt, valid, rr[hh])
                beta = jnp.exp(zl)
                w16 = w.astype(BF16)
                e = _dot(dos[hh], vb, 1, 1) * w16.astype(F32)
                e_before = deltas[hh] - er[hh] - _split_dot(e, tri_ge)
                dz = e - beta * (e + e_before)
                if valid is not None:
                    dz = jnp.where(valid, dz, 0.0)
                dz16 = dz.astype(BF16)
                dqs.append(_dot(dz16, kb, 1, 0))
                dkc = _dot(dz16, qs[hh], 0, 0)
                dvc = _dot(w16, dos[hh], 0, 0)
                dk_blk = dkc if dk_blk is None else dk_blk + dkc
                dv_blk = dvc if dv_blk is None else dv_blk + dvc
                new_r.append(rr[hh] + jnp.sum(lk, axis=-1, keepdims=True))
                new_e.append(er[hh] + jnp.sum(e, axis=-1, keepdims=True))
            dk_ref[pl.ds(start, tk), :] += dk_blk
            dv_ref[pl.ds(start, tk), :] += dv_blk
            return tuple(new_r), tuple(new_e), dq + jnp.where(lo, dqs[0], dqs[1])

        zero = jnp.zeros((tq, 1), F32)
        carry = ((zero, zero), (zero, zero), jnp.zeros((tq, LANES), F32))
        for dg in reversed(range(nd)):
            carry = block(i * nd + dg, carry, dg)
        below = i * nd

        def step(st):
            jj, _, cr = st
            cr = block(below - 1 - jj, cr, None)
            return jj + 1, _attn_alive(cr[0]), cr

        _, _, carry = lax.while_loop(lambda st: (st[0] < below) & st[1], step,
                                     (jnp.int32(0), _attn_alive(carry[0]), carry))
        dq_ref[...] = carry[2]

    shp = jax.ShapeDtypeStruct((s, width // 3), F32)
    return _carrying_call(
        body,
        cargo,
        name=name,
        grid=(nsec, s // tq),
        in_specs=[qspec, kspec, vspec, qspec, qspec],
        out_specs=(qspec, accspec, accspec),
        out_shape=(shp, shp, shp),
        scratch_shapes=[],
        operands=(qkv16, qkv16, qkv16, o, do),
        compiler_params=_params(("arbitrary", "arbitrary")),
    )


def _adam_update(wv, gv, mv, vv):
    c1 = 1.0 - ADAM_B1 ** ADAM_STEP
    c2 = 1.0 - ADAM_B2 ** ADAM_STEP
    nm = ADAM_B1 * mv + (1.0 - ADAM_B1) * gv
    nv = ADAM_B2 * vv + (1.0 - ADAM_B2) * (gv * gv)
    delta = -ADAM_LR * ((nm / c1) / (jnp.sqrt(nv / c2) + ADAM_EPS) + ADAM_WD * wv)
    return delta, nm, nv


def _sum_slabs(r, *, name):
    n, rows, lanes = r.shape
    tr = _tile(rows, (1024, 512, 256, 128, 64, 32, 16))

    def body(r_ref, o_ref):
        acc = r_ref[0]
        for d in range(1, n):
            acc = acc + r_ref[d]
        o_ref[...] = acc

    return pl.pallas_call(
        body,
        name=name,
        grid=(rows // tr,),
        in_specs=[pl.BlockSpec((n, tr, lanes), lambda i: (0, i, 0))],
        out_specs=pl.BlockSpec((tr, lanes), lambda i: (i, 0)),
        out_shape=jax.ShapeDtypeStruct((rows, lanes), F32),
        compiler_params=_params(("parallel",)),
    )(r)


def _adamw(w, g, m, v, *, name):
    rows, cols = w.shape
    tr = _tile(rows, (1024, 512, 256, 128, 64, 32, 16))
    spec = pl.BlockSpec((tr, cols), lambda i: (i, 0))

    def body(w_ref, g_ref, m_ref, v_ref, d_ref, nm_ref, nv_ref):
        d_ref[...], nm_ref[...], nv_ref[...] = _adam_update(w_ref[...], g_ref[...], m_ref[...], v_ref[...])

    shp = jax.ShapeDtypeStruct((rows, cols), F32)
    return pl.pallas_call(
        body,
        name=name,
        grid=(rows // tr,),
        in_specs=[spec, spec, spec, spec],
        out_specs=(spec, spec, spec),
        out_shape=(shp, shp, shp),
        compiler_params=_params(("parallel",)),
    )(w, g, m, v)


def _adamw_sum(w, received, m, v, *, name):
    layers, rows, cols = w.shape
    n = received[0].shape[0]
    tr = _tile(rows, (256, 128, 64, 32, 16))
    spec = pl.BlockSpec((1, tr, cols), lambda li, i: (li, i, 0))

    def partial_spec(own):
        return pl.BlockSpec((n, tr, cols), lambda li, i: (0, jnp.where(li == own, i, 0), 0))

    def body(w_ref, *refs):
        r_refs, (m_ref, v_ref, g_ref, d_ref, nm_ref, nv_ref) = refs[:layers], refs[layers:]
        for own in range(layers):
            @pl.when(pl.program_id(0) == own)
            def _(r_ref=r_refs[own]):
                gv = r_ref[0].astype(F32)
                for d in range(1, n):
                    gv = gv + r_ref[d].astype(F32)
                g_ref[0] = gv
                d_ref[0], nm_ref[0], nv_ref[0] = _adam_update(w_ref[0], gv, m_ref[0], v_ref[0])

    shp = jax.ShapeDtypeStruct((layers, rows, cols), F32)
    return pl.pallas_call(
        body,
        name=name,
        grid=(layers, rows // tr),
        in_specs=[spec] + [partial_spec(own) for own in range(layers)] + [spec, spec],
        out_specs=(spec, spec, spec, spec),
        out_shape=(shp, shp, shp, shp),
        compiler_params=_params(("arbitrary", "arbitrary")),
    )(w, *received, m, v)


_MATMUL_WEIGHTS = (("conv_w_in", 2), ("conv_w_out", 1), ("attn_w_qkv", 2), ("attn_w_o", 1), ("ffn_w_up", 2),
                   ("ffn_w_down", 1))
_TAP_WEIGHTS = (("conv_a_dw_w", 2), ("conv_b_dw_w", 2), ("ffn_dw_w", 2))
_REPLICATED = ("mix_norm_g", "ffn_norm_g", "conv_a_dw_b", "conv_a_ln_g", "conv_a_ln_b", "attn_q_g", "attn_k_g",
               "ffn_dw_b")
_WEIGHT_ORDER = ("mix_norm_g", "ffn_norm_g", "conv_w_in", "conv_a_dw_w", "conv_a_dw_b", "conv_a_ln_g",
                 "conv_a_ln_b", "conv_b_dw_w", "conv_w_out", "attn_w_qkv", "attn_q_g", "attn_k_g", "attn_w_o",
                 "ffn_w_up", "ffn_dw_w", "ffn_dw_b", "ffn_w_down")


def _assemble(gathered, axis):
    n, l, a, b = gathered.shape
    if axis == 2:
        return jnp.transpose(gathered, (1, 2, 0, 3)).reshape(l, a, n * b)
    return jnp.transpose(gathered, (1, 0, 2, 3)).reshape(l, n * a, b)


def _split_shards(full, axis):
    l, a, b = full.shape
    if axis == 2:
        return jnp.transpose(full.reshape(l, a, N_DEV, b // N_DEV), (2, 0, 1, 3))
    return jnp.transpose(full.reshape(l, N_DEV, a // N_DEV, b), (1, 0, 2, 3))


def kernel(x, mix_norm_g, ffn_norm_g, conv_w_in, conv_a_dw_w, conv_a_dw_b, conv_a_ln_g, conv_a_ln_b, conv_b_dw_w, conv_w_out, attn_w_qkv, attn_q_g, attn_k_g, attn_w_o, ffn_w_up, ffn_dw_w, ffn_dw_b, ffn_w_down, loss_target, m_mix_norm_g, m_ffn_norm_g, m_conv_w_in, m_conv_a_dw_w, m_conv_a_dw_b, m_conv_a_ln_g, m_conv_a_ln_b, m_conv_b_dw_w, m_conv_w_out, m_attn_w_qkv, m_attn_q_g, m_attn_k_g, m_attn_w_o, m_ffn_w_up, m_ffn_dw_w, m_ffn_dw_b, m_ffn_w_down, v_mix_norm_g, v_ffn_norm_g, v_conv_w_in, v_conv_a_dw_w, v_conv_a_dw_b, v_conv_a_ln_g, v_conv_a_ln_b, v_conv_b_dw_w, v_conv_w_out, v_attn_w_qkv, v_attn_q_g, v_attn_k_g, v_attn_w_o, v_ffn_w_up, v_ffn_dw_w, v_ffn_dw_b, v_ffn_w_down):
    args = dict(locals())
    weights = {n: args[n] for n in _WEIGHT_ORDER}
    mom_m = {n: args["m_" + n] for n in _WEIGHT_ORDER}
    mom_v = {n: args["v_" + n] for n in _WEIGHT_ORDER}

    _, s, d = x.shape
    depth = mix_norm_g.shape[0]
    dh = attn_q_g.shape[1]
    x0 = x.reshape(s, d)
    target = loss_target.reshape(s, d)

    shard_axis = dict(_MATMUL_WEIGHTS)
    shard16 = {n: weights[n].astype(BF16) for n, _ in _MATMUL_WEIGHTS}
    full = {}

    def layer_keys(layer):
        if layer >= depth:
            return []
        mixer = ("conv_w_in", "conv_w_out") if layer % 2 == 0 else ("attn_w_qkv", "attn_w_o")
        return [(mixer[0], layer // 2), (mixer[1], layer // 2), ("ffn_w_up", layer), ("ffn_w_down", layer)]

    def shards(keys):
        return [shard16[n][idx] for n, idx in keys]

    def assemble(keys, gathered):
        for (n, idx), g in zip(keys, gathered):
            if shard_axis[n] == 2:
                full[(n, idx)] = _assemble_cols(g, name=f"assemble_{n}_{idx}")
            else:
                full[(n, idx)] = g.reshape(N_DEV * g.shape[1], g.shape[2])

    assemble(layer_keys(0), _all_gather(shards(layer_keys(0)), "gather_layer0_weights"))
    tap_shapes = [weights[n].shape for n, _ in _TAP_WEIGHTS]
    g_tap = _all_gather([_pack([weights[n] for n, _ in _TAP_WEIGHTS], F32)], "gather_tap_weights")[0]
    for (n, axis), part in zip(_TAP_WEIGHTS, _unpack(g_tap, tap_shapes, lead=(N_DEV,))):
        full[n] = _assemble(part, axis)
    f = ffn_w_down.shape[1] * N_DEV

    saved = []
    xs = x0
    for layer in range(depth):
        i = layer // 2
        nxt = layer_keys(layer + 1)
        sv = {"x_in": xs}
        hn = _rmsnorm_fwd(xs, mix_norm_g[layer:layer + 1], name=f"mix_norm_fwd_{layer}")
        sv["h"] = hn
        if layer % 2 == 0:
            p = _matmul(hn, full[("conv_w_in", i)], name=f"conv_in_{layer}")
            (ab, a1), got = _conv_act_fwd(p, full["conv_a_dw_w"][i], conv_a_dw_b[i:i + 1], conv_a_ln_g[i:i + 1],
                                          conv_a_ln_b[i:i + 1], full["conv_b_dw_w"][i], gather_cargo=shards(nxt[0:1]),
                                          name=f"conv_act_fwd_{layer}")
            assemble(nxt[0:1], got)
            sv.update(p=p, a1=a1, mix=ab)
            xs = _matmul(ab, full[("conv_w_out", i)], add=xs, name=f"conv_out_{layer}")
            riders = [nxt[1:2], nxt[2:3], nxt[3:4]]
        else:
            reps = LANES // dh
            gains = jnp.concatenate([
                jnp.broadcast_to(jnp.tile(attn_q_g[i], reps) * dh ** -0.5, (8, LANES)),
                jnp.broadcast_to(jnp.tile(attn_k_g[i], reps), (8, LANES)),
                jnp.ones((8, LANES), F32)], axis=0)
            qkv = _matmul(hn, full[("attn_w_qkv", i)], name=f"attn_qkv_{layer}")
            qkv16 = _qkv_prep_fwd(qkv, gains, dh, name=f"qkv_prep_fwd_{layer}")
            o, got = _attn_fwd(qkv16, dh, gather_cargo=shards(nxt), name=f"attn_fwd_{layer}")
            assemble(nxt, got)
            sv.update(qkv=qkv, qkv16=qkv16, gains=gains, mix=o)
            xs = _matmul(o, full[("attn_w_o", i)], add=xs, name=f"attn_out_{layer}")
            riders = [[], [], []]
        sv["x_mid"] = xs
        h2 = _rmsnorm_fwd(xs, ffn_norm_g[layer:layer + 1], name=f"ffn_norm_fwd_{layer}")
        w_up = full[("ffn_w_up", layer)]
        ug, got = _matmul(h2, w_up[:, :f], out_dtype=BF16, gather_cargo=shards(riders[0]),
                          name=f"ffn_up_gate_{layer}")
        assemble(riders[0], got)
        uv = _matmul(h2, w_up[:, f:], out_dtype=BF16, name=f"ffn_up_val_{layer}")
        taps = full["ffn_dw_w"][layer]
        bias = ffn_dw_b[layer:layer + 1]
        act, got = _ffn_act_fwd(ug, uv, taps[:, :f], taps[:, f:], bias[:, :f], bias[:, f:],
                                gather_cargo=shards(riders[1]), name=f"ffn_act_fwd_{layer}")
        assemble(riders[1], got)
        sv.update(h2=h2, ug=ug, uv=uv, act=act)
        xs, got = _matmul(act, full[("ffn_w_down", layer)], add=xs, gather_cargo=shards(riders[2]),
                          name=f"ffn_down_{layer}")
        assemble(riders[2], got)
        saved.append(sv)

    sq, dx, dx16 = _loss_head(xs, target, name="loss_head")
    loss = lax.psum(0.5 * sq[0, 0] / d, ("x", "y", "c"))

    grads = {n: [None] * weights[n].shape[0] for n in _WEIGHT_ORDER}
    shard_axis = dict(_MATMUL_WEIGHTS)
    queued, arrived = [], {}

    def queue(n, idx, g):
        a, b = weights[n].shape[1:]
        slabs = _split_cols(g, name=f"split_{n}_{idx}") if shard_axis[n] == 2 else g.reshape(N_DEV, a, b)
        queued.append(((n, idx), slabs))

    def carried(call):
        keys = [k for k, _ in queued]
        slabs = [sl for _, sl in queued]
        queued.clear()
        outs, received = call(slabs)
        arrived.update(zip(keys, received))
        return outs

    for layer in reversed(range(depth)):
        i = layer // 2
        sv = saved[layer]
        w_up = full[("ffn_w_up", layer)]
        taps = full["ffn_dw_w"][layer]
        bias = ffn_dw_b[layer:layer + 1]
        dact = _matmul(dx16, full[("ffn_w_down", layer)], tb=True, out_dtype=BF16, name=f"ffn_down_dx_{layer}")
        queue("ffn_w_down", layer, _weight_grad(sv["act"], dx16, name=f"ffn_down_dw_{layer}"))
        dug, duv, sumg, sumv = carried(lambda cargo: _ffn_act_bwd(
            sv["ug"], sv["uv"], dact, taps[:, :f], taps[:, f:], bias[:, :f], bias[:, f:], cargo=cargo,
            name=f"ffn_act_bwd_{layer}"))
        sums = jnp.concatenate([sumg, sumv], axis=1).reshape(4, 8, 2 * f).sum(axis=1)
        grads["ffn_dw_w"][layer] = sums[:3]
        grads["ffn_dw_b"][layer] = sums[3]
        dh2 = _matmul(dug, w_up[:, :f], tb=True, name=f"ffn_up_gate_dx_{layer}")
        dh2 = _matmul(duv, w_up[:, f:], tb=True, add=dh2, name=f"ffn_up_val_dx_{layer}")
        queue("ffn_w_up", layer, jnp.concatenate(
            [_weight_grad(sv["h2"], dug, name=f"ffn_up_gate_dw_{layer}"),
             _weight_grad(sv["h2"], duv, name=f"ffn_up_val_dw_{layer}")], axis=1))
        dx, dx16, dg = _rmsnorm_bwd(sv["x_mid"], ffn_norm_g[layer:layer + 1], dh2, dx, name=f"ffn_norm_bwd_{layer}")
        grads["ffn_norm_g"][layer] = dg[0]

        if layer % 2 == 0:
            dab = _matmul(dx16, full[("conv_w_out", i)], tb=True, name=f"conv_out_dx_{layer}")
            queue("conv_w_out", i, _weight_grad(sv["mix"], dx16, name=f"conv_out_dw_{layer}"))
            ka = full["conv_a_dw_w"].shape[1]
            kb = full["conv_b_dw_w"].shape[1]
            dp, dwa, dba, dlng, dlnb, dwb = carried(lambda cargo: _conv_act_bwd(
                sv["p"], sv["a1"], dab, full["conv_a_dw_w"][i], conv_a_ln_g[i:i + 1], conv_a_ln_b[i:i + 1],
                full["conv_b_dw_w"][i], cargo=cargo, name=f"conv_act_bwd_{layer}"))
            grads["conv_a_dw_w"][i] = dwa[:ka]
            grads["conv_a_dw_b"][i] = dba[0]
            grads["conv_a_ln_g"][i] = dlng[0]
            grads["conv_a_ln_b"][i] = dlnb[0]
            grads["conv_b_dw_w"][i] = dwb[:kb]
            dhn = _matmul(dp, full[("conv_w_in", i)], tb=True, name=f"conv_in_dx_{layer}")
            queue("conv_w_in", i, _weight_grad(sv["h"], dp, name=f"conv_in_dw_{layer}"))
        else:
            do = _matmul(dx16, full[("attn_w_o", i)], tb=True, name=f"attn_out_dx_{layer}")
            queue("attn_w_o", i, _weight_grad(sv["mix"], dx16, name=f"attn_out_dw_{layer}"))
            dq, dk, dv = carried(lambda cargo: _attn_bwd(sv["qkv16"], sv["mix"], do, dh, cargo=cargo,
                                                         name=f"attn_bwd_{layer}"))
            dqkv, dgains = _qkv_prep_bwd(sv["qkv"], dq, dk, dv, sv["gains"], dh, name=f"qkv_prep_bwd_{layer}")
            grads["attn_q_g"][i] = dgains[0].reshape(-1, dh).sum(axis=0) * dh ** -0.5
            grads["attn_k_g"][i] = dgains[8].reshape(-1, dh).sum(axis=0)
            dhn = _matmul(dqkv, full[("attn_w_qkv", i)], tb=True, name=f"attn_qkv_dx_{layer}")
            queue("attn_w_qkv", i, _weight_grad(sv["h"], dqkv, name=f"attn_qkv_dw_{layer}"))
        dx, dx16, dg = _rmsnorm_bwd(sv["x_in"], mix_norm_g[layer:layer + 1], dhn, dx, name=f"mix_norm_bwd_{layer}")
        grads["mix_norm_g"][layer] = dg[0]
    carried(lambda cargo: ((), _exchange(cargo, "exchange_last_grads")))

    me = _my_index()
    final_grads, delta, new_m, new_v = {}, {}, {}, {}
    for n, _ in _MATMUL_WEIGHTS:
        final_grads[n], delta[n], new_m[n], new_v[n] = _adamw_sum(
            weights[n], [arrived[(n, idx)] for idx in range(weights[n].shape[0])], mom_m[n], mom_v[n],
            name=f"adamw_{n}")

    small_names = list(_REPLICATED) + [n for n, _ in _TAP_WEIGHTS]
    local = {n: jnp.stack(grads[n], axis=0) for n in small_names}
    small_shapes = [local[n].shape for n in small_names]
    g_small = _sum_slabs(_all_gather([_pack([local[n] for n in small_names], F32)], "gather_small_grads")[0],
                         name="sum_small_grads")
    reduced = dict(zip(small_names, _unpack(g_small, small_shapes)))
    for n in _REPLICATED:
        final_grads[n] = reduced[n]
    for n, axis in _TAP_WEIGHTS:
        final_grads[n] = lax.dynamic_index_in_dim(_split_shards(reduced[n], axis), me, axis=0, keepdims=False)
    shapes = [weights[n].shape for n in small_names]
    d_s, m_s, v_s = _adamw(
        _pack([weights[n] for n in small_names], F32), _pack([final_grads[n] for n in small_names], F32),
        _pack([mom_m[n] for n in small_names], F32), _pack([mom_v[n] for n in small_names], F32), name="adamw_small")
    delta.update(zip(small_names, _unpack(d_s, shapes)))
    new_m.update(zip(small_names, _unpack(m_s, shapes)))
    new_v.update(zip(small_names, _unpack(v_s, shapes)))

    order = list(_WEIGHT_ORDER)
    return (loss, dx.reshape(x.shape), *[final_grads[n] for n in order], *[delta[n] for n in order],
            *[new_m[n] for n in order], *[new_v[n] for n in order])
```

```python
import jax
import jax.numpy as jnp
from jax import lax
from jax.experimental import pallas as pl
from jax.experimental.pallas import tpu as pltpu

F32 = jnp.float32
BF16 = jnp.bfloat16
EPS = 1e-6
N_DEV = 8
MESH_ID = pl.DeviceIdType.MESH

ADAM_LR = 0.001
ADAM_B1 = 0.9
ADAM_B2 = 0.999
ADAM_EPS = 1e-08
ADAM_WD = 0.01
ADAM_STEP = 10

V7X_VMEM_LIMIT_BYTES = 48 * 1024 * 1024
PACK_QUANTUM = 2048
A_HALO = 32
S_HALO = 8
FFN_HALO_BLOCK = 16
LANES = 128
ATTN_Q_BLOCK = 256
ATTN_K_BLOCK = 256
MATMUL_MAX_SINGLE_K = 3072
MXU_WIDTH = 256
FFN_CHUNK_ROWS = 64
ATTN_DEAD_LOG_WEIGHT = -110.0


def _tile(n, prefs):
    for p in prefs:
        if n % p == 0:
            return p
    return n


def _params(sem=None):
    return pltpu.CompilerParams(dimension_semantics=sem, vmem_limit_bytes=V7X_VMEM_LIMIT_BYTES)


def _sigmoid(x):
    return 1.0 / (1.0 + jnp.exp(-x))


def _dot(a, b, ca, cb):
    return lax.dot_general(a, b, (((ca,), (cb,)), ((), ())), preferred_element_type=F32)


def _my_index():
    return 4 * lax.axis_index("x") + 2 * lax.axis_index("y") + lax.axis_index("c")


def _all_gather(shards, name):
    n = len(shards)

    def body(*refs):
        x_refs, out_refs = refs[:n], refs[n:2 * n]
        send_sems, recv_sems, local_sems = refs[2 * n:]
        x, y, c = lax.axis_index("x"), lax.axis_index("y"), lax.axis_index("c")
        me, sibling = (x, y, c), (x, y, 1 - c)
        chips = [(1 - x, y), (x, 1 - y), (1 - x, 1 - y)]

        def copy(a, k, block, to, src=None):
            slot = out_refs[a].at[4 * block[0] + 2 * block[1] + block[2]]
            return pltpu.make_async_remote_copy(
                src_ref=slot if src is None else src, dst_ref=slot,
                send_sem=send_sems.at[7 * a + k], recv_sem=recv_sems.at[7 * a + k],
                device_id=to, device_id_type=MESH_ID)

        mine = [pltpu.make_async_copy(x_refs[a], out_refs[a].at[4 * x + 2 * y + c], local_sems.at[a])
                for a in range(n)]
        for cp in mine:
            cp.start()
        first = []
        for a in range(n):
            first.append(copy(a, 0, me, sibling, src=x_refs[a]))
            first += [copy(a, 1 + j, me, (*chip, c), src=x_refs[a]) for j, chip in enumerate(chips)]
        for cp in first:
            cp.start()
        passed = []
        for j, chip in enumerate(chips):
            for a in range(n):
                copy(a, 1 + j, (*chip, c), me).wait_recv()
                fwd = copy(a, 4 + j, (*chip, c), sibling)
                fwd.start()
                passed.append(fwd)
        for a in range(n):
            copy(a, 0, sibling, me).wait_recv()
            for j, chip in enumerate(chips):
                copy(a, 4 + j, (*chip, 1 - c), me).wait_recv()
        for cp in first + passed:
            cp.wait_send()
        for cp in mine:
            cp.wait()

    hbm = pl.BlockSpec(memory_space=pl.ANY)
    return pl.pallas_call(
        body,
        name=name,
        out_shape=[jax.ShapeDtypeStruct((N_DEV,) + sh.shape, sh.dtype) for sh in shards],
        in_specs=[hbm] * n,
        out_specs=[hbm] * n,
        scratch_shapes=[
            pltpu.SemaphoreType.DMA((7 * n,)),
            pltpu.SemaphoreType.DMA((7 * n,)),
            pltpu.SemaphoreType.DMA((n,)),
        ],
    )(*shards)


def _exchange(slabs, name):
    n = len(slabs)

    def body(*refs):
        copies = _exchange_copies(refs[:n], refs[n:2 * n], *refs[2 * n:])
        _exchange_start(copies)
        _exchange_wait(copies)

    hbm = pl.BlockSpec(memory_space=pl.ANY)
    return pl.pallas_call(
        body,
        name=name,
        out_shape=[jax.ShapeDtypeStruct(g.shape, g.dtype) for g in slabs],
        in_specs=[hbm] * n,
        out_specs=[hbm] * n,
        scratch_shapes=_exchange_semaphores(n),
    )(*slabs)


def _exchange_semaphores(n):
    return [pltpu.SemaphoreType.DMA((7 * n,)), pltpu.SemaphoreType.DMA((7 * n,)), pltpu.SemaphoreType.DMA((n,))]


def _exchange_copies(g_refs, out_refs, send_sems, recv_sems, local_sems, gather=False):
    n = len(g_refs)
    x, y, c = lax.axis_index("x"), lax.axis_index("y"), lax.axis_index("c")
    me = 4 * x + 2 * y + c

    def part(a, d):
        return g_refs[a] if gather else g_refs[a].at[d]

    local = [pltpu.make_async_copy(part(a, me), out_refs[a].at[me], local_sems.at[a]) for a in range(n)]
    remote = []
    for k in range(1, N_DEV):
        px = (x + ((k >> 2) & 1)) % 2
        py = (y + ((k >> 1) & 1)) % 2
        pc = (c + (k & 1)) % 2
        for a in range(n):
            remote.append(pltpu.make_async_remote_copy(
                src_ref=part(a, 4 * px + 2 * py + pc), dst_ref=out_refs[a].at[me],
                send_sem=send_sems.at[7 * a + k - 1], recv_sem=recv_sems.at[7 * a + k - 1],
                device_id=(px, py, pc), device_id_type=MESH_ID))
    return local, remote


def _exchange_start(copies):
    for cp in copies[0] + copies[1]:
        cp.start()


def _exchange_wait(copies):
    local, remote = copies
    for cp in remote:
        cp.wait_recv()
    for cp in remote:
        cp.wait_send()
    for cp in local:
        cp.wait()


def _carrying_call(body, cargo, *, name, grid, in_specs, out_specs, out_shape, scratch_shapes, operands,
                   compiler_params, gather=False):
    n = len(cargo)
    n_in, n_out, n_scr = len(in_specs), len(out_specs), len(scratch_shapes)
    if n == 0:
        return pl.pallas_call(body, name=name, grid=grid, in_specs=in_specs, out_specs=out_specs,
                              out_shape=out_shape, scratch_shapes=scratch_shapes,
                              compiler_params=compiler_params)(*operands), []

    def carrying(*refs):
        cuts = [0, n_in, n_in + n, n_in + n + n_out, n_in + 2 * n + n_out, n_in + 2 * n + n_out + n_scr, len(refs)]
        ins, g_refs, outs, r_refs, scr, sems = [refs[a:b] for a, b in zip(cuts[:-1], cuts[1:])]
        steps = [pl.program_id(ax) for ax in range(len(grid))]
        first = steps[0] == 0
        last = steps[0] == grid[0] - 1
        for ax in range(1, len(grid)):
            first = first & (steps[ax] == 0)
            last = last & (steps[ax] == grid[ax] - 1)

        @pl.when(first)
        def _():
            _exchange_start(_exchange_copies(g_refs, r_refs, *sems, gather=gather))

        body(*ins, *outs, *scr)

        @pl.when(last)
        def _():
            _exchange_wait(_exchange_copies(g_refs, r_refs, *sems, gather=gather))

    hbm = pl.BlockSpec(memory_space=pl.ANY)
    lead = (N_DEV,) if gather else ()
    res = pl.pallas_call(
        carrying,
        name=name,
        grid=grid,
        in_specs=list(in_specs) + [hbm] * n,
        out_specs=list(out_specs) + [hbm] * n,
        out_shape=list(out_shape) + [jax.ShapeDtypeStruct(lead + g.shape, g.dtype) for g in cargo],
        scratch_shapes=list(scratch_shapes) + _exchange_semaphores(n),
        compiler_params=compiler_params,
    )(*operands, *cargo)
    return res[:n_out], res[n_out:]


def _assemble_cols(g, *, name):
    n, r, w = g.shape
    tr = _tile(r, (256, 128, 64, 32, 16))

    def body(g_ref, o_ref):
        for j in range(n):
            o_ref[:, j * w:(j + 1) * w] = g_ref[j]

    return pl.pallas_call(
        body,
        name=name,
        grid=(r // tr,),
        in_specs=[pl.BlockSpec((n, tr, w), lambda i: (0, i, 0))],
        out_specs=pl.BlockSpec((tr, n * w), lambda i: (i, 0)),
        out_shape=jax.ShapeDtypeStruct((r, n * w), g.dtype),
        compiler_params=_params(("parallel",)),
    )(g)


def _split_cols(x, *, name):
    r, nw = x.shape
    w = nw // N_DEV
    tr = _tile(r, (256, 128, 64, 32, 16))

    def body(x_ref, o_ref):
        for j in range(N_DEV):
            o_ref[j] = x_ref[:, j * w:(j + 1) * w]

    return pl.pallas_call(
        body,
        name=name,
        grid=(r // tr,),
        in_specs=[pl.BlockSpec((tr, nw), lambda i: (i, 0))],
        out_specs=pl.BlockSpec((N_DEV, tr, w), lambda i: (0, i, 0)),
        out_shape=jax.ShapeDtypeStruct((N_DEV, r, w), x.dtype),
        compiler_params=_params(("parallel",)),
    )(x)


def _padded(n):
    return -(-n // PACK_QUANTUM) * PACK_QUANTUM


def _pack(parts, dtype, lead=()):
    flat = []
    for p in parts:
        p = p.astype(dtype).reshape(lead + (-1,))
        n = p.shape[-1]
        flat.append(jnp.pad(p, [(0, 0)] * len(lead) + [(0, _padded(n) - n)]))
    return jnp.concatenate(flat, axis=-1).reshape(lead + (-1, 128))


def _unpack(buf, shapes, lead=()):
    flat = buf.reshape(lead + (-1,))
    out, off = [], 0
    for shp in shapes:
        n = 1
        for d in shp:
            n *= d
        out.append(flat[..., off:off + n].reshape(lead + tuple(shp)))
        off += _padded(n)
    return out


def _half_if_tiled(n):
    half = n // 2
    return (half,) if n % 2 == 0 and half % LANES == 0 and half <= 1536 else ()


def _matmul(a, b, *, ta=False, tb=False, out_dtype=F32, add=None, gather_cargo=None, name):
    if ta:
        kdim, m = a.shape
    else:
        m, kdim = a.shape
    n = b.shape[0] if tb else b.shape[1]
    bm = _tile(m, (1024,) + _half_if_tiled(m) + (512, 256, 128))
    bn = _tile(n, (512, 256, 128))
    if bn < 512 and n % MXU_WIDTH == 0 and n <= MATMUL_MAX_SINGLE_K:
        bn, bm = n, _tile(m, (512, 256, 128))
    bk = kdim if kdim <= MATMUL_MAX_SINGLE_K else _tile(kdim, (1024, 512, 256, 128))
    nk = kdim // bk

    a_spec = pl.BlockSpec((bk, bm), lambda i, j, k: (k, i)) if ta else pl.BlockSpec((bm, bk), lambda i, j, k: (i, k))
    b_spec = pl.BlockSpec((bn, bk), lambda i, j, k: (j, k)) if tb else pl.BlockSpec((bk, bn), lambda i, j, k: (k, j))
    o_spec = pl.BlockSpec((bm, bn), lambda i, j, k: (i, j))
    in_specs = [a_spec, b_spec] + ([o_spec] if add is not None else [])
    operands = [a, b] + ([add] if add is not None else [])

    def product(a_ref, b_ref):
        return _dot(a_ref[...].astype(BF16), b_ref[...].astype(BF16), 0 if ta else 1, 1 if tb else 0)

    def finish(r, refs):
        if add is not None:
            r = r + refs[2][...]
        return r

    def body_single(*refs):
        o_ref = refs[-1]
        o_ref[...] = finish(product(refs[0], refs[1]), refs).astype(o_ref.dtype)

    def body_multi(*refs):
        o_ref, acc_ref = refs[-2], refs[-1]
        k = pl.program_id(2)

        @pl.when(k == 0)
        def _():
            acc_ref[...] = jnp.zeros_like(acc_ref)

        acc_ref[...] += product(refs[0], refs[1])

        @pl.when(k == nk - 1)
        def _():
            o_ref[...] = finish(acc_ref[...], refs).astype(o_ref.dtype)

    outs, received = _carrying_call(
        body_single if nk == 1 else body_multi,
        () if gather_cargo is None else gather_cargo,
        gather=True,
        name=name,
        grid=(m // bm, n // bn, nk),
        in_specs=in_specs,
        out_specs=[o_spec],
        out_shape=[jax.ShapeDtypeStruct((m, n), out_dtype)],
        scratch_shapes=[] if nk == 1 else [pltpu.VMEM((bm, bn), F32)],
        operands=operands,
        compiler_params=_params(("parallel", "parallel", "arbitrary") if gather_cargo is None else ("arbitrary",) * 3),
    )
    return outs[0] if gather_cargo is None else (outs[0], received)


def _weight_grad(x, dy, *, name):
    return _matmul(x, dy, ta=True, out_dtype=BF16, name=name)


def _rmsnorm_fwd(x, g, *, name):
    r, d = x.shape
    tr = _tile(r, (1024, 512, 256, 128))

    def body(x_ref, g_ref, o_ref):
        xv = x_ref[...]
        y = xv * lax.rsqrt(jnp.mean(xv * xv, axis=-1, keepdims=True) + EPS)
        y = y * g_ref[...]
        o_ref[...] = y.astype(o_ref.dtype)

    return pl.pallas_call(
        body,
        name=name,
        grid=(r // tr,),
        in_specs=[pl.BlockSpec((tr, d), lambda i: (i, 0)), pl.BlockSpec((1, d), lambda i: (0, 0))],
        out_specs=pl.BlockSpec((tr, d), lambda i: (i, 0)),
        out_shape=jax.ShapeDtypeStruct((r, d), BF16),
        compiler_params=_params(("parallel",)),
    )(x, g)


def _rmsnorm_bwd(x, g, dy, res, *, name):
    r, d = x.shape
    tr = _tile(r, (512, 256, 128))
    row = pl.BlockSpec((tr, d), lambda i: (i, 0))
    vec = pl.BlockSpec((1, d), lambda i: (0, 0))
    operands = [x, g, dy] + ([res] if res is not None else [])

    def body(*refs):
        if res is not None:
            x_ref, g_ref, dy_ref, res_ref, dx_ref, dx16_ref, dg_ref = refs
        else:
            x_ref, g_ref, dy_ref, dx_ref, dx16_ref, dg_ref = refs
        xv = x_ref[...]
        dyv = dy_ref[...].astype(F32)
        rs = lax.rsqrt(jnp.mean(xv * xv, axis=-1, keepdims=True) + EPS)
        xn = xv * rs
        dyg = dyv * g_ref[...]
        dx = rs * (dyg - xn * jnp.mean(dyg * xn, axis=-1, keepdims=True))
        if res is not None:
            dx = dx + res_ref[...]
        dx_ref[...] = dx
        dx16_ref[...] = dx.astype(dx16_ref.dtype)

        @pl.when(pl.program_id(0) == 0)
        def _():
            dg_ref[...] = jnp.zeros_like(dg_ref)

        dg_ref[...] += jnp.sum(dyv * xn, axis=0, keepdims=True)

    return pl.pallas_call(
        body,
        name=name,
        grid=(r // tr,),
        in_specs=[row, vec, row] + ([row] if res is not None else []),
        out_specs=(row, row, vec),
        out_shape=(jax.ShapeDtypeStruct((r, d), F32), jax.ShapeDtypeStruct((r, d), BF16),
                   jax.ShapeDtypeStruct((1, d), F32)),
        compiler_params=_params(("arbitrary",)),
    )(*operands)


def _loss_head(y, target, *, name):
    s, d = y.shape
    ts = _tile(s, (512, 256, 128))
    row = pl.BlockSpec((ts, d), lambda i: (i, 0))
    acc = pl.BlockSpec((8, 128), lambda i: (0, 0))

    def body(y_ref, t_ref, l_ref, dy_ref, dy16_ref):
        e = y_ref[...] - t_ref[...]
        dy = e * (1.0 / d)
        dy_ref[...] = dy
        dy16_ref[...] = dy.astype(dy16_ref.dtype)

        @pl.when(pl.program_id(0) == 0)
        def _():
            l_ref[...] = jnp.zeros_like(l_ref)

        sq = jnp.sum(jnp.sum(e * e, axis=-1, keepdims=True), axis=0, keepdims=True)
        l_ref[...] += jnp.broadcast_to(sq, l_ref.shape)

    return pl.pallas_call(
        body,
        name=name,
        grid=(s // ts,),
        in_specs=[row, row],
        out_specs=(acc, row, row),
        out_shape=(jax.ShapeDtypeStruct((8, 128), F32), jax.ShapeDtypeStruct((s, d), F32),
                   jax.ShapeDtypeStruct((s, d), BF16)),
        compiler_params=_params(("arbitrary",)),
    )(y, target)


def _ffn_tiles(s, f):
    return _tile(s, (512, 256, 128)), _tile(f, _half_if_tiled(f) + (512, 256, 128))


def _conv3(xs, w, b):
    return b + xs[0] * w[0:1] + xs[1] * w[1:2] + xs[2] * w[2:3]


def _shifted3(x, rows):
    return [x[S_HALO - 2 + k:S_HALO - 2 + k + rows] for k in range(3)]


def _ffn_act_fwd(ug, uv, wg, wv, bg, bv, *, gather_cargo=(), name):
    s, f = ug.shape
    ts, tc = _ffn_tiles(s, f)
    rc = _tile(ts, (FFN_CHUNK_ROWS, 32, 16, 8))
    hb = ts // FFN_HALO_BLOCK
    main = pl.BlockSpec((ts, tc), lambda j, i: (i, j))
    prev = pl.BlockSpec((FFN_HALO_BLOCK, tc), lambda j, i: (jnp.maximum(i * hb - 1, 0), j))
    taps = pl.BlockSpec((3, tc), lambda j, i: (0, j))
    bias = pl.BlockSpec((1, tc), lambda j, i: (0, j))
    lead = FFN_HALO_BLOCK - S_HALO

    def body(ug_ref, ugp_ref, uv_ref, uvp_ref, wg_ref, wv_ref, bg_ref, bv_ref, o_ref, gbuf, vbuf):
        first = pl.program_id(1) == 0
        for buf, p_ref, m_ref in ((gbuf, ugp_ref, ug_ref), (vbuf, uvp_ref, uv_ref)):
            buf[0:FFN_HALO_BLOCK, :] = jnp.where(first, 0.0, p_ref[...].astype(F32))
            buf[FFN_HALO_BLOCK:, :] = m_ref[...].astype(F32)

        def chunk(ci, carry):
            r0 = pl.multiple_of(ci * rc, rc)
            rows = pl.ds(pl.multiple_of(r0 + lead, S_HALO), rc + S_HALO)
            for lb in range(tc // LANES):
                ls = pl.ds(lb * LANES, LANES)
                gate = _conv3(_shifted3(gbuf[rows, ls], rc), wg_ref[:, ls], bg_ref[:, ls])
                val = _conv3(_shifted3(vbuf[rows, ls], rc), wv_ref[:, ls], bv_ref[:, ls])
                o_ref[pl.ds(r0, rc), ls] = (gate * _sigmoid(gate) * val).astype(o_ref.dtype)
            return carry

        lax.fori_loop(0, ts // rc, chunk, 0)

    outs, gathered = _carrying_call(
        body,
        gather_cargo,
        gather=True,
        name=name,
        grid=(f // tc, s // ts),
        in_specs=[main, prev, main, prev, taps, taps, bias, bias],
        out_specs=[main],
        out_shape=[jax.ShapeDtypeStruct((s, f), BF16)],
        scratch_shapes=[pltpu.VMEM((FFN_HALO_BLOCK + ts, tc), F32), pltpu.VMEM((FFN_HALO_BLOCK + ts, tc), F32)],
        operands=(ug, ug, uv, uv, wg, wv, bg, bv),
        compiler_params=_params(("arbitrary", "arbitrary")),
    )
    return outs[0], gathered


def _ffn_act_bwd(ug, uv, df, wg, wv, bg, bv, *, cargo=(), name):
    s, f = ug.shape
    ts, tc = _ffn_tiles(s, f)
    rc = _tile(ts, (FFN_CHUNK_ROWS, 32, 16, 8))
    hb = ts // FFN_HALO_BLOCK
    last_hb = s // FFN_HALO_BLOCK - 1
    n_row = s // ts
    lead = FFN_HALO_BLOCK - S_HALO
    main = pl.BlockSpec((ts, tc), lambda j, i: (i, j))
    prev = pl.BlockSpec((FFN_HALO_BLOCK, tc), lambda j, i: (jnp.maximum(i * hb - 1, 0), j))
    nxt = pl.BlockSpec((FFN_HALO_BLOCK, tc), lambda j, i: (jnp.minimum((i + 1) * hb, last_hb), j))
    taps = pl.BlockSpec((3, tc), lambda j, i: (0, j))
    bias = pl.BlockSpec((1, tc), lambda j, i: (0, j))
    sums = pl.BlockSpec((32, tc), lambda j, i: (0, j))
    ext = rc + S_HALO

    def body(ug_ref, ugp_ref, ugn_ref, uv_ref, uvp_ref, uvn_ref, df_ref, dfn_ref, wg_ref, wv_ref, bg_ref, bv_ref,
             dug_ref, duv_ref, sumg_ref, sumv_ref, gbuf, vbuf, dfbuf):
        i = pl.program_id(1)
        first = i == 0
        last = i == n_row - 1

        @pl.when(first)
        def _():
            sumg_ref[...] = jnp.zeros_like(sumg_ref)
            sumv_ref[...] = jnp.zeros_like(sumv_ref)

        for buf, p_ref, m_ref, n_ref in ((gbuf, ugp_ref, ug_ref, ugn_ref), (vbuf, uvp_ref, uv_ref, uvn_ref)):
            buf[0:FFN_HALO_BLOCK, :] = jnp.where(first, 0.0, p_ref[...].astype(F32))
            buf[FFN_HALO_BLOCK:FFN_HALO_BLOCK + ts, :] = m_ref[...].astype(F32)
            buf[FFN_HALO_BLOCK + ts:, :] = n_ref[...].astype(F32)
        dfbuf[0:ts, :] = df_ref[...].astype(F32)
        dfbuf[ts:, :] = jnp.where(last, 0.0, dfn_ref[...].astype(F32))

        def chunk(ci, carry):
            r0 = pl.multiple_of(ci * rc, rc)
            rows = pl.ds(pl.multiple_of(r0 + lead, S_HALO), ext + S_HALO)
            for lb in range(tc // LANES):
                ls = pl.ds(lb * LANES, LANES)
                xg = _shifted3(gbuf[rows, ls], ext)
                xv = _shifted3(vbuf[rows, ls], ext)
                wgv, wvv = wg_ref[:, ls], wv_ref[:, ls]
                gate = _conv3(xg, wgv, bg_ref[:, ls])
                val = _conv3(xv, wvv, bv_ref[:, ls])
                dfe = dfbuf[pl.ds(r0, ext), ls]
                sg = _sigmoid(gate)
                dgate = dfe * val * (sg * (1.0 + gate * (1.0 - sg)))
                dval = dfe * (gate * sg)
                for dz, xs, w, du_ref, sum_ref in ((dgate, xg, wgv, dug_ref, sumg_ref),
                                                   (dval, xv, wvv, duv_ref, sumv_ref)):
                    du = dz[2:2 + rc] * w[0:1] + dz[1:1 + rc] * w[1:2] + dz[0:rc] * w[2:3]
                    du_ref[pl.ds(r0, rc), ls] = du.astype(du_ref.dtype)
                    dm = dz[0:rc]
                    for k in range(3):
                        sum_ref[8 * k:8 * k + 8, ls] += (dm * xs[k][0:rc]).reshape(rc // 8, 8, LANES).sum(axis=0)
                    sum_ref[24:32, ls] += dm.reshape(rc // 8, 8, LANES).sum(axis=0)
            return carry

        lax.fori_loop(0, ts // rc, chunk, 0)

    return _carrying_call(
        body,
        cargo,
        name=name,
        grid=(f // tc, n_row),
        in_specs=[main, prev, nxt, main, prev, nxt, main, nxt, taps, taps, bias, bias],
        out_specs=(main, main, sums, sums),
        out_shape=(jax.ShapeDtypeStruct((s, f), BF16), jax.ShapeDtypeStruct((s, f), BF16),
                   jax.ShapeDtypeStruct((32, f), F32), jax.ShapeDtypeStruct((32, f), F32)),
        scratch_shapes=[pltpu.VMEM((2 * FFN_HALO_BLOCK + ts, tc), F32), pltpu.VMEM((2 * FFN_HALO_BLOCK + ts, tc), F32),
                        pltpu.VMEM((FFN_HALO_BLOCK + ts, tc), F32)],
        operands=(ug, ug, ug, uv, uv, uv, df, df, wg, wv, bg, bv),
        compiler_params=_params(("arbitrary", "arbitrary")),
    )


def _layernorm_stats(a1):
    mu = jnp.mean(a1, axis=-1, keepdims=True)
    xc = a1 - mu
    rstd = lax.rsqrt(jnp.mean(xc * xc, axis=-1, keepdims=True) + EPS)
    return xc * rstd, rstd


def _conv_act_fwd(p, wa, ba, lng, lnb, wb, *, gather_cargo=(), name):
    s, c5 = p.shape
    c = c5 // 5
    ka, kb = wa.shape[0], wb.shape[0]
    ts = _tile(s, (256, 128))
    hb = ts // A_HALO
    main = pl.BlockSpec((ts, c5), lambda i: (i, 0))
    prev = pl.BlockSpec((A_HALO, c5), lambda i: (jnp.maximum(i * hb - 1, 0), 0))

    def full(arr):
        return pl.BlockSpec(arr.shape, lambda i: (0, 0))

    def body(p_ref, pp_ref, wa_ref, ba_ref, lng_ref, lnb_ref, wb_ref, ab_ref, a1_ref, gbuf, cbuf):
        first = pl.program_id(0) == 0
        gbuf[0:A_HALO, :] = jnp.where(first, 0.0, pp_ref[:, 0:c] * _sigmoid(pp_ref[:, c:2 * c]))
        gbuf[A_HALO:, :] = p_ref[:, 0:c] * _sigmoid(p_ref[:, c:2 * c])
        cbuf[0:A_HALO, :] = jnp.where(first, 0.0, pp_ref[:, 3 * c:4 * c] * pp_ref[:, 4 * c:5 * c])
        cbuf[A_HALO:, :] = p_ref[:, 3 * c:4 * c] * p_ref[:, 4 * c:5 * c]

        a1 = jnp.broadcast_to(ba_ref[...], (ts, c))
        for k in range(ka):
            off = A_HALO - (ka - 1) + k
            a1 = a1 + gbuf[off:off + ts, :] * wa_ref[k:k + 1, :]
        a1_ref[...] = a1
        xhat, _ = _layernorm_stats(a1)
        a2 = xhat * lng_ref[...] + lnb_ref[...]
        ab_ref[:, 0:c] = (a2 * _sigmoid(a2)).astype(ab_ref.dtype)

        cv = jnp.zeros((ts, c), F32)
        for k in range(kb):
            off = A_HALO - (kb - 1) + k
            cv = cv + cbuf[off:off + ts, :] * wb_ref[k:k + 1, :]
        ab_ref[:, c:2 * c] = (p_ref[:, 2 * c:3 * c] * cv).astype(ab_ref.dtype)

    return _carrying_call(
        body,
        gather_cargo,
        gather=True,
        name=name,
        grid=(s // ts,),
        in_specs=[main, prev, full(wa), full(ba), full(lng), full(lnb), full(wb)],
        out_specs=(pl.BlockSpec((ts, 2 * c), lambda i: (i, 0)), pl.BlockSpec((ts, c), lambda i: (i, 0))),
        out_shape=(jax.ShapeDtypeStruct((s, 2 * c), BF16), jax.ShapeDtypeStruct((s, c), F32)),
        scratch_shapes=[pltpu.VMEM((A_HALO + ts, c), F32), pltpu.VMEM((A_HALO + ts, c), F32)],
        operands=(p, p, wa, ba, lng, lnb, wb),
        compiler_params=_params(("arbitrary",)),
    )


def _conv_act_bwd(p, a1, dab, wa, lng, lnb, wb, *, cargo=(), name):
    s, c5 = p.shape
    c = c5 // 5
    ka, kb = wa.shape[0], wb.shape[0]
    ts = _tile(s, (256, 128))
    hb = ts // A_HALO
    n_row = s // ts
    last_hb = s // A_HALO - 1
    ext = ts + A_HALO

    def rows(width):
        return (pl.BlockSpec((ts, width), lambda i: (i, 0)),
                pl.BlockSpec((A_HALO, width), lambda i: (jnp.maximum(i * hb - 1, 0), 0)),
                pl.BlockSpec((A_HALO, width), lambda i: (jnp.minimum((i + 1) * hb, last_hb), 0)))

    p_main, p_prev, p_next = rows(c5)
    d_main, _, d_next = rows(2 * c)
    a_main, _, a_next = rows(c)

    def full(shape):
        return pl.BlockSpec(shape, lambda i: (0, 0))

    def body(p_ref, pp_ref, pn_ref, a1_ref, a1n_ref, d_ref, dn_ref, wa_ref, lng_ref, lnb_ref, wb_ref,
             dp_ref, dwa_ref, dba_ref, dlng_ref, dlnb_ref, dwb_ref, gbuf, cbuf, da1buf, dcvbuf):
        i = pl.program_id(0)
        first = i == 0
        last = i == n_row - 1

        @pl.when(first)
        def _():
            dwa_ref[...] = jnp.zeros_like(dwa_ref)
            dba_ref[...] = jnp.zeros_like(dba_ref)
            dlng_ref[...] = jnp.zeros_like(dlng_ref)
            dlnb_ref[...] = jnp.zeros_like(dlnb_ref)
            dwb_ref[...] = jnp.zeros_like(dwb_ref)

        sig_gate = _sigmoid(p_ref[:, c:2 * c])
        gbuf[0:A_HALO, :] = jnp.where(first, 0.0, pp_ref[:, 0:c] * _sigmoid(pp_ref[:, c:2 * c]))
        gbuf[A_HALO:, :] = p_ref[:, 0:c] * sig_gate
        cbuf[0:A_HALO, :] = jnp.where(first, 0.0, pp_ref[:, 3 * c:4 * c] * pp_ref[:, 4 * c:5 * c])
        cbuf[A_HALO:, :] = p_ref[:, 3 * c:4 * c] * p_ref[:, 4 * c:5 * c]

        def ln_back(a1v, dav):
            xhat, rstd = _layernorm_stats(a1v)
            a2 = xhat * lng_ref[...] + lnb_ref[...]
            sg = _sigmoid(a2)
            da2 = dav * (sg * (1.0 + a2 * (1.0 - sg)))
            dxh = da2 * lng_ref[...]
            da1 = rstd * (dxh - jnp.mean(dxh, axis=-1, keepdims=True)
                          - xhat * jnp.mean(dxh * xhat, axis=-1, keepdims=True))
            return da1, da2, xhat

        da1_m, da2_m, xhat_m = ln_back(a1_ref[...], d_ref[:, 0:c])
        da1_n, _, _ = ln_back(a1n_ref[...], dn_ref[:, 0:c])
        da1buf[0:ts, :] = da1_m
        da1buf[ts:, :] = jnp.where(last, 0.0, da1_n)
        dlng_ref[...] += jnp.sum(da2_m * xhat_m, axis=0, keepdims=True)
        dlnb_ref[...] += jnp.sum(da2_m, axis=0, keepdims=True)
        dba_ref[...] += jnp.sum(da1_m, axis=0, keepdims=True)

        dglu = jnp.zeros((ts, c), F32)
        for k in range(ka):
            off = (ka - 1) - k
            dglu = dglu + da1buf[off:off + ts, :] * wa_ref[k:k + 1, :]
            goff = A_HALO - (ka - 1) + k
            dwa_ref[k:k + 1, :] += jnp.sum(da1_m * gbuf[goff:goff + ts, :], axis=0, keepdims=True)
        dp_ref[:, 0:c] = (dglu * sig_gate).astype(dp_ref.dtype)
        dp_ref[:, c:2 * c] = (dglu * p_ref[:, 0:c] * sig_gate * (1.0 - sig_gate)).astype(dp_ref.dtype)

        cv = jnp.zeros((ts, c), F32)
        for k in range(kb):
            off = A_HALO - (kb - 1) + k
            cv = cv + cbuf[off:off + ts, :] * wb_ref[k:k + 1, :]
        db_m = d_ref[:, c:2 * c]
        dp_ref[:, 2 * c:3 * c] = (db_m * cv).astype(dp_ref.dtype)
        dcv_m = db_m * p_ref[:, 2 * c:3 * c]
        dcvbuf[0:ts, :] = dcv_m
        dcvbuf[ts:, :] = jnp.where(last, 0.0, dn_ref[:, c:2 * c] * pn_ref[:, 2 * c:3 * c])
        dbc = jnp.zeros((ts, c), F32)
        for k in range(kb):
            off = (kb - 1) - k
            dbc = dbc + dcvbuf[off:off + ts, :] * wb_ref[k:k + 1, :]
            coff = A_HALO - (kb - 1) + k
            dwb_ref[k:k + 1, :] += jnp.sum(dcv_m * cbuf[coff:coff + ts, :], axis=0, keepdims=True)
        dp_ref[:, 3 * c:4 * c] = (dbc * p_ref[:, 4 * c:5 * c]).astype(dp_ref.dtype)
        dp_ref[:, 4 * c:5 * c] = (dbc * p_ref[:, 3 * c:4 * c]).astype(dp_ref.dtype)

    return _carrying_call(
        body,
        cargo,
        name=name,
        grid=(n_row,),
        in_specs=[p_main, p_prev, p_next, a_main, a_next, d_main, d_next,
                  full(wa.shape), full(lng.shape), full(lnb.shape), full(wb.shape)],
        out_specs=(pl.BlockSpec((ts, c5), lambda i: (i, 0)), full((32, c)), full((1, c)), full((1, c)),
                   full((1, c)), full((8, c))),
        out_shape=(
            jax.ShapeDtypeStruct((s, c5), BF16), jax.ShapeDtypeStruct((32, c), F32),
            jax.ShapeDtypeStruct((1, c), F32), jax.ShapeDtypeStruct((1, c), F32),
            jax.ShapeDtypeStruct((1, c), F32), jax.ShapeDtypeStruct((8, c), F32),
        ),
        scratch_shapes=[
            pltpu.VMEM((A_HALO + ts, c), F32), pltpu.VMEM((A_HALO + ts, c), F32),
            pltpu.VMEM((ext, c), F32), pltpu.VMEM((ext, c), F32),
        ],
        operands=(p, p, p, a1, a1, dab, dab, wa, lng, lnb, wb),
        compiler_params=_params(("arbitrary",)),
    )


def _head_pair_geometry(width, dh):
    assert 2 * dh == LANES, "the attention kernels pair two heads in one 128-lane block"
    return width // (3 * LANES)


def _group_sum(v, lo):
    s_lo = jnp.sum(jnp.where(lo, v, 0.0), axis=-1, keepdims=True)
    s_hi = jnp.sum(jnp.where(lo, 0.0, v), axis=-1, keepdims=True)
    return jnp.where(lo, s_lo, s_hi)


def _qkv_prep_fwd(qkv, gains, dh, *, name):
    s, width = qkv.shape
    nsec = _head_pair_geometry(width, dh)
    tr = _tile(s, (256, 128))
    blk = pl.BlockSpec((tr, nsec * LANES), lambda i, sec: (i, sec))

    def body(x_ref, g_ref, o_ref):
        sec = pl.program_id(1)

        @pl.when(sec == 2)
        def _():
            o_ref[...] = x_ref[...].astype(o_ref.dtype)

        @pl.when(sec != 2)
        def _():
            lo = lax.broadcasted_iota(jnp.int32, (1, LANES), 1) < dh
            for lb in range(nsec):
                ls = pl.ds(lb * LANES, LANES)
                xv = x_ref[:, ls]
                rs = lax.rsqrt(_group_sum(xv * xv, lo) * (1.0 / dh) + EPS)
                o_ref[:, ls] = (xv * rs * g_ref[0:1, :]).astype(o_ref.dtype)

    return pl.pallas_call(
        body,
        name=name,
        grid=(s // tr, 3),
        in_specs=[blk, pl.BlockSpec((8, LANES), lambda i, sec: (sec, 0))],
        out_specs=blk,
        out_shape=jax.ShapeDtypeStruct((s, width), BF16),
        compiler_params=_params(("parallel", "parallel")),
    )(qkv, gains)


def _qkv_prep_bwd(qkv, dq, dk, dv, gains, dh, *, name):
    s, width = qkv.shape
    nsec = _head_pair_geometry(width, dh)
    tr = _tile(s, (256, 128))
    blk = pl.BlockSpec((tr, nsec * LANES), lambda sec, i: (i, sec))
    gspec = pl.BlockSpec((8, LANES), lambda sec, i: (sec, 0))

    def grad_spec(own):
        return pl.BlockSpec((tr, nsec * LANES), lambda sec, i: (jnp.where(sec == own, i, 0), 0))

    def body(x_ref, dq_ref, dk_ref, dv_ref, g_ref, o_ref, dg_ref):
        sec = pl.program_id(0)

        @pl.when(pl.program_id(1) == 0)
        def _():
            dg_ref[...] = jnp.zeros_like(dg_ref)

        def normed(d_ref):
            lo = lax.broadcasted_iota(jnp.int32, (1, LANES), 1) < dh
            dg = jnp.zeros((1, LANES), F32)
            for lb in range(nsec):
                ls = pl.ds(lb * LANES, LANES)
                xv = x_ref[:, ls]
                d = d_ref[:, ls]
                rs = lax.rsqrt(_group_sum(xv * xv, lo) * (1.0 / dh) + EPS)
                xn = xv * rs
                dyg = d * g_ref[0:1, :]
                dx = rs * (dyg - xn * (_group_sum(dyg * xn, lo) * (1.0 / dh)))
                o_ref[:, ls] = dx.astype(o_ref.dtype)
                dg = dg + jnp.sum(d * xn, axis=0, keepdims=True)
            dg_ref[...] += jnp.broadcast_to(dg, dg_ref.shape)

        @pl.when(sec == 0)
        def _():
            normed(dq_ref)

        @pl.when(sec == 1)
        def _():
            normed(dk_ref)

        @pl.when(sec == 2)
        def _():
            o_ref[...] = dv_ref[...].astype(o_ref.dtype)

    return pl.pallas_call(
        body,
        name=name,
        grid=(3, s // tr),
        in_specs=[blk, grad_spec(0), grad_spec(1), grad_spec(2), gspec],
        out_specs=(blk, gspec),
        out_shape=(jax.ShapeDtypeStruct((s, width), BF16), jax.ShapeDtypeStruct((24, LANES), F32)),
        compiler_params=_params(("arbitrary", "arbitrary")),
    )(qkv, dq, dk, dv, gains)


def _split_dot(v, tri):
    hi = v.astype(BF16)
    lo = (v - hi.astype(F32)).astype(BF16)
    return _dot(hi, tri, 1, 0) + _dot(lo, tri, 1, 0)


def _attn_scores(qm, kb, tri_gt, valid, r_run):
    z = _dot(qm, kb, 1, 1)
    zl = jnp.minimum(z, 0.0) - jnp.log(1.0 + jnp.exp(-jnp.abs(z)))
    lk = zl - z
    if valid is not None:
        lk = jnp.where(valid, lk, 0.0)
    after = _split_dot(lk, tri_gt)
    w = jnp.exp(zl + after + r_run)
    if valid is not None:
        w = jnp.where(valid, w, 0.0)
    return zl, lk, w


def _attn_alive(rr):
    return jnp.max(jnp.maximum(rr[0], rr[1])) > ATTN_DEAD_LOG_WEIGHT


def _attn_walk_variants(i, walk):
    @pl.when(i == 0)
    def _():
        walk(False)

    @pl.when(i > 0)
    def _():
        walk(True)


def _attn_specs(s, width, dh, tq):
    nsec = _head_pair_geometry(width, dh)
    qspec = pl.BlockSpec((tq, LANES), lambda h, i: (i, h))
    kspec = pl.BlockSpec((s, LANES), lambda h, i: (0, nsec + h))
    vspec = pl.BlockSpec((s, LANES), lambda h, i: (0, 2 * nsec + h))
    return nsec, qspec, kspec, vspec


def _attn_masks(tq, tk, dh):
    lo = lax.broadcasted_iota(jnp.int32, (1, LANES), 1) < dh
    r_io = lax.broadcasted_iota(jnp.int32, (tk, tk), 0)
    c_io = lax.broadcasted_iota(jnp.int32, (tk, tk), 1)
    qrow = lax.broadcasted_iota(jnp.int32, (tq, tk), 0)
    kcol = lax.broadcasted_iota(jnp.int32, (tq, tk), 1)
    return lo, r_io, c_io, qrow, kcol


def _attn_fwd(qkv16, dh, *, gather_cargo=(), name):
    s, width = qkv16.shape
    tq = _tile(s, (ATTN_Q_BLOCK, ATTN_K_BLOCK))
    tk = _tile(tq, (ATTN_K_BLOCK,))
    nd = tq // tk
    nsec, qspec, kspec, vspec = _attn_specs(s, width, dh, tq)

    def body(q_ref, k_ref, v_ref, o_ref):
        i = pl.program_id(1)
        lo, r_io, c_io, qrow, kcol = _attn_masks(tq, tk, dh)
        q2 = q_ref[...]
        zq = jnp.zeros_like(q2)
        qs = (jnp.where(lo, q2, zq), jnp.where(lo, zq, q2))
        tri_gt = (r_io > c_io).astype(BF16)

        def block(j, carry, diag):
            rr, acc = carry
            start = pl.multiple_of(j * tk, tk)
            kb = k_ref[pl.ds(start, tk), :]
            vb = v_ref[pl.ds(start, tk), :]
            valid = None if diag is None else kcol + diag * tk < qrow
            outs, new_r = [], []
            for hh in range(2):
                _, lk, w = _attn_scores(qs[hh], kb, tri_gt, valid, rr[hh])
                outs.append(_dot(w.astype(BF16), vb, 1, 0))
                new_r.append(rr[hh] + jnp.sum(lk, axis=-1, keepdims=True))
            return tuple(new_r), acc + jnp.where(lo, outs[0], outs[1])

        below = i * nd

        def step(st):
            jj, _, cr = st
            cr = block(below - 1 - jj, cr, None)
            return jj + 1, _attn_alive(cr[0]), cr

        def walk(inline_first_below):
            zero = jnp.zeros((tq, 1), F32)
            carry = ((zero, zero), jnp.zeros((tq, LANES), F32))
            for dg in reversed(range(nd)):
                carry = block(i * nd + dg, carry, dg)
            if inline_first_below:
                carry = block(below - 1, carry, None)
            _, _, carry = lax.while_loop(lambda st: (st[0] < below) & st[1], step,
                                         (jnp.int32(int(inline_first_below)), _attn_alive(carry[0]), carry))
            o_ref[...] = carry[1]

        _attn_walk_variants(i, walk)

    outs, gathered = _carrying_call(
        body,
        gather_cargo,
        gather=True,
        name=name,
        grid=(nsec, s // tq),
        in_specs=[qspec, kspec, vspec],
        out_specs=[qspec],
        out_shape=[jax.ShapeDtypeStruct((s, width // 3), F32)],
        scratch_shapes=[],
        operands=(qkv16, qkv16, qkv16),
        compiler_params=_params(("arbitrary", "arbitrary")),
    )
    return outs[0], gathered


def _attn_bwd(qkv16, o, do, dh, *, cargo=(), name):
    s, width = qkv16.shape
    tq = _tile(s, (ATTN_Q_BLOCK, ATTN_K_BLOCK))
    tk = _tile(tq, (ATTN_K_BLOCK,))
    nd = tq // tk
    nsec, qspec, kspec, vspec = _attn_specs(s, width, dh, tq)
    accspec = pl.BlockSpec((s, LANES), lambda h, i: (0, h))

    def body(q_ref, k_ref, v_ref, o_ref, do_ref, dq_ref, dk_ref, dv_ref):
        i = pl.program_id(1)

        @pl.when(i == 0)
        def _():
            dk_ref[...] = jnp.zeros_like(dk_ref)
            dv_ref[...] = jnp.zeros_like(dv_ref)

        lo, r_io, c_io, qrow, kcol = _attn_masks(tq, tk, dh)
        q2 = q_ref[...]
        zq = jnp.zeros_like(q2)
        qs = (jnp.where(lo, q2, zq), jnp.where(lo, zq, q2))
        do16 = do_ref[...].astype(BF16)
        dos = (jnp.where(lo, do16, zq), jnp.where(lo, zq, do16))
        prod = do16.astype(F32) * o_ref[...]
        deltas = (jnp.sum(jnp.where(lo, prod, 0.0), axis=-1, keepdims=True),
                  jnp.sum(jnp.where(lo, 0.0, prod), axis=-1, keepdims=True))
        tri_gt = (r_io > c_io).astype(BF16)
        tri_ge = (r_io >= c_io).astype(BF16)

        def block(j, carry, diag):
            rr, er, dq = carry
            start = pl.multiple_of(j * tk, tk)
            kb = k_ref[pl.ds(start, tk), :]
            vb = v_ref[pl.ds(start, tk), :]
            valid = None if diag is None else kcol + diag * tk < qrow
            new_r, new_e, dqs = [], [], []
            dk_blk = dv_blk = None
            for hh in range(2):
                zl, lk, w = _attn_scores(qs[hh], kb, tri_gt, valid, rr[hh])
                beta = jnp.exp(zl)
                w16 = w.astype(BF16)
                e = _dot(dos[hh], vb, 1, 1) * w16.astype(F32)
                e_before = deltas[hh] - er[hh] - _split_dot(e, tri_ge)
                dz = e - beta * (e + e_before)
                if valid is not None:
                    dz = jnp.where(valid, dz, 0.0)
                dz16 = dz.astype(BF16)
                dqs.append(_dot(dz16, kb, 1, 0))
                dkc = _dot(dz16, qs[hh], 0, 0)
                dvc = _dot(w16, dos[hh], 0, 0)
                dk_blk = dkc if dk_blk is None else dk_blk + dkc
                dv_blk = dvc if dv_blk is None else dv_blk + dvc
                new_r.append(rr[hh] + jnp.sum(lk, axis=-1, keepdims=True))
                new_e.append(er[hh] + jnp.sum(e, axis=-1, keepdims=True))
            dk_ref[pl.ds(start, tk), :] += dk_blk
            dv_ref[pl.ds(start, tk), :] += dv_blk
            return tuple(new_r), tuple(new_e), dq + jnp.where(lo, dqs[0], dqs[1])

        below = i * nd

        def step(st):
            jj, _, cr = st
            cr = block(below - 1 - jj, cr, None)
            return jj + 1, _attn_alive(cr[0]), cr

        def walk(inline_first_below):
            zero = jnp.zeros((tq, 1), F32)
            carry = ((zero, zero), (zero, zero), jnp.zeros((tq, LANES), F32))
            for dg in reversed(range(nd)):
                carry = block(i * nd + dg, carry, dg)
            if inline_first_below:
                carry = block(below - 1, carry, None)
            _, _, carry = lax.while_loop(lambda st: (st[0] < below) & st[1], step,
                                         (jnp.int32(int(inline_first_below)), _attn_alive(carry[0]), carry))
            dq_ref[...] = carry[2]

        _attn_walk_variants(i, walk)

    shp = jax.ShapeDtypeStruct((s, width // 3), F32)
    return _carrying_call(
        body,
        cargo,
        name=name,
        grid=(nsec, s // tq),
        in_specs=[qspec, kspec, vspec, qspec, qspec],
        out_specs=(qspec, accspec, accspec),
        out_shape=(shp, shp, shp),
        scratch_shapes=[],
        operands=(qkv16, qkv16, qkv16, o, do),
        compiler_params=_params(("arbitrary", "arbitrary")),
    )


def _adam_update(wv, gv, mv, vv):
    c1 = 1.0 - ADAM_B1 ** ADAM_STEP
    c2 = 1.0 - ADAM_B2 ** ADAM_STEP
    nm = ADAM_B1 * mv + (1.0 - ADAM_B1) * gv
    nv = ADAM_B2 * vv + (1.0 - ADAM_B2) * (gv * gv)
    delta = -ADAM_LR * ((nm / c1) / (jnp.sqrt(nv / c2) + ADAM_EPS) + ADAM_WD * wv)
    return delta, nm, nv


def _sum_slabs(r, *, name):
    n, rows, lanes = r.shape
    tr = _tile(rows, (1024, 512, 256, 128, 64, 32, 16))

    def body(r_ref, o_ref):
        acc = r_ref[0]
        for d in range(1, n):
            acc = acc + r_ref[d]
        o_ref[...] = acc

    return pl.pallas_call(
        body,
        name=name,
        grid=(rows // tr,),
        in_specs=[pl.BlockSpec((n, tr, lanes), lambda i: (0, i, 0))],
        out_specs=pl.BlockSpec((tr, lanes), lambda i: (i, 0)),
        out_shape=jax.ShapeDtypeStruct((rows, lanes), F32),
        compiler_params=_params(("parallel",)),
    )(r)


def _adamw(w, g, m, v, *, name):
    rows, cols = w.shape
    tr = _tile(rows, (1024, 512, 256, 128, 64, 32, 16))
    spec = pl.BlockSpec((tr, cols), lambda i: (i, 0))

    def body(w_ref, g_ref, m_ref, v_ref, d_ref, nm_ref, nv_ref):
        d_ref[...], nm_ref[...], nv_ref[...] = _adam_update(w_ref[...], g_ref[...], m_ref[...], v_ref[...])

    shp = jax.ShapeDtypeStruct((rows, cols), F32)
    return pl.pallas_call(
        body,
        name=name,
        grid=(rows // tr,),
        in_specs=[spec, spec, spec, spec],
        out_specs=(spec, spec, spec),
        out_shape=(shp, shp, shp),
        compiler_params=_params(("parallel",)),
    )(w, g, m, v)


def _adamw_sum(w, received, m, v, *, name):
    layers, rows, cols = w.shape
    n = received[0].shape[0]
    tr = _tile(rows, (256, 128, 64, 32, 16))
    spec = pl.BlockSpec((1, tr, cols), lambda li, i: (li, i, 0))

    def partial_spec(own):
        return pl.BlockSpec((n, tr, cols), lambda li, i: (0, jnp.where(li == own, i, 0), 0))

    def body(w_ref, *refs):
        r_refs, (m_ref, v_ref, g_ref, d_ref, nm_ref, nv_ref) = refs[:layers], refs[layers:]
        for own in range(layers):
            @pl.when(pl.program_id(0) == own)
            def _(r_ref=r_refs[own]):
                gv = r_ref[0].astype(F32)
                for d in range(1, n):
                    gv = gv + r_ref[d].astype(F32)
                g_ref[0] = gv
                d_ref[0], nm_ref[0], nv_ref[0] = _adam_update(w_ref[0], gv, m_ref[0], v_ref[0])

    shp = jax.ShapeDtypeStruct((layers, rows, cols), F32)
    return pl.pallas_call(
        body,
        name=name,
        grid=(layers, rows // tr),
        in_specs=[spec] + [partial_spec(own) for own in range(layers)] + [spec, spec],
        out_specs=(spec, spec, spec, spec),
        out_shape=(shp, shp, shp, shp),
        compiler_params=_params(("arbitrary", "arbitrary")),
    )(w, *received, m, v)


_MATMUL_WEIGHTS = (("conv_w_in", 2), ("conv_w_out", 1), ("attn_w_qkv", 2), ("attn_w_o", 1), ("ffn_w_up", 2),
                   ("ffn_w_down", 1))
_TAP_WEIGHTS = (("conv_a_dw_w", 2), ("conv_b_dw_w", 2), ("ffn_dw_w", 2))
_REPLICATED = ("mix_norm_g", "ffn_norm_g", "conv_a_dw_b", "conv_a_ln_g", "conv_a_ln_b", "attn_q_g", "attn_k_g",
               "ffn_dw_b")
_WEIGHT_ORDER = ("mix_norm_g", "ffn_norm_g", "conv_w_in", "conv_a_dw_w", "conv_a_dw_b", "conv_a_ln_g",
                 "conv_a_ln_b", "conv_b_dw_w", "conv_w_out", "attn_w_qkv", "attn_q_g", "attn_k_g", "attn_w_o",
                 "ffn_w_up", "ffn_dw_w", "ffn_dw_b", "ffn_w_down")


def _assemble(gathered, axis):
    n, l, a, b = gathered.shape
    if axis == 2:
        return jnp.transpose(gathered, (1, 2, 0, 3)).reshape(l, a, n * b)
    return jnp.transpose(gathered, (1, 0, 2, 3)).reshape(l, n * a, b)


def _split_shards(full, axis):
    l, a, b = full.shape
    if axis == 2:
        return jnp.transpose(full.reshape(l, a, N_DEV, b // N_DEV), (2, 0, 1, 3))
    return jnp.transpose(full.reshape(l, N_DEV, a // N_DEV, b), (1, 0, 2, 3))


def kernel(x, mix_norm_g, ffn_norm_g, conv_w_in, conv_a_dw_w, conv_a_dw_b, conv_a_ln_g, conv_a_ln_b, conv_b_dw_w, conv_w_out, attn_w_qkv, attn_q_g, attn_k_g, attn_w_o, ffn_w_up, ffn_dw_w, ffn_dw_b, ffn_w_down, loss_target, m_mix_norm_g, m_ffn_norm_g, m_conv_w_in, m_conv_a_dw_w, m_conv_a_dw_b, m_conv_a_ln_g, m_conv_a_ln_b, m_conv_b_dw_w, m_conv_w_out, m_attn_w_qkv, m_attn_q_g, m_attn_k_g, m_attn_w_o, m_ffn_w_up, m_ffn_dw_w, m_ffn_dw_b, m_ffn_w_down, v_mix_norm_g, v_ffn_norm_g, v_conv_w_in, v_conv_a_dw_w, v_conv_a_dw_b, v_conv_a_ln_g, v_conv_a_ln_b, v_conv_b_dw_w, v_conv_w_out, v_attn_w_qkv, v_attn_q_g, v_attn_k_g, v_attn_w_o, v_ffn_w_up, v_ffn_dw_w, v_ffn_dw_b, v_ffn_w_down):
    args = dict(locals())
    weights = {n: args[n] for n in _WEIGHT_ORDER}
    mom_m = {n: args["m_" + n] for n in _WEIGHT_ORDER}
    mom_v = {n: args["v_" + n] for n in _WEIGHT_ORDER}

    _, s, d = x.shape
    depth = mix_norm_g.shape[0]
    dh = attn_q_g.shape[1]
    x0 = x.reshape(s, d)
    target = loss_target.reshape(s, d)

    shard_axis = dict(_MATMUL_WEIGHTS)
    shard16 = {n: weights[n].astype(BF16) for n, _ in _MATMUL_WEIGHTS}
    full = {}

    def layer_keys(layer):
        if layer >= depth:
            return []
        mixer = ("conv_w_in", "conv_w_out") if layer % 2 == 0 else ("attn_w_qkv", "attn_w_o")
        return [(mixer[0], layer // 2), (mixer[1], layer // 2), ("ffn_w_up", layer), ("ffn_w_down", layer)]

    def shards(keys):
        return [shard16[n][idx] for n, idx in keys]

    def assemble(keys, gathered):
        for (n, idx), g in zip(keys, gathered):
            if shard_axis[n] == 2:
                full[(n, idx)] = _assemble_cols(g, name=f"assemble_{n}_{idx}")
            else:
                full[(n, idx)] = g.reshape(N_DEV * g.shape[1], g.shape[2])

    assemble(layer_keys(0), _all_gather(shards(layer_keys(0)), "gather_layer0_weights"))
    tap_shapes = [weights[n].shape for n, _ in _TAP_WEIGHTS]
    g_tap = _all_gather([_pack([weights[n] for n, _ in _TAP_WEIGHTS], F32)], "gather_tap_weights")[0]
    for (n, axis), part in zip(_TAP_WEIGHTS, _unpack(g_tap, tap_shapes, lead=(N_DEV,))):
        full[n] = _assemble(part, axis)
    f = ffn_w_down.shape[1] * N_DEV

    saved = []
    xs = x0
    for layer in range(depth):
        i = layer // 2
        nxt = layer_keys(layer + 1)
        sv = {"x_in": xs}
        hn = _rmsnorm_fwd(xs, mix_norm_g[layer:layer + 1], name=f"mix_norm_fwd_{layer}")
        sv["h"] = hn
        if layer % 2 == 0:
            p = _matmul(hn, full[("conv_w_in", i)], name=f"conv_in_{layer}")
            (ab, a1), got = _conv_act_fwd(p, full["conv_a_dw_w"][i], conv_a_dw_b[i:i + 1], conv_a_ln_g[i:i + 1],
                                          conv_a_ln_b[i:i + 1], full["conv_b_dw_w"][i], gather_cargo=shards(nxt[0:1]),
                                          name=f"conv_act_fwd_{layer}")
            assemble(nxt[0:1], got)
            sv.update(p=p, a1=a1, mix=ab)
            xs = _matmul(ab, full[("conv_w_out", i)], add=xs, name=f"conv_out_{layer}")
            riders = [nxt[1:2], nxt[2:3], nxt[3:4]]
        else:
            reps = LANES // dh
            gains = jnp.concatenate([
                jnp.broadcast_to(jnp.tile(attn_q_g[i], reps) * dh ** -0.5, (8, LANES)),
                jnp.broadcast_to(jnp.tile(attn_k_g[i], reps), (8, LANES)),
                jnp.ones((8, LANES), F32)], axis=0)
            qkv = _matmul(hn, full[("attn_w_qkv", i)], name=f"attn_qkv_{layer}")
            qkv16 = _qkv_prep_fwd(qkv, gains, dh, name=f"qkv_prep_fwd_{layer}")
            o, got = _attn_fwd(qkv16, dh, gather_cargo=shards(nxt), name=f"attn_fwd_{layer}")
            assemble(nxt, got)
            sv.update(qkv=qkv, qkv16=qkv16, gains=gains, mix=o)
            xs = _matmul(o, full[("attn_w_o", i)], add=xs, name=f"attn_out_{layer}")
            riders = [[], [], []]
        sv["x_mid"] = xs
        h2 = _rmsnorm_fwd(xs, ffn_norm_g[layer:layer + 1], name=f"ffn_norm_fwd_{layer}")
        w_up = full[("ffn_w_up", layer)]
        ug, got = _matmul(h2, w_up[:, :f], out_dtype=BF16, gather_cargo=shards(riders[0]),
                          name=f"ffn_up_gate_{layer}")
        assemble(riders[0], got)
        uv = _matmul(h2, w_up[:, f:], out_dtype=BF16, name=f"ffn_up_val_{layer}")
        taps = full["ffn_dw_w"][layer]
        bias = ffn_dw_b[layer:layer + 1]
        act, got = _ffn_act_fwd(ug, uv, taps[:, :f], taps[:, f:], bias[:, :f], bias[:, f:],
                                gather_cargo=shards(riders[1]), name=f"ffn_act_fwd_{layer}")
        assemble(riders[1], got)
        sv.update(h2=h2, ug=ug, uv=uv, act=act)
        xs, got = _matmul(act, full[("ffn_w_down", layer)], add=xs, gather_cargo=shards(riders[2]),
                          name=f"ffn_down_{layer}")
        assemble(riders[2], got)
        saved.append(sv)

    sq, dx, dx16 = _loss_head(xs, target, name="loss_head")
    loss = lax.psum(0.5 * sq[0, 0] / d, ("x", "y", "c"))

    grads = {n: [None] * weights[n].shape[0] for n in _WEIGHT_ORDER}
    shard_axis = dict(_MATMUL_WEIGHTS)
    queued, arrived = [], {}

    def queue(n, idx, g):
        a, b = weights[n].shape[1:]
        slabs = _split_cols(g, name=f"split_{n}_{idx}") if shard_axis[n] == 2 else g.reshape(N_DEV, a, b)
        queued.append(((n, idx), slabs))

    def carried(call):
        keys = [k for k, _ in queued]
        slabs = [sl for _, sl in queued]
        queued.clear()
        outs, received = call(slabs)
        arrived.update(zip(keys, received))
        return outs

    for layer in reversed(range(depth)):
        i = layer // 2
        sv = saved[layer]
        w_up = full[("ffn_w_up", layer)]
        taps = full["ffn_dw_w"][layer]
        bias = ffn_dw_b[layer:layer + 1]
        dact = _matmul(dx16, full[("ffn_w_down", layer)], tb=True, out_dtype=BF16, name=f"ffn_down_dx_{layer}")
        queue("ffn_w_down", layer, _weight_grad(sv["act"], dx16, name=f"ffn_down_dw_{layer}"))
        dug, duv, sumg, sumv = carried(lambda cargo: _ffn_act_bwd(
            sv["ug"], sv["uv"], dact, taps[:, :f], taps[:, f:], bias[:, :f], bias[:, f:], cargo=cargo,
            name=f"ffn_act_bwd_{layer}"))
        sums = jnp.concatenate([sumg, sumv], axis=1).reshape(4, 8, 2 * f).sum(axis=1)
        grads["ffn_dw_w"][layer] = sums[:3]
        grads["ffn_dw_b"][layer] = sums[3]
        dh2 = _matmul(dug, w_up[:, :f], tb=True, name=f"ffn_up_gate_dx_{layer}")
        dh2 = _matmul(duv, w_up[:, f:], tb=True, add=dh2, name=f"ffn_up_val_dx_{layer}")
        queue("ffn_w_up", layer, jnp.concatenate(
            [_weight_grad(sv["h2"], dug, name=f"ffn_up_gate_dw_{layer}"),
             _weight_grad(sv["h2"], duv, name=f"ffn_up_val_dw_{layer}")], axis=1))
        dx, dx16, dg = _rmsnorm_bwd(sv["x_mid"], ffn_norm_g[layer:layer + 1], dh2, dx, name=f"ffn_norm_bwd_{layer}")
        grads["ffn_norm_g"][layer] = dg[0]

        if layer % 2 == 0:
            dab = _matmul(dx16, full[("conv_w_out", i)], tb=True, name=f"conv_out_dx_{layer}")
            queue("conv_w_out", i, _weight_grad(sv["mix"], dx16, name=f"conv_out_dw_{layer}"))
            ka = full["conv_a_dw_w"].shape[1]
            kb = full["conv_b_dw_w"].shape[1]
            dp, dwa, dba, dlng, dlnb, dwb = carried(lambda cargo: _conv_act_bwd(
                sv["p"], sv["a1"], dab, full["conv_a_dw_w"][i], conv_a_ln_g[i:i + 1], conv_a_ln_b[i:i + 1],
                full["conv_b_dw_w"][i], cargo=cargo, name=f"conv_act_bwd_{layer}"))
            grads["conv_a_dw_w"][i] = dwa[:ka]
            grads["conv_a_dw_b"][i] = dba[0]
            grads["conv_a_ln_g"][i] = dlng[0]
            grads["conv_a_ln_b"][i] = dlnb[0]
            grads["conv_b_dw_w"][i] = dwb[:kb]
            dhn = _matmul(dp, full[("conv_w_in", i)], tb=True, name=f"conv_in_dx_{layer}")
            queue("conv_w_in", i, _weight_grad(sv["h"], dp, name=f"conv_in_dw_{layer}"))
        else:
            do = _matmul(dx16, full[("attn_w_o", i)], tb=True, name=f"attn_out_dx_{layer}")
            queue("attn_w_o", i, _weight_grad(sv["mix"], dx16, name=f"attn_out_dw_{layer}"))
            dq, dk, dv = carried(lambda cargo: _attn_bwd(sv["qkv16"], sv["mix"], do, dh, cargo=cargo,
                                                         name=f"attn_bwd_{layer}"))
            dqkv, dgains = _qkv_prep_bwd(sv["qkv"], dq, dk, dv, sv["gains"], dh, name=f"qkv_prep_bwd_{layer}")
            grads["attn_q_g"][i] = dgains[0].reshape(-1, dh).sum(axis=0) * dh ** -0.5
            grads["attn_k_g"][i] = dgains[8].reshape(-1, dh).sum(axis=0)
            dhn = _matmul(dqkv, full[("attn_w_qkv", i)], tb=True, name=f"attn_qkv_dx_{layer}")
            queue("attn_w_qkv", i, _weight_grad(sv["h"], dqkv, name=f"attn_qkv_dw_{layer}"))
        dx, dx16, dg = _rmsnorm_bwd(sv["x_in"], mix_norm_g[layer:layer + 1], dhn, dx, name=f"mix_norm_bwd_{layer}")
        grads["mix_norm_g"][layer] = dg[0]
    carried(lambda cargo: ((), _exchange(cargo, "exchange_last_grads")))

    me = _my_index()
    final_grads, delta, new_m, new_v = {}, {}, {}, {}
    for n, _ in _MATMUL_WEIGHTS:
        final_grads[n], delta[n], new_m[n], new_v[n] = _adamw_sum(
            weights[n], [arrived[(n, idx)] for idx in range(weights[n].shape[0])], mom_m[n], mom_v[n],
            name=f"adamw_{n}")

    small_names = list(_REPLICATED) + [n for n, _ in _TAP_WEIGHTS]
    local = {n: jnp.stack(grads[n], axis=0) for n in small_names}
    small_shapes = [local[n].shape for n in small_names]
    g_small = _sum_slabs(_all_gather([_pack([local[n] for n in small_names], F32)], "gather_small_grads")[0],
                         name="sum_small_grads")
    reduced = dict(zip(small_names, _unpack(g_small, small_shapes)))
    for n in _REPLICATED:
        final_grads[n] = reduced[n]
    for n, axis in _TAP_WEIGHTS:
        final_grads[n] = lax.dynamic_index_in_dim(_split_shards(reduced[n], axis), me, axis=0, keepdims=False)
    shapes = [weights[n].shape for n in small_names]
    d_s, m_s, v_s = _adamw(
        _pack([weights[n] for n in small_names], F32), _pack([final_grads[n] for n in small_names], F32),
        _pack([mom_m[n] for n in small_names], F32), _pack([mom_v[n] for n in small_names], F32), name="adamw_small")
    delta.update(zip(small_names, _unpack(d_s, shapes)))
    new_m.update(zip(small_names, _unpack(m_s, shapes)))
    new_v.update(zip(small_names, _unpack(v_s, shapes)))

    order = list(_WEIGHT_ORDER)
    return (loss, dx.reshape(x.shape), *[final_grads[n] for n in order], *[delta[n] for n in order],
            *[new_m[n] for n in order], *[new_v[n] for n in order])
```

```python
import jax
import jax.numpy as jnp
from jax import lax
from jax.experimental import pallas as pl
from jax.experimental.pallas import tpu as pltpu

F32 = jnp.float32
BF16 = jnp.bfloat16
EPS = 1e-6
N_DEV = 8
MESH_ID = pl.DeviceIdType.MESH

ADAM_LR = 0.001
ADAM_B1 = 0.9
ADAM_B2 = 0.999
ADAM_EPS = 1e-08
ADAM_WD = 0.01
ADAM_STEP = 10

V7X_VMEM_LIMIT_BYTES = 48 * 1024 * 1024
PACK_QUANTUM = 2048
A_HALO = 32
S_HALO = 8
FFN_HALO_BLOCK = 16
LANES = 128
ATTN_Q_BLOCK = 256
ATTN_K_BLOCK = 256
MATMUL_MAX_SINGLE_K = 3072
MXU_WIDTH = 256
FFN_CHUNK_ROWS = 64
ATTN_DEAD_LOG_WEIGHT = -110.0


def _tile(n, prefs):
    for p in prefs:
        if n % p == 0:
            return p
    return n


def _params(sem=None):
    return pltpu.CompilerParams(dimension_semantics=sem, vmem_limit_bytes=V7X_VMEM_LIMIT_BYTES)


def _sigmoid(x):
    return 1.0 / (1.0 + jnp.exp(-x))


def _dot(a, b, ca, cb):
    return lax.dot_general(a, b, (((ca,), (cb,)), ((), ())), preferred_element_type=F32)


def _my_index():
    return 4 * lax.axis_index("x") + 2 * lax.axis_index("y") + lax.axis_index("c")


def _all_gather(shards, name):
    n = len(shards)

    def body(*refs):
        x_refs, out_refs = refs[:n], refs[n:2 * n]
        send_sems, recv_sems, local_sems = refs[2 * n:]
        x, y, c = lax.axis_index("x"), lax.axis_index("y"), lax.axis_index("c")
        me, sibling = (x, y, c), (x, y, 1 - c)
        chips = [(1 - x, y), (x, 1 - y), (1 - x, 1 - y)]

        def copy(a, k, block, to, src=None):
            slot = out_refs[a].at[4 * block[0] + 2 * block[1] + block[2]]
            return pltpu.make_async_remote_copy(
                src_ref=slot if src is None else src, dst_ref=slot,
                send_sem=send_sems.at[7 * a + k], recv_sem=recv_sems.at[7 * a + k],
                device_id=to, device_id_type=MESH_ID)

        mine = [pltpu.make_async_copy(x_refs[a], out_refs[a].at[4 * x + 2 * y + c], local_sems.at[a])
                for a in range(n)]
        for cp in mine:
            cp.start()
        first = []
        for a in range(n):
            first.append(copy(a, 0, me, sibling, src=x_refs[a]))
            first += [copy(a, 1 + j, me, (*chip, c), src=x_refs[a]) for j, chip in enumerate(chips)]
        for cp in first:
            cp.start()
        passed = []
        for j, chip in enumerate(chips):
            for a in range(n):
                copy(a, 1 + j, (*chip, c), me).wait_recv()
                fwd = copy(a, 4 + j, (*chip, c), sibling)
                fwd.start()
                passed.append(fwd)
        for a in range(n):
            copy(a, 0, sibling, me).wait_recv()
            for j, chip in enumerate(chips):
                copy(a, 4 + j, (*chip, 1 - c), me).wait_recv()
        for cp in first + passed:
            cp.wait_send()
        for cp in mine:
            cp.wait()

    hbm = pl.BlockSpec(memory_space=pl.ANY)
    return pl.pallas_call(
        body,
        name=name,
        out_shape=[jax.ShapeDtypeStruct((N_DEV,) + sh.shape, sh.dtype) for sh in shards],
        in_specs=[hbm] * n,
        out_specs=[hbm] * n,
        scratch_shapes=[
            pltpu.SemaphoreType.DMA((7 * n,)),
            pltpu.SemaphoreType.DMA((7 * n,)),
            pltpu.SemaphoreType.DMA((n,)),
        ],
    )(*shards)


def _exchange_semaphores(n):
    return [pltpu.SemaphoreType.DMA((7 * n,)), pltpu.SemaphoreType.DMA((7 * n,)), pltpu.SemaphoreType.DMA((n,))]


def _exchange_copies(g_refs, out_refs, send_sems, recv_sems, local_sems, gather=False):
    n = len(g_refs)
    x, y, c = lax.axis_index("x"), lax.axis_index("y"), lax.axis_index("c")
    me = 4 * x + 2 * y + c

    def part(a, d):
        return g_refs[a] if gather else g_refs[a].at[d]

    local = [pltpu.make_async_copy(part(a, me), out_refs[a].at[me], local_sems.at[a]) for a in range(n)]
    remote = []
    for k in range(1, N_DEV):
        px = (x + ((k >> 2) & 1)) % 2
        py = (y + ((k >> 1) & 1)) % 2
        pc = (c + (k & 1)) % 2
        for a in range(n):
            remote.append(pltpu.make_async_remote_copy(
                src_ref=part(a, 4 * px + 2 * py + pc), dst_ref=out_refs[a].at[me],
                send_sem=send_sems.at[7 * a + k - 1], recv_sem=recv_sems.at[7 * a + k - 1],
                device_id=(px, py, pc), device_id_type=MESH_ID))
    return local, remote


def _exchange_start(copies):
    for cp in copies[0] + copies[1]:
        cp.start()


def _exchange_wait(copies):
    local, remote = copies
    for cp in remote:
        cp.wait_recv()
    for cp in remote:
        cp.wait_send()
    for cp in local:
        cp.wait()


def _carrying_call(body, cargo, *, name, grid, in_specs, out_specs, out_shape, scratch_shapes, operands,
                   compiler_params, gather=False):
    n = len(cargo)
    n_in, n_out, n_scr = len(in_specs), len(out_specs), len(scratch_shapes)
    if n == 0:
        return pl.pallas_call(body, name=name, grid=grid, in_specs=in_specs, out_specs=out_specs,
                              out_shape=out_shape, scratch_shapes=scratch_shapes,
                              compiler_params=compiler_params)(*operands), []

    def carrying(*refs):
        cuts = [0, n_in, n_in + n, n_in + n + n_out, n_in + 2 * n + n_out, n_in + 2 * n + n_out + n_scr, len(refs)]
        ins, g_refs, outs, r_refs, scr, sems = [refs[a:b] for a, b in zip(cuts[:-1], cuts[1:])]
        steps = [pl.program_id(ax) for ax in range(len(grid))]
        first = steps[0] == 0
        last = steps[0] == grid[0] - 1
        for ax in range(1, len(grid)):
            first = first & (steps[ax] == 0)
            last = last & (steps[ax] == grid[ax] - 1)

        @pl.when(first)
        def _():
            _exchange_start(_exchange_copies(g_refs, r_refs, *sems, gather=gather))

        body(*ins, *outs, *scr)

        @pl.when(last)
        def _():
            _exchange_wait(_exchange_copies(g_refs, r_refs, *sems, gather=gather))

    hbm = pl.BlockSpec(memory_space=pl.ANY)
    lead = (N_DEV,) if gather else ()
    res = pl.pallas_call(
        carrying,
        name=name,
        grid=grid,
        in_specs=list(in_specs) + [hbm] * n,
        out_specs=list(out_specs) + [hbm] * n,
        out_shape=list(out_shape) + [jax.ShapeDtypeStruct(lead + g.shape, g.dtype) for g in cargo],
        scratch_shapes=list(scratch_shapes) + _exchange_semaphores(n),
        compiler_params=compiler_params,
    )(*operands, *cargo)
    return res[:n_out], res[n_out:]


def _assemble_cols(g, *, name):
    n, r, w = g.shape
    tr = _tile(r, (256, 128, 64, 32, 16))

    def body(g_ref, o_ref):
        for j in range(n):
            o_ref[:, j * w:(j + 1) * w] = g_ref[j]

    return pl.pallas_call(
        body,
        name=name,
        grid=(r // tr,),
        in_specs=[pl.BlockSpec((n, tr, w), lambda i: (0, i, 0))],
        out_specs=pl.BlockSpec((tr, n * w), lambda i: (i, 0)),
        out_shape=jax.ShapeDtypeStruct((r, n * w), g.dtype),
        compiler_params=_params(("parallel",)),
    )(g)


def _split_cols(x, *, name):
    r, nw = x.shape
    w = nw // N_DEV
    tr = _tile(r, (256, 128, 64, 32, 16))

    def body(x_ref, o_ref):
        for j in range(N_DEV):
            o_ref[j] = x_ref[:, j * w:(j + 1) * w]

    return pl.pallas_call(
        body,
        name=name,
        grid=(r // tr,),
        in_specs=[pl.BlockSpec((tr, nw), lambda i: (i, 0))],
        out_specs=pl.BlockSpec((N_DEV, tr, w), lambda i: (0, i, 0)),
        out_shape=jax.ShapeDtypeStruct((N_DEV, r, w), x.dtype),
        compiler_params=_params(("parallel",)),
    )(x)


def _padded(n):
    return -(-n // PACK_QUANTUM) * PACK_QUANTUM


def _pack(parts, dtype, lead=()):
    flat = []
    for p in parts:
        p = p.astype(dtype).reshape(lead + (-1,))
        n = p.shape[-1]
        flat.append(jnp.pad(p, [(0, 0)] * len(lead) + [(0, _padded(n) - n)]))
    return jnp.concatenate(flat, axis=-1).reshape(lead + (-1, 128))


def _unpack(buf, shapes, lead=()):
    flat = buf.reshape(lead + (-1,))
    out, off = [], 0
    for shp in shapes:
        n = 1
        for d in shp:
            n *= d
        out.append(flat[..., off:off + n].reshape(lead + tuple(shp)))
        off += _padded(n)
    return out


def _half_if_tiled(n):
    half = n // 2
    return (half,) if n % 2 == 0 and half % LANES == 0 and half <= 1536 else ()


def _matmul(a, b, *, ta=False, tb=False, out_dtype=F32, add=None, gather_cargo=None, name):
    if ta:
        kdim, m = a.shape
    else:
        m, kdim = a.shape
    n = b.shape[0] if tb else b.shape[1]
    bm = _tile(m, (1024,) + _half_if_tiled(m) + (512, 256, 128))
    bn = _tile(n, (512, 256, 128))
    if bn < 512 and n % MXU_WIDTH == 0 and n <= MATMUL_MAX_SINGLE_K:
        bn, bm = n, _tile(m, (512, 256, 128))
    bk = kdim if kdim <= MATMUL_MAX_SINGLE_K else _tile(kdim, (1024, 512, 256, 128))
    nk = kdim // bk

    a_spec = pl.BlockSpec((bk, bm), lambda i, j, k: (k, i)) if ta else pl.BlockSpec((bm, bk), lambda i, j, k: (i, k))
    b_spec = pl.BlockSpec((bn, bk), lambda i, j, k: (j, k)) if tb else pl.BlockSpec((bk, bn), lambda i, j, k: (k, j))
    o_spec = pl.BlockSpec((bm, bn), lambda i, j, k: (i, j))
    in_specs = [a_spec, b_spec] + ([o_spec] if add is not None else [])
    operands = [a, b] + ([add] if add is not None else [])

    def product(a_ref, b_ref):
        return _dot(a_ref[...].astype(BF16), b_ref[...].astype(BF16), 0 if ta else 1, 1 if tb else 0)

    def finish(r, refs):
        if add is not None:
            r = r + refs[2][...]
        return r

    def body_single(*refs):
        o_ref = refs[-1]
        o_ref[...] = finish(product(refs[0], refs[1]), refs).astype(o_ref.dtype)

    def body_multi(*refs):
        o_ref, acc_ref = refs[-2], refs[-1]
        k = pl.program_id(2)

        @pl.when(k == 0)
        def _():
            acc_ref[...] = jnp.zeros_like(acc_ref)

        acc_ref[...] += product(refs[0], refs[1])

        @pl.when(k == nk - 1)
        def _():
            o_ref[...] = finish(acc_ref[...], refs).astype(o_ref.dtype)

    outs, received = _carrying_call(
        body_single if nk == 1 else body_multi,
        () if gather_cargo is None else gather_cargo,
        gather=True,
        name=name,
        grid=(m // bm, n // bn, nk),
        in_specs=in_specs,
        out_specs=[o_spec],
        out_shape=[jax.ShapeDtypeStruct((m, n), out_dtype)],
        scratch_shapes=[] if nk == 1 else [pltpu.VMEM((bm, bn), F32)],
        operands=operands,
        compiler_params=_params(("parallel", "parallel", "arbitrary") if gather_cargo is None else ("arbitrary",) * 3),
    )
    return outs[0] if gather_cargo is None else (outs[0], received)


def _rows_matmul(a, b, *, tb, add, row_ins, vec_ins, epilogue, row_out_dtypes, n_sums, cargo, gather, name):
    m, kdim = a.shape
    n = b.shape[0] if tb else b.shape[1]
    assert kdim <= MATMUL_MAX_SINGLE_K
    bm = _tile(m, (512, 256, 128))
    row = pl.BlockSpec((bm, n), lambda i: (i, 0))
    vec = pl.BlockSpec((1, n), lambda i: (0, 0))
    n_add = int(add is not None)
    n_ins = len(row_ins) + len(vec_ins)
    n_rows = len(row_out_dtypes)

    def body(a_ref, b_ref, *refs):
        y = _dot(a_ref[...].astype(BF16), b_ref[...].astype(BF16), 1, 1 if tb else 0)
        if add is not None:
            y = y + refs[0][...]
        ins = [r[...] for r in refs[n_add:n_add + n_ins]]
        outs = refs[n_add + n_ins:]
        rows, sums = epilogue(y, *ins)
        for ref, val in zip(outs[:n_rows], rows):
            ref[...] = val.astype(ref.dtype)
        if n_sums:
            @pl.when(pl.program_id(0) == 0)
            def _():
                for ref in outs[n_rows:]:
                    ref[...] = jnp.zeros_like(ref)

            for ref, val in zip(outs[n_rows:], sums):
                ref[...] += val

    return _carrying_call(
        body,
        cargo,
        gather=gather,
        name=name,
        grid=(m // bm,),
        in_specs=[pl.BlockSpec((bm, kdim), lambda i: (i, 0)), pl.BlockSpec(b.shape, lambda i: (0, 0))]
        + [row] * (n_add + len(row_ins)) + [vec] * len(vec_ins),
        out_specs=[row] * n_rows + [vec] * n_sums,
        out_shape=[jax.ShapeDtypeStruct((m, n), dt) for dt in row_out_dtypes]
        + [jax.ShapeDtypeStruct((1, n), F32)] * n_sums,
        scratch_shapes=[],
        operands=[a, b] + ([add] if add is not None else []) + list(row_ins) + list(vec_ins),
        compiler_params=_params(("arbitrary",)),
    )


def _matmul_add_norm(a, b, add, g, *, gather_cargo=(), name):
    def epilogue(y, gv):
        h = y * lax.rsqrt(jnp.mean(y * y, axis=-1, keepdims=True) + EPS) * gv
        return (y, h), ()

    return _rows_matmul(a, b, tb=False, add=add, row_ins=(), vec_ins=(g,), epilogue=epilogue,
                        row_out_dtypes=(F32, BF16), n_sums=0, cargo=gather_cargo, gather=True, name=name)


def _matmul_norm_bwd(a, b, x, g, res, *, add=None, cargo=(), name):
    def epilogue(dyv, xv, resv, gv):
        rs = lax.rsqrt(jnp.mean(xv * xv, axis=-1, keepdims=True) + EPS)
        xn = xv * rs
        dyg = dyv * gv
        dx = rs * (dyg - xn * jnp.mean(dyg * xn, axis=-1, keepdims=True)) + resv
        return (dx, dx), (jnp.sum(dyv * xn, axis=0, keepdims=True),)

    return _rows_matmul(a, b, tb=True, add=add, row_ins=(x, res), vec_ins=(g,), epilogue=epilogue,
                        row_out_dtypes=(F32, BF16), n_sums=1, cargo=cargo, gather=False, name=name)


def _weight_grad(x, dy, *, name):
    return _matmul(x, dy, ta=True, out_dtype=BF16, name=name)


def _rmsnorm_fwd(x, g, *, name):
    r, d = x.shape
    tr = _tile(r, (1024, 512, 256, 128))

    def body(x_ref, g_ref, o_ref):
        xv = x_ref[...]
        y = xv * lax.rsqrt(jnp.mean(xv * xv, axis=-1, keepdims=True) + EPS)
        y = y * g_ref[...]
        o_ref[...] = y.astype(o_ref.dtype)

    return pl.pallas_call(
        body,
        name=name,
        grid=(r // tr,),
        in_specs=[pl.BlockSpec((tr, d), lambda i: (i, 0)), pl.BlockSpec((1, d), lambda i: (0, 0))],
        out_specs=pl.BlockSpec((tr, d), lambda i: (i, 0)),
        out_shape=jax.ShapeDtypeStruct((r, d), BF16),
        compiler_params=_params(("parallel",)),
    )(x, g)


def _loss_head(y, target, *, name):
    s, d = y.shape
    ts = _tile(s, (512, 256, 128))
    row = pl.BlockSpec((ts, d), lambda i: (i, 0))
    acc = pl.BlockSpec((8, 128), lambda i: (0, 0))

    def body(y_ref, t_ref, l_ref, dy_ref, dy16_ref):
        e = y_ref[...] - t_ref[...]
        dy = e * (1.0 / d)
        dy_ref[...] = dy
        dy16_ref[...] = dy.astype(dy16_ref.dtype)

        @pl.when(pl.program_id(0) == 0)
        def _():
            l_ref[...] = jnp.zeros_like(l_ref)

        sq = jnp.sum(jnp.sum(e * e, axis=-1, keepdims=True), axis=0, keepdims=True)
        l_ref[...] += jnp.broadcast_to(sq, l_ref.shape)

    return pl.pallas_call(
        body,
        name=name,
        grid=(s // ts,),
        in_specs=[row, row],
        out_specs=(acc, row, row),
        out_shape=(jax.ShapeDtypeStruct((8, 128), F32), jax.ShapeDtypeStruct((s, d), F32),
                   jax.ShapeDtypeStruct((s, d), BF16)),
        compiler_params=_params(("arbitrary",)),
    )(y, target)


def _ffn_tiles(s, f):
    return _tile(s, (512, 256, 128)), _tile(f, _half_if_tiled(f) + (512, 256, 128))


def _conv3(xs, w, b):
    return b + xs[0] * w[0:1] + xs[1] * w[1:2] + xs[2] * w[2:3]


def _shifted3(x, rows):
    return [x[S_HALO - 2 + k:S_HALO - 2 + k + rows] for k in range(3)]


def _ffn_act_fwd(ug, uv, wg, wv, bg, bv, *, gather_cargo=(), name):
    s, f = ug.shape
    ts, tc = _ffn_tiles(s, f)
    rc = _tile(ts, (FFN_CHUNK_ROWS, 32, 16, 8))
    hb = ts // FFN_HALO_BLOCK
    main = pl.BlockSpec((ts, tc), lambda j, i: (i, j))
    prev = pl.BlockSpec((FFN_HALO_BLOCK, tc), lambda j, i: (jnp.maximum(i * hb - 1, 0), j))
    taps = pl.BlockSpec((3, tc), lambda j, i: (0, j))
    bias = pl.BlockSpec((1, tc), lambda j, i: (0, j))
    lead = FFN_HALO_BLOCK - S_HALO

    def body(ug_ref, ugp_ref, uv_ref, uvp_ref, wg_ref, wv_ref, bg_ref, bv_ref, o_ref, gbuf, vbuf):
        first = pl.program_id(1) == 0
        for buf, p_ref, m_ref in ((gbuf, ugp_ref, ug_ref), (vbuf, uvp_ref, uv_ref)):
            buf[0:FFN_HALO_BLOCK, :] = jnp.where(first, 0.0, p_ref[...].astype(F32))
            buf[FFN_HALO_BLOCK:, :] = m_ref[...].astype(F32)

        def chunk(ci, carry):
            r0 = pl.multiple_of(ci * rc, rc)
            rows = pl.ds(pl.multiple_of(r0 + lead, S_HALO), rc + S_HALO)
            for lb in range(tc // LANES):
                ls = pl.ds(lb * LANES, LANES)
                gate = _conv3(_shifted3(gbuf[rows, ls], rc), wg_ref[:, ls], bg_ref[:, ls])
                val = _conv3(_shifted3(vbuf[rows, ls], rc), wv_ref[:, ls], bv_ref[:, ls])
                o_ref[pl.ds(r0, rc), ls] = (gate * _sigmoid(gate) * val).astype(o_ref.dtype)
            return carry

        lax.fori_loop(0, ts // rc, chunk, 0)

    outs, gathered = _carrying_call(
        body,
        gather_cargo,
        gather=True,
        name=name,
        grid=(f // tc, s // ts),
        in_specs=[main, prev, main, prev, taps, taps, bias, bias],
        out_specs=[main],
        out_shape=[jax.ShapeDtypeStruct((s, f), BF16)],
        scratch_shapes=[pltpu.VMEM((FFN_HALO_BLOCK + ts, tc), F32), pltpu.VMEM((FFN_HALO_BLOCK + ts, tc), F32)],
        operands=(ug, ug, uv, uv, wg, wv, bg, bv),
        compiler_params=_params(("arbitrary", "arbitrary")),
    )
    return outs[0], gathered


def _ffn_act_bwd(ug, uv, df, wg, wv, bg, bv, *, cargo=(), name):
    s, f = ug.shape
    ts, tc = _ffn_tiles(s, f)
    rc = _tile(ts, (FFN_CHUNK_ROWS, 32, 16, 8))
    hb = ts // FFN_HALO_BLOCK
    last_hb = s // FFN_HALO_BLOCK - 1
    n_row = s // ts
    lead = FFN_HALO_BLOCK - S_HALO
    main = pl.BlockSpec((ts, tc), lambda j, i: (i, j))
    prev = pl.BlockSpec((FFN_HALO_BLOCK, tc), lambda j, i: (jnp.maximum(i * hb - 1, 0), j))
    nxt = pl.BlockSpec((FFN_HALO_BLOCK, tc), lambda j, i: (jnp.minimum((i + 1) * hb, last_hb), j))
    taps = pl.BlockSpec((3, tc), lambda j, i: (0, j))
    bias = pl.BlockSpec((1, tc), lambda j, i: (0, j))
    sums = pl.BlockSpec((32, tc), lambda j, i: (0, j))
    ext = rc + S_HALO

    def body(ug_ref, ugp_ref, ugn_ref, uv_ref, uvp_ref, uvn_ref, df_ref, dfn_ref, wg_ref, wv_ref, bg_ref, bv_ref,
             dug_ref, duv_ref, sumg_ref, sumv_ref, gbuf, vbuf, dfbuf):
        i = pl.program_id(1)
        first = i == 0
        last = i == n_row - 1

        @pl.when(first)
        def _():
            sumg_ref[...] = jnp.zeros_like(sumg_ref)
            sumv_ref[...] = jnp.zeros_like(sumv_ref)

        for buf, p_ref, m_ref, n_ref in ((gbuf, ugp_ref, ug_ref, ugn_ref), (vbuf, uvp_ref, uv_ref, uvn_ref)):
            buf[0:FFN_HALO_BLOCK, :] = jnp.where(first, 0.0, p_ref[...].astype(F32))
            buf[FFN_HALO_BLOCK:FFN_HALO_BLOCK + ts, :] = m_ref[...].astype(F32)
            buf[FFN_HALO_BLOCK + ts:, :] = n_ref[...].astype(F32)
        dfbuf[0:ts, :] = df_ref[...].astype(F32)
        dfbuf[ts:, :] = jnp.where(last, 0.0, dfn_ref[...].astype(F32))

        def chunk(ci, carry):
            r0 = pl.multiple_of(ci * rc, rc)
            rows = pl.ds(pl.multiple_of(r0 + lead, S_HALO), ext + S_HALO)
            for lb in range(tc // LANES):
                ls = pl.ds(lb * LANES, LANES)
                xg = _shifted3(gbuf[rows, ls], ext)
                xv = _shifted3(vbuf[rows, ls], ext)
                wgv, wvv = wg_ref[:, ls], wv_ref[:, ls]
                gate = _conv3(xg, wgv, bg_ref[:, ls])
                val = _conv3(xv, wvv, bv_ref[:, ls])
                dfe = dfbuf[pl.ds(r0, ext), ls]
                sg = _sigmoid(gate)
                dgate = dfe * val * (sg * (1.0 + gate * (1.0 - sg)))
                dval = dfe * (gate * sg)
                for dz, xs, w, du_ref, sum_ref in ((dgate, xg, wgv, dug_ref, sumg_ref),
                                                   (dval, xv, wvv, duv_ref, sumv_ref)):
                    du = dz[2:2 + rc] * w[0:1] + dz[1:1 + rc] * w[1:2] + dz[0:rc] * w[2:3]
                    du_ref[pl.ds(r0, rc), ls] = du.astype(du_ref.dtype)
                    dm = dz[0:rc]
                    for k in range(3):
                        sum_ref[8 * k:8 * k + 8, ls] += (dm * xs[k][0:rc]).reshape(rc // 8, 8, LANES).sum(axis=0)
                    sum_ref[24:32, ls] += dm.reshape(rc // 8, 8, LANES).sum(axis=0)
            return carry

        lax.fori_loop(0, ts // rc, chunk, 0)

    return _carrying_call(
        body,
        cargo,
        name=name,
        grid=(f // tc, n_row),
        in_specs=[main, prev, nxt, main, prev, nxt, main, nxt, taps, taps, bias, bias],
        out_specs=(main, main, sums, sums),
        out_shape=(jax.ShapeDtypeStruct((s, f), BF16), jax.ShapeDtypeStruct((s, f), BF16),
                   jax.ShapeDtypeStruct((32, f), F32), jax.ShapeDtypeStruct((32, f), F32)),
        scratch_shapes=[pltpu.VMEM((2 * FFN_HALO_BLOCK + ts, tc), F32), pltpu.VMEM((2 * FFN_HALO_BLOCK + ts, tc), F32),
                        pltpu.VMEM((FFN_HALO_BLOCK + ts, tc), F32)],
        operands=(ug, ug, ug, uv, uv, uv, df, df, wg, wv, bg, bv),
        compiler_params=_params(("arbitrary", "arbitrary")),
    )


def _layernorm_stats(a1):
    mu = jnp.mean(a1, axis=-1, keepdims=True)
    xc = a1 - mu
    rstd = lax.rsqrt(jnp.mean(xc * xc, axis=-1, keepdims=True) + EPS)
    return xc * rstd, rstd


def _conv_act_fwd(p, wa, ba, lng, lnb, wb, *, gather_cargo=(), name):
    s, c5 = p.shape
    c = c5 // 5
    ka, kb = wa.shape[0], wb.shape[0]
    ts = _tile(s, (256, 128))
    hb = ts // A_HALO
    main = pl.BlockSpec((ts, c5), lambda i: (i, 0))
    prev = pl.BlockSpec((A_HALO, c5), lambda i: (jnp.maximum(i * hb - 1, 0), 0))

    def full(arr):
        return pl.BlockSpec(arr.shape, lambda i: (0, 0))

    def body(p_ref, pp_ref, wa_ref, ba_ref, lng_ref, lnb_ref, wb_ref, ab_ref, a1_ref, gbuf, cbuf):
        first = pl.program_id(0) == 0
        gbuf[0:A_HALO, :] = jnp.where(first, 0.0, pp_ref[:, 0:c] * _sigmoid(pp_ref[:, c:2 * c]))
        gbuf[A_HALO:, :] = p_ref[:, 0:c] * _sigmoid(p_ref[:, c:2 * c])
        cbuf[0:A_HALO, :] = jnp.where(first, 0.0, pp_ref[:, 3 * c:4 * c] * pp_ref[:, 4 * c:5 * c])
        cbuf[A_HALO:, :] = p_ref[:, 3 * c:4 * c] * p_ref[:, 4 * c:5 * c]

        a1 = jnp.broadcast_to(ba_ref[...], (ts, c))
        for k in range(ka):
            off = A_HALO - (ka - 1) + k
            a1 = a1 + gbuf[off:off + ts, :] * wa_ref[k:k + 1, :]
        a1_ref[...] = a1
        xhat, _ = _layernorm_stats(a1)
        a2 = xhat * lng_ref[...] + lnb_ref[...]
        ab_ref[:, 0:c] = (a2 * _sigmoid(a2)).astype(ab_ref.dtype)

        cv = jnp.zeros((ts, c), F32)
        for k in range(kb):
            off = A_HALO - (kb - 1) + k
            cv = cv + cbuf[off:off + ts, :] * wb_ref[k:k + 1, :]
        ab_ref[:, c:2 * c] = (p_ref[:, 2 * c:3 * c] * cv).astype(ab_ref.dtype)

    return _carrying_call(
        body,
        gather_cargo,
        gather=True,
        name=name,
        grid=(s // ts,),
        in_specs=[main, prev, full(wa), full(ba), full(lng), full(lnb), full(wb)],
        out_specs=(pl.BlockSpec((ts, 2 * c), lambda i: (i, 0)), pl.BlockSpec((ts, c), lambda i: (i, 0))),
        out_shape=(jax.ShapeDtypeStruct((s, 2 * c), BF16), jax.ShapeDtypeStruct((s, c), F32)),
        scratch_shapes=[pltpu.VMEM((A_HALO + ts, c), F32), pltpu.VMEM((A_HALO + ts, c), F32)],
        operands=(p, p, wa, ba, lng, lnb, wb),
        compiler_params=_params(("arbitrary",)),
    )


def _conv_act_bwd(p, a1, dab, wa, lng, lnb, wb, *, cargo=(), name):
    s, c5 = p.shape
    c = c5 // 5
    ka, kb = wa.shape[0], wb.shape[0]
    ts = _tile(s, (256, 128))
    hb = ts // A_HALO
    n_row = s // ts
    last_hb = s // A_HALO - 1
    ext = ts + A_HALO

    def rows(width):
        return (pl.BlockSpec((ts, width), lambda i: (i, 0)),
                pl.BlockSpec((A_HALO, width), lambda i: (jnp.maximum(i * hb - 1, 0), 0)),
                pl.BlockSpec((A_HALO, width), lambda i: (jnp.minimum((i + 1) * hb, last_hb), 0)))

    p_main, p_prev, p_next = rows(c5)
    d_main, _, d_next = rows(2 * c)
    a_main, _, a_next = rows(c)

    def full(shape):
        return pl.BlockSpec(shape, lambda i: (0, 0))

    def body(p_ref, pp_ref, pn_ref, a1_ref, a1n_ref, d_ref, dn_ref, wa_ref, lng_ref, lnb_ref, wb_ref,
             dp_ref, dwa_ref, dba_ref, dlng_ref, dlnb_ref, dwb_ref, gbuf, cbuf, da1buf, dcvbuf):
        i = pl.program_id(0)
        first = i == 0
        last = i == n_row - 1

        @pl.when(first)
        def _():
            dwa_ref[...] = jnp.zeros_like(dwa_ref)
            dba_ref[...] = jnp.zeros_like(dba_ref)
            dlng_ref[...] = jnp.zeros_like(dlng_ref)
            dlnb_ref[...] = jnp.zeros_like(dlnb_ref)
            dwb_ref[...] = jnp.zeros_like(dwb_ref)

        sig_gate = _sigmoid(p_ref[:, c:2 * c])
        gbuf[0:A_HALO, :] = jnp.where(first, 0.0, pp_ref[:, 0:c] * _sigmoid(pp_ref[:, c:2 * c]))
        gbuf[A_HALO:, :] = p_ref[:, 0:c] * sig_gate
        cbuf[0:A_HALO, :] = jnp.where(first, 0.0, pp_ref[:, 3 * c:4 * c] * pp_ref[:, 4 * c:5 * c])
        cbuf[A_HALO:, :] = p_ref[:, 3 * c:4 * c] * p_ref[:, 4 * c:5 * c]

        def ln_back(a1v, dav):
            xhat, rstd = _layernorm_stats(a1v)
            a2 = xhat * lng_ref[...] + lnb_ref[...]
            sg = _sigmoid(a2)
            da2 = dav * (sg * (1.0 + a2 * (1.0 - sg)))
            dxh = da2 * lng_ref[...]
            da1 = rstd * (dxh - jnp.mean(dxh, axis=-1, keepdims=True)
                          - xhat * jnp.mean(dxh * xhat, axis=-1, keepdims=True))
            return da1, da2, xhat

        da1_m, da2_m, xhat_m = ln_back(a1_ref[...], d_ref[:, 0:c])
        da1_n, _, _ = ln_back(a1n_ref[...], dn_ref[:, 0:c])
        da1buf[0:ts, :] = da1_m
        da1buf[ts:, :] = jnp.where(last, 0.0, da1_n)
        dlng_ref[...] += jnp.sum(da2_m * xhat_m, axis=0, keepdims=True)
        dlnb_ref[...] += jnp.sum(da2_m, axis=0, keepdims=True)
        dba_ref[...] += jnp.sum(da1_m, axis=0, keepdims=True)

        dglu = jnp.zeros((ts, c), F32)
        for k in range(ka):
            off = (ka - 1) - k
            dglu = dglu + da1buf[off:off + ts, :] * wa_ref[k:k + 1, :]
            goff = A_HALO - (ka - 1) + k
            dwa_ref[k:k + 1, :] += jnp.sum(da1_m * gbuf[goff:goff + ts, :], axis=0, keepdims=True)
        dp_ref[:, 0:c] = (dglu * sig_gate).astype(dp_ref.dtype)
        dp_ref[:, c:2 * c] = (dglu * p_ref[:, 0:c] * sig_gate * (1.0 - sig_gate)).astype(dp_ref.dtype)

        cv = jnp.zeros((ts, c), F32)
        for k in range(kb):
            off = A_HALO - (kb - 1) + k
            cv = cv + cbuf[off:off + ts, :] * wb_ref[k:k + 1, :]
        db_m = d_ref[:, c:2 * c]
        dp_ref[:, 2 * c:3 * c] = (db_m * cv).astype(dp_ref.dtype)
        dcv_m = db_m * p_ref[:, 2 * c:3 * c]
        dcvbuf[0:ts, :] = dcv_m
        dcvbuf[ts:, :] = jnp.where(last, 0.0, dn_ref[:, c:2 * c] * pn_ref[:, 2 * c:3 * c])
        dbc = jnp.zeros((ts, c), F32)
        for k in range(kb):
            off = (kb - 1) - k
            dbc = dbc + dcvbuf[off:off + ts, :] * wb_ref[k:k + 1, :]
            coff = A_HALO - (kb - 1) + k
            dwb_ref[k:k + 1, :] += jnp.sum(dcv_m * cbuf[coff:coff + ts, :], axis=0, keepdims=True)
        dp_ref[:, 3 * c:4 * c] = (dbc * p_ref[:, 4 * c:5 * c]).astype(dp_ref.dtype)
        dp_ref[:, 4 * c:5 * c] = (dbc * p_ref[:, 3 * c:4 * c]).astype(dp_ref.dtype)

    return _carrying_call(
        body,
        cargo,
        name=name,
        grid=(n_row,),
        in_specs=[p_main, p_prev, p_next, a_main, a_next, d_main, d_next,
                  full(wa.shape), full(lng.shape), full(lnb.shape), full(wb.shape)],
        out_specs=(pl.BlockSpec((ts, c5), lambda i: (i, 0)), full((32, c)), full((1, c)), full((1, c)),
                   full((1, c)), full((8, c))),
        out_shape=(
            jax.ShapeDtypeStruct((s, c5), BF16), jax.ShapeDtypeStruct((32, c), F32),
            jax.ShapeDtypeStruct((1, c), F32), jax.ShapeDtypeStruct((1, c), F32),
            jax.ShapeDtypeStruct((1, c), F32), jax.ShapeDtypeStruct((8, c), F32),
        ),
        scratch_shapes=[
            pltpu.VMEM((A_HALO + ts, c), F32), pltpu.VMEM((A_HALO + ts, c), F32),
            pltpu.VMEM((ext, c), F32), pltpu.VMEM((ext, c), F32),
        ],
        operands=(p, p, p, a1, a1, dab, dab, wa, lng, lnb, wb),
        compiler_params=_params(("arbitrary",)),
    )


def _head_pair_geometry(width, dh):
    assert 2 * dh == LANES, "the attention kernels pair two heads in one 128-lane block"
    return width // (3 * LANES)


def _group_sum(v, lo):
    s_lo = jnp.sum(jnp.where(lo, v, 0.0), axis=-1, keepdims=True)
    s_hi = jnp.sum(jnp.where(lo, 0.0, v), axis=-1, keepdims=True)
    return jnp.where(lo, s_lo, s_hi)


def _qkv_prep_fwd(qkv, gains, dh, *, name):
    s, width = qkv.shape
    nsec = _head_pair_geometry(width, dh)
    tr = _tile(s, (256, 128))
    blk = pl.BlockSpec((tr, nsec * LANES), lambda i, sec: (i, sec))

    def body(x_ref, g_ref, o_ref):
        sec = pl.program_id(1)

        @pl.when(sec == 2)
        def _():
            o_ref[...] = x_ref[...].astype(o_ref.dtype)

        @pl.when(sec != 2)
        def _():
            lo = lax.broadcasted_iota(jnp.int32, (1, LANES), 1) < dh
            for lb in range(nsec):
                ls = pl.ds(lb * LANES, LANES)
                xv = x_ref[:, ls]
                rs = lax.rsqrt(_group_sum(xv * xv, lo) * (1.0 / dh) + EPS)
                o_ref[:, ls] = (xv * rs * g_ref[0:1, :]).astype(o_ref.dtype)

    return pl.pallas_call(
        body,
        name=name,
        grid=(s // tr, 3),
        in_specs=[blk, pl.BlockSpec((8, LANES), lambda i, sec: (sec, 0))],
        out_specs=blk,
        out_shape=jax.ShapeDtypeStruct((s, width), BF16),
        compiler_params=_params(("parallel", "parallel")),
    )(qkv, gains)


def _qkv_prep_bwd(qkv, dq, dk, dv, gains, dh, *, name):
    s, width = qkv.shape
    nsec = _head_pair_geometry(width, dh)
    tr = _tile(s, (256, 128))
    blk = pl.BlockSpec((tr, nsec * LANES), lambda sec, i: (i, sec))
    gspec = pl.BlockSpec((8, LANES), lambda sec, i: (sec, 0))

    def grad_spec(own):
        return pl.BlockSpec((tr, nsec * LANES), lambda sec, i: (jnp.where(sec == own, i, 0), 0))

    def body(x_ref, dq_ref, dk_ref, dv_ref, g_ref, o_ref, dg_ref):
        sec = pl.program_id(0)

        @pl.when(pl.program_id(1) == 0)
        def _():
            dg_ref[...] = jnp.zeros_like(dg_ref)

        def normed(d_ref):
            lo = lax.broadcasted_iota(jnp.int32, (1, LANES), 1) < dh
            dg = jnp.zeros((1, LANES), F32)
            for lb in range(nsec):
                ls = pl.ds(lb * LANES, LANES)
                xv = x_ref[:, ls]
                d = d_ref[:, ls]
                rs = lax.rsqrt(_group_sum(xv * xv, lo) * (1.0 / dh) + EPS)
                xn = xv * rs
                dyg = d * g_ref[0:1, :]
                dx = rs * (dyg - xn * (_group_sum(dyg * xn, lo) * (1.0 / dh)))
                o_ref[:, ls] = dx.astype(o_ref.dtype)
                dg = dg + jnp.sum(d * xn, axis=0, keepdims=True)
            dg_ref[...] += jnp.broadcast_to(dg, dg_ref.shape)

        @pl.when(sec == 0)
        def _():
            normed(dq_ref)

        @pl.when(sec == 1)
        def _():
            normed(dk_ref)

        @pl.when(sec == 2)
        def _():
            o_ref[...] = dv_ref[...].astype(o_ref.dtype)

    return pl.pallas_call(
        body,
        name=name,
        grid=(3, s // tr),
        in_specs=[blk, grad_spec(0), grad_spec(1), grad_spec(2), gspec],
        out_specs=(blk, gspec),
        out_shape=(jax.ShapeDtypeStruct((s, width), BF16), jax.ShapeDtypeStruct((24, LANES), F32)),
        compiler_params=_params(("arbitrary", "arbitrary")),
    )(qkv, dq, dk, dv, gains)


def _split_dot(v, tri):
    hi = v.astype(BF16)
    lo = (v - hi.astype(F32)).astype(BF16)
    return _dot(hi, tri, 1, 0) + _dot(lo, tri, 1, 0)


def _attn_scores(qm, kb, tri_gt, valid, r_run):
    z = _dot(qm, kb, 1, 1)
    zl = jnp.minimum(z, 0.0) - jnp.log(1.0 + jnp.exp(-jnp.abs(z)))
    lk = zl - z
    if valid is not None:
        lk = jnp.where(valid, lk, 0.0)
    after = _split_dot(lk, tri_gt)
    w = jnp.exp(zl + after + r_run)
    if valid is not None:
        w = jnp.where(valid, w, 0.0)
    return zl, lk, w


def _attn_alive(rr):
    return jnp.max(jnp.maximum(rr[0], rr[1])) > ATTN_DEAD_LOG_WEIGHT


def _attn_walk_variants(i, walk):
    @pl.when(i == 0)
    def _():
        walk(False)

    @pl.when(i > 0)
    def _():
        walk(True)


def _attn_specs(s, width, dh, tq):
    nsec = _head_pair_geometry(width, dh)
    qspec = pl.BlockSpec((tq, LANES), lambda h, i: (i, h))
    kspec = pl.BlockSpec((s, LANES), lambda h, i: (0, nsec + h))
    vspec = pl.BlockSpec((s, LANES), lambda h, i: (0, 2 * nsec + h))
    return nsec, qspec, kspec, vspec


def _attn_masks(tq, tk, dh):
    lo = lax.broadcasted_iota(jnp.int32, (1, LANES), 1) < dh
    r_io = lax.broadcasted_iota(jnp.int32, (tk, tk), 0)
    c_io = lax.broadcasted_iota(jnp.int32, (tk, tk), 1)
    qrow = lax.broadcasted_iota(jnp.int32, (tq, tk), 0)
    kcol = lax.broadcasted_iota(jnp.int32, (tq, tk), 1)
    return lo, r_io, c_io, qrow, kcol


def _attn_fwd(qkv16, dh, *, gather_cargo=(), name):
    s, width = qkv16.shape
    tq = _tile(s, (ATTN_Q_BLOCK, ATTN_K_BLOCK))
    tk = _tile(tq, (ATTN_K_BLOCK,))
    nd = tq // tk
    nsec, qspec, kspec, vspec = _attn_specs(s, width, dh, tq)

    def body(q_ref, k_ref, v_ref, o_ref):
        i = pl.program_id(1)
        lo, r_io, c_io, qrow, kcol = _attn_masks(tq, tk, dh)
        q2 = q_ref[...]
        zq = jnp.zeros_like(q2)
        qs = (jnp.where(lo, q2, zq), jnp.where(lo, zq, q2))
        tri_gt = (r_io > c_io).astype(BF16)

        def block(j, carry, diag):
            rr, acc = carry
            start = pl.multiple_of(j * tk, tk)
            kb = k_ref[pl.ds(start, tk), :]
            vb = v_ref[pl.ds(start, tk), :]
            valid = None if diag is None else kcol + diag * tk < qrow
            outs, new_r = [], []
            for hh in range(2):
                _, lk, w = _attn_scores(qs[hh], kb, tri_gt, valid, rr[hh])
                outs.append(_dot(w.astype(BF16), vb, 1, 0))
                new_r.append(rr[hh] + jnp.sum(lk, axis=-1, keepdims=True))
            return tuple(new_r), acc + jnp.where(lo, outs[0], outs[1])

        below = i * nd

        def step(st):
            jj, _, cr = st
            cr = block(below - 1 - jj, cr, None)
            return jj + 1, _attn_alive(cr[0]), cr

        def walk(inline_first_below):
            zero = jnp.zeros((tq, 1), F32)
            carry = ((zero, zero), jnp.zeros((tq, LANES), F32))
            for dg in reversed(range(nd)):
                carry = block(i * nd + dg, carry, dg)
            if inline_first_below:
                carry = block(below - 1, carry, None)
            _, _, carry = lax.while_loop(lambda st: (st[0] < below) & st[1], step,
                                         (jnp.int32(int(inline_first_below)), _attn_alive(carry[0]), carry))
            o_ref[...] = carry[1]

        _attn_walk_variants(i, walk)

    outs, gathered = _carrying_call(
        body,
        gather_cargo,
        gather=True,
        name=name,
        grid=(nsec, s // tq),
        in_specs=[qspec, kspec, vspec],
        out_specs=[qspec],
        out_shape=[jax.ShapeDtypeStruct((s, width // 3), F32)],
        scratch_shapes=[],
        operands=(qkv16, qkv16, qkv16),
        compiler_params=_params(("arbitrary", "arbitrary")),
    )
    return outs[0], gathered


def _attn_bwd(qkv16, o, do, dh, *, cargo=(), name):
    s, width = qkv16.shape
    tq = _tile(s, (ATTN_Q_BLOCK, ATTN_K_BLOCK))
    tk = _tile(tq, (ATTN_K_BLOCK,))
    nd = tq // tk
    nsec, qspec, kspec, vspec = _attn_specs(s, width, dh, tq)
    accspec = pl.BlockSpec((s, LANES), lambda h, i: (0, h))

    def body(q_ref, k_ref, v_ref, o_ref, do_ref, dq_ref, dk_ref, dv_ref):
        i = pl.program_id(1)

        @pl.when(i == 0)
        def _():
            dk_ref[...] = jnp.zeros_like(dk_ref)
            dv_ref[...] = jnp.zeros_like(dv_ref)

        lo, r_io, c_io, qrow, kcol = _attn_masks(tq, tk, dh)
        q2 = q_ref[...]
        zq = jnp.zeros_like(q2)
        qs = (jnp.where(lo, q2, zq), jnp.where(lo, zq, q2))
        do16 = do_ref[...].astype(BF16)
        dos = (jnp.where(lo, do16, zq), jnp.where(lo, zq, do16))
        prod = do16.astype(F32) * o_ref[...]
        deltas = (jnp.sum(jnp.where(lo, prod, 0.0), axis=-1, keepdims=True),
                  jnp.sum(jnp.where(lo, 0.0, prod), axis=-1, keepdims=True))
        tri_gt = (r_io > c_io).astype(BF16)
        tri_ge = (r_io >= c_io).astype(BF16)

        def block(j, carry, diag):
            rr, er, dq = carry
            start = pl.multiple_of(j * tk, tk)
            kb = k_ref[pl.ds(start, tk), :]
            vb = v_ref[pl.ds(start, tk), :]
            valid = None if diag is None else kcol + diag * tk < qrow
            new_r, new_e, dqs = [], [], []
            dk_blk = dv_blk = None
            for hh in range(2):
                zl, lk, w = _attn_scores(qs[hh], kb, tri_gt, valid, rr[hh])
                beta = jnp.exp(zl)
                w16 = w.astype(BF16)
                e = _dot(dos[hh], vb, 1, 1) * w16.astype(F32)
                e_before = deltas[hh] - er[hh] - _split_dot(e, tri_ge)
                dz = e - beta * (e + e_before)
                if valid is not None:
                    dz = jnp.where(valid, dz, 0.0)
                dz16 = dz.astype(BF16)
                dqs.append(_dot(dz16, kb, 1, 0))
                dkc = _dot(dz16, qs[hh], 0, 0)
                dvc = _dot(w16, dos[hh], 0, 0)
                dk_blk = dkc if dk_blk is None else dk_blk + dkc
                dv_blk = dvc if dv_blk is None else dv_blk + dvc
                new_r.append(rr[hh] + jnp.sum(lk, axis=-1, keepdims=True))
                new_e.append(er[hh] + jnp.sum(e, axis=-1, keepdims=True))
            dk_ref[pl.ds(start, tk), :] += dk_blk
            dv_ref[pl.ds(start, tk), :] += dv_blk
            return tuple(new_r), tuple(new_e), dq + jnp.where(lo, dqs[0], dqs[1])

        below = i * nd

        def step(st):
            jj, _, cr = st
            cr = block(below - 1 - jj, cr, None)
            return jj + 1, _attn_alive(cr[0]), cr

        def walk(inline_first_below):
            zero = jnp.zeros((tq, 1), F32)
            carry = ((zero, zero), (zero, zero), jnp.zeros((tq, LANES), F32))
            for dg in reversed(range(nd)):
                carry = block(i * nd + dg, carry, dg)
            if inline_first_below:
                carry = block(below - 1, carry, None)
            _, _, carry = lax.while_loop(lambda st: (st[0] < below) & st[1], step,
                                         (jnp.int32(int(inline_first_below)), _attn_alive(carry[0]), carry))
            dq_ref[...] = carry[2]

        _attn_walk_variants(i, walk)

    shp = jax.ShapeDtypeStruct((s, width // 3), F32)
    return _carrying_call(
        body,
        cargo,
        name=name,
        grid=(nsec, s // tq),
        in_specs=[qspec, kspec, vspec, qspec, qspec],
        out_specs=(qspec, accspec, accspec),
        out_shape=(shp, shp, shp),
        scratch_shapes=[],
        operands=(qkv16, qkv16, qkv16, o, do),
        compiler_params=_params(("arbitrary", "arbitrary")),
    )


def _adam_update(wv, gv, mv, vv):
    c1 = 1.0 - ADAM_B1 ** ADAM_STEP
    c2 = 1.0 - ADAM_B2 ** ADAM_STEP
    nm = ADAM_B1 * mv + (1.0 - ADAM_B1) * gv
    nv = ADAM_B2 * vv + (1.0 - ADAM_B2) * (gv * gv)
    delta = -ADAM_LR * ((nm / c1) / (jnp.sqrt(nv / c2) + ADAM_EPS) + ADAM_WD * wv)
    return delta, nm, nv


def _sum_slabs(r, *, name):
    n, rows, lanes = r.shape
    tr = _tile(rows, (1024, 512, 256, 128, 64, 32, 16))

    def body(r_ref, o_ref):
        acc = r_ref[0]
        for d in range(1, n):
            acc = acc + r_ref[d]
        o_ref[...] = acc

    return pl.pallas_call(
        body,
        name=name,
        grid=(rows // tr,),
        in_specs=[pl.BlockSpec((n, tr, lanes), lambda i: (0, i, 0))],
        out_specs=pl.BlockSpec((tr, lanes), lambda i: (i, 0)),
        out_shape=jax.ShapeDtypeStruct((rows, lanes), F32),
        compiler_params=_params(("parallel",)),
    )(r)


def _adamw(w, g, m, v, *, name):
    rows, cols = w.shape
    tr = _tile(rows, (1024, 512, 256, 128, 64, 32, 16))
    spec = pl.BlockSpec((tr, cols), lambda i: (i, 0))

    def body(w_ref, g_ref, m_ref, v_ref, d_ref, nm_ref, nv_ref):
        d_ref[...], nm_ref[...], nv_ref[...] = _adam_update(w_ref[...], g_ref[...], m_ref[...], v_ref[...])

    shp = jax.ShapeDtypeStruct((rows, cols), F32)
    return pl.pallas_call(
        body,
        name=name,
        grid=(rows // tr,),
        in_specs=[spec, spec, spec, spec],
        out_specs=(spec, spec, spec),
        out_shape=(shp, shp, shp),
        compiler_params=_params(("parallel",)),
    )(w, g, m, v)


def _adamw_sum(w, received, m, v, *, name):
    layers, rows, cols = w.shape
    n = received[0].shape[0]
    tr = _tile(rows, (256, 128, 64, 32, 16))
    spec = pl.BlockSpec((1, tr, cols), lambda li, i: (li, i, 0))

    def partial_spec(own):
        return pl.BlockSpec((n, tr, cols), lambda li, i: (0, jnp.where(li == own, i, 0), 0))

    def body(w_ref, *refs):
        r_refs, (m_ref, v_ref, g_ref, d_ref, nm_ref, nv_ref) = refs[:layers], refs[layers:]
        for own in range(layers):
            @pl.when(pl.program_id(0) == own)
            def _(r_ref=r_refs[own]):
                gv = r_ref[0].astype(F32)
                for d in range(1, n):
                    gv = gv + r_ref[d].astype(F32)
                g_ref[0] = gv
                d_ref[0], nm_ref[0], nv_ref[0] = _adam_update(w_ref[0], gv, m_ref[0], v_ref[0])

    shp = jax.ShapeDtypeStruct((layers, rows, cols), F32)
    return pl.pallas_call(
        body,
        name=name,
        grid=(layers, rows // tr),
        in_specs=[spec] + [partial_spec(own) for own in range(layers)] + [spec, spec],
        out_specs=(spec, spec, spec, spec),
        out_shape=(shp, shp, shp, shp),
        compiler_params=_params(("arbitrary", "arbitrary")),
    )(w, *received, m, v)


_MATMUL_WEIGHTS = (("conv_w_in", 2), ("conv_w_out", 1), ("attn_w_qkv", 2), ("attn_w_o", 1), ("ffn_w_up", 2),
                   ("ffn_w_down", 1))
_TAP_WEIGHTS = (("conv_a_dw_w", 2), ("conv_b_dw_w", 2), ("ffn_dw_w", 2))
_REPLICATED = ("mix_norm_g", "ffn_norm_g", "conv_a_dw_b", "conv_a_ln_g", "conv_a_ln_b", "attn_q_g", "attn_k_g",
               "ffn_dw_b")
_WEIGHT_ORDER = ("mix_norm_g", "ffn_norm_g", "conv_w_in", "conv_a_dw_w", "conv_a_dw_b", "conv_a_ln_g",
                 "conv_a_ln_b", "conv_b_dw_w", "conv_w_out", "attn_w_qkv", "attn_q_g", "attn_k_g", "attn_w_o",
                 "ffn_w_up", "ffn_dw_w", "ffn_dw_b", "ffn_w_down")


def _assemble(gathered, axis):
    n, l, a, b = gathered.shape
    if axis == 2:
        return jnp.transpose(gathered, (1, 2, 0, 3)).reshape(l, a, n * b)
    return jnp.transpose(gathered, (1, 0, 2, 3)).reshape(l, n * a, b)


def _split_shards(full, axis):
    l, a, b = full.shape
    if axis == 2:
        return jnp.transpose(full.reshape(l, a, N_DEV, b // N_DEV), (2, 0, 1, 3))
    return jnp.transpose(full.reshape(l, N_DEV, a // N_DEV, b), (1, 0, 2, 3))


def kernel(x, mix_norm_g, ffn_norm_g, conv_w_in, conv_a_dw_w, conv_a_dw_b, conv_a_ln_g, conv_a_ln_b, conv_b_dw_w, conv_w_out, attn_w_qkv, attn_q_g, attn_k_g, attn_w_o, ffn_w_up, ffn_dw_w, ffn_dw_b, ffn_w_down, loss_target, m_mix_norm_g, m_ffn_norm_g, m_conv_w_in, m_conv_a_dw_w, m_conv_a_dw_b, m_conv_a_ln_g, m_conv_a_ln_b, m_conv_b_dw_w, m_conv_w_out, m_attn_w_qkv, m_attn_q_g, m_attn_k_g, m_attn_w_o, m_ffn_w_up, m_ffn_dw_w, m_ffn_dw_b, m_ffn_w_down, v_mix_norm_g, v_ffn_norm_g, v_conv_w_in, v_conv_a_dw_w, v_conv_a_dw_b, v_conv_a_ln_g, v_conv_a_ln_b, v_conv_b_dw_w, v_conv_w_out, v_attn_w_qkv, v_attn_q_g, v_attn_k_g, v_attn_w_o, v_ffn_w_up, v_ffn_dw_w, v_ffn_dw_b, v_ffn_w_down):
    args = dict(locals())
    weights = {n: args[n] for n in _WEIGHT_ORDER}
    mom_m = {n: args["m_" + n] for n in _WEIGHT_ORDER}
    mom_v = {n: args["v_" + n] for n in _WEIGHT_ORDER}

    _, s, d = x.shape
    depth = mix_norm_g.shape[0]
    dh = attn_q_g.shape[1]
    x0 = x.reshape(s, d)
    target = loss_target.reshape(s, d)

    shard_axis = dict(_MATMUL_WEIGHTS)
    shard16 = {n: weights[n].astype(BF16) for n, _ in _MATMUL_WEIGHTS}
    full = {}

    def layer_keys(layer):
        if layer >= depth:
            return []
        mixer = ("conv_w_in", "conv_w_out") if layer % 2 == 0 else ("attn_w_qkv", "attn_w_o")
        return [(mixer[0], layer // 2), (mixer[1], layer // 2), ("ffn_w_up", layer), ("ffn_w_down", layer)]

    def shards(keys):
        return [shard16[n][idx] for n, idx in keys]

    def assemble(keys, gathered):
        for (n, idx), g in zip(keys, gathered):
            if shard_axis[n] == 2:
                full[(n, idx)] = _assemble_cols(g, name=f"assemble_{n}_{idx}")
            else:
                full[(n, idx)] = g.reshape(N_DEV * g.shape[1], g.shape[2])

    assemble(layer_keys(0)[:2], _all_gather(shards(layer_keys(0)[:2]), "gather_first_mixer_weights"))
    tap_shapes = [weights[n].shape for n, _ in _TAP_WEIGHTS]
    g_tap = _all_gather([_pack([weights[n] for n, _ in _TAP_WEIGHTS], F32)], "gather_tap_weights")[0]
    for (n, axis), part in zip(_TAP_WEIGHTS, _unpack(g_tap, tap_shapes, lead=(N_DEV,))):
        full[n] = _assemble(part, axis)
    f = ffn_w_down.shape[1] * N_DEV

    saved = []
    xs = x0
    hn = _rmsnorm_fwd(xs, mix_norm_g[0:1], name="mix_norm_fwd_0")
    for layer in range(depth):
        i = layer // 2
        nxt = layer_keys(layer + 1)
        sv = {"x_in": xs, "h": hn}
        if layer % 2 == 0:
            own_ffn = layer_keys(0)[2:] if layer == 0 else []
            p, got = _matmul(hn, full[("conv_w_in", i)], gather_cargo=shards(own_ffn), name=f"conv_in_{layer}")
            assemble(own_ffn, got)
            (ab, a1), got = _conv_act_fwd(p, full["conv_a_dw_w"][i], conv_a_dw_b[i:i + 1], conv_a_ln_g[i:i + 1],
                                          conv_a_ln_b[i:i + 1], full["conv_b_dw_w"][i], gather_cargo=shards(nxt[0:1]),
                                          name=f"conv_act_fwd_{layer}")
            assemble(nxt[0:1], got)
            sv.update(p=p, a1=a1, mix=ab)
            (xs, h2), _ = _matmul_add_norm(ab, full[("conv_w_out", i)], xs, ffn_norm_g[layer:layer + 1],
                                           name=f"conv_out_{layer}")
            riders = [nxt[1:2], nxt[2:3], nxt[3:4]]
        else:
            reps = LANES // dh
            gains = jnp.concatenate([
                jnp.broadcast_to(jnp.tile(attn_q_g[i], reps) * dh ** -0.5, (8, LANES)),
                jnp.broadcast_to(jnp.tile(attn_k_g[i], reps), (8, LANES)),
                jnp.ones((8, LANES), F32)], axis=0)
            qkv = _matmul(hn, full[("attn_w_qkv", i)], name=f"attn_qkv_{layer}")
            qkv16 = _qkv_prep_fwd(qkv, gains, dh, name=f"qkv_prep_fwd_{layer}")
            o, got = _attn_fwd(qkv16, dh, gather_cargo=shards(nxt), name=f"attn_fwd_{layer}")
            assemble(nxt, got)
            sv.update(qkv=qkv, qkv16=qkv16, gains=gains, mix=o)
            (xs, h2), _ = _matmul_add_norm(o, full[("attn_w_o", i)], xs, ffn_norm_g[layer:layer + 1],
                                           name=f"attn_out_{layer}")
            riders = [[], [], []]
        sv["x_mid"] = xs
        w_up = full[("ffn_w_up", layer)]
        ug, got = _matmul(h2, w_up[:, :f], out_dtype=BF16, gather_cargo=shards(riders[0]),
                          name=f"ffn_up_gate_{layer}")
        assemble(riders[0], got)
        uv = _matmul(h2, w_up[:, f:], out_dtype=BF16, name=f"ffn_up_val_{layer}")
        taps = full["ffn_dw_w"][layer]
        bias = ffn_dw_b[layer:layer + 1]
        act, got = _ffn_act_fwd(ug, uv, taps[:, :f], taps[:, f:], bias[:, :f], bias[:, f:],
                                gather_cargo=shards(riders[1]), name=f"ffn_act_fwd_{layer}")
        assemble(riders[1], got)
        sv.update(h2=h2, ug=ug, uv=uv, act=act)
        if layer + 1 < depth:
            (xs, hn), got = _matmul_add_norm(act, full[("ffn_w_down", layer)], xs, mix_norm_g[layer + 1:layer + 2],
                                             gather_cargo=shards(riders[2]), name=f"ffn_down_{layer}")
        else:
            xs, got = _matmul(act, full[("ffn_w_down", layer)], add=xs, gather_cargo=shards(riders[2]),
                              name=f"ffn_down_{layer}")
        assemble(riders[2], got)
        saved.append(sv)

    sq, dx, dx16 = _loss_head(xs, target, name="loss_head")
    loss = lax.psum(0.5 * sq[0, 0] / d, ("x", "y", "c"))

    grads = {n: [None] * weights[n].shape[0] for n in _WEIGHT_ORDER}
    queued, arrived = [], {}

    def queue(n, idx, g):
        a, b = weights[n].shape[1:]
        slabs = _split_cols(g, name=f"split_{n}_{idx}") if shard_axis[n] == 2 else g.reshape(N_DEV, a, b)
        queued.append(((n, idx), slabs))

    def carried(call):
        keys = [k for k, _ in queued]
        slabs = [sl for _, sl in queued]
        queued.clear()
        outs, received = call(slabs)
        arrived.update(zip(keys, received))
        return outs

    for layer in reversed(range(depth)):
        i = layer // 2
        sv = saved[layer]
        w_up = full[("ffn_w_up", layer)]
        taps = full["ffn_dw_w"][layer]
        bias = ffn_dw_b[layer:layer + 1]
        dact = _matmul(dx16, full[("ffn_w_down", layer)], tb=True, out_dtype=BF16, name=f"ffn_down_dx_{layer}")
        queue("ffn_w_down", layer, _weight_grad(sv["act"], dx16, name=f"ffn_down_dw_{layer}"))
        dug, duv, sumg, sumv = carried(lambda cargo: _ffn_act_bwd(
            sv["ug"], sv["uv"], dact, taps[:, :f], taps[:, f:], bias[:, :f], bias[:, f:], cargo=cargo,
            name=f"ffn_act_bwd_{layer}"))
        sums = jnp.concatenate([sumg, sumv], axis=1).reshape(4, 8, 2 * f).sum(axis=1)
        grads["ffn_dw_w"][layer] = sums[:3]
        grads["ffn_dw_b"][layer] = sums[3]
        dh2 = _matmul(dug, w_up[:, :f], tb=True, name=f"ffn_up_gate_dx_{layer}")
        (dx, dx16, dg), _ = _matmul_norm_bwd(duv, w_up[:, f:], sv["x_mid"], ffn_norm_g[layer:layer + 1], dx, add=dh2,
                                             name=f"ffn_up_val_dx_{layer}")
        grads["ffn_norm_g"][layer] = dg[0]
        queue("ffn_w_up", layer, jnp.concatenate(
            [_weight_grad(sv["h2"], dug, name=f"ffn_up_gate_dw_{layer}"),
             _weight_grad(sv["h2"], duv, name=f"ffn_up_val_dw_{layer}")], axis=1))

        if layer % 2 == 0:
            dab = _matmul(dx16, full[("conv_w_out", i)], tb=True, name=f"conv_out_dx_{layer}")
            queue("conv_w_out", i, _weight_grad(sv["mix"], dx16, name=f"conv_out_dw_{layer}"))
            ka = full["conv_a_dw_w"].shape[1]
            kb = full["conv_b_dw_w"].shape[1]
            dp, dwa, dba, dlng, dlnb, dwb = carried(lambda cargo: _conv_act_bwd(
                sv["p"], sv["a1"], dab, full["conv_a_dw_w"][i], conv_a_ln_g[i:i + 1], conv_a_ln_b[i:i + 1],
                full["conv_b_dw_w"][i], cargo=cargo, name=f"conv_act_bwd_{layer}"))
            grads["conv_a_dw_w"][i] = dwa[:ka]
            grads["conv_a_dw_b"][i] = dba[0]
            grads["conv_a_ln_g"][i] = dlng[0]
            grads["conv_a_ln_b"][i] = dlnb[0]
            grads["conv_b_dw_w"][i] = dwb[:kb]
            queue("conv_w_in", i, _weight_grad(sv["h"], dp, name=f"conv_in_dw_{layer}"))
            dy, w_in = dp, full[("conv_w_in", i)]
        else:
            do = _matmul(dx16, full[("attn_w_o", i)], tb=True, name=f"attn_out_dx_{layer}")
            queue("attn_w_o", i, _weight_grad(sv["mix"], dx16, name=f"attn_out_dw_{layer}"))
            dq, dk, dv = carried(lambda cargo: _attn_bwd(sv["qkv16"], sv["mix"], do, dh, cargo=cargo,
                                                         name=f"attn_bwd_{layer}"))
            dqkv, dgains = _qkv_prep_bwd(sv["qkv"], dq, dk, dv, sv["gains"], dh, name=f"qkv_prep_bwd_{layer}")
            grads["attn_q_g"][i] = dgains[0].reshape(-1, dh).sum(axis=0) * dh ** -0.5
            grads["attn_k_g"][i] = dgains[8].reshape(-1, dh).sum(axis=0)
            queue("attn_w_qkv", i, _weight_grad(sv["h"], dqkv, name=f"attn_qkv_dw_{layer}"))
            dy, w_in = dqkv, full[("attn_w_qkv", i)]

        def mixer_in_bwd(cargo):
            return _matmul_norm_bwd(dy, w_in, sv["x_in"], mix_norm_g[layer:layer + 1], dx, cargo=cargo,
                                    name=f"mixer_in_dx_{layer}")

        dx, dx16, dg = carried(mixer_in_bwd) if layer == 0 else mixer_in_bwd(())[0]
        grads["mix_norm_g"][layer] = dg[0]

    me = _my_index()
    final_grads, delta, new_m, new_v = {}, {}, {}, {}
    for n, _ in _MATMUL_WEIGHTS:
        final_grads[n], delta[n], new_m[n], new_v[n] = _adamw_sum(
            weights[n], [arrived[(n, idx)] for idx in range(weights[n].shape[0])], mom_m[n], mom_v[n],
            name=f"adamw_{n}")

    small_names = list(_REPLICATED) + [n for n, _ in _TAP_WEIGHTS]
    local = {n: jnp.stack(grads[n], axis=0) for n in small_names}
    small_shapes = [local[n].shape for n in small_names]
    g_small = _sum_slabs(_all_gather([_pack([local[n] for n in small_names], F32)], "gather_small_grads")[0],
                         name="sum_small_grads")
    reduced = dict(zip(small_names, _unpack(g_small, small_shapes)))
    for n in _REPLICATED:
        final_grads[n] = reduced[n]
    for n, axis in _TAP_WEIGHTS:
        final_grads[n] = lax.dynamic_index_in_dim(_split_shards(reduced[n], axis), me, axis=0, keepdims=False)
    shapes = [weights[n].shape for n in small_names]
    d_s, m_s, v_s = _adamw(
        _pack([weights[n] for n in small_names], F32), _pack([final_grads[n] for n in small_names], F32),
        _pack([mom_m[n] for n in small_names], F32), _pack([mom_v[n] for n in small_names], F32), name="adamw_small")
    delta.update(zip(small_names, _unpack(d_s, shapes)))
    new_m.update(zip(small_names, _unpack(m_s, shapes)))
    new_v.update(zip(small_names, _unpack(v_s, shapes)))

    order = list(_WEIGHT_ORDER)
    return (loss, dx.reshape(x.shape), *[final_grads[n] for n in order], *[delta[n] for n in order],
            *[new_m[n] for n in order], *[new_v[n] for n in order])
```

```python
import jax
import jax.numpy as jnp
from jax import lax
from jax.experimental import pallas as pl
from jax.experimental.pallas import tpu as pltpu

F32 = jnp.float32
BF16 = jnp.bfloat16
EPS = 1e-6
N_DEV = 8
MESH_ID = pl.DeviceIdType.MESH

ADAM_LR = 0.001
ADAM_B1 = 0.9
ADAM_B2 = 0.999
ADAM_EPS = 1e-08
ADAM_WD = 0.01
ADAM_STEP = 10

V7X_VMEM_LIMIT_BYTES = 48 * 1024 * 1024
PACK_QUANTUM = 2048
A_HALO = 32
S_HALO = 8
FFN_HALO_BLOCK = 16
LANES = 128
ATTN_Q_BLOCK = 256
ATTN_K_BLOCK = 256
MATMUL_MAX_SINGLE_K = 3072
MXU_WIDTH = 256
FFN_CHUNK_ROWS = 64
ATTN_DEAD_LOG_WEIGHT = -110.0


def _tile(n, prefs):
    for p in prefs:
        if n % p == 0:
            return p
    return n


def _params(sem=None):
    return pltpu.CompilerParams(dimension_semantics=sem, vmem_limit_bytes=V7X_VMEM_LIMIT_BYTES)


def _sigmoid(x):
    return 1.0 / (1.0 + jnp.exp(-x))


def _dot(a, b, ca, cb):
    return lax.dot_general(a, b, (((ca,), (cb,)), ((), ())), preferred_element_type=F32)


def _my_index():
    return 4 * lax.axis_index("x") + 2 * lax.axis_index("y") + lax.axis_index("c")


def _all_gather(shards, name):
    n = len(shards)

    def body(*refs):
        x_refs, out_refs = refs[:n], refs[n:2 * n]
        send_sems, recv_sems, local_sems = refs[2 * n:]
        x, y, c = lax.axis_index("x"), lax.axis_index("y"), lax.axis_index("c")
        me, sibling = (x, y, c), (x, y, 1 - c)
        chips = [(1 - x, y), (x, 1 - y), (1 - x, 1 - y)]

        def copy(a, k, block, to, src=None):
            slot = out_refs[a].at[4 * block[0] + 2 * block[1] + block[2]]
            return pltpu.make_async_remote_copy(
                src_ref=slot if src is None else src, dst_ref=slot,
                send_sem=send_sems.at[7 * a + k], recv_sem=recv_sems.at[7 * a + k],
                device_id=to, device_id_type=MESH_ID)

        mine = [pltpu.make_async_copy(x_refs[a], out_refs[a].at[4 * x + 2 * y + c], local_sems.at[a])
                for a in range(n)]
        for cp in mine:
            cp.start()
        first = []
        for a in range(n):
            first.append(copy(a, 0, me, sibling, src=x_refs[a]))
            first += [copy(a, 1 + j, me, (*chip, c), src=x_refs[a]) for j, chip in enumerate(chips)]
        for cp in first:
            cp.start()
        passed = []
        for j, chip in enumerate(chips):
            for a in range(n):
                copy(a, 1 + j, (*chip, c), me).wait_recv()
                fwd = copy(a, 4 + j, (*chip, c), sibling)
                fwd.start()
                passed.append(fwd)
        for a in range(n):
            copy(a, 0, sibling, me).wait_recv()
            for j, chip in enumerate(chips):
                copy(a, 4 + j, (*chip, 1 - c), me).wait_recv()
        for cp in first + passed:
            cp.wait_send()
        for cp in mine:
            cp.wait()

    hbm = pl.BlockSpec(memory_space=pl.ANY)
    return pl.pallas_call(
        body,
        name=name,
        out_shape=[jax.ShapeDtypeStruct((N_DEV,) + sh.shape, sh.dtype) for sh in shards],
        in_specs=[hbm] * n,
        out_specs=[hbm] * n,
        scratch_shapes=[
            pltpu.SemaphoreType.DMA((7 * n,)),
            pltpu.SemaphoreType.DMA((7 * n,)),
            pltpu.SemaphoreType.DMA((n,)),
        ],
    )(*shards)


def _exchange_semaphores(n):
    return [pltpu.SemaphoreType.DMA((7 * n,)), pltpu.SemaphoreType.DMA((7 * n,)), pltpu.SemaphoreType.DMA((n,))]


def _exchange_copies(g_refs, out_refs, send_sems, recv_sems, local_sems, gather=False):
    n = len(g_refs)
    x, y, c = lax.axis_index("x"), lax.axis_index("y"), lax.axis_index("c")
    me = 4 * x + 2 * y + c

    def part(a, d):
        return g_refs[a] if gather else g_refs[a].at[d]

    local = [pltpu.make_async_copy(part(a, me), out_refs[a].at[me], local_sems.at[a]) for a in range(n)]
    remote = []
    for k in range(1, N_DEV):
        px = (x + ((k >> 2) & 1)) % 2
        py = (y + ((k >> 1) & 1)) % 2
        pc = (c + (k & 1)) % 2
        for a in range(n):
            remote.append(pltpu.make_async_remote_copy(
                src_ref=part(a, 4 * px + 2 * py + pc), dst_ref=out_refs[a].at[me],
                send_sem=send_sems.at[7 * a + k - 1], recv_sem=recv_sems.at[7 * a + k - 1],
                device_id=(px, py, pc), device_id_type=MESH_ID))
    return local, remote


def _exchange_start(copies):
    for cp in copies[0] + copies[1]:
        cp.start()


def _exchange_wait(copies):
    local, remote = copies
    for cp in remote:
        cp.wait_recv()
    for cp in remote:
        cp.wait_send()
    for cp in local:
        cp.wait()


def _carrying_call(body, cargo, *, name, grid, in_specs, out_specs, out_shape, scratch_shapes, operands,
                   compiler_params, gather=False):
    n = len(cargo)
    n_in, n_out, n_scr = len(in_specs), len(out_specs), len(scratch_shapes)
    if n == 0:
        return pl.pallas_call(body, name=name, grid=grid, in_specs=in_specs, out_specs=out_specs,
                              out_shape=out_shape, scratch_shapes=scratch_shapes,
                              compiler_params=compiler_params)(*operands), []

    def carrying(*refs):
        cuts = [0, n_in, n_in + n, n_in + n + n_out, n_in + 2 * n + n_out, n_in + 2 * n + n_out + n_scr, len(refs)]
        ins, g_refs, outs, r_refs, scr, sems = [refs[a:b] for a, b in zip(cuts[:-1], cuts[1:])]
        steps = [pl.program_id(ax) for ax in range(len(grid))]
        first = steps[0] == 0
        last = steps[0] == grid[0] - 1
        for ax in range(1, len(grid)):
            first = first & (steps[ax] == 0)
            last = last & (steps[ax] == grid[ax] - 1)

        @pl.when(first)
        def _():
            _exchange_start(_exchange_copies(g_refs, r_refs, *sems, gather=gather))

        body(*ins, *outs, *scr)

        @pl.when(last)
        def _():
            _exchange_wait(_exchange_copies(g_refs, r_refs, *sems, gather=gather))

    hbm = pl.BlockSpec(memory_space=pl.ANY)
    lead = (N_DEV,) if gather else ()
    res = pl.pallas_call(
        carrying,
        name=name,
        grid=grid,
        in_specs=list(in_specs) + [hbm] * n,
        out_specs=list(out_specs) + [hbm] * n,
        out_shape=list(out_shape) + [jax.ShapeDtypeStruct(lead + g.shape, g.dtype) for g in cargo],
        scratch_shapes=list(scratch_shapes) + _exchange_semaphores(n),
        compiler_params=compiler_params,
    )(*operands, *cargo)
    return res[:n_out], res[n_out:]


def _assemble_cols(g, *, name):
    n, r, w = g.shape
    tr = _tile(r, (256, 128, 64, 32, 16))

    def body(g_ref, o_ref):
        for j in range(n):
            o_ref[:, j * w:(j + 1) * w] = g_ref[j]

    return pl.pallas_call(
        body,
        name=name,
        grid=(r // tr,),
        in_specs=[pl.BlockSpec((n, tr, w), lambda i: (0, i, 0))],
        out_specs=pl.BlockSpec((tr, n * w), lambda i: (i, 0)),
        out_shape=jax.ShapeDtypeStruct((r, n * w), g.dtype),
        compiler_params=_params(("parallel",)),
    )(g)


def _split_cols(x, *, name):
    r, nw = x.shape
    w = nw // N_DEV
    tr = _tile(r, (256, 128, 64, 32, 16))

    def body(x_ref, o_ref):
        for j in range(N_DEV):
            o_ref[j] = x_ref[:, j * w:(j + 1) * w]

    return pl.pallas_call(
        body,
        name=name,
        grid=(r // tr,),
        in_specs=[pl.BlockSpec((tr, nw), lambda i: (i, 0))],
        out_specs=pl.BlockSpec((N_DEV, tr, w), lambda i: (0, i, 0)),
        out_shape=jax.ShapeDtypeStruct((N_DEV, r, w), x.dtype),
        compiler_params=_params(("parallel",)),
    )(x)


def _padded(n):
    return -(-n // PACK_QUANTUM) * PACK_QUANTUM


def _pack(parts, dtype, lead=()):
    flat = []
    for p in parts:
        p = p.astype(dtype).reshape(lead + (-1,))
        n = p.shape[-1]
        flat.append(jnp.pad(p, [(0, 0)] * len(lead) + [(0, _padded(n) - n)]))
    return jnp.concatenate(flat, axis=-1).reshape(lead + (-1, 128))


def _unpack(buf, shapes, lead=()):
    flat = buf.reshape(lead + (-1,))
    out, off = [], 0
    for shp in shapes:
        n = 1
        for d in shp:
            n *= d
        out.append(flat[..., off:off + n].reshape(lead + tuple(shp)))
        off += _padded(n)
    return out


def _half_if_tiled(n):
    half = n // 2
    return (half,) if n % 2 == 0 and half % LANES == 0 and half <= 1536 else ()


def _matmul(a, b, *, ta=False, tb=False, b_half=None, out_dtype=F32, add=None, gather_cargo=None, name):
    if ta:
        kdim, m = a.shape
    else:
        m, kdim = a.shape
    halves = 1 if b_half is None else 2
    n = b.shape[0] if tb else b.shape[1] // halves
    bm = _tile(m, (1024,) + _half_if_tiled(m) + (512, 256, 128))
    bn = _tile(n, (512, 256, 128))
    if bn < 512 and n % MXU_WIDTH == 0 and n <= MATMUL_MAX_SINGLE_K:
        bn, bm = n, _tile(m, (512, 256, 128))
    bk = kdim if kdim <= MATMUL_MAX_SINGLE_K else _tile(kdim, (1024, 512, 256, 128))
    nk = kdim // bk

    a_spec = pl.BlockSpec((bk, bm), lambda i, j, k: (k, i)) if ta else pl.BlockSpec((bm, bk), lambda i, j, k: (i, k))
    col0 = 0 if b_half is None else b_half * ((nk if tb else n // bn))
    b_spec = (pl.BlockSpec((bn, bk), lambda i, j, k: (j, k + col0)) if tb
              else pl.BlockSpec((bk, bn), lambda i, j, k: (k, j + col0)))
    o_spec = pl.BlockSpec((bm, bn), lambda i, j, k: (i, j))
    in_specs = [a_spec, b_spec] + ([o_spec] if add is not None else [])
    operands = [a, b] + ([add] if add is not None else [])

    def product(a_ref, b_ref):
        return _dot(a_ref[...].astype(BF16), b_ref[...].astype(BF16), 0 if ta else 1, 1 if tb else 0)

    def finish(r, refs):
        if add is not None:
            r = r + refs[2][...]
        return r

    def body_single(*refs):
        o_ref = refs[-1]
        o_ref[...] = finish(product(refs[0], refs[1]), refs).astype(o_ref.dtype)

    def body_multi(*refs):
        o_ref, acc_ref = refs[-2], refs[-1]
        k = pl.program_id(2)

        @pl.when(k == 0)
        def _():
            acc_ref[...] = jnp.zeros_like(acc_ref)

        acc_ref[...] += product(refs[0], refs[1])

        @pl.when(k == nk - 1)
        def _():
            o_ref[...] = finish(acc_ref[...], refs).astype(o_ref.dtype)

    outs, received = _carrying_call(
        body_single if nk == 1 else body_multi,
        () if gather_cargo is None else gather_cargo,
        gather=True,
        name=name,
        grid=(m // bm, n // bn, nk),
        in_specs=in_specs,
        out_specs=[o_spec],
        out_shape=[jax.ShapeDtypeStruct((m, n), out_dtype)],
        scratch_shapes=[] if nk == 1 else [pltpu.VMEM((bm, bn), F32)],
        operands=operands,
        compiler_params=_params(("parallel", "parallel", "arbitrary") if gather_cargo is None else ("arbitrary",) * 3),
    )
    return outs[0] if gather_cargo is None else (outs[0], received)


def _rows_matmul(a, b, *, tb, add, row_ins, vec_ins, epilogue, row_out_dtypes, n_sums, cargo, gather, name,
                 b_half=None):
    m, kdim = a.shape
    n = b.shape[0] if tb else b.shape[1]
    assert kdim <= MATMUL_MAX_SINGLE_K and (b_half is None or tb)
    b_spec = (pl.BlockSpec(b.shape, lambda i: (0, 0)) if b_half is None
              else pl.BlockSpec((n, kdim), lambda i: (0, b_half)))
    bm = _tile(m, (512, 256, 128))
    row = pl.BlockSpec((bm, n), lambda i: (i, 0))
    vec = pl.BlockSpec((1, n), lambda i: (0, 0))
    n_add = int(add is not None)
    n_ins = len(row_ins) + len(vec_ins)
    n_rows = len(row_out_dtypes)

    def body(a_ref, b_ref, *refs):
        y = _dot(a_ref[...].astype(BF16), b_ref[...].astype(BF16), 1, 1 if tb else 0)
        if add is not None:
            y = y + refs[0][...]
        ins = [r[...] for r in refs[n_add:n_add + n_ins]]
        outs = refs[n_add + n_ins:]
        rows, sums = epilogue(y, *ins)
        for ref, val in zip(outs[:n_rows], rows):
            ref[...] = val.astype(ref.dtype)
        if n_sums:
            @pl.when(pl.program_id(0) == 0)
            def _():
                for ref in outs[n_rows:]:
                    ref[...] = jnp.zeros_like(ref)

            for ref, val in zip(outs[n_rows:], sums):
                ref[...] += val

    return _carrying_call(
        body,
        cargo,
        gather=gather,
        name=name,
        grid=(m // bm,),
        in_specs=[pl.BlockSpec((bm, kdim), lambda i: (i, 0)), b_spec]
        + [row] * (n_add + len(row_ins)) + [vec] * len(vec_ins),
        out_specs=[row] * n_rows + [vec] * n_sums,
        out_shape=[jax.ShapeDtypeStruct((m, n), dt) for dt in row_out_dtypes]
        + [jax.ShapeDtypeStruct((1, n), F32)] * n_sums,
        scratch_shapes=[],
        operands=[a, b] + ([add] if add is not None else []) + list(row_ins) + list(vec_ins),
        compiler_params=_params(("arbitrary",)),
    )


def _matmul_add_norm(a, b, add, g, *, gather_cargo=(), name):
    def epilogue(y, gv):
        h = y * lax.rsqrt(jnp.mean(y * y, axis=-1, keepdims=True) + EPS) * gv
        return (y, h), ()

    return _rows_matmul(a, b, tb=False, add=add, row_ins=(), vec_ins=(g,), epilogue=epilogue,
                        row_out_dtypes=(F32, BF16), n_sums=0, cargo=gather_cargo, gather=True, name=name)


def _matmul_norm_bwd(a, b, x, g, res, *, b_half=None, add=None, cargo=(), name):
    def epilogue(dyv, xv, resv, gv):
        rs = lax.rsqrt(jnp.mean(xv * xv, axis=-1, keepdims=True) + EPS)
        xn = xv * rs
        dyg = dyv * gv
        dx = rs * (dyg - xn * jnp.mean(dyg * xn, axis=-1, keepdims=True)) + resv
        return (dx, dx), (jnp.sum(dyv * xn, axis=0, keepdims=True),)

    return _rows_matmul(a, b, tb=True, b_half=b_half, add=add, row_ins=(x, res), vec_ins=(g,), epilogue=epilogue,
                        row_out_dtypes=(F32, BF16), n_sums=1, cargo=cargo, gather=False, name=name)


def _weight_grad(x, dy, *, name):
    return _matmul(x, dy, ta=True, out_dtype=BF16, name=name)


def _rmsnorm_fwd(x, g, *, name):
    r, d = x.shape
    tr = _tile(r, (1024, 512, 256, 128))

    def body(x_ref, g_ref, o_ref):
        xv = x_ref[...]
        y = xv * lax.rsqrt(jnp.mean(xv * xv, axis=-1, keepdims=True) + EPS)
        y = y * g_ref[...]
        o_ref[...] = y.astype(o_ref.dtype)

    return pl.pallas_call(
        body,
        name=name,
        grid=(r // tr,),
        in_specs=[pl.BlockSpec((tr, d), lambda i: (i, 0)), pl.BlockSpec((1, d), lambda i: (0, 0))],
        out_specs=pl.BlockSpec((tr, d), lambda i: (i, 0)),
        out_shape=jax.ShapeDtypeStruct((r, d), BF16),
        compiler_params=_params(("parallel",)),
    )(x, g)


def _loss_head(y, target, *, name):
    s, d = y.shape
    ts = _tile(s, (512, 256, 128))
    row = pl.BlockSpec((ts, d), lambda i: (i, 0))
    acc = pl.BlockSpec((8, 128), lambda i: (0, 0))

    def body(y_ref, t_ref, l_ref, dy_ref, dy16_ref):
        e = y_ref[...] - t_ref[...]
        dy = e * (1.0 / d)
        dy_ref[...] = dy
        dy16_ref[...] = dy.astype(dy16_ref.dtype)

        @pl.when(pl.program_id(0) == 0)
        def _():
            l_ref[...] = jnp.zeros_like(l_ref)

        sq = jnp.sum(jnp.sum(e * e, axis=-1, keepdims=True), axis=0, keepdims=True)
        l_ref[...] += jnp.broadcast_to(sq, l_ref.shape)

    return pl.pallas_call(
        body,
        name=name,
        grid=(s // ts,),
        in_specs=[row, row],
        out_specs=(acc, row, row),
        out_shape=(jax.ShapeDtypeStruct((8, 128), F32), jax.ShapeDtypeStruct((s, d), F32),
                   jax.ShapeDtypeStruct((s, d), BF16)),
        compiler_params=_params(("arbitrary",)),
    )(y, target)


def _ffn_tiles(s, f):
    return _tile(s, (512, 256, 128)), _tile(f, _half_if_tiled(f) + (512, 256, 128))


def _conv3(xs, w, b):
    return b + xs[0] * w[0:1] + xs[1] * w[1:2] + xs[2] * w[2:3]


def _shifted3(x, rows):
    return [x[S_HALO - 2 + k:S_HALO - 2 + k + rows] for k in range(3)]


def _ffn_act_fwd(ug, uv, wg, wv, bg, bv, *, gather_cargo=(), name):
    s, f = ug.shape
    ts, tc = _ffn_tiles(s, f)
    rc = _tile(ts, (FFN_CHUNK_ROWS, 32, 16, 8))
    hb = ts // FFN_HALO_BLOCK
    main = pl.BlockSpec((ts, tc), lambda j, i: (i, j))
    prev = pl.BlockSpec((FFN_HALO_BLOCK, tc), lambda j, i: (jnp.maximum(i * hb - 1, 0), j))
    taps = pl.BlockSpec((3, tc), lambda j, i: (0, j))
    bias = pl.BlockSpec((1, tc), lambda j, i: (0, j))
    lead = FFN_HALO_BLOCK - S_HALO

    def body(ug_ref, ugp_ref, uv_ref, uvp_ref, wg_ref, wv_ref, bg_ref, bv_ref, o_ref, gbuf, vbuf):
        first = pl.program_id(1) == 0
        for buf, p_ref, m_ref in ((gbuf, ugp_ref, ug_ref), (vbuf, uvp_ref, uv_ref)):
            buf[0:FFN_HALO_BLOCK, :] = jnp.where(first, 0.0, p_ref[...].astype(F32))
            buf[FFN_HALO_BLOCK:, :] = m_ref[...].astype(F32)

        def chunk(ci, carry):
            r0 = pl.multiple_of(ci * rc, rc)
            rows = pl.ds(pl.multiple_of(r0 + lead, S_HALO), rc + S_HALO)
            for lb in range(tc // LANES):
                ls = pl.ds(lb * LANES, LANES)
                gate = _conv3(_shifted3(gbuf[rows, ls], rc), wg_ref[:, ls], bg_ref[:, ls])
                val = _conv3(_shifted3(vbuf[rows, ls], rc), wv_ref[:, ls], bv_ref[:, ls])
                o_ref[pl.ds(r0, rc), ls] = (gate * _sigmoid(gate) * val).astype(o_ref.dtype)
            return carry

        lax.fori_loop(0, ts // rc, chunk, 0)

    outs, gathered = _carrying_call(
        body,
        gather_cargo,
        gather=True,
        name=name,
        grid=(f // tc, s // ts),
        in_specs=[main, prev, main, prev, taps, taps, bias, bias],
        out_specs=[main],
        out_shape=[jax.ShapeDtypeStruct((s, f), BF16)],
        scratch_shapes=[pltpu.VMEM((FFN_HALO_BLOCK + ts, tc), F32), pltpu.VMEM((FFN_HALO_BLOCK + ts, tc), F32)],
        operands=(ug, ug, uv, uv, wg, wv, bg, bv),
        compiler_params=_params(("arbitrary", "arbitrary")),
    )
    return outs[0], gathered


def _ffn_act_bwd(ug, uv, df, wg, wv, bg, bv, *, cargo=(), name):
    s, f = ug.shape
    ts, tc = _ffn_tiles(s, f)
    rc = _tile(ts, (FFN_CHUNK_ROWS, 32, 16, 8))
    hb = ts // FFN_HALO_BLOCK
    last_hb = s // FFN_HALO_BLOCK - 1
    n_row = s // ts
    lead = FFN_HALO_BLOCK - S_HALO
    main = pl.BlockSpec((ts, tc), lambda j, i: (i, j))
    prev = pl.BlockSpec((FFN_HALO_BLOCK, tc), lambda j, i: (jnp.maximum(i * hb - 1, 0), j))
    nxt = pl.BlockSpec((FFN_HALO_BLOCK, tc), lambda j, i: (jnp.minimum((i + 1) * hb, last_hb), j))
    taps = pl.BlockSpec((3, tc), lambda j, i: (0, j))
    bias = pl.BlockSpec((1, tc), lambda j, i: (0, j))
    sums = pl.BlockSpec((32, tc), lambda j, i: (0, j))
    ext = rc + S_HALO

    def body(ug_ref, ugp_ref, ugn_ref, uv_ref, uvp_ref, uvn_ref, df_ref, dfn_ref, wg_ref, wv_ref, bg_ref, bv_ref,
             dug_ref, duv_ref, sumg_ref, sumv_ref, gbuf, vbuf, dfbuf):
        i = pl.program_id(1)
        first = i == 0
        last = i == n_row - 1

        @pl.when(first)
        def _():
            sumg_ref[...] = jnp.zeros_like(sumg_ref)
            sumv_ref[...] = jnp.zeros_like(sumv_ref)

        for buf, p_ref, m_ref, n_ref in ((gbuf, ugp_ref, ug_ref, ugn_ref), (vbuf, uvp_ref, uv_ref, uvn_ref)):
            buf[0:FFN_HALO_BLOCK, :] = jnp.where(first, 0.0, p_ref[...].astype(F32))
            buf[FFN_HALO_BLOCK:FFN_HALO_BLOCK + ts, :] = m_ref[...].astype(F32)
            buf[FFN_HALO_BLOCK + ts:, :] = n_ref[...].astype(F32)
        dfbuf[0:ts, :] = df_ref[...].astype(F32)
        dfbuf[ts:, :] = jnp.where(last, 0.0, dfn_ref[...].astype(F32))

        def chunk(ci, carry):
            r0 = pl.multiple_of(ci * rc, rc)
            rows = pl.ds(pl.multiple_of(r0 + lead, S_HALO), ext + S_HALO)
            for lb in range(tc // LANES):
                ls = pl.ds(lb * LANES, LANES)
                xg = _shifted3(gbuf[rows, ls], ext)
                xv = _shifted3(vbuf[rows, ls], ext)
                wgv, wvv = wg_ref[:, ls], wv_ref[:, ls]
                gate = _conv3(xg, wgv, bg_ref[:, ls])
                val = _conv3(xv, wvv, bv_ref[:, ls])
                dfe = dfbuf[pl.ds(r0, ext), ls]
                sg = _sigmoid(gate)
                dgate = dfe * val * (sg * (1.0 + gate * (1.0 - sg)))
                dval = dfe * (gate * sg)
                for dz, xs, w, du_ref, sum_ref in ((dgate, xg, wgv, dug_ref, sumg_ref),
                                                   (dval, xv, wvv, duv_ref, sumv_ref)):
                    du = dz[2:2 + rc] * w[0:1] + dz[1:1 + rc] * w[1:2] + dz[0:rc] * w[2:3]
                    du_ref[pl.ds(r0, rc), ls] = du.astype(du_ref.dtype)
                    dm = dz[0:rc]
                    for k in range(3):
                        sum_ref[8 * k:8 * k + 8, ls] += (dm * xs[k][0:rc]).reshape(rc // 8, 8, LANES).sum(axis=0)
                    sum_ref[24:32, ls] += dm.reshape(rc // 8, 8, LANES).sum(axis=0)
            return carry

        lax.fori_loop(0, ts // rc, chunk, 0)

    return _carrying_call(
        body,
        cargo,
        name=name,
        grid=(f // tc, n_row),
        in_specs=[main, prev, nxt, main, prev, nxt, main, nxt, taps, taps, bias, bias],
        out_specs=(main, main, sums, sums),
        out_shape=(jax.ShapeDtypeStruct((s, f), BF16), jax.ShapeDtypeStruct((s, f), BF16),
                   jax.ShapeDtypeStruct((32, f), F32), jax.ShapeDtypeStruct((32, f), F32)),
        scratch_shapes=[pltpu.VMEM((2 * FFN_HALO_BLOCK + ts, tc), F32), pltpu.VMEM((2 * FFN_HALO_BLOCK + ts, tc), F32),
                        pltpu.VMEM((FFN_HALO_BLOCK + ts, tc), F32)],
        operands=(ug, ug, ug, uv, uv, uv, df, df, wg, wv, bg, bv),
        compiler_params=_params(("arbitrary", "arbitrary")),
    )


def _layernorm_stats(a1):
    mu = jnp.mean(a1, axis=-1, keepdims=True)
    xc = a1 - mu
    rstd = lax.rsqrt(jnp.mean(xc * xc, axis=-1, keepdims=True) + EPS)
    return xc * rstd, rstd


def _conv_act_fwd(p, wa, ba, lng, lnb, wb, *, gather_cargo=(), name):
    s, c5 = p.shape
    c = c5 // 5
    ka, kb = wa.shape[0], wb.shape[0]
    ts = _tile(s, (256, 128))
    hb = ts // A_HALO
    main = pl.BlockSpec((ts, c5), lambda i: (i, 0))
    prev = pl.BlockSpec((A_HALO, c5), lambda i: (jnp.maximum(i * hb - 1, 0), 0))

    def full(arr):
        return pl.BlockSpec(arr.shape, lambda i: (0, 0))

    def body(p_ref, pp_ref, wa_ref, ba_ref, lng_ref, lnb_ref, wb_ref, ab_ref, a1_ref, gbuf, cbuf):
        first = pl.program_id(0) == 0
        gbuf[0:A_HALO, :] = jnp.where(first, 0.0, pp_ref[:, 0:c] * _sigmoid(pp_ref[:, c:2 * c]))
        gbuf[A_HALO:, :] = p_ref[:, 0:c] * _sigmoid(p_ref[:, c:2 * c])
        cbuf[0:A_HALO, :] = jnp.where(first, 0.0, pp_ref[:, 3 * c:4 * c] * pp_ref[:, 4 * c:5 * c])
        cbuf[A_HALO:, :] = p_ref[:, 3 * c:4 * c] * p_ref[:, 4 * c:5 * c]

        a1 = jnp.broadcast_to(ba_ref[...], (ts, c))
        for k in range(ka):
            off = A_HALO - (ka - 1) + k
            a1 = a1 + gbuf[off:off + ts, :] * wa_ref[k:k + 1, :]
        a1_ref[...] = a1
        xhat, _ = _layernorm_stats(a1)
        a2 = xhat * lng_ref[...] + lnb_ref[...]
        ab_ref[:, 0:c] = (a2 * _sigmoid(a2)).astype(ab_ref.dtype)

        cv = jnp.zeros((ts, c), F32)
        for k in range(kb):
            off = A_HALO - (kb - 1) + k
            cv = cv + cbuf[off:off + ts, :] * wb_ref[k:k + 1, :]
        ab_ref[:, c:2 * c] = (p_ref[:, 2 * c:3 * c] * cv).astype(ab_ref.dtype)

    return _carrying_call(
        body,
        gather_cargo,
        gather=True,
        name=name,
        grid=(s // ts,),
        in_specs=[main, prev, full(wa), full(ba), full(lng), full(lnb), full(wb)],
        out_specs=(pl.BlockSpec((ts, 2 * c), lambda i: (i, 0)), pl.BlockSpec((ts, c), lambda i: (i, 0))),
        out_shape=(jax.ShapeDtypeStruct((s, 2 * c), BF16), jax.ShapeDtypeStruct((s, c), F32)),
        scratch_shapes=[pltpu.VMEM((A_HALO + ts, c), F32), pltpu.VMEM((A_HALO + ts, c), F32)],
        operands=(p, p, wa, ba, lng, lnb, wb),
        compiler_params=_params(("arbitrary",)),
    )


def _conv_act_bwd(p, a1, dab, wa, lng, lnb, wb, *, cargo=(), name):
    s, c5 = p.shape
    c = c5 // 5
    ka, kb = wa.shape[0], wb.shape[0]
    ts = _tile(s, (256, 128))
    hb = ts // A_HALO
    n_row = s // ts
    last_hb = s // A_HALO - 1
    ext = ts + A_HALO

    def rows(width):
        return (pl.BlockSpec((ts, width), lambda i: (i, 0)),
                pl.BlockSpec((A_HALO, width), lambda i: (jnp.maximum(i * hb - 1, 0), 0)),
                pl.BlockSpec((A_HALO, width), lambda i: (jnp.minimum((i + 1) * hb, last_hb), 0)))

    p_main, p_prev, p_next = rows(c5)
    d_main, _, d_next = rows(2 * c)
    a_main, _, a_next = rows(c)

    def full(shape):
        return pl.BlockSpec(shape, lambda i: (0, 0))

    def body(p_ref, pp_ref, pn_ref, a1_ref, a1n_ref, d_ref, dn_ref, wa_ref, lng_ref, lnb_ref, wb_ref,
             dp_ref, dwa_ref, dba_ref, dlng_ref, dlnb_ref, dwb_ref, gbuf, cbuf, da1buf, dcvbuf):
        i = pl.program_id(0)
        first = i == 0
        last = i == n_row - 1

        @pl.when(first)
        def _():
            dwa_ref[...] = jnp.zeros_like(dwa_ref)
            dba_ref[...] = jnp.zeros_like(dba_ref)
            dlng_ref[...] = jnp.zeros_like(dlng_ref)
            dlnb_ref[...] = jnp.zeros_like(dlnb_ref)
            dwb_ref[...] = jnp.zeros_like(dwb_ref)

        sig_gate = _sigmoid(p_ref[:, c:2 * c])
        gbuf[0:A_HALO, :] = jnp.where(first, 0.0, pp_ref[:, 0:c] * _sigmoid(pp_ref[:, c:2 * c]))
        gbuf[A_HALO:, :] = p_ref[:, 0:c] * sig_gate
        cbuf[0:A_HALO, :] = jnp.where(first, 0.0, pp_ref[:, 3 * c:4 * c] * pp_ref[:, 4 * c:5 * c])
        cbuf[A_HALO:, :] = p_ref[:, 3 * c:4 * c] * p_ref[:, 4 * c:5 * c]

        def ln_back(a1v, dav):
            xhat, rstd = _layernorm_stats(a1v)
            a2 = xhat * lng_ref[...] + lnb_ref[...]
            sg = _sigmoid(a2)
            da2 = dav * (sg * (1.0 + a2 * (1.0 - sg)))
            dxh = da2 * lng_ref[...]
            da1 = rstd * (dxh - jnp.mean(dxh, axis=-1, keepdims=True)
                          - xhat * jnp.mean(dxh * xhat, axis=-1, keepdims=True))
            return da1, da2, xhat

        da1_m, da2_m, xhat_m = ln_back(a1_ref[...], d_ref[:, 0:c])
        da1_n, _, _ = ln_back(a1n_ref[...], dn_ref[:, 0:c])
        da1buf[0:ts, :] = da1_m
        da1buf[ts:, :] = jnp.where(last, 0.0, da1_n)
        dlng_ref[...] += jnp.sum(da2_m * xhat_m, axis=0, keepdims=True)
        dlnb_ref[...] += jnp.sum(da2_m, axis=0, keepdims=True)
        dba_ref[...] += jnp.sum(da1_m, axis=0, keepdims=True)

        dglu = jnp.zeros((ts, c), F32)
        for k in range(ka):
            off = (ka - 1) - k
            dglu = dglu + da1buf[off:off + ts, :] * wa_ref[k:k + 1, :]
            goff = A_HALO - (ka - 1) + k
            dwa_ref[k:k + 1, :] += jnp.sum(da1_m * gbuf[goff:goff + ts, :], axis=0, keepdims=True)
        dp_ref[:, 0:c] = (dglu * sig_gate).astype(dp_ref.dtype)
        dp_ref[:, c:2 * c] = (dglu * p_ref[:, 0:c] * sig_gate * (1.0 - sig_gate)).astype(dp_ref.dtype)

        cv = jnp.zeros((ts, c), F32)
        for k in range(kb):
            off = A_HALO - (kb - 1) + k
            cv = cv + cbuf[off:off + ts, :] * wb_ref[k:k + 1, :]
        db_m = d_ref[:, c:2 * c]
        dp_ref[:, 2 * c:3 * c] = (db_m * cv).astype(dp_ref.dtype)
        dcv_m = db_m * p_ref[:, 2 * c:3 * c]
        dcvbuf[0:ts, :] = dcv_m
        dcvbuf[ts:, :] = jnp.where(last, 0.0, dn_ref[:, c:2 * c] * pn_ref[:, 2 * c:3 * c])
        dbc = jnp.zeros((ts, c), F32)
        for k in range(kb):
            off = (kb - 1) - k
            dbc = dbc + dcvbuf[off:off + ts, :] * wb_ref[k:k + 1, :]
            coff = A_HALO - (kb - 1) + k
            dwb_ref[k:k + 1, :] += jnp.sum(dcv_m * cbuf[coff:coff + ts, :], axis=0, keepdims=True)
        dp_ref[:, 3 * c:4 * c] = (dbc * p_ref[:, 4 * c:5 * c]).astype(dp_ref.dtype)
        dp_ref[:, 4 * c:5 * c] = (dbc * p_ref[:, 3 * c:4 * c]).astype(dp_ref.dtype)

    return _carrying_call(
        body,
        cargo,
        name=name,
        grid=(n_row,),
        in_specs=[p_main, p_prev, p_next, a_main, a_next, d_main, d_next,
                  full(wa.shape), full(lng.shape), full(lnb.shape), full(wb.shape)],
        out_specs=(pl.BlockSpec((ts, c5), lambda i: (i, 0)), full((32, c)), full((1, c)), full((1, c)),
                   full((1, c)), full((8, c))),
        out_shape=(
            jax.ShapeDtypeStruct((s, c5), BF16), jax.ShapeDtypeStruct((32, c), F32),
            jax.ShapeDtypeStruct((1, c), F32), jax.ShapeDtypeStruct((1, c), F32),
            jax.ShapeDtypeStruct((1, c), F32), jax.ShapeDtypeStruct((8, c), F32),
        ),
        scratch_shapes=[
            pltpu.VMEM((A_HALO + ts, c), F32), pltpu.VMEM((A_HALO + ts, c), F32),
            pltpu.VMEM((ext, c), F32), pltpu.VMEM((ext, c), F32),
        ],
        operands=(p, p, p, a1, a1, dab, dab, wa, lng, lnb, wb),
        compiler_params=_params(("arbitrary",)),
    )


def _head_pair_geometry(width, dh):
    assert 2 * dh == LANES, "the attention kernels pair two heads in one 128-lane block"
    return width // (3 * LANES)


def _group_sum(v, lo):
    s_lo = jnp.sum(jnp.where(lo, v, 0.0), axis=-1, keepdims=True)
    s_hi = jnp.sum(jnp.where(lo, 0.0, v), axis=-1, keepdims=True)
    return jnp.where(lo, s_lo, s_hi)


def _qkv_prep_fwd(qkv, gains, dh, *, name):
    s, width = qkv.shape
    nsec = _head_pair_geometry(width, dh)
    tr = _tile(s, (512, 256, 128))
    blk = pl.BlockSpec((tr, nsec * LANES), lambda i, sec: (i, sec))

    def body(x_ref, g_ref, o_ref):
        sec = pl.program_id(1)

        @pl.when(sec == 2)
        def _():
            o_ref[...] = x_ref[...].astype(o_ref.dtype)

        @pl.when(sec != 2)
        def _():
            lo = lax.broadcasted_iota(jnp.int32, (1, LANES), 1) < dh
            for lb in range(nsec):
                ls = pl.ds(lb * LANES, LANES)
                xv = x_ref[:, ls]
                rs = lax.rsqrt(_group_sum(xv * xv, lo) * (1.0 / dh) + EPS)
                o_ref[:, ls] = (xv * rs * g_ref[0:1, :]).astype(o_ref.dtype)

    return pl.pallas_call(
        body,
        name=name,
        grid=(s // tr, 3),
        in_specs=[blk, pl.BlockSpec((8, LANES), lambda i, sec: (sec, 0))],
        out_specs=blk,
        out_shape=jax.ShapeDtypeStruct((s, width), BF16),
        compiler_params=_params(("parallel", "parallel")),
    )(qkv, gains)


def _qkv_prep_bwd(qkv, dq, dk, dv, gains, dh, *, name):
    s, width = qkv.shape
    nsec = _head_pair_geometry(width, dh)
    tr = _tile(s, (512, 256, 128))
    blk = pl.BlockSpec((tr, nsec * LANES), lambda sec, i: (i, sec))
    gspec = pl.BlockSpec((8, LANES), lambda sec, i: (sec, 0))

    def grad_spec(own):
        return pl.BlockSpec((tr, nsec * LANES), lambda sec, i: (jnp.where(sec == own, i, 0), 0))

    def body(x_ref, dq_ref, dk_ref, dv_ref, g_ref, o_ref, dg_ref):
        sec = pl.program_id(0)

        @pl.when(pl.program_id(1) == 0)
        def _():
            dg_ref[...] = jnp.zeros_like(dg_ref)

        def normed(d_ref):
            lo = lax.broadcasted_iota(jnp.int32, (1, LANES), 1) < dh
            dg = jnp.zeros((1, LANES), F32)
            for lb in range(nsec):
                ls = pl.ds(lb * LANES, LANES)
                xv = x_ref[:, ls]
                d = d_ref[:, ls]
                rs = lax.rsqrt(_group_sum(xv * xv, lo) * (1.0 / dh) + EPS)
                xn = xv * rs
                dyg = d * g_ref[0:1, :]
                dx = rs * (dyg - xn * (_group_sum(dyg * xn, lo) * (1.0 / dh)))
                o_ref[:, ls] = dx.astype(o_ref.dtype)
                dg = dg + jnp.sum(d * xn, axis=0, keepdims=True)
            dg_ref[...] += jnp.broadcast_to(dg, dg_ref.shape)

        @pl.when(sec == 0)
        def _():
            normed(dq_ref)

        @pl.when(sec == 1)
        def _():
            normed(dk_ref)

        @pl.when(sec == 2)
        def _():
            o_ref[...] = dv_ref[...].astype(o_ref.dtype)

    return pl.pallas_call(
        body,
        name=name,
        grid=(3, s // tr),
        in_specs=[blk, grad_spec(0), grad_spec(1), grad_spec(2), gspec],
        out_specs=(blk, gspec),
        out_shape=(jax.ShapeDtypeStruct((s, width), BF16), jax.ShapeDtypeStruct((24, LANES), F32)),
        compiler_params=_params(("arbitrary", "arbitrary")),
    )(qkv, dq, dk, dv, gains)


def _split_dot(v, tri):
    hi = v.astype(BF16)
    lo = (v - hi.astype(F32)).astype(BF16)
    return _dot(hi, tri, 1, 0) + _dot(lo, tri, 1, 0)


def _attn_scores(qm, kb, tri_gt, valid, r_run):
    z = _dot(qm, kb, 1, 1)
    zl = jnp.minimum(z, 0.0) - jnp.log(1.0 + jnp.exp(-jnp.abs(z)))
    lk = zl - z
    if valid is not None:
        lk = jnp.where(valid, lk, 0.0)
    after = _split_dot(lk, tri_gt)
    w = jnp.exp(zl + after + r_run)
    if valid is not None:
        w = jnp.where(valid, w, 0.0)
    return zl, lk, w


def _attn_alive(rr):
    return jnp.max(jnp.maximum(rr[0], rr[1])) > ATTN_DEAD_LOG_WEIGHT


def _attn_walk_variants(i, walk):
    @pl.when(i == 0)
    def _():
        walk(False)

    @pl.when(i > 0)
    def _():
        walk(True)


def _attn_specs(s, width, dh, tq):
    nsec = _head_pair_geometry(width, dh)
    qspec = pl.BlockSpec((tq, LANES), lambda h, i: (i, h))
    kspec = pl.BlockSpec((s, LANES), lambda h, i: (0, nsec + h))
    vspec = pl.BlockSpec((s, LANES), lambda h, i: (0, 2 * nsec + h))
    return nsec, qspec, kspec, vspec


def _attn_masks(tq, tk, dh):
    lo = lax.broadcasted_iota(jnp.int32, (1, LANES), 1) < dh
    r_io = lax.broadcasted_iota(jnp.int32, (tk, tk), 0)
    c_io = lax.broadcasted_iota(jnp.int32, (tk, tk), 1)
    qrow = lax.broadcasted_iota(jnp.int32, (tq, tk), 0)
    kcol = lax.broadcasted_iota(jnp.int32, (tq, tk), 1)
    return lo, r_io, c_io, qrow, kcol


def _attn_fwd(qkv16, dh, *, gather_cargo=(), name):
    s, width = qkv16.shape
    tq = _tile(s, (ATTN_Q_BLOCK, ATTN_K_BLOCK))
    tk = _tile(tq, (ATTN_K_BLOCK,))
    nd = tq // tk
    nsec, qspec, kspec, vspec = _attn_specs(s, width, dh, tq)

    def body(q_ref, k_ref, v_ref, o_ref):
        i = pl.program_id(1)
        lo, r_io, c_io, qrow, kcol = _attn_masks(tq, tk, dh)
        q2 = q_ref[...]
        zq = jnp.zeros_like(q2)
        qs = (jnp.where(lo, q2, zq), jnp.where(lo, zq, q2))
        tri_gt = (r_io > c_io).astype(BF16)

        def block(j, carry, diag):
            rr, acc = carry
            start = pl.multiple_of(j * tk, tk)
            kb = k_ref[pl.ds(start, tk), :]
            vb = v_ref[pl.ds(start, tk), :]
            valid = None if diag is None else kcol + diag * tk < qrow
            outs, new_r = [], []
            for hh in range(2):
                _, lk, w = _attn_scores(qs[hh], kb, tri_gt, valid, rr[hh])
                outs.append(_dot(w.astype(BF16), vb, 1, 0))
                new_r.append(rr[hh] + jnp.sum(lk, axis=-1, keepdims=True))
            return tuple(new_r), acc + jnp.where(lo, outs[0], outs[1])

        below = i * nd

        def step(st):
            jj, _, cr = st
            cr = block(below - 1 - jj, cr, None)
            return jj + 1, _attn_alive(cr[0]), cr

        def walk(inline_first_below):
            zero = jnp.zeros((tq, 1), F32)
            carry = ((zero, zero), jnp.zeros((tq, LANES), F32))
            for dg in reversed(range(nd)):
                carry = block(i * nd + dg, carry, dg)
            if inline_first_below:
                carry = block(below - 1, carry, None)
            _, _, carry = lax.while_loop(lambda st: (st[0] < below) & st[1], step,
                                         (jnp.int32(int(inline_first_below)), _attn_alive(carry[0]), carry))
            o_ref[...] = carry[1]

        _attn_walk_variants(i, walk)

    outs, gathered = _carrying_call(
        body,
        gather_cargo,
        gather=True,
        name=name,
        grid=(nsec, s // tq),
        in_specs=[qspec, kspec, vspec],
        out_specs=[qspec],
        out_shape=[jax.ShapeDtypeStruct((s, width // 3), F32)],
        scratch_shapes=[],
        operands=(qkv16, qkv16, qkv16),
        compiler_params=_params(("arbitrary", "arbitrary")),
    )
    return outs[0], gathered


def _attn_bwd(qkv16, o, do, dh, *, cargo=(), name):
    s, width = qkv16.shape
    tq = _tile(s, (ATTN_Q_BLOCK, ATTN_K_BLOCK))
    tk = _tile(tq, (ATTN_K_BLOCK,))
    nd = tq // tk
    nsec, qspec, kspec, vspec = _attn_specs(s, width, dh, tq)
    accspec = pl.BlockSpec((s, LANES), lambda h, i: (0, h))

    def body(q_ref, k_ref, v_ref, o_ref, do_ref, dq_ref, dk_ref, dv_ref):
        i = pl.program_id(1)

        @pl.when(i == 0)
        def _():
            dk_ref[...] = jnp.zeros_like(dk_ref)
            dv_ref[...] = jnp.zeros_like(dv_ref)

        lo, r_io, c_io, qrow, kcol = _attn_masks(tq, tk, dh)
        q2 = q_ref[...]
        zq = jnp.zeros_like(q2)
        qs = (jnp.where(lo, q2, zq), jnp.where(lo, zq, q2))
        do16 = do_ref[...].astype(BF16)
        dos = (jnp.where(lo, do16, zq), jnp.where(lo, zq, do16))
        prod = do16.astype(F32) * o_ref[...]
        deltas = (jnp.sum(jnp.where(lo, prod, 0.0), axis=-1, keepdims=True),
                  jnp.sum(jnp.where(lo, 0.0, prod), axis=-1, keepdims=True))
        tri_gt = (r_io > c_io).astype(BF16)
        tri_ge = (r_io >= c_io).astype(BF16)

        def block(j, carry, diag):
            rr, er, dq = carry
            start = pl.multiple_of(j * tk, tk)
            kb = k_ref[pl.ds(start, tk), :]
            vb = v_ref[pl.ds(start, tk), :]
            valid = None if diag is None else kcol + diag * tk < qrow
            new_r, new_e, dqs = [], [], []
            dk_blk = dv_blk = None
            for hh in range(2):
                zl, lk, w = _attn_scores(qs[hh], kb, tri_gt, valid, rr[hh])
                beta = jnp.exp(zl)
                w16 = w.astype(BF16)
                e = _dot(dos[hh], vb, 1, 1) * w16.astype(F32)
                e_before = deltas[hh] - er[hh] - _split_dot(e, tri_ge)
                dz = e - beta * (e + e_before)
                if valid is not None:
                    dz = jnp.where(valid, dz, 0.0)
                dz16 = dz.astype(BF16)
                dqs.append(_dot(dz16, kb, 1, 0))
                dkc = _dot(dz16, qs[hh], 0, 0)
                dvc = _dot(w16, dos[hh], 0, 0)
                dk_blk = dkc if dk_blk is None else dk_blk + dkc
                dv_blk = dvc if dv_blk is None else dv_blk + dvc
                new_r.append(rr[hh] + jnp.sum(lk, axis=-1, keepdims=True))
                new_e.append(er[hh] + jnp.sum(e, axis=-1, keepdims=True))
            dk_ref[pl.ds(start, tk), :] += dk_blk
            dv_ref[pl.ds(start, tk), :] += dv_blk
            return tuple(new_r), tuple(new_e), dq + jnp.where(lo, dqs[0], dqs[1])

        below = i * nd

        def step(st):
            jj, _, cr = st
            cr = block(below - 1 - jj, cr, None)
            return jj + 1, _attn_alive(cr[0]), cr

        def walk(inline_first_below):
            zero = jnp.zeros((tq, 1), F32)
            carry = ((zero, zero), (zero, zero), jnp.zeros((tq, LANES), F32))
            for dg in reversed(range(nd)):
                carry = block(i * nd + dg, carry, dg)
            if inline_first_below:
                carry = block(below - 1, carry, None)
            _, _, carry = lax.while_loop(lambda st: (st[0] < below) & st[1], step,
                                         (jnp.int32(int(inline_first_below)), _attn_alive(carry[0]), carry))
            dq_ref[...] = carry[2]

        _attn_walk_variants(i, walk)

    shp = jax.ShapeDtypeStruct((s, width // 3), F32)
    return _carrying_call(
        body,
        cargo,
        name=name,
        grid=(nsec, s // tq),
        in_specs=[qspec, kspec, vspec, qspec, qspec],
        out_specs=(qspec, accspec, accspec),
        out_shape=(shp, shp, shp),
        scratch_shapes=[],
        operands=(qkv16, qkv16, qkv16, o, do),
        compiler_params=_params(("arbitrary", "arbitrary")),
    )


def _adam_update(wv, gv, mv, vv):
    c1 = 1.0 - ADAM_B1 ** ADAM_STEP
    c2 = 1.0 - ADAM_B2 ** ADAM_STEP
    nm = ADAM_B1 * mv + (1.0 - ADAM_B1) * gv
    nv = ADAM_B2 * vv + (1.0 - ADAM_B2) * (gv * gv)
    delta = -ADAM_LR * ((nm / c1) / (jnp.sqrt(nv / c2) + ADAM_EPS) + ADAM_WD * wv)
    return delta, nm, nv


def _sum_slabs(r, *, name):
    n, rows, lanes = r.shape
    tr = _tile(rows, (1024, 512, 256, 128, 64, 32, 16))

    def body(r_ref, o_ref):
        acc = r_ref[0]
        for d in range(1, n):
            acc = acc + r_ref[d]
        o_ref[...] = acc

    return pl.pallas_call(
        body,
        name=name,
        grid=(rows // tr,),
        in_specs=[pl.BlockSpec((n, tr, lanes), lambda i: (0, i, 0))],
        out_specs=pl.BlockSpec((tr, lanes), lambda i: (i, 0)),
        out_shape=jax.ShapeDtypeStruct((rows, lanes), F32),
        compiler_params=_params(("parallel",)),
    )(r)


def _adamw(w, g, m, v, *, name):
    rows, cols = w.shape
    tr = _tile(rows, (1024, 512, 256, 128, 64, 32, 16))
    spec = pl.BlockSpec((tr, cols), lambda i: (i, 0))

    def body(w_ref, g_ref, m_ref, v_ref, d_ref, nm_ref, nv_ref):
        d_ref[...], nm_ref[...], nv_ref[...] = _adam_update(w_ref[...], g_ref[...], m_ref[...], v_ref[...])

    shp = jax.ShapeDtypeStruct((rows, cols), F32)
    return pl.pallas_call(
        body,
        name=name,
        grid=(rows // tr,),
        in_specs=[spec, spec, spec, spec],
        out_specs=(spec, spec, spec),
        out_shape=(shp, shp, shp),
        compiler_params=_params(("parallel",)),
    )(w, g, m, v)


def _adamw_sum(w, received, m, v, *, name):
    layers, rows, cols = w.shape
    n = received[0].shape[0]
    tr = _tile(rows, (256, 128, 64, 32, 16))
    spec = pl.BlockSpec((1, tr, cols), lambda li, i: (li, i, 0))

    def partial_spec(own):
        return pl.BlockSpec((n, tr, cols), lambda li, i: (0, jnp.where(li == own, i, 0), 0))

    def body(w_ref, *refs):
        r_refs, (m_ref, v_ref, g_ref, d_ref, nm_ref, nv_ref) = refs[:layers], refs[layers:]
        for own in range(layers):
            @pl.when(pl.program_id(0) == own)
            def _(r_ref=r_refs[own]):
                gv = r_ref[0].astype(F32)
                for d in range(1, n):
                    gv = gv + r_ref[d].astype(F32)
                g_ref[0] = gv
                d_ref[0], nm_ref[0], nv_ref[0] = _adam_update(w_ref[0], gv, m_ref[0], v_ref[0])

    shp = jax.ShapeDtypeStruct((layers, rows, cols), F32)
    return pl.pallas_call(
        body,
        name=name,
        grid=(layers, rows // tr),
        in_specs=[spec] + [partial_spec(own) for own in range(layers)] + [spec, spec],
        out_specs=(spec, spec, spec, spec),
        out_shape=(shp, shp, shp, shp),
        compiler_params=_params(("arbitrary", "arbitrary")),
    )(w, *received, m, v)


_MATMUL_WEIGHTS = (("conv_w_in", 2), ("conv_w_out", 1), ("attn_w_qkv", 2), ("attn_w_o", 1), ("ffn_w_up", 2),
                   ("ffn_w_down", 1))
_TAP_WEIGHTS = (("conv_a_dw_w", 2), ("conv_b_dw_w", 2), ("ffn_dw_w", 2))
_REPLICATED = ("mix_norm_g", "ffn_norm_g", "conv_a_dw_b", "conv_a_ln_g", "conv_a_ln_b", "attn_q_g", "attn_k_g",
               "ffn_dw_b")
_WEIGHT_ORDER = ("mix_norm_g", "ffn_norm_g", "conv_w_in", "conv_a_dw_w", "conv_a_dw_b", "conv_a_ln_g",
                 "conv_a_ln_b", "conv_b_dw_w", "conv_w_out", "attn_w_qkv", "attn_q_g", "attn_k_g", "attn_w_o",
                 "ffn_w_up", "ffn_dw_w", "ffn_dw_b", "ffn_w_down")


def _assemble(gathered, axis):
    n, l, a, b = gathered.shape
    if axis == 2:
        return jnp.transpose(gathered, (1, 2, 0, 3)).reshape(l, a, n * b)
    return jnp.transpose(gathered, (1, 0, 2, 3)).reshape(l, n * a, b)


def _split_shards(full, axis):
    l, a, b = full.shape
    if axis == 2:
        return jnp.transpose(full.reshape(l, a, N_DEV, b // N_DEV), (2, 0, 1, 3))
    return jnp.transpose(full.reshape(l, N_DEV, a // N_DEV, b), (1, 0, 2, 3))


def kernel(x, mix_norm_g, ffn_norm_g, conv_w_in, conv_a_dw_w, conv_a_dw_b, conv_a_ln_g, conv_a_ln_b, conv_b_dw_w, conv_w_out, attn_w_qkv, attn_q_g, attn_k_g, attn_w_o, ffn_w_up, ffn_dw_w, ffn_dw_b, ffn_w_down, loss_target, m_mix_norm_g, m_ffn_norm_g, m_conv_w_in, m_conv_a_dw_w, m_conv_a_dw_b, m_conv_a_ln_g, m_conv_a_ln_b, m_conv_b_dw_w, m_conv_w_out, m_attn_w_qkv, m_attn_q_g, m_attn_k_g, m_attn_w_o, m_ffn_w_up, m_ffn_dw_w, m_ffn_dw_b, m_ffn_w_down, v_mix_norm_g, v_ffn_norm_g, v_conv_w_in, v_conv_a_dw_w, v_conv_a_dw_b, v_conv_a_ln_g, v_conv_a_ln_b, v_conv_b_dw_w, v_conv_w_out, v_attn_w_qkv, v_attn_q_g, v_attn_k_g, v_attn_w_o, v_ffn_w_up, v_ffn_dw_w, v_ffn_dw_b, v_ffn_w_down):
    args = dict(locals())
    weights = {n: args[n] for n in _WEIGHT_ORDER}
    mom_m = {n: args["m_" + n] for n in _WEIGHT_ORDER}
    mom_v = {n: args["v_" + n] for n in _WEIGHT_ORDER}

    _, s, d = x.shape
    depth = mix_norm_g.shape[0]
    dh = attn_q_g.shape[1]
    x0 = x.reshape(s, d)
    target = loss_target.reshape(s, d)

    shard_axis = dict(_MATMUL_WEIGHTS)
    shard16 = {n: weights[n].astype(BF16) for n, _ in _MATMUL_WEIGHTS}
    full = {}

    def layer_keys(layer):
        if layer >= depth:
            return []
        mixer = ("conv_w_in", "conv_w_out") if layer % 2 == 0 else ("attn_w_qkv", "attn_w_o")
        return [(mixer[0], layer // 2), (mixer[1], layer // 2), ("ffn_w_up", layer), ("ffn_w_down", layer)]

    def shards(keys):
        return [shard16[n][idx] for n, idx in keys]

    def assemble(keys, gathered):
        for (n, idx), g in zip(keys, gathered):
            if shard_axis[n] == 2:
                full[(n, idx)] = _assemble_cols(g, name=f"assemble_{n}_{idx}")
            else:
                full[(n, idx)] = g.reshape(N_DEV * g.shape[1], g.shape[2])

    mixer_in0, mixer_out0, ffn_up0, ffn_down0 = layer_keys(0)
    up_front = [mixer_in0, mixer_out0, ffn_down0]
    assemble(up_front, _all_gather(shards(up_front), "gather_first_weights"))
    tap_shapes = [weights[n].shape for n, _ in _TAP_WEIGHTS]
    g_tap = _all_gather([_pack([weights[n] for n, _ in _TAP_WEIGHTS], F32)], "gather_tap_weights")[0]
    for (n, axis), part in zip(_TAP_WEIGHTS, _unpack(g_tap, tap_shapes, lead=(N_DEV,))):
        full[n] = _assemble(part, axis)
    f = ffn_w_down.shape[1] * N_DEV

    saved = []
    xs = x0
    hn = _rmsnorm_fwd(xs, mix_norm_g[0:1], name="mix_norm_fwd_0")
    for layer in range(depth):
        i = layer // 2
        nxt = layer_keys(layer + 1)
        sv = {"x_in": xs, "h": hn}
        if layer % 2 == 0:
            on_in, on_act = ([], nxt[0:1]) if layer > 0 else (nxt[0:1], [ffn_up0])
            p, got = _matmul(hn, full[("conv_w_in", i)], gather_cargo=shards(on_in), name=f"conv_in_{layer}")
            assemble(on_in, got)
            (ab, a1), got = _conv_act_fwd(p, full["conv_a_dw_w"][i], conv_a_dw_b[i:i + 1], conv_a_ln_g[i:i + 1],
                                          conv_a_ln_b[i:i + 1], full["conv_b_dw_w"][i], gather_cargo=shards(on_act),
                                          name=f"conv_act_fwd_{layer}")
            assemble(on_act, got)
            sv.update(p=p, a1=a1, mix=ab)
            (xs, h2), _ = _matmul_add_norm(ab, full[("conv_w_out", i)], xs, ffn_norm_g[layer:layer + 1],
                                           name=f"conv_out_{layer}")
            riders = [nxt[1:2], nxt[2:3], nxt[3:4]]
        else:
            reps = LANES // dh
            gains = jnp.concatenate([
                jnp.broadcast_to(jnp.tile(attn_q_g[i], reps) * dh ** -0.5, (8, LANES)),
                jnp.broadcast_to(jnp.tile(attn_k_g[i], reps), (8, LANES)),
                jnp.ones((8, LANES), F32)], axis=0)
            qkv = _matmul(hn, full[("attn_w_qkv", i)], name=f"attn_qkv_{layer}")
            qkv16 = _qkv_prep_fwd(qkv, gains, dh, name=f"qkv_prep_fwd_{layer}")
            o, got = _attn_fwd(qkv16, dh, gather_cargo=shards(nxt), name=f"attn_fwd_{layer}")
            assemble(nxt, got)
            sv.update(qkv=qkv, qkv16=qkv16, gains=gains, mix=o)
            (xs, h2), _ = _matmul_add_norm(o, full[("attn_w_o", i)], xs, ffn_norm_g[layer:layer + 1],
                                           name=f"attn_out_{layer}")
            riders = [[], [], []]
        sv["x_mid"] = xs
        w_up = full[("ffn_w_up", layer)]
        ug, got = _matmul(h2, w_up, b_half=0, out_dtype=BF16, gather_cargo=shards(riders[0]),
                          name=f"ffn_up_gate_{layer}")
        assemble(riders[0], got)
        uv = _matmul(h2, w_up, b_half=1, out_dtype=BF16, name=f"ffn_up_val_{layer}")
        taps = full["ffn_dw_w"][layer]
        bias = ffn_dw_b[layer:layer + 1]
        act, got = _ffn_act_fwd(ug, uv, taps[:, :f], taps[:, f:], bias[:, :f], bias[:, f:],
                                gather_cargo=shards(riders[1]), name=f"ffn_act_fwd_{layer}")
        assemble(riders[1], got)
        sv.update(h2=h2, ug=ug, uv=uv, act=act)
        if layer + 1 < depth:
            (xs, hn), got = _matmul_add_norm(act, full[("ffn_w_down", layer)], xs, mix_norm_g[layer + 1:layer + 2],
                                             gather_cargo=shards(riders[2]), name=f"ffn_down_{layer}")
        else:
            xs, got = _matmul(act, full[("ffn_w_down", layer)], add=xs, gather_cargo=shards(riders[2]),
                              name=f"ffn_down_{layer}")
        assemble(riders[2], got)
        saved.append(sv)

    sq, dx, dx16 = _loss_head(xs, target, name="loss_head")
    loss = lax.psum(0.5 * sq[0, 0] / d, ("x", "y", "c"))

    grads = {n: [None] * weights[n].shape[0] for n in _WEIGHT_ORDER}
    queued, arrived = [], {}

    def queue(n, idx, g):
        a, b = weights[n].shape[1:]
        slabs = _split_cols(g, name=f"split_{n}_{idx}") if shard_axis[n] == 2 else g.reshape(N_DEV, a, b)
        queued.append(((n, idx), slabs))

    def carried(call):
        keys = [k for k, _ in queued]
        slabs = [sl for _, sl in queued]
        queued.clear()
        outs, received = call(slabs)
        arrived.update(zip(keys, received))
        return outs

    for layer in reversed(range(depth)):
        i = layer // 2
        sv = saved[layer]
        w_up = full[("ffn_w_up", layer)]
        taps = full["ffn_dw_w"][layer]
        bias = ffn_dw_b[layer:layer + 1]
        dact = _matmul(dx16, full[("ffn_w_down", layer)], tb=True, out_dtype=BF16, name=f"ffn_down_dx_{layer}")
        queue("ffn_w_down", layer, _weight_grad(sv["act"], dx16, name=f"ffn_down_dw_{layer}"))
        dug, duv, sumg, sumv = carried(lambda cargo: _ffn_act_bwd(
            sv["ug"], sv["uv"], dact, taps[:, :f], taps[:, f:], bias[:, :f], bias[:, f:], cargo=cargo,
            name=f"ffn_act_bwd_{layer}"))
        sums = jnp.concatenate([sumg, sumv], axis=1).reshape(4, 8, 2 * f).sum(axis=1)
        grads["ffn_dw_w"][layer] = sums[:3]
        grads["ffn_dw_b"][layer] = sums[3]
        dh2 = _matmul(dug, w_up, tb=True, b_half=0, name=f"ffn_up_gate_dx_{layer}")
        (dx, dx16, dg), _ = _matmul_norm_bwd(duv, w_up, sv["x_mid"], ffn_norm_g[layer:layer + 1], dx, b_half=1,
                                             add=dh2, name=f"ffn_up_val_dx_{layer}")
        grads["ffn_norm_g"][layer] = dg[0]
        queue("ffn_w_up", layer, jnp.concatenate(
            [_weight_grad(sv["h2"], dug, name=f"ffn_up_gate_dw_{layer}"),
             _weight_grad(sv["h2"], duv, name=f"ffn_up_val_dw_{layer}")], axis=1))

        if layer % 2 == 0:
            dab = _matmul(dx16, full[("conv_w_out", i)], tb=True, name=f"conv_out_dx_{layer}")
            queue("conv_w_out", i, _weight_grad(sv["mix"], dx16, name=f"conv_out_dw_{layer}"))
            ka = full["conv_a_dw_w"].shape[1]
            kb = full["conv_b_dw_w"].shape[1]
            dp, dwa, dba, dlng, dlnb, dwb = carried(lambda cargo: _conv_act_bwd(
                sv["p"], sv["a1"], dab, full["conv_a_dw_w"][i], conv_a_ln_g[i:i + 1], conv_a_ln_b[i:i + 1],
                full["conv_b_dw_w"][i], cargo=cargo, name=f"conv_act_bwd_{layer}"))
            grads["conv_a_dw_w"][i] = dwa[:ka]
            grads["conv_a_dw_b"][i] = dba[0]
            grads["conv_a_ln_g"][i] = dlng[0]
            grads["conv_a_ln_b"][i] = dlnb[0]
            grads["conv_b_dw_w"][i] = dwb[:kb]
            queue("conv_w_in", i, _weight_grad(sv["h"], dp, name=f"conv_in_dw_{layer}"))
            dy, w_in = dp, full[("conv_w_in", i)]
        else:
            do = _matmul(dx16, full[("attn_w_o", i)], tb=True, name=f"attn_out_dx_{layer}")
            queue("attn_w_o", i, _weight_grad(sv["mix"], dx16, name=f"attn_out_dw_{layer}"))
            dq, dk, dv = carried(lambda cargo: _attn_bwd(sv["qkv16"], sv["mix"], do, dh, cargo=cargo,
                                                         name=f"attn_bwd_{layer}"))
            dqkv, dgains = _qkv_prep_bwd(sv["qkv"], dq, dk, dv, sv["gains"], dh, name=f"qkv_prep_bwd_{layer}")
            grads["attn_q_g"][i] = dgains[0].reshape(-1, dh).sum(axis=0) * dh ** -0.5
            grads["attn_k_g"][i] = dgains[8].reshape(-1, dh).sum(axis=0)
            queue("attn_w_qkv", i, _weight_grad(sv["h"], dqkv, name=f"attn_qkv_dw_{layer}"))
            dy, w_in = dqkv, full[("attn_w_qkv", i)]

        def mixer_in_bwd(cargo):
            return _matmul_norm_bwd(dy, w_in, sv["x_in"], mix_norm_g[layer:layer + 1], dx, cargo=cargo,
                                    name=f"mixer_in_dx_{layer}")

        dx, dx16, dg = carried(mixer_in_bwd) if layer == 0 else mixer_in_bwd(())[0]
        grads["mix_norm_g"][layer] = dg[0]

    me = _my_index()
    final_grads, delta, new_m, new_v = {}, {}, {}, {}
    for n, _ in _MATMUL_WEIGHTS:
        final_grads[n], delta[n], new_m[n], new_v[n] = _adamw_sum(
            weights[n], [arrived[(n, idx)] for idx in range(weights[n].shape[0])], mom_m[n], mom_v[n],
            name=f"adamw_{n}")

    small_names = list(_REPLICATED) + [n for n, _ in _TAP_WEIGHTS]
    local = {n: jnp.stack(grads[n], axis=0) for n in small_names}
    small_shapes = [local[n].shape for n in small_names]
    g_small = _sum_slabs(_all_gather([_pack([local[n] for n in small_names], F32)], "gather_small_grads")[0],
                         name="sum_small_grads")
    reduced = dict(zip(small_names, _unpack(g_small, small_shapes)))
    for n in _REPLICATED:
        final_grads[n] = reduced[n]
    for n, axis in _TAP_WEIGHTS:
        final_grads[n] = lax.dynamic_index_in_dim(_split_shards(reduced[n], axis), me, axis=0, keepdims=False)
    shapes = [weights[n].shape for n in small_names]
    d_s, m_s, v_s = _adamw(
        _pack([weights[n] for n in small_names], F32), _pack([final_grads[n] for n in small_names], F32),
        _pack([mom_m[n] for n in small_names], F32), _pack([mom_v[n] for n in small_names], F32), name="adamw_small")
    delta.update(zip(small_names, _unpack(d_s, shapes)))
    new_m.update(zip(small_names, _unpack(m_s, shapes)))
    new_v.update(zip(small_names, _unpack(v_s, shapes)))

    order = list(_WEIGHT_ORDER)
    return (loss, dx.reshape(x.shape), *[final_grads[n] for n in order], *[delta[n] for n in order],
            *[new_m[n] for n in order], *[new_v[n] for n in order])
```

```python
import jax
import jax.numpy as jnp
from jax import lax
from jax.experimental import pallas as pl
from jax.experimental.pallas import tpu as pltpu

F32 = jnp.float32
BF16 = jnp.bfloat16
EPS = 1e-6
N_DEV = 8
MESH_ID = pl.DeviceIdType.MESH

ADAM_LR = 0.001
ADAM_B1 = 0.9
ADAM_B2 = 0.999
ADAM_EPS = 1e-08
ADAM_WD = 0.01
ADAM_STEP = 10

V7X_VMEM_LIMIT_BYTES = 48 * 1024 * 1024
PACK_QUANTUM = 2048
A_HALO = 32
S_HALO = 8
FFN_HALO_BLOCK = 16
LANES = 128
ATTN_Q_BLOCK = 256
ATTN_K_BLOCK = 256
MATMUL_MAX_SINGLE_K = 3072
MXU_WIDTH = 256
FFN_CHUNK_ROWS = 128
ATTN_DEAD_LOG_WEIGHT = -110.0


def _tile(n, prefs):
    for p in prefs:
        if n % p == 0:
            return p
    return n


def _params(sem=None):
    return pltpu.CompilerParams(dimension_semantics=sem, vmem_limit_bytes=V7X_VMEM_LIMIT_BYTES)


def _sigmoid(x):
    return 1.0 / (1.0 + jnp.exp(-x))


def _dot(a, b, ca, cb):
    return lax.dot_general(a, b, (((ca,), (cb,)), ((), ())), preferred_element_type=F32)


def _my_index():
    return 4 * lax.axis_index("x") + 2 * lax.axis_index("y") + lax.axis_index("c")


def _all_gather(shards, name):
    n = len(shards)

    def body(*refs):
        x_refs, out_refs = refs[:n], refs[n:2 * n]
        send_sems, recv_sems, local_sems = refs[2 * n:]
        x, y, c = lax.axis_index("x"), lax.axis_index("y"), lax.axis_index("c")
        me, sibling = (x, y, c), (x, y, 1 - c)
        chips = [(1 - x, y), (x, 1 - y), (1 - x, 1 - y)]

        def copy(a, k, block, to, src=None):
            slot = out_refs[a].at[4 * block[0] + 2 * block[1] + block[2]]
            return pltpu.make_async_remote_copy(
                src_ref=slot if src is None else src, dst_ref=slot,
                send_sem=send_sems.at[7 * a + k], recv_sem=recv_sems.at[7 * a + k],
                device_id=to, device_id_type=MESH_ID)

        mine = [pltpu.make_async_copy(x_refs[a], out_refs[a].at[4 * x + 2 * y + c], local_sems.at[a])
                for a in range(n)]
        for cp in mine:
            cp.start()
        first = []
        for a in range(n):
            first.append(copy(a, 0, me, sibling, src=x_refs[a]))
            first += [copy(a, 1 + j, me, (*chip, c), src=x_refs[a]) for j, chip in enumerate(chips)]
        for cp in first:
            cp.start()
        passed = []
        for j, chip in enumerate(chips):
            for a in range(n):
                copy(a, 1 + j, (*chip, c), me).wait_recv()
                fwd = copy(a, 4 + j, (*chip, c), sibling)
                fwd.start()
                passed.append(fwd)
        for a in range(n):
            copy(a, 0, sibling, me).wait_recv()
            for j, chip in enumerate(chips):
                copy(a, 4 + j, (*chip, 1 - c), me).wait_recv()
        for cp in first + passed:
            cp.wait_send()
        for cp in mine:
            cp.wait()

    hbm = pl.BlockSpec(memory_space=pl.ANY)
    return pl.pallas_call(
        body,
        name=name,
        out_shape=[jax.ShapeDtypeStruct((N_DEV,) + sh.shape, sh.dtype) for sh in shards],
        in_specs=[hbm] * n,
        out_specs=[hbm] * n,
        scratch_shapes=[
            pltpu.SemaphoreType.DMA((7 * n,)),
            pltpu.SemaphoreType.DMA((7 * n,)),
            pltpu.SemaphoreType.DMA((n,)),
        ],
    )(*shards)


def _exchange_semaphores(n):
    return [pltpu.SemaphoreType.DMA((7 * n,)), pltpu.SemaphoreType.DMA((7 * n,)), pltpu.SemaphoreType.DMA((n,))]


def _exchange_copies(g_refs, out_refs, send_sems, recv_sems, local_sems, gather=False):
    n = len(g_refs)
    x, y, c = lax.axis_index("x"), lax.axis_index("y"), lax.axis_index("c")
    me = 4 * x + 2 * y + c

    def part(a, d):
        return g_refs[a] if gather else g_refs[a].at[d]

    local = [pltpu.make_async_copy(part(a, me), out_refs[a].at[me], local_sems.at[a]) for a in range(n)]
    remote = []
    for k in range(1, N_DEV):
        px = (x + ((k >> 2) & 1)) % 2
        py = (y + ((k >> 1) & 1)) % 2
        pc = (c + (k & 1)) % 2
        for a in range(n):
            remote.append(pltpu.make_async_remote_copy(
                src_ref=part(a, 4 * px + 2 * py + pc), dst_ref=out_refs[a].at[me],
                send_sem=send_sems.at[7 * a + k - 1], recv_sem=recv_sems.at[7 * a + k - 1],
                device_id=(px, py, pc), device_id_type=MESH_ID))
    return local, remote


def _exchange_start(copies):
    for cp in copies[0] + copies[1]:
        cp.start()


def _exchange_wait(copies):
    local, remote = copies
    for cp in remote:
        cp.wait_recv()
    for cp in remote:
        cp.wait_send()
    for cp in local:
        cp.wait()


def _carrying_call(body, cargo, *, name, grid, in_specs, out_specs, out_shape, scratch_shapes, operands,
                   compiler_params, gather=False):
    n = len(cargo)
    n_in, n_out, n_scr = len(in_specs), len(out_specs), len(scratch_shapes)
    if n == 0:
        return pl.pallas_call(body, name=name, grid=grid, in_specs=in_specs, out_specs=out_specs,
                              out_shape=out_shape, scratch_shapes=scratch_shapes,
                              compiler_params=compiler_params)(*operands), []

    def carrying(*refs):
        cuts = [0, n_in, n_in + n, n_in + n + n_out, n_in + 2 * n + n_out, n_in + 2 * n + n_out + n_scr, len(refs)]
        ins, g_refs, outs, r_refs, scr, sems = [refs[a:b] for a, b in zip(cuts[:-1], cuts[1:])]
        steps = [pl.program_id(ax) for ax in range(len(grid))]
        first = steps[0] == 0
        last = steps[0] == grid[0] - 1
        for ax in range(1, len(grid)):
            first = first & (steps[ax] == 0)
            last = last & (steps[ax] == grid[ax] - 1)

        @pl.when(first)
        def _():
            _exchange_start(_exchange_copies(g_refs, r_refs, *sems, gather=gather))

        body(*ins, *outs, *scr)

        @pl.when(last)
        def _():
            _exchange_wait(_exchange_copies(g_refs, r_refs, *sems, gather=gather))

    hbm = pl.BlockSpec(memory_space=pl.ANY)
    lead = (N_DEV,) if gather else ()
    res = pl.pallas_call(
        carrying,
        name=name,
        grid=grid,
        in_specs=list(in_specs) + [hbm] * n,
        out_specs=list(out_specs) + [hbm] * n,
        out_shape=list(out_shape) + [jax.ShapeDtypeStruct(lead + g.shape, g.dtype) for g in cargo],
        scratch_shapes=list(scratch_shapes) + _exchange_semaphores(n),
        compiler_params=compiler_params,
    )(*operands, *cargo)
    return res[:n_out], res[n_out:]


def _assemble_cols(g, *, name):
    n, r, w = g.shape
    tr = _tile(r, (256, 128, 64, 32, 16))

    def body(g_ref, o_ref):
        for j in range(n):
            o_ref[:, j * w:(j + 1) * w] = g_ref[j]

    return pl.pallas_call(
        body,
        name=name,
        grid=(r // tr,),
        in_specs=[pl.BlockSpec((n, tr, w), lambda i: (0, i, 0))],
        out_specs=pl.BlockSpec((tr, n * w), lambda i: (i, 0)),
        out_shape=jax.ShapeDtypeStruct((r, n * w), g.dtype),
        compiler_params=_params(("parallel",)),
    )(g)


def _split_cols(x, *, name):
    r, nw = x.shape
    w = nw // N_DEV
    tr = _tile(r, (256, 128, 64, 32, 16))

    def body(x_ref, o_ref):
        for j in range(N_DEV):
            o_ref[j] = x_ref[:, j * w:(j + 1) * w]

    return pl.pallas_call(
        body,
        name=name,
        grid=(r // tr,),
        in_specs=[pl.BlockSpec((tr, nw), lambda i: (i, 0))],
        out_specs=pl.BlockSpec((N_DEV, tr, w), lambda i: (0, i, 0)),
        out_shape=jax.ShapeDtypeStruct((N_DEV, r, w), x.dtype),
        compiler_params=_params(("parallel",)),
    )(x)


def _padded(n):
    return -(-n // PACK_QUANTUM) * PACK_QUANTUM


def _pack(parts, dtype, lead=()):
    flat = []
    for p in parts:
        p = p.astype(dtype).reshape(lead + (-1,))
        n = p.shape[-1]
        flat.append(jnp.pad(p, [(0, 0)] * len(lead) + [(0, _padded(n) - n)]))
    return jnp.concatenate(flat, axis=-1).reshape(lead + (-1, 128))


def _unpack(buf, shapes, lead=()):
    flat = buf.reshape(lead + (-1,))
    out, off = [], 0
    for shp in shapes:
        n = 1
        for d in shp:
            n *= d
        out.append(flat[..., off:off + n].reshape(lead + tuple(shp)))
        off += _padded(n)
    return out


def _half_if_tiled(n):
    half = n // 2
    return (half,) if n % 2 == 0 and half % LANES == 0 and half <= 1536 else ()


def _matmul(a, b, *, ta=False, tb=False, b_half=None, out_dtype=F32, add=None, gather_cargo=None, name):
    if ta:
        kdim, m = a.shape
    else:
        m, kdim = a.shape
    halves = 1 if b_half is None else 2
    n = b.shape[0] if tb else b.shape[1] // halves
    bm = _tile(m, (1024,) + _half_if_tiled(m) + (512, 256, 128))
    bn = _tile(n, (512, 256, 128))
    if bn < 512 and n % MXU_WIDTH == 0 and n <= MATMUL_MAX_SINGLE_K:
        bn, bm = n, _tile(m, (512, 256, 128))
    bk = kdim if kdim <= MATMUL_MAX_SINGLE_K else _tile(kdim, (1024, 512, 256, 128))
    nk = kdim // bk

    a_spec = pl.BlockSpec((bk, bm), lambda i, j, k: (k, i)) if ta else pl.BlockSpec((bm, bk), lambda i, j, k: (i, k))
    col0 = 0 if b_half is None else b_half * ((nk if tb else n // bn))
    b_spec = (pl.BlockSpec((bn, bk), lambda i, j, k: (j, k + col0)) if tb
              else pl.BlockSpec((bk, bn), lambda i, j, k: (k, j + col0)))
    o_spec = pl.BlockSpec((bm, bn), lambda i, j, k: (i, j))
    in_specs = [a_spec, b_spec] + ([o_spec] if add is not None else [])
    operands = [a, b] + ([add] if add is not None else [])

    def product(a_ref, b_ref):
        return _dot(a_ref[...].astype(BF16), b_ref[...].astype(BF16), 0 if ta else 1, 1 if tb else 0)

    def finish(r, refs):
        if add is not None:
            r = r + refs[2][...]
        return r

    def body_single(*refs):
        o_ref = refs[-1]
        o_ref[...] = finish(product(refs[0], refs[1]), refs).astype(o_ref.dtype)

    def body_multi(*refs):
        o_ref, acc_ref = refs[-2], refs[-1]
        k = pl.program_id(2)

        @pl.when(k == 0)
        def _():
            acc_ref[...] = jnp.zeros_like(acc_ref)

        acc_ref[...] += product(refs[0], refs[1])

        @pl.when(k == nk - 1)
        def _():
            o_ref[...] = finish(acc_ref[...], refs).astype(o_ref.dtype)

    outs, received = _carrying_call(
        body_single if nk == 1 else body_multi,
        () if gather_cargo is None else gather_cargo,
        gather=True,
        name=name,
        grid=(m // bm, n // bn, nk),
        in_specs=in_specs,
        out_specs=[o_spec],
        out_shape=[jax.ShapeDtypeStruct((m, n), out_dtype)],
        scratch_shapes=[] if nk == 1 else [pltpu.VMEM((bm, bn), F32)],
        operands=operands,
        compiler_params=_params(("parallel", "parallel", "arbitrary") if gather_cargo is None else ("arbitrary",) * 3),
    )
    return outs[0] if gather_cargo is None else (outs[0], received)


def _rows_matmul(a, b, *, tb, add, row_ins, vec_ins, epilogue, row_out_dtypes, n_sums, cargo, gather, name,
                 b_half=None):
    m, kdim = a.shape
    n = b.shape[0] if tb else b.shape[1]
    assert kdim <= MATMUL_MAX_SINGLE_K and (b_half is None or tb)
    b_spec = (pl.BlockSpec(b.shape, lambda i: (0, 0)) if b_half is None
              else pl.BlockSpec((n, kdim), lambda i: (0, b_half)))
    bm = _tile(m, (512, 256, 128))
    row = pl.BlockSpec((bm, n), lambda i: (i, 0))
    vec = pl.BlockSpec((1, n), lambda i: (0, 0))
    n_add = int(add is not None)
    n_ins = len(row_ins) + len(vec_ins)
    n_rows = len(row_out_dtypes)

    def body(a_ref, b_ref, *refs):
        y = _dot(a_ref[...].astype(BF16), b_ref[...].astype(BF16), 1, 1 if tb else 0)
        if add is not None:
            y = y + refs[0][...]
        ins = [r[...] for r in refs[n_add:n_add + n_ins]]
        outs = refs[n_add + n_ins:]
        rows, sums = epilogue(y, *ins)
        for ref, val in zip(outs[:n_rows], rows):
            ref[...] = val.astype(ref.dtype)
        if n_sums:
            @pl.when(pl.program_id(0) == 0)
            def _():
                for ref in outs[n_rows:]:
                    ref[...] = jnp.zeros_like(ref)

            for ref, val in zip(outs[n_rows:], sums):
                ref[...] += val

    return _carrying_call(
        body,
        cargo,
        gather=gather,
        name=name,
        grid=(m // bm,),
        in_specs=[pl.BlockSpec((bm, kdim), lambda i: (i, 0)), b_spec]
        + [row] * (n_add + len(row_ins)) + [vec] * len(vec_ins),
        out_specs=[row] * n_rows + [vec] * n_sums,
        out_shape=[jax.ShapeDtypeStruct((m, n), dt) for dt in row_out_dtypes]
        + [jax.ShapeDtypeStruct((1, n), F32)] * n_sums,
        scratch_shapes=[],
        operands=[a, b] + ([add] if add is not None else []) + list(row_ins) + list(vec_ins),
        compiler_params=_params(("arbitrary",)),
    )


def _matmul_add_norm(a, b, add, g, *, gather_cargo=(), name):
    def epilogue(y, gv):
        h = y * lax.rsqrt(jnp.mean(y * y, axis=-1, keepdims=True) + EPS) * gv
        return (y, h), ()

    return _rows_matmul(a, b, tb=False, add=add, row_ins=(), vec_ins=(g,), epilogue=epilogue,
                        row_out_dtypes=(F32, BF16), n_sums=0, cargo=gather_cargo, gather=True, name=name)


def _matmul_norm_bwd(a, b, x, g, res, *, b_half=None, add=None, cargo=(), name):
    def epilogue(dyv, xv, resv, gv):
        rs = lax.rsqrt(jnp.mean(xv * xv, axis=-1, keepdims=True) + EPS)
        xn = xv * rs
        dyg = dyv * gv
        dx = rs * (dyg - xn * jnp.mean(dyg * xn, axis=-1, keepdims=True)) + resv
        return (dx, dx), (jnp.sum(dyv * xn, axis=0, keepdims=True),)

    return _rows_matmul(a, b, tb=True, b_half=b_half, add=add, row_ins=(x, res), vec_ins=(g,), epilogue=epilogue,
                        row_out_dtypes=(F32, BF16), n_sums=1, cargo=cargo, gather=False, name=name)


def _weight_grad(x, dy, *, name):
    return _matmul(x, dy, ta=True, out_dtype=BF16, name=name)


def _rmsnorm_fwd(x, g, *, name):
    r, d = x.shape
    tr = _tile(r, (1024, 512, 256, 128))

    def body(x_ref, g_ref, o_ref):
        xv = x_ref[...]
        y = xv * lax.rsqrt(jnp.mean(xv * xv, axis=-1, keepdims=True) + EPS)
        y = y * g_ref[...]
        o_ref[...] = y.astype(o_ref.dtype)

    return pl.pallas_call(
        body,
        name=name,
        grid=(r // tr,),
        in_specs=[pl.BlockSpec((tr, d), lambda i: (i, 0)), pl.BlockSpec((1, d), lambda i: (0, 0))],
        out_specs=pl.BlockSpec((tr, d), lambda i: (i, 0)),
        out_shape=jax.ShapeDtypeStruct((r, d), BF16),
        compiler_params=_params(("parallel",)),
    )(x, g)


def _loss_head(y, target, *, name):
    s, d = y.shape
    ts = _tile(s, (512, 256, 128))
    row = pl.BlockSpec((ts, d), lambda i: (i, 0))
    acc = pl.BlockSpec((8, 128), lambda i: (0, 0))

    def body(y_ref, t_ref, l_ref, dy_ref, dy16_ref):
        e = y_ref[...] - t_ref[...]
        dy = e * (1.0 / d)
        dy_ref[...] = dy
        dy16_ref[...] = dy.astype(dy16_ref.dtype)

        @pl.when(pl.program_id(0) == 0)
        def _():
            l_ref[...] = jnp.zeros_like(l_ref)

        sq = jnp.sum(jnp.sum(e * e, axis=-1, keepdims=True), axis=0, keepdims=True)
        l_ref[...] += jnp.broadcast_to(sq, l_ref.shape)

    return pl.pallas_call(
        body,
        name=name,
        grid=(s // ts,),
        in_specs=[row, row],
        out_specs=(acc, row, row),
        out_shape=(jax.ShapeDtypeStruct((8, 128), F32), jax.ShapeDtypeStruct((s, d), F32),
                   jax.ShapeDtypeStruct((s, d), BF16)),
        compiler_params=_params(("arbitrary",)),
    )(y, target)


def _ffn_tiles(s, f):
    return _tile(s, (512, 256, 128)), _tile(f, _half_if_tiled(f) + (512, 256, 128))


def _conv3(xs, w, b):
    return b + xs[0] * w[0:1] + xs[1] * w[1:2] + xs[2] * w[2:3]


def _shifted3(x, rows):
    return [x[S_HALO - 2 + k:S_HALO - 2 + k + rows] for k in range(3)]


def _ffn_act_fwd(ug, uv, wg, wv, bg, bv, *, gather_cargo=(), name):
    s, f = ug.shape
    ts, tc = _ffn_tiles(s, f)
    rc = _tile(ts, (FFN_CHUNK_ROWS, 32, 16, 8))
    hb = ts // FFN_HALO_BLOCK
    main = pl.BlockSpec((ts, tc), lambda j, i: (i, j))
    prev = pl.BlockSpec((FFN_HALO_BLOCK, tc), lambda j, i: (jnp.maximum(i * hb - 1, 0), j))
    taps = pl.BlockSpec((3, tc), lambda j, i: (0, j))
    bias = pl.BlockSpec((1, tc), lambda j, i: (0, j))
    lead = FFN_HALO_BLOCK - S_HALO

    def body(ug_ref, ugp_ref, uv_ref, uvp_ref, wg_ref, wv_ref, bg_ref, bv_ref, o_ref, gbuf, vbuf):
        first = pl.program_id(1) == 0
        for buf, p_ref, m_ref in ((gbuf, ugp_ref, ug_ref), (vbuf, uvp_ref, uv_ref)):
            buf[0:FFN_HALO_BLOCK, :] = jnp.where(first, 0.0, p_ref[...].astype(F32))
            buf[FFN_HALO_BLOCK:, :] = m_ref[...].astype(F32)

        def chunk(ci, carry):
            r0 = pl.multiple_of(ci * rc, rc)
            rows = pl.ds(pl.multiple_of(r0 + lead, S_HALO), rc + S_HALO)
            for lb in range(tc // LANES):
                ls = pl.ds(lb * LANES, LANES)
                gate = _conv3(_shifted3(gbuf[rows, ls], rc), wg_ref[:, ls], bg_ref[:, ls])
                val = _conv3(_shifted3(vbuf[rows, ls], rc), wv_ref[:, ls], bv_ref[:, ls])
                o_ref[pl.ds(r0, rc), ls] = (gate * _sigmoid(gate) * val).astype(o_ref.dtype)
            return carry

        lax.fori_loop(0, ts // rc, chunk, 0)

    outs, gathered = _carrying_call(
        body,
        gather_cargo,
        gather=True,
        name=name,
        grid=(f // tc, s // ts),
        in_specs=[main, prev, main, prev, taps, taps, bias, bias],
        out_specs=[main],
        out_shape=[jax.ShapeDtypeStruct((s, f), BF16)],
        scratch_shapes=[pltpu.VMEM((FFN_HALO_BLOCK + ts, tc), F32), pltpu.VMEM((FFN_HALO_BLOCK + ts, tc), F32)],
        operands=(ug, ug, uv, uv, wg, wv, bg, bv),
        compiler_params=_params(("arbitrary", "arbitrary")),
    )
    return outs[0], gathered


def _ffn_act_bwd(ug, uv, df, wg, wv, bg, bv, *, cargo=(), name):
    s, f = ug.shape
    ts, tc = _ffn_tiles(s, f)
    rc = _tile(ts, (FFN_CHUNK_ROWS, 32, 16, 8))
    hb = ts // FFN_HALO_BLOCK
    last_hb = s // FFN_HALO_BLOCK - 1
    n_row = s // ts
    lead = FFN_HALO_BLOCK - S_HALO
    main = pl.BlockSpec((ts, tc), lambda j, i: (i, j))
    prev = pl.BlockSpec((FFN_HALO_BLOCK, tc), lambda j, i: (jnp.maximum(i * hb - 1, 0), j))
    nxt = pl.BlockSpec((FFN_HALO_BLOCK, tc), lambda j, i: (jnp.minimum((i + 1) * hb, last_hb), j))
    taps = pl.BlockSpec((3, tc), lambda j, i: (0, j))
    bias = pl.BlockSpec((1, tc), lambda j, i: (0, j))
    sums = pl.BlockSpec((32, tc), lambda j, i: (0, j))
    ext = rc + S_HALO

    def body(ug_ref, ugp_ref, ugn_ref, uv_ref, uvp_ref, uvn_ref, df_ref, dfn_ref, wg_ref, wv_ref, bg_ref, bv_ref,
             dug_ref, duv_ref, sumg_ref, sumv_ref, gbuf, vbuf, dfbuf):
        i = pl.program_id(1)
        first = i == 0
        last = i == n_row - 1

        @pl.when(first)
        def _():
            sumg_ref[...] = jnp.zeros_like(sumg_ref)
            sumv_ref[...] = jnp.zeros_like(sumv_ref)

        for buf, p_ref, m_ref, n_ref in ((gbuf, ugp_ref, ug_ref, ugn_ref), (vbuf, uvp_ref, uv_ref, uvn_ref)):
            buf[0:FFN_HALO_BLOCK, :] = jnp.where(first, 0.0, p_ref[...].astype(F32))
            buf[FFN_HALO_BLOCK:FFN_HALO_BLOCK + ts, :] = m_ref[...].astype(F32)
            buf[FFN_HALO_BLOCK + ts:, :] = n_ref[...].astype(F32)
        dfbuf[0:ts, :] = df_ref[...].astype(F32)
        dfbuf[ts:, :] = jnp.where(last, 0.0, dfn_ref[...].astype(F32))

        def chunk(ci, carry):
            r0 = pl.multiple_of(ci * rc, rc)
            rows = pl.ds(pl.multiple_of(r0 + lead, S_HALO), ext + S_HALO)
            for lb in range(tc // LANES):
                ls = pl.ds(lb * LANES, LANES)
                xg = _shifted3(gbuf[rows, ls], ext)
                xv = _shifted3(vbuf[rows, ls], ext)
                wgv, wvv = wg_ref[:, ls], wv_ref[:, ls]
                gate = _conv3(xg, wgv, bg_ref[:, ls])
                val = _conv3(xv, wvv, bv_ref[:, ls])
                dfe = dfbuf[pl.ds(r0, ext), ls]
                sg = _sigmoid(gate)
                dgate = dfe * val * (sg * (1.0 + gate * (1.0 - sg)))
                dval = dfe * (gate * sg)
                for dz, xs, w, du_ref, sum_ref in ((dgate, xg, wgv, dug_ref, sumg_ref),
                                                   (dval, xv, wvv, duv_ref, sumv_ref)):
                    du = dz[2:2 + rc] * w[0:1] + dz[1:1 + rc] * w[1:2] + dz[0:rc] * w[2:3]
                    du_ref[pl.ds(r0, rc), ls] = du.astype(du_ref.dtype)
                    dm = dz[0:rc]
                    for k in range(3):
                        sum_ref[8 * k:8 * k + 8, ls] += (dm * xs[k][0:rc]).reshape(rc // 8, 8, LANES).sum(axis=0)
                    sum_ref[24:32, ls] += dm.reshape(rc // 8, 8, LANES).sum(axis=0)
            return carry

        lax.fori_loop(0, ts // rc, chunk, 0)

    return _carrying_call(
        body,
        cargo,
        name=name,
        grid=(f // tc, n_row),
        in_specs=[main, prev, nxt, main, prev, nxt, main, nxt, taps, taps, bias, bias],
        out_specs=(main, main, sums, sums),
        out_shape=(jax.ShapeDtypeStruct((s, f), BF16), jax.ShapeDtypeStruct((s, f), BF16),
                   jax.ShapeDtypeStruct((32, f), F32), jax.ShapeDtypeStruct((32, f), F32)),
        scratch_shapes=[pltpu.VMEM((2 * FFN_HALO_BLOCK + ts, tc), F32), pltpu.VMEM((2 * FFN_HALO_BLOCK + ts, tc), F32),
                        pltpu.VMEM((FFN_HALO_BLOCK + ts, tc), F32)],
        operands=(ug, ug, ug, uv, uv, uv, df, df, wg, wv, bg, bv),
        compiler_params=_params(("arbitrary", "arbitrary")),
    )


def _layernorm_stats(a1):
    mu = jnp.mean(a1, axis=-1, keepdims=True)
    xc = a1 - mu
    rstd = lax.rsqrt(jnp.mean(xc * xc, axis=-1, keepdims=True) + EPS)
    return xc * rstd, rstd


def _conv_act_fwd(p, wa, ba, lng, lnb, wb, *, gather_cargo=(), name):
    s, c5 = p.shape
    c = c5 // 5
    ka, kb = wa.shape[0], wb.shape[0]
    ts = _tile(s, (256, 128))
    hb = ts // A_HALO
    main = pl.BlockSpec((ts, c5), lambda i: (i, 0))
    prev = pl.BlockSpec((A_HALO, c5), lambda i: (jnp.maximum(i * hb - 1, 0), 0))

    def full(arr):
        return pl.BlockSpec(arr.shape, lambda i: (0, 0))

    def body(p_ref, pp_ref, wa_ref, ba_ref, lng_ref, lnb_ref, wb_ref, ab_ref, a1_ref, gbuf, cbuf):
        first = pl.program_id(0) == 0
        gbuf[0:A_HALO, :] = jnp.where(first, 0.0, pp_ref[:, 0:c] * _sigmoid(pp_ref[:, c:2 * c]))
        gbuf[A_HALO:, :] = p_ref[:, 0:c] * _sigmoid(p_ref[:, c:2 * c])
        cbuf[0:A_HALO, :] = jnp.where(first, 0.0, pp_ref[:, 3 * c:4 * c] * pp_ref[:, 4 * c:5 * c])
        cbuf[A_HALO:, :] = p_ref[:, 3 * c:4 * c] * p_ref[:, 4 * c:5 * c]

        a1 = jnp.broadcast_to(ba_ref[...], (ts, c))
        for k in range(ka):
            off = A_HALO - (ka - 1) + k
            a1 = a1 + gbuf[off:off + ts, :] * wa_ref[k:k + 1, :]
        a1_ref[...] = a1
        xhat, _ = _layernorm_stats(a1)
        a2 = xhat * lng_ref[...] + lnb_ref[...]
        ab_ref[:, 0:c] = (a2 * _sigmoid(a2)).astype(ab_ref.dtype)

        cv = jnp.zeros((ts, c), F32)
        for k in range(kb):
            off = A_HALO - (kb - 1) + k
            cv = cv + cbuf[off:off + ts, :] * wb_ref[k:k + 1, :]
        ab_ref[:, c:2 * c] = (p_ref[:, 2 * c:3 * c] * cv).astype(ab_ref.dtype)

    return _carrying_call(
        body,
        gather_cargo,
        gather=True,
        name=name,
        grid=(s // ts,),
        in_specs=[main, prev, full(wa), full(ba), full(lng), full(lnb), full(wb)],
        out_specs=(pl.BlockSpec((ts, 2 * c), lambda i: (i, 0)), pl.BlockSpec((ts, c), lambda i: (i, 0))),
        out_shape=(jax.ShapeDtypeStruct((s, 2 * c), BF16), jax.ShapeDtypeStruct((s, c), F32)),
        scratch_shapes=[pltpu.VMEM((A_HALO + ts, c), F32), pltpu.VMEM((A_HALO + ts, c), F32)],
        operands=(p, p, wa, ba, lng, lnb, wb),
        compiler_params=_params(("arbitrary",)),
    )


def _conv_act_bwd(p, a1, dab, wa, lng, lnb, wb, *, cargo=(), name):
    s, c5 = p.shape
    c = c5 // 5
    ka, kb = wa.shape[0], wb.shape[0]
    ts = _tile(s, (128,))
    hb = ts // A_HALO
    n_row = s // ts
    last_hb = s // A_HALO - 1
    ext = ts + A_HALO

    def rows(width):
        return (pl.BlockSpec((ts, width), lambda i: (i, 0)),
                pl.BlockSpec((A_HALO, width), lambda i: (jnp.maximum(i * hb - 1, 0), 0)),
                pl.BlockSpec((A_HALO, width), lambda i: (jnp.minimum((i + 1) * hb, last_hb), 0)))

    p_main, p_prev, p_next = rows(c5)
    d_main, _, d_next = rows(2 * c)
    a_main, _, a_next = rows(c)

    def full(shape):
        return pl.BlockSpec(shape, lambda i: (0, 0))

    def body(p_ref, pp_ref, pn_ref, a1_ref, a1n_ref, d_ref, dn_ref, wa_ref, lng_ref, lnb_ref, wb_ref,
             dp_ref, dwa_ref, dba_ref, dlng_ref, dlnb_ref, dwb_ref, gbuf, cbuf, da1buf, dcvbuf):
        i = pl.program_id(0)
        first = i == 0
        last = i == n_row - 1

        @pl.when(first)
        def _():
            dwa_ref[...] = jnp.zeros_like(dwa_ref)
            dba_ref[...] = jnp.zeros_like(dba_ref)
            dlng_ref[...] = jnp.zeros_like(dlng_ref)
            dlnb_ref[...] = jnp.zeros_like(dlnb_ref)
            dwb_ref[...] = jnp.zeros_like(dwb_ref)

        sig_gate = _sigmoid(p_ref[:, c:2 * c])
        gbuf[0:A_HALO, :] = jnp.where(first, 0.0, pp_ref[:, 0:c] * _sigmoid(pp_ref[:, c:2 * c]))
        gbuf[A_HALO:, :] = p_ref[:, 0:c] * sig_gate
        cbuf[0:A_HALO, :] = jnp.where(first, 0.0, pp_ref[:, 3 * c:4 * c] * pp_ref[:, 4 * c:5 * c])
        cbuf[A_HALO:, :] = p_ref[:, 3 * c:4 * c] * p_ref[:, 4 * c:5 * c]

        def ln_back(a1v, dav):
            xhat, rstd = _layernorm_stats(a1v)
            a2 = xhat * lng_ref[...] + lnb_ref[...]
            sg = _sigmoid(a2)
            da2 = dav * (sg * (1.0 + a2 * (1.0 - sg)))
            dxh = da2 * lng_ref[...]
            da1 = rstd * (dxh - jnp.mean(dxh, axis=-1, keepdims=True)
                          - xhat * jnp.mean(dxh * xhat, axis=-1, keepdims=True))
            return da1, da2, xhat

        da1_m, da2_m, xhat_m = ln_back(a1_ref[...], d_ref[:, 0:c])
        da1_n, _, _ = ln_back(a1n_ref[...], dn_ref[:, 0:c])
        da1buf[0:ts, :] = da1_m
        da1buf[ts:, :] = jnp.where(last, 0.0, da1_n)
        dlng_ref[...] += jnp.sum(da2_m * xhat_m, axis=0, keepdims=True)
        dlnb_ref[...] += jnp.sum(da2_m, axis=0, keepdims=True)
        dba_ref[...] += jnp.sum(da1_m, axis=0, keepdims=True)

        dglu = jnp.zeros((ts, c), F32)
        for k in range(ka):
            off = (ka - 1) - k
            dglu = dglu + da1buf[off:off + ts, :] * wa_ref[k:k + 1, :]
            goff = A_HALO - (ka - 1) + k
            dwa_ref[k:k + 1, :] += jnp.sum(da1_m * gbuf[goff:goff + ts, :], axis=0, keepdims=True)
        dp_ref[:, 0:c] = (dglu * sig_gate).astype(dp_ref.dtype)
        dp_ref[:, c:2 * c] = (dglu * p_ref[:, 0:c] * sig_gate * (1.0 - sig_gate)).astype(dp_ref.dtype)

        cv = jnp.zeros((ts, c), F32)
        for k in range(kb):
            off = A_HALO - (kb - 1) + k
            cv = cv + cbuf[off:off + ts, :] * wb_ref[k:k + 1, :]
        db_m = d_ref[:, c:2 * c]
        dp_ref[:, 2 * c:3 * c] = (db_m * cv).astype(dp_ref.dtype)
        dcv_m = db_m * p_ref[:, 2 * c:3 * c]
        dcvbuf[0:ts, :] = dcv_m
        dcvbuf[ts:, :] = jnp.where(last, 0.0, dn_ref[:, c:2 * c] * pn_ref[:, 2 * c:3 * c])
        dbc = jnp.zeros((ts, c), F32)
        for k in range(kb):
            off = (kb - 1) - k
            dbc = dbc + dcvbuf[off:off + ts, :] * wb_ref[k:k + 1, :]
            coff = A_HALO - (kb - 1) + k
            dwb_ref[k:k + 1, :] += jnp.sum(dcv_m * cbuf[coff:coff + ts, :], axis=0, keepdims=True)
        dp_ref[:, 3 * c:4 * c] = (dbc * p_ref[:, 4 * c:5 * c]).astype(dp_ref.dtype)
        dp_ref[:, 4 * c:5 * c] = (dbc * p_ref[:, 3 * c:4 * c]).astype(dp_ref.dtype)

    return _carrying_call(
        body,
        cargo,
        name=name,
        grid=(n_row,),
        in_specs=[p_main, p_prev, p_next, a_main, a_next, d_main, d_next,
                  full(wa.shape), full(lng.shape), full(lnb.shape), full(wb.shape)],
        out_specs=(pl.BlockSpec((ts, c5), lambda i: (i, 0)), full((32, c)), full((1, c)), full((1, c)),
                   full((1, c)), full((8, c))),
        out_shape=(
            jax.ShapeDtypeStruct((s, c5), BF16), jax.ShapeDtypeStruct((32, c), F32),
            jax.ShapeDtypeStruct((1, c), F32), jax.ShapeDtypeStruct((1, c), F32),
            jax.ShapeDtypeStruct((1, c), F32), jax.ShapeDtypeStruct((8, c), F32),
        ),
        scratch_shapes=[
            pltpu.VMEM((A_HALO + ts, c), F32), pltpu.VMEM((A_HALO + ts, c), F32),
            pltpu.VMEM((ext, c), F32), pltpu.VMEM((ext, c), F32),
        ],
        operands=(p, p, p, a1, a1, dab, dab, wa, lng, lnb, wb),
        compiler_params=_params(("arbitrary",)),
    )


def _head_pair_geometry(width, dh):
    assert 2 * dh == LANES, "the attention kernels pair two heads in one 128-lane block"
    return width // (3 * LANES)


def _group_sum(v, lo):
    s_lo = jnp.sum(jnp.where(lo, v, 0.0), axis=-1, keepdims=True)
    s_hi = jnp.sum(jnp.where(lo, 0.0, v), axis=-1, keepdims=True)
    return jnp.where(lo, s_lo, s_hi)


def _qkv_prep_fwd(qkv, gains, dh, *, name):
    s, width = qkv.shape
    nsec = _head_pair_geometry(width, dh)
    tr = _tile(s, (512, 256, 128))
    blk = pl.BlockSpec((tr, nsec * LANES), lambda i, sec: (i, sec))

    def body(x_ref, g_ref, o_ref):
        sec = pl.program_id(1)

        @pl.when(sec == 2)
        def _():
            o_ref[...] = x_ref[...].astype(o_ref.dtype)

        @pl.when(sec != 2)
        def _():
            lo = lax.broadcasted_iota(jnp.int32, (1, LANES), 1) < dh
            for lb in range(nsec):
                ls = pl.ds(lb * LANES, LANES)
                xv = x_ref[:, ls].astype(F32)
                rs = lax.rsqrt(_group_sum(xv * xv, lo) * (1.0 / dh) + EPS)
                o_ref[:, ls] = (xv * rs * g_ref[0:1, :]).astype(o_ref.dtype)

    return pl.pallas_call(
        body,
        name=name,
        grid=(s // tr, 3),
        in_specs=[blk, pl.BlockSpec((8, LANES), lambda i, sec: (sec, 0))],
        out_specs=blk,
        out_shape=jax.ShapeDtypeStruct((s, width), BF16),
        compiler_params=_params(("parallel", "parallel")),
    )(qkv, gains)


def _qkv_prep_bwd(qkv, dq, dk, dv, gains, dh, *, name):
    s, width = qkv.shape
    nsec = _head_pair_geometry(width, dh)
    tr = _tile(s, (512, 256, 128))
    blk = pl.BlockSpec((tr, nsec * LANES), lambda sec, i: (i, sec))
    gspec = pl.BlockSpec((8, LANES), lambda sec, i: (sec, 0))

    def grad_spec(own):
        return pl.BlockSpec((tr, nsec * LANES), lambda sec, i: (jnp.where(sec == own, i, 0), 0))

    def body(x_ref, dq_ref, dk_ref, dv_ref, g_ref, o_ref, dg_ref):
        sec = pl.program_id(0)

        @pl.when(pl.program_id(1) == 0)
        def _():
            dg_ref[...] = jnp.zeros_like(dg_ref)

        def normed(d_ref):
            lo = lax.broadcasted_iota(jnp.int32, (1, LANES), 1) < dh
            dg = jnp.zeros((1, LANES), F32)
            for lb in range(nsec):
                ls = pl.ds(lb * LANES, LANES)
                xv = x_ref[:, ls].astype(F32)
                d = d_ref[:, ls]
                rs = lax.rsqrt(_group_sum(xv * xv, lo) * (1.0 / dh) + EPS)
                xn = xv * rs
                dyg = d * g_ref[0:1, :]
                dx = rs * (dyg - xn * (_group_sum(dyg * xn, lo) * (1.0 / dh)))
                o_ref[:, ls] = dx.astype(o_ref.dtype)
                dg = dg + jnp.sum(d * xn, axis=0, keepdims=True)
            dg_ref[...] += jnp.broadcast_to(dg, dg_ref.shape)

        @pl.when(sec == 0)
        def _():
            normed(dq_ref)

        @pl.when(sec == 1)
        def _():
            normed(dk_ref)

        @pl.when(sec == 2)
        def _():
            o_ref[...] = dv_ref[...].astype(o_ref.dtype)

    return pl.pallas_call(
        body,
        name=name,
        grid=(3, s // tr),
        in_specs=[blk, grad_spec(0), grad_spec(1), grad_spec(2), gspec],
        out_specs=(blk, gspec),
        out_shape=(jax.ShapeDtypeStruct((s, width), BF16), jax.ShapeDtypeStruct((24, LANES), F32)),
        compiler_params=_params(("arbitrary", "arbitrary")),
    )(qkv, dq, dk, dv, gains)


def _split_dot(v, tri):
    hi = v.astype(BF16)
    lo = (v - hi.astype(F32)).astype(BF16)
    return _dot(hi, tri, 1, 0) + _dot(lo, tri, 1, 0)


def _attn_scores(qm, kb, tri_gt, valid, r_run):
    z = _dot(qm, kb, 1, 1)
    zl = jnp.minimum(z, 0.0) - jnp.log(1.0 + jnp.exp(-jnp.abs(z)))
    lk = zl - z
    if valid is not None:
        lk = jnp.where(valid, lk, 0.0)
    after = _split_dot(lk, tri_gt)
    w = jnp.exp(zl + after + r_run)
    if valid is not None:
        w = jnp.where(valid, w, 0.0)
    return zl, lk, w


def _attn_alive(rr):
    return jnp.max(jnp.maximum(rr[0], rr[1])) > ATTN_DEAD_LOG_WEIGHT


def _attn_walk_variants(i, walk):
    @pl.when(i == 0)
    def _():
        walk(False)

    @pl.when(i > 0)
    def _():
        walk(True)


def _attn_specs(s, width, dh, tq):
    nsec = _head_pair_geometry(width, dh)
    qspec = pl.BlockSpec((tq, LANES), lambda h, i: (i, h))
    kspec = pl.BlockSpec((s, LANES), lambda h, i: (0, nsec + h))
    vspec = pl.BlockSpec((s, LANES), lambda h, i: (0, 2 * nsec + h))
    return nsec, qspec, kspec, vspec


def _attn_masks(tq, tk, dh):
    lo = lax.broadcasted_iota(jnp.int32, (1, LANES), 1) < dh
    r_io = lax.broadcasted_iota(jnp.int32, (tk, tk), 0)
    c_io = lax.broadcasted_iota(jnp.int32, (tk, tk), 1)
    qrow = lax.broadcasted_iota(jnp.int32, (tq, tk), 0)
    kcol = lax.broadcasted_iota(jnp.int32, (tq, tk), 1)
    return lo, r_io, c_io, qrow, kcol


def _attn_fwd(qkv16, dh, *, gather_cargo=(), name):
    s, width = qkv16.shape
    tq = _tile(s, (ATTN_Q_BLOCK, ATTN_K_BLOCK))
    tk = _tile(tq, (ATTN_K_BLOCK,))
    nd = tq // tk
    nsec, qspec, kspec, vspec = _attn_specs(s, width, dh, tq)

    def body(q_ref, k_ref, v_ref, o_ref):
        i = pl.program_id(1)
        lo, r_io, c_io, qrow, kcol = _attn_masks(tq, tk, dh)
        q2 = q_ref[...]
        zq = jnp.zeros_like(q2)
        qs = (jnp.where(lo, q2, zq), jnp.where(lo, zq, q2))
        tri_gt = (r_io > c_io).astype(BF16)

        def block(j, carry, diag):
            rr, acc = carry
            start = pl.multiple_of(j * tk, tk)
            kb = k_ref[pl.ds(start, tk), :]
            vb = v_ref[pl.ds(start, tk), :]
            valid = None if diag is None else kcol + diag * tk < qrow
            outs, new_r = [], []
            for hh in range(2):
                _, lk, w = _attn_scores(qs[hh], kb, tri_gt, valid, rr[hh])
                outs.append(_dot(w.astype(BF16), vb, 1, 0))
                new_r.append(rr[hh] + jnp.sum(lk, axis=-1, keepdims=True))
            return tuple(new_r), acc + jnp.where(lo, outs[0], outs[1])

        below = i * nd

        def step(st):
            jj, _, cr = st
            cr = block(below - 1 - jj, cr, None)
            return jj + 1, _attn_alive(cr[0]), cr

        def walk(inline_first_below):
            zero = jnp.zeros((tq, 1), F32)
            carry = ((zero, zero), jnp.zeros((tq, LANES), F32))
            for dg in reversed(range(nd)):
                carry = block(i * nd + dg, carry, dg)
            if inline_first_below:
                carry = block(below - 1, carry, None)
            _, _, carry = lax.while_loop(lambda st: (st[0] < below) & st[1], step,
                                         (jnp.int32(int(inline_first_below)), _attn_alive(carry[0]), carry))
            o_ref[...] = carry[1]

        _attn_walk_variants(i, walk)

    outs, gathered = _carrying_call(
        body,
        gather_cargo,
        gather=True,
        name=name,
        grid=(nsec, s // tq),
        in_specs=[qspec, kspec, vspec],
        out_specs=[qspec],
        out_shape=[jax.ShapeDtypeStruct((s, width // 3), F32)],
        scratch_shapes=[],
        operands=(qkv16, qkv16, qkv16),
        compiler_params=_params(("arbitrary", "arbitrary")),
    )
    return outs[0], gathered


def _attn_bwd(qkv16, o, do, dh, *, cargo=(), name):
    s, width = qkv16.shape
    tq = _tile(s, (ATTN_Q_BLOCK, ATTN_K_BLOCK))
    tk = _tile(tq, (ATTN_K_BLOCK,))
    nd = tq // tk
    nsec, qspec, kspec, vspec = _attn_specs(s, width, dh, tq)
    accspec = pl.BlockSpec((s, LANES), lambda h, i: (0, h))

    def body(q_ref, k_ref, v_ref, o_ref, do_ref, dq_ref, dk_ref, dv_ref):
        i = pl.program_id(1)

        @pl.when(i == 0)
        def _():
            dk_ref[...] = jnp.zeros_like(dk_ref)
            dv_ref[...] = jnp.zeros_like(dv_ref)

        lo, r_io, c_io, qrow, kcol = _attn_masks(tq, tk, dh)
        q2 = q_ref[...]
        zq = jnp.zeros_like(q2)
        qs = (jnp.where(lo, q2, zq), jnp.where(lo, zq, q2))
        do16 = do_ref[...].astype(BF16)
        dos = (jnp.where(lo, do16, zq), jnp.where(lo, zq, do16))
        prod = do16.astype(F32) * o_ref[...]
        deltas = (jnp.sum(jnp.where(lo, prod, 0.0), axis=-1, keepdims=True),
                  jnp.sum(jnp.where(lo, 0.0, prod), axis=-1, keepdims=True))
        tri_gt = (r_io > c_io).astype(BF16)
        tri_ge = (r_io >= c_io).astype(BF16)

        def block(j, carry, diag):
            rr, er, dq = carry
            start = pl.multiple_of(j * tk, tk)
            kb = k_ref[pl.ds(start, tk), :]
            vb = v_ref[pl.ds(start, tk), :]
            valid = None if diag is None else kcol + diag * tk < qrow
            new_r, new_e, dqs = [], [], []
            dk_blk = dv_blk = None
            for hh in range(2):
                zl, lk, w = _attn_scores(qs[hh], kb, tri_gt, valid, rr[hh])
                beta = jnp.exp(zl)
                w16 = w.astype(BF16)
                e = _dot(dos[hh], vb, 1, 1) * w16.astype(F32)
                e_before = deltas[hh] - er[hh] - _split_dot(e, tri_ge)
                dz = e - beta * (e + e_before)
                if valid is not None:
                    dz = jnp.where(valid, dz, 0.0)
                dz16 = dz.astype(BF16)
                dqs.append(_dot(dz16, kb, 1, 0))
                dkc = _dot(dz16, qs[hh], 0, 0)
                dvc = _dot(w16, dos[hh], 0, 0)
                dk_blk = dkc if dk_blk is None else dk_blk + dkc
                dv_blk = dvc if dv_blk is None else dv_blk + dvc
                new_r.append(rr[hh] + jnp.sum(lk, axis=-1, keepdims=True))
                new_e.append(er[hh] + jnp.sum(e, axis=-1, keepdims=True))
            dk_ref[pl.ds(start, tk), :] += dk_blk
            dv_ref[pl.ds(start, tk), :] += dv_blk
            return tuple(new_r), tuple(new_e), dq + jnp.where(lo, dqs[0], dqs[1])

        below = i * nd

        def step(st):
            jj, _, cr = st
            cr = block(below - 1 - jj, cr, None)
            return jj + 1, _attn_alive(cr[0]), cr

        def walk(inline_first_below):
            zero = jnp.zeros((tq, 1), F32)
            carry = ((zero, zero), (zero, zero), jnp.zeros((tq, LANES), F32))
            for dg in reversed(range(nd)):
                carry = block(i * nd + dg, carry, dg)
            if inline_first_below:
                carry = block(below - 1, carry, None)
            _, _, carry = lax.while_loop(lambda st: (st[0] < below) & st[1], step,
                                         (jnp.int32(int(inline_first_below)), _attn_alive(carry[0]), carry))
            dq_ref[...] = carry[2]

        _attn_walk_variants(i, walk)

    shp = jax.ShapeDtypeStruct((s, width // 3), F32)
    return _carrying_call(
        body,
        cargo,
        name=name,
        grid=(nsec, s // tq),
        in_specs=[qspec, kspec, vspec, qspec, qspec],
        out_specs=(qspec, accspec, accspec),
        out_shape=(shp, shp, shp),
        scratch_shapes=[],
        operands=(qkv16, qkv16, qkv16, o, do),
        compiler_params=_params(("arbitrary", "arbitrary")),
    )


def _adam_update(wv, gv, mv, vv):
    c1 = 1.0 - ADAM_B1 ** ADAM_STEP
    c2 = 1.0 - ADAM_B2 ** ADAM_STEP
    nm = ADAM_B1 * mv + (1.0 - ADAM_B1) * gv
    nv = ADAM_B2 * vv + (1.0 - ADAM_B2) * (gv * gv)
    delta = -ADAM_LR * ((nm / c1) / (jnp.sqrt(nv / c2) + ADAM_EPS) + ADAM_WD * wv)
    return delta, nm, nv


def _sum_slabs(r, *, name):
    n, rows, lanes = r.shape
    tr = _tile(rows, (1024, 512, 256, 128, 64, 32, 16))

    def body(r_ref, o_ref):
        acc = r_ref[0]
        for d in range(1, n):
            acc = acc + r_ref[d]
        o_ref[...] = acc

    return pl.pallas_call(
        body,
        name=name,
        grid=(rows // tr,),
        in_specs=[pl.BlockSpec((n, tr, lanes), lambda i: (0, i, 0))],
        out_specs=pl.BlockSpec((tr, lanes), lambda i: (i, 0)),
        out_shape=jax.ShapeDtypeStruct((rows, lanes), F32),
        compiler_params=_params(("parallel",)),
    )(r)


def _adamw(w, g, m, v, *, name):
    rows, cols = w.shape
    tr = _tile(rows, (1024, 512, 256, 128, 64, 32, 16))
    spec = pl.BlockSpec((tr, cols), lambda i: (i, 0))

    def body(w_ref, g_ref, m_ref, v_ref, d_ref, nm_ref, nv_ref):
        d_ref[...], nm_ref[...], nv_ref[...] = _adam_update(w_ref[...], g_ref[...], m_ref[...], v_ref[...])

    shp = jax.ShapeDtypeStruct((rows, cols), F32)
    return pl.pallas_call(
        body,
        name=name,
        grid=(rows // tr,),
        in_specs=[spec, spec, spec, spec],
        out_specs=(spec, spec, spec),
        out_shape=(shp, shp, shp),
        compiler_params=_params(("parallel",)),
    )(w, g, m, v)


def _adamw_sum(w, received, m, v, *, name):
    layers, rows, cols = w.shape
    n = received[0].shape[0]
    tr = _tile(rows, (256, 128, 64, 32, 16))
    spec = pl.BlockSpec((1, tr, cols), lambda li, i: (li, i, 0))

    def partial_spec(own):
        return pl.BlockSpec((n, tr, cols), lambda li, i: (0, jnp.where(li == own, i, 0), 0))

    def body(w_ref, *refs):
        r_refs, (m_ref, v_ref, g_ref, d_ref, nm_ref, nv_ref) = refs[:layers], refs[layers:]
        for own in range(layers):
            @pl.when(pl.program_id(0) == own)
            def _(r_ref=r_refs[own]):
                gv = r_ref[0].astype(F32)
                for d in range(1, n):
                    gv = gv + r_ref[d].astype(F32)
                g_ref[0] = gv
                d_ref[0], nm_ref[0], nv_ref[0] = _adam_update(w_ref[0], gv, m_ref[0], v_ref[0])

    shp = jax.ShapeDtypeStruct((layers, rows, cols), F32)
    return pl.pallas_call(
        body,
        name=name,
        grid=(layers, rows // tr),
        in_specs=[spec] + [partial_spec(own) for own in range(layers)] + [spec, spec],
        out_specs=(spec, spec, spec, spec),
        out_shape=(shp, shp, shp, shp),
        compiler_params=_params(("arbitrary", "arbitrary")),
    )(w, *received, m, v)


_MATMUL_WEIGHTS = (("conv_w_in", 2), ("conv_w_out", 1), ("attn_w_qkv", 2), ("attn_w_o", 1), ("ffn_w_up", 2),
                   ("ffn_w_down", 1))
_TAP_WEIGHTS = (("conv_a_dw_w", 2), ("conv_b_dw_w", 2), ("ffn_dw_w", 2))
_REPLICATED = ("mix_norm_g", "ffn_norm_g", "conv_a_dw_b", "conv_a_ln_g", "conv_a_ln_b", "attn_q_g", "attn_k_g",
               "ffn_dw_b")
_WEIGHT_ORDER = ("mix_norm_g", "ffn_norm_g", "conv_w_in", "conv_a_dw_w", "conv_a_dw_b", "conv_a_ln_g",
                 "conv_a_ln_b", "conv_b_dw_w", "conv_w_out", "attn_w_qkv", "attn_q_g", "attn_k_g", "attn_w_o",
                 "ffn_w_up", "ffn_dw_w", "ffn_dw_b", "ffn_w_down")


def _assemble(gathered, axis):
    n, l, a, b = gathered.shape
    if axis == 2:
        return jnp.transpose(gathered, (1, 2, 0, 3)).reshape(l, a, n * b)
    return jnp.transpose(gathered, (1, 0, 2, 3)).reshape(l, n * a, b)


def _split_shards(full, axis):
    l, a, b = full.shape
    if axis == 2:
        return jnp.transpose(full.reshape(l, a, N_DEV, b // N_DEV), (2, 0, 1, 3))
    return jnp.transpose(full.reshape(l, N_DEV, a // N_DEV, b), (1, 0, 2, 3))


def kernel(x, mix_norm_g, ffn_norm_g, conv_w_in, conv_a_dw_w, conv_a_dw_b, conv_a_ln_g, conv_a_ln_b, conv_b_dw_w, conv_w_out, attn_w_qkv, attn_q_g, attn_k_g, attn_w_o, ffn_w_up, ffn_dw_w, ffn_dw_b, ffn_w_down, loss_target, m_mix_norm_g, m_ffn_norm_g, m_conv_w_in, m_conv_a_dw_w, m_conv_a_dw_b, m_conv_a_ln_g, m_conv_a_ln_b, m_conv_b_dw_w, m_conv_w_out, m_attn_w_qkv, m_attn_q_g, m_attn_k_g, m_attn_w_o, m_ffn_w_up, m_ffn_dw_w, m_ffn_dw_b, m_ffn_w_down, v_mix_norm_g, v_ffn_norm_g, v_conv_w_in, v_conv_a_dw_w, v_conv_a_dw_b, v_conv_a_ln_g, v_conv_a_ln_b, v_conv_b_dw_w, v_conv_w_out, v_attn_w_qkv, v_attn_q_g, v_attn_k_g, v_attn_w_o, v_ffn_w_up, v_ffn_dw_w, v_ffn_dw_b, v_ffn_w_down):
    args = dict(locals())
    weights = {n: args[n] for n in _WEIGHT_ORDER}
    mom_m = {n: args["m_" + n] for n in _WEIGHT_ORDER}
    mom_v = {n: args["v_" + n] for n in _WEIGHT_ORDER}

    _, s, d = x.shape
    depth = mix_norm_g.shape[0]
    dh = attn_q_g.shape[1]
    x0 = x.reshape(s, d)
    target = loss_target.reshape(s, d)

    shard_axis = dict(_MATMUL_WEIGHTS)
    shard16 = {n: weights[n].astype(BF16) for n, _ in _MATMUL_WEIGHTS}
    full = {}

    def layer_keys(layer):
        if layer >= depth:
            return []
        mixer = ("conv_w_in", "conv_w_out") if layer % 2 == 0 else ("attn_w_qkv", "attn_w_o")
        return [(mixer[0], layer // 2), (mixer[1], layer // 2), ("ffn_w_up", layer), ("ffn_w_down", layer)]

    def shards(keys):
        return [shard16[n][idx] for n, idx in keys]

    def assemble(keys, gathered):
        for (n, idx), g in zip(keys, gathered):
            if shard_axis[n] == 2:
                full[(n, idx)] = _assemble_cols(g, name=f"assemble_{n}_{idx}")
            else:
                full[(n, idx)] = g.reshape(N_DEV * g.shape[1], g.shape[2])

    mixer_in0, mixer_out0, ffn_up0, ffn_down0 = layer_keys(0)
    up_front = [mixer_in0, mixer_out0, ffn_down0]
    assemble(up_front, _all_gather(shards(up_front), "gather_first_weights"))
    tap_shapes = [weights[n].shape for n, _ in _TAP_WEIGHTS]
    g_tap = _all_gather([_pack([weights[n] for n, _ in _TAP_WEIGHTS], F32)], "gather_tap_weights")[0]
    for (n, axis), part in zip(_TAP_WEIGHTS, _unpack(g_tap, tap_shapes, lead=(N_DEV,))):
        full[n] = _assemble(part, axis)
    f = ffn_w_down.shape[1] * N_DEV

    saved = []
    xs = x0
    hn = _rmsnorm_fwd(xs, mix_norm_g[0:1], name="mix_norm_fwd_0")
    for layer in range(depth):
        i = layer // 2
        nxt = layer_keys(layer + 1)
        sv = {"x_in": xs, "h": hn}
        if layer % 2 == 0:
            on_in, on_act = ([], nxt[0:1]) if layer > 0 else (nxt[0:1], [ffn_up0])
            p, got = _matmul(hn, full[("conv_w_in", i)], gather_cargo=shards(on_in), name=f"conv_in_{layer}")
            assemble(on_in, got)
            (ab, a1), got = _conv_act_fwd(p, full["conv_a_dw_w"][i], conv_a_dw_b[i:i + 1], conv_a_ln_g[i:i + 1],
                                          conv_a_ln_b[i:i + 1], full["conv_b_dw_w"][i], gather_cargo=shards(on_act),
                                          name=f"conv_act_fwd_{layer}")
            assemble(on_act, got)
            sv.update(p=p, a1=a1, mix=ab)
            (xs, h2), _ = _matmul_add_norm(ab, full[("conv_w_out", i)], xs, ffn_norm_g[layer:layer + 1],
                                           name=f"conv_out_{layer}")
            riders = [nxt[1:2], nxt[2:3], nxt[3:4]]
        else:
            reps = LANES // dh
            gains = jnp.concatenate([
                jnp.broadcast_to(jnp.tile(attn_q_g[i], reps) * dh ** -0.5, (8, LANES)),
                jnp.broadcast_to(jnp.tile(attn_k_g[i], reps), (8, LANES)),
                jnp.ones((8, LANES), F32)], axis=0)
            qkv = _matmul(hn, full[("attn_w_qkv", i)], out_dtype=BF16, name=f"attn_qkv_{layer}")
            qkv16 = _qkv_prep_fwd(qkv, gains, dh, name=f"qkv_prep_fwd_{layer}")
            o, got = _attn_fwd(qkv16, dh, gather_cargo=shards(nxt), name=f"attn_fwd_{layer}")
            assemble(nxt, got)
            sv.update(qkv=qkv, qkv16=qkv16, gains=gains, mix=o)
            (xs, h2), _ = _matmul_add_norm(o, full[("attn_w_o", i)], xs, ffn_norm_g[layer:layer + 1],
                                           name=f"attn_out_{layer}")
            riders = [[], [], []]
        sv["x_mid"] = xs
        w_up = full[("ffn_w_up", layer)]
        ug, got = _matmul(h2, w_up, b_half=0, out_dtype=BF16, gather_cargo=shards(riders[0]),
                          name=f"ffn_up_gate_{layer}")
        assemble(riders[0], got)
        uv = _matmul(h2, w_up, b_half=1, out_dtype=BF16, name=f"ffn_up_val_{layer}")
        taps = full["ffn_dw_w"][layer]
        bias = ffn_dw_b[layer:layer + 1]
        act, got = _ffn_act_fwd(ug, uv, taps[:, :f], taps[:, f:], bias[:, :f], bias[:, f:],
                                gather_cargo=shards(riders[1]), name=f"ffn_act_fwd_{layer}")
        assemble(riders[1], got)
        sv.update(h2=h2, ug=ug, uv=uv, act=act)
        if layer + 1 < depth:
            (xs, hn), got = _matmul_add_norm(act, full[("ffn_w_down", layer)], xs, mix_norm_g[layer + 1:layer + 2],
                                             gather_cargo=shards(riders[2]), name=f"ffn_down_{layer}")
        else:
            xs, got = _matmul(act, full[("ffn_w_down", layer)], add=xs, gather_cargo=shards(riders[2]),
                              name=f"ffn_down_{layer}")
        assemble(riders[2], got)
        saved.append(sv)

    sq, dx, dx16 = _loss_head(xs, target, name="loss_head")
    loss = lax.psum(0.5 * sq[0, 0] / d, ("x", "y", "c"))

    grads = {n: [None] * weights[n].shape[0] for n in _WEIGHT_ORDER}
    queued, arrived = [], {}

    def queue(n, idx, g):
        a, b = weights[n].shape[1:]
        slabs = _split_cols(g, name=f"split_{n}_{idx}") if shard_axis[n] == 2 else g.reshape(N_DEV, a, b)
        queued.append(((n, idx), slabs))

    def carried(call):
        keys = [k for k, _ in queued]
        slabs = [sl for _, sl in queued]
        queued.clear()
        outs, received = call(slabs)
        arrived.update(zip(keys, received))
        return outs

    for layer in reversed(range(depth)):
        i = layer // 2
        sv = saved[layer]
        w_up = full[("ffn_w_up", layer)]
        taps = full["ffn_dw_w"][layer]
        bias = ffn_dw_b[layer:layer + 1]
        dact = _matmul(dx16, full[("ffn_w_down", layer)], tb=True, out_dtype=BF16, name=f"ffn_down_dx_{layer}")
        queue("ffn_w_down", layer, _weight_grad(sv["act"], dx16, name=f"ffn_down_dw_{layer}"))
        dug, duv, sumg, sumv = carried(lambda cargo: _ffn_act_bwd(
            sv["ug"], sv["uv"], dact, taps[:, :f], taps[:, f:], bias[:, :f], bias[:, f:], cargo=cargo,
            name=f"ffn_act_bwd_{layer}"))
        sums = jnp.concatenate([sumg, sumv], axis=1).reshape(4, 8, 2 * f).sum(axis=1)
        grads["ffn_dw_w"][layer] = sums[:3]
        grads["ffn_dw_b"][layer] = sums[3]
        dh2 = _matmul(dug, w_up, tb=True, b_half=0, name=f"ffn_up_gate_dx_{layer}")
        (dx, dx16, dg), _ = _matmul_norm_bwd(duv, w_up, sv["x_mid"], ffn_norm_g[layer:layer + 1], dx, b_half=1,
                                             add=dh2, name=f"ffn_up_val_dx_{layer}")
        grads["ffn_norm_g"][layer] = dg[0]
        queue("ffn_w_up", layer, jnp.concatenate(
            [_weight_grad(sv["h2"], dug, name=f"ffn_up_gate_dw_{layer}"),
             _weight_grad(sv["h2"], duv, name=f"ffn_up_val_dw_{layer}")], axis=1))

        if layer % 2 == 0:
            dab = _matmul(dx16, full[("conv_w_out", i)], tb=True, name=f"conv_out_dx_{layer}")
            queue("conv_w_out", i, _weight_grad(sv["mix"], dx16, name=f"conv_out_dw_{layer}"))
            ka = full["conv_a_dw_w"].shape[1]
            kb = full["conv_b_dw_w"].shape[1]
            dp, dwa, dba, dlng, dlnb, dwb = carried(lambda cargo: _conv_act_bwd(
                sv["p"], sv["a1"], dab, full["conv_a_dw_w"][i], conv_a_ln_g[i:i + 1], conv_a_ln_b[i:i + 1],
                full["conv_b_dw_w"][i], cargo=cargo, name=f"conv_act_bwd_{layer}"))
            grads["conv_a_dw_w"][i] = dwa[:ka]
            grads["conv_a_dw_b"][i] = dba[0]
            grads["conv_a_ln_g"][i] = dlng[0]
            grads["conv_a_ln_b"][i] = dlnb[0]
            grads["conv_b_dw_w"][i] = dwb[:kb]
            queue("conv_w_in", i, _weight_grad(sv["h"], dp, name=f"conv_in_dw_{layer}"))
            dy, w_in = dp, full[("conv_w_in", i)]
        else:
            do = _matmul(dx16, full[("attn_w_o", i)], tb=True, name=f"attn_out_dx_{layer}")
            queue("attn_w_o", i, _weight_grad(sv["mix"], dx16, name=f"attn_out_dw_{layer}"))
            dq, dk, dv = carried(lambda cargo: _attn_bwd(sv["qkv16"], sv["mix"], do, dh, cargo=cargo,
                                                         name=f"attn_bwd_{layer}"))
            dqkv, dgains = _qkv_prep_bwd(sv["qkv"], dq, dk, dv, sv["gains"], dh, name=f"qkv_prep_bwd_{layer}")
            grads["attn_q_g"][i] = dgains[0].reshape(-1, dh).sum(axis=0) * dh ** -0.5
            grads["attn_k_g"][i] = dgains[8].reshape(-1, dh).sum(axis=0)
            queue("attn_w_qkv", i, _weight_grad(sv["h"], dqkv, name=f"attn_qkv_dw_{layer}"))
            dy, w_in = dqkv, full[("attn_w_qkv", i)]

        def mixer_in_bwd(cargo):
            return _matmul_norm_bwd(dy, w_in, sv["x_in"], mix_norm_g[layer:layer + 1], dx, cargo=cargo,
                                    name=f"mixer_in_dx_{layer}")

        dx, dx16, dg = carried(mixer_in_bwd) if layer == 0 else mixer_in_bwd(())[0]
        grads["mix_norm_g"][layer] = dg[0]

    me = _my_index()
    final_grads, delta, new_m, new_v = {}, {}, {}, {}
    for n, _ in _MATMUL_WEIGHTS:
        final_grads[n], delta[n], new_m[n], new_v[n] = _adamw_sum(
            weights[n], [arrived[(n, idx)] for idx in range(weights[n].shape[0])], mom_m[n], mom_v[n],
            name=f"adamw_{n}")

    small_names = list(_REPLICATED) + [n for n, _ in _TAP_WEIGHTS]
    local = {n: jnp.stack(grads[n], axis=0) for n in small_names}
    small_shapes = [local[n].shape for n in small_names]
    g_small = _sum_slabs(_all_gather([_pack([local[n] for n in small_names], F32)], "gather_small_grads")[0],
                         name="sum_small_grads")
    reduced = dict(zip(small_names, _unpack(g_small, small_shapes)))
    for n in _REPLICATED:
        final_grads[n] = reduced[n]
    for n, axis in _TAP_WEIGHTS:
        final_grads[n] = lax.dynamic_index_in_dim(_split_shards(reduced[n], axis), me, axis=0, keepdims=False)
    shapes = [weights[n].shape for n in small_names]
    d_s, m_s, v_s = _adamw(
        _pack([weights[n] for n in small_names], F32), _pack([final_grads[n] for n in small_names], F32),
        _pack([mom_m[n] for n in small_names], F32), _pack([mom_v[n] for n in small_names], F32), name="adamw_small")
    delta.update(zip(small_names, _unpack(d_s, shapes)))
    new_m.update(zip(small_names, _unpack(m_s, shapes)))
    new_v.update(zip(small_names, _unpack(v_s, shapes)))

    order = list(_WEIGHT_ORDER)
    return (loss, dx.reshape(x.shape), *[final_grads[n] for n in order], *[delta[n] for n in order],
            *[new_m[n] for n in order], *[new_v[n] for n in order])
```

```python
import jax
import jax.numpy as jnp
from jax import lax
from jax.experimental import pallas as pl
from jax.experimental.pallas import tpu as pltpu

F32 = jnp.float32
BF16 = jnp.bfloat16
EPS = 1e-6
N_DEV = 8
MESH_ID = pl.DeviceIdType.MESH

ADAM_LR = 0.001
ADAM_B1 = 0.9
ADAM_B2 = 0.999
ADAM_EPS = 1e-08
ADAM_WD = 0.01
ADAM_STEP = 10

V7X_VMEM_LIMIT_BYTES = 48 * 1024 * 1024
PACK_QUANTUM = 2048
A_HALO = 32
S_HALO = 8
FFN_HALO_BLOCK = 16
LANES = 128
ATTN_Q_BLOCK = 256
ATTN_K_BLOCK = 256
MATMUL_MAX_SINGLE_K = 3072
MXU_WIDTH = 256
FFN_CHUNK_ROWS = 128
ATTN_DEAD_LOG_WEIGHT = -110.0


def _tile(n, prefs):
    for p in prefs:
        if n % p == 0:
            return p
    return n


def _params(sem=None):
    return pltpu.CompilerParams(dimension_semantics=sem, vmem_limit_bytes=V7X_VMEM_LIMIT_BYTES)


def _sigmoid(x):
    return 1.0 / (1.0 + jnp.exp(-x))


def _dot(a, b, ca, cb):
    return lax.dot_general(a, b, (((ca,), (cb,)), ((), ())), preferred_element_type=F32)


def _my_index():
    return 4 * lax.axis_index("x") + 2 * lax.axis_index("y") + lax.axis_index("c")


def _all_gather(shards, name):
    n = len(shards)

    def body(*refs):
        x_refs, out_refs = refs[:n], refs[n:2 * n]
        send_sems, recv_sems, local_sems = refs[2 * n:]
        x, y, c = lax.axis_index("x"), lax.axis_index("y"), lax.axis_index("c")
        me, sibling = (x, y, c), (x, y, 1 - c)
        chips = [(1 - x, y), (x, 1 - y), (1 - x, 1 - y)]

        def copy(a, k, block, to, src=None):
            slot = out_refs[a].at[4 * block[0] + 2 * block[1] + block[2]]
            return pltpu.make_async_remote_copy(
                src_ref=slot if src is None else src, dst_ref=slot,
                send_sem=send_sems.at[7 * a + k], recv_sem=recv_sems.at[7 * a + k],
                device_id=to, device_id_type=MESH_ID)

        mine = [pltpu.make_async_copy(x_refs[a], out_refs[a].at[4 * x + 2 * y + c], local_sems.at[a])
                for a in range(n)]
        for cp in mine:
            cp.start()
        first = []
        for a in range(n):
            first.append(copy(a, 0, me, sibling, src=x_refs[a]))
            first += [copy(a, 1 + j, me, (*chip, c), src=x_refs[a]) for j, chip in enumerate(chips)]
        for cp in first:
            cp.start()
        passed = []
        for j, chip in enumerate(chips):
            for a in range(n):
                copy(a, 1 + j, (*chip, c), me).wait_recv()
                fwd = copy(a, 4 + j, (*chip, c), sibling)
                fwd.start()
                passed.append(fwd)
        for a in range(n):
            copy(a, 0, sibling, me).wait_recv()
            for j, chip in enumerate(chips):
                copy(a, 4 + j, (*chip, 1 - c), me).wait_recv()
        for cp in first + passed:
            cp.wait_send()
        for cp in mine:
            cp.wait()

    hbm = pl.BlockSpec(memory_space=pl.ANY)
    return pl.pallas_call(
        body,
        name=name,
        out_shape=[jax.ShapeDtypeStruct((N_DEV,) + sh.shape, sh.dtype) for sh in shards],
        in_specs=[hbm] * n,
        out_specs=[hbm] * n,
        scratch_shapes=[
            pltpu.SemaphoreType.DMA((7 * n,)),
            pltpu.SemaphoreType.DMA((7 * n,)),
            pltpu.SemaphoreType.DMA((n,)),
        ],
    )(*shards)


def _exchange_semaphores(n):
    return [pltpu.SemaphoreType.DMA((7 * n,)), pltpu.SemaphoreType.DMA((7 * n,)), pltpu.SemaphoreType.DMA((n,))]


def _exchange_copies(g_refs, out_refs, send_sems, recv_sems, local_sems, gather=False):
    n = len(g_refs)
    x, y, c = lax.axis_index("x"), lax.axis_index("y"), lax.axis_index("c")
    me = 4 * x + 2 * y + c

    def part(a, d):
        return g_refs[a] if gather else g_refs[a].at[d]

    local = [pltpu.make_async_copy(part(a, me), out_refs[a].at[me], local_sems.at[a]) for a in range(n)]
    remote = []
    for k in range(1, N_DEV):
        px = (x + ((k >> 2) & 1)) % 2
        py = (y + ((k >> 1) & 1)) % 2
        pc = (c + (k & 1)) % 2
        for a in range(n):
            remote.append(pltpu.make_async_remote_copy(
                src_ref=part(a, 4 * px + 2 * py + pc), dst_ref=out_refs[a].at[me],
                send_sem=send_sems.at[7 * a + k - 1], recv_sem=recv_sems.at[7 * a + k - 1],
                device_id=(px, py, pc), device_id_type=MESH_ID))
    return local, remote


def _exchange_start(copies):
    for cp in copies[0] + copies[1]:
        cp.start()


def _exchange_wait(copies):
    local, remote = copies
    for cp in remote:
        cp.wait_recv()
    for cp in remote:
        cp.wait_send()
    for cp in local:
        cp.wait()


def _carrying_call(body, cargo, *, name, grid, in_specs, out_specs, out_shape, scratch_shapes, operands,
                   compiler_params, gather=False):
    n = len(cargo)
    n_in, n_out, n_scr = len(in_specs), len(out_specs), len(scratch_shapes)
    if n == 0:
        return pl.pallas_call(body, name=name, grid=grid, in_specs=in_specs, out_specs=out_specs,
                              out_shape=out_shape, scratch_shapes=scratch_shapes,
                              compiler_params=compiler_params)(*operands), []

    def carrying(*refs):
        cuts = [0, n_in, n_in + n, n_in + n + n_out, n_in + 2 * n + n_out, n_in + 2 * n + n_out + n_scr, len(refs)]
        ins, g_refs, outs, r_refs, scr, sems = [refs[a:b] for a, b in zip(cuts[:-1], cuts[1:])]
        steps = [pl.program_id(ax) for ax in range(len(grid))]
        first = steps[0] == 0
        last = steps[0] == grid[0] - 1
        for ax in range(1, len(grid)):
            first = first & (steps[ax] == 0)
            last = last & (steps[ax] == grid[ax] - 1)

        @pl.when(first)
        def _():
            _exchange_start(_exchange_copies(g_refs, r_refs, *sems, gather=gather))

        body(*ins, *outs, *scr)

        @pl.when(last)
        def _():
            _exchange_wait(_exchange_copies(g_refs, r_refs, *sems, gather=gather))

    hbm = pl.BlockSpec(memory_space=pl.ANY)
    lead = (N_DEV,) if gather else ()
    res = pl.pallas_call(
        carrying,
        name=name,
        grid=grid,
        in_specs=list(in_specs) + [hbm] * n,
        out_specs=list(out_specs) + [hbm] * n,
        out_shape=list(out_shape) + [jax.ShapeDtypeStruct(lead + g.shape, g.dtype) for g in cargo],
        scratch_shapes=list(scratch_shapes) + _exchange_semaphores(n),
        compiler_params=compiler_params,
    )(*operands, *cargo)
    return res[:n_out], res[n_out:]


def _assemble_cols(g, *, name):
    n, r, w = g.shape
    tr = _tile(r, (256, 128, 64, 32, 16))

    def body(g_ref, o_ref):
        for j in range(n):
            o_ref[:, j * w:(j + 1) * w] = g_ref[j]

    return pl.pallas_call(
        body,
        name=name,
        grid=(r // tr,),
        in_specs=[pl.BlockSpec((n, tr, w), lambda i: (0, i, 0))],
        out_specs=pl.BlockSpec((tr, n * w), lambda i: (i, 0)),
        out_shape=jax.ShapeDtypeStruct((r, n * w), g.dtype),
        compiler_params=_params(("parallel",)),
    )(g)


def _split_cols(x, *, name):
    r, nw = x.shape
    w = nw // N_DEV
    tr = _tile(r, (256, 128, 64, 32, 16))

    def body(x_ref, o_ref):
        for j in range(N_DEV):
            o_ref[j] = x_ref[:, j * w:(j + 1) * w]

    return pl.pallas_call(
        body,
        name=name,
        grid=(r // tr,),
        in_specs=[pl.BlockSpec((tr, nw), lambda i: (i, 0))],
        out_specs=pl.BlockSpec((N_DEV, tr, w), lambda i: (0, i, 0)),
        out_shape=jax.ShapeDtypeStruct((N_DEV, r, w), x.dtype),
        compiler_params=_params(("parallel",)),
    )(x)


def _padded(n):
    return -(-n // PACK_QUANTUM) * PACK_QUANTUM


def _pack(parts, dtype, lead=()):
    flat = []
    for p in parts:
        p = p.astype(dtype).reshape(lead + (-1,))
        n = p.shape[-1]
        flat.append(jnp.pad(p, [(0, 0)] * len(lead) + [(0, _padded(n) - n)]))
    return jnp.concatenate(flat, axis=-1).reshape(lead + (-1, 128))


def _unpack(buf, shapes, lead=()):
    flat = buf.reshape(lead + (-1,))
    out, off = [], 0
    for shp in shapes:
        n = 1
        for d in shp:
            n *= d
        out.append(flat[..., off:off + n].reshape(lead + tuple(shp)))
        off += _padded(n)
    return out


def _half_if_tiled(n):
    half = n // 2
    return (half,) if n % 2 == 0 and half % LANES == 0 and half <= 1536 else ()


def _matmul(a, b, *, ta=False, tb=False, b_half=None, out_dtype=F32, add=None, gather_cargo=None, name):
    if ta:
        kdim, m = a.shape
    else:
        m, kdim = a.shape
    halves = 1 if b_half is None else 2
    n = b.shape[0] if tb else b.shape[1] // halves
    bm = _tile(m, (1024,) + _half_if_tiled(m) + (512, 256, 128))
    bn = _tile(n, (512, 256, 128))
    if n % MXU_WIDTH == 0 and n <= MATMUL_MAX_SINGLE_K and m % 512 == 0:
        bn, bm = n, 512
    bk = kdim if kdim <= MATMUL_MAX_SINGLE_K else _tile(kdim, (1024, 512, 256, 128))
    nk = kdim // bk

    a_spec = pl.BlockSpec((bk, bm), lambda i, j, k: (k, i)) if ta else pl.BlockSpec((bm, bk), lambda i, j, k: (i, k))
    col0 = 0 if b_half is None else b_half * ((nk if tb else n // bn))
    b_spec = (pl.BlockSpec((bn, bk), lambda i, j, k: (j, k + col0)) if tb
              else pl.BlockSpec((bk, bn), lambda i, j, k: (k, j + col0)))
    o_spec = pl.BlockSpec((bm, bn), lambda i, j, k: (i, j))
    in_specs = [a_spec, b_spec] + ([o_spec] if add is not None else [])
    operands = [a, b] + ([add] if add is not None else [])

    def product(a_ref, b_ref):
        return _dot(a_ref[...].astype(BF16), b_ref[...].astype(BF16), 0 if ta else 1, 1 if tb else 0)

    def finish(r, refs):
        if add is not None:
            r = r + refs[2][...]
        return r

    def body_single(*refs):
        o_ref = refs[-1]
        o_ref[...] = finish(product(refs[0], refs[1]), refs).astype(o_ref.dtype)

    def body_multi(*refs):
        o_ref, acc_ref = refs[-2], refs[-1]
        k = pl.program_id(2)

        @pl.when(k == 0)
        def _():
            acc_ref[...] = jnp.zeros_like(acc_ref)

        acc_ref[...] += product(refs[0], refs[1])

        @pl.when(k == nk - 1)
        def _():
            o_ref[...] = finish(acc_ref[...], refs).astype(o_ref.dtype)

    outs, received = _carrying_call(
        body_single if nk == 1 else body_multi,
        () if gather_cargo is None else gather_cargo,
        gather=True,
        name=name,
        grid=(m // bm, n // bn, nk),
        in_specs=in_specs,
        out_specs=[o_spec],
        out_shape=[jax.ShapeDtypeStruct((m, n), out_dtype)],
        scratch_shapes=[] if nk == 1 else [pltpu.VMEM((bm, bn), F32)],
        operands=operands,
        compiler_params=_params(("parallel", "parallel", "arbitrary") if gather_cargo is None else ("arbitrary",) * 3),
    )
    return outs[0] if gather_cargo is None else (outs[0], received)


def _rows_matmul(a, b, *, tb, add, row_ins, vec_ins, epilogue, row_out_dtypes, n_sums, cargo, gather, name,
                 b_half=None):
    m, kdim = a.shape
    n = b.shape[0] if tb else b.shape[1]
    assert kdim <= MATMUL_MAX_SINGLE_K and (b_half is None or tb)
    b_spec = (pl.BlockSpec(b.shape, lambda i: (0, 0)) if b_half is None
              else pl.BlockSpec((n, kdim), lambda i: (0, b_half)))
    bm = _tile(m, (512, 256, 128))
    row = pl.BlockSpec((bm, n), lambda i: (i, 0))
    vec = pl.BlockSpec((1, n), lambda i: (0, 0))
    n_add = int(add is not None)
    n_ins = len(row_ins) + len(vec_ins)
    n_rows = len(row_out_dtypes)

    def body(a_ref, b_ref, *refs):
        y = _dot(a_ref[...].astype(BF16), b_ref[...].astype(BF16), 1, 1 if tb else 0)
        if add is not None:
            y = y + refs[0][...]
        ins = [r[...] for r in refs[n_add:n_add + n_ins]]
        outs = refs[n_add + n_ins:]
        rows, sums = epilogue(y, *ins)
        for ref, val in zip(outs[:n_rows], rows):
            ref[...] = val.astype(ref.dtype)
        if n_sums:
            @pl.when(pl.program_id(0) == 0)
            def _():
                for ref in outs[n_rows:]:
                    ref[...] = jnp.zeros_like(ref)

            for ref, val in zip(outs[n_rows:], sums):
                ref[...] += val

    return _carrying_call(
        body,
        cargo,
        gather=gather,
        name=name,
        grid=(m // bm,),
        in_specs=[pl.BlockSpec((bm, kdim), lambda i: (i, 0)), b_spec]
        + [row] * (n_add + len(row_ins)) + [vec] * len(vec_ins),
        out_specs=[row] * n_rows + [vec] * n_sums,
        out_shape=[jax.ShapeDtypeStruct((m, n), dt) for dt in row_out_dtypes]
        + [jax.ShapeDtypeStruct((1, n), F32)] * n_sums,
        scratch_shapes=[],
        operands=[a, b] + ([add] if add is not None else []) + list(row_ins) + list(vec_ins),
        compiler_params=_params(("arbitrary",)),
    )


def _matmul_add_norm(a, b, add, g, *, gather_cargo=(), name):
    def epilogue(y, gv):
        h = y * lax.rsqrt(jnp.mean(y * y, axis=-1, keepdims=True) + EPS) * gv
        return (y, h), ()

    return _rows_matmul(a, b, tb=False, add=add, row_ins=(), vec_ins=(g,), epilogue=epilogue,
                        row_out_dtypes=(F32, BF16), n_sums=0, cargo=gather_cargo, gather=True, name=name)


def _matmul_norm_bwd(a, b, x, g, res, *, b_half=None, add=None, cargo=(), name):
    def epilogue(dyv, xv, resv, gv):
        rs = lax.rsqrt(jnp.mean(xv * xv, axis=-1, keepdims=True) + EPS)
        xn = xv * rs
        dyg = dyv * gv
        dx = rs * (dyg - xn * jnp.mean(dyg * xn, axis=-1, keepdims=True)) + resv
        return (dx, dx), (jnp.sum(dyv * xn, axis=0, keepdims=True),)

    return _rows_matmul(a, b, tb=True, b_half=b_half, add=add, row_ins=(x, res), vec_ins=(g,), epilogue=epilogue,
                        row_out_dtypes=(F32, BF16), n_sums=1, cargo=cargo, gather=False, name=name)


def _weight_grad(x, dy, *, name):
    return _matmul(x, dy, ta=True, out_dtype=BF16, name=name)


def _rmsnorm_fwd(x, g, *, name):
    r, d = x.shape
    tr = _tile(r, (1024, 512, 256, 128))

    def body(x_ref, g_ref, o_ref):
        xv = x_ref[...]
        y = xv * lax.rsqrt(jnp.mean(xv * xv, axis=-1, keepdims=True) + EPS)
        y = y * g_ref[...]
        o_ref[...] = y.astype(o_ref.dtype)

    return pl.pallas_call(
        body,
        name=name,
        grid=(r // tr,),
        in_specs=[pl.BlockSpec((tr, d), lambda i: (i, 0)), pl.BlockSpec((1, d), lambda i: (0, 0))],
        out_specs=pl.BlockSpec((tr, d), lambda i: (i, 0)),
        out_shape=jax.ShapeDtypeStruct((r, d), BF16),
        compiler_params=_params(("parallel",)),
    )(x, g)


def _loss_head(y, target, *, name):
    s, d = y.shape
    ts = _tile(s, (512, 256, 128))
    row = pl.BlockSpec((ts, d), lambda i: (i, 0))
    acc = pl.BlockSpec((8, 128), lambda i: (0, 0))

    def body(y_ref, t_ref, l_ref, dy_ref, dy16_ref):
        e = y_ref[...] - t_ref[...]
        dy = e * (1.0 / d)
        dy_ref[...] = dy
        dy16_ref[...] = dy.astype(dy16_ref.dtype)

        @pl.when(pl.program_id(0) == 0)
        def _():
            l_ref[...] = jnp.zeros_like(l_ref)

        sq = jnp.sum(jnp.sum(e * e, axis=-1, keepdims=True), axis=0, keepdims=True)
        l_ref[...] += jnp.broadcast_to(sq, l_ref.shape)

    return pl.pallas_call(
        body,
        name=name,
        grid=(s // ts,),
        in_specs=[row, row],
        out_specs=(acc, row, row),
        out_shape=(jax.ShapeDtypeStruct((8, 128), F32), jax.ShapeDtypeStruct((s, d), F32),
                   jax.ShapeDtypeStruct((s, d), BF16)),
        compiler_params=_params(("arbitrary",)),
    )(y, target)


def _ffn_tiles(s, f):
    return _tile(s, (512, 256, 128)), _tile(f, _half_if_tiled(f) + (512, 256, 128))


def _conv3(xs, w, b):
    return b + xs[0] * w[0:1] + xs[1] * w[1:2] + xs[2] * w[2:3]


def _shifted3(x, rows):
    return [x[S_HALO - 2 + k:S_HALO - 2 + k + rows] for k in range(3)]


def _ffn_act_fwd(ug, uv, wg, wv, bg, bv, *, gather_cargo=(), name):
    s, f = ug.shape
    ts, tc = _ffn_tiles(s, f)
    rc = _tile(ts, (FFN_CHUNK_ROWS, 32, 16, 8))
    hb = ts // FFN_HALO_BLOCK
    main = pl.BlockSpec((ts, tc), lambda j, i: (i, j))
    prev = pl.BlockSpec((FFN_HALO_BLOCK, tc), lambda j, i: (jnp.maximum(i * hb - 1, 0), j))
    taps = pl.BlockSpec((3, tc), lambda j, i: (0, j))
    bias = pl.BlockSpec((1, tc), lambda j, i: (0, j))
    lead = FFN_HALO_BLOCK - S_HALO

    def body(ug_ref, ugp_ref, uv_ref, uvp_ref, wg_ref, wv_ref, bg_ref, bv_ref, o_ref, gbuf, vbuf):
        first = pl.program_id(1) == 0
        for buf, p_ref, m_ref in ((gbuf, ugp_ref, ug_ref), (vbuf, uvp_ref, uv_ref)):
            buf[0:FFN_HALO_BLOCK, :] = jnp.where(first, 0.0, p_ref[...].astype(F32))
            buf[FFN_HALO_BLOCK:, :] = m_ref[...].astype(F32)

        def chunk(ci, carry):
            r0 = pl.multiple_of(ci * rc, rc)
            rows = pl.ds(pl.multiple_of(r0 + lead, S_HALO), rc + S_HALO)
            for lb in range(tc // LANES):
                ls = pl.ds(lb * LANES, LANES)
                gate = _conv3(_shifted3(gbuf[rows, ls], rc), wg_ref[:, ls], bg_ref[:, ls])
                val = _conv3(_shifted3(vbuf[rows, ls], rc), wv_ref[:, ls], bv_ref[:, ls])
                o_ref[pl.ds(r0, rc), ls] = (gate * _sigmoid(gate) * val).astype(o_ref.dtype)
            return carry

        lax.fori_loop(0, ts // rc, chunk, 0)

    outs, gathered = _carrying_call(
        body,
        gather_cargo,
        gather=True,
        name=name,
        grid=(f // tc, s // ts),
        in_specs=[main, prev, main, prev, taps, taps, bias, bias],
        out_specs=[main],
        out_shape=[jax.ShapeDtypeStruct((s, f), BF16)],
        scratch_shapes=[pltpu.VMEM((FFN_HALO_BLOCK + ts, tc), F32), pltpu.VMEM((FFN_HALO_BLOCK + ts, tc), F32)],
        operands=(ug, ug, uv, uv, wg, wv, bg, bv),
        compiler_params=_params(("arbitrary", "arbitrary")),
    )
    return outs[0], gathered


def _ffn_act_bwd(ug, uv, df, wg, wv, bg, bv, *, cargo=(), name):
    s, f = ug.shape
    ts, tc = _ffn_tiles(s, f)
    rc = _tile(ts, (FFN_CHUNK_ROWS, 32, 16, 8))
    hb = ts // FFN_HALO_BLOCK
    last_hb = s // FFN_HALO_BLOCK - 1
    n_row = s // ts
    lead = FFN_HALO_BLOCK - S_HALO
    main = pl.BlockSpec((ts, tc), lambda j, i: (i, j))
    prev = pl.BlockSpec((FFN_HALO_BLOCK, tc), lambda j, i: (jnp.maximum(i * hb - 1, 0), j))
    nxt = pl.BlockSpec((FFN_HALO_BLOCK, tc), lambda j, i: (jnp.minimum((i + 1) * hb, last_hb), j))
    taps = pl.BlockSpec((3, tc), lambda j, i: (0, j))
    bias = pl.BlockSpec((1, tc), lambda j, i: (0, j))
    sums = pl.BlockSpec((32, tc), lambda j, i: (0, j))
    ext = rc + S_HALO

    def body(ug_ref, ugp_ref, ugn_ref, uv_ref, uvp_ref, uvn_ref, df_ref, dfn_ref, wg_ref, wv_ref, bg_ref, bv_ref,
             dug_ref, duv_ref, sumg_ref, sumv_ref, gbuf, vbuf, dfbuf):
        i = pl.program_id(1)
        first = i == 0
        last = i == n_row - 1

        @pl.when(first)
        def _():
            sumg_ref[...] = jnp.zeros_like(sumg_ref)
            sumv_ref[...] = jnp.zeros_like(sumv_ref)

        for buf, p_ref, m_ref, n_ref in ((gbuf, ugp_ref, ug_ref, ugn_ref), (vbuf, uvp_ref, uv_ref, uvn_ref)):
            buf[0:FFN_HALO_BLOCK, :] = jnp.where(first, 0.0, p_ref[...].astype(F32))
            buf[FFN_HALO_BLOCK:FFN_HALO_BLOCK + ts, :] = m_ref[...].astype(F32)
            buf[FFN_HALO_BLOCK + ts:, :] = n_ref[...].astype(F32)
        dfbuf[0:ts, :] = df_ref[...].astype(F32)
        dfbuf[ts:, :] = jnp.where(last, 0.0, dfn_ref[...].astype(F32))

        def chunk(ci, carry):
            r0 = pl.multiple_of(ci * rc, rc)
            rows = pl.ds(pl.multiple_of(r0 + lead, S_HALO), ext + S_HALO)
            for lb in range(tc // LANES):
                ls = pl.ds(lb * LANES, LANES)
                xg = _shifted3(gbuf[rows, ls], ext)
                xv = _shifted3(vbuf[rows, ls], ext)
                wgv, wvv = wg_ref[:, ls], wv_ref[:, ls]
                gate = _conv3(xg, wgv, bg_ref[:, ls])
                val = _conv3(xv, wvv, bv_ref[:, ls])
                dfe = dfbuf[pl.ds(r0, ext), ls]
                sg = _sigmoid(gate)
                dgate = dfe * val * (sg * (1.0 + gate * (1.0 - sg)))
                dval = dfe * (gate * sg)
                for dz, xs, w, du_ref, sum_ref in ((dgate, xg, wgv, dug_ref, sumg_ref),
                                                   (dval, xv, wvv, duv_ref, sumv_ref)):
                    du = dz[2:2 + rc] * w[0:1] + dz[1:1 + rc] * w[1:2] + dz[0:rc] * w[2:3]
                    du_ref[pl.ds(r0, rc), ls] = du.astype(du_ref.dtype)
                    dm = dz[0:rc]
                    for k in range(3):
                        sum_ref[8 * k:8 * k + 8, ls] += (dm * xs[k][0:rc]).reshape(rc // 8, 8, LANES).sum(axis=0)
                    sum_ref[24:32, ls] += dm.reshape(rc // 8, 8, LANES).sum(axis=0)
            return carry

        lax.fori_loop(0, ts // rc, chunk, 0)

    return _carrying_call(
        body,
        cargo,
        name=name,
        grid=(f // tc, n_row),
        in_specs=[main, prev, nxt, main, prev, nxt, main, nxt, taps, taps, bias, bias],
        out_specs=(main, main, sums, sums),
        out_shape=(jax.ShapeDtypeStruct((s, f), BF16), jax.ShapeDtypeStruct((s, f), BF16),
                   jax.ShapeDtypeStruct((32, f), F32), jax.ShapeDtypeStruct((32, f), F32)),
        scratch_shapes=[pltpu.VMEM((2 * FFN_HALO_BLOCK + ts, tc), F32), pltpu.VMEM((2 * FFN_HALO_BLOCK + ts, tc), F32),
                        pltpu.VMEM((FFN_HALO_BLOCK + ts, tc), F32)],
        operands=(ug, ug, ug, uv, uv, uv, df, df, wg, wv, bg, bv),
        compiler_params=_params(("arbitrary", "arbitrary")),
    )


def _layernorm_stats(a1):
    mu = jnp.mean(a1, axis=-1, keepdims=True)
    xc = a1 - mu
    rstd = lax.rsqrt(jnp.mean(xc * xc, axis=-1, keepdims=True) + EPS)
    return xc * rstd, rstd


def _conv_act_fwd(p, wa, ba, lng, lnb, wb, *, gather_cargo=(), name):
    s, c5 = p.shape
    c = c5 // 5
    ka, kb = wa.shape[0], wb.shape[0]
    ts = _tile(s, (256, 128))
    hb = ts // A_HALO
    main = pl.BlockSpec((ts, c5), lambda i: (i, 0))
    prev = pl.BlockSpec((A_HALO, c5), lambda i: (jnp.maximum(i * hb - 1, 0), 0))

    def full(arr):
        return pl.BlockSpec(arr.shape, lambda i: (0, 0))

    def body(p_ref, pp_ref, wa_ref, ba_ref, lng_ref, lnb_ref, wb_ref, ab_ref, a1_ref, gbuf, cbuf):
        first = pl.program_id(0) == 0
        gbuf[0:A_HALO, :] = jnp.where(first, 0.0, pp_ref[:, 0:c] * _sigmoid(pp_ref[:, c:2 * c]))
        gbuf[A_HALO:, :] = p_ref[:, 0:c] * _sigmoid(p_ref[:, c:2 * c])
        cbuf[0:A_HALO, :] = jnp.where(first, 0.0, pp_ref[:, 3 * c:4 * c] * pp_ref[:, 4 * c:5 * c])
        cbuf[A_HALO:, :] = p_ref[:, 3 * c:4 * c] * p_ref[:, 4 * c:5 * c]

        a1 = jnp.broadcast_to(ba_ref[...], (ts, c))
        for k in range(ka):
            off = A_HALO - (ka - 1) + k
            a1 = a1 + gbuf[off:off + ts, :] * wa_ref[k:k + 1, :]
        a1_ref[...] = a1
        xhat, _ = _layernorm_stats(a1)
        a2 = xhat * lng_ref[...] + lnb_ref[...]
        ab_ref[:, 0:c] = (a2 * _sigmoid(a2)).astype(ab_ref.dtype)

        cv = jnp.zeros((ts, c), F32)
        for k in range(kb):
            off = A_HALO - (kb - 1) + k
            cv = cv + cbuf[off:off + ts, :] * wb_ref[k:k + 1, :]
        ab_ref[:, c:2 * c] = (p_ref[:, 2 * c:3 * c] * cv).astype(ab_ref.dtype)

    return _carrying_call(
        body,
        gather_cargo,
        gather=True,
        name=name,
        grid=(s // ts,),
        in_specs=[main, prev, full(wa), full(ba), full(lng), full(lnb), full(wb)],
        out_specs=(pl.BlockSpec((ts, 2 * c), lambda i: (i, 0)), pl.BlockSpec((ts, c), lambda i: (i, 0))),
        out_shape=(jax.ShapeDtypeStruct((s, 2 * c), BF16), jax.ShapeDtypeStruct((s, c), F32)),
        scratch_shapes=[pltpu.VMEM((A_HALO + ts, c), F32), pltpu.VMEM((A_HALO + ts, c), F32)],
        operands=(p, p, wa, ba, lng, lnb, wb),
        compiler_params=_params(("arbitrary",)),
    )


def _conv_act_bwd(p, a1, dab, wa, lng, lnb, wb, *, cargo=(), name):
    s, c5 = p.shape
    c = c5 // 5
    ka, kb = wa.shape[0], wb.shape[0]
    ts = _tile(s, (128,))
    hb = ts // A_HALO
    n_row = s // ts
    last_hb = s // A_HALO - 1
    ext = ts + A_HALO

    def rows(width):
        return (pl.BlockSpec((ts, width), lambda i: (i, 0)),
                pl.BlockSpec((A_HALO, width), lambda i: (jnp.maximum(i * hb - 1, 0), 0)),
                pl.BlockSpec((A_HALO, width), lambda i: (jnp.minimum((i + 1) * hb, last_hb), 0)))

    p_main, p_prev, p_next = rows(c5)
    d_main, _, d_next = rows(2 * c)
    a_main, _, a_next = rows(c)

    def full(shape):
        return pl.BlockSpec(shape, lambda i: (0, 0))

    def body(p_ref, pp_ref, pn_ref, a1_ref, a1n_ref, d_ref, dn_ref, wa_ref, lng_ref, lnb_ref, wb_ref,
             dp_ref, dwa_ref, dba_ref, dlng_ref, dlnb_ref, dwb_ref, gbuf, cbuf, da1buf, dcvbuf):
        i = pl.program_id(0)
        first = i == 0
        last = i == n_row - 1

        @pl.when(first)
        def _():
            dwa_ref[...] = jnp.zeros_like(dwa_ref)
            dba_ref[...] = jnp.zeros_like(dba_ref)
            dlng_ref[...] = jnp.zeros_like(dlng_ref)
            dlnb_ref[...] = jnp.zeros_like(dlnb_ref)
            dwb_ref[...] = jnp.zeros_like(dwb_ref)

        sig_gate = _sigmoid(p_ref[:, c:2 * c])
        gbuf[0:A_HALO, :] = jnp.where(first, 0.0, pp_ref[:, 0:c] * _sigmoid(pp_ref[:, c:2 * c]))
        gbuf[A_HALO:, :] = p_ref[:, 0:c] * sig_gate
        cbuf[0:A_HALO, :] = jnp.where(first, 0.0, pp_ref[:, 3 * c:4 * c] * pp_ref[:, 4 * c:5 * c])
        cbuf[A_HALO:, :] = p_ref[:, 3 * c:4 * c] * p_ref[:, 4 * c:5 * c]

        def ln_back(a1v, dav):
            xhat, rstd = _layernorm_stats(a1v)
            a2 = xhat * lng_ref[...] + lnb_ref[...]
            sg = _sigmoid(a2)
            da2 = dav * (sg * (1.0 + a2 * (1.0 - sg)))
            dxh = da2 * lng_ref[...]
            da1 = rstd * (dxh - jnp.mean(dxh, axis=-1, keepdims=True)
                          - xhat * jnp.mean(dxh * xhat, axis=-1, keepdims=True))
            return da1, da2, xhat

        da1_m, da2_m, xhat_m = ln_back(a1_ref[...], d_ref[:, 0:c])
        da1_n, _, _ = ln_back(a1n_ref[...], dn_ref[:, 0:c])
        da1buf[0:ts, :] = da1_m
        da1buf[ts:, :] = jnp.where(last, 0.0, da1_n)
        dlng_ref[...] += jnp.sum(da2_m * xhat_m, axis=0, keepdims=True)
        dlnb_ref[...] += jnp.sum(da2_m, axis=0, keepdims=True)
        dba_ref[...] += jnp.sum(da1_m, axis=0, keepdims=True)

        dglu = jnp.zeros((ts, c), F32)
        for k in range(ka):
            off = (ka - 1) - k
            dglu = dglu + da1buf[off:off + ts, :] * wa_ref[k:k + 1, :]
            goff = A_HALO - (ka - 1) + k
            dwa_ref[k:k + 1, :] += jnp.sum(da1_m * gbuf[goff:goff + ts, :], axis=0, keepdims=True)
        dp_ref[:, 0:c] = (dglu * sig_gate).astype(dp_ref.dtype)
        dp_ref[:, c:2 * c] = (dglu * p_ref[:, 0:c] * sig_gate * (1.0 - sig_gate)).astype(dp_ref.dtype)

        cv = jnp.zeros((ts, c), F32)
        for k in range(kb):
            off = A_HALO - (kb - 1) + k
            cv = cv + cbuf[off:off + ts, :] * wb_ref[k:k + 1, :]
        db_m = d_ref[:, c:2 * c]
        dp_ref[:, 2 * c:3 * c] = (db_m * cv).astype(dp_ref.dtype)
        dcv_m = db_m * p_ref[:, 2 * c:3 * c]
        dcvbuf[0:ts, :] = dcv_m
        dcvbuf[ts:, :] = jnp.where(last, 0.0, dn_ref[:, c:2 * c] * pn_ref[:, 2 * c:3 * c])
        dbc = jnp.zeros((ts, c), F32)
        for k in range(kb):
            off = (kb - 1) - k
            dbc = dbc + dcvbuf[off:off + ts, :] * wb_ref[k:k + 1, :]
            coff = A_HALO - (kb - 1) + k
            dwb_ref[k:k + 1, :] += jnp.sum(dcv_m * cbuf[coff:coff + ts, :], axis=0, keepdims=True)
        dp_ref[:, 3 * c:4 * c] = (dbc * p_ref[:, 4 * c:5 * c]).astype(dp_ref.dtype)
        dp_ref[:, 4 * c:5 * c] = (dbc * p_ref[:, 3 * c:4 * c]).astype(dp_ref.dtype)

    return _carrying_call(
        body,
        cargo,
        name=name,
        grid=(n_row,),
        in_specs=[p_main, p_prev, p_next, a_main, a_next, d_main, d_next,
                  full(wa.shape), full(lng.shape), full(lnb.shape), full(wb.shape)],
        out_specs=(pl.BlockSpec((ts, c5), lambda i: (i, 0)), full((32, c)), full((1, c)), full((1, c)),
                   full((1, c)), full((8, c))),
        out_shape=(
            jax.ShapeDtypeStruct((s, c5), BF16), jax.ShapeDtypeStruct((32, c), F32),
            jax.ShapeDtypeStruct((1, c), F32), jax.ShapeDtypeStruct((1, c), F32),
            jax.ShapeDtypeStruct((1, c), F32), jax.ShapeDtypeStruct((8, c), F32),
        ),
        scratch_shapes=[
            pltpu.VMEM((A_HALO + ts, c), F32), pltpu.VMEM((A_HALO + ts, c), F32),
            pltpu.VMEM((ext, c), F32), pltpu.VMEM((ext, c), F32),
        ],
        operands=(p, p, p, a1, a1, dab, dab, wa, lng, lnb, wb),
        compiler_params=_params(("arbitrary",)),
    )


def _head_pair_geometry(width, dh):
    assert 2 * dh == LANES, "the attention kernels pair two heads in one 128-lane block"
    return width // (3 * LANES)


def _group_sum(v, lo):
    s_lo = jnp.sum(jnp.where(lo, v, 0.0), axis=-1, keepdims=True)
    s_hi = jnp.sum(jnp.where(lo, 0.0, v), axis=-1, keepdims=True)
    return jnp.where(lo, s_lo, s_hi)


def _qkv_prep_fwd(qkv, gains, dh, *, name):
    s, width = qkv.shape
    nsec = _head_pair_geometry(width, dh)
    tr = _tile(s, (512, 256, 128))
    blk = pl.BlockSpec((tr, nsec * LANES), lambda i, sec: (i, sec))

    def body(x_ref, g_ref, o_ref):
        sec = pl.program_id(1)

        @pl.when(sec == 2)
        def _():
            o_ref[...] = x_ref[...].astype(o_ref.dtype)

        @pl.when(sec != 2)
        def _():
            lo = lax.broadcasted_iota(jnp.int32, (1, LANES), 1) < dh
            for lb in range(nsec):
                ls = pl.ds(lb * LANES, LANES)
                xv = x_ref[:, ls].astype(F32)
                rs = lax.rsqrt(_group_sum(xv * xv, lo) * (1.0 / dh) + EPS)
                o_ref[:, ls] = (xv * rs * g_ref[0:1, :]).astype(o_ref.dtype)

    return pl.pallas_call(
        body,
        name=name,
        grid=(s // tr, 3),
        in_specs=[blk, pl.BlockSpec((8, LANES), lambda i, sec: (sec, 0))],
        out_specs=blk,
        out_shape=jax.ShapeDtypeStruct((s, width), BF16),
        compiler_params=_params(("parallel", "parallel")),
    )(qkv, gains)


def _qkv_prep_bwd(qkv, dq, dk, dv, gains, dh, *, name):
    s, width = qkv.shape
    nsec = _head_pair_geometry(width, dh)
    tr = _tile(s, (512, 256, 128))
    blk = pl.BlockSpec((tr, nsec * LANES), lambda sec, i: (i, sec))
    gspec = pl.BlockSpec((8, LANES), lambda sec, i: (sec, 0))

    def grad_spec(own):
        return pl.BlockSpec((tr, nsec * LANES), lambda sec, i: (jnp.where(sec == own, i, 0), 0))

    def body(x_ref, dq_ref, dk_ref, dv_ref, g_ref, o_ref, dg_ref):
        sec = pl.program_id(0)

        @pl.when(pl.program_id(1) == 0)
        def _():
            dg_ref[...] = jnp.zeros_like(dg_ref)

        def normed(d_ref):
            lo = lax.broadcasted_iota(jnp.int32, (1, LANES), 1) < dh
            dg = jnp.zeros((1, LANES), F32)
            for lb in range(nsec):
                ls = pl.ds(lb * LANES, LANES)
                xv = x_ref[:, ls].astype(F32)
                d = d_ref[:, ls]
                rs = lax.rsqrt(_group_sum(xv * xv, lo) * (1.0 / dh) + EPS)
                xn = xv * rs
                dyg = d * g_ref[0:1, :]
                dx = rs * (dyg - xn * (_group_sum(dyg * xn, lo) * (1.0 / dh)))
                o_ref[:, ls] = dx.astype(o_ref.dtype)
                dg = dg + jnp.sum(d * xn, axis=0, keepdims=True)
            dg_ref[...] += jnp.broadcast_to(dg, dg_ref.shape)

        @pl.when(sec == 0)
        def _():
            normed(dq_ref)

        @pl.when(sec == 1)
        def _():
            normed(dk_ref)

        @pl.when(sec == 2)
        def _():
            o_ref[...] = dv_ref[...].astype(o_ref.dtype)

    return pl.pallas_call(
        body,
        name=name,
        grid=(3, s // tr),
        in_specs=[blk, grad_spec(0), grad_spec(1), grad_spec(2), gspec],
        out_specs=(blk, gspec),
        out_shape=(jax.ShapeDtypeStruct((s, width), BF16), jax.ShapeDtypeStruct((24, LANES), F32)),
        compiler_params=_params(("arbitrary", "arbitrary")),
    )(qkv, dq, dk, dv, gains)


def _split_dot(v, tri):
    hi = v.astype(BF16)
    lo = (v - hi.astype(F32)).astype(BF16)
    return _dot(hi, tri, 1, 0) + _dot(lo, tri, 1, 0)


def _attn_scores(qm, kb, tri_gt, valid, r_run):
    z = _dot(qm, kb, 1, 1)
    zl = jnp.minimum(z, 0.0) - jnp.log(1.0 + jnp.exp(-jnp.abs(z)))
    lk = zl - z
    if valid is not None:
        lk = jnp.where(valid, lk, 0.0)
    after = _split_dot(lk, tri_gt)
    w = jnp.exp(zl + after + r_run)
    if valid is not None:
        w = jnp.where(valid, w, 0.0)
    return zl, lk, w


def _attn_alive(rr):
    return jnp.max(jnp.maximum(rr[0], rr[1])) > ATTN_DEAD_LOG_WEIGHT


def _attn_walk_variants(i, walk):
    @pl.when(i == 0)
    def _():
        walk(False)

    @pl.when(i > 0)
    def _():
        walk(True)


def _attn_specs(s, width, dh, tq):
    nsec = _head_pair_geometry(width, dh)
    qspec = pl.BlockSpec((tq, LANES), lambda h, i: (i, h))
    kspec = pl.BlockSpec((s, LANES), lambda h, i: (0, nsec + h))
    vspec = pl.BlockSpec((s, LANES), lambda h, i: (0, 2 * nsec + h))
    return nsec, qspec, kspec, vspec


def _attn_masks(tq, tk, dh):
    lo = lax.broadcasted_iota(jnp.int32, (1, LANES), 1) < dh
    r_io = lax.broadcasted_iota(jnp.int32, (tk, tk), 0)
    c_io = lax.broadcasted_iota(jnp.int32, (tk, tk), 1)
    qrow = lax.broadcasted_iota(jnp.int32, (tq, tk), 0)
    kcol = lax.broadcasted_iota(jnp.int32, (tq, tk), 1)
    return lo, r_io, c_io, qrow, kcol


def _attn_fwd(qkv16, dh, *, gather_cargo=(), name):
    s, width = qkv16.shape
    tq = _tile(s, (ATTN_Q_BLOCK, ATTN_K_BLOCK))
    tk = _tile(tq, (ATTN_K_BLOCK,))
    nd = tq // tk
    nsec, qspec, kspec, vspec = _attn_specs(s, width, dh, tq)

    def body(q_ref, k_ref, v_ref, o_ref):
        i = pl.program_id(1)
        lo, r_io, c_io, qrow, kcol = _attn_masks(tq, tk, dh)
        q2 = q_ref[...]
        zq = jnp.zeros_like(q2)
        qs = (jnp.where(lo, q2, zq), jnp.where(lo, zq, q2))
        tri_gt = (r_io > c_io).astype(BF16)

        def block(j, carry, diag):
            rr, acc = carry
            start = pl.multiple_of(j * tk, tk)
            kb = k_ref[pl.ds(start, tk), :]
            vb = v_ref[pl.ds(start, tk), :]
            valid = None if diag is None else kcol + diag * tk < qrow
            outs, new_r = [], []
            for hh in range(2):
                _, lk, w = _attn_scores(qs[hh], kb, tri_gt, valid, rr[hh])
                outs.append(_dot(w.astype(BF16), vb, 1, 0))
                new_r.append(rr[hh] + jnp.sum(lk, axis=-1, keepdims=True))
            return tuple(new_r), acc + jnp.where(lo, outs[0], outs[1])

        below = i * nd

        def step(st):
            jj, _, cr = st
            cr = block(below - 1 - jj, cr, None)
            return jj + 1, _attn_alive(cr[0]), cr

        def walk(inline_first_below):
            zero = jnp.zeros((tq, 1), F32)
            carry = ((zero, zero), jnp.zeros((tq, LANES), F32))
            for dg in reversed(range(nd)):
                carry = block(i * nd + dg, carry, dg)
            if inline_first_below:
                carry = block(below - 1, carry, None)
            _, _, carry = lax.while_loop(lambda st: (st[0] < below) & st[1], step,
                                         (jnp.int32(int(inline_first_below)), _attn_alive(carry[0]), carry))
            o_ref[...] = carry[1]

        _attn_walk_variants(i, walk)

    outs, gathered = _carrying_call(
        body,
        gather_cargo,
        gather=True,
        name=name,
        grid=(nsec, s // tq),
        in_specs=[qspec, kspec, vspec],
        out_specs=[qspec],
        out_shape=[jax.ShapeDtypeStruct((s, width // 3), F32)],
        scratch_shapes=[],
        operands=(qkv16, qkv16, qkv16),
        compiler_params=_params(("arbitrary", "arbitrary")),
    )
    return outs[0], gathered


def _attn_bwd(qkv16, o, do, dh, *, cargo=(), name):
    s, width = qkv16.shape
    tq = _tile(s, (ATTN_Q_BLOCK, ATTN_K_BLOCK))
    tk = _tile(tq, (ATTN_K_BLOCK,))
    nd = tq // tk
    nsec, qspec, kspec, vspec = _attn_specs(s, width, dh, tq)
    accspec = pl.BlockSpec((s, LANES), lambda h, i: (0, h))

    def body(q_ref, k_ref, v_ref, o_ref, do_ref, dq_ref, dk_ref, dv_ref):
        i = pl.program_id(1)

        @pl.when(i == 0)
        def _():
            dk_ref[...] = jnp.zeros_like(dk_ref)
            dv_ref[...] = jnp.zeros_like(dv_ref)

        lo, r_io, c_io, qrow, kcol = _attn_masks(tq, tk, dh)
        q2 = q_ref[...]
        zq = jnp.zeros_like(q2)
        qs = (jnp.where(lo, q2, zq), jnp.where(lo, zq, q2))
        do16 = do_ref[...].astype(BF16)
        dos = (jnp.where(lo, do16, zq), jnp.where(lo, zq, do16))
        prod = do16.astype(F32) * o_ref[...]
        deltas = (jnp.sum(jnp.where(lo, prod, 0.0), axis=-1, keepdims=True),
                  jnp.sum(jnp.where(lo, 0.0, prod), axis=-1, keepdims=True))
        tri_gt = (r_io > c_io).astype(BF16)
        tri_ge = (r_io >= c_io).astype(BF16)

        def block(j, carry, diag):
            rr, er, dq = carry
            start = pl.multiple_of(j * tk, tk)
            kb = k_ref[pl.ds(start, tk), :]
            vb = v_ref[pl.ds(start, tk), :]
            valid = None if diag is None else kcol + diag * tk < qrow
            new_r, new_e, dqs = [], [], []
            dk_blk = dv_blk = None
            for hh in range(2):
                zl, lk, w = _attn_scores(qs[hh], kb, tri_gt, valid, rr[hh])
                beta = jnp.exp(zl)
                w16 = w.astype(BF16)
                e = _dot(dos[hh], vb, 1, 1) * w16.astype(F32)
                e_before = deltas[hh] - er[hh] - _split_dot(e, tri_ge)
                dz = e - beta * (e + e_before)
                if valid is not None:
                    dz = jnp.where(valid, dz, 0.0)
                dz16 = dz.astype(BF16)
                dqs.append(_dot(dz16, kb, 1, 0))
                dkc = _dot(dz16, qs[hh], 0, 0)
                dvc = _dot(w16, dos[hh], 0, 0)
                dk_blk = dkc if dk_blk is None else dk_blk + dkc
                dv_blk = dvc if dv_blk is None else dv_blk + dvc
                new_r.append(rr[hh] + jnp.sum(lk, axis=-1, keepdims=True))
                new_e.append(er[hh] + jnp.sum(e, axis=-1, keepdims=True))
            dk_ref[pl.ds(start, tk), :] += dk_blk
            dv_ref[pl.ds(start, tk), :] += dv_blk
            return tuple(new_r), tuple(new_e), dq + jnp.where(lo, dqs[0], dqs[1])

        below = i * nd

        def step(st):
            jj, _, cr = st
            cr = block(below - 1 - jj, cr, None)
            return jj + 1, _attn_alive(cr[0]), cr

        def walk(inline_first_below):
            zero = jnp.zeros((tq, 1), F32)
            carry = ((zero, zero), (zero, zero), jnp.zeros((tq, LANES), F32))
            for dg in reversed(range(nd)):
                carry = block(i * nd + dg, carry, dg)
            if inline_first_below:
                carry = block(below - 1, carry, None)
            _, _, carry = lax.while_loop(lambda st: (st[0] < below) & st[1], step,
                                         (jnp.int32(int(inline_first_below)), _attn_alive(carry[0]), carry))
            dq_ref[...] = carry[2]

        _attn_walk_variants(i, walk)

    shp = jax.ShapeDtypeStruct((s, width // 3), F32)
    return _carrying_call(
        body,
        cargo,
        name=name,
        grid=(nsec, s // tq),
        in_specs=[qspec, kspec, vspec, qspec, qspec],
        out_specs=(qspec, accspec, accspec),
        out_shape=(shp, shp, shp),
        scratch_shapes=[],
        operands=(qkv16, qkv16, qkv16, o, do),
        compiler_params=_params(("arbitrary", "arbitrary")),
    )


def _adam_update(wv, gv, mv, vv):
    c1 = 1.0 - ADAM_B1 ** ADAM_STEP
    c2 = 1.0 - ADAM_B2 ** ADAM_STEP
    nm = ADAM_B1 * mv + (1.0 - ADAM_B1) * gv
    nv = ADAM_B2 * vv + (1.0 - ADAM_B2) * (gv * gv)
    delta = -ADAM_LR * ((nm / c1) / (jnp.sqrt(nv / c2) + ADAM_EPS) + ADAM_WD * wv)
    return delta, nm, nv


def _sum_slabs(r, *, name):
    n, rows, lanes = r.shape
    tr = _tile(rows, (1024, 512, 256, 128, 64, 32, 16))

    def body(r_ref, o_ref):
        acc = r_ref[0]
        for d in range(1, n):
            acc = acc + r_ref[d]
        o_ref[...] = acc

    return pl.pallas_call(
        body,
        name=name,
        grid=(rows // tr,),
        in_specs=[pl.BlockSpec((n, tr, lanes), lambda i: (0, i, 0))],
        out_specs=pl.BlockSpec((tr, lanes), lambda i: (i, 0)),
        out_shape=jax.ShapeDtypeStruct((rows, lanes), F32),
        compiler_params=_params(("parallel",)),
    )(r)


def _adamw(w, g, m, v, *, name):
    rows, cols = w.shape
    tr = _tile(rows, (1024, 512, 256, 128, 64, 32, 16))
    spec = pl.BlockSpec((tr, cols), lambda i: (i, 0))

    def body(w_ref, g_ref, m_ref, v_ref, d_ref, nm_ref, nv_ref):
        d_ref[...], nm_ref[...], nv_ref[...] = _adam_update(w_ref[...], g_ref[...], m_ref[...], v_ref[...])

    shp = jax.ShapeDtypeStruct((rows, cols), F32)
    return pl.pallas_call(
        body,
        name=name,
        grid=(rows // tr,),
        in_specs=[spec, spec, spec, spec],
        out_specs=(spec, spec, spec),
        out_shape=(shp, shp, shp),
        compiler_params=_params(("parallel",)),
    )(w, g, m, v)


def _adamw_sum(w, received, m, v, *, name):
    layers, rows, cols = w.shape
    n = received[0].shape[0]
    tr = _tile(rows, (256, 128, 64, 32, 16))
    spec = pl.BlockSpec((1, tr, cols), lambda li, i: (li, i, 0))

    def partial_spec(own):
        return pl.BlockSpec((n, tr, cols), lambda li, i: (0, jnp.where(li == own, i, 0), 0))

    def body(w_ref, *refs):
        r_refs, (m_ref, v_ref, g_ref, d_ref, nm_ref, nv_ref) = refs[:layers], refs[layers:]
        for own in range(layers):
            @pl.when(pl.program_id(0) == own)
            def _(r_ref=r_refs[own]):
                gv = r_ref[0].astype(F32)
                for d in range(1, n):
                    gv = gv + r_ref[d].astype(F32)
                g_ref[0] = gv
                d_ref[0], nm_ref[0], nv_ref[0] = _adam_update(w_ref[0], gv, m_ref[0], v_ref[0])

    shp = jax.ShapeDtypeStruct((layers, rows, cols), F32)
    return pl.pallas_call(
        body,
        name=name,
        grid=(layers, rows // tr),
        in_specs=[spec] + [partial_spec(own) for own in range(layers)] + [spec, spec],
        out_specs=(spec, spec, spec, spec),
        out_shape=(shp, shp, shp, shp),
        compiler_params=_params(("arbitrary", "arbitrary")),
    )(w, *received, m, v)


_MATMUL_WEIGHTS = (("conv_w_in", 2), ("conv_w_out", 1), ("attn_w_qkv", 2), ("attn_w_o", 1), ("ffn_w_up", 2),
                   ("ffn_w_down", 1))
_TAP_WEIGHTS = (("conv_a_dw_w", 2), ("conv_b_dw_w", 2), ("ffn_dw_w", 2))
_REPLICATED = ("mix_norm_g", "ffn_norm_g", "conv_a_dw_b", "conv_a_ln_g", "conv_a_ln_b", "attn_q_g", "attn_k_g",
               "ffn_dw_b")
_WEIGHT_ORDER = ("mix_norm_g", "ffn_norm_g", "conv_w_in", "conv_a_dw_w", "conv_a_dw_b", "conv_a_ln_g",
                 "conv_a_ln_b", "conv_b_dw_w", "conv_w_out", "attn_w_qkv", "attn_q_g", "attn_k_g", "attn_w_o",
                 "ffn_w_up", "ffn_dw_w", "ffn_dw_b", "ffn_w_down")


def _assemble(gathered, axis):
    n, l, a, b = gathered.shape
    if axis == 2:
        return jnp.transpose(gathered, (1, 2, 0, 3)).reshape(l, a, n * b)
    return jnp.transpose(gathered, (1, 0, 2, 3)).reshape(l, n * a, b)


def _split_shards(full, axis):
    l, a, b = full.shape
    if axis == 2:
        return jnp.transpose(full.reshape(l, a, N_DEV, b // N_DEV), (2, 0, 1, 3))
    return jnp.transpose(full.reshape(l, N_DEV, a // N_DEV, b), (1, 0, 2, 3))


def kernel(x, mix_norm_g, ffn_norm_g, conv_w_in, conv_a_dw_w, conv_a_dw_b, conv_a_ln_g, conv_a_ln_b, conv_b_dw_w, conv_w_out, attn_w_qkv, attn_q_g, attn_k_g, attn_w_o, ffn_w_up, ffn_dw_w, ffn_dw_b, ffn_w_down, loss_target, m_mix_norm_g, m_ffn_norm_g, m_conv_w_in, m_conv_a_dw_w, m_conv_a_dw_b, m_conv_a_ln_g, m_conv_a_ln_b, m_conv_b_dw_w, m_conv_w_out, m_attn_w_qkv, m_attn_q_g, m_attn_k_g, m_attn_w_o, m_ffn_w_up, m_ffn_dw_w, m_ffn_dw_b, m_ffn_w_down, v_mix_norm_g, v_ffn_norm_g, v_conv_w_in, v_conv_a_dw_w, v_conv_a_dw_b, v_conv_a_ln_g, v_conv_a_ln_b, v_conv_b_dw_w, v_conv_w_out, v_attn_w_qkv, v_attn_q_g, v_attn_k_g, v_attn_w_o, v_ffn_w_up, v_ffn_dw_w, v_ffn_dw_b, v_ffn_w_down):
    args = dict(locals())
    weights = {n: args[n] for n in _WEIGHT_ORDER}
    mom_m = {n: args["m_" + n] for n in _WEIGHT_ORDER}
    mom_v = {n: args["v_" + n] for n in _WEIGHT_ORDER}

    _, s, d = x.shape
    depth = mix_norm_g.shape[0]
    dh = attn_q_g.shape[1]
    x0 = x.reshape(s, d)
    target = loss_target.reshape(s, d)

    shard_axis = dict(_MATMUL_WEIGHTS)
    shard16 = {n: weights[n].astype(BF16) for n, _ in _MATMUL_WEIGHTS}
    full = {}

    def layer_keys(layer):
        if layer >= depth:
            return []
        mixer = ("conv_w_in", "conv_w_out") if layer % 2 == 0 else ("attn_w_qkv", "attn_w_o")
        return [(mixer[0], layer // 2), (mixer[1], layer // 2), ("ffn_w_up", layer), ("ffn_w_down", layer)]

    def shards(keys):
        return [shard16[n][idx] for n, idx in keys]

    def assemble(keys, gathered):
        for (n, idx), g in zip(keys, gathered):
            if shard_axis[n] == 2:
                full[(n, idx)] = _assemble_cols(g, name=f"assemble_{n}_{idx}")
            else:
                full[(n, idx)] = g.reshape(N_DEV * g.shape[1], g.shape[2])

    mixer_in0, mixer_out0, ffn_up0, ffn_down0 = layer_keys(0)
    up_front = [mixer_in0, mixer_out0, ffn_down0]
    assemble(up_front, _all_gather(shards(up_front), "gather_first_weights"))
    tap_shapes = [weights[n].shape for n, _ in _TAP_WEIGHTS]
    g_tap = _all_gather([_pack([weights[n] for n, _ in _TAP_WEIGHTS], F32)], "gather_tap_weights")[0]
    for (n, axis), part in zip(_TAP_WEIGHTS, _unpack(g_tap, tap_shapes, lead=(N_DEV,))):
        full[n] = _assemble(part, axis)
    f = ffn_w_down.shape[1] * N_DEV

    saved = []
    xs = x0
    hn = _rmsnorm_fwd(xs, mix_norm_g[0:1], name="mix_norm_fwd_0")
    for layer in range(depth):
        i = layer // 2
        nxt = layer_keys(layer + 1)
        sv = {"x_in": xs, "h": hn}
        if layer % 2 == 0:
            on_in, on_act = ([], nxt[0:1]) if layer > 0 else (nxt[0:1], [ffn_up0])
            p, got = _matmul(hn, full[("conv_w_in", i)], gather_cargo=shards(on_in), name=f"conv_in_{layer}")
            assemble(on_in, got)
            (ab, a1), got = _conv_act_fwd(p, full["conv_a_dw_w"][i], conv_a_dw_b[i:i + 1], conv_a_ln_g[i:i + 1],
                                          conv_a_ln_b[i:i + 1], full["conv_b_dw_w"][i], gather_cargo=shards(on_act),
                                          name=f"conv_act_fwd_{layer}")
            assemble(on_act, got)
            sv.update(p=p, a1=a1, mix=ab)
            (xs, h2), _ = _matmul_add_norm(ab, full[("conv_w_out", i)], xs, ffn_norm_g[layer:layer + 1],
                                           name=f"conv_out_{layer}")
            riders = [nxt[1:2], nxt[2:3], nxt[3:4]]
        else:
            reps = LANES // dh
            gains = jnp.concatenate([
                jnp.broadcast_to(jnp.tile(attn_q_g[i], reps) * dh ** -0.5, (8, LANES)),
                jnp.broadcast_to(jnp.tile(attn_k_g[i], reps), (8, LANES)),
                jnp.ones((8, LANES), F32)], axis=0)
            qkv = _matmul(hn, full[("attn_w_qkv", i)], out_dtype=BF16, name=f"attn_qkv_{layer}")
            qkv16 = _qkv_prep_fwd(qkv, gains, dh, name=f"qkv_prep_fwd_{layer}")
            o, got = _attn_fwd(qkv16, dh, gather_cargo=shards(nxt), name=f"attn_fwd_{layer}")
            assemble(nxt, got)
            sv.update(qkv=qkv, qkv16=qkv16, gains=gains, mix=o)
            (xs, h2), _ = _matmul_add_norm(o, full[("attn_w_o", i)], xs, ffn_norm_g[layer:layer + 1],
                                           name=f"attn_out_{layer}")
            riders = [[], [], []]
        sv["x_mid"] = xs
        w_up = full[("ffn_w_up", layer)]
        ug, got = _matmul(h2, w_up, b_half=0, out_dtype=BF16, gather_cargo=shards(riders[0]),
                          name=f"ffn_up_gate_{layer}")
        assemble(riders[0], got)
        uv = _matmul(h2, w_up, b_half=1, out_dtype=BF16, name=f"ffn_up_val_{layer}")
        taps = full["ffn_dw_w"][layer]
        bias = ffn_dw_b[layer:layer + 1]
        act, got = _ffn_act_fwd(ug, uv, taps[:, :f], taps[:, f:], bias[:, :f], bias[:, f:],
                                gather_cargo=shards(riders[1]), name=f"ffn_act_fwd_{layer}")
        assemble(riders[1], got)
        sv.update(h2=h2, ug=ug, uv=uv, act=act)
        if layer + 1 < depth:
            (xs, hn), got = _matmul_add_norm(act, full[("ffn_w_down", layer)], xs, mix_norm_g[layer + 1:layer + 2],
                                             gather_cargo=shards(riders[2]), name=f"ffn_down_{layer}")
        else:
            xs, got = _matmul(act, full[("ffn_w_down", layer)], add=xs, gather_cargo=shards(riders[2]),
                              name=f"ffn_down_{layer}")
        assemble(riders[2], got)
        saved.append(sv)

    sq, dx, dx16 = _loss_head(xs, target, name="loss_head")
    loss = lax.psum(0.5 * sq[0, 0] / d, ("x", "y", "c"))

    grads = {n: [None] * weights[n].shape[0] for n in _WEIGHT_ORDER}
    queued, arrived = [], {}

    def queue(n, idx, g):
        a, b = weights[n].shape[1:]
        slabs = _split_cols(g, name=f"split_{n}_{idx}") if shard_axis[n] == 2 else g.reshape(N_DEV, a, b)
        queued.append(((n, idx), slabs))

    def carried(call):
        keys = [k for k, _ in queued]
        slabs = [sl for _, sl in queued]
        queued.clear()
        outs, received = call(slabs)
        arrived.update(zip(keys, received))
        return outs

    for layer in reversed(range(depth)):
        i = layer // 2
        sv = saved[layer]
        w_up = full[("ffn_w_up", layer)]
        taps = full["ffn_dw_w"][layer]
        bias = ffn_dw_b[layer:layer + 1]
        dact = _matmul(dx16, full[("ffn_w_down", layer)], tb=True, out_dtype=BF16, name=f"ffn_down_dx_{layer}")
        queue("ffn_w_down", layer, _weight_grad(sv["act"], dx16, name=f"ffn_down_dw_{layer}"))
        dug, duv, sumg, sumv = carried(lambda cargo: _ffn_act_bwd(
            sv["ug"], sv["uv"], dact, taps[:, :f], taps[:, f:], bias[:, :f], bias[:, f:], cargo=cargo,
            name=f"ffn_act_bwd_{layer}"))
        sums = jnp.concatenate([sumg, sumv], axis=1).reshape(4, 8, 2 * f).sum(axis=1)
        grads["ffn_dw_w"][layer] = sums[:3]
        grads["ffn_dw_b"][layer] = sums[3]
        dh2 = _matmul(dug, w_up, tb=True, b_half=0, name=f"ffn_up_gate_dx_{layer}")
        (dx, dx16, dg), _ = _matmul_norm_bwd(duv, w_up, sv["x_mid"], ffn_norm_g[layer:layer + 1], dx, b_half=1,
                                             add=dh2, name=f"ffn_up_val_dx_{layer}")
        grads["ffn_norm_g"][layer] = dg[0]
        queue("ffn_w_up", layer, jnp.concatenate(
            [_weight_grad(sv["h2"], dug, name=f"ffn_up_gate_dw_{layer}"),
             _weight_grad(sv["h2"], duv, name=f"ffn_up_val_dw_{layer}")], axis=1))

        if layer % 2 == 0:
            dab = _matmul(dx16, full[("conv_w_out", i)], tb=True, name=f"conv_out_dx_{layer}")
            queue("conv_w_out", i, _weight_grad(sv["mix"], dx16, name=f"conv_out_dw_{layer}"))
            ka = full["conv_a_dw_w"].shape[1]
            kb = full["conv_b_dw_w"].shape[1]
            dp, dwa, dba, dlng, dlnb, dwb = carried(lambda cargo: _conv_act_bwd(
                sv["p"], sv["a1"], dab, full["conv_a_dw_w"][i], conv_a_ln_g[i:i + 1], conv_a_ln_b[i:i + 1],
                full["conv_b_dw_w"][i], cargo=cargo, name=f"conv_act_bwd_{layer}"))
            grads["conv_a_dw_w"][i] = dwa[:ka]
            grads["conv_a_dw_b"][i] = dba[0]
            grads["conv_a_ln_g"][i] = dlng[0]
            grads["conv_a_ln_b"][i] = dlnb[0]
            grads["conv_b_dw_w"][i] = dwb[:kb]
            queue("conv_w_in", i, _weight_grad(sv["h"], dp, name=f"conv_in_dw_{layer}"))
            dy, w_in = dp, full[("conv_w_in", i)]
        else:
            do = _matmul(dx16, full[("attn_w_o", i)], tb=True, name=f"attn_out_dx_{layer}")
            queue("attn_w_o", i, _weight_grad(sv["mix"], dx16, name=f"attn_out_dw_{layer}"))
            dq, dk, dv = carried(lambda cargo: _attn_bwd(sv["qkv16"], sv["mix"], do, dh, cargo=cargo,
                                                         name=f"attn_bwd_{layer}"))
            dqkv, dgains = _qkv_prep_bwd(sv["qkv"], dq, dk, dv, sv["gains"], dh, name=f"qkv_prep_bwd_{layer}")
            grads["attn_q_g"][i] = dgains[0].reshape(-1, dh).sum(axis=0) * dh ** -0.5
            grads["attn_k_g"][i] = dgains[8].reshape(-1, dh).sum(axis=0)
            queue("attn_w_qkv", i, _weight_grad(sv["h"], dqkv, name=f"attn_qkv_dw_{layer}"))
            dy, w_in = dqkv, full[("attn_w_qkv", i)]

        def mixer_in_bwd(cargo):
            return _matmul_norm_bwd(dy, w_in, sv["x_in"], mix_norm_g[layer:layer + 1], dx, cargo=cargo,
                                    name=f"mixer_in_dx_{layer}")

        dx, dx16, dg = carried(mixer_in_bwd) if layer == 0 else mixer_in_bwd(())[0]
        grads["mix_norm_g"][layer] = dg[0]

    me = _my_index()
    final_grads, delta, new_m, new_v = {}, {}, {}, {}
    for n, _ in _MATMUL_WEIGHTS:
        final_grads[n], delta[n], new_m[n], new_v[n] = _adamw_sum(
            weights[n], [arrived[(n, idx)] for idx in range(weights[n].shape[0])], mom_m[n], mom_v[n],
            name=f"adamw_{n}")

    small_names = list(_REPLICATED) + [n for n, _ in _TAP_WEIGHTS]
    local = {n: jnp.stack(grads[n], axis=0) for n in small_names}
    small_shapes = [local[n].shape for n in small_names]
    g_small = _sum_slabs(_all_gather([_pack([local[n] for n in small_names], F32)], "gather_small_grads")[0],
                         name="sum_small_grads")
    reduced = dict(zip(small_names, _unpack(g_small, small_shapes)))
    for n in _REPLICATED:
        final_grads[n] = reduced[n]
    for n, axis in _TAP_WEIGHTS:
        final_grads[n] = lax.dynamic_index_in_dim(_split_shards(reduced[n], axis), me, axis=0, keepdims=False)
    shapes = [weights[n].shape for n in small_names]
    d_s, m_s, v_s = _adamw(
        _pack([weights[n] for n in small_names], F32), _pack([final_grads[n] for n in small_names], F32),
        _pack([mom_m[n] for n in small_names], F32), _pack([mom_v[n] for n in small_names], F32), name="adamw_small")
    delta.update(zip(small_names, _unpack(d_s, shapes)))
    new_m.update(zip(small_names, _unpack(m_s, shapes)))
    new_v.update(zip(small_names, _unpack(v_s, shapes)))

    order = list(_WEIGHT_ORDER)
    return (loss, dx.reshape(x.shape), *[final_grads[n] for n in order], *[delta[n] for n in order],
            *[new_m[n] for n in order], *[new_v[n] for n in order])
```

```python
import jax
import jax.numpy as jnp
from jax import lax
from jax.experimental import pallas as pl
from jax.experimental.pallas import tpu as pltpu

F32 = jnp.float32
BF16 = jnp.bfloat16
EPS = 1e-6
N_DEV = 8
MESH_ID = pl.DeviceIdType.MESH

ADAM_LR = 0.001
ADAM_B1 = 0.9
ADAM_B2 = 0.999
ADAM_EPS = 1e-08
ADAM_WD = 0.01
ADAM_STEP = 10

V7X_VMEM_LIMIT_BYTES = 48 * 1024 * 1024
PACK_QUANTUM = 2048
A_HALO = 32
S_HALO = 8
FFN_HALO_BLOCK = 16
LANES = 128
ATTN_Q_BLOCK = 256
ATTN_K_BLOCK = 256
MATMUL_MAX_SINGLE_K = 3072
MXU_WIDTH = 256
FFN_CHUNK_ROWS = 128
ATTN_DEAD_LOG_WEIGHT = -110.0


def _tile(n, prefs):
    for p in prefs:
        if n % p == 0:
            return p
    return n


def _params(sem=None):
    return pltpu.CompilerParams(dimension_semantics=sem, vmem_limit_bytes=V7X_VMEM_LIMIT_BYTES)


def _sigmoid(x):
    return 1.0 / (1.0 + jnp.exp(-x))


def _dot(a, b, ca, cb):
    return lax.dot_general(a, b, (((ca,), (cb,)), ((), ())), preferred_element_type=F32)


def _my_index():
    return 4 * lax.axis_index("x") + 2 * lax.axis_index("y") + lax.axis_index("c")


def _all_gather(shards, name):
    n = len(shards)

    def body(*refs):
        x_refs, out_refs = refs[:n], refs[n:2 * n]
        send_sems, recv_sems, local_sems = refs[2 * n:]
        x, y, c = lax.axis_index("x"), lax.axis_index("y"), lax.axis_index("c")
        me, sibling = (x, y, c), (x, y, 1 - c)
        chips = [(1 - x, y), (x, 1 - y), (1 - x, 1 - y)]

        def copy(a, k, block, to, src=None):
            slot = out_refs[a].at[4 * block[0] + 2 * block[1] + block[2]]
            return pltpu.make_async_remote_copy(
                src_ref=slot if src is None else src, dst_ref=slot,
                send_sem=send_sems.at[7 * a + k], recv_sem=recv_sems.at[7 * a + k],
                device_id=to, device_id_type=MESH_ID)

        mine = [pltpu.make_async_copy(x_refs[a], out_refs[a].at[4 * x + 2 * y + c], local_sems.at[a])
                for a in range(n)]
        for cp in mine:
            cp.start()
        first = []
        for a in range(n):
            first.append(copy(a, 0, me, sibling, src=x_refs[a]))
            first += [copy(a, 1 + j, me, (*chip, c), src=x_refs[a]) for j, chip in enumerate(chips)]
        for cp in first:
            cp.start()
        passed = []
        for j, chip in enumerate(chips):
            for a in range(n):
                copy(a, 1 + j, (*chip, c), me).wait_recv()
                fwd = copy(a, 4 + j, (*chip, c), sibling)
                fwd.start()
                passed.append(fwd)
        for a in range(n):
            copy(a, 0, sibling, me).wait_recv()
            for j, chip in enumerate(chips):
                copy(a, 4 + j, (*chip, 1 - c), me).wait_recv()
        for cp in first + passed:
            cp.wait_send()
        for cp in mine:
            cp.wait()

    hbm = pl.BlockSpec(memory_space=pl.ANY)
    return pl.pallas_call(
        body,
        name=name,
        out_shape=[jax.ShapeDtypeStruct((N_DEV,) + sh.shape, sh.dtype) for sh in shards],
        in_specs=[hbm] * n,
        out_specs=[hbm] * n,
        scratch_shapes=[
            pltpu.SemaphoreType.DMA((7 * n,)),
            pltpu.SemaphoreType.DMA((7 * n,)),
            pltpu.SemaphoreType.DMA((n,)),
        ],
    )(*shards)


def _exchange_semaphores(n):
    return [pltpu.SemaphoreType.DMA((7 * n,)), pltpu.SemaphoreType.DMA((7 * n,)), pltpu.SemaphoreType.DMA((n,))]


def _exchange_copies(g_refs, out_refs, send_sems, recv_sems, local_sems, gather=False):
    n = len(g_refs)
    x, y, c = lax.axis_index("x"), lax.axis_index("y"), lax.axis_index("c")
    me = 4 * x + 2 * y + c

    def part(a, d):
        return g_refs[a] if gather else g_refs[a].at[d]

    local = [pltpu.make_async_copy(part(a, me), out_refs[a].at[me], local_sems.at[a]) for a in range(n)]
    remote = []
    for k in range(1, N_DEV):
        px = (x + ((k >> 2) & 1)) % 2
        py = (y + ((k >> 1) & 1)) % 2
        pc = (c + (k & 1)) % 2
        for a in range(n):
            remote.append(pltpu.make_async_remote_copy(
                src_ref=part(a, 4 * px + 2 * py + pc), dst_ref=out_refs[a].at[me],
                send_sem=send_sems.at[7 * a + k - 1], recv_sem=recv_sems.at[7 * a + k - 1],
                device_id=(px, py, pc), device_id_type=MESH_ID))
    return local, remote


def _exchange_start(copies):
    for cp in copies[0] + copies[1]:
        cp.start()


def _exchange_wait(copies):
    local, remote = copies
    for cp in remote:
        cp.wait_recv()
    for cp in remote:
        cp.wait_send()
    for cp in local:
        cp.wait()


def _carrying_call(body, cargo, *, name, grid, in_specs, out_specs, out_shape, scratch_shapes, operands,
                   compiler_params, gather=False):
    n = len(cargo)
    n_in, n_out, n_scr = len(in_specs), len(out_specs), len(scratch_shapes)
    if n == 0:
        return pl.pallas_call(body, name=name, grid=grid, in_specs=in_specs, out_specs=out_specs,
                              out_shape=out_shape, scratch_shapes=scratch_shapes,
                              compiler_params=compiler_params)(*operands), []

    def carrying(*refs):
        cuts = [0, n_in, n_in + n, n_in + n + n_out, n_in + 2 * n + n_out, n_in + 2 * n + n_out + n_scr, len(refs)]
        ins, g_refs, outs, r_refs, scr, sems = [refs[a:b] for a, b in zip(cuts[:-1], cuts[1:])]
        steps = [pl.program_id(ax) for ax in range(len(grid))]
        first = steps[0] == 0
        last = steps[0] == grid[0] - 1
        for ax in range(1, len(grid)):
            first = first & (steps[ax] == 0)
            last = last & (steps[ax] == grid[ax] - 1)

        @pl.when(first)
        def _():
            _exchange_start(_exchange_copies(g_refs, r_refs, *sems, gather=gather))

        body(*ins, *outs, *scr)

        @pl.when(last)
        def _():
            _exchange_wait(_exchange_copies(g_refs, r_refs, *sems, gather=gather))

    hbm = pl.BlockSpec(memory_space=pl.ANY)
    lead = (N_DEV,) if gather else ()
    res = pl.pallas_call(
        carrying,
        name=name,
        grid=grid,
        in_specs=list(in_specs) + [hbm] * n,
        out_specs=list(out_specs) + [hbm] * n,
        out_shape=list(out_shape) + [jax.ShapeDtypeStruct(lead + g.shape, g.dtype) for g in cargo],
        scratch_shapes=list(scratch_shapes) + _exchange_semaphores(n),
        compiler_params=compiler_params,
    )(*operands, *cargo)
    return res[:n_out], res[n_out:]


def _assemble_cols(g, *, name):
    n, r, w = g.shape
    tr = _tile(r, (256, 128, 64, 32, 16))

    def body(g_ref, o_ref):
        for j in range(n):
            o_ref[:, j * w:(j + 1) * w] = g_ref[j]

    return pl.pallas_call(
        body,
        name=name,
        grid=(r // tr,),
        in_specs=[pl.BlockSpec((n, tr, w), lambda i: (0, i, 0))],
        out_specs=pl.BlockSpec((tr, n * w), lambda i: (i, 0)),
        out_shape=jax.ShapeDtypeStruct((r, n * w), g.dtype),
        compiler_params=_params(("parallel",)),
    )(g)


def _split_cols(x, *, name):
    r, nw = x.shape
    w = nw // N_DEV
    tr = _tile(r, (256, 128, 64, 32, 16))

    def body(x_ref, o_ref):
        for j in range(N_DEV):
            o_ref[j] = x_ref[:, j * w:(j + 1) * w]

    return pl.pallas_call(
        body,
        name=name,
        grid=(r // tr,),
        in_specs=[pl.BlockSpec((tr, nw), lambda i: (i, 0))],
        out_specs=pl.BlockSpec((N_DEV, tr, w), lambda i: (0, i, 0)),
        out_shape=jax.ShapeDtypeStruct((N_DEV, r, w), x.dtype),
        compiler_params=_params(("parallel",)),
    )(x)


def _padded(n):
    return -(-n // PACK_QUANTUM) * PACK_QUANTUM


def _pack(parts, dtype, lead=()):
    flat = []
    for p in parts:
        p = p.astype(dtype).reshape(lead + (-1,))
        n = p.shape[-1]
        flat.append(jnp.pad(p, [(0, 0)] * len(lead) + [(0, _padded(n) - n)]))
    return jnp.concatenate(flat, axis=-1).reshape(lead + (-1, 128))


def _unpack(buf, shapes, lead=()):
    flat = buf.reshape(lead + (-1,))
    out, off = [], 0
    for shp in shapes:
        n = 1
        for d in shp:
            n *= d
        out.append(flat[..., off:off + n].reshape(lead + tuple(shp)))
        off += _padded(n)
    return out


def _half_if_tiled(n):
    half = n // 2
    return (half,) if n % 2 == 0 and half % LANES == 0 and half <= 1536 else ()


def _matmul(a, b, *, ta=False, tb=False, b_half=None, out_dtype=F32, add=None, gather_cargo=None, name):
    if ta:
        kdim, m = a.shape
    else:
        m, kdim = a.shape
    halves = 1 if b_half is None else 2
    n = b.shape[0] if tb else b.shape[1] // halves
    bm = _tile(m, (1024,) + _half_if_tiled(m) + (512, 256, 128))
    bn = _tile(n, (512, 256, 128))
    if n % MXU_WIDTH == 0 and n <= MATMUL_MAX_SINGLE_K:
        bn, bm = n, _tile(m, (512,) + _half_if_tiled(m) + (256, 128))
    bk = kdim if kdim <= MATMUL_MAX_SINGLE_K else _tile(kdim, (1024, 512, 256, 128))
    nk = kdim // bk

    a_spec = pl.BlockSpec((bk, bm), lambda i, j, k: (k, i)) if ta else pl.BlockSpec((bm, bk), lambda i, j, k: (i, k))
    col0 = 0 if b_half is None else b_half * ((nk if tb else n // bn))
    b_spec = (pl.BlockSpec((bn, bk), lambda i, j, k: (j, k + col0)) if tb
              else pl.BlockSpec((bk, bn), lambda i, j, k: (k, j + col0)))
    o_spec = pl.BlockSpec((bm, bn), lambda i, j, k: (i, j))
    in_specs = [a_spec, b_spec] + ([o_spec] if add is not None else [])
    operands = [a, b] + ([add] if add is not None else [])

    def product(a_ref, b_ref):
        return _dot(a_ref[...].astype(BF16), b_ref[...].astype(BF16), 0 if ta else 1, 1 if tb else 0)

    def finish(r, refs):
        if add is not None:
            r = r + refs[2][...]
        return r

    def body_single(*refs):
        o_ref = refs[-1]
        o_ref[...] = finish(product(refs[0], refs[1]), refs).astype(o_ref.dtype)

    def body_multi(*refs):
        o_ref, acc_ref = refs[-2], refs[-1]
        k = pl.program_id(2)

        @pl.when(k == 0)
        def _():
            acc_ref[...] = jnp.zeros_like(acc_ref)

        acc_ref[...] += product(refs[0], refs[1])

        @pl.when(k == nk - 1)
        def _():
            o_ref[...] = finish(acc_ref[...], refs).astype(o_ref.dtype)

    outs, received = _carrying_call(
        body_single if nk == 1 else body_multi,
        () if gather_cargo is None else gather_cargo,
        gather=True,
        name=name,
        grid=(m // bm, n // bn, nk),
        in_specs=in_specs,
        out_specs=[o_spec],
        out_shape=[jax.ShapeDtypeStruct((m, n), out_dtype)],
        scratch_shapes=[] if nk == 1 else [pltpu.VMEM((bm, bn), F32)],
        operands=operands,
        compiler_params=_params(("parallel", "parallel", "arbitrary") if gather_cargo is None else ("arbitrary",) * 3),
    )
    return outs[0] if gather_cargo is None else (outs[0], received)


def _rows_matmul(a, b, *, tb, add, row_ins, vec_ins, epilogue, row_out_dtypes, n_sums, cargo, gather, name):
    m, kdim = a.shape
    n = b.shape[0] if tb else b.shape[1]
    assert kdim <= MATMUL_MAX_SINGLE_K
    b_spec = pl.BlockSpec(b.shape, lambda i: (0, 0))
    bm = _tile(m, (512, 256, 128))
    row = pl.BlockSpec((bm, n), lambda i: (i, 0))
    vec = pl.BlockSpec((1, n), lambda i: (0, 0))
    n_add = int(add is not None)
    n_ins = len(row_ins) + len(vec_ins)
    n_rows = len(row_out_dtypes)

    def body(a_ref, b_ref, *refs):
        y = _dot(a_ref[...].astype(BF16), b_ref[...].astype(BF16), 1, 1 if tb else 0)
        if add is not None:
            y = y + refs[0][...]
        ins = [r[...] for r in refs[n_add:n_add + n_ins]]
        outs = refs[n_add + n_ins:]
        rows, sums = epilogue(y, *ins)
        for ref, val in zip(outs[:n_rows], rows):
            ref[...] = val.astype(ref.dtype)
        if n_sums:
            @pl.when(pl.program_id(0) == 0)
            def _():
                for ref in outs[n_rows:]:
                    ref[...] = jnp.zeros_like(ref)

            for ref, val in zip(outs[n_rows:], sums):
                ref[...] += val

    return _carrying_call(
        body,
        cargo,
        gather=gather,
        name=name,
        grid=(m // bm,),
        in_specs=[pl.BlockSpec((bm, kdim), lambda i: (i, 0)), b_spec]
        + [row] * (n_add + len(row_ins)) + [vec] * len(vec_ins),
        out_specs=[row] * n_rows + [vec] * n_sums,
        out_shape=[jax.ShapeDtypeStruct((m, n), dt) for dt in row_out_dtypes]
        + [jax.ShapeDtypeStruct((1, n), F32)] * n_sums,
        scratch_shapes=[],
        operands=[a, b] + ([add] if add is not None else []) + list(row_ins) + list(vec_ins),
        compiler_params=_params(("arbitrary",)),
    )


def _matmul_add_norm(a, b, add, g, *, gather_cargo=(), name):
    def epilogue(y, gv):
        h = y * lax.rsqrt(jnp.mean(y * y, axis=-1, keepdims=True) + EPS) * gv
        return (y, h), ()

    return _rows_matmul(a, b, tb=False, add=add, row_ins=(), vec_ins=(g,), epilogue=epilogue,
                        row_out_dtypes=(F32, BF16), n_sums=0, cargo=gather_cargo, gather=True, name=name)


def _matmul_norm_bwd(a, b, x, g, res, *, add=None, cargo=(), name):
    def epilogue(dyv, xv, resv, gv):
        rs = lax.rsqrt(jnp.mean(xv * xv, axis=-1, keepdims=True) + EPS)
        xn = xv * rs
        dyg = dyv * gv
        dx = rs * (dyg - xn * jnp.mean(dyg * xn, axis=-1, keepdims=True)) + resv
        return (dx, dx), (jnp.sum(dyv * xn, axis=0, keepdims=True),)

    return _rows_matmul(a, b, tb=True, add=add, row_ins=(x, res), vec_ins=(g,), epilogue=epilogue,
                        row_out_dtypes=(F32, BF16), n_sums=1, cargo=cargo, gather=False, name=name)


def _weight_grad(x, dy, *, name):
    return _matmul(x, dy, ta=True, out_dtype=BF16, name=name)


def _rmsnorm_fwd(x, g, *, name):
    r, d = x.shape
    tr = _tile(r, (1024, 512, 256, 128))

    def body(x_ref, g_ref, o_ref):
        xv = x_ref[...]
        y = xv * lax.rsqrt(jnp.mean(xv * xv, axis=-1, keepdims=True) + EPS)
        y = y * g_ref[...]
        o_ref[...] = y.astype(o_ref.dtype)

    return pl.pallas_call(
        body,
        name=name,
        grid=(r // tr,),
        in_specs=[pl.BlockSpec((tr, d), lambda i: (i, 0)), pl.BlockSpec((1, d), lambda i: (0, 0))],
        out_specs=pl.BlockSpec((tr, d), lambda i: (i, 0)),
        out_shape=jax.ShapeDtypeStruct((r, d), BF16),
        compiler_params=_params(("parallel",)),
    )(x, g)


def _loss_head(y, target, *, name):
    s, d = y.shape
    ts = _tile(s, (512, 256, 128))
    row = pl.BlockSpec((ts, d), lambda i: (i, 0))
    acc = pl.BlockSpec((8, 128), lambda i: (0, 0))

    def body(y_ref, t_ref, l_ref, dy_ref, dy16_ref):
        e = y_ref[...] - t_ref[...]
        dy = e * (1.0 / d)
        dy_ref[...] = dy
        dy16_ref[...] = dy.astype(dy16_ref.dtype)

        @pl.when(pl.program_id(0) == 0)
        def _():
            l_ref[...] = jnp.zeros_like(l_ref)

        sq = jnp.sum(jnp.sum(e * e, axis=-1, keepdims=True), axis=0, keepdims=True)
        l_ref[...] += jnp.broadcast_to(sq, l_ref.shape)

    return pl.pallas_call(
        body,
        name=name,
        grid=(s // ts,),
        in_specs=[row, row],
        out_specs=(acc, row, row),
        out_shape=(jax.ShapeDtypeStruct((8, 128), F32), jax.ShapeDtypeStruct((s, d), F32),
                   jax.ShapeDtypeStruct((s, d), BF16)),
        compiler_params=_params(("arbitrary",)),
    )(y, target)


def _ffn_tiles(s, f):
    return _tile(s, (512, 256, 128)), _tile(f, _half_if_tiled(f) + (512, 256, 128))


def _conv3(xs, w, b):
    return b + xs[0] * w[0:1] + xs[1] * w[1:2] + xs[2] * w[2:3]


def _shifted3(x, rows):
    return [x[S_HALO - 2 + k:S_HALO - 2 + k + rows] for k in range(3)]


def _ffn_act_fwd(ug, uv, wg, wv, bg, bv, *, gather_cargo=(), name):
    s, f = ug.shape
    ts, tc = _ffn_tiles(s, f)
    rc = _tile(ts, (FFN_CHUNK_ROWS, 32, 16, 8))
    hb = ts // FFN_HALO_BLOCK
    main = pl.BlockSpec((ts, tc), lambda j, i: (i, j))
    prev = pl.BlockSpec((FFN_HALO_BLOCK, tc), lambda j, i: (jnp.maximum(i * hb - 1, 0), j))
    taps = pl.BlockSpec((3, tc), lambda j, i: (0, j))
    bias = pl.BlockSpec((1, tc), lambda j, i: (0, j))
    lead = FFN_HALO_BLOCK - S_HALO

    def body(ug_ref, ugp_ref, uv_ref, uvp_ref, wg_ref, wv_ref, bg_ref, bv_ref, o_ref, gbuf, vbuf):
        first = pl.program_id(1) == 0
        for buf, p_ref, m_ref in ((gbuf, ugp_ref, ug_ref), (vbuf, uvp_ref, uv_ref)):
            buf[0:FFN_HALO_BLOCK, :] = jnp.where(first, 0.0, p_ref[...].astype(F32))
            buf[FFN_HALO_BLOCK:, :] = m_ref[...].astype(F32)

        def chunk(ci, carry):
            r0 = pl.multiple_of(ci * rc, rc)
            rows = pl.ds(pl.multiple_of(r0 + lead, S_HALO), rc + S_HALO)
            for lb in range(tc // LANES):
                ls = pl.ds(lb * LANES, LANES)
                gate = _conv3(_shifted3(gbuf[rows, ls], rc), wg_ref[:, ls], bg_ref[:, ls])
                val = _conv3(_shifted3(vbuf[rows, ls], rc), wv_ref[:, ls], bv_ref[:, ls])
                o_ref[pl.ds(r0, rc), ls] = (gate * _sigmoid(gate) * val).astype(o_ref.dtype)
            return carry

        lax.fori_loop(0, ts // rc, chunk, 0)

    outs, gathered = _carrying_call(
        body,
        gather_cargo,
        gather=True,
        name=name,
        grid=(f // tc, s // ts),
        in_specs=[main, prev, main, prev, taps, taps, bias, bias],
        out_specs=[main],
        out_shape=[jax.ShapeDtypeStruct((s, f), BF16)],
        scratch_shapes=[pltpu.VMEM((FFN_HALO_BLOCK + ts, tc), F32), pltpu.VMEM((FFN_HALO_BLOCK + ts, tc), F32)],
        operands=(ug, ug, uv, uv, wg, wv, bg, bv),
        compiler_params=_params(("arbitrary", "arbitrary")),
    )
    return outs[0], gathered


def _ffn_act_bwd(ug, uv, df, wg, wv, bg, bv, *, cargo=(), name):
    s, f = ug.shape
    ts, tc = _ffn_tiles(s, f)
    rc = _tile(ts, (FFN_CHUNK_ROWS, 32, 16, 8))
    hb = ts // FFN_HALO_BLOCK
    last_hb = s // FFN_HALO_BLOCK - 1
    n_row = s // ts
    lead = FFN_HALO_BLOCK - S_HALO
    main = pl.BlockSpec((ts, tc), lambda j, i: (i, j))
    prev = pl.BlockSpec((FFN_HALO_BLOCK, tc), lambda j, i: (jnp.maximum(i * hb - 1, 0), j))
    nxt = pl.BlockSpec((FFN_HALO_BLOCK, tc), lambda j, i: (jnp.minimum((i + 1) * hb, last_hb), j))
    taps = pl.BlockSpec((3, tc), lambda j, i: (0, j))
    bias = pl.BlockSpec((1, tc), lambda j, i: (0, j))
    sums = pl.BlockSpec((32, tc), lambda j, i: (0, j))
    ext = rc + S_HALO

    def body(ug_ref, ugp_ref, ugn_ref, uv_ref, uvp_ref, uvn_ref, df_ref, dfn_ref, wg_ref, wv_ref, bg_ref, bv_ref,
             dug_ref, duv_ref, sumg_ref, sumv_ref, gbuf, vbuf, dfbuf):
        i = pl.program_id(1)
        first = i == 0
        last = i == n_row - 1

        @pl.when(first)
        def _():
            sumg_ref[...] = jnp.zeros_like(sumg_ref)
            sumv_ref[...] = jnp.zeros_like(sumv_ref)

        for buf, p_ref, m_ref, n_ref in ((gbuf, ugp_ref, ug_ref, ugn_ref), (vbuf, uvp_ref, uv_ref, uvn_ref)):
            buf[0:FFN_HALO_BLOCK, :] = jnp.where(first, 0.0, p_ref[...].astype(F32))
            buf[FFN_HALO_BLOCK:FFN_HALO_BLOCK + ts, :] = m_ref[...].astype(F32)
            buf[FFN_HALO_BLOCK + ts:, :] = n_ref[...].astype(F32)
        dfbuf[0:ts, :] = df_ref[...].astype(F32)
        dfbuf[ts:, :] = jnp.where(last, 0.0, dfn_ref[...].astype(F32))

        def chunk(ci, carry):
            r0 = pl.multiple_of(ci * rc, rc)
            rows = pl.ds(pl.multiple_of(r0 + lead, S_HALO), ext + S_HALO)
            for lb in range(tc // LANES):
                ls = pl.ds(lb * LANES, LANES)
                xg = _shifted3(gbuf[rows, ls], ext)
                xv = _shifted3(vbuf[rows, ls], ext)
                wgv, wvv = wg_ref[:, ls], wv_ref[:, ls]
                gate = _conv3(xg, wgv, bg_ref[:, ls])
                val = _conv3(xv, wvv, bv_ref[:, ls])
                dfe = dfbuf[pl.ds(r0, ext), ls]
                sg = _sigmoid(gate)
                dgate = dfe * val * (sg * (1.0 + gate * (1.0 - sg)))
                dval = dfe * (gate * sg)
                for dz, xs, w, du_ref, sum_ref in ((dgate, xg, wgv, dug_ref, sumg_ref),
                                                   (dval, xv, wvv, duv_ref, sumv_ref)):
                    du = dz[2:2 + rc] * w[0:1] + dz[1:1 + rc] * w[1:2] + dz[0:rc] * w[2:3]
                    du_ref[pl.ds(r0, rc), ls] = du.astype(du_ref.dtype)
                    dm = dz[0:rc]
                    for k in range(3):
                        sum_ref[8 * k:8 * k + 8, ls] += (dm * xs[k][0:rc]).reshape(rc // 8, 8, LANES).sum(axis=0)
                    sum_ref[24:32, ls] += dm.reshape(rc // 8, 8, LANES).sum(axis=0)
            return carry

        lax.fori_loop(0, ts // rc, chunk, 0)

    return _carrying_call(
        body,
        cargo,
        name=name,
        grid=(f // tc, n_row),
        in_specs=[main, prev, nxt, main, prev, nxt, main, nxt, taps, taps, bias, bias],
        out_specs=(main, main, sums, sums),
        out_shape=(jax.ShapeDtypeStruct((s, f), BF16), jax.ShapeDtypeStruct((s, f), BF16),
                   jax.ShapeDtypeStruct((32, f), F32), jax.ShapeDtypeStruct((32, f), F32)),
        scratch_shapes=[pltpu.VMEM((2 * FFN_HALO_BLOCK + ts, tc), F32), pltpu.VMEM((2 * FFN_HALO_BLOCK + ts, tc), F32),
                        pltpu.VMEM((FFN_HALO_BLOCK + ts, tc), F32)],
        operands=(ug, ug, ug, uv, uv, uv, df, df, wg, wv, bg, bv),
        compiler_params=_params(("arbitrary", "arbitrary")),
    )


def _layernorm_stats(a1):
    mu = jnp.mean(a1, axis=-1, keepdims=True)
    xc = a1 - mu
    rstd = lax.rsqrt(jnp.mean(xc * xc, axis=-1, keepdims=True) + EPS)
    return xc * rstd, rstd


def _conv_act_fwd(p, wa, ba, lng, lnb, wb, *, gather_cargo=(), name):
    s, c5 = p.shape
    c = c5 // 5
    ka, kb = wa.shape[0], wb.shape[0]
    ts = _tile(s, (256, 128))
    hb = ts // A_HALO
    main = pl.BlockSpec((ts, c5), lambda i: (i, 0))
    prev = pl.BlockSpec((A_HALO, c5), lambda i: (jnp.maximum(i * hb - 1, 0), 0))

    def full(arr):
        return pl.BlockSpec(arr.shape, lambda i: (0, 0))

    def body(p_ref, pp_ref, wa_ref, ba_ref, lng_ref, lnb_ref, wb_ref, ab_ref, a1_ref, gbuf, cbuf):
        first = pl.program_id(0) == 0
        gbuf[0:A_HALO, :] = jnp.where(first, 0.0, pp_ref[:, 0:c] * _sigmoid(pp_ref[:, c:2 * c]))
        gbuf[A_HALO:, :] = p_ref[:, 0:c] * _sigmoid(p_ref[:, c:2 * c])
        cbuf[0:A_HALO, :] = jnp.where(first, 0.0, pp_ref[:, 3 * c:4 * c] * pp_ref[:, 4 * c:5 * c])
        cbuf[A_HALO:, :] = p_ref[:, 3 * c:4 * c] * p_ref[:, 4 * c:5 * c]

        a1 = jnp.broadcast_to(ba_ref[...], (ts, c))
        for k in range(ka):
            off = A_HALO - (ka - 1) + k
            a1 = a1 + gbuf[off:off + ts, :] * wa_ref[k:k + 1, :]
        a1_ref[...] = a1
        xhat, _ = _layernorm_stats(a1)
        a2 = xhat * lng_ref[...] + lnb_ref[...]
        ab_ref[:, 0:c] = (a2 * _sigmoid(a2)).astype(ab_ref.dtype)

        cv = jnp.zeros((ts, c), F32)
        for k in range(kb):
            off = A_HALO - (kb - 1) + k
            cv = cv + cbuf[off:off + ts, :] * wb_ref[k:k + 1, :]
        ab_ref[:, c:2 * c] = (p_ref[:, 2 * c:3 * c] * cv).astype(ab_ref.dtype)

    return _carrying_call(
        body,
        gather_cargo,
        gather=True,
        name=name,
        grid=(s // ts,),
        in_specs=[main, prev, full(wa), full(ba), full(lng), full(lnb), full(wb)],
        out_specs=(pl.BlockSpec((ts, 2 * c), lambda i: (i, 0)), pl.BlockSpec((ts, c), lambda i: (i, 0))),
        out_shape=(jax.ShapeDtypeStruct((s, 2 * c), BF16), jax.ShapeDtypeStruct((s, c), F32)),
        scratch_shapes=[pltpu.VMEM((A_HALO + ts, c), F32), pltpu.VMEM((A_HALO + ts, c), F32)],
        operands=(p, p, wa, ba, lng, lnb, wb),
        compiler_params=_params(("arbitrary",)),
    )


def _conv_act_bwd(p, a1, dab, wa, lng, lnb, wb, *, cargo=(), name):
    s, c5 = p.shape
    c = c5 // 5
    ka, kb = wa.shape[0], wb.shape[0]
    ts = _tile(s, (128,))
    hb = ts // A_HALO
    n_row = s // ts
    last_hb = s // A_HALO - 1
    ext = ts + A_HALO

    def rows(width):
        return (pl.BlockSpec((ts, width), lambda i: (i, 0)),
                pl.BlockSpec((A_HALO, width), lambda i: (jnp.maximum(i * hb - 1, 0), 0)),
                pl.BlockSpec((A_HALO, width), lambda i: (jnp.minimum((i + 1) * hb, last_hb), 0)))

    p_main, p_prev, p_next = rows(c5)
    d_main, _, d_next = rows(2 * c)
    a_main, _, a_next = rows(c)

    def full(shape):
        return pl.BlockSpec(shape, lambda i: (0, 0))

    def body(p_ref, pp_ref, pn_ref, a1_ref, a1n_ref, d_ref, dn_ref, wa_ref, lng_ref, lnb_ref, wb_ref,
             dp_ref, dwa_ref, dba_ref, dlng_ref, dlnb_ref, dwb_ref, gbuf, cbuf, da1buf, dcvbuf):
        i = pl.program_id(0)
        first = i == 0
        last = i == n_row - 1

        @pl.when(first)
        def _():
            dwa_ref[...] = jnp.zeros_like(dwa_ref)
            dba_ref[...] = jnp.zeros_like(dba_ref)
            dlng_ref[...] = jnp.zeros_like(dlng_ref)
            dlnb_ref[...] = jnp.zeros_like(dlnb_ref)
            dwb_ref[...] = jnp.zeros_like(dwb_ref)

        sig_gate = _sigmoid(p_ref[:, c:2 * c])
        gbuf[0:A_HALO, :] = jnp.where(first, 0.0, pp_ref[:, 0:c] * _sigmoid(pp_ref[:, c:2 * c]))
        gbuf[A_HALO:, :] = p_ref[:, 0:c] * sig_gate
        cbuf[0:A_HALO, :] = jnp.where(first, 0.0, pp_ref[:, 3 * c:4 * c] * pp_ref[:, 4 * c:5 * c])
        cbuf[A_HALO:, :] = p_ref[:, 3 * c:4 * c] * p_ref[:, 4 * c:5 * c]

        def ln_back(a1v, dav):
            xhat, rstd = _layernorm_stats(a1v)
            a2 = xhat * lng_ref[...] + lnb_ref[...]
            sg = _sigmoid(a2)
            da2 = dav * (sg * (1.0 + a2 * (1.0 - sg)))
            dxh = da2 * lng_ref[...]
            da1 = rstd * (dxh - jnp.mean(dxh, axis=-1, keepdims=True)
                          - xhat * jnp.mean(dxh * xhat, axis=-1, keepdims=True))
            return da1, da2, xhat

        da1_m, da2_m, xhat_m = ln_back(a1_ref[...], d_ref[:, 0:c])
        da1_n, _, _ = ln_back(a1n_ref[...], dn_ref[:, 0:c])
        da1buf[0:ts, :] = da1_m
        da1buf[ts:, :] = jnp.where(last, 0.0, da1_n)
        dlng_ref[...] += jnp.sum(da2_m * xhat_m, axis=0, keepdims=True)
        dlnb_ref[...] += jnp.sum(da2_m, axis=0, keepdims=True)
        dba_ref[...] += jnp.sum(da1_m, axis=0, keepdims=True)

        dglu = jnp.zeros((ts, c), F32)
        for k in range(ka):
            off = (ka - 1) - k
            dglu = dglu + da1buf[off:off + ts, :] * wa_ref[k:k + 1, :]
            goff = A_HALO - (ka - 1) + k
            dwa_ref[k:k + 1, :] += jnp.sum(da1_m * gbuf[goff:goff + ts, :], axis=0, keepdims=True)
        dp_ref[:, 0:c] = (dglu * sig_gate).astype(dp_ref.dtype)
        dp_ref[:, c:2 * c] = (dglu * p_ref[:, 0:c] * sig_gate * (1.0 - sig_gate)).astype(dp_ref.dtype)

        cv = jnp.zeros((ts, c), F32)
        for k in range(kb):
            off = A_HALO - (kb - 1) + k
            cv = cv + cbuf[off:off + ts, :] * wb_ref[k:k + 1, :]
        db_m = d_ref[:, c:2 * c]
        dp_ref[:, 2 * c:3 * c] = (db_m * cv).astype(dp_ref.dtype)
        dcv_m = db_m * p_ref[:, 2 * c:3 * c]
        dcvbuf[0:ts, :] = dcv_m
        dcvbuf[ts:, :] = jnp.where(last, 0.0, dn_ref[:, c:2 * c] * pn_ref[:, 2 * c:3 * c])
        dbc = jnp.zeros((ts, c), F32)
        for k in range(kb):
            off = (kb - 1) - k
            dbc = dbc + dcvbuf[off:off + ts, :] * wb_ref[k:k + 1, :]
            coff = A_HALO - (kb - 1) + k
            dwb_ref[k:k + 1, :] += jnp.sum(dcv_m * cbuf[coff:coff + ts, :], axis=0, keepdims=True)
        dp_ref[:, 3 * c:4 * c] = (dbc * p_ref[:, 4 * c:5 * c]).astype(dp_ref.dtype)
        dp_ref[:, 4 * c:5 * c] = (dbc * p_ref[:, 3 * c:4 * c]).astype(dp_ref.dtype)

    return _carrying_call(
        body,
        cargo,
        name=name,
        grid=(n_row,),
        in_specs=[p_main, p_prev, p_next, a_main, a_next, d_main, d_next,
                  full(wa.shape), full(lng.shape), full(lnb.shape), full(wb.shape)],
        out_specs=(pl.BlockSpec((ts, c5), lambda i: (i, 0)), full((32, c)), full((1, c)), full((1, c)),
                   full((1, c)), full((8, c))),
        out_shape=(
            jax.ShapeDtypeStruct((s, c5), BF16), jax.ShapeDtypeStruct((32, c), F32),
            jax.ShapeDtypeStruct((1, c), F32), jax.ShapeDtypeStruct((1, c), F32),
            jax.ShapeDtypeStruct((1, c), F32), jax.ShapeDtypeStruct((8, c), F32),
        ),
        scratch_shapes=[
            pltpu.VMEM((A_HALO + ts, c), F32), pltpu.VMEM((A_HALO + ts, c), F32),
            pltpu.VMEM((ext, c), F32), pltpu.VMEM((ext, c), F32),
        ],
        operands=(p, p, p, a1, a1, dab, dab, wa, lng, lnb, wb),
        compiler_params=_params(("arbitrary",)),
    )


def _head_pair_geometry(width, dh):
    assert 2 * dh == LANES, "the attention kernels pair two heads in one 128-lane block"
    return width // (3 * LANES)


def _group_sum(v, lo):
    s_lo = jnp.sum(jnp.where(lo, v, 0.0), axis=-1, keepdims=True)
    s_hi = jnp.sum(jnp.where(lo, 0.0, v), axis=-1, keepdims=True)
    return jnp.where(lo, s_lo, s_hi)


def _qkv_prep_fwd(qkv, gains, dh, *, name):
    s, width = qkv.shape
    nsec = _head_pair_geometry(width, dh)
    tr = _tile(s, (512, 256, 128))
    blk = pl.BlockSpec((tr, nsec * LANES), lambda i, sec: (i, sec))

    def body(x_ref, g_ref, o_ref):
        sec = pl.program_id(1)

        @pl.when(sec == 2)
        def _():
            o_ref[...] = x_ref[...].astype(o_ref.dtype)

        @pl.when(sec != 2)
        def _():
            lo = lax.broadcasted_iota(jnp.int32, (1, LANES), 1) < dh
            for lb in range(nsec):
                ls = pl.ds(lb * LANES, LANES)
                xv = x_ref[:, ls].astype(F32)
                rs = lax.rsqrt(_group_sum(xv * xv, lo) * (1.0 / dh) + EPS)
                o_ref[:, ls] = (xv * rs * g_ref[0:1, :]).astype(o_ref.dtype)

    return pl.pallas_call(
        body,
        name=name,
        grid=(s // tr, 3),
        in_specs=[blk, pl.BlockSpec((8, LANES), lambda i, sec: (sec, 0))],
        out_specs=blk,
        out_shape=jax.ShapeDtypeStruct((s, width), BF16),
        compiler_params=_params(("parallel", "parallel")),
    )(qkv, gains)


def _qkv_prep_bwd(qkv, dq, dk, dv, gains, dh, *, name):
    s, width = qkv.shape
    nsec = _head_pair_geometry(width, dh)
    tr = _tile(s, (512, 256, 128))
    blk = pl.BlockSpec((tr, nsec * LANES), lambda sec, i: (i, sec))
    gspec = pl.BlockSpec((8, LANES), lambda sec, i: (sec, 0))

    def grad_spec(own):
        return pl.BlockSpec((tr, nsec * LANES), lambda sec, i: (jnp.where(sec == own, i, 0), 0))

    def body(x_ref, dq_ref, dk_ref, dv_ref, g_ref, o_ref, dg_ref):
        sec = pl.program_id(0)

        @pl.when(pl.program_id(1) == 0)
        def _():
            dg_ref[...] = jnp.zeros_like(dg_ref)

        def normed(d_ref):
            lo = lax.broadcasted_iota(jnp.int32, (1, LANES), 1) < dh
            dg = jnp.zeros((1, LANES), F32)
            for lb in range(nsec):
                ls = pl.ds(lb * LANES, LANES)
                xv = x_ref[:, ls].astype(F32)
                d = d_ref[:, ls]
                rs = lax.rsqrt(_group_sum(xv * xv, lo) * (1.0 / dh) + EPS)
                xn = xv * rs
                dyg = d * g_ref[0:1, :]
                dx = rs * (dyg - xn * (_group_sum(dyg * xn, lo) * (1.0 / dh)))
                o_ref[:, ls] = dx.astype(o_ref.dtype)
                dg = dg + jnp.sum(d * xn, axis=0, keepdims=True)
            dg_ref[...] += jnp.broadcast_to(dg, dg_ref.shape)

        @pl.when(sec == 0)
        def _():
            normed(dq_ref)

        @pl.when(sec == 1)
        def _():
            normed(dk_ref)

        @pl.when(sec == 2)
        def _():
            o_ref[...] = dv_ref[...].astype(o_ref.dtype)

    return pl.pallas_call(
        body,
        name=name,
        grid=(3, s // tr),
        in_specs=[blk, grad_spec(0), grad_spec(1), grad_spec(2), gspec],
        out_specs=(blk, gspec),
        out_shape=(jax.ShapeDtypeStruct((s, width), BF16), jax.ShapeDtypeStruct((24, LANES), F32)),
        compiler_params=_params(("arbitrary", "arbitrary")),
    )(qkv, dq, dk, dv, gains)


def _split_dot(v, tri):
    hi = v.astype(BF16)
    lo = (v - hi.astype(F32)).astype(BF16)
    return _dot(hi, tri, 1, 0) + _dot(lo, tri, 1, 0)


def _attn_scores(qm, kb, tri_gt, valid, r_run):
    z = _dot(qm, kb, 1, 1)
    zl = jnp.minimum(z, 0.0) - jnp.log(1.0 + jnp.exp(-jnp.abs(z)))
    lk = zl - z
    if valid is not None:
        lk = jnp.where(valid, lk, 0.0)
    after = _split_dot(lk, tri_gt)
    w = jnp.exp(zl + after + r_run)
    if valid is not None:
        w = jnp.where(valid, w, 0.0)
    return zl, lk, w


def _attn_alive(rr):
    return jnp.max(jnp.maximum(rr[0], rr[1])) > ATTN_DEAD_LOG_WEIGHT


def _attn_walk_variants(i, walk):
    @pl.when(i == 0)
    def _():
        walk(False)

    @pl.when(i > 0)
    def _():
        walk(True)


def _attn_specs(s, width, dh, tq):
    nsec = _head_pair_geometry(width, dh)
    qspec = pl.BlockSpec((tq, LANES), lambda h, i: (i, h))
    kspec = pl.BlockSpec((s, LANES), lambda h, i: (0, nsec + h))
    vspec = pl.BlockSpec((s, LANES), lambda h, i: (0, 2 * nsec + h))
    return nsec, qspec, kspec, vspec


def _attn_masks(tq, tk, dh):
    lo = lax.broadcasted_iota(jnp.int32, (1, LANES), 1) < dh
    r_io = lax.broadcasted_iota(jnp.int32, (tk, tk), 0)
    c_io = lax.broadcasted_iota(jnp.int32, (tk, tk), 1)
    qrow = lax.broadcasted_iota(jnp.int32, (tq, tk), 0)
    kcol = lax.broadcasted_iota(jnp.int32, (tq, tk), 1)
    return lo, r_io, c_io, qrow, kcol


def _attn_fwd(qkv16, dh, *, gather_cargo=(), name):
    s, width = qkv16.shape
    tq = _tile(s, (ATTN_Q_BLOCK, ATTN_K_BLOCK))
    tk = _tile(tq, (ATTN_K_BLOCK,))
    nd = tq // tk
    nsec, qspec, kspec, vspec = _attn_specs(s, width, dh, tq)

    def body(q_ref, k_ref, v_ref, o_ref):
        i = pl.program_id(1)
        lo, r_io, c_io, qrow, kcol = _attn_masks(tq, tk, dh)
        q2 = q_ref[...]
        zq = jnp.zeros_like(q2)
        qs = (jnp.where(lo, q2, zq), jnp.where(lo, zq, q2))
        tri_gt = (r_io > c_io).astype(BF16)

        def block(j, carry, diag):
            rr, acc = carry
            start = pl.multiple_of(j * tk, tk)
            kb = k_ref[pl.ds(start, tk), :]
            vb = v_ref[pl.ds(start, tk), :]
            valid = None if diag is None else kcol + diag * tk < qrow
            outs, new_r = [], []
            for hh in range(2):
                _, lk, w = _attn_scores(qs[hh], kb, tri_gt, valid, rr[hh])
                outs.append(_dot(w.astype(BF16), vb, 1, 0))
                new_r.append(rr[hh] + jnp.sum(lk, axis=-1, keepdims=True))
            return tuple(new_r), acc + jnp.where(lo, outs[0], outs[1])

        below = i * nd

        def step(st):
            jj, _, cr = st
            cr = block(below - 1 - jj, cr, None)
            return jj + 1, _attn_alive(cr[0]), cr

        def walk(inline_first_below):
            zero = jnp.zeros((tq, 1), F32)
            carry = ((zero, zero), jnp.zeros((tq, LANES), F32))
            for dg in reversed(range(nd)):
                carry = block(i * nd + dg, carry, dg)
            if inline_first_below:
                carry = block(below - 1, carry, None)
            _, _, carry = lax.while_loop(lambda st: (st[0] < below) & st[1], step,
                                         (jnp.int32(int(inline_first_below)), _attn_alive(carry[0]), carry))
            o_ref[...] = carry[1]

        _attn_walk_variants(i, walk)

    outs, gathered = _carrying_call(
        body,
        gather_cargo,
        gather=True,
        name=name,
        grid=(nsec, s // tq),
        in_specs=[qspec, kspec, vspec],
        out_specs=[qspec],
        out_shape=[jax.ShapeDtypeStruct((s, width // 3), F32)],
        scratch_shapes=[],
        operands=(qkv16, qkv16, qkv16),
        compiler_params=_params(("arbitrary", "arbitrary")),
    )
    return outs[0], gathered


def _attn_bwd(qkv16, o, do, dh, *, cargo=(), name):
    s, width = qkv16.shape
    tq = _tile(s, (ATTN_Q_BLOCK, ATTN_K_BLOCK))
    tk = _tile(tq, (ATTN_K_BLOCK,))
    nd = tq // tk
    nsec, qspec, kspec, vspec = _attn_specs(s, width, dh, tq)
    accspec = pl.BlockSpec((s, LANES), lambda h, i: (0, h))

    def body(q_ref, k_ref, v_ref, o_ref, do_ref, dq_ref, dk_ref, dv_ref):
        i = pl.program_id(1)

        @pl.when(i == 0)
        def _():
            dk_ref[...] = jnp.zeros_like(dk_ref)
            dv_ref[...] = jnp.zeros_like(dv_ref)

        lo, r_io, c_io, qrow, kcol = _attn_masks(tq, tk, dh)
        q2 = q_ref[...]
        zq = jnp.zeros_like(q2)
        qs = (jnp.where(lo, q2, zq), jnp.where(lo, zq, q2))
        do16 = do_ref[...].astype(BF16)
        dos = (jnp.where(lo, do16, zq), jnp.where(lo, zq, do16))
        prod = do16.astype(F32) * o_ref[...]
        deltas = (jnp.sum(jnp.where(lo, prod, 0.0), axis=-1, keepdims=True),
                  jnp.sum(jnp.where(lo, 0.0, prod), axis=-1, keepdims=True))
        tri_gt = (r_io > c_io).astype(BF16)
        tri_ge = (r_io >= c_io).astype(BF16)

        def block(j, carry, diag):
            rr, er, dq = carry
            start = pl.multiple_of(j * tk, tk)
            kb = k_ref[pl.ds(start, tk), :]
            vb = v_ref[pl.ds(start, tk), :]
            valid = None if diag is None else kcol + diag * tk < qrow
            new_r, new_e, dqs = [], [], []
            dk_blk = dv_blk = None
            for hh in range(2):
                zl, lk, w = _attn_scores(qs[hh], kb, tri_gt, valid, rr[hh])
                beta = jnp.exp(zl)
                w16 = w.astype(BF16)
                e = _dot(dos[hh], vb, 1, 1) * w16.astype(F32)
                e_before = deltas[hh] - er[hh] - _split_dot(e, tri_ge)
                dz = e - beta * (e + e_before)
                if valid is not None:
                    dz = jnp.where(valid, dz, 0.0)
                dz16 = dz.astype(BF16)
                dqs.append(_dot(dz16, kb, 1, 0))
                dkc = _dot(dz16, qs[hh], 0, 0)
                dvc = _dot(w16, dos[hh], 0, 0)
                dk_blk = dkc if dk_blk is None else dk_blk + dkc
                dv_blk = dvc if dv_blk is None else dv_blk + dvc
                new_r.append(rr[hh] + jnp.sum(lk, axis=-1, keepdims=True))
                new_e.append(er[hh] + jnp.sum(e, axis=-1, keepdims=True))
            dk_ref[pl.ds(start, tk), :] += dk_blk
            dv_ref[pl.ds(start, tk), :] += dv_blk
            return tuple(new_r), tuple(new_e), dq + jnp.where(lo, dqs[0], dqs[1])

        below = i * nd

        def step(st):
            jj, _, cr = st
            cr = block(below - 1 - jj, cr, None)
            return jj + 1, _attn_alive(cr[0]), cr

        def walk(inline_first_below):
            zero = jnp.zeros((tq, 1), F32)
            carry = ((zero, zero), (zero, zero), jnp.zeros((tq, LANES), F32))
            for dg in reversed(range(nd)):
                carry = block(i * nd + dg, carry, dg)
            if inline_first_below:
                carry = block(below - 1, carry, None)
            _, _, carry = lax.while_loop(lambda st: (st[0] < below) & st[1], step,
                                         (jnp.int32(int(inline_first_below)), _attn_alive(carry[0]), carry))
            dq_ref[...] = carry[2]

        _attn_walk_variants(i, walk)

    shp = jax.ShapeDtypeStruct((s, width // 3), F32)
    return _carrying_call(
        body,
        cargo,
        name=name,
        grid=(nsec, s // tq),
        in_specs=[qspec, kspec, vspec, qspec, qspec],
        out_specs=(qspec, accspec, accspec),
        out_shape=(shp, shp, shp),
        scratch_shapes=[],
        operands=(qkv16, qkv16, qkv16, o, do),
        compiler_params=_params(("arbitrary", "arbitrary")),
    )


def _adam_update(wv, gv, mv, vv):
    c1 = 1.0 - ADAM_B1 ** ADAM_STEP
    c2 = 1.0 - ADAM_B2 ** ADAM_STEP
    nm = ADAM_B1 * mv + (1.0 - ADAM_B1) * gv
    nv = ADAM_B2 * vv + (1.0 - ADAM_B2) * (gv * gv)
    delta = -ADAM_LR * ((nm / c1) / (jnp.sqrt(nv / c2) + ADAM_EPS) + ADAM_WD * wv)
    return delta, nm, nv


def _sum_slabs(r, *, name):
    n, rows, lanes = r.shape
    tr = _tile(rows, (1024, 512, 256, 128, 64, 32, 16))

    def body(r_ref, o_ref):
        acc = r_ref[0]
        for d in range(1, n):
            acc = acc + r_ref[d]
        o_ref[...] = acc

    return pl.pallas_call(
        body,
        name=name,
        grid=(rows // tr,),
        in_specs=[pl.BlockSpec((n, tr, lanes), lambda i: (0, i, 0))],
        out_specs=pl.BlockSpec((tr, lanes), lambda i: (i, 0)),
        out_shape=jax.ShapeDtypeStruct((rows, lanes), F32),
        compiler_params=_params(("parallel",)),
    )(r)


def _adamw(w, g, m, v, *, name):
    rows, cols = w.shape
    tr = _tile(rows, (1024, 512, 256, 128, 64, 32, 16))
    spec = pl.BlockSpec((tr, cols), lambda i: (i, 0))

    def body(w_ref, g_ref, m_ref, v_ref, d_ref, nm_ref, nv_ref):
        d_ref[...], nm_ref[...], nv_ref[...] = _adam_update(w_ref[...], g_ref[...], m_ref[...], v_ref[...])

    shp = jax.ShapeDtypeStruct((rows, cols), F32)
    return pl.pallas_call(
        body,
        name=name,
        grid=(rows // tr,),
        in_specs=[spec, spec, spec, spec],
        out_specs=(spec, spec, spec),
        out_shape=(shp, shp, shp),
        compiler_params=_params(("parallel",)),
    )(w, g, m, v)


def _adamw_sum(w, received, m, v, *, name):
    layers, rows, cols = w.shape
    n = received[0].shape[0]
    tr = _tile(rows, (256, 128, 64, 32, 16))
    spec = pl.BlockSpec((1, tr, cols), lambda li, i: (li, i, 0))

    def partial_spec(own):
        return pl.BlockSpec((n, tr, cols), lambda li, i: (0, jnp.where(li == own, i, 0), 0))

    def body(w_ref, *refs):
        r_refs, (m_ref, v_ref, g_ref, d_ref, nm_ref, nv_ref) = refs[:layers], refs[layers:]
        for own in range(layers):
            @pl.when(pl.program_id(0) == own)
            def _(r_ref=r_refs[own]):
                gv = r_ref[0].astype(F32)
                for d in range(1, n):
                    gv = gv + r_ref[d].astype(F32)
                g_ref[0] = gv
                d_ref[0], nm_ref[0], nv_ref[0] = _adam_update(w_ref[0], gv, m_ref[0], v_ref[0])

    shp = jax.ShapeDtypeStruct((layers, rows, cols), F32)
    return pl.pallas_call(
        body,
        name=name,
        grid=(layers, rows // tr),
        in_specs=[spec] + [partial_spec(own) for own in range(layers)] + [spec, spec],
        out_specs=(spec, spec, spec, spec),
        out_shape=(shp, shp, shp, shp),
        compiler_params=_params(("arbitrary", "arbitrary")),
    )(w, *received, m, v)


_MATMUL_WEIGHTS = (("conv_w_in", 2), ("conv_w_out", 1), ("attn_w_qkv", 2), ("attn_w_o", 1), ("ffn_w_up", 2),
                   ("ffn_w_down", 1))
_TAP_WEIGHTS = (("conv_a_dw_w", 2), ("conv_b_dw_w", 2), ("ffn_dw_w", 2))
_REPLICATED = ("mix_norm_g", "ffn_norm_g", "conv_a_dw_b", "conv_a_ln_g", "conv_a_ln_b", "attn_q_g", "attn_k_g",
               "ffn_dw_b")
_WEIGHT_ORDER = ("mix_norm_g", "ffn_norm_g", "conv_w_in", "conv_a_dw_w", "conv_a_dw_b", "conv_a_ln_g",
                 "conv_a_ln_b", "conv_b_dw_w", "conv_w_out", "attn_w_qkv", "attn_q_g", "attn_k_g", "attn_w_o",
                 "ffn_w_up", "ffn_dw_w", "ffn_dw_b", "ffn_w_down")


def _assemble(gathered, axis):
    n, l, a, b = gathered.shape
    if axis == 2:
        return jnp.transpose(gathered, (1, 2, 0, 3)).reshape(l, a, n * b)
    return jnp.transpose(gathered, (1, 0, 2, 3)).reshape(l, n * a, b)


def _split_shards(full, axis):
    l, a, b = full.shape
    if axis == 2:
        return jnp.transpose(full.reshape(l, a, N_DEV, b // N_DEV), (2, 0, 1, 3))
    return jnp.transpose(full.reshape(l, N_DEV, a // N_DEV, b), (1, 0, 2, 3))


def kernel(x, mix_norm_g, ffn_norm_g, conv_w_in, conv_a_dw_w, conv_a_dw_b, conv_a_ln_g, conv_a_ln_b, conv_b_dw_w, conv_w_out, attn_w_qkv, attn_q_g, attn_k_g, attn_w_o, ffn_w_up, ffn_dw_w, ffn_dw_b, ffn_w_down, loss_target, m_mix_norm_g, m_ffn_norm_g, m_conv_w_in, m_conv_a_dw_w, m_conv_a_dw_b, m_conv_a_ln_g, m_conv_a_ln_b, m_conv_b_dw_w, m_conv_w_out, m_attn_w_qkv, m_attn_q_g, m_attn_k_g, m_attn_w_o, m_ffn_w_up, m_ffn_dw_w, m_ffn_dw_b, m_ffn_w_down, v_mix_norm_g, v_ffn_norm_g, v_conv_w_in, v_conv_a_dw_w, v_conv_a_dw_b, v_conv_a_ln_g, v_conv_a_ln_b, v_conv_b_dw_w, v_conv_w_out, v_attn_w_qkv, v_attn_q_g, v_attn_k_g, v_attn_w_o, v_ffn_w_up, v_ffn_dw_w, v_ffn_dw_b, v_ffn_w_down):
    args = dict(locals())
    weights = {n: args[n] for n in _WEIGHT_ORDER}
    mom_m = {n: args["m_" + n] for n in _WEIGHT_ORDER}
    mom_v = {n: args["v_" + n] for n in _WEIGHT_ORDER}

    _, s, d = x.shape
    depth = mix_norm_g.shape[0]
    dh = attn_q_g.shape[1]
    x0 = x.reshape(s, d)
    target = loss_target.reshape(s, d)

    shard_axis = dict(_MATMUL_WEIGHTS)
    shard16 = {n: weights[n].astype(BF16) for n, _ in _MATMUL_WEIGHTS}
    full = {}

    def layer_keys(layer):
        if layer >= depth:
            return []
        mixer = ("conv_w_in", "conv_w_out") if layer % 2 == 0 else ("attn_w_qkv", "attn_w_o")
        return [(mixer[0], layer // 2), (mixer[1], layer // 2), ("ffn_w_up", layer), ("ffn_w_down", layer)]

    def shards(keys):
        return [shard16[n][idx] for n, idx in keys]

    def assemble(keys, gathered):
        for (n, idx), g in zip(keys, gathered):
            if shard_axis[n] == 2:
                full[(n, idx)] = _assemble_cols(g, name=f"assemble_{n}_{idx}")
            else:
                full[(n, idx)] = g.reshape(N_DEV * g.shape[1], g.shape[2])

    mixer_in0, mixer_out0, ffn_up0, ffn_down0 = layer_keys(0)
    up_front = [mixer_in0, mixer_out0, ffn_down0]
    assemble(up_front, _all_gather(shards(up_front), "gather_first_weights"))
    tap_shapes = [weights[n].shape for n, _ in _TAP_WEIGHTS]
    g_tap = _all_gather([_pack([weights[n] for n, _ in _TAP_WEIGHTS], F32)], "gather_tap_weights")[0]
    for (n, axis), part in zip(_TAP_WEIGHTS, _unpack(g_tap, tap_shapes, lead=(N_DEV,))):
        full[n] = _assemble(part, axis)
    f = ffn_w_down.shape[1] * N_DEV

    saved = []
    xs = x0
    hn = _rmsnorm_fwd(xs, mix_norm_g[0:1], name="mix_norm_fwd_0")
    for layer in range(depth):
        i = layer // 2
        nxt = layer_keys(layer + 1)
        sv = {"x_in": xs, "h": hn}
        if layer % 2 == 0:
            on_in, on_act = ([], nxt[0:1]) if layer > 0 else (nxt[0:1], [ffn_up0])
            p, got = _matmul(hn, full[("conv_w_in", i)], gather_cargo=shards(on_in), name=f"conv_in_{layer}")
            assemble(on_in, got)
            (ab, a1), got = _conv_act_fwd(p, full["conv_a_dw_w"][i], conv_a_dw_b[i:i + 1], conv_a_ln_g[i:i + 1],
                                          conv_a_ln_b[i:i + 1], full["conv_b_dw_w"][i], gather_cargo=shards(on_act),
                                          name=f"conv_act_fwd_{layer}")
            assemble(on_act, got)
            sv.update(p=p, a1=a1, mix=ab)
            (xs, h2), _ = _matmul_add_norm(ab, full[("conv_w_out", i)], xs, ffn_norm_g[layer:layer + 1],
                                           name=f"conv_out_{layer}")
            riders = [nxt[1:2], nxt[2:3], nxt[3:4]]
        else:
            reps = LANES // dh
            gains = jnp.concatenate([
                jnp.broadcast_to(jnp.tile(attn_q_g[i], reps) * dh ** -0.5, (8, LANES)),
                jnp.broadcast_to(jnp.tile(attn_k_g[i], reps), (8, LANES)),
                jnp.ones((8, LANES), F32)], axis=0)
            qkv = _matmul(hn, full[("attn_w_qkv", i)], out_dtype=BF16, name=f"attn_qkv_{layer}")
            qkv16 = _qkv_prep_fwd(qkv, gains, dh, name=f"qkv_prep_fwd_{layer}")
            o, got = _attn_fwd(qkv16, dh, gather_cargo=shards(nxt), name=f"attn_fwd_{layer}")
            assemble(nxt, got)
            sv.update(qkv=qkv, qkv16=qkv16, gains=gains, mix=o)
            (xs, h2), _ = _matmul_add_norm(o, full[("attn_w_o", i)], xs, ffn_norm_g[layer:layer + 1],
                                           name=f"attn_out_{layer}")
            riders = [[], [], []]
        sv["x_mid"] = xs
        w_up = full[("ffn_w_up", layer)]
        ug, got = _matmul(h2, w_up, b_half=0, out_dtype=BF16, gather_cargo=shards(riders[0]),
                          name=f"ffn_up_gate_{layer}")
        assemble(riders[0], got)
        uv = _matmul(h2, w_up, b_half=1, out_dtype=BF16, name=f"ffn_up_val_{layer}")
        taps = full["ffn_dw_w"][layer]
        bias = ffn_dw_b[layer:layer + 1]
        act, got = _ffn_act_fwd(ug, uv, taps[:, :f], taps[:, f:], bias[:, :f], bias[:, f:],
                                gather_cargo=shards(riders[1]), name=f"ffn_act_fwd_{layer}")
        assemble(riders[1], got)
        sv.update(h2=h2, ug=ug, uv=uv, act=act)
        if layer + 1 < depth:
            (xs, hn), got = _matmul_add_norm(act, full[("ffn_w_down", layer)], xs, mix_norm_g[layer + 1:layer + 2],
                                             gather_cargo=shards(riders[2]), name=f"ffn_down_{layer}")
        else:
            xs, got = _matmul(act, full[("ffn_w_down", layer)], add=xs, gather_cargo=shards(riders[2]),
                              name=f"ffn_down_{layer}")
        assemble(riders[2], got)
        saved.append(sv)

    sq, dx, dx16 = _loss_head(xs, target, name="loss_head")
    loss = lax.psum(0.5 * sq[0, 0] / d, ("x", "y", "c"))

    grads = {n: [None] * weights[n].shape[0] for n in _WEIGHT_ORDER}
    queued, arrived = [], {}

    def queue(n, idx, g):
        a, b = weights[n].shape[1:]
        slabs = _split_cols(g, name=f"split_{n}_{idx}") if shard_axis[n] == 2 else g.reshape(N_DEV, a, b)
        queued.append(((n, idx), slabs))

    def carried(call):
        keys = [k for k, _ in queued]
        slabs = [sl for _, sl in queued]
        queued.clear()
        outs, received = call(slabs)
        arrived.update(zip(keys, received))
        return outs

    for layer in reversed(range(depth)):
        i = layer // 2
        sv = saved[layer]
        w_up = full[("ffn_w_up", layer)]
        taps = full["ffn_dw_w"][layer]
        bias = ffn_dw_b[layer:layer + 1]
        dact = _matmul(dx16, full[("ffn_w_down", layer)], tb=True, out_dtype=BF16, name=f"ffn_down_dx_{layer}")
        queue("ffn_w_down", layer, _weight_grad(sv["act"], dx16, name=f"ffn_down_dw_{layer}"))
        dug, duv, sumg, sumv = carried(lambda cargo: _ffn_act_bwd(
            sv["ug"], sv["uv"], dact, taps[:, :f], taps[:, f:], bias[:, :f], bias[:, f:], cargo=cargo,
            name=f"ffn_act_bwd_{layer}"))
        sums = jnp.concatenate([sumg, sumv], axis=1).reshape(4, 8, 2 * f).sum(axis=1)
        grads["ffn_dw_w"][layer] = sums[:3]
        grads["ffn_dw_b"][layer] = sums[3]
        dh2 = _matmul(dug, w_up[:, :f], tb=True, name=f"ffn_up_gate_dx_{layer}")
        (dx, dx16, dg), _ = _matmul_norm_bwd(duv, w_up[:, f:], sv["x_mid"], ffn_norm_g[layer:layer + 1], dx, add=dh2,
                                             name=f"ffn_up_val_dx_{layer}")
        grads["ffn_norm_g"][layer] = dg[0]
        queue("ffn_w_up", layer, jnp.concatenate(
            [_weight_grad(sv["h2"], dug, name=f"ffn_up_gate_dw_{layer}"),
             _weight_grad(sv["h2"], duv, name=f"ffn_up_val_dw_{layer}")], axis=1))

        if layer % 2 == 0:
            dab = _matmul(dx16, full[("conv_w_out", i)], tb=True, name=f"conv_out_dx_{layer}")
            queue("conv_w_out", i, _weight_grad(sv["mix"], dx16, name=f"conv_out_dw_{layer}"))
            ka = full["conv_a_dw_w"].shape[1]
            kb = full["conv_b_dw_w"].shape[1]
            dp, dwa, dba, dlng, dlnb, dwb = carried(lambda cargo: _conv_act_bwd(
                sv["p"], sv["a1"], dab, full["conv_a_dw_w"][i], conv_a_ln_g[i:i + 1], conv_a_ln_b[i:i + 1],
                full["conv_b_dw_w"][i], cargo=cargo, name=f"conv_act_bwd_{layer}"))
            grads["conv_a_dw_w"][i] = dwa[:ka]
            grads["conv_a_dw_b"][i] = dba[0]
            grads["conv_a_ln_g"][i] = dlng[0]
            grads["conv_a_ln_b"][i] = dlnb[0]
            grads["conv_b_dw_w"][i] = dwb[:kb]
            queue("conv_w_in", i, _weight_grad(sv["h"], dp, name=f"conv_in_dw_{layer}"))
            dy, w_in = dp, full[("conv_w_in", i)]
        else:
            do = _matmul(dx16, full[("attn_w_o", i)], tb=True, name=f"attn_out_dx_{layer}")
            queue("attn_w_o", i, _weight_grad(sv["mix"], dx16, name=f"attn_out_dw_{layer}"))
            dq, dk, dv = carried(lambda cargo: _attn_bwd(sv["qkv16"], sv["mix"], do, dh, cargo=cargo,
                                                         name=f"attn_bwd_{layer}"))
            dqkv, dgains = _qkv_prep_bwd(sv["qkv"], dq, dk, dv, sv["gains"], dh, name=f"qkv_prep_bwd_{layer}")
            grads["attn_q_g"][i] = dgains[0].reshape(-1, dh).sum(axis=0) * dh ** -0.5
            grads["attn_k_g"][i] = dgains[8].reshape(-1, dh).sum(axis=0)
            queue("attn_w_qkv", i, _weight_grad(sv["h"], dqkv, name=f"attn_qkv_dw_{layer}"))
            dy, w_in = dqkv, full[("attn_w_qkv", i)]

        def mixer_in_bwd(cargo):
            return _matmul_norm_bwd(dy, w_in, sv["x_in"], mix_norm_g[layer:layer + 1], dx, cargo=cargo,
                                    name=f"mixer_in_dx_{layer}")

        dx, dx16, dg = carried(mixer_in_bwd) if layer == 0 else mixer_in_bwd(())[0]
        grads["mix_norm_g"][layer] = dg[0]

    me = _my_index()
    final_grads, delta, new_m, new_v = {}, {}, {}, {}
    for n, _ in _MATMUL_WEIGHTS:
        final_grads[n], delta[n], new_m[n], new_v[n] = _adamw_sum(
            weights[n], [arrived[(n, idx)] for idx in range(weights[n].shape[0])], mom_m[n], mom_v[n],
            name=f"adamw_{n}")

    small_names = list(_REPLICATED) + [n for n, _ in _TAP_WEIGHTS]
    local = {n: jnp.stack(grads[n], axis=0) for n in small_names}
    small_shapes = [local[n].shape for n in small_names]
    g_small = _sum_slabs(_all_gather([_pack([local[n] for n in small_names], F32)], "gather_small_grads")[0],
                         name="sum_small_grads")
    reduced = dict(zip(small_names, _unpack(g_small, small_shapes)))
    for n in _REPLICATED:
        final_grads[n] = reduced[n]
    for n, axis in _TAP_WEIGHTS:
        final_grads[n] = lax.dynamic_index_in_dim(_split_shards(reduced[n], axis), me, axis=0, keepdims=False)
    shapes = [weights[n].shape for n in small_names]
    d_s, m_s, v_s = _adamw(
        _pack([weights[n] for n in small_names], F32), _pack([final_grads[n] for n in small_names], F32),
        _pack([mom_m[n] for n in small_names], F32), _pack([mom_v[n] for n in small_names], F32), name="adamw_small")
    delta.update(zip(small_names, _unpack(d_s, shapes)))
    new_m.update(zip(small_names, _unpack(m_s, shapes)))
    new_v.update(zip(small_names, _unpack(v_s, shapes)))

    order = list(_WEIGHT_ORDER)
    return (loss, dx.reshape(x.shape), *[final_grads[n] for n in order], *[delta[n] for n in order],
            *[new_m[n] for n in order], *[new_v[n] for n in order])
```

```python
import jax
import jax.numpy as jnp
from jax import lax
from jax.experimental import pallas as pl
from jax.experimental.pallas import tpu as pltpu

F32 = jnp.float32
BF16 = jnp.bfloat16
EPS = 1e-6
N_DEV = 8
MESH_ID = pl.DeviceIdType.MESH

ADAM_LR = 0.001
ADAM_B1 = 0.9
ADAM_B2 = 0.999
ADAM_EPS = 1e-08
ADAM_WD = 0.01
ADAM_STEP = 10

V7X_VMEM_LIMIT_BYTES = 48 * 1024 * 1024
PACK_QUANTUM = 2048
A_HALO = 32
S_HALO = 8
FFN_HALO_BLOCK = 16
LANES = 128
ATTN_Q_BLOCK = 256
ATTN_K_BLOCK = 256
MATMUL_MAX_SINGLE_K = 3072
MXU_WIDTH = 256
FFN_CHUNK_ROWS = 128
ATTN_DEAD_LOG_WEIGHT = -110.0


def _tile(n, prefs):
    for p in prefs:
        if n % p == 0:
            return p
    return n


def _params(sem=None):
    return pltpu.CompilerParams(dimension_semantics=sem, vmem_limit_bytes=V7X_VMEM_LIMIT_BYTES)


def _sigmoid(x):
    return 1.0 / (1.0 + jnp.exp(-x))


def _dot(a, b, ca, cb):
    return lax.dot_general(a, b, (((ca,), (cb,)), ((), ())), preferred_element_type=F32)


def _my_index():
    return 4 * lax.axis_index("x") + 2 * lax.axis_index("y") + lax.axis_index("c")


def _all_gather(shards, name):
    n = len(shards)

    def body(*refs):
        x_refs, out_refs = refs[:n], refs[n:2 * n]
        send_sems, recv_sems, local_sems = refs[2 * n:]
        x, y, c = lax.axis_index("x"), lax.axis_index("y"), lax.axis_index("c")
        me, sibling = (x, y, c), (x, y, 1 - c)
        chips = [(1 - x, y), (x, 1 - y), (1 - x, 1 - y)]

        def copy(a, k, block, to, src=None):
            slot = out_refs[a].at[4 * block[0] + 2 * block[1] + block[2]]
            return pltpu.make_async_remote_copy(
                src_ref=slot if src is None else src, dst_ref=slot,
                send_sem=send_sems.at[7 * a + k], recv_sem=recv_sems.at[7 * a + k],
                device_id=to, device_id_type=MESH_ID)

        mine = [pltpu.make_async_copy(x_refs[a], out_refs[a].at[4 * x + 2 * y + c], local_sems.at[a])
                for a in range(n)]
        for cp in mine:
            cp.start()
        first = []
        for a in range(n):
            first.append(copy(a, 0, me, sibling, src=x_refs[a]))
            first += [copy(a, 1 + j, me, (*chip, c), src=x_refs[a]) for j, chip in enumerate(chips)]
        for cp in first:
            cp.start()
        passed = []
        for j, chip in enumerate(chips):
            for a in range(n):
                copy(a, 1 + j, (*chip, c), me).wait_recv()
                fwd = copy(a, 4 + j, (*chip, c), sibling)
                fwd.start()
                passed.append(fwd)
        for a in range(n):
            copy(a, 0, sibling, me).wait_recv()
            for j, chip in enumerate(chips):
                copy(a, 4 + j, (*chip, 1 - c), me).wait_recv()
        for cp in first + passed:
            cp.wait_send()
        for cp in mine:
            cp.wait()

    hbm = pl.BlockSpec(memory_space=pl.ANY)
    return pl.pallas_call(
        body,
        name=name,
        out_shape=[jax.ShapeDtypeStruct((N_DEV,) + sh.shape, sh.dtype) for sh in shards],
        in_specs=[hbm] * n,
        out_specs=[hbm] * n,
        scratch_shapes=[
            pltpu.SemaphoreType.DMA((7 * n,)),
            pltpu.SemaphoreType.DMA((7 * n,)),
            pltpu.SemaphoreType.DMA((n,)),
        ],
    )(*shards)


def _exchange_semaphores(n):
    return [pltpu.SemaphoreType.DMA((7 * n,)), pltpu.SemaphoreType.DMA((7 * n,)), pltpu.SemaphoreType.DMA((n,))]


def _exchange_copies(g_refs, out_refs, send_sems, recv_sems, local_sems, gather=False):
    n = len(g_refs)
    x, y, c = lax.axis_index("x"), lax.axis_index("y"), lax.axis_index("c")
    me = 4 * x + 2 * y + c

    def part(a, d):
        return g_refs[a] if gather else g_refs[a].at[d]

    local = [pltpu.make_async_copy(part(a, me), out_refs[a].at[me], local_sems.at[a]) for a in range(n)]
    remote = []
    for k in range(1, N_DEV):
        px = (x + ((k >> 2) & 1)) % 2
        py = (y + ((k >> 1) & 1)) % 2
        pc = (c + (k & 1)) % 2
        for a in range(n):
            remote.append(pltpu.make_async_remote_copy(
                src_ref=part(a, 4 * px + 2 * py + pc), dst_ref=out_refs[a].at[me],
                send_sem=send_sems.at[7 * a + k - 1], recv_sem=recv_sems.at[7 * a + k - 1],
                device_id=(px, py, pc), device_id_type=MESH_ID))
    return local, remote


def _exchange_start(copies):
    for cp in copies[0] + copies[1]:
        cp.start()


def _exchange_wait(copies):
    local, remote = copies
    for cp in remote:
        cp.wait_recv()
    for cp in remote:
        cp.wait_send()
    for cp in local:
        cp.wait()


def _carrying_call(body, cargo, *, name, grid, in_specs, out_specs, out_shape, scratch_shapes, operands,
                   compiler_params, gather=False):
    n = len(cargo)
    n_in, n_out, n_scr = len(in_specs), len(out_specs), len(scratch_shapes)
    if n == 0:
        return pl.pallas_call(body, name=name, grid=grid, in_specs=in_specs, out_specs=out_specs,
                              out_shape=out_shape, scratch_shapes=scratch_shapes,
                              compiler_params=compiler_params)(*operands), []

    def carrying(*refs):
        cuts = [0, n_in, n_in + n, n_in + n + n_out, n_in + 2 * n + n_out, n_in + 2 * n + n_out + n_scr, len(refs)]
        ins, g_refs, outs, r_refs, scr, sems = [refs[a:b] for a, b in zip(cuts[:-1], cuts[1:])]
        steps = [pl.program_id(ax) for ax in range(len(grid))]
        first = steps[0] == 0
        last = steps[0] == grid[0] - 1
        for ax in range(1, len(grid)):
            first = first & (steps[ax] == 0)
            last = last & (steps[ax] == grid[ax] - 1)

        @pl.when(first)
        def _():
            _exchange_start(_exchange_copies(g_refs, r_refs, *sems, gather=gather))

        body(*ins, *outs, *scr)

        @pl.when(last)
        def _():
            _exchange_wait(_exchange_copies(g_refs, r_refs, *sems, gather=gather))

    hbm = pl.BlockSpec(memory_space=pl.ANY)
    lead = (N_DEV,) if gather else ()
    res = pl.pallas_call(
        carrying,
        name=name,
        grid=grid,
        in_specs=list(in_specs) + [hbm] * n,
        out_specs=list(out_specs) + [hbm] * n,
        out_shape=list(out_shape) + [jax.ShapeDtypeStruct(lead + g.shape, g.dtype) for g in cargo],
        scratch_shapes=list(scratch_shapes) + _exchange_semaphores(n),
        compiler_params=compiler_params,
    )(*operands, *cargo)
    return res[:n_out], res[n_out:]


def _assemble_cols(g, *, name):
    n, r, w = g.shape
    tr = _tile(r, (256, 128, 64, 32, 16))

    def body(g_ref, o_ref):
        for j in range(n):
            o_ref[:, j * w:(j + 1) * w] = g_ref[j]

    return pl.pallas_call(
        body,
        name=name,
        grid=(r // tr,),
        in_specs=[pl.BlockSpec((n, tr, w), lambda i: (0, i, 0))],
        out_specs=pl.BlockSpec((tr, n * w), lambda i: (i, 0)),
        out_shape=jax.ShapeDtypeStruct((r, n * w), g.dtype),
        compiler_params=_params(("parallel",)),
    )(g)


def _split_cols(x, *, name):
    r, nw = x.shape
    w = nw // N_DEV
    tr = _tile(r, (256, 128, 64, 32, 16))

    def body(x_ref, o_ref):
        for j in range(N_DEV):
            o_ref[j] = x_ref[:, j * w:(j + 1) * w]

    return pl.pallas_call(
        body,
        name=name,
        grid=(r // tr,),
        in_specs=[pl.BlockSpec((tr, nw), lambda i: (i, 0))],
        out_specs=pl.BlockSpec((N_DEV, tr, w), lambda i: (0, i, 0)),
        out_shape=jax.ShapeDtypeStruct((N_DEV, r, w), x.dtype),
        compiler_params=_params(("parallel",)),
    )(x)


def _padded(n):
    return -(-n // PACK_QUANTUM) * PACK_QUANTUM


def _pack(parts, dtype, lead=()):
    flat = []
    for p in parts:
        p = p.astype(dtype).reshape(lead + (-1,))
        n = p.shape[-1]
        flat.append(jnp.pad(p, [(0, 0)] * len(lead) + [(0, _padded(n) - n)]))
    return jnp.concatenate(flat, axis=-1).reshape(lead + (-1, 128))


def _unpack(buf, shapes, lead=()):
    flat = buf.reshape(lead + (-1,))
    out, off = [], 0
    for shp in shapes:
        n = 1
        for d in shp:
            n *= d
        out.append(flat[..., off:off + n].reshape(lead + tuple(shp)))
        off += _padded(n)
    return out


def _half_if_tiled(n):
    half = n // 2
    return (half,) if n % 2 == 0 and half % LANES == 0 and half <= 1536 else ()


def _matmul(a, b, *, ta=False, tb=False, b_half=None, out_dtype=F32, add=None, gather_cargo=None, name):
    if ta:
        kdim, m = a.shape
    else:
        m, kdim = a.shape
    halves = 1 if b_half is None else 2
    n = b.shape[0] if tb else b.shape[1] // halves
    bm = _tile(m, (1024,) + _half_if_tiled(m) + (512, 256, 128))
    bn = _tile(n, (512, 256, 128))
    if n % MXU_WIDTH == 0 and n <= MATMUL_MAX_SINGLE_K:
        bn, bm = n, _tile(m, (512,) + _half_if_tiled(m) + (256, 128))
    bk = kdim if kdim <= MATMUL_MAX_SINGLE_K else _tile(kdim, (1024, 512, 256, 128))
    nk = kdim // bk

    a_spec = pl.BlockSpec((bk, bm), lambda i, j, k: (k, i)) if ta else pl.BlockSpec((bm, bk), lambda i, j, k: (i, k))
    col0 = 0 if b_half is None else b_half * ((nk if tb else n // bn))
    b_spec = (pl.BlockSpec((bn, bk), lambda i, j, k: (j, k + col0)) if tb
              else pl.BlockSpec((bk, bn), lambda i, j, k: (k, j + col0)))
    o_spec = pl.BlockSpec((bm, bn), lambda i, j, k: (i, j))
    in_specs = [a_spec, b_spec] + ([o_spec] if add is not None else [])
    operands = [a, b] + ([add] if add is not None else [])

    def product(a_ref, b_ref):
        return _dot(a_ref[...].astype(BF16), b_ref[...].astype(BF16), 0 if ta else 1, 1 if tb else 0)

    def finish(r, refs):
        if add is not None:
            r = r + refs[2][...]
        return r

    def body_single(*refs):
        o_ref = refs[-1]
        o_ref[...] = finish(product(refs[0], refs[1]), refs).astype(o_ref.dtype)

    def body_multi(*refs):
        o_ref, acc_ref = refs[-2], refs[-1]
        k = pl.program_id(2)

        @pl.when(k == 0)
        def _():
            acc_ref[...] = jnp.zeros_like(acc_ref)

        acc_ref[...] += product(refs[0], refs[1])

        @pl.when(k == nk - 1)
        def _():
            o_ref[...] = finish(acc_ref[...], refs).astype(o_ref.dtype)

    outs, received = _carrying_call(
        body_single if nk == 1 else body_multi,
        () if gather_cargo is None else gather_cargo,
        gather=True,
        name=name,
        grid=(m // bm, n // bn, nk),
        in_specs=in_specs,
        out_specs=[o_spec],
        out_shape=[jax.ShapeDtypeStruct((m, n), out_dtype)],
        scratch_shapes=[] if nk == 1 else [pltpu.VMEM((bm, bn), F32)],
        operands=operands,
        compiler_params=_params(("parallel", "parallel", "arbitrary") if gather_cargo is None else ("arbitrary",) * 3),
    )
    return outs[0] if gather_cargo is None else (outs[0], received)


def _rows_matmul(a, b, *, tb, add, row_ins, vec_ins, epilogue, row_out_dtypes, n_sums, cargo, gather, name):
    m, kdim = a.shape
    n = b.shape[0] if tb else b.shape[1]
    assert kdim <= MATMUL_MAX_SINGLE_K
    b_spec = pl.BlockSpec(b.shape, lambda i: (0, 0))
    bm = _tile(m, (512, 256, 128))
    row = pl.BlockSpec((bm, n), lambda i: (i, 0))
    vec = pl.BlockSpec((1, n), lambda i: (0, 0))
    n_add = int(add is not None)
    n_ins = len(row_ins) + len(vec_ins)
    n_rows = len(row_out_dtypes)

    def body(a_ref, b_ref, *refs):
        y = _dot(a_ref[...].astype(BF16), b_ref[...].astype(BF16), 1, 1 if tb else 0)
        if add is not None:
            y = y + refs[0][...]
        ins = [r[...] for r in refs[n_add:n_add + n_ins]]
        outs = refs[n_add + n_ins:]
        rows, sums = epilogue(y, *ins)
        for ref, val in zip(outs[:n_rows], rows):
            ref[...] = val.astype(ref.dtype)
        if n_sums:
            @pl.when(pl.program_id(0) == 0)
            def _():
                for ref in outs[n_rows:]:
                    ref[...] = jnp.zeros_like(ref)

            for ref, val in zip(outs[n_rows:], sums):
                ref[...] += val

    return _carrying_call(
        body,
        cargo,
        gather=gather,
        name=name,
        grid=(m // bm,),
        in_specs=[pl.BlockSpec((bm, kdim), lambda i: (i, 0)), b_spec]
        + [row] * (n_add + len(row_ins)) + [vec] * len(vec_ins),
        out_specs=[row] * n_rows + [vec] * n_sums,
        out_shape=[jax.ShapeDtypeStruct((m, n), dt) for dt in row_out_dtypes]
        + [jax.ShapeDtypeStruct((1, n), F32)] * n_sums,
        scratch_shapes=[],
        operands=[a, b] + ([add] if add is not None else []) + list(row_ins) + list(vec_ins),
        compiler_params=_params(("arbitrary",)),
    )


def _matmul_add_norm(a, b, add, g, *, gather_cargo=(), name):
    def epilogue(y, gv):
        h = y * lax.rsqrt(jnp.mean(y * y, axis=-1, keepdims=True) + EPS) * gv
        return (y, h), ()

    return _rows_matmul(a, b, tb=False, add=add, row_ins=(), vec_ins=(g,), epilogue=epilogue,
                        row_out_dtypes=(F32, BF16), n_sums=0, cargo=gather_cargo, gather=True, name=name)


def _matmul_norm_bwd(a, b, x, g, res, *, add=None, cargo=(), name):
    def epilogue(dyv, xv, resv, gv):
        rs = lax.rsqrt(jnp.mean(xv * xv, axis=-1, keepdims=True) + EPS)
        xn = xv * rs
        dyg = dyv * gv
        dx = rs * (dyg - xn * jnp.mean(dyg * xn, axis=-1, keepdims=True)) + resv
        return (dx, dx), (jnp.sum(dyv * xn, axis=0, keepdims=True),)

    return _rows_matmul(a, b, tb=True, add=add, row_ins=(x, res), vec_ins=(g,), epilogue=epilogue,
                        row_out_dtypes=(F32, BF16), n_sums=1, cargo=cargo, gather=False, name=name)


def _weight_grad(x, dy, *, name):
    return _matmul(x, dy, ta=True, out_dtype=BF16, name=name)


def _rmsnorm_fwd(x, g, *, name):
    r, d = x.shape
    tr = _tile(r, (1024, 512, 256, 128))

    def body(x_ref, g_ref, o_ref):
        xv = x_ref[...]
        y = xv * lax.rsqrt(jnp.mean(xv * xv, axis=-1, keepdims=True) + EPS)
        y = y * g_ref[...]
        o_ref[...] = y.astype(o_ref.dtype)

    return pl.pallas_call(
        body,
        name=name,
        grid=(r // tr,),
        in_specs=[pl.BlockSpec((tr, d), lambda i: (i, 0)), pl.BlockSpec((1, d), lambda i: (0, 0))],
        out_specs=pl.BlockSpec((tr, d), lambda i: (i, 0)),
        out_shape=jax.ShapeDtypeStruct((r, d), BF16),
        compiler_params=_params(("parallel",)),
    )(x, g)


def _matmul_loss(a, b, add, target, *, name):
    d = b.shape[1]

    def epilogue(y, t):
        e = y - t
        dy = e * (1.0 / d)
        return (dy, dy), (jnp.sum(e * e, axis=0, keepdims=True),)

    outs, _ = _rows_matmul(a, b, tb=False, add=add, row_ins=(target,), vec_ins=(), epilogue=epilogue,
                           row_out_dtypes=(F32, BF16), n_sums=1, cargo=(), gather=True, name=name)
    return outs


def _ffn_tiles(s, f):
    return _tile(s, (512, 256, 128)), _tile(f, _half_if_tiled(f) + (512, 256, 128))


def _conv3(xs, w, b):
    return b + xs[0] * w[0:1] + xs[1] * w[1:2] + xs[2] * w[2:3]


def _shifted3(x, rows):
    return [x[S_HALO - 2 + k:S_HALO - 2 + k + rows] for k in range(3)]


def _ffn_act_fwd(ug, uv, wg, wv, bg, bv, *, gather_cargo=(), name):
    s, f = ug.shape
    ts, tc = _ffn_tiles(s, f)
    rc = _tile(ts, (FFN_CHUNK_ROWS, 32, 16, 8))
    hb = ts // FFN_HALO_BLOCK
    main = pl.BlockSpec((ts, tc), lambda j, i: (i, j))
    prev = pl.BlockSpec((FFN_HALO_BLOCK, tc), lambda j, i: (jnp.maximum(i * hb - 1, 0), j))
    taps = pl.BlockSpec((3, tc), lambda j, i: (0, j))
    bias = pl.BlockSpec((1, tc), lambda j, i: (0, j))
    lead = FFN_HALO_BLOCK - S_HALO

    def body(ug_ref, ugp_ref, uv_ref, uvp_ref, wg_ref, wv_ref, bg_ref, bv_ref, o_ref, gbuf, vbuf):
        first = pl.program_id(1) == 0
        for buf, p_ref, m_ref in ((gbuf, ugp_ref, ug_ref), (vbuf, uvp_ref, uv_ref)):
            buf[0:FFN_HALO_BLOCK, :] = jnp.where(first, 0.0, p_ref[...].astype(F32))
            buf[FFN_HALO_BLOCK:, :] = m_ref[...].astype(F32)

        def chunk(ci, carry):
            r0 = pl.multiple_of(ci * rc, rc)
            rows = pl.ds(pl.multiple_of(r0 + lead, S_HALO), rc + S_HALO)
            for lb in range(tc // LANES):
                ls = pl.ds(lb * LANES, LANES)
                gate = _conv3(_shifted3(gbuf[rows, ls], rc), wg_ref[:, ls], bg_ref[:, ls])
                val = _conv3(_shifted3(vbuf[rows, ls], rc), wv_ref[:, ls], bv_ref[:, ls])
                o_ref[pl.ds(r0, rc), ls] = (gate * _sigmoid(gate) * val).astype(o_ref.dtype)
            return carry

        lax.fori_loop(0, ts // rc, chunk, 0)

    outs, gathered = _carrying_call(
        body,
        gather_cargo,
        gather=True,
        name=name,
        grid=(f // tc, s // ts),
        in_specs=[main, prev, main, prev, taps, taps, bias, bias],
        out_specs=[main],
        out_shape=[jax.ShapeDtypeStruct((s, f), BF16)],
        scratch_shapes=[pltpu.VMEM((FFN_HALO_BLOCK + ts, tc), F32), pltpu.VMEM((FFN_HALO_BLOCK + ts, tc), F32)],
        operands=(ug, ug, uv, uv, wg, wv, bg, bv),
        compiler_params=_params(("arbitrary", "arbitrary")),
    )
    return outs[0], gathered


def _ffn_act_bwd(ug, uv, df, wg, wv, bg, bv, *, cargo=(), name):
    s, f = ug.shape
    ts, tc = _ffn_tiles(s, f)
    rc = _tile(ts, (FFN_CHUNK_ROWS, 32, 16, 8))
    hb = ts // FFN_HALO_BLOCK
    last_hb = s // FFN_HALO_BLOCK - 1
    n_row = s // ts
    lead = FFN_HALO_BLOCK - S_HALO
    main = pl.BlockSpec((ts, tc), lambda j, i: (i, j))
    prev = pl.BlockSpec((FFN_HALO_BLOCK, tc), lambda j, i: (jnp.maximum(i * hb - 1, 0), j))
    nxt = pl.BlockSpec((FFN_HALO_BLOCK, tc), lambda j, i: (jnp.minimum((i + 1) * hb, last_hb), j))
    taps = pl.BlockSpec((3, tc), lambda j, i: (0, j))
    bias = pl.BlockSpec((1, tc), lambda j, i: (0, j))
    sums = pl.BlockSpec((32, tc), lambda j, i: (0, j))
    ext = rc + S_HALO

    def body(ug_ref, ugp_ref, ugn_ref, uv_ref, uvp_ref, uvn_ref, df_ref, dfn_ref, wg_ref, wv_ref, bg_ref, bv_ref,
             dug_ref, duv_ref, sumg_ref, sumv_ref, gbuf, vbuf, dfbuf):
        i = pl.program_id(1)
        first = i == 0
        last = i == n_row - 1

        @pl.when(first)
        def _():
            sumg_ref[...] = jnp.zeros_like(sumg_ref)
            sumv_ref[...] = jnp.zeros_like(sumv_ref)

        for buf, p_ref, m_ref, n_ref in ((gbuf, ugp_ref, ug_ref, ugn_ref), (vbuf, uvp_ref, uv_ref, uvn_ref)):
            buf[0:FFN_HALO_BLOCK, :] = jnp.where(first, 0.0, p_ref[...].astype(F32))
            buf[FFN_HALO_BLOCK:FFN_HALO_BLOCK + ts, :] = m_ref[...].astype(F32)
            buf[FFN_HALO_BLOCK + ts:, :] = n_ref[...].astype(F32)
        dfbuf[0:ts, :] = df_ref[...].astype(F32)
        dfbuf[ts:, :] = jnp.where(last, 0.0, dfn_ref[...].astype(F32))

        def chunk(ci, carry):
            r0 = pl.multiple_of(ci * rc, rc)
            rows = pl.ds(pl.multiple_of(r0 + lead, S_HALO), ext + S_HALO)
            for lb in range(tc // LANES):
                ls = pl.ds(lb * LANES, LANES)
                xg = _shifted3(gbuf[rows, ls], ext)
                xv = _shifted3(vbuf[rows, ls], ext)
                wgv, wvv = wg_ref[:, ls], wv_ref[:, ls]
                gate = _conv3(xg, wgv, bg_ref[:, ls])
                val = _conv3(xv, wvv, bv_ref[:, ls])
                dfe = dfbuf[pl.ds(r0, ext), ls]
                sg = _sigmoid(gate)
                dgate = dfe * val * (sg * (1.0 + gate * (1.0 - sg)))
                dval = dfe * (gate * sg)
                for dz, xs, w, du_ref, sum_ref in ((dgate, xg, wgv, dug_ref, sumg_ref),
                                                   (dval, xv, wvv, duv_ref, sumv_ref)):
                    du = dz[2:2 + rc] * w[0:1] + dz[1:1 + rc] * w[1:2] + dz[0:rc] * w[2:3]
                    du_ref[pl.ds(r0, rc), ls] = du.astype(du_ref.dtype)
                    dm = dz[0:rc]
                    for k in range(3):
                        sum_ref[8 * k:8 * k + 8, ls] += (dm * xs[k][0:rc]).reshape(rc // 8, 8, LANES).sum(axis=0)
                    sum_ref[24:32, ls] += dm.reshape(rc // 8, 8, LANES).sum(axis=0)
            return carry

        lax.fori_loop(0, ts // rc, chunk, 0)

    return _carrying_call(
        body,
        cargo,
        name=name,
        grid=(f // tc, n_row),
        in_specs=[main, prev, nxt, main, prev, nxt, main, nxt, taps, taps, bias, bias],
        out_specs=(main, main, sums, sums),
        out_shape=(jax.ShapeDtypeStruct((s, f), BF16), jax.ShapeDtypeStruct((s, f), BF16),
                   jax.ShapeDtypeStruct((32, f), F32), jax.ShapeDtypeStruct((32, f), F32)),
        scratch_shapes=[pltpu.VMEM((2 * FFN_HALO_BLOCK + ts, tc), F32), pltpu.VMEM((2 * FFN_HALO_BLOCK + ts, tc), F32),
                        pltpu.VMEM((FFN_HALO_BLOCK + ts, tc), F32)],
        operands=(ug, ug, ug, uv, uv, uv, df, df, wg, wv, bg, bv),
        compiler_params=_params(("arbitrary", "arbitrary")),
    )


def _layernorm_stats(a1):
    mu = jnp.mean(a1, axis=-1, keepdims=True)
    xc = a1 - mu
    rstd = lax.rsqrt(jnp.mean(xc * xc, axis=-1, keepdims=True) + EPS)
    return xc * rstd, rstd


def _conv_act_fwd(p, wa, ba, lng, lnb, wb, *, gather_cargo=(), name):
    s, c5 = p.shape
    c = c5 // 5
    ka, kb = wa.shape[0], wb.shape[0]
    ts = _tile(s, (256, 128))
    hb = ts // A_HALO
    main = pl.BlockSpec((ts, c5), lambda i: (i, 0))
    prev = pl.BlockSpec((A_HALO, c5), lambda i: (jnp.maximum(i * hb - 1, 0), 0))

    def full(arr):
        return pl.BlockSpec(arr.shape, lambda i: (0, 0))

    def body(p_ref, pp_ref, wa_ref, ba_ref, lng_ref, lnb_ref, wb_ref, ab_ref, a1_ref, gbuf, cbuf):
        first = pl.program_id(0) == 0
        gbuf[0:A_HALO, :] = jnp.where(first, 0.0, pp_ref[:, 0:c] * _sigmoid(pp_ref[:, c:2 * c]))
        gbuf[A_HALO:, :] = p_ref[:, 0:c] * _sigmoid(p_ref[:, c:2 * c])
        cbuf[0:A_HALO, :] = jnp.where(first, 0.0, pp_ref[:, 3 * c:4 * c] * pp_ref[:, 4 * c:5 * c])
        cbuf[A_HALO:, :] = p_ref[:, 3 * c:4 * c] * p_ref[:, 4 * c:5 * c]

        a1 = jnp.broadcast_to(ba_ref[...], (ts, c))
        for k in range(ka):
            off = A_HALO - (ka - 1) + k
            a1 = a1 + gbuf[off:off + ts, :] * wa_ref[k:k + 1, :]
        a1_ref[...] = a1
        xhat, _ = _layernorm_stats(a1)
        a2 = xhat * lng_ref[...] + lnb_ref[...]
        ab_ref[:, 0:c] = (a2 * _sigmoid(a2)).astype(ab_ref.dtype)

        cv = jnp.zeros((ts, c), F32)
        for k in range(kb):
            off = A_HALO - (kb - 1) + k
            cv = cv + cbuf[off:off + ts, :] * wb_ref[k:k + 1, :]
        ab_ref[:, c:2 * c] = (p_ref[:, 2 * c:3 * c] * cv).astype(ab_ref.dtype)

    return _carrying_call(
        body,
        gather_cargo,
        gather=True,
        name=name,
        grid=(s // ts,),
        in_specs=[main, prev, full(wa), full(ba), full(lng), full(lnb), full(wb)],
        out_specs=(pl.BlockSpec((ts, 2 * c), lambda i: (i, 0)), pl.BlockSpec((ts, c), lambda i: (i, 0))),
        out_shape=(jax.ShapeDtypeStruct((s, 2 * c), BF16), jax.ShapeDtypeStruct((s, c), F32)),
        scratch_shapes=[pltpu.VMEM((A_HALO + ts, c), F32), pltpu.VMEM((A_HALO + ts, c), F32)],
        operands=(p, p, wa, ba, lng, lnb, wb),
        compiler_params=_params(("arbitrary",)),
    )


def _conv_act_bwd(p, a1, dab, wa, lng, lnb, wb, *, cargo=(), name):
    s, c5 = p.shape
    c = c5 // 5
    ka, kb = wa.shape[0], wb.shape[0]
    ts = _tile(s, (128,))
    hb = ts // A_HALO
    n_row = s // ts
    last_hb = s // A_HALO - 1
    ext = ts + A_HALO

    def rows(width):
        return (pl.BlockSpec((ts, width), lambda i: (i, 0)),
                pl.BlockSpec((A_HALO, width), lambda i: (jnp.maximum(i * hb - 1, 0), 0)),
                pl.BlockSpec((A_HALO, width), lambda i: (jnp.minimum((i + 1) * hb, last_hb), 0)))

    p_main, p_prev, p_next = rows(c5)
    d_main, _, d_next = rows(2 * c)
    a_main, _, a_next = rows(c)

    def full(shape):
        return pl.BlockSpec(shape, lambda i: (0, 0))

    def body(p_ref, pp_ref, pn_ref, a1_ref, a1n_ref, d_ref, dn_ref, wa_ref, lng_ref, lnb_ref, wb_ref,
             dp_ref, dwa_ref, dba_ref, dlng_ref, dlnb_ref, dwb_ref, gbuf, cbuf, da1buf, dcvbuf):
        i = pl.program_id(0)
        first = i == 0
        last = i == n_row - 1

        @pl.when(first)
        def _():
            dwa_ref[...] = jnp.zeros_like(dwa_ref)
            dba_ref[...] = jnp.zeros_like(dba_ref)
            dlng_ref[...] = jnp.zeros_like(dlng_ref)
            dlnb_ref[...] = jnp.zeros_like(dlnb_ref)
            dwb_ref[...] = jnp.zeros_like(dwb_ref)

        sig_gate = _sigmoid(p_ref[:, c:2 * c])
        gbuf[0:A_HALO, :] = jnp.where(first, 0.0, pp_ref[:, 0:c] * _sigmoid(pp_ref[:, c:2 * c]))
        gbuf[A_HALO:, :] = p_ref[:, 0:c] * sig_gate
        cbuf[0:A_HALO, :] = jnp.where(first, 0.0, pp_ref[:, 3 * c:4 * c] * pp_ref[:, 4 * c:5 * c])
        cbuf[A_HALO:, :] = p_ref[:, 3 * c:4 * c] * p_ref[:, 4 * c:5 * c]

        def ln_back(a1v, dav):
            xhat, rstd = _layernorm_stats(a1v)
            a2 = xhat * lng_ref[...] + lnb_ref[...]
            sg = _sigmoid(a2)
            da2 = dav * (sg * (1.0 + a2 * (1.0 - sg)))
            dxh = da2 * lng_ref[...]
            da1 = rstd * (dxh - jnp.mean(dxh, axis=-1, keepdims=True)
                          - xhat * jnp.mean(dxh * xhat, axis=-1, keepdims=True))
            return da1, da2, xhat

        da1_m, da2_m, xhat_m = ln_back(a1_ref[...], d_ref[:, 0:c])
        da1_n, _, _ = ln_back(a1n_ref[...], dn_ref[:, 0:c])
        da1buf[0:ts, :] = da1_m
        da1buf[ts:, :] = jnp.where(last, 0.0, da1_n)
        dlng_ref[...] += jnp.sum(da2_m * xhat_m, axis=0, keepdims=True)
        dlnb_ref[...] += jnp.sum(da2_m, axis=0, keepdims=True)
        dba_ref[...] += jnp.sum(da1_m, axis=0, keepdims=True)

        dglu = jnp.zeros((ts, c), F32)
        for k in range(ka):
            off = (ka - 1) - k
            dglu = dglu + da1buf[off:off + ts, :] * wa_ref[k:k + 1, :]
            goff = A_HALO - (ka - 1) + k
            dwa_ref[k:k + 1, :] += jnp.sum(da1_m * gbuf[goff:goff + ts, :], axis=0, keepdims=True)
        dp_ref[:, 0:c] = (dglu * sig_gate).astype(dp_ref.dtype)
        dp_ref[:, c:2 * c] = (dglu * p_ref[:, 0:c] * sig_gate * (1.0 - sig_gate)).astype(dp_ref.dtype)

        cv = jnp.zeros((ts, c), F32)
        for k in range(kb):
            off = A_HALO - (kb - 1) + k
            cv = cv + cbuf[off:off + ts, :] * wb_ref[k:k + 1, :]
        db_m = d_ref[:, c:2 * c]
        dp_ref[:, 2 * c:3 * c] = (db_m * cv).astype(dp_ref.dtype)
        dcv_m = db_m * p_ref[:, 2 * c:3 * c]
        dcvbuf[0:ts, :] = dcv_m
        dcvbuf[ts:, :] = jnp.where(last, 0.0, dn_ref[:, c:2 * c] * pn_ref[:, 2 * c:3 * c])
        dbc = jnp.zeros((ts, c), F32)
        for k in range(kb):
            off = (kb - 1) - k
            dbc = dbc + dcvbuf[off:off + ts, :] * wb_ref[k:k + 1, :]
            coff = A_HALO - (kb - 1) + k
            dwb_ref[k:k + 1, :] += jnp.sum(dcv_m * cbuf[coff:coff + ts, :], axis=0, keepdims=True)
        dp_ref[:, 3 * c:4 * c] = (dbc * p_ref[:, 4 * c:5 * c]).astype(dp_ref.dtype)
        dp_ref[:, 4 * c:5 * c] = (dbc * p_ref[:, 3 * c:4 * c]).astype(dp_ref.dtype)

    return _carrying_call(
        body,
        cargo,
        name=name,
        grid=(n_row,),
        in_specs=[p_main, p_prev, p_next, a_main, a_next, d_main, d_next,
                  full(wa.shape), full(lng.shape), full(lnb.shape), full(wb.shape)],
        out_specs=(pl.BlockSpec((ts, c5), lambda i: (i, 0)), full((32, c)), full((1, c)), full((1, c)),
                   full((1, c)), full((8, c))),
        out_shape=(
            jax.ShapeDtypeStruct((s, c5), BF16), jax.ShapeDtypeStruct((32, c), F32),
            jax.ShapeDtypeStruct((1, c), F32), jax.ShapeDtypeStruct((1, c), F32),
            jax.ShapeDtypeStruct((1, c), F32), jax.ShapeDtypeStruct((8, c), F32),
        ),
        scratch_shapes=[
            pltpu.VMEM((A_HALO + ts, c), F32), pltpu.VMEM((A_HALO + ts, c), F32),
            pltpu.VMEM((ext, c), F32), pltpu.VMEM((ext, c), F32),
        ],
        operands=(p, p, p, a1, a1, dab, dab, wa, lng, lnb, wb),
        compiler_params=_params(("arbitrary",)),
    )


def _head_pair_geometry(width, dh):
    assert 2 * dh == LANES, "the attention kernels pair two heads in one 128-lane block"
    return width // (3 * LANES)


def _group_sum(v, lo):
    s_lo = jnp.sum(jnp.where(lo, v, 0.0), axis=-1, keepdims=True)
    s_hi = jnp.sum(jnp.where(lo, 0.0, v), axis=-1, keepdims=True)
    return jnp.where(lo, s_lo, s_hi)


def _qkv_prep_fwd(qkv, gains, dh, *, name):
    s, width = qkv.shape
    nsec = _head_pair_geometry(width, dh)
    tr = _tile(s, (512, 256, 128))
    blk = pl.BlockSpec((tr, nsec * LANES), lambda i, sec: (i, sec))

    def body(x_ref, g_ref, o_ref):
        sec = pl.program_id(1)

        @pl.when(sec == 2)
        def _():
            o_ref[...] = x_ref[...].astype(o_ref.dtype)

        @pl.when(sec != 2)
        def _():
            lo = lax.broadcasted_iota(jnp.int32, (1, LANES), 1) < dh
            for lb in range(nsec):
                ls = pl.ds(lb * LANES, LANES)
                xv = x_ref[:, ls].astype(F32)
                rs = lax.rsqrt(_group_sum(xv * xv, lo) * (1.0 / dh) + EPS)
                o_ref[:, ls] = (xv * rs * g_ref[0:1, :]).astype(o_ref.dtype)

    return pl.pallas_call(
        body,
        name=name,
        grid=(s // tr, 3),
        in_specs=[blk, pl.BlockSpec((8, LANES), lambda i, sec: (sec, 0))],
        out_specs=blk,
        out_shape=jax.ShapeDtypeStruct((s, width), BF16),
        compiler_params=_params(("parallel", "parallel")),
    )(qkv, gains)


def _qkv_prep_bwd(qkv, dq, dk, dv, gains, dh, *, name):
    s, width = qkv.shape
    nsec = _head_pair_geometry(width, dh)
    tr = _tile(s, (512, 256, 128))
    blk = pl.BlockSpec((tr, nsec * LANES), lambda sec, i: (i, sec))
    gspec = pl.BlockSpec((8, LANES), lambda sec, i: (sec, 0))

    def grad_spec(own):
        return pl.BlockSpec((tr, nsec * LANES), lambda sec, i: (jnp.where(sec == own, i, 0), 0))

    def body(x_ref, dq_ref, dk_ref, dv_ref, g_ref, o_ref, dg_ref):
        sec = pl.program_id(0)

        @pl.when(pl.program_id(1) == 0)
        def _():
            dg_ref[...] = jnp.zeros_like(dg_ref)

        def normed(d_ref):
            lo = lax.broadcasted_iota(jnp.int32, (1, LANES), 1) < dh
            dg = jnp.zeros((1, LANES), F32)
            for lb in range(nsec):
                ls = pl.ds(lb * LANES, LANES)
                xv = x_ref[:, ls].astype(F32)
                d = d_ref[:, ls]
                rs = lax.rsqrt(_group_sum(xv * xv, lo) * (1.0 / dh) + EPS)
                xn = xv * rs
                dyg = d * g_ref[0:1, :]
                dx = rs * (dyg - xn * (_group_sum(dyg * xn, lo) * (1.0 / dh)))
                o_ref[:, ls] = dx.astype(o_ref.dtype)
                dg = dg + jnp.sum(d * xn, axis=0, keepdims=True)
            dg_ref[...] += jnp.broadcast_to(dg, dg_ref.shape)

        @pl.when(sec == 0)
        def _():
            normed(dq_ref)

        @pl.when(sec == 1)
        def _():
            normed(dk_ref)

        @pl.when(sec == 2)
        def _():
            o_ref[...] = dv_ref[...].astype(o_ref.dtype)

    return pl.pallas_call(
        body,
        name=name,
        grid=(3, s // tr),
        in_specs=[blk, grad_spec(0), grad_spec(1), grad_spec(2), gspec],
        out_specs=(blk, gspec),
        out_shape=(jax.ShapeDtypeStruct((s, width), BF16), jax.ShapeDtypeStruct((24, LANES), F32)),
        compiler_params=_params(("arbitrary", "arbitrary")),
    )(qkv, dq, dk, dv, gains)


def _split_dot(v, tri):
    hi = v.astype(BF16)
    lo = (v - hi.astype(F32)).astype(BF16)
    return _dot(hi, tri, 1, 0) + _dot(lo, tri, 1, 0)


def _attn_scores(qm, kb, tri_gt, valid, r_run):
    z = _dot(qm, kb, 1, 1)
    zl = jnp.minimum(z, 0.0) - jnp.log(1.0 + jnp.exp(-jnp.abs(z)))
    lk = zl - z
    if valid is not None:
        lk = jnp.where(valid, lk, 0.0)
    after = _split_dot(lk, tri_gt)
    w = jnp.exp(zl + after + r_run)
    if valid is not None:
        w = jnp.where(valid, w, 0.0)
    return zl, lk, w


def _attn_alive(rr):
    return jnp.max(jnp.maximum(rr[0], rr[1])) > ATTN_DEAD_LOG_WEIGHT


def _attn_walk_variants(i, walk):
    @pl.when(i == 0)
    def _():
        walk(False)

    @pl.when(i > 0)
    def _():
        walk(True)


def _attn_specs(s, width, dh, tq):
    nsec = _head_pair_geometry(width, dh)
    qspec = pl.BlockSpec((tq, LANES), lambda h, i: (i, h))
    kspec = pl.BlockSpec((s, LANES), lambda h, i: (0, nsec + h))
    vspec = pl.BlockSpec((s, LANES), lambda h, i: (0, 2 * nsec + h))
    return nsec, qspec, kspec, vspec


def _attn_masks(tq, tk, dh):
    lo = lax.broadcasted_iota(jnp.int32, (1, LANES), 1) < dh
    r_io = lax.broadcasted_iota(jnp.int32, (tk, tk), 0)
    c_io = lax.broadcasted_iota(jnp.int32, (tk, tk), 1)
    qrow = lax.broadcasted_iota(jnp.int32, (tq, tk), 0)
    kcol = lax.broadcasted_iota(jnp.int32, (tq, tk), 1)
    return lo, r_io, c_io, qrow, kcol


def _attn_fwd(qkv16, dh, *, gather_cargo=(), name):
    s, width = qkv16.shape
    tq = _tile(s, (ATTN_Q_BLOCK, ATTN_K_BLOCK))
    tk = _tile(tq, (ATTN_K_BLOCK,))
    nd = tq // tk
    nsec, qspec, kspec, vspec = _attn_specs(s, width, dh, tq)

    def body(q_ref, k_ref, v_ref, o_ref):
        i = pl.program_id(1)
        lo, r_io, c_io, qrow, kcol = _attn_masks(tq, tk, dh)
        q2 = q_ref[...]
        zq = jnp.zeros_like(q2)
        qs = (jnp.where(lo, q2, zq), jnp.where(lo, zq, q2))
        tri_gt = (r_io > c_io).astype(BF16)

        def block(j, carry, diag):
            rr, acc = carry
            start = pl.multiple_of(j * tk, tk)
            kb = k_ref[pl.ds(start, tk), :]
            vb = v_ref[pl.ds(start, tk), :]
            valid = None if diag is None else kcol + diag * tk < qrow
            outs, new_r = [], []
            for hh in range(2):
                _, lk, w = _attn_scores(qs[hh], kb, tri_gt, valid, rr[hh])
                outs.append(_dot(w.astype(BF16), vb, 1, 0))
                new_r.append(rr[hh] + jnp.sum(lk, axis=-1, keepdims=True))
            return tuple(new_r), acc + jnp.where(lo, outs[0], outs[1])

        below = i * nd

        def step(st):
            jj, _, cr = st
            cr = block(below - 1 - jj, cr, None)
            return jj + 1, _attn_alive(cr[0]), cr

        def walk(inline_first_below):
            zero = jnp.zeros((tq, 1), F32)
            carry = ((zero, zero), jnp.zeros((tq, LANES), F32))
            for dg in reversed(range(nd)):
                carry = block(i * nd + dg, carry, dg)
            if inline_first_below:
                carry = block(below - 1, carry, None)
            _, _, carry = lax.while_loop(lambda st: (st[0] < below) & st[1], step,
                                         (jnp.int32(int(inline_first_below)), _attn_alive(carry[0]), carry))
            o_ref[...] = carry[1]

        _attn_walk_variants(i, walk)

    outs, gathered = _carrying_call(
        body,
        gather_cargo,
        gather=True,
        name=name,
        grid=(nsec, s // tq),
        in_specs=[qspec, kspec, vspec],
        out_specs=[qspec],
        out_shape=[jax.ShapeDtypeStruct((s, width // 3), F32)],
        scratch_shapes=[],
        operands=(qkv16, qkv16, qkv16),
        compiler_params=_params(("arbitrary", "arbitrary")),
    )
    return outs[0], gathered


def _attn_bwd(qkv16, o, do, dh, *, cargo=(), name):
    s, width = qkv16.shape
    tq = _tile(s, (ATTN_Q_BLOCK, ATTN_K_BLOCK))
    tk = _tile(tq, (ATTN_K_BLOCK,))
    nd = tq // tk
    nsec, qspec, kspec, vspec = _attn_specs(s, width, dh, tq)
    accspec = pl.BlockSpec((s, LANES), lambda h, i: (0, h))

    def body(q_ref, k_ref, v_ref, o_ref, do_ref, dq_ref, dk_ref, dv_ref):
        i = pl.program_id(1)

        @pl.when(i == 0)
        def _():
            dk_ref[...] = jnp.zeros_like(dk_ref)
            dv_ref[...] = jnp.zeros_like(dv_ref)

        lo, r_io, c_io, qrow, kcol = _attn_masks(tq, tk, dh)
        q2 = q_ref[...]
        zq = jnp.zeros_like(q2)
        qs = (jnp.where(lo, q2, zq), jnp.where(lo, zq, q2))
        do16 = do_ref[...].astype(BF16)
        dos = (jnp.where(lo, do16, zq), jnp.where(lo, zq, do16))
        prod = do16.astype(F32) * o_ref[...]
        deltas = (jnp.sum(jnp.where(lo, prod, 0.0), axis=-1, keepdims=True),
                  jnp.sum(jnp.where(lo, 0.0, prod), axis=-1, keepdims=True))
        tri_gt = (r_io > c_io).astype(BF16)
        tri_ge = (r_io >= c_io).astype(BF16)

        def block(j, carry, diag):
            rr, er, dq = carry
            start = pl.multiple_of(j * tk, tk)
            kb = k_ref[pl.ds(start, tk), :]
            vb = v_ref[pl.ds(start, tk), :]
            valid = None if diag is None else kcol + diag * tk < qrow
            new_r, new_e, dqs = [], [], []
            dk_blk = dv_blk = None
            for hh in range(2):
                zl, lk, w = _attn_scores(qs[hh], kb, tri_gt, valid, rr[hh])
                beta = jnp.exp(zl)
                w16 = w.astype(BF16)
                e = _dot(dos[hh], vb, 1, 1) * w16.astype(F32)
                e_before = deltas[hh] - er[hh] - _split_dot(e, tri_ge)
                dz = e - beta * (e + e_before)
                if valid is not None:
                    dz = jnp.where(valid, dz, 0.0)
                dz16 = dz.astype(BF16)
                dqs.append(_dot(dz16, kb, 1, 0))
                dkc = _dot(dz16, qs[hh], 0, 0)
                dvc = _dot(w16, dos[hh], 0, 0)
                dk_blk = dkc if dk_blk is None else dk_blk + dkc
                dv_blk = dvc if dv_blk is None else dv_blk + dvc
                new_r.append(rr[hh] + jnp.sum(lk, axis=-1, keepdims=True))
                new_e.append(er[hh] + jnp.sum(e, axis=-1, keepdims=True))
            dk_ref[pl.ds(start, tk), :] += dk_blk
            dv_ref[pl.ds(start, tk), :] += dv_blk
            return tuple(new_r), tuple(new_e), dq + jnp.where(lo, dqs[0], dqs[1])

        below = i * nd

        def step(st):
            jj, _, cr = st
            cr = block(below - 1 - jj, cr, None)
            return jj + 1, _attn_alive(cr[0]), cr

        def walk(inline_first_below):
            zero = jnp.zeros((tq, 1), F32)
            carry = ((zero, zero), (zero, zero), jnp.zeros((tq, LANES), F32))
            for dg in reversed(range(nd)):
                carry = block(i * nd + dg, carry, dg)
            if inline_first_below:
                carry = block(below - 1, carry, None)
            _, _, carry = lax.while_loop(lambda st: (st[0] < below) & st[1], step,
                                         (jnp.int32(int(inline_first_below)), _attn_alive(carry[0]), carry))
            dq_ref[...] = carry[2]

        _attn_walk_variants(i, walk)

    shp = jax.ShapeDtypeStruct((s, width // 3), F32)
    return _carrying_call(
        body,
        cargo,
        name=name,
        grid=(nsec, s // tq),
        in_specs=[qspec, kspec, vspec, qspec, qspec],
        out_specs=(qspec, accspec, accspec),
        out_shape=(shp, shp, shp),
        scratch_shapes=[],
        operands=(qkv16, qkv16, qkv16, o, do),
        compiler_params=_params(("arbitrary", "arbitrary")),
    )


def _adam_update(wv, gv, mv, vv):
    c1 = 1.0 - ADAM_B1 ** ADAM_STEP
    c2 = 1.0 - ADAM_B2 ** ADAM_STEP
    nm = ADAM_B1 * mv + (1.0 - ADAM_B1) * gv
    nv = ADAM_B2 * vv + (1.0 - ADAM_B2) * (gv * gv)
    delta = -ADAM_LR * ((nm / c1) / (jnp.sqrt(nv / c2) + ADAM_EPS) + ADAM_WD * wv)
    return delta, nm, nv


def _sum_slabs(r, *, name):
    n, rows, lanes = r.shape
    tr = _tile(rows, (1024, 512, 256, 128, 64, 32, 16))

    def body(r_ref, o_ref):
        acc = r_ref[0]
        for d in range(1, n):
            acc = acc + r_ref[d]
        o_ref[...] = acc

    return pl.pallas_call(
        body,
        name=name,
        grid=(rows // tr,),
        in_specs=[pl.BlockSpec((n, tr, lanes), lambda i: (0, i, 0))],
        out_specs=pl.BlockSpec((tr, lanes), lambda i: (i, 0)),
        out_shape=jax.ShapeDtypeStruct((rows, lanes), F32),
        compiler_params=_params(("parallel",)),
    )(r)


def _adamw(w, g, m, v, *, name):
    rows, cols = w.shape
    tr = _tile(rows, (1024, 512, 256, 128, 64, 32, 16))
    spec = pl.BlockSpec((tr, cols), lambda i: (i, 0))

    def body(w_ref, g_ref, m_ref, v_ref, d_ref, nm_ref, nv_ref):
        d_ref[...], nm_ref[...], nv_ref[...] = _adam_update(w_ref[...], g_ref[...], m_ref[...], v_ref[...])

    shp = jax.ShapeDtypeStruct((rows, cols), F32)
    return pl.pallas_call(
        body,
        name=name,
        grid=(rows // tr,),
        in_specs=[spec, spec, spec, spec],
        out_specs=(spec, spec, spec),
        out_shape=(shp, shp, shp),
        compiler_params=_params(("parallel",)),
    )(w, g, m, v)


def _adamw_sum(w, received, m, v, *, name):
    layers, rows, cols = w.shape
    n = received[0].shape[0]
    tr = _tile(rows, (256, 128, 64, 32, 16))
    spec = pl.BlockSpec((1, tr, cols), lambda li, i: (li, i, 0))

    def partial_spec(own):
        return pl.BlockSpec((n, tr, cols), lambda li, i: (0, jnp.where(li == own, i, 0), 0))

    def body(w_ref, *refs):
        r_refs, (m_ref, v_ref, g_ref, d_ref, nm_ref, nv_ref) = refs[:layers], refs[layers:]
        for own in range(layers):
            @pl.when(pl.program_id(0) == own)
            def _(r_ref=r_refs[own]):
                gv = r_ref[0].astype(F32)
                for d in range(1, n):
                    gv = gv + r_ref[d].astype(F32)
                g_ref[0] = gv
                d_ref[0], nm_ref[0], nv_ref[0] = _adam_update(w_ref[0], gv, m_ref[0], v_ref[0])

    shp = jax.ShapeDtypeStruct((layers, rows, cols), F32)
    return pl.pallas_call(
        body,
        name=name,
        grid=(layers, rows // tr),
        in_specs=[spec] + [partial_spec(own) for own in range(layers)] + [spec, spec],
        out_specs=(spec, spec, spec, spec),
        out_shape=(shp, shp, shp, shp),
        compiler_params=_params(("arbitrary", "arbitrary")),
    )(w, *received, m, v)


_MATMUL_WEIGHTS = (("conv_w_in", 2), ("conv_w_out", 1), ("attn_w_qkv", 2), ("attn_w_o", 1), ("ffn_w_up", 2),
                   ("ffn_w_down", 1))
_TAP_WEIGHTS = (("conv_a_dw_w", 2), ("conv_b_dw_w", 2), ("ffn_dw_w", 2))
_REPLICATED = ("mix_norm_g", "ffn_norm_g", "conv_a_dw_b", "conv_a_ln_g", "conv_a_ln_b", "attn_q_g", "attn_k_g",
               "ffn_dw_b")
_WEIGHT_ORDER = ("mix_norm_g", "ffn_norm_g", "conv_w_in", "conv_a_dw_w", "conv_a_dw_b", "conv_a_ln_g",
                 "conv_a_ln_b", "conv_b_dw_w", "conv_w_out", "attn_w_qkv", "attn_q_g", "attn_k_g", "attn_w_o",
                 "ffn_w_up", "ffn_dw_w", "ffn_dw_b", "ffn_w_down")


def _assemble(gathered, axis):
    n, l, a, b = gathered.shape
    if axis == 2:
        return jnp.transpose(gathered, (1, 2, 0, 3)).reshape(l, a, n * b)
    return jnp.transpose(gathered, (1, 0, 2, 3)).reshape(l, n * a, b)


def _split_shards(full, axis):
    l, a, b = full.shape
    if axis == 2:
        return jnp.transpose(full.reshape(l, a, N_DEV, b // N_DEV), (2, 0, 1, 3))
    return jnp.transpose(full.reshape(l, N_DEV, a // N_DEV, b), (1, 0, 2, 3))


def kernel(x, mix_norm_g, ffn_norm_g, conv_w_in, conv_a_dw_w, conv_a_dw_b, conv_a_ln_g, conv_a_ln_b, conv_b_dw_w, conv_w_out, attn_w_qkv, attn_q_g, attn_k_g, attn_w_o, ffn_w_up, ffn_dw_w, ffn_dw_b, ffn_w_down, loss_target, m_mix_norm_g, m_ffn_norm_g, m_conv_w_in, m_conv_a_dw_w, m_conv_a_dw_b, m_conv_a_ln_g, m_conv_a_ln_b, m_conv_b_dw_w, m_conv_w_out, m_attn_w_qkv, m_attn_q_g, m_attn_k_g, m_attn_w_o, m_ffn_w_up, m_ffn_dw_w, m_ffn_dw_b, m_ffn_w_down, v_mix_norm_g, v_ffn_norm_g, v_conv_w_in, v_conv_a_dw_w, v_conv_a_dw_b, v_conv_a_ln_g, v_conv_a_ln_b, v_conv_b_dw_w, v_conv_w_out, v_attn_w_qkv, v_attn_q_g, v_attn_k_g, v_attn_w_o, v_ffn_w_up, v_ffn_dw_w, v_ffn_dw_b, v_ffn_w_down):
    args = dict(locals())
    weights = {n: args[n] for n in _WEIGHT_ORDER}
    mom_m = {n: args["m_" + n] for n in _WEIGHT_ORDER}
    mom_v = {n: args["v_" + n] for n in _WEIGHT_ORDER}

    _, s, d = x.shape
    depth = mix_norm_g.shape[0]
    dh = attn_q_g.shape[1]
    x0 = x.reshape(s, d)
    target = loss_target.reshape(s, d)

    shard_axis = dict(_MATMUL_WEIGHTS)
    shard16 = {n: weights[n].astype(BF16) for n, _ in _MATMUL_WEIGHTS}
    full = {}

    def layer_keys(layer):
        if layer >= depth:
            return []
        mixer = ("conv_w_in", "conv_w_out") if layer % 2 == 0 else ("attn_w_qkv", "attn_w_o")
        return [(mixer[0], layer // 2), (mixer[1], layer // 2), ("ffn_w_up", layer), ("ffn_w_down", layer)]

    def shards(keys):
        return [shard16[n][idx] for n, idx in keys]

    def assemble(keys, gathered):
        for (n, idx), g in zip(keys, gathered):
            if shard_axis[n] == 2:
                full[(n, idx)] = _assemble_cols(g, name=f"assemble_{n}_{idx}")
            else:
                full[(n, idx)] = g.reshape(N_DEV * g.shape[1], g.shape[2])

    mixer_in0, mixer_out0, ffn_up0, ffn_down0 = layer_keys(0)
    up_front = [mixer_in0, mixer_out0, ffn_down0]
    assemble(up_front, _all_gather(shards(up_front), "gather_first_weights"))
    tap_shapes = [weights[n].shape for n, _ in _TAP_WEIGHTS]
    g_tap = _all_gather([_pack([weights[n] for n, _ in _TAP_WEIGHTS], F32)], "gather_tap_weights")[0]
    for (n, axis), part in zip(_TAP_WEIGHTS, _unpack(g_tap, tap_shapes, lead=(N_DEV,))):
        full[n] = _assemble(part, axis)
    f = ffn_w_down.shape[1] * N_DEV

    saved = []
    xs = x0
    hn = _rmsnorm_fwd(xs, mix_norm_g[0:1], name="mix_norm_fwd_0")
    for layer in range(depth):
        i = layer // 2
        nxt = layer_keys(layer + 1)
        sv = {"x_in": xs, "h": hn}
        if layer % 2 == 0:
            on_in, on_act = ([], nxt[0:1]) if layer > 0 else (nxt[0:1], [ffn_up0])
            p, got = _matmul(hn, full[("conv_w_in", i)], gather_cargo=shards(on_in), name=f"conv_in_{layer}")
            assemble(on_in, got)
            (ab, a1), got = _conv_act_fwd(p, full["conv_a_dw_w"][i], conv_a_dw_b[i:i + 1], conv_a_ln_g[i:i + 1],
                                          conv_a_ln_b[i:i + 1], full["conv_b_dw_w"][i], gather_cargo=shards(on_act),
                                          name=f"conv_act_fwd_{layer}")
            assemble(on_act, got)
            sv.update(p=p, a1=a1, mix=ab)
            (xs, h2), _ = _matmul_add_norm(ab, full[("conv_w_out", i)], xs, ffn_norm_g[layer:layer + 1],
                                           name=f"conv_out_{layer}")
            riders = [nxt[1:2], nxt[2:3], nxt[3:4]]
        else:
            reps = LANES // dh
            gains = jnp.concatenate([
                jnp.broadcast_to(jnp.tile(attn_q_g[i], reps) * dh ** -0.5, (8, LANES)),
                jnp.broadcast_to(jnp.tile(attn_k_g[i], reps), (8, LANES)),
                jnp.ones((8, LANES), F32)], axis=0)
            qkv = _matmul(hn, full[("attn_w_qkv", i)], out_dtype=BF16, name=f"attn_qkv_{layer}")
            qkv16 = _qkv_prep_fwd(qkv, gains, dh, name=f"qkv_prep_fwd_{layer}")
            o, got = _attn_fwd(qkv16, dh, gather_cargo=shards(nxt), name=f"attn_fwd_{layer}")
            assemble(nxt, got)
            sv.update(qkv=qkv, qkv16=qkv16, gains=gains, mix=o)
            (xs, h2), _ = _matmul_add_norm(o, full[("attn_w_o", i)], xs, ffn_norm_g[layer:layer + 1],
                                           name=f"attn_out_{layer}")
            riders = [[], [], []]
        sv["x_mid"] = xs
        w_up = full[("ffn_w_up", layer)]
        ug, got = _matmul(h2, w_up, b_half=0, out_dtype=BF16, gather_cargo=shards(riders[0]),
                          name=f"ffn_up_gate_{layer}")
        assemble(riders[0], got)
        uv = _matmul(h2, w_up, b_half=1, out_dtype=BF16, name=f"ffn_up_val_{layer}")
        taps = full["ffn_dw_w"][layer]
        bias = ffn_dw_b[layer:layer + 1]
        act, got = _ffn_act_fwd(ug, uv, taps[:, :f], taps[:, f:], bias[:, :f], bias[:, f:],
                                gather_cargo=shards(riders[1]), name=f"ffn_act_fwd_{layer}")
        assemble(riders[1], got)
        sv.update(h2=h2, ug=ug, uv=uv, act=act)
        if layer + 1 < depth:
            (xs, hn), got = _matmul_add_norm(act, full[("ffn_w_down", layer)], xs, mix_norm_g[layer + 1:layer + 2],
                                             gather_cargo=shards(riders[2]), name=f"ffn_down_{layer}")
            assemble(riders[2], got)
        else:
            dx, dx16, sq = _matmul_loss(act, full[("ffn_w_down", layer)], xs, target, name=f"ffn_down_loss_{layer}")
        saved.append(sv)

    loss = lax.psum(0.5 * jnp.sum(sq) / d, ("x", "y", "c"))

    grads = {n: [None] * weights[n].shape[0] for n in _WEIGHT_ORDER}
    queued, arrived = [], {}

    def queue(n, idx, g):
        a, b = weights[n].shape[1:]
        slabs = _split_cols(g, name=f"split_{n}_{idx}") if shard_axis[n] == 2 else g.reshape(N_DEV, a, b)
        queued.append(((n, idx), slabs))

    def carried(call):
        keys = [k for k, _ in queued]
        slabs = [sl for _, sl in queued]
        queued.clear()
        outs, received = call(slabs)
        arrived.update(zip(keys, received))
        return outs

    for layer in reversed(range(depth)):
        i = layer // 2
        sv = saved[layer]
        w_up = full[("ffn_w_up", layer)]
        taps = full["ffn_dw_w"][layer]
        bias = ffn_dw_b[layer:layer + 1]
        dact = _matmul(dx16, full[("ffn_w_down", layer)], tb=True, out_dtype=BF16, name=f"ffn_down_dx_{layer}")
        queue("ffn_w_down", layer, _weight_grad(sv["act"], dx16, name=f"ffn_down_dw_{layer}"))
        dug, duv, sumg, sumv = carried(lambda cargo: _ffn_act_bwd(
            sv["ug"], sv["uv"], dact, taps[:, :f], taps[:, f:], bias[:, :f], bias[:, f:], cargo=cargo,
            name=f"ffn_act_bwd_{layer}"))
        sums = jnp.concatenate([sumg, sumv], axis=1).reshape(4, 8, 2 * f).sum(axis=1)
        grads["ffn_dw_w"][layer] = sums[:3]
        grads["ffn_dw_b"][layer] = sums[3]
        dh2 = _matmul(dug, w_up[:, :f], tb=True, name=f"ffn_up_gate_dx_{layer}")
        (dx, dx16, dg), _ = _matmul_norm_bwd(duv, w_up[:, f:], sv["x_mid"], ffn_norm_g[layer:layer + 1], dx, add=dh2,
                                             name=f"ffn_up_val_dx_{layer}")
        grads["ffn_norm_g"][layer] = dg[0]
        queue("ffn_w_up", layer, jnp.concatenate(
            [_weight_grad(sv["h2"], dug, name=f"ffn_up_gate_dw_{layer}"),
             _weight_grad(sv["h2"], duv, name=f"ffn_up_val_dw_{layer}")], axis=1))

        if layer % 2 == 0:
            dab = _matmul(dx16, full[("conv_w_out", i)], tb=True, name=f"conv_out_dx_{layer}")
            queue("conv_w_out", i, _weight_grad(sv["mix"], dx16, name=f"conv_out_dw_{layer}"))
            ka = full["conv_a_dw_w"].shape[1]
            kb = full["conv_b_dw_w"].shape[1]
            dp, dwa, dba, dlng, dlnb, dwb = carried(lambda cargo: _conv_act_bwd(
                sv["p"], sv["a1"], dab, full["conv_a_dw_w"][i], conv_a_ln_g[i:i + 1], conv_a_ln_b[i:i + 1],
                full["conv_b_dw_w"][i], cargo=cargo, name=f"conv_act_bwd_{layer}"))
            grads["conv_a_dw_w"][i] = dwa[:ka]
            grads["conv_a_dw_b"][i] = dba[0]
            grads["conv_a_ln_g"][i] = dlng[0]
            grads["conv_a_ln_b"][i] = dlnb[0]
            grads["conv_b_dw_w"][i] = dwb[:kb]
            queue("conv_w_in", i, _weight_grad(sv["h"], dp, name=f"conv_in_dw_{layer}"))
            dy, w_in = dp, full[("conv_w_in", i)]
        else:
            do = _matmul(dx16, full[("attn_w_o", i)], tb=True, name=f"attn_out_dx_{layer}")
            queue("attn_w_o", i, _weight_grad(sv["mix"], dx16, name=f"attn_out_dw_{layer}"))
            dq, dk, dv = carried(lambda cargo: _attn_bwd(sv["qkv16"], sv["mix"], do, dh, cargo=cargo,
                                                         name=f"attn_bwd_{layer}"))
            dqkv, dgains = _qkv_prep_bwd(sv["qkv"], dq, dk, dv, sv["gains"], dh, name=f"qkv_prep_bwd_{layer}")
            grads["attn_q_g"][i] = dgains[0].reshape(-1, dh).sum(axis=0) * dh ** -0.5
            grads["attn_k_g"][i] = dgains[8].reshape(-1, dh).sum(axis=0)
            queue("attn_w_qkv", i, _weight_grad(sv["h"], dqkv, name=f"attn_qkv_dw_{layer}"))
            dy, w_in = dqkv, full[("attn_w_qkv", i)]

        def mixer_in_bwd(cargo):
            return _matmul_norm_bwd(dy, w_in, sv["x_in"], mix_norm_g[layer:layer + 1], dx, cargo=cargo,
                                    name=f"mixer_in_dx_{layer}")

        dx, dx16, dg = carried(mixer_in_bwd) if layer == 0 else mixer_in_bwd(())[0]
        grads["mix_norm_g"][layer] = dg[0]

    me = _my_index()
    final_grads, delta, new_m, new_v = {}, {}, {}, {}
    for n, _ in _MATMUL_WEIGHTS:
        final_grads[n], delta[n], new_m[n], new_v[n] = _adamw_sum(
            weights[n], [arrived[(n, idx)] for idx in range(weights[n].shape[0])], mom_m[n], mom_v[n],
            name=f"adamw_{n}")

    small_names = list(_REPLICATED) + [n for n, _ in _TAP_WEIGHTS]
    local = {n: jnp.stack(grads[n], axis=0) for n in small_names}
    small_shapes = [local[n].shape for n in small_names]
    g_small = _sum_slabs(_all_gather([_pack([local[n] for n in small_names], F32)], "gather_small_grads")[0],
                         name="sum_small_grads")
    reduced = dict(zip(small_names, _unpack(g_small, small_shapes)))
    for n in _REPLICATED:
        final_grads[n] = reduced[n]
    for n, axis in _TAP_WEIGHTS:
        final_grads[n] = lax.dynamic_index_in_dim(_split_shards(reduced[n], axis), me, axis=0, keepdims=False)
    shapes = [weights[n].shape for n in small_names]
    d_s, m_s, v_s = _adamw(
        _pack([weights[n] for n in small_names], F32), _pack([final_grads[n] for n in small_names], F32),
        _pack([mom_m[n] for n in small_names], F32), _pack([mom_v[n] for n in small_names], F32), name="adamw_small")
    delta.update(zip(small_names, _unpack(d_s, shapes)))
    new_m.update(zip(small_names, _unpack(m_s, shapes)))
    new_v.update(zip(small_names, _unpack(v_s, shapes)))

    order = list(_WEIGHT_ORDER)
    return (loss, dx.reshape(x.shape), *[final_grads[n] for n in order], *[delta[n] for n in order],
            *[new_m[n] for n in order], *[new_v[n] for n in order])
```
